```python
import jax, jax.numpy as jnp
from jax import lax
import numpy as np

D_MODEL = 2048
BATCH = 8
SEQ = 4096
DEPTH = 1

MIX_WIDTH = D_MODEL
CONV_WIDTH = MIX_WIDTH // 2
CONV_GROUPS = 8
CONV_K = 3
DN_HEADS = 8
DN_HEAD_DIM = 128
DN_WIDTH = DN_HEADS * DN_HEAD_DIM
DN_CONV_K = 4
CHUNK = 64
D_FF = 5632
FFN_CONV_K = 3
PLE_DIM = 256
EPS = 1e-6
IN_COLS = 3 * CONV_WIDTH + 4 * DN_WIDTH + 2 * DN_HEADS

kernel_name = "hybrid_shortconv_gated_deltanet_convffn_ple"


def rmsnorm(x, g):
    xf = x.astype(jnp.float32)
    y = xf * lax.rsqrt(jnp.mean(xf * xf, axis=-1, keepdims=True) + EPS) * g.astype(jnp.float32)
    return y.astype(x.dtype)


def causal_dwconv(x, w):
    K = w.shape[0]
    S = x.shape[1]
    xp = jnp.pad(x, ((0, 0), (K - 1, 0), (0, 0)))
    y = xp[:, 0:S] * w[0]
    for j in range(1, K):
        y = y + xp[:, j:j + S] * w[j]
    return y


def l2norm(x):
    return x * lax.rsqrt(jnp.sum(x * x, axis=-1, keepdims=True) + EPS)


def chunk_gated_delta(q, k, v, g, beta):
    B, H, S, dk = q.shape
    dv = v.shape[-1]
    N = S // CHUNK
    q = q * (dk ** -0.5)
    qc = q.reshape(B, H, N, CHUNK, dk)
    kc = k.reshape(B, H, N, CHUNK, dk)
    vc = v.reshape(B, H, N, CHUNK, dv)
    bc = beta.reshape(B, H, N, CHUNK)
    gcum = jnp.cumsum(g.reshape(B, H, N, CHUNK), axis=-1)
    idx = jnp.arange(CHUNK)
    causal = idx[:, None] >= idx[None, :]
    strict = idx[:, None] > idx[None, :]
    diff = gcum[..., :, None] - gcum[..., None, :]
    decay = jnp.exp(jnp.where(causal, diff, -jnp.inf))
    kk = jnp.einsum('bhncd,bhnmd->bhncm', kc, kc)
    L = jnp.where(strict, kk * decay * bc[..., :, None], 0.0)
    A = L + jnp.eye(CHUNK, dtype=jnp.float32)
    rhs = jnp.concatenate([vc * bc[..., None],
                           kc * (bc * jnp.exp(gcum))[..., None]], axis=-1)
    sol = lax.linalg.triangular_solve(A, rhs, left_side=True, lower=True)
    u = sol[..., :dv]
    w = sol[..., dv:]
    qk = jnp.einsum('bhncd,bhnmd->bhncm', qc, kc) * decay
    q_dec = qc * jnp.exp(gcum)[..., None]
    k_dec = kc * jnp.exp(gcum[..., -1:] - gcum)[..., None]
    g_last = jnp.exp(gcum[..., -1])

    def step(state, inp):
        u_n, w_n, qk_n, qd_n, kd_n, gl_n = inp
        v_new = u_n - jnp.einsum('bhcd,bhde->bhce', w_n, state)
        o = (jnp.einsum('bhcd,bhde->bhce', qd_n, state)
             + jnp.einsum('bhcm,bhme->bhce', qk_n, v_new))
        state = state * gl_n[..., None, None] + jnp.einsum('bhcd,bhce->bhde', kd_n, v_new)
        return state, o

    to_front = lambda t: jnp.moveaxis(t, 2, 0)
    xs = (to_front(u), to_front(w), to_front(qk), to_front(q_dec), to_front(k_dec),
          jnp.moveaxis(g_last, 2, 0))
    s0 = jnp.zeros((B, H, dk, dv), jnp.float32)
    _, o = lax.scan(step, s0, xs)
    return jnp.moveaxis(o, 0, 2).reshape(B, H, S, dv)


def hybrid_layer(x, p_i, norm_mix_g, w_in, conv_a_w, conv_qkv_w, a_log, dt_bias, dn_norm_g,
                 w_out, norm_ffn_g, w_up, conv_ffn_w, w_down, norm_ple_g, w_ple_gate, w_ple_proj):
    Bsz, S, _ = x.shape
    h = rmsnorm(x, norm_mix_g)
    proj = h @ w_in
    s1 = CONV_WIDTH
    s2 = 2 * CONV_WIDTH
    s3 = 3 * CONV_WIDTH
    s4 = s3 + 3 * DN_WIDTH
    s5 = s4 + DN_WIDTH
    s6 = s5 + DN_HEADS
    a_x, a_b, a_c, qkv, z, a_dec, b_beta = jnp.split(proj, [s1, s2, s3, s4, s5, s6], axis=-1)

    y_a = a_b * causal_dwconv(a_c * a_x, conv_a_w)

    qkv = jax.nn.silu(causal_dwconv(qkv, conv_qkv_w)).astype(jnp.float32)
    q, k, v = jnp.split(qkv, 3, axis=-1)
    q = l2norm(q.reshape(Bsz, S, DN_HEADS, DN_HEAD_DIM))
    k = l2norm(k.reshape(Bsz, S, DN_HEADS, DN_HEAD_DIM))
    v = v.reshape(Bsz, S, DN_HEADS, DN_HEAD_DIM)
    g = -jnp.exp(a_log.astype(jnp.float32)) * jax.nn.softplus(
        a_dec.astype(jnp.float32) + dt_bias.astype(jnp.float32))
    beta = jax.nn.sigmoid(b_beta.astype(jnp.float32))
    tr = lambda t: jnp.swapaxes(t, 1, 2)
    o = chunk_gated_delta(tr(q), tr(k), tr(v), tr(g), tr(beta))
    o = tr(o)
    zf = z.astype(jnp.float32).reshape(Bsz, S, DN_HEADS, DN_HEAD_DIM)
    o = (o * lax.rsqrt(jnp.mean(o * o, axis=-1, keepdims=True) + EPS)
         * dn_norm_g.astype(jnp.float32) * jax.nn.silu(zf))
    y_b = o.reshape(Bsz, S, DN_WIDTH).astype(x.dtype)

    x = x + jnp.concatenate([y_a, y_b], axis=-1) @ w_out

    h = rmsnorm(x, norm_ffn_g)
    up = causal_dwconv(h @ w_up, conv_ffn_w)
    gate, val = jnp.split(up, 2, axis=-1)
    x = x + (jax.nn.silu(gate) * val) @ w_down

    ple_gate = jax.nn.sigmoid(rmsnorm(x, norm_ple_g) @ w_ple_gate)
    x = x + ple_gate * (p_i @ w_ple_proj)
    return x


def _fwd_setup_inputs(seed: int = 0) -> dict:
    key = jax.random.key(seed)
    ks = jax.random.split(key, 20)
    f32 = jnp.float32
    nrm = lambda k, shape, scale: jax.random.normal(k, shape, f32) * scale
    gain = lambda k, shape: 1.0 + 0.02 * jax.random.normal(k, shape, f32)
    return {
        "x": jax.random.normal(ks[0], (BATCH, SEQ, D_MODEL), f32),
        "p": jax.random.normal(ks[1], (DEPTH, BATCH, SEQ, PLE_DIM), f32),
        "norm_mix_g": gain(ks[2], (DEPTH, D_MODEL)),
        "w_in": nrm(ks[3], (DEPTH, D_MODEL, IN_COLS), D_MODEL ** -0.5),
        "conv_a_w": nrm(ks[4], (DEPTH, CONV_K, CONV_WIDTH), CONV_K ** -0.5),
        "conv_qkv_w": nrm(ks[5], (DEPTH, DN_CONV_K, 3 * DN_WIDTH), DN_CONV_K ** -0.5),
        "a_log": jnp.log(jax.random.uniform(ks[6], (DEPTH, DN_HEADS), f32, 1.0, 16.0)),
        "dt_bias": 0.1 * jax.random.normal(ks[7], (DEPTH, DN_HEADS), f32),
        "dn_norm_g": gain(ks[8], (DEPTH, DN_HEAD_DIM)),
        "w_out": nrm(ks[9], (DEPTH, MIX_WIDTH, D_MODEL), MIX_WIDTH ** -0.5),
        "norm_ffn_g": gain(ks[10], (DEPTH, D_MODEL)),
        "w_up": nrm(ks[11], (DEPTH, D_MODEL, 2 * D_FF), D_MODEL ** -0.5),
        "conv_ffn_w": nrm(ks[12], (DEPTH, FFN_CONV_K, 2 * D_FF), FFN_CONV_K ** -0.5),
        "w_down": nrm(ks[13], (DEPTH, D_FF, D_MODEL), D_FF ** -0.5),
        "norm_ple_g": gain(ks[14], (DEPTH, D_MODEL)),
        "w_ple_gate": nrm(ks[15], (DEPTH, D_MODEL, D_MODEL), D_MODEL ** -0.5),
        "w_ple_proj": nrm(ks[16], (DEPTH, PLE_DIM, D_MODEL), PLE_DIM ** -0.5),
        "final_norm_g": gain(ks[17], (D_MODEL,)),
    }


def _fwd_reference(x, p, norm_mix_g, w_in, conv_a_w, conv_qkv_w, a_log, dt_bias, dn_norm_g,
              w_out, norm_ffn_g, w_up, conv_ffn_w, w_down, norm_ple_g, w_ple_gate,
              w_ple_proj, final_norm_g):
    for i in range(DEPTH):
        x = hybrid_layer(x, p[i], norm_mix_g[i], w_in[i], conv_a_w[i], conv_qkv_w[i],
                         a_log[i], dt_bias[i], dn_norm_g[i], w_out[i], norm_ffn_g[i],
                         w_up[i], conv_ffn_w[i], w_down[i], norm_ple_g[i], w_ple_gate[i],
                         w_ple_proj[i])
    return rmsnorm(x, final_norm_g)


import jax as _jax
import jax.numpy as _jnp

TWIN_FORMAT = 'train_step'
FWD_PARAMS = ['x', 'p', 'norm_mix_g', 'w_in', 'conv_a_w', 'conv_qkv_w', 'a_log', 'dt_bias', 'dn_norm_g', 'w_out', 'norm_ffn_g', 'w_up', 'conv_ffn_w', 'w_down', 'norm_ple_g', 'w_ple_gate', 'w_ple_proj', 'final_norm_g']
TWIN_WEIGHTS = ['norm_mix_g', 'w_in', 'conv_a_w', 'conv_qkv_w', 'a_log', 'dt_bias', 'dn_norm_g', 'w_out', 'norm_ffn_g', 'w_up', 'conv_ffn_w', 'w_down', 'norm_ple_g', 'w_ple_gate', 'w_ple_proj', 'final_norm_g']
TWIN_DIFF_INPUT = 'x'
TWIN_INPUTS = ['x', 'p', 'norm_mix_g', 'w_in', 'conv_a_w', 'conv_qkv_w', 'a_log', 'dt_bias', 'dn_norm_g', 'w_out', 'norm_ffn_g', 'w_up', 'conv_ffn_w', 'w_down', 'norm_ple_g', 'w_ple_gate', 'w_ple_proj', 'final_norm_g', 'loss_target', 'm_norm_mix_g', 'm_w_in', 'm_conv_a_w', 'm_conv_qkv_w', 'm_a_log', 'm_dt_bias', 'm_dn_norm_g', 'm_w_out', 'm_norm_ffn_g', 'm_w_up', 'm_conv_ffn_w', 'm_w_down', 'm_norm_ple_g', 'm_w_ple_gate', 'm_w_ple_proj', 'm_final_norm_g', 'v_norm_mix_g', 'v_w_in', 'v_conv_a_w', 'v_conv_qkv_w', 'v_a_log', 'v_dt_bias', 'v_dn_norm_g', 'v_w_out', 'v_norm_ffn_g', 'v_w_up', 'v_conv_ffn_w', 'v_w_down', 'v_norm_ple_g', 'v_w_ple_gate', 'v_w_ple_proj', 'v_final_norm_g']
TWIN_OUTPUTS = ['loss', 'grad_x', 'grad_norm_mix_g', 'grad_w_in', 'grad_conv_a_w', 'grad_conv_qkv_w', 'grad_a_log', 'grad_dt_bias', 'grad_dn_norm_g', 'grad_w_out', 'grad_norm_ffn_g', 'grad_w_up', 'grad_conv_ffn_w', 'grad_w_down', 'grad_norm_ple_g', 'grad_w_ple_gate', 'grad_w_ple_proj', 'grad_final_norm_g', 'delta_norm_mix_g', 'delta_w_in', 'delta_conv_a_w', 'delta_conv_qkv_w', 'delta_a_log', 'delta_dt_bias', 'delta_dn_norm_g', 'delta_w_out', 'delta_norm_ffn_g', 'delta_w_up', 'delta_conv_ffn_w', 'delta_w_down', 'delta_norm_ple_g', 'delta_w_ple_gate', 'delta_w_ple_proj', 'delta_final_norm_g', 'new_m_norm_mix_g', 'new_m_w_in', 'new_m_conv_a_w', 'new_m_conv_qkv_w', 'new_m_a_log', 'new_m_dt_bias', 'new_m_dn_norm_g', 'new_m_w_out', 'new_m_norm_ffn_g', 'new_m_w_up', 'new_m_conv_ffn_w', 'new_m_w_down', 'new_m_norm_ple_g', 'new_m_w_ple_gate', 'new_m_w_ple_proj', 'new_m_final_norm_g', 'new_v_norm_mix_g', 'new_v_w_in', 'new_v_conv_a_w', 'new_v_conv_qkv_w', 'new_v_a_log', 'new_v_dt_bias', 'new_v_dn_norm_g', 'new_v_w_out', 'new_v_norm_ffn_g', 'new_v_w_up', 'new_v_conv_ffn_w', 'new_v_w_down', 'new_v_norm_ple_g', 'new_v_w_ple_gate', 'new_v_w_ple_proj', 'new_v_final_norm_g']
TWIN_LEAF_KINDS = {'loss': 'loss', 'grad_x': 'grad_x', 'grad_norm_mix_g': 'grad_w', 'grad_w_in': 'grad_w', 'grad_conv_a_w': 'grad_w', 'grad_conv_qkv_w': 'grad_w', 'grad_a_log': 'grad_w', 'grad_dt_bias': 'grad_w', 'grad_dn_norm_g': 'grad_w', 'grad_w_out': 'grad_w', 'grad_norm_ffn_g': 'grad_w', 'grad_w_up': 'grad_w', 'grad_conv_ffn_w': 'grad_w', 'grad_w_down': 'grad_w', 'grad_norm_ple_g': 'grad_w', 'grad_w_ple_gate': 'grad_w', 'grad_w_ple_proj': 'grad_w', 'grad_final_norm_g': 'grad_w', 'delta_norm_mix_g': 'delta_w', 'delta_w_in': 'delta_w', 'delta_conv_a_w': 'delta_w', 'delta_conv_qkv_w': 'delta_w', 'delta_a_log': 'delta_w', 'delta_dt_bias': 'delta_w', 'delta_dn_norm_g': 'delta_w', 'delta_w_out': 'delta_w', 'delta_norm_ffn_g': 'delta_w', 'delta_w_up': 'delta_w', 'delta_conv_ffn_w': 'delta_w', 'delta_w_down': 'delta_w', 'delta_norm_ple_g': 'delta_w', 'delta_w_ple_gate': 'delta_w', 'delta_w_ple_proj': 'delta_w', 'delta_final_norm_g': 'delta_w', 'new_m_norm_mix_g': 'new_m', 'new_m_w_in': 'new_m', 'new_m_conv_a_w': 'new_m', 'new_m_conv_qkv_w': 'new_m', 'new_m_a_log': 'new_m', 'new_m_dt_bias': 'new_m', 'new_m_dn_norm_g': 'new_m', 'new_m_w_out': 'new_m', 'new_m_norm_ffn_g': 'new_m', 'new_m_w_up': 'new_m', 'new_m_conv_ffn_w': 'new_m', 'new_m_w_down': 'new_m', 'new_m_norm_ple_g': 'new_m', 'new_m_w_ple_gate': 'new_m', 'new_m_w_ple_proj': 'new_m', 'new_m_final_norm_g': 'new_m', 'new_v_norm_mix_g': 'new_v', 'new_v_w_in': 'new_v', 'new_v_conv_a_w': 'new_v', 'new_v_conv_qkv_w': 'new_v', 'new_v_a_log': 'new_v', 'new_v_dt_bias': 'new_v', 'new_v_dn_norm_g': 'new_v', 'new_v_w_out': 'new_v', 'new_v_norm_ffn_g': 'new_v', 'new_v_w_up': 'new_v', 'new_v_conv_ffn_w': 'new_v', 'new_v_w_down': 'new_v', 'new_v_norm_ple_g': 'new_v', 'new_v_w_ple_gate': 'new_v', 'new_v_w_ple_proj': 'new_v', 'new_v_final_norm_g': 'new_v'}


def _forward(args):
    return _fwd_reference(*[args[k] for k in FWD_PARAMS])


def _output_shape():
    def fwd():
        inp = _fwd_setup_inputs(0)
        return _fwd_reference(*[inp[k] for k in FWD_PARAMS])
    out = _jax.eval_shape(fwd)
    return out.shape, out.dtype

N_MICROBATCH = 1
ADAM_LR = 0.001
ADAM_B1 = 0.9
ADAM_B2 = 0.999
ADAM_EPS = 1e-08
ADAM_WD = 0.01
ADAM_STEP = 10
PER_EXAMPLE_BATCH_AXIS = {'x': 0, 'p': 1, 'loss_target': 0}
SHARED_INPUTS = []
_WEIGHT_DTYPES = {'norm_mix_g': _jnp.float32, 'w_in': _jnp.float32, 'conv_a_w': _jnp.float32, 'conv_qkv_w': _jnp.float32, 'a_log': _jnp.float32, 'dt_bias': _jnp.float32, 'dn_norm_g': _jnp.float32, 'w_out': _jnp.float32, 'norm_ffn_g': _jnp.float32, 'w_up': _jnp.float32, 'conv_ffn_w': _jnp.float32, 'w_down': _jnp.float32, 'norm_ple_g': _jnp.float32, 'w_ple_gate': _jnp.float32, 'w_ple_proj': _jnp.float32, 'final_norm_g': _jnp.float32}
MOMENT_SCALE = {'norm_mix_g': 9.960312e-02, 'w_in': 5.363591e-02, 'conv_a_w': 7.230514e-02, 'conv_qkv_w': 3.229897e-02, 'a_log': 5.005526e-02, 'dt_bias': 4.143356e-02, 'dn_norm_g': 1.075189e-01, 'w_out': 5.717252e-02, 'norm_ffn_g': 5.332497e-02, 'w_up': 2.236457e-02, 'conv_ffn_w': 2.240560e-02, 'w_down': 3.642891e-02, 'norm_ple_g': 1.233636e-02, 'w_ple_gate': 1.237010e-02, 'w_ple_proj': 3.180796e-02, 'final_norm_g': 1.599043e+01}


def _to_microbatches(a, axis):
    t = _jnp.moveaxis(a, axis, 0)
    t = t.reshape((N_MICROBATCH, t.shape[0] // N_MICROBATCH) + t.shape[1:])
    return _jnp.moveaxis(t, 1, axis + 1)


def setup_inputs(seed: int = 0) -> dict:
    inp = _fwd_setup_inputs(seed)
    key = _jax.random.fold_in(_jax.random.key(seed), 7919)
    shape, _ = _output_shape()
    out = dict(inp)
    out["loss_target"] = _jax.random.normal(_jax.random.fold_in(key, 0), shape, _jnp.float32)
    for i, name in enumerate(TWIN_WEIGHTS):
        w = inp[name].astype(_jnp.float32)
        if MOMENT_SCALE is None:
            s = _jnp.sqrt(_jnp.mean(_jnp.square(w)) + 1e-30)
        else:
            s = MOMENT_SCALE[name]
        km, kv = _jax.random.split(_jax.random.fold_in(key, i + 1))
        out[name] = w
        out["m_" + name] = s * _jax.random.normal(km, w.shape, _jnp.float32)
        out["v_" + name] = (s * s) * _jax.random.uniform(kv, w.shape, _jnp.float32, 0.5, 1.5)
    if N_MICROBATCH > 1:
        for name, axis in PER_EXAMPLE_BATCH_AXIS.items():
            out[name] = _to_microbatches(out[name], axis)
    return {'x': out['x'], 'p': out['p'], 'norm_mix_g': out['norm_mix_g'], 'w_in': out['w_in'], 'conv_a_w': out['conv_a_w'], 'conv_qkv_w': out['conv_qkv_w'], 'a_log': out['a_log'], 'dt_bias': out['dt_bias'], 'dn_norm_g': out['dn_norm_g'], 'w_out': out['w_out'], 'norm_ffn_g': out['norm_ffn_g'], 'w_up': out['w_up'], 'conv_ffn_w': out['conv_ffn_w'], 'w_down': out['w_down'], 'norm_ple_g': out['norm_ple_g'], 'w_ple_gate': out['w_ple_gate'], 'w_ple_proj': out['w_ple_proj'], 'final_norm_g': out['final_norm_g'], 'loss_target': out['loss_target'], 'm_norm_mix_g': out['m_norm_mix_g'], 'm_w_in': out['m_w_in'], 'm_conv_a_w': out['m_conv_a_w'], 'm_conv_qkv_w': out['m_conv_qkv_w'], 'm_a_log': out['m_a_log'], 'm_dt_bias': out['m_dt_bias'], 'm_dn_norm_g': out['m_dn_norm_g'], 'm_w_out': out['m_w_out'], 'm_norm_ffn_g': out['m_norm_ffn_g'], 'm_w_up': out['m_w_up'], 'm_conv_ffn_w': out['m_conv_ffn_w'], 'm_w_down': out['m_w_down'], 'm_norm_ple_g': out['m_norm_ple_g'], 'm_w_ple_gate': out['m_w_ple_gate'], 'm_w_ple_proj': out['m_w_ple_proj'], 'm_final_norm_g': out['m_final_norm_g'], 'v_norm_mix_g': out['v_norm_mix_g'], 'v_w_in': out['v_w_in'], 'v_conv_a_w': out['v_conv_a_w'], 'v_conv_qkv_w': out['v_conv_qkv_w'], 'v_a_log': out['v_a_log'], 'v_dt_bias': out['v_dt_bias'], 'v_dn_norm_g': out['v_dn_norm_g'], 'v_w_out': out['v_w_out'], 'v_norm_ffn_g': out['v_norm_ffn_g'], 'v_w_up': out['v_w_up'], 'v_conv_ffn_w': out['v_conv_ffn_w'], 'v_w_down': out['v_w_down'], 'v_norm_ple_g': out['v_norm_ple_g'], 'v_w_ple_gate': out['v_w_ple_gate'], 'v_w_ple_proj': out['v_w_ple_proj'], 'v_final_norm_g': out['v_final_norm_g']}


def _loss(weights, diff, rest, loss_target):
    with _jax.named_scope("forward"):
        args = {**rest, TWIN_DIFF_INPUT: diff, **{k: w.astype(_WEIGHT_DTYPES[k]) for k, w in weights.items()}}
        y = _forward(args)
    with _jax.named_scope("loss_head"):
        err = _jnp.square(y.astype(_jnp.float32) - loss_target)
        return 0.5 * _jnp.sum(_jnp.mean(err, axis=-1)) if err.ndim else 0.5 * err


def _adamw(w, g, m, v):
    m = ADAM_B1 * m + (1.0 - ADAM_B1) * g
    v = ADAM_B2 * v + (1.0 - ADAM_B2) * _jnp.square(g)
    m_hat = m / (1.0 - ADAM_B1 ** ADAM_STEP)
    v_hat = v / (1.0 - ADAM_B2 ** ADAM_STEP)
    delta = -ADAM_LR * (m_hat / (_jnp.sqrt(v_hat) + ADAM_EPS) + ADAM_WD * w)
    return delta, m, v


def reference(x, p, norm_mix_g, w_in, conv_a_w, conv_qkv_w, a_log, dt_bias, dn_norm_g, w_out, norm_ffn_g, w_up, conv_ffn_w, w_down, norm_ple_g, w_ple_gate, w_ple_proj, final_norm_g, loss_target, m_norm_mix_g, m_w_in, m_conv_a_w, m_conv_qkv_w, m_a_log, m_dt_bias, m_dn_norm_g, m_w_out, m_norm_ffn_g, m_w_up, m_conv_ffn_w, m_w_down, m_norm_ple_g, m_w_ple_gate, m_w_ple_proj, m_final_norm_g, v_norm_mix_g, v_w_in, v_conv_a_w, v_conv_qkv_w, v_a_log, v_dt_bias, v_dn_norm_g, v_w_out, v_norm_ffn_g, v_w_up, v_conv_ffn_w, v_w_down, v_norm_ple_g, v_w_ple_gate, v_w_ple_proj, v_final_norm_g):
    given = dict(x=x, p=p, norm_mix_g=norm_mix_g, w_in=w_in, conv_a_w=conv_a_w, conv_qkv_w=conv_qkv_w, a_log=a_log, dt_bias=dt_bias, dn_norm_g=dn_norm_g, w_out=w_out, norm_ffn_g=norm_ffn_g, w_up=w_up, conv_ffn_w=conv_ffn_w, w_down=w_down, norm_ple_g=norm_ple_g, w_ple_gate=w_ple_gate, w_ple_proj=w_ple_proj, final_norm_g=final_norm_g, loss_target=loss_target, m_norm_mix_g=m_norm_mix_g, m_w_in=m_w_in, m_conv_a_w=m_conv_a_w, m_conv_qkv_w=m_conv_qkv_w, m_a_log=m_a_log, m_dt_bias=m_dt_bias, m_dn_norm_g=m_dn_norm_g, m_w_out=m_w_out, m_norm_ffn_g=m_norm_ffn_g, m_w_up=m_w_up, m_conv_ffn_w=m_conv_ffn_w, m_w_down=m_w_down, m_norm_ple_g=m_norm_ple_g, m_w_ple_gate=m_w_ple_gate, m_w_ple_proj=m_w_ple_proj, m_final_norm_g=m_final_norm_g, v_norm_mix_g=v_norm_mix_g, v_w_in=v_w_in, v_conv_a_w=v_conv_a_w, v_conv_qkv_w=v_conv_qkv_w, v_a_log=v_a_log, v_dt_bias=v_dt_bias, v_dn_norm_g=v_dn_norm_g, v_w_out=v_w_out, v_norm_ffn_g=v_norm_ffn_g, v_w_up=v_w_up, v_conv_ffn_w=v_conv_ffn_w, v_w_down=v_w_down, v_norm_ple_g=v_norm_ple_g, v_w_ple_gate=v_w_ple_gate, v_w_ple_proj=v_w_ple_proj, v_final_norm_g=v_final_norm_g)
    weights = {n: given[n] for n in TWIN_WEIGHTS}
    shared = {n: given[n] for n in SHARED_INPUTS}
    per_example = {n: given[n] for n in ['x', 'p']}
    grad_fn = _jax.value_and_grad(_loss, argnums=(0, 1))

    def one_microbatch(ex, loss_target):
        ex = dict(ex)
        diff = ex.pop(TWIN_DIFF_INPUT)
        return grad_fn(weights, diff, {**shared, **ex}, loss_target)

    if N_MICROBATCH == 1:
        loss, (grad_w, grad_x) = one_microbatch(per_example, given["loss_target"])
    else:
        def body(carry, xs):
            loss_sum, grad_sum = carry
            l_k, (gw_k, gx_k) = one_microbatch(xs[0], xs[1])
            with _jax.named_scope("update"):
                return (loss_sum + l_k, _jax.tree.map(_jnp.add, grad_sum, gw_k)), gx_k

        init = (_jnp.zeros((), _jnp.float32), _jax.tree.map(_jnp.zeros_like, weights))
        (loss, grad_w), grad_x = _jax.lax.scan(body, init, (per_example, given["loss_target"]))
    with _jax.named_scope("update"):
        delta_w, new_m, new_v = {}, {}, {}
        for n in TWIN_WEIGHTS:
            delta_w[n], new_m[n], new_v[n] = _adamw(weights[n], grad_w[n], given["m_" + n], given["v_" + n])
    return (loss, grad_x, *[grad_w[n] for n in TWIN_WEIGHTS], *[delta_w[n] for n in TWIN_WEIGHTS],
            *[new_m[n] for n in TWIN_WEIGHTS], *[new_v[n] for n in TWIN_WEIGHTS])
```

```python
import functools

import jax
import jax.numpy as jnp
from jax import lax
from jax.experimental import pallas as pl
from jax.experimental.pallas import tpu as pltpu

F32 = jnp.float32
BF16 = jnp.bfloat16
LANES = 128
HALO = 8
HEAD_DIM = 128
CHUNK = 64
EPS = 1e-6
VMEM_LIMIT = 48 * 1024 * 1024
HI = lax.Precision.HIGHEST
MESH = pl.DeviceIdType.MESH

ADAM_LR, ADAM_B1, ADAM_B2, ADAM_EPS, ADAM_WD, ADAM_STEP = 0.001, 0.9, 0.999, 1e-08, 0.01, 10


def _tile(n, cap, unit):
    if n <= cap:
        return n
    d = (cap // unit) * unit
    while d >= unit:
        if n % d == 0:
            return d
        d -= unit
    raise ValueError(f"no tile for {n} (cap {cap}, unit {unit})")


def _sigmoid(x):
    return 1.0 / (1.0 + jnp.exp(-x))


def _mm(a, b, *, mode, out_dtypes, name, epi=None, extras=(), tm_cap=512, tn_cap=512, tk_cap=512):
    if mode == "nn":
        (M, K), N = a.shape, b.shape[1]
    elif mode == "nt":
        (M, K), N = a.shape, b.shape[0]
    else:
        (K, M), N = a.shape, b.shape[1]
    tm, tn, tk = _tile(M, tm_cap, LANES), _tile(N, tn_cap, LANES), _tile(K, tk_cap, LANES)
    nk = K // tk
    a_spec = pl.BlockSpec((tk, tm), lambda i, j, k: (k, i)) if mode == "tn" else pl.BlockSpec((tm, tk), lambda i, j, k: (i, k))
    b_spec = pl.BlockSpec((tn, tk), lambda i, j, k: (j, k)) if mode == "nt" else pl.BlockSpec((tk, tn), lambda i, j, k: (k, j))
    mn_spec = pl.BlockSpec((tm, tn), lambda i, j, k: (i, j))
    dims = {"nn": (((1,), (0,)), ((), ())), "nt": (((1,), (1,)), ((), ())), "tn": (((0,), (0,)), ((), ()))}[mode]
    n_ex, n_out = len(extras), len(out_dtypes)

    def body(*refs):
        a_ref, b_ref = refs[0], refs[1]
        ex_refs = refs[2:2 + n_ex]
        out_refs = refs[2 + n_ex:2 + n_ex + n_out]
        acc = refs[-1]
        k = pl.program_id(2)

        @pl.when(k == 0)
        def _():
            acc[...] = jnp.zeros_like(acc)

        acc[...] += lax.dot_general(a_ref[...].astype(BF16), b_ref[...].astype(BF16), dims, preferred_element_type=F32)

        @pl.when(k == nk - 1)
        def _():
            outs = (acc[...],) if epi is None else epi(acc[...], *[r[...] for r in ex_refs])
            for r, o in zip(out_refs, outs):
                r[...] = o.astype(r.dtype)

    outs = pl.pallas_call(
        body, name=name, grid=(M // tm, N // tn, nk),
        in_specs=[a_spec, b_spec] + [mn_spec] * n_ex,
        out_specs=[mn_spec] * n_out,
        out_shape=[jax.ShapeDtypeStruct((M, N), dt) for dt in out_dtypes],
        scratch_shapes=[pltpu.VMEM((tm, tn), F32)],
        compiler_params=pltpu.CompilerParams(dimension_semantics=("parallel", "parallel", "arbitrary"),
                                             vmem_limit_bytes=VMEM_LIMIT),
    )(a, b, *extras)
    return outs[0] if n_out == 1 else outs


def _tiled(fn, *, T, C, ins, out_dtypes=(), acc_rows=(), tb=256, cb=512, name):
    tb = _tile(T, tb, HALO)
    nI, nJ = T // tb, C // cb
    hb, nH = tb // HALO, T // HALO
    specs, args, kinds = [], [], []
    for kind, arr, cmap in ins:
        cm = cmap if cmap is not None else (lambda j: j)
        kinds.append(kind)
        if kind == "cur":
            specs.append(pl.BlockSpec((tb, cb), lambda j, i, cm=cm: (i, cm(j))))
            args.append(arr)
        elif kind == "ext":
            specs.append(pl.BlockSpec((HALO, cb), lambda j, i, cm=cm: (jnp.maximum(i * hb - 1, 0), cm(j))))
            specs.append(pl.BlockSpec((tb, cb), lambda j, i, cm=cm: (i, cm(j))))
            specs.append(pl.BlockSpec((HALO, cb), lambda j, i, cm=cm: (jnp.minimum((i + 1) * hb, nH - 1), cm(j))))
            args += [arr, arr, arr]
        elif kind == "row":
            specs.append(pl.BlockSpec((arr.shape[0], cb), lambda j, i, cm=cm: (0, cm(j))))
            args.append(arr)
        elif kind == "stack":
            specs.append(pl.BlockSpec((arr.shape[0], tb, cb), lambda j, i, cm=cm: (0, i, cm(j))))
            args.append(arr)
        else:
            raise ValueError(kind)
    n_in = len(args)
    n_out, n_acc = len(out_dtypes), len(acc_rows)

    def body(*refs):
        j, i = pl.program_id(0), pl.program_id(1)
        vals, r = [], 0
        for kind in kinds:
            if kind == "ext":
                prev = jnp.where(i == 0, 0.0, refs[r][...].astype(F32))
                cur = refs[r + 1][...].astype(F32)
                nxt = jnp.where(i == nI - 1, 0.0, refs[r + 2][...].astype(F32))
                vals.append(jnp.concatenate([prev, cur, nxt], axis=0))
                r += 3
            else:
                vals.append(refs[r][...])
                r += 1
        res = fn(j, i, *vals)
        for ref, o in zip(refs[n_in:n_in + n_out], res[:n_out]):
            ref[...] = o.astype(ref.dtype)
        for ref, o in zip(refs[n_in + n_out:], res[n_out:]):
            @pl.when(i == 0)
            def _(ref=ref, o=o):
                ref[...] = o

            @pl.when(i > 0)
            def _(ref=ref, o=o):
                ref[...] += o

    outs = pl.pallas_call(
        body, name=name, grid=(nJ, nI), in_specs=specs,
        out_specs=[pl.BlockSpec((tb, cb), lambda j, i: (i, j))] * n_out
        + [pl.BlockSpec((rows, cb), lambda j, i: (0, j)) for rows in acc_rows],
        out_shape=[jax.ShapeDtypeStruct((T, C), dt) for dt in out_dtypes]
        + [jax.ShapeDtypeStruct((rows, C), F32) for rows in acc_rows],
        compiler_params=pltpu.CompilerParams(dimension_semantics=("parallel", "arbitrary"),
                                             vmem_limit_bytes=VMEM_LIMIT),
    )(*args)
    return outs


def _conv_causal(xe, w):
    K = w.shape[0]
    y = xe * w[K - 1:K]
    for j in range(K - 1):
        y = y + pltpu.roll(xe, K - 1 - j, 0) * w[j:j + 1]
    return y


def _conv_anti(de, w):
    K, n = w.shape[0], de.shape[0]
    y = de * w[K - 1:K]
    for j in range(K - 1):
        y = y + pltpu.roll(de, n - (K - 1 - j), 0) * w[j:j + 1]
    return y


def _conv_dw(dce, xe, K):
    n = dce.shape[0]
    tb = n - 2 * HALO
    rows = []
    for j in range(K):
        xs = xe if j == K - 1 else pltpu.roll(xe, K - 1 - j, 0)
        rows.append(jnp.sum((dce * xs)[HALO:HALO + tb], axis=0, keepdims=True))
    rows.append(jnp.zeros((HALO - K, dce.shape[1]), F32))
    return jnp.concatenate(rows, axis=0)


def _own(xe):
    return xe[HALO:xe.shape[0] - HALO]


def _row0(v):
    return jnp.concatenate([v, jnp.zeros((HALO - 1, v.shape[1]), F32)], axis=0)


def _per_head(fn, *xs):
    n = xs[0].shape[1] // HEAD_DIM
    outs = [fn(*[x[:, g * HEAD_DIM:(g + 1) * HEAD_DIM] for x in xs]) for g in range(n)]
    return outs[0] if n == 1 else jnp.concatenate(outs, axis=1)


def _rms_fwd(x, g, name):
    T, D = x.shape

    def fn(j, i, xv, gv):
        r = lax.rsqrt(jnp.mean(xv * xv, axis=1, keepdims=True) + EPS)
        return (xv * r * gv,)

    return _tiled(fn, T=T, C=D, ins=[("cur", x, None), ("row", g, None)], out_dtypes=[BF16], cb=D, name=name)[0]


def _rms_bwd_math(dy, xv, gv):
    r = lax.rsqrt(jnp.mean(xv * xv, axis=1, keepdims=True) + EPS)
    xh = xv * r
    dxh = dy * gv
    dx = r * (dxh - xh * jnp.mean(dxh * xh, axis=1, keepdims=True))
    dg = jnp.sum(dy * xh, axis=0, keepdims=True)
    return dx, dg


def _rms_bwd(dh, x, g, dres, name):
    T, D = x.shape

    def fn(j, i, dhv, xv, gv, dr):
        dx, dg = _rms_bwd_math(dhv, xv, gv)
        return dr + dx, _row0(dg)

    return _tiled(fn, T=T, C=D, ins=[("cur", dh, None), ("cur", x, None), ("row", g, None), ("cur", dres, None)],
                  out_dtypes=[F32], acc_rows=[HALO], cb=D, name=name)


def _final_fb(x3, tgt, g):
    T, D = x3.shape

    def fn(j, i, xv, tv, gv):
        r = lax.rsqrt(jnp.mean(xv * xv, axis=1, keepdims=True) + EPS)
        xh = xv * r
        e = xh * gv - tv
        dy = e * (1.0 / D)
        dxh = dy * gv
        dx = r * (dxh - xh * jnp.mean(dxh * xh, axis=1, keepdims=True))
        dg = jnp.sum(dy * xh, axis=0, keepdims=True)
        ls = jnp.sum(e * e, axis=0, keepdims=True) * (0.5 / D)
        return dx, jnp.concatenate([dg, ls, jnp.zeros((HALO - 2, D), F32)], axis=0)

    return _tiled(fn, T=T, C=D, ins=[("cur", x3, None), ("cur", tgt, None), ("row", g, None)],
                  out_dtypes=[F32], acc_rows=[HALO], cb=D, name="final_fb")


def _ga_fwd(proj, w_a, CW, cb):
    T = proj.shape[0]
    n = CW // cb

    def fn(j, i, ax, ab, ac, w):
        c = _conv_causal(ac * ax, w)
        return (ab * _own(c),)

    return _tiled(fn, T=T, C=CW, ins=[("ext", proj, None), ("cur", proj, lambda j: j + n), ("ext", proj, lambda j: j + 2 * n),
                                       ("row", w_a, None)], out_dtypes=[BF16], cb=cb, name="ga_fwd")[0]


def _ga_bwd(dymix, proj, w_a, CW, cb):
    T = proj.shape[0]
    n = CW // cb
    K = w_a.shape[0]

    def fn(j, i, dy, ax, ab, ac, w):
        u = ac * ax
        c = _conv_causal(u, w)
        dc = dy * ab
        du = _conv_anti(dc, w)
        return _own(du * ac), _own(dy * c), _own(du * ax), _conv_dw(dc, u, K)

    return _tiled(fn, T=T, C=CW, ins=[("ext", dymix, None), ("ext", proj, None), ("ext", proj, lambda j: j + n),
                                       ("ext", proj, lambda j: j + 2 * n), ("row", w_a, None)],
                  out_dtypes=[BF16, BF16, BF16], acc_rows=[HALO], cb=cb, name="ga_bwd")


def _l2n(s):
    return s * lax.rsqrt(jnp.sum(s * s, axis=1, keepdims=True) + EPS)


def _qkv_fwd(proj, w_sec, coff, normalize, DNW, cb, name):
    T = proj.shape[0]

    def fn(j, i, pre, w):
        c = _own(_conv_causal(pre, w))
        s = c * _sigmoid(c)
        return (_per_head(_l2n, s) if normalize else s,)

    return _tiled(fn, T=T, C=DNW, ins=[("ext", proj, lambda j: j + coff), ("row", w_sec, None)],
                  out_dtypes=[F32], cb=cb, name=name)[0]


def _qkv_bwd(dsec, proj, w_sec, coff, normalize, DNW, cb, name):
    T = proj.shape[0]
    K = w_sec.shape[0]

    def l2n_bwd(s, dn):
        r = lax.rsqrt(jnp.sum(s * s, axis=1, keepdims=True) + EPS)
        nrm = s * r
        return r * (dn - nrm * jnp.sum(dn * nrm, axis=1, keepdims=True))

    def fn(j, i, dn, pre, w):
        c = _conv_causal(pre, w)
        sg = _sigmoid(c)
        s = c * sg
        ds = _per_head(l2n_bwd, s, dn) if normalize else dn
        dc = ds * (sg * (1.0 + c * (1.0 - sg)))
        return _own(_conv_anti(dc, w)), _conv_dw(dc, pre, K)

    return _tiled(fn, T=T, C=DNW, ins=[("ext", dsec, None), ("ext", proj, lambda j: j + coff), ("row", w_sec, None)],
                  out_dtypes=[BF16], acc_rows=[HALO], cb=cb, name=name)


def _gb_fwd(small, a_log_row, dt_row, H):
    T = small.shape[0]

    def fn(j, i, sm, al, dt):
        z = sm + dt
        sp = jnp.maximum(z, 0.0) + jnp.log(1.0 + jnp.exp(-jnp.abs(z)))
        g = -jnp.exp(al) * sp
        beta = _sigmoid(pltpu.roll(sm, LANES - H, 1))
        return g, beta

    return _tiled(fn, T=T, C=LANES, ins=[("cur", small, None), ("row", a_log_row, None), ("row", dt_row, None)],
                  out_dtypes=[F32, F32], cb=LANES, name="gb_fwd")


def _gb_bwd(dgB, dbB, small, g, beta, a_log_row, dt_row, H):
    T = small.shape[0]

    def fn(j, i, dgv, dbv, sm, gv, bv, al, dt):
        lane = lax.broadcasted_iota(jnp.int32, sm.shape, 1)
        dg = jnp.zeros(sm.shape, F32)
        db = jnp.zeros(sm.shape, F32)
        for h in range(H):
            dg = jnp.where(lane == h, jnp.sum(dgv[h], axis=1, keepdims=True), dg)
            db = jnp.where(lane == h, jnp.sum(dbv[h], axis=1, keepdims=True), db)
        da = dg * (-jnp.exp(al)) * _sigmoid(sm + dt)
        dbb = db * bv * (1.0 - bv)
        dsm = jnp.where(lane < H, da, 0.0) + pltpu.roll(jnp.where(lane < H, dbb, 0.0), H, 1)
        d_alog = jnp.sum(jnp.where(lane < H, dg * gv, 0.0), axis=0, keepdims=True)
        d_dt = jnp.sum(jnp.where(lane < H, da, 0.0), axis=0, keepdims=True)
        return dsm, jnp.concatenate([d_alog, d_dt, jnp.zeros((HALO - 2, LANES), F32)], axis=0)

    return _tiled(fn, T=T, C=LANES, ins=[("stack", dgB, None), ("stack", dbB, None), ("cur", small, None), ("cur", g, None),
                                          ("cur", beta, None), ("row", a_log_row, None), ("row", dt_row, None)],
                  out_dtypes=[BF16], acc_rows=[HALO], cb=LANES, name="gb_bwd")


def _dot(a, b):
    return jnp.dot(a, b, precision=HI, preferred_element_type=F32)


def _dot_nt(a, b):
    return lax.dot_general(a, b, (((1,), (1,)), ((), ())), precision=HI, preferred_element_type=F32)


def _dot_tn(a, b):
    return lax.dot_general(a, b, (((0,), (0,)), ((), ())), precision=HI, preferred_element_type=F32)


def _chunk_fn(q, k, v, gB, bB, S):
    C = CHUNK
    row = lax.broadcasted_iota(jnp.int32, (C, C), 0)
    col = lax.broadcasted_iota(jnp.int32, (C, C), 1)
    causal = row >= col
    strict = row > col
    tril = jnp.where(causal, 1.0, 0.0).astype(F32)
    eye = jnp.where(row == col, 1.0, 0.0).astype(F32)
    gc = _dot(tril, gB)
    R = _dot_nt(jnp.full((C, HEAD_DIM), 1.0 / HEAD_DIM, F32), gc)
    decay = jnp.where(causal, jnp.exp(jnp.where(causal, gc[:, :C] - R, 0.0)), 0.0)
    L = jnp.where(strict, _dot_nt(k, k) * decay * bB[:, :C], 0.0)
    inv = eye - L
    P = L
    for _ in range(5):
        P = _dot(P, P)
        inv = _dot(inv, eye + P)
    eg = jnp.exp(gc)
    u = _dot(inv, v * bB)
    w = _dot(inv, k * bB * eg)
    qs = q * (HEAD_DIM ** -0.5)
    qk = _dot_nt(qs, k) * decay
    gl = gc[C - 1:C, :]
    v_new = u - _dot(w, S)
    o = _dot(qs * eg, S) + _dot(qk, v_new)
    S_new = S * jnp.exp(gl) + _dot_tn(k * jnp.exp(gl - gc), v_new)
    return o, S_new


def _sel_lane(x, h):
    lane = lax.broadcasted_iota(jnp.int32, x.shape, 1)
    return jnp.broadcast_to(jnp.sum(jnp.where(lane == h, x, 0.0), axis=1, keepdims=True), x.shape)


def _delta_fwd(q, k, v, g, beta):
    T = q.shape[0]
    H, N = q.shape[1] // HEAD_DIM, T // CHUNK

    def body(q_ref, k_ref, v_ref, g_ref, b_ref, o_ref, s_ref, S):
        h, n = pl.program_id(0), pl.program_id(1)

        @pl.when(n == 0)
        def _():
            S[...] = jnp.zeros_like(S)

        s_ref[0, 0] = S[...]
        o, S_new = _chunk_fn(q_ref[...], k_ref[...], v_ref[...], _sel_lane(g_ref[...], h), _sel_lane(b_ref[...], h), S[...])
        o_ref[...] = o
        S[...] = S_new

    blk = pl.BlockSpec((CHUNK, HEAD_DIM), lambda h, n: (n, h))
    gblk = pl.BlockSpec((CHUNK, LANES), lambda h, n: (n, 0))
    return pl.pallas_call(
        body, name="delta_fwd", grid=(H, N), in_specs=[blk, blk, blk, gblk, gblk],
        out_specs=[blk, pl.BlockSpec((1, 1, HEAD_DIM, HEAD_DIM), lambda h, n: (h, n, 0, 0))],
        out_shape=[jax.ShapeDtypeStruct((T, H * HEAD_DIM), F32), jax.ShapeDtypeStruct((H, N, HEAD_DIM, HEAD_DIM), F32)],
        scratch_shapes=[pltpu.VMEM((HEAD_DIM, HEAD_DIM), F32)],
        compiler_params=pltpu.CompilerParams(dimension_semantics=("parallel", "arbitrary")),
    )(q, k, v, g, beta)


def _delta_bwd(q, k, v, g, beta, S0, do):
    T = q.shape[0]
    H, N = q.shape[1] // HEAD_DIM, T // CHUNK

    def body(q_ref, k_ref, v_ref, g_ref, b_ref, s_ref, do_ref, dq_ref, dk_ref, dv_ref, dg_ref, db_ref, dS):
        h, n = pl.program_id(0), pl.program_id(1)

        @pl.when(n == 0)
        def _():
            dS[...] = jnp.zeros_like(dS)

        _, vjp = jax.vjp(_chunk_fn, q_ref[...], k_ref[...], v_ref[...], _sel_lane(g_ref[...], h),
                         _sel_lane(b_ref[...], h), s_ref[0, 0])
        dq, dk, dv, dgB, dbB, dS_prev = vjp((do_ref[...], dS[...]))
        dq_ref[...] = dq
        dk_ref[...] = dk
        dv_ref[...] = dv
        dg_ref[0] = dgB
        db_ref[0] = dbB
        dS[...] = dS_prev

    blk = pl.BlockSpec((CHUNK, HEAD_DIM), lambda h, n: (N - 1 - n, h))
    gblk = pl.BlockSpec((CHUNK, LANES), lambda h, n: (N - 1 - n, 0))
    hblk = pl.BlockSpec((1, CHUNK, LANES), lambda h, n: (h, N - 1 - n, 0))
    sd = jax.ShapeDtypeStruct
    return pl.pallas_call(
        body, name="delta_bwd", grid=(H, N),
        in_specs=[blk, blk, blk, gblk, gblk, pl.BlockSpec((1, 1, HEAD_DIM, HEAD_DIM), lambda h, n: (h, N - 1 - n, 0, 0)), blk],
        out_specs=[blk, blk, blk, hblk, hblk],
        out_shape=[sd((T, H * HEAD_DIM), F32)] * 3 + [sd((H, T, LANES), F32)] * 2,
        scratch_shapes=[pltpu.VMEM((HEAD_DIM, HEAD_DIM), F32)],
        compiler_params=pltpu.CompilerParams(dimension_semantics=("parallel", "arbitrary")),
    )(q, k, v, g, beta, S0, do)


def _gnorm_fwd(o, proj, z_coff, gdn_t, DNW):
    T = o.shape[0]

    def fn(j, i, ov, zv, gv):
        def one(oh, zh, gh):
            r = lax.rsqrt(jnp.mean(oh * oh, axis=1, keepdims=True) + EPS)
            return oh * r * gh * (zh * _sigmoid(zh))
        return (_per_head(one, ov, zv, jnp.broadcast_to(gv, ov.shape)),)

    return _tiled(fn, T=T, C=DNW, ins=[("cur", o, None), ("cur", proj, lambda j: j + z_coff), ("row", gdn_t, None)],
                  out_dtypes=[BF16], cb=DNW, name="gnorm_fwd")[0]


def _gnorm_bwd(dymix, y_coff, o, proj, z_coff, gdn_t, DNW):
    T = o.shape[0]
    nh = DNW // HEAD_DIM

    def fn(j, i, dy, ov, zv, gv):
        dos, dzs, dgs = [], [], jnp.zeros((1, HEAD_DIM), F32)
        for h in range(nh):
            sl = slice(h * HEAD_DIM, (h + 1) * HEAD_DIM)
            dyh, oh, zh, gh = dy[:, sl].astype(F32), ov[:, sl], zv[:, sl], gv[:, sl]
            r = lax.rsqrt(jnp.mean(oh * oh, axis=1, keepdims=True) + EPS)
            on = oh * r
            sg = _sigmoid(zh)
            sz = zh * sg
            dzs.append(dyh * on * gh * (sg * (1.0 + zh * (1.0 - sg))))
            don = dyh * gh * sz
            dos.append(r * (don - on * jnp.mean(don * on, axis=1, keepdims=True)))
            dgs = dgs + jnp.sum(dyh * on * sz, axis=0, keepdims=True)
        cat = (lambda xs: xs[0] if nh == 1 else jnp.concatenate(xs, axis=1))
        return cat(dos), cat(dzs), _row0(dgs)

    T_ = T
    nI = T_ // _tile(T_, 256, HALO)
    tb = T_ // nI
    specs_cb = DNW

    def body_wrap():
        def body(dy_ref, o_ref, z_ref, g_ref, do_ref, dz_ref, dg_ref):
            i = pl.program_id(0)
            d_o, d_z, d_g = fn(0, i, dy_ref[...], o_ref[...], z_ref[...], g_ref[...])
            do_ref[...] = d_o
            dz_ref[...] = d_z.astype(dz_ref.dtype)

            @pl.when(i == 0)
            def _():
                dg_ref[...] = d_g

            @pl.when(i > 0)
            def _():
                dg_ref[...] += d_g

        return pl.pallas_call(
            body, name="gnorm_bwd", grid=(nI,),
            in_specs=[pl.BlockSpec((tb, specs_cb), lambda i: (i, y_coff)), pl.BlockSpec((tb, specs_cb), lambda i: (i, 0)),
                      pl.BlockSpec((tb, specs_cb), lambda i: (i, z_coff)), pl.BlockSpec((1, specs_cb), lambda i: (0, 0))],
            out_specs=[pl.BlockSpec((tb, specs_cb), lambda i: (i, 0)), pl.BlockSpec((tb, specs_cb), lambda i: (i, 0)),
                       pl.BlockSpec((HALO, HEAD_DIM), lambda i: (0, 0))],
            out_shape=[jax.ShapeDtypeStruct((T_, DNW), F32), jax.ShapeDtypeStruct((T_, DNW), BF16),
                       jax.ShapeDtypeStruct((HALO, HEAD_DIM), F32)],
            compiler_params=pltpu.CompilerParams(dimension_semantics=("arbitrary",), vmem_limit_bytes=VMEM_LIMIT),
        )(dymix, o, proj, gdn_t)

    return body_wrap()


def _ffn_fwd(up_g, up_v, w_g, w_v, cb):
    T, F = up_g.shape

    def fn(j, i, ug, uv, wg, wv):
        cg = _own(_conv_causal(ug, wg))
        cv = _own(_conv_causal(uv, wv))
        return (cg * _sigmoid(cg) * cv,)

    return _tiled(fn, T=T, C=F, ins=[("ext", up_g, None), ("ext", up_v, None), ("row", w_g, None), ("row", w_v, None)],
                  out_dtypes=[BF16], cb=cb, name="ffn_fwd")[0]


def _ffn_bwd(dact, up_g, up_v, w_g, w_v, cb):
    T, F = up_g.shape
    K = w_g.shape[0]

    def fn(j, i, da, ug, uv, wg, wv):
        cg = _conv_causal(ug, wg)
        cv = _conv_causal(uv, wv)
        sg = _sigmoid(cg)
        dgate = da * cv * (sg * (1.0 + cg * (1.0 - sg)))
        dval = da * (cg * sg)
        return (_own(_conv_anti(dgate, wg)), _own(_conv_anti(dval, wv)), _conv_dw(dgate, ug, K), _conv_dw(dval, uv, K))

    return _tiled(fn, T=T, C=F, ins=[("ext", dact, None), ("ext", up_g, None), ("ext", up_v, None), ("row", w_g, None),
                                      ("row", w_v, None)], out_dtypes=[BF16, BF16], acc_rows=[HALO, HALO], cb=cb, name="ffn_bwd")


def _ple_bwd(dx3, pp, pg):
    T, D = dx3.shape

    def fn(j, i, d, ppv, pgv):
        return d * ppv * pgv * (1.0 - pgv), d * pgv

    return _tiled(fn, T=T, C=D, ins=[("cur", dx3, None), ("cur", pp, None), ("cur", pg, None)],
                  out_dtypes=[BF16, BF16], cb=_tile(D, 512, LANES), name="ple_bwd")


def _adamw(w, g, m, v, name):
    R, Cc = w.shape
    cb = _tile(Cc, 512, LANES) if Cc % LANES == 0 else Cc
    c1 = 1.0 / (1.0 - ADAM_B1 ** ADAM_STEP)
    c2 = 1.0 / (1.0 - ADAM_B2 ** ADAM_STEP)

    def fn(j, i, wv, gv, mv, vv):
        m2 = ADAM_B1 * mv + (1.0 - ADAM_B1) * gv
        v2 = ADAM_B2 * vv + (1.0 - ADAM_B2) * (gv * gv)
        delta = -ADAM_LR * ((m2 * c1) / (jnp.sqrt(v2 * c2) + ADAM_EPS) + ADAM_WD * wv)
        return delta, m2, v2

    return _tiled(fn, T=R, C=Cc, ins=[("cur", w, None), ("cur", g, None), ("cur", m, None), ("cur", v, None)],
                  out_dtypes=[F32, F32, F32], cb=cb, name=name)


def _sum_stack(st, name):
    S, R, Cc = st.shape
    cb = _tile(Cc, 512, LANES) if Cc % LANES == 0 else Cc

    def fn(j, i, sv):
        t = sv[0]
        for s in range(1, S):
            t = t + sv[s]
        return (t,)

    return _tiled(fn, T=R, C=Cc, ins=[("stack", st, None)], out_dtypes=[F32], cb=cb, name=name)[0]


ANY = pl.BlockSpec(memory_space=pl.ANY)


def _place():
    x, y, c = lax.axis_index("x"), lax.axis_index("y"), lax.axis_index("c")
    return x, y, c, 2 * x + y


def _chip_dev(s, c):
    return (s // 2, s % 2, c)


def _gather_chips(shard, name):
    two, R2, Cc = shard.shape

    def body(sh_ref, out_ref, send1, recv1, send2, recv2, local):
        x, y, c, s = _place()
        mine = pltpu.make_async_copy(sh_ref, out_ref.at[s], local)
        mine.start()
        first = []
        for m in range(1, 4):
            t = s ^ m
            cp = pltpu.make_async_remote_copy(sh_ref.at[c], out_ref.at[s, c], send1.at[m - 1], recv1.at[m - 1],
                                              device_id=_chip_dev(t, c), device_id_type=MESH)
            cp.start()
            first.append(cp)
        passed = []
        for m in range(1, 4):
            t = s ^ m
            pltpu.make_async_remote_copy(sh_ref.at[c], out_ref.at[t, c], send1.at[m - 1], recv1.at[m - 1],
                                         device_id=_chip_dev(t, c), device_id_type=MESH).wait_recv()
            cp = pltpu.make_async_remote_copy(out_ref.at[t, c], out_ref.at[t, c], send2.at[m - 1], recv2.at[m - 1],
                                              device_id=(x, y, 1 - c), device_id_type=MESH)
            cp.start()
            passed.append(cp)
        for m in range(1, 4):
            t = s ^ m
            pltpu.make_async_remote_copy(out_ref.at[t, 1 - c], out_ref.at[t, 1 - c], send2.at[m - 1], recv2.at[m - 1],
                                         device_id=(x, y, 1 - c), device_id_type=MESH).wait_recv()
        for cp in first + passed:
            cp.wait_send()
        mine.wait()

    return pl.pallas_call(
        body, name=name, in_specs=[ANY], out_specs=ANY,
        out_shape=jax.ShapeDtypeStruct((4, two, R2, Cc), shard.dtype),
        scratch_shapes=[pltpu.SemaphoreType.DMA((3,)), pltpu.SemaphoreType.DMA((3,)), pltpu.SemaphoreType.DMA((3,)),
                        pltpu.SemaphoreType.DMA((3,)), pltpu.SemaphoreType.DMA],
    )(shard)


def _sibling_swap(G, name):
    S4, two, R2, Cc = G.shape

    def body(g_ref, out_ref, send, recv):
        x, y, c, s = _place()
        cps = []
        for t in range(S4):
            cp = pltpu.make_async_remote_copy(g_ref.at[t, 1 - c], out_ref.at[t], send.at[t], recv.at[t],
                                              device_id=(x, y, 1 - c), device_id_type=MESH)
            cp.start()
            cps.append(cp)
        for cp in cps:
            cp.wait()

    return pl.pallas_call(
        body, name=name, in_specs=[ANY], out_specs=ANY, out_shape=jax.ShapeDtypeStruct((S4, R2, Cc), G.dtype),
        scratch_shapes=[pltpu.SemaphoreType.DMA((S4,)), pltpu.SemaphoreType.DMA((S4,))],
    )(G)


def _add_half(G, A, cidx, name):
    S4, two, R2, Cc = G.shape
    tb = _tile(R2, 256, HALO)
    cb = _tile(Cc, 512, LANES) if Cc % LANES == 0 else Cc
    nI, nJ = R2 // tb, Cc // cb

    def body(c_ref, g_ref, a_ref, o_ref):
        o_ref[...] = g_ref[0, 0] + a_ref[0]

    grid_spec = pltpu.PrefetchScalarGridSpec(
        num_scalar_prefetch=1, grid=(S4, nI, nJ),
        in_specs=[pl.BlockSpec((1, 1, tb, cb), lambda t, i, j, c_ref: (t, c_ref[0], i, j)),
                  pl.BlockSpec((1, tb, cb), lambda t, i, j, c_ref: (t, i, j))],
        out_specs=pl.BlockSpec((tb, cb), lambda t, i, j, c_ref: (t * nI + i, j)))
    return pl.pallas_call(body, name=name, grid_spec=grid_spec, out_shape=jax.ShapeDtypeStruct((S4 * R2, Cc), F32),
                          compiler_params=pltpu.CompilerParams(dimension_semantics=("parallel", "parallel", "parallel")),
                          )(cidx, G, A)


def _chip_all_to_all(S1, name):
    S4, R2, Cc = S1.shape

    def body(s_ref, out_ref, send, recv, local):
        x, y, c, s = _place()
        mine = pltpu.make_async_copy(s_ref.at[s], out_ref.at[s], local)
        mine.start()
        cps = []
        for m in range(1, 4):
            t = s ^ m
            cp = pltpu.make_async_remote_copy(s_ref.at[t], out_ref.at[s], send.at[m - 1], recv.at[m - 1],
                                              device_id=_chip_dev(t, c), device_id_type=MESH)
            cp.start()
            cps.append(cp)
        for cp in cps:
            cp.wait()
        mine.wait()

    return pl.pallas_call(
        body, name=name, in_specs=[ANY], out_specs=ANY, out_shape=jax.ShapeDtypeStruct((S4, R2, Cc), S1.dtype),
        scratch_shapes=[pltpu.SemaphoreType.DMA((3,)), pltpu.SemaphoreType.DMA((3,)), pltpu.SemaphoreType.DMA],
    )(S1)


def _sibling_gather(Hs, name):
    R2, Cc = Hs.shape

    def body(h_ref, out_ref, send, recv, local):
        x, y, c, s = _place()
        mine = pltpu.make_async_copy(h_ref, out_ref.at[c], local)
        mine.start()
        cp = pltpu.make_async_remote_copy(h_ref, out_ref.at[c], send, recv, device_id=(x, y, 1 - c), device_id_type=MESH)
        cp.start()
        cp.wait()
        mine.wait()

    return pl.pallas_call(
        body, name=name, in_specs=[ANY], out_specs=ANY, out_shape=jax.ShapeDtypeStruct((2, R2, Cc), Hs.dtype),
        scratch_shapes=[pltpu.SemaphoreType.DMA, pltpu.SemaphoreType.DMA, pltpu.SemaphoreType.DMA],
    )(Hs)


def _gather_all(buf, name):
    R, Cc = buf.shape

    def body(b_ref, out_ref, send, recv, local):
        x, y, c, s = _place()
        d = 2 * s + c
        mine = pltpu.make_async_copy(b_ref, out_ref.at[d], local)
        mine.start()
        cps = []
        for m in range(1, 8):
            t = d ^ m
            cp = pltpu.make_async_remote_copy(b_ref, out_ref.at[d], send.at[m - 1], recv.at[m - 1],
                                              device_id=(t // 4, (t // 2) % 2, t % 2), device_id_type=MESH)
            cp.start()
            cps.append(cp)
        for cp in cps:
            cp.wait()
        mine.wait()

    return pl.pallas_call(
        body, name=name, in_specs=[ANY], out_specs=ANY, out_shape=jax.ShapeDtypeStruct((8, R, Cc), buf.dtype),
        scratch_shapes=[pltpu.SemaphoreType.DMA((7,)), pltpu.SemaphoreType.DMA((7,)), pltpu.SemaphoreType.DMA],
    )(buf)


def _reduce_to_shard(G, cidx, name):
    S4, R, Cc = G.shape
    R2 = R // 2
    G4 = G.reshape(S4, 2, R2, Cc)
    A = _sibling_swap(G4, name + "_swap")
    S1 = _add_half(G4, A, cidx, name + "_add").reshape(S4, R2, Cc)
    B = _chip_all_to_all(S1, name + "_a2a")
    Hs = _sum_stack(B, name + "_sum")
    return _sibling_gather(Hs, name + "_gather").reshape(R, Cc)


def _pack_rows(vs):
    flat = jnp.concatenate([v.reshape(-1) for v in vs])
    n = flat.shape[0]
    rows = -(-n // (LANES * 2 * HALO)) * 2 * HALO
    return jnp.pad(flat, (0, rows * LANES - n)).reshape(rows, LANES)


def _unpack_rows(buf, shapes):
    flat = buf.reshape(-1)
    outs, o = [], 0
    for shp in shapes:
        n = 1
        for d in shp:
            n *= d
        outs.append(flat[o:o + n].reshape(shp))
        o += n
    return outs


def kernel(x, p, norm_mix_g, w_in, conv_a_w, conv_qkv_w, a_log, dt_bias, dn_norm_g, w_out, norm_ffn_g, w_up, conv_ffn_w, w_down, norm_ple_g, w_ple_gate, w_ple_proj, final_norm_g, loss_target, m_norm_mix_g, m_w_in, m_conv_a_w, m_conv_qkv_w, m_a_log, m_dt_bias, m_dn_norm_g, m_w_out, m_norm_ffn_g, m_w_up, m_conv_ffn_w, m_w_down, m_norm_ple_g, m_w_ple_gate, m_w_ple_proj, m_final_norm_g, v_norm_mix_g, v_w_in, v_conv_a_w, v_conv_qkv_w, v_a_log, v_dt_bias, v_dn_norm_g, v_w_out, v_norm_ffn_g, v_w_up, v_conv_ffn_w, v_w_down, v_norm_ple_g, v_w_ple_gate, v_w_ple_proj, v_final_norm_g):
    xs = x[0]
    ps = p[0, 0]
    tgt = loss_target[0]
    T, D = xs.shape
    H = a_log.shape[-1]
    DNW = H * HEAD_DIM
    CW = conv_a_w.shape[-1] * 4
    F = w_down.shape[1] * 4
    PD = ps.shape[-1]
    IN_MAIN = 3 * CW + 4 * DNW
    IN_COLS = IN_MAIN + 2 * H
    assert w_in.shape[-1] * 4 == IN_COLS and CW + DNW == D and 2 * H <= LANES
    cb = _tile(min(CW, DNW), 512, LANES)
    while F % cb:
        cb -= LANES
    cidx = lax.axis_index("c").astype(jnp.int32).reshape(1)
    chip = 2 * lax.axis_index("x") + lax.axis_index("y")

    def gather_w(w, name):
        sh = w[0].astype(BF16)
        R, Cc = sh.shape
        return _gather_chips(sh.reshape(2, R // 2, Cc), name).reshape(4, R, Cc)

    def cols(g4):
        return jnp.transpose(g4, (1, 0, 2)).reshape(g4.shape[1], 4 * g4.shape[2])

    def rows(g4):
        return g4.reshape(4 * g4.shape[1], g4.shape[2])

    w_in_f = cols(gather_w(w_in, "ag_w_in"))
    w_in_main = w_in_f[:, :IN_MAIN]
    w_in_small = jnp.pad(w_in_f[:, IN_MAIN:], ((0, 0), (0, LANES - 2 * H)))
    w_out_f = rows(gather_w(w_out, "ag_w_out"))
    w_out_a, w_out_b = w_out_f[:CW], w_out_f[CW:]
    w_up_4 = gather_w(w_up, "ag_w_up")
    w_up_g = cols(w_up_4)[:, :F]
    w_up_v = cols(w_up_4)[:, F:]
    w_down_f = rows(gather_w(w_down, "ag_w_down"))
    w_pg_f = rows(gather_w(w_ple_gate, "ag_w_pg"))
    w_pp_f = cols(gather_w(w_ple_proj, "ag_w_pp"))

    conv_shapes = [conv_a_w[0].shape, conv_qkv_w[0].shape, conv_ffn_w[0].shape]
    cpack = _pack_rows([conv_a_w[0], conv_qkv_w[0], conv_ffn_w[0]])
    cg = _gather_chips(cpack.reshape(2, cpack.shape[0] // 2, LANES), "ag_conv").reshape(4, cpack.shape[0], LANES)
    parts = [_unpack_rows(cg[t], conv_shapes) for t in range(4)]
    cw_a = jnp.concatenate([parts[t][0] for t in range(4)], axis=1)
    cw_qkv = jnp.concatenate([parts[t][1] for t in range(4)], axis=1)
    cw_ffn = jnp.concatenate([parts[t][2] for t in range(4)], axis=1)
    cw_q, cw_k, cw_v = cw_qkv[:, :DNW], cw_qkv[:, DNW:2 * DNW], cw_qkv[:, 2 * DNW:]
    cw_fg, cw_fv = cw_ffn[:, :F], cw_ffn[:, F:]
    pad_row = lambda v: jnp.pad(v, ((0, 0), (0, LANES - v.shape[1])))
    a_log_row, dt_row = pad_row(a_log), pad_row(dt_bias)
    gdn_t = jnp.tile(dn_norm_g, (1, H))
    gfin = final_norm_g.reshape(1, D)

    h1 = _rms_fwd(xs, norm_mix_g, "rms1")
    proj = _mm(h1, w_in_main, mode="nn", out_dtypes=[F32], name="mm_proj")
    small = _mm(h1, w_in_small, mode="nn", out_dtypes=[F32], name="mm_small")
    ya = _ga_fwd(proj, cw_a, CW, cb)
    nq = 3 * CW // cb
    nd = DNW // cb
    qn = _qkv_fwd(proj, cw_q, nq, True, DNW, cb, "q_fwd")
    kn = _qkv_fwd(proj, cw_k, nq + nd, True, DNW, cb, "k_fwd")
    vs = _qkv_fwd(proj, cw_v, nq + 2 * nd, False, DNW, cb, "v_fwd")
    g, beta = _gb_fwd(small, a_log_row, dt_row, H)
    o, S0 = _delta_fwd(qn, kn, vs, g, beta)
    z_coff = (3 * CW + 3 * DNW) // DNW
    assert (3 * CW + 3 * DNW) % DNW == 0 and CW % DNW == 0
    yb = _gnorm_fwd(o, proj, z_coff, gdn_t, DNW)
    add = lambda acc, r: (r + acc,)
    x1 = _mm(ya, w_out_a, mode="nn", out_dtypes=[F32], epi=add, extras=[xs], name="mm_out_a")
    x1 = _mm(yb, w_out_b, mode="nn", out_dtypes=[F32], epi=add, extras=[x1], name="mm_out_b")
    h2 = _rms_fwd(x1, norm_ffn_g, "rms2")
    up_g = _mm(h2, w_up_g, mode="nn", out_dtypes=[F32], name="mm_up_g")
    up_v = _mm(h2, w_up_v, mode="nn", out_dtypes=[F32], name="mm_up_v")
    act = _ffn_fwd(up_g, up_v, cw_fg, cw_fv, cb)
    x2 = _mm(act, w_down_f, mode="nn", out_dtypes=[F32], epi=add, extras=[x1], name="mm_down")
    h3 = _rms_fwd(x2, norm_ple_g, "rms3")
    pp = _mm(ps, w_pp_f, mode="nn", out_dtypes=[F32], name="mm_pp")

    def ple_epi(acc, x2v, ppv):
        pg = _sigmoid(acc)
        return x2v + pg * ppv, pg

    x3, pg = _mm(h3, w_pg_f, mode="nn", out_dtypes=[F32, F32], epi=ple_epi, extras=[x2, pp], name="mm_pg")

    dx3, fin = _final_fb(x3, tgt, gfin)
    loss = lax.psum(jnp.sum(fin[1]), ("x", "y", "c"))
    d_gfin = fin[0:1]
    dpg, dpp = _ple_bwd(dx3, pp, pg)
    dW_pp = _mm(ps, dpp, mode="tn", out_dtypes=[F32], name="mm_dw_pp")
    dW_pg = _mm(h3, dpg, mode="tn", out_dtypes=[F32], name="mm_dw_pg")
    dh3 = _mm(dpg, w_pg_f, mode="nt", out_dtypes=[F32], name="mm_dh3")
    dx2, d_gple = _rms_bwd(dh3, x2, norm_ple_g, dx3, "rms3_bwd")
    dW_down = _mm(act, dx2, mode="tn", out_dtypes=[F32], name="mm_dw_down")
    dact = _mm(dx2, w_down_f, mode="nt", out_dtypes=[F32], name="mm_dact")
    dup_g, dup_v, dcw_fg, dcw_fv = _ffn_bwd(dact, up_g, up_v, cw_fg, cw_fv, cb)
    dW_up_g = _mm(h2, dup_g, mode="tn", out_dtypes=[F32], name="mm_dw_up_g")
    dW_up_v = _mm(h2, dup_v, mode="tn", out_dtypes=[F32], name="mm_dw_up_v")
    dh2 = _mm(dup_g, w_up_g, mode="nt", out_dtypes=[F32], name="mm_dh2_g")
    dh2 = _mm(dup_v, w_up_v, mode="nt", out_dtypes=[F32], epi=add, extras=[dh2], name="mm_dh2_v")
    dx1, d_gffn = _rms_bwd(dh2, x1, norm_ffn_g, dx2, "rms2_bwd")
    dW_out_a = _mm(ya, dx1, mode="tn", out_dtypes=[F32], name="mm_dw_out_a")
    dW_out_b = _mm(yb, dx1, mode="tn", out_dtypes=[F32], name="mm_dw_out_b")
    dymix = _mm(dx1, w_out_f, mode="nt", out_dtypes=[F32], name="mm_dymix")
    dax, dab, dac, dcw_a = _ga_bwd(dymix, proj, cw_a, CW, cb)
    do, dz, d_gdn = _gnorm_bwd(dymix, CW // DNW, o, proj, z_coff, gdn_t, DNW)
    dqn, dkn, dvs, dgB, dbB = _delta_bwd(qn, kn, vs, g, beta, S0, do)
    dq_pre, dcw_q = _qkv_bwd(dqn, proj, cw_q, nq, True, DNW, cb, "q_bwd")
    dk_pre, dcw_k = _qkv_bwd(dkn, proj, cw_k, nq + nd, True, DNW, cb, "k_bwd")
    dv_pre, dcw_v = _qkv_bwd(dvs, proj, cw_v, nq + 2 * nd, False, DNW, cb, "v_bwd")
    dsmall, d_ab = _gb_bwd(dgB, dbB, small, g, beta, a_log_row, dt_row, H)
    dproj = jnp.concatenate([dax, dab, dac, dq_pre, dk_pre, dv_pre, dz], axis=1)
    dW_in_main = _mm(h1, dproj, mode="tn", out_dtypes=[F32], name="mm_dw_in")
    dW_in_small = _mm(h1, dsmall, mode="tn", out_dtypes=[F32], name="mm_dw_in_small")
    dh1 = _mm(dproj, w_in_main, mode="nt", out_dtypes=[F32], name="mm_dh1")
    dh1 = _mm(dsmall, w_in_small, mode="nt", out_dtypes=[F32], epi=add, extras=[dh1], name="mm_dh1_small")
    dx, d_gmix = _rms_bwd(dh1, xs, norm_mix_g, dx1, "rms1_bwd")

    def split_cols(dW):
        R, C4 = dW.shape
        return jnp.transpose(dW.reshape(R, 4, C4 // 4), (1, 0, 2))

    def split_rows(dW):
        return dW.reshape(4, dW.shape[0] // 4, dW.shape[1])

    G_in = split_cols(jnp.concatenate([dW_in_main, dW_in_small[:, :2 * H]], axis=1))
    G_out = split_rows(jnp.concatenate([dW_out_a, dW_out_b], axis=0))
    G_up = split_cols(jnp.concatenate([dW_up_g, dW_up_v], axis=1))
    G_down = split_rows(dW_down)
    G_pg = split_rows(dW_pg)
    G_pp = split_cols(dW_pp)

    def update(G, w, m, v, name):
        gr = _reduce_to_shard(G, cidx, "rs_" + name)
        delta, m2, v2 = _adamw(w[0], gr, m[0], v[0], "adamw_" + name)
        return gr[None], delta[None], m2[None], v2[None]

    big = {
        "w_in": update(G_in, w_in, m_w_in, v_w_in, "w_in"),
        "w_out": update(G_out, w_out, m_w_out, v_w_out, "w_out"),
        "w_up": update(G_up, w_up, m_w_up, v_w_up, "w_up"),
        "w_down": update(G_down, w_down, m_w_down, v_w_down, "w_down"),
        "w_ple_gate": update(G_pg, w_ple_gate, m_w_ple_gate, v_w_ple_gate, "w_pg"),
        "w_ple_proj": update(G_pp, w_ple_proj, m_w_ple_proj, v_w_ple_proj, "w_pp"),
    }

    small_grads = [d_gmix[0:1], dcw_a[:cw_a.shape[0]], jnp.concatenate([dcw_q, dcw_k, dcw_v], axis=1)[:cw_qkv.shape[0]],
                   d_ab[0:1, :H], d_ab[1:2, :H], d_gdn[0:1], d_gffn[0:1],
                   jnp.concatenate([dcw_fg, dcw_fv], axis=1)[:cw_ffn.shape[0]], d_gple[0:1], d_gfin]
    small_shapes = [v.shape for v in small_grads]
    gpack = _pack_rows(small_grads)
    gsum = _sum_stack(_gather_all(gpack, "ag_small"), "sum_small")
    (g_gmix, g_cwa, g_cwqkv, g_alog, g_dt, g_gdn, g_gffn, g_cwffn, g_gple, g_gfin) = _unpack_rows(gsum, small_shapes)

    def my_cols(v):
        Cc = v.shape[1] // 4
        return lax.dynamic_slice_in_dim(v, chip * Cc, Cc, axis=1)

    g_small = [g_gmix, my_cols(g_cwa), my_cols(g_cwqkv), g_alog, g_dt, g_gdn, g_gffn, my_cols(g_cwffn), g_gple, g_gfin]
    w_small = [norm_mix_g, conv_a_w[0], conv_qkv_w[0], a_log, dt_bias, dn_norm_g, norm_ffn_g, conv_ffn_w[0], norm_ple_g, gfin]
    m_small = [m_norm_mix_g, m_conv_a_w[0], m_conv_qkv_w[0], m_a_log, m_dt_bias, m_dn_norm_g, m_norm_ffn_g, m_conv_ffn_w[0],
               m_norm_ple_g, m_final_norm_g.reshape(1, D)]
    v_small = [v_norm_mix_g, v_conv_a_w[0], v_conv_qkv_w[0], v_a_log, v_dt_bias, v_dn_norm_g, v_norm_ffn_g, v_conv_ffn_w[0],
               v_norm_ple_g, v_final_norm_g.reshape(1, D)]
    shp = [v.shape for v in w_small]
    ds_, ms_, vs_ = _adamw(_pack_rows(w_small), _pack_rows(g_small), _pack_rows(m_small), _pack_rows(v_small), "adamw_small")
    out_shapes = [norm_mix_g.shape, conv_a_w.shape, conv_qkv_w.shape, a_log.shape, dt_bias.shape, dn_norm_g.shape,
                  norm_ffn_g.shape, conv_ffn_w.shape, norm_ple_g.shape, final_norm_g.shape]
    rs = lambda vals: [v.reshape(s) for v, s in zip(vals, out_shapes)]
    sg, sd_, sm_, sv_ = rs(g_small), rs(_unpack_rows(ds_, shp)), rs(_unpack_rows(ms_, shp)), rs(_unpack_rows(vs_, shp))
    names_small = ["norm_mix_g", "conv_a_w", "conv_qkv_w", "a_log", "dt_bias", "dn_norm_g", "norm_ffn_g", "conv_ffn_w",
                   "norm_ple_g", "final_norm_g"]
    res = {n: (sg[i], sd_[i], sm_[i], sv_[i]) for i, n in enumerate(names_small)}
    res.update(big)
    order = ["norm_mix_g", "w_in", "conv_a_w", "conv_qkv_w", "a_log", "dt_bias", "dn_norm_g", "w_out", "norm_ffn_g", "w_up",
             "conv_ffn_w", "w_down", "norm_ple_g", "w_ple_gate", "w_ple_proj", "final_norm_g"]
    return (loss, dx[None], *[res[n][0] for n in order], *[res[n][1] for n in order], *[res[n][2] for n in order],
            *[res[n][3] for n in order])
```

```python
import functools

import jax
import jax.numpy as jnp
from jax import lax
from jax.experimental import pallas as pl
from jax.experimental.pallas import tpu as pltpu

F32 = jnp.float32
BF16 = jnp.bfloat16
LANES = 128
HALO = 8
HEAD_DIM = 128
CHUNK = 64
EPS = 1e-6
VMEM_LIMIT = 56 * 1024 * 1024
MM_VMEM_BUDGET = 36 * 1024 * 1024
MESH = pl.DeviceIdType.MESH

ADAM_LR, ADAM_B1, ADAM_B2, ADAM_EPS, ADAM_WD, ADAM_STEP = 0.001, 0.9, 0.999, 1e-08, 0.01, 10


def _tile(n, cap, unit):
    if n <= cap:
        return n
    d = (cap // unit) * unit
    while d >= unit:
        if n % d == 0:
            return d
        d -= unit
    raise ValueError(f"no tile for {n} (cap {cap}, unit {unit})")


def _sigmoid(x):
    return 1.0 / (1.0 + jnp.exp(-x))


def _mm_tiles(M, N, K, a_bytes, n_blocks_mn):
    tm, tk = _tile(M, 1024, LANES), _tile(K, 2048, LANES)
    for cap in (1024, 512, 256, 128):
        tn = _tile(N, cap, LANES)
        acc = 4 * tm * tn if K // tk > 1 else 0
        if 2 * tm * tk * a_bytes + 2 * tk * tn * 2 + 2 * 4 * tm * tn * n_blocks_mn + acc <= MM_VMEM_BUDGET:
            break
    return tm, tn, tk


def _mm(a, b, *, mode, out_dtypes, name, epi=None, extras=()):
    if mode == "nn":
        (M, K), N = a.shape, b.shape[1]
    elif mode == "nt":
        (M, K), N = a.shape, b.shape[0]
    else:
        (K, M), N = a.shape, b.shape[1]
    n_ex, n_out = len(extras), len(out_dtypes)
    tm, tn, tk = _mm_tiles(M, N, K, a.dtype.itemsize, n_ex + n_out)
    nk = K // tk
    a_spec = pl.BlockSpec((tk, tm), lambda i, j, k: (k, i)) if mode == "tn" else pl.BlockSpec((tm, tk), lambda i, j, k: (i, k))
    b_spec = pl.BlockSpec((tn, tk), lambda i, j, k: (j, k)) if mode == "nt" else pl.BlockSpec((tk, tn), lambda i, j, k: (k, j))
    mn_spec = pl.BlockSpec((tm, tn), lambda i, j, k: (i, j))
    dims = {"nn": (((1,), (0,)), ((), ())), "nt": (((1,), (1,)), ((), ())), "tn": (((0,), (0,)), ((), ()))}[mode]

    def body(*refs):
        a_ref, b_ref = refs[0], refs[1]
        ex_refs = refs[2:2 + n_ex]
        out_refs = refs[2 + n_ex:2 + n_ex + n_out]
        part = lax.dot_general(a_ref[...].astype(BF16), b_ref[...].astype(BF16), dims, preferred_element_type=F32)

        def finish(acc):
            outs = (acc,) if epi is None else epi(acc, *[r[...] for r in ex_refs])
            for r, o in zip(out_refs, outs):
                r[...] = o.astype(r.dtype)

        if nk == 1:
            finish(part)
            return
        acc_ref = refs[-1]
        k = pl.program_id(2)

        @pl.when(k == 0)
        def _():
            acc_ref[...] = part

        @pl.when(jnp.logical_and(k > 0, k < nk - 1))
        def _():
            acc_ref[...] += part

        @pl.when(k == nk - 1)
        def _():
            finish(acc_ref[...] + part)

    outs = pl.pallas_call(
        body, name=name, grid=(M // tm, N // tn, nk),
        in_specs=[a_spec, b_spec] + [mn_spec] * n_ex,
        out_specs=[mn_spec] * n_out,
        out_shape=[jax.ShapeDtypeStruct((M, N), dt) for dt in out_dtypes],
        scratch_shapes=[pltpu.VMEM((tm, tn), F32)] if nk > 1 else [],
        compiler_params=pltpu.CompilerParams(dimension_semantics=("parallel", "parallel", "arbitrary"),
                                             vmem_limit_bytes=VMEM_LIMIT),
    )(a, b, *extras)
    return outs[0] if n_out == 1 else outs


def _tiled(fn, *, T, C, ins, out_dtypes=(), acc_rows=(), tb=256, cb=512, name):
    tb = _tile(T, tb, HALO)
    nI, nJ = T // tb, C // cb
    hb, nH = tb // HALO, T // HALO
    specs, args, kinds = [], [], []
    for kind, arr, cmap in ins:
        cm = cmap if cmap is not None else (lambda j: j)
        kinds.append(kind)
        if kind == "cur":
            specs.append(pl.BlockSpec((tb, cb), lambda j, i, cm=cm: (i, cm(j))))
            args.append(arr)
        elif kind == "ext":
            specs.append(pl.BlockSpec((HALO, cb), lambda j, i, cm=cm: (jnp.maximum(i * hb - 1, 0), cm(j))))
            specs.append(pl.BlockSpec((tb, cb), lambda j, i, cm=cm: (i, cm(j))))
            specs.append(pl.BlockSpec((HALO, cb), lambda j, i, cm=cm: (jnp.minimum((i + 1) * hb, nH - 1), cm(j))))
            args += [arr, arr, arr]
        elif kind == "row":
            specs.append(pl.BlockSpec((arr.shape[0], cb), lambda j, i, cm=cm: (0, cm(j))))
            args.append(arr)
        elif kind == "stack":
            specs.append(pl.BlockSpec((arr.shape[0], tb, cb), lambda j, i, cm=cm: (0, i, cm(j))))
            args.append(arr)
        else:
            raise ValueError(kind)
    n_in = len(args)
    n_out, n_acc = len(out_dtypes), len(acc_rows)

    def body(*refs):
        j, i = pl.program_id(0), pl.program_id(1)
        vals, r = [], 0
        for kind in kinds:
            if kind == "ext":
                prev = jnp.where(i == 0, 0.0, refs[r][...].astype(F32))
                cur = refs[r + 1][...].astype(F32)
                nxt = jnp.where(i == nI - 1, 0.0, refs[r + 2][...].astype(F32))
                vals.append(jnp.concatenate([prev, cur, nxt], axis=0))
                r += 3
            else:
                vals.append(refs[r][...])
                r += 1
        res = fn(j, i, *vals)
        for ref, o in zip(refs[n_in:n_in + n_out], res[:n_out]):
            ref[...] = o.astype(ref.dtype)
        for ref, o in zip(refs[n_in + n_out:], res[n_out:]):
            @pl.when(i == 0)
            def _(ref=ref, o=o):
                ref[...] = o

            @pl.when(i > 0)
            def _(ref=ref, o=o):
                ref[...] += o

    outs = pl.pallas_call(
        body, name=name, grid=(nJ, nI), in_specs=specs,
        out_specs=[pl.BlockSpec((tb, cb), lambda j, i: (i, j))] * n_out
        + [pl.BlockSpec((rows, cb), lambda j, i: (0, j)) for rows in acc_rows],
        out_shape=[jax.ShapeDtypeStruct((T, C), dt) for dt in out_dtypes]
        + [jax.ShapeDtypeStruct((rows, C), F32) for rows in acc_rows],
        compiler_params=pltpu.CompilerParams(dimension_semantics=("parallel", "arbitrary"),
                                             vmem_limit_bytes=VMEM_LIMIT),
    )(*args)
    return outs


def _conv_causal(xe, w):
    K = w.shape[0]
    y = xe * w[K - 1:K]
    for j in range(K - 1):
        y = y + pltpu.roll(xe, K - 1 - j, 0) * w[j:j + 1]
    return y


def _conv_anti(de, w):
    K, n = w.shape[0], de.shape[0]
    y = de * w[K - 1:K]
    for j in range(K - 1):
        y = y + pltpu.roll(de, n - (K - 1 - j), 0) * w[j:j + 1]
    return y


def _conv_dw(dce, xe, K):
    n = dce.shape[0]
    tb = n - 2 * HALO
    rows = []
    for j in range(K):
        xs = xe if j == K - 1 else pltpu.roll(xe, K - 1 - j, 0)
        rows.append(jnp.sum((dce * xs)[HALO:HALO + tb], axis=0, keepdims=True))
    rows.append(jnp.zeros((HALO - K, dce.shape[1]), F32))
    return jnp.concatenate(rows, axis=0)


def _own(xe):
    return xe[HALO:xe.shape[0] - HALO]


def _row0(v):
    return jnp.concatenate([v, jnp.zeros((HALO - 1, v.shape[1]), F32)], axis=0)


def _per_head(fn, *xs):
    n = xs[0].shape[1] // HEAD_DIM
    outs = [fn(*[x[:, g * HEAD_DIM:(g + 1) * HEAD_DIM] for x in xs]) for g in range(n)]
    return outs[0] if n == 1 else jnp.concatenate(outs, axis=1)


def _rms_fwd(x, g, name):
    T, D = x.shape

    def fn(j, i, xv, gv):
        r = lax.rsqrt(jnp.mean(xv * xv, axis=1, keepdims=True) + EPS)
        return (xv * r * gv,)

    return _tiled(fn, T=T, C=D, ins=[("cur", x, None), ("row", g, None)], out_dtypes=[BF16], cb=D, name=name)[0]


def _rms_bwd_math(dy, xv, gv):
    r = lax.rsqrt(jnp.mean(xv * xv, axis=1, keepdims=True) + EPS)
    xh = xv * r
    dxh = dy * gv
    dx = r * (dxh - xh * jnp.mean(dxh * xh, axis=1, keepdims=True))
    dg = jnp.sum(dy * xh, axis=0, keepdims=True)
    return dx, dg


def _rms_bwd(dh, x, g, dres, name):
    T, D = x.shape

    def fn(j, i, dhv, xv, gv, dr):
        dx, dg = _rms_bwd_math(dhv, xv, gv)
        return dr + dx, dr + dx, _row0(dg)

    return _tiled(fn, T=T, C=D, ins=[("cur", dh, None), ("cur", x, None), ("row", g, None), ("cur", dres, None)],
                  out_dtypes=[F32, BF16], acc_rows=[HALO], cb=D, name=name)


def _final_fb(x3, tgt, g):
    T, D = x3.shape

    def fn(j, i, xv, tv, gv):
        r = lax.rsqrt(jnp.mean(xv * xv, axis=1, keepdims=True) + EPS)
        xh = xv * r
        e = xh * gv - tv
        dy = e * (1.0 / D)
        dxh = dy * gv
        dx = r * (dxh - xh * jnp.mean(dxh * xh, axis=1, keepdims=True))
        dg = jnp.sum(dy * xh, axis=0, keepdims=True)
        ls = jnp.sum(e * e, axis=0, keepdims=True) * (0.5 / D)
        return dx, jnp.concatenate([dg, ls, jnp.zeros((HALO - 2, D), F32)], axis=0)

    return _tiled(fn, T=T, C=D, ins=[("cur", x3, None), ("cur", tgt, None), ("row", g, None)],
                  out_dtypes=[F32], acc_rows=[HALO], cb=D, name="final_fb")


def _ga_fwd(proj, w_a, CW, cb):
    T = proj.shape[0]
    n = CW // cb

    def fn(j, i, ax, ab, ac, w):
        c = _conv_causal(ac * ax, w)
        return (ab * _own(c),)

    return _tiled(fn, T=T, C=CW, ins=[("ext", proj, None), ("cur", proj, lambda j: j + n), ("ext", proj, lambda j: j + 2 * n),
                                       ("row", w_a, None)], out_dtypes=[BF16], cb=cb, name="ga_fwd")[0]


def _ga_bwd(dymix, proj, w_a, CW, cb):
    T = proj.shape[0]
    n = CW // cb
    K = w_a.shape[0]

    def fn(j, i, dy, ax, ab, ac, w):
        u = ac * ax
        c = _conv_causal(u, w)
        dc = dy * ab
        du = _conv_anti(dc, w)
        return _own(du * ac), _own(dy * c), _own(du * ax), _conv_dw(dc, u, K)

    return _tiled(fn, T=T, C=CW, ins=[("ext", dymix, None), ("ext", proj, None), ("ext", proj, lambda j: j + n),
                                       ("ext", proj, lambda j: j + 2 * n), ("row", w_a, None)],
                  out_dtypes=[BF16, BF16, BF16], acc_rows=[HALO], cb=cb, name="ga_bwd")


def _l2n(s):
    return s * lax.rsqrt(jnp.sum(s * s, axis=1, keepdims=True) + EPS)


def _qkv_fwd(proj, w_sec, coff, normalize, DNW, cb, name):
    T = proj.shape[0]

    def fn(j, i, pre, w):
        c = _own(_conv_causal(pre, w))
        s = c * _sigmoid(c)
        return (_per_head(_l2n, s) if normalize else s,)

    return _tiled(fn, T=T, C=DNW, ins=[("ext", proj, lambda j: j + coff), ("row", w_sec, None)],
                  out_dtypes=[F32], cb=cb, name=name)[0]


def _qkv_bwd(dsec, proj, w_sec, coff, normalize, DNW, cb, name):
    T = proj.shape[0]
    K = w_sec.shape[0]

    def l2n_bwd(s, dn):
        r = lax.rsqrt(jnp.sum(s * s, axis=1, keepdims=True) + EPS)
        nrm = s * r
        return r * (dn - nrm * jnp.sum(dn * nrm, axis=1, keepdims=True))

    def fn(j, i, dn, pre, w):
        c = _conv_causal(pre, w)
        sg = _sigmoid(c)
        s = c * sg
        ds = _per_head(l2n_bwd, s, dn) if normalize else dn
        dc = ds * (sg * (1.0 + c * (1.0 - sg)))
        return _own(_conv_anti(dc, w)), _conv_dw(dc, pre, K)

    return _tiled(fn, T=T, C=DNW, ins=[("ext", dsec, None), ("ext", proj, lambda j: j + coff), ("row", w_sec, None)],
                  out_dtypes=[BF16], acc_rows=[HALO], cb=cb, name=name)


def _gb_fwd(small, a_log_row, dt_row, H):
    T = small.shape[0]

    def fn(j, i, sm, al, dt):
        z = sm + dt
        sp = jnp.maximum(z, 0.0) + jnp.log(1.0 + jnp.exp(-jnp.abs(z)))
        g = -jnp.exp(al) * sp
        beta = _sigmoid(pltpu.roll(sm, LANES - H, 1))
        return g, beta

    return _tiled(fn, T=T, C=LANES, ins=[("cur", small, None), ("row", a_log_row, None), ("row", dt_row, None)],
                  out_dtypes=[F32, F32], cb=LANES, name="gb_fwd")


def _gb_bwd(dgB, dbB, small, g, beta, a_log_row, dt_row, H):
    T = small.shape[0]

    def fn(j, i, dgv, dbv, sm, gv, bv, al, dt):
        lane = lax.broadcasted_iota(jnp.int32, sm.shape, 1)
        dg = jnp.zeros(sm.shape, F32)
        db = jnp.zeros(sm.shape, F32)
        for h in range(H):
            dg = jnp.where(lane == h, jnp.sum(dgv[h], axis=1, keepdims=True), dg)
            db = jnp.where(lane == h, jnp.sum(dbv[h], axis=1, keepdims=True), db)
        da = dg * (-jnp.exp(al)) * _sigmoid(sm + dt)
        dbb = db * bv * (1.0 - bv)
        dsm = jnp.where(lane < H, da, 0.0) + pltpu.roll(jnp.where(lane < H, dbb, 0.0), H, 1)
        d_alog = jnp.sum(jnp.where(lane < H, dg * gv, 0.0), axis=0, keepdims=True)
        d_dt = jnp.sum(jnp.where(lane < H, da, 0.0), axis=0, keepdims=True)
        return dsm, jnp.concatenate([d_alog, d_dt, jnp.zeros((HALO - 2, LANES), F32)], axis=0)

    return _tiled(fn, T=T, C=LANES, ins=[("stack", dgB, None), ("stack", dbB, None), ("cur", small, None), ("cur", g, None),
                                          ("cur", beta, None), ("row", a_log_row, None), ("row", dt_row, None)],
                  out_dtypes=[BF16], acc_rows=[HALO], cb=LANES, name="gb_bwd")


_DIMS = {"nn": (((1,), (0,)), ((), ())), "nt": (((1,), (1,)), ((), ())), "tn": (((0,), (0,)), ((), ()))}
_DOT_BWD = {"nn": (("nt", "gb"), ("tn", "ag")), "nt": (("nn", "gb"), ("tn", "ga")), "tn": (("nt", "bg"), ("nn", "ag"))}


def _split(a):
    hi = a.astype(BF16)
    return hi, (a - hi.astype(F32)).astype(BF16)


def _raw_dot(a, b, kind, passes):
    dg = lambda x, y: lax.dot_general(x, y, _DIMS[kind], preferred_element_type=F32)
    if passes == 1:
        return dg(a.astype(BF16), b.astype(BF16))
    ah, al = _split(a)
    bh, bl = _split(b)
    return dg(ah, bh) + (dg(ah, bl) + dg(al, bh))


@functools.lru_cache(maxsize=None)
def _dotf(kind, passes):
    @jax.custom_vjp
    def f(a, b):
        return _raw_dot(a, b, kind, passes)

    def fwd(a, b):
        return _raw_dot(a, b, kind, passes), (a, b)

    def bwd(res, g):
        ops = {"a": res[0], "b": res[1], "g": g}
        (ka, oa), (kb, ob) = _DOT_BWD[kind]
        return (_raw_dot(ops[oa[0]], ops[oa[1]], ka, passes), _raw_dot(ops[ob[0]], ops[ob[1]], kb, passes))

    f.defvjp(fwd, bwd)
    return f


def _chunk_fn(q, k, v, gB, bB, S):
    C = CHUNK
    d3, d3nt = _dotf("nn", 3), _dotf("nt", 3)
    d1, d1nt, d1tn = _dotf("nn", 1), _dotf("nt", 1), _dotf("tn", 1)
    each = lambda f, *ls: tuple(f(*xs) for xs in zip(*ls))
    row = lax.broadcasted_iota(jnp.int32, (C, C), 0)
    col = lax.broadcasted_iota(jnp.int32, (C, C), 1)
    causal = row >= col
    strict = row > col
    tril = jnp.where(causal, 1.0, 0.0).astype(F32)
    eye = jnp.where(row == col, 1.0, 0.0).astype(F32)
    avg = jnp.full((C, HEAD_DIM), 1.0 / HEAD_DIM, F32)
    gc = each(lambda g: d3(tril, g), gB)
    R = each(lambda g: d3nt(avg, g), gc)
    decay = each(lambda g, r: jnp.where(causal, jnp.exp(jnp.where(causal, g[:, :C] - r, 0.0)), 0.0), gc, R)
    kk = each(lambda x: d1nt(x, x), k)
    L = each(lambda a, d, b: jnp.where(strict, a * d * b[:, :C], 0.0), kk, decay, bB)
    inv = each(lambda l: eye - l, L)
    P = L
    for _ in range(5):
        P = each(lambda p: d3(p, p), P)
        inv = each(lambda a, p: d3(a, eye + p), inv, P)
    eg = each(jnp.exp, gc)
    u = each(lambda a, x, b: d3(a, x * b), inv, v, bB)
    w = each(lambda a, x, b, e: d3(a, x * b * e), inv, k, bB, eg)
    qs = each(lambda x: x * (HEAD_DIM ** -0.5), q)
    qk = each(lambda a, x, d: d1nt(a, x) * d, qs, k, decay)
    gl = each(lambda g: g[C - 1:C, :], gc)
    v_new = each(lambda a, b, s: a - d1(b, s), u, w, S)
    o1 = each(lambda a, e, s: d1(a * e, s), qs, eg, S)
    o = each(lambda a, b, c: a + d1(b, c), o1, qk, v_new)
    kv = each(lambda x, a, g, vn: d1tn(x * jnp.exp(a - g), vn), k, gl, gc, v_new)
    S_new = each(lambda s, a, b: s * jnp.exp(a) + b, S, gl, kv)
    return o, S_new


def _sel_lane(x, h):
    lane = lax.broadcasted_iota(jnp.int32, x.shape, 1)
    return jnp.broadcast_to(jnp.sum(jnp.where(lane == h, x, 0.0), axis=1, keepdims=True), x.shape)


def _head(ref, h):
    return ref[:, h * HEAD_DIM:(h + 1) * HEAD_DIM]


def _delta_fwd(q, k, v, g, beta):
    T = q.shape[0]
    H, N = q.shape[1] // HEAD_DIM, T // CHUNK

    def body(q_ref, k_ref, v_ref, g_ref, b_ref, o_ref, s_ref, S):
        @pl.when(pl.program_id(0) == 0)
        def _():
            S[...] = jnp.zeros_like(S)

        gv, bv = g_ref[...], b_ref[...]
        heads = lambda f: tuple(f(h) for h in range(H))
        S_in = heads(lambda h: S[h])
        for h in range(H):
            s_ref[h, 0] = S_in[h]
        o, S_new = _chunk_fn(heads(lambda h: _head(q_ref, h)), heads(lambda h: _head(k_ref, h)), heads(lambda h: _head(v_ref, h)),
                             heads(lambda h: _sel_lane(gv, h)), heads(lambda h: _sel_lane(bv, h)), S_in)
        for h in range(H):
            o_ref[:, h * HEAD_DIM:(h + 1) * HEAD_DIM] = o[h]
            S[h] = S_new[h]

    blk = pl.BlockSpec((CHUNK, H * HEAD_DIM), lambda n: (n, 0))
    gblk = pl.BlockSpec((CHUNK, LANES), lambda n: (n, 0))
    return pl.pallas_call(
        body, name="delta_fwd", grid=(N,), in_specs=[blk, blk, blk, gblk, gblk],
        out_specs=[blk, pl.BlockSpec((H, 1, HEAD_DIM, HEAD_DIM), lambda n: (0, n, 0, 0))],
        out_shape=[jax.ShapeDtypeStruct((T, H * HEAD_DIM), F32), jax.ShapeDtypeStruct((H, N, HEAD_DIM, HEAD_DIM), F32)],
        scratch_shapes=[pltpu.VMEM((H, HEAD_DIM, HEAD_DIM), F32)],
        compiler_params=pltpu.CompilerParams(dimension_semantics=("arbitrary",)),
    )(q, k, v, g, beta)


def _delta_bwd(q, k, v, g, beta, S0, do):
    T = q.shape[0]
    H, N = q.shape[1] // HEAD_DIM, T // CHUNK

    def body(q_ref, k_ref, v_ref, g_ref, b_ref, s_ref, do_ref, dq_ref, dk_ref, dv_ref, dg_ref, db_ref, dS):
        @pl.when(pl.program_id(0) == 0)
        def _():
            dS[...] = jnp.zeros_like(dS)

        gv, bv = g_ref[...], b_ref[...]
        heads = lambda f: tuple(f(h) for h in range(H))
        _, vjp = jax.vjp(_chunk_fn, heads(lambda h: _head(q_ref, h)), heads(lambda h: _head(k_ref, h)),
                         heads(lambda h: _head(v_ref, h)), heads(lambda h: _sel_lane(gv, h)), heads(lambda h: _sel_lane(bv, h)),
                         heads(lambda h: s_ref[h, 0]))
        dq, dk, dv, dgB, dbB, dS_prev = vjp((heads(lambda h: _head(do_ref, h)), heads(lambda h: dS[h])))
        for h in range(H):
            sl = slice(h * HEAD_DIM, (h + 1) * HEAD_DIM)
            dq_ref[:, sl] = dq[h]
            dk_ref[:, sl] = dk[h]
            dv_ref[:, sl] = dv[h]
            dg_ref[h] = dgB[h]
            db_ref[h] = dbB[h]
            dS[h] = dS_prev[h]

    blk = pl.BlockSpec((CHUNK, H * HEAD_DIM), lambda n: (N - 1 - n, 0))
    gblk = pl.BlockSpec((CHUNK, LANES), lambda n: (N - 1 - n, 0))
    hblk = pl.BlockSpec((H, CHUNK, LANES), lambda n: (0, N - 1 - n, 0))
    sd = jax.ShapeDtypeStruct
    return pl.pallas_call(
        body, name="delta_bwd", grid=(N,),
        in_specs=[blk, blk, blk, gblk, gblk, pl.BlockSpec((H, 1, HEAD_DIM, HEAD_DIM), lambda n: (0, N - 1 - n, 0, 0)), blk],
        out_specs=[blk, blk, blk, hblk, hblk],
        out_shape=[sd((T, H * HEAD_DIM), F32)] * 3 + [sd((H, T, LANES), F32)] * 2,
        scratch_shapes=[pltpu.VMEM((H, HEAD_DIM, HEAD_DIM), F32)],
        compiler_params=pltpu.CompilerParams(dimension_semantics=("arbitrary",)),
    )(q, k, v, g, beta, S0, do)


def _gnorm_fwd(o, proj, z_coff, gdn_t, DNW):
    T = o.shape[0]

    def fn(j, i, ov, zv, gv):
        def one(oh, zh, gh):
            r = lax.rsqrt(jnp.mean(oh * oh, axis=1, keepdims=True) + EPS)
            return oh * r * gh * (zh * _sigmoid(zh))
        return (_per_head(one, ov, zv, jnp.broadcast_to(gv, ov.shape)),)

    return _tiled(fn, T=T, C=DNW, ins=[("cur", o, None), ("cur", proj, lambda j: j + z_coff), ("row", gdn_t, None)],
                  out_dtypes=[BF16], cb=DNW, name="gnorm_fwd")[0]


def _gnorm_bwd(dymix, y_coff, o, proj, z_coff, gdn_t, DNW):
    T = o.shape[0]
    nh = DNW // HEAD_DIM

    def fn(j, i, dy, ov, zv, gv):
        dos, dzs, dgs = [], [], jnp.zeros((1, HEAD_DIM), F32)
        for h in range(nh):
            sl = slice(h * HEAD_DIM, (h + 1) * HEAD_DIM)
            dyh, oh, zh, gh = dy[:, sl].astype(F32), ov[:, sl], zv[:, sl], gv[:, sl]
            r = lax.rsqrt(jnp.mean(oh * oh, axis=1, keepdims=True) + EPS)
            on = oh * r
            sg = _sigmoid(zh)
            sz = zh * sg
            dzs.append(dyh * on * gh * (sg * (1.0 + zh * (1.0 - sg))))
            don = dyh * gh * sz
            dos.append(r * (don - on * jnp.mean(don * on, axis=1, keepdims=True)))
            dgs = dgs + jnp.sum(dyh * on * sz, axis=0, keepdims=True)
        cat = (lambda xs: xs[0] if nh == 1 else jnp.concatenate(xs, axis=1))
        return cat(dos), cat(dzs), _row0(dgs)

    T_ = T
    nI = T_ // _tile(T_, 256, HALO)
    tb = T_ // nI
    specs_cb = DNW

    def body_wrap():
        def body(dy_ref, o_ref, z_ref, g_ref, do_ref, dz_ref, dg_ref):
            i = pl.program_id(0)
            d_o, d_z, d_g = fn(0, i, dy_ref[...], o_ref[...], z_ref[...], g_ref[...])
            do_ref[...] = d_o
            dz_ref[...] = d_z.astype(dz_ref.dtype)

            @pl.when(i == 0)
            def _():
                dg_ref[...] = d_g

            @pl.when(i > 0)
            def _():
                dg_ref[...] += d_g

        return pl.pallas_call(
            body, name="gnorm_bwd", grid=(nI,),
            in_specs=[pl.BlockSpec((tb, specs_cb), lambda i: (i, y_coff)), pl.BlockSpec((tb, specs_cb), lambda i: (i, 0)),
                      pl.BlockSpec((tb, specs_cb), lambda i: (i, z_coff)), pl.BlockSpec((1, specs_cb), lambda i: (0, 0))],
            out_specs=[pl.BlockSpec((tb, specs_cb), lambda i: (i, 0)), pl.BlockSpec((tb, specs_cb), lambda i: (i, 0)),
                       pl.BlockSpec((HALO, HEAD_DIM), lambda i: (0, 0))],
            out_shape=[jax.ShapeDtypeStruct((T_, DNW), F32), jax.ShapeDtypeStruct((T_, DNW), BF16),
                       jax.ShapeDtypeStruct((HALO, HEAD_DIM), F32)],
            compiler_params=pltpu.CompilerParams(dimension_semantics=("arbitrary",), vmem_limit_bytes=VMEM_LIMIT),
        )(dymix, o, proj, gdn_t)

    return body_wrap()


def _ffn_fwd(up_g, up_v, w_g, w_v, cb):
    T, F = up_g.shape

    def fn(j, i, ug, uv, wg, wv):
        cg = _own(_conv_causal(ug, wg))
        cv = _own(_conv_causal(uv, wv))
        return (cg * _sigmoid(cg) * cv,)

    return _tiled(fn, T=T, C=F, ins=[("ext", up_g, None), ("ext", up_v, None), ("row", w_g, None), ("row", w_v, None)],
                  out_dtypes=[BF16], cb=cb, name="ffn_fwd")[0]


def _ffn_bwd(dact, up_g, up_v, w_g, w_v, cb):
    T, F = up_g.shape
    K = w_g.shape[0]

    def fn(j, i, da, ug, uv, wg, wv):
        cg = _conv_causal(ug, wg)
        cv = _conv_causal(uv, wv)
        sg = _sigmoid(cg)
        dgate = da * cv * (sg * (1.0 + cg * (1.0 - sg)))
        dval = da * (cg * sg)
        return (_own(_conv_anti(dgate, wg)), _own(_conv_anti(dval, wv)), _conv_dw(dgate, ug, K), _conv_dw(dval, uv, K))

    return _tiled(fn, T=T, C=F, ins=[("ext", dact, None), ("ext", up_g, None), ("ext", up_v, None), ("row", w_g, None),
                                      ("row", w_v, None)], out_dtypes=[BF16, BF16], acc_rows=[HALO, HALO], cb=cb, name="ffn_bwd")


def _ple_bwd(dx3, pp, pg):
    T, D = dx3.shape

    def fn(j, i, d, ppv, pgv):
        return d * ppv * pgv * (1.0 - pgv), d * pgv

    return _tiled(fn, T=T, C=D, ins=[("cur", dx3, None), ("cur", pp, None), ("cur", pg, None)],
                  out_dtypes=[BF16, BF16], cb=_tile(D, 512, LANES), name="ple_bwd")


def _adamw(w, g, m, v, name):
    R, Cc = w.shape
    cb = _tile(Cc, 512, LANES) if Cc % LANES == 0 else Cc
    c1 = 1.0 / (1.0 - ADAM_B1 ** ADAM_STEP)
    c2 = 1.0 / (1.0 - ADAM_B2 ** ADAM_STEP)

    def fn(j, i, wv, gv, mv, vv):
        m2 = ADAM_B1 * mv + (1.0 - ADAM_B1) * gv
        v2 = ADAM_B2 * vv + (1.0 - ADAM_B2) * (gv * gv)
        delta = -ADAM_LR * ((m2 * c1) / (jnp.sqrt(v2 * c2) + ADAM_EPS) + ADAM_WD * wv)
        return delta, m2, v2

    return _tiled(fn, T=R, C=Cc, ins=[("cur", w, None), ("cur", g, None), ("cur", m, None), ("cur", v, None)],
                  out_dtypes=[F32, F32, F32], cb=cb, name=name)


def _sum_stack(st, name):
    S, R, Cc = st.shape
    cb = _tile(Cc, 512, LANES) if Cc % LANES == 0 else Cc

    def fn(j, i, sv):
        t = sv[0]
        for s in range(1, S):
            t = t + sv[s]
        return (t,)

    return _tiled(fn, T=R, C=Cc, ins=[("stack", st, None)], out_dtypes=[F32], cb=cb, name=name)[0]


ANY = pl.BlockSpec(memory_space=pl.ANY)


def _place():
    x, y, c = lax.axis_index("x"), lax.axis_index("y"), lax.axis_index("c")
    return x, y, c, 2 * x + y


def _chip_dev(s, c):
    return (s // 2, s % 2, c)


def _gather_chips(shard, name):
    two, R2, Cc = shard.shape

    def body(sh_ref, out_ref, send1, recv1, send2, recv2):
        x, y, c, s = _place()
        first = []
        for m in range(1, 4):
            t = s ^ m
            cp = pltpu.make_async_remote_copy(sh_ref.at[c], out_ref.at[s, c], send1.at[m - 1], recv1.at[m - 1],
                                              device_id=_chip_dev(t, c), device_id_type=MESH)
            cp.start()
            first.append(cp)
        passed = []
        for m in range(1, 4):
            t = s ^ m
            pltpu.make_async_remote_copy(sh_ref.at[c], out_ref.at[t, c], send1.at[m - 1], recv1.at[m - 1],
                                         device_id=_chip_dev(t, c), device_id_type=MESH).wait_recv()
            cp = pltpu.make_async_remote_copy(out_ref.at[t, c], out_ref.at[t, c], send2.at[m - 1], recv2.at[m - 1],
                                              device_id=(x, y, 1 - c), device_id_type=MESH)
            cp.start()
            passed.append(cp)
        for m in range(1, 4):
            t = s ^ m
            pltpu.make_async_remote_copy(out_ref.at[t, 1 - c], out_ref.at[t, 1 - c], send2.at[m - 1], recv2.at[m - 1],
                                         device_id=(x, y, 1 - c), device_id_type=MESH).wait_recv()
        for cp in first + passed:
            cp.wait_send()

    return pl.pallas_call(
        body, name=name, in_specs=[ANY], out_specs=ANY,
        out_shape=jax.ShapeDtypeStruct((4, two, R2, Cc), shard.dtype),
        scratch_shapes=[pltpu.SemaphoreType.DMA((3,)), pltpu.SemaphoreType.DMA((3,)), pltpu.SemaphoreType.DMA((3,)),
                        pltpu.SemaphoreType.DMA((3,))],
    )(shard)


def _sibling_swap(G, name):
    S4, two, R2, Cc = G.shape

    def body(g_ref, out_ref, send, recv):
        x, y, c, s = _place()
        cps = []
        for t in range(S4):
            cp = pltpu.make_async_remote_copy(g_ref.at[t, 1 - c], out_ref.at[t], send.at[t], recv.at[t],
                                              device_id=(x, y, 1 - c), device_id_type=MESH)
            cp.start()
            cps.append(cp)
        for cp in cps:
            cp.wait()

    return pl.pallas_call(
        body, name=name, in_specs=[ANY], out_specs=ANY, out_shape=jax.ShapeDtypeStruct((S4, R2, Cc), G.dtype),
        scratch_shapes=[pltpu.SemaphoreType.DMA((S4,)), pltpu.SemaphoreType.DMA((S4,))],
    )(G)


def _add_half(G, A, cidx, name):
    S4, two, R2, Cc = G.shape
    tb = _tile(R2, 256, HALO)
    cb = _tile(Cc, 512, LANES) if Cc % LANES == 0 else Cc
    nI, nJ = R2 // tb, Cc // cb

    def body(c_ref, g_ref, a_ref, o_ref):
        o_ref[...] = g_ref[0, 0] + a_ref[0]

    grid_spec = pltpu.PrefetchScalarGridSpec(
        num_scalar_prefetch=1, grid=(S4, nI, nJ),
        in_specs=[pl.BlockSpec((1, 1, tb, cb), lambda t, i, j, c_ref: (t, c_ref[0], i, j)),
                  pl.BlockSpec((1, tb, cb), lambda t, i, j, c_ref: (t, i, j))],
        out_specs=pl.BlockSpec((tb, cb), lambda t, i, j, c_ref: (t * nI + i, j)))
    return pl.pallas_call(body, name=name, grid_spec=grid_spec, out_shape=jax.ShapeDtypeStruct((S4 * R2, Cc), F32),
                          compiler_params=pltpu.CompilerParams(dimension_semantics=("parallel", "parallel", "parallel")),
                          )(cidx, G, A)


def _chip_all_to_all(S1, name):
    S4, R2, Cc = S1.shape

    def body(s_ref, out_ref, send, recv):
        x, y, c, s = _place()
        cps = []
        for m in range(1, 4):
            t = s ^ m
            cp = pltpu.make_async_remote_copy(s_ref.at[t], out_ref.at[m - 1], send.at[m - 1], recv.at[m - 1],
                                              device_id=_chip_dev(t, c), device_id_type=MESH)
            cp.start()
            cps.append(cp)
        for cp in cps:
            cp.wait()

    return pl.pallas_call(
        body, name=name, in_specs=[ANY], out_specs=ANY, out_shape=jax.ShapeDtypeStruct((3, R2, Cc), S1.dtype),
        scratch_shapes=[pltpu.SemaphoreType.DMA((3,)), pltpu.SemaphoreType.DMA((3,))],
    )(S1)


def _add_own(S1, B, chip_idx, name):
    S4, R2, Cc = S1.shape
    tb = _tile(R2, 256, HALO)
    cb = _tile(Cc, 512, LANES) if Cc % LANES == 0 else Cc

    def body(s_idx, s_ref, b_ref, o_ref):
        o_ref[...] = ((s_ref[0] + b_ref[0]) + b_ref[1]) + b_ref[2]

    grid_spec = pltpu.PrefetchScalarGridSpec(
        num_scalar_prefetch=1, grid=(R2 // tb, Cc // cb),
        in_specs=[pl.BlockSpec((1, tb, cb), lambda i, j, s_idx: (s_idx[0], i, j)),
                  pl.BlockSpec((3, tb, cb), lambda i, j, s_idx: (0, i, j))],
        out_specs=pl.BlockSpec((tb, cb), lambda i, j, s_idx: (i, j)))
    return pl.pallas_call(body, name=name, grid_spec=grid_spec, out_shape=jax.ShapeDtypeStruct((R2, Cc), F32),
                          compiler_params=pltpu.CompilerParams(dimension_semantics=("parallel", "parallel")))(chip_idx, S1, B)


def _sibling_send(Hs, name):
    R2, Cc = Hs.shape

    def body(h_ref, out_ref, send, recv):
        x, y, c, s = _place()
        cp = pltpu.make_async_remote_copy(h_ref, out_ref, send, recv, device_id=(x, y, 1 - c), device_id_type=MESH)
        cp.start()
        cp.wait()

    return pl.pallas_call(
        body, name=name, in_specs=[ANY], out_specs=ANY, out_shape=jax.ShapeDtypeStruct((R2, Cc), Hs.dtype),
        scratch_shapes=[pltpu.SemaphoreType.DMA, pltpu.SemaphoreType.DMA],
    )(Hs)


def _gather_all(buf, name):
    R, Cc = buf.shape

    def body(b_ref, out_ref, send, recv, local):
        x, y, c, s = _place()
        d = 2 * s + c
        mine = pltpu.make_async_copy(b_ref, out_ref.at[d], local)
        mine.start()
        cps = []
        for m in range(1, 8):
            t = d ^ m
            cp = pltpu.make_async_remote_copy(b_ref, out_ref.at[d], send.at[m - 1], recv.at[m - 1],
                                              device_id=(t // 4, (t // 2) % 2, t % 2), device_id_type=MESH)
            cp.start()
            cps.append(cp)
        for cp in cps:
            cp.wait()
        mine.wait()

    return pl.pallas_call(
        body, name=name, in_specs=[ANY], out_specs=ANY, out_shape=jax.ShapeDtypeStruct((8, R, Cc), buf.dtype),
        scratch_shapes=[pltpu.SemaphoreType.DMA((7,)), pltpu.SemaphoreType.DMA((7,)), pltpu.SemaphoreType.DMA],
    )(buf)


def _reduce_to_shard(G, cidx, chip_idx, name):
    S4, R, Cc = G.shape
    R2 = R // 2
    G4 = G.reshape(S4, 2, R2, Cc)
    A = _sibling_swap(G4, name + "_swap")
    S1 = _add_half(G4, A, cidx, name + "_add").reshape(S4, R2, Cc)
    B = _chip_all_to_all(S1, name + "_a2a")
    Hs = _add_own(S1, B, chip_idx, name + "_sum")
    Ho = _sibling_send(Hs, name + "_gather")
    lo = jnp.where(cidx[0] == 0, Hs, Ho)
    hi = jnp.where(cidx[0] == 0, Ho, Hs)
    return jnp.concatenate([lo, hi], axis=0)


def _pack_rows(vs):
    flat = jnp.concatenate([v.reshape(-1) for v in vs])
    n = flat.shape[0]
    rows = -(-n // (LANES * 2 * HALO)) * 2 * HALO
    return jnp.pad(flat, (0, rows * LANES - n)).reshape(rows, LANES)


def _unpack_rows(buf, shapes):
    flat = buf.reshape(-1)
    outs, o = [], 0
    for shp in shapes:
        n = 1
        for d in shp:
            n *= d
        outs.append(flat[o:o + n].reshape(shp))
        o += n
    return outs


def kernel(x, p, norm_mix_g, w_in, conv_a_w, conv_qkv_w, a_log, dt_bias, dn_norm_g, w_out, norm_ffn_g, w_up, conv_ffn_w, w_down, norm_ple_g, w_ple_gate, w_ple_proj, final_norm_g, loss_target, m_norm_mix_g, m_w_in, m_conv_a_w, m_conv_qkv_w, m_a_log, m_dt_bias, m_dn_norm_g, m_w_out, m_norm_ffn_g, m_w_up, m_conv_ffn_w, m_w_down, m_norm_ple_g, m_w_ple_gate, m_w_ple_proj, m_final_norm_g, v_norm_mix_g, v_w_in, v_conv_a_w, v_conv_qkv_w, v_a_log, v_dt_bias, v_dn_norm_g, v_w_out, v_norm_ffn_g, v_w_up, v_conv_ffn_w, v_w_down, v_norm_ple_g, v_w_ple_gate, v_w_ple_proj, v_final_norm_g):
    xs = x[0]
    ps = p[0, 0]
    tgt = loss_target[0]
    T, D = xs.shape
    H = a_log.shape[-1]
    DNW = H * HEAD_DIM
    CW = conv_a_w.shape[-1] * 4
    F = w_down.shape[1] * 4
    PD = ps.shape[-1]
    IN_MAIN = 3 * CW + 4 * DNW
    IN_COLS = IN_MAIN + 2 * H
    assert w_in.shape[-1] * 4 == IN_COLS and CW + DNW == D and 2 * H <= LANES
    cb = _tile(min(CW, DNW), 512, LANES)
    while F % cb:
        cb -= LANES
    cidx = lax.axis_index("c").astype(jnp.int32).reshape(1)
    chip = 2 * lax.axis_index("x") + lax.axis_index("y")

    def gather_w(w, name):
        sh = w[0].astype(BF16)
        R, Cc = sh.shape
        g4 = _gather_chips(sh.reshape(2, R // 2, Cc), name).reshape(4, R, Cc)
        return lax.dynamic_update_slice(g4, sh[None], (chip, 0, 0))

    def cols(g4):
        return jnp.transpose(g4, (1, 0, 2)).reshape(g4.shape[1], 4 * g4.shape[2])

    def rows(g4):
        return g4.reshape(4 * g4.shape[1], g4.shape[2])

    w_in_f = cols(gather_w(w_in, "ag_w_in"))
    w_in_main = w_in_f[:, :IN_MAIN]
    w_in_small = jnp.pad(w_in_f[:, IN_MAIN:], ((0, 0), (0, LANES - 2 * H)))
    w_out_f = rows(gather_w(w_out, "ag_w_out"))
    w_out_a, w_out_b = w_out_f[:CW], w_out_f[CW:]
    w_up_4 = gather_w(w_up, "ag_w_up")
    w_up_g = cols(w_up_4)[:, :F]
    w_up_v = cols(w_up_4)[:, F:]
    w_down_f = rows(gather_w(w_down, "ag_w_down"))
    w_pg_f = rows(gather_w(w_ple_gate, "ag_w_pg"))
    w_pp_f = cols(gather_w(w_ple_proj, "ag_w_pp"))

    conv_shapes = [conv_a_w[0].shape, conv_qkv_w[0].shape, conv_ffn_w[0].shape]
    cpack = _pack_rows([conv_a_w[0], conv_qkv_w[0], conv_ffn_w[0]])
    cg = _gather_chips(cpack.reshape(2, cpack.shape[0] // 2, LANES), "ag_conv").reshape(4, cpack.shape[0], LANES)
    cg = lax.dynamic_update_slice(cg, cpack[None], (chip, 0, 0))
    parts = [_unpack_rows(cg[t], conv_shapes) for t in range(4)]
    cw_a = jnp.concatenate([parts[t][0] for t in range(4)], axis=1)
    cw_qkv = jnp.concatenate([parts[t][1] for t in range(4)], axis=1)
    cw_ffn = jnp.concatenate([parts[t][2] for t in range(4)], axis=1)
    cw_q, cw_k, cw_v = cw_qkv[:, :DNW], cw_qkv[:, DNW:2 * DNW], cw_qkv[:, 2 * DNW:]
    cw_fg, cw_fv = cw_ffn[:, :F], cw_ffn[:, F:]
    pad_row = lambda v: jnp.pad(v, ((0, 0), (0, LANES - v.shape[1])))
    a_log_row, dt_row = pad_row(a_log), pad_row(dt_bias)
    gdn_t = jnp.tile(dn_norm_g, (1, H))
    gfin = final_norm_g.reshape(1, D)

    h1 = _rms_fwd(xs, norm_mix_g, "rms1")
    proj = _mm(h1, w_in_main, mode="nn", out_dtypes=[F32], name="mm_proj")
    small = _mm(h1, w_in_small, mode="nn", out_dtypes=[F32], name="mm_small")
    ya = _ga_fwd(proj, cw_a, CW, cb)
    nq = 3 * CW // cb
    nd = DNW // cb
    qn = _qkv_fwd(proj, cw_q, nq, True, DNW, cb, "q_fwd")
    kn = _qkv_fwd(proj, cw_k, nq + nd, True, DNW, cb, "k_fwd")
    vs = _qkv_fwd(proj, cw_v, nq + 2 * nd, False, DNW, cb, "v_fwd")
    g, beta = _gb_fwd(small, a_log_row, dt_row, H)
    o, S0 = _delta_fwd(qn, kn, vs, g, beta)
    z_coff = (3 * CW + 3 * DNW) // DNW
    assert (3 * CW + 3 * DNW) % DNW == 0 and CW % DNW == 0
    yb = _gnorm_fwd(o, proj, z_coff, gdn_t, DNW)
    add = lambda acc, r: (r + acc,)
    x1 = _mm(ya, w_out_a, mode="nn", out_dtypes=[F32], epi=add, extras=[xs], name="mm_out_a")
    x1 = _mm(yb, w_out_b, mode="nn", out_dtypes=[F32], epi=add, extras=[x1], name="mm_out_b")
    h2 = _rms_fwd(x1, norm_ffn_g, "rms2")
    up_g = _mm(h2, w_up_g, mode="nn", out_dtypes=[F32], name="mm_up_g")
    up_v = _mm(h2, w_up_v, mode="nn", out_dtypes=[F32], name="mm_up_v")
    act = _ffn_fwd(up_g, up_v, cw_fg, cw_fv, cb)
    x2 = _mm(act, w_down_f, mode="nn", out_dtypes=[F32], epi=add, extras=[x1], name="mm_down")
    h3 = _rms_fwd(x2, norm_ple_g, "rms3")
    pp = _mm(ps, w_pp_f, mode="nn", out_dtypes=[F32], name="mm_pp")

    def ple_epi(acc, x2v, ppv):
        pg = _sigmoid(acc)
        return x2v + pg * ppv, pg

    x3, pg = _mm(h3, w_pg_f, mode="nn", out_dtypes=[F32, F32], epi=ple_epi, extras=[x2, pp], name="mm_pg")

    dx3, fin = _final_fb(x3, tgt, gfin)
    loss = lax.psum(jnp.sum(fin[1]), ("x", "y", "c"))
    d_gfin = fin[0:1]
    dpg, dpp = _ple_bwd(dx3, pp, pg)
    dW_pp = _mm(ps, dpp, mode="tn", out_dtypes=[F32], name="mm_dw_pp")
    dW_pg = _mm(h3, dpg, mode="tn", out_dtypes=[F32], name="mm_dw_pg")
    dh3 = _mm(dpg, w_pg_f, mode="nt", out_dtypes=[F32], name="mm_dh3")
    dx2, dx2_b, d_gple = _rms_bwd(dh3, x2, norm_ple_g, dx3, "rms3_bwd")
    dW_down = _mm(act, dx2_b, mode="tn", out_dtypes=[F32], name="mm_dw_down")
    dact = _mm(dx2_b, w_down_f, mode="nt", out_dtypes=[F32], name="mm_dact")
    dup_g, dup_v, dcw_fg, dcw_fv = _ffn_bwd(dact, up_g, up_v, cw_fg, cw_fv, cb)
    dW_up_g = _mm(h2, dup_g, mode="tn", out_dtypes=[F32], name="mm_dw_up_g")
    dW_up_v = _mm(h2, dup_v, mode="tn", out_dtypes=[F32], name="mm_dw_up_v")
    dh2 = _mm(dup_g, w_up_g, mode="nt", out_dtypes=[F32], name="mm_dh2_g")
    dh2 = _mm(dup_v, w_up_v, mode="nt", out_dtypes=[F32], epi=add, extras=[dh2], name="mm_dh2_v")
    dx1, dx1_b, d_gffn = _rms_bwd(dh2, x1, norm_ffn_g, dx2, "rms2_bwd")
    dW_out_a = _mm(ya, dx1_b, mode="tn", out_dtypes=[F32], name="mm_dw_out_a")
    dW_out_b = _mm(yb, dx1_b, mode="tn", out_dtypes=[F32], name="mm_dw_out_b")
    dymix = _mm(dx1_b, w_out_f, mode="nt", out_dtypes=[F32], name="mm_dymix")
    dax, dab, dac, dcw_a = _ga_bwd(dymix, proj, cw_a, CW, cb)
    do, dz, d_gdn = _gnorm_bwd(dymix, CW // DNW, o, proj, z_coff, gdn_t, DNW)
    dqn, dkn, dvs, dgB, dbB = _delta_bwd(qn, kn, vs, g, beta, S0, do)
    dq_pre, dcw_q = _qkv_bwd(dqn, proj, cw_q, nq, True, DNW, cb, "q_bwd")
    dk_pre, dcw_k = _qkv_bwd(dkn, proj, cw_k, nq + nd, True, DNW, cb, "k_bwd")
    dv_pre, dcw_v = _qkv_bwd(dvs, proj, cw_v, nq + 2 * nd, False, DNW, cb, "v_bwd")
    dsmall, d_ab = _gb_bwd(dgB, dbB, small, g, beta, a_log_row, dt_row, H)
    dproj = jnp.concatenate([dax, dab, dac, dq_pre, dk_pre, dv_pre, dz], axis=1)
    dW_in_main = _mm(h1, dproj, mode="tn", out_dtypes=[F32], name="mm_dw_in")
    dW_in_small = _mm(h1, dsmall, mode="tn", out_dtypes=[F32], name="mm_dw_in_small")
    dh1 = _mm(dproj, w_in_main, mode="nt", out_dtypes=[F32], name="mm_dh1")
    dh1 = _mm(dsmall, w_in_small, mode="nt", out_dtypes=[F32], epi=add, extras=[dh1], name="mm_dh1_small")
    dx, _, d_gmix = _rms_bwd(dh1, xs, norm_mix_g, dx1, "rms1_bwd")

    def split_cols(dW):
        R, C4 = dW.shape
        return jnp.transpose(dW.reshape(R, 4, C4 // 4), (1, 0, 2))

    def split_rows(dW):
        return dW.reshape(4, dW.shape[0] // 4, dW.shape[1])

    G_in = split_cols(jnp.concatenate([dW_in_main, dW_in_small[:, :2 * H]], axis=1))
    G_out = split_rows(jnp.concatenate([dW_out_a, dW_out_b], axis=0))
    G_up = split_cols(jnp.concatenate([dW_up_g, dW_up_v], axis=1))
    G_down = split_rows(dW_down)
    G_pg = split_rows(dW_pg)
    G_pp = split_cols(dW_pp)

    def update(G, w, m, v, name):
        gr = _reduce_to_shard(G, cidx, chip.astype(jnp.int32).reshape(1), "rs_" + name)
        delta, m2, v2 = _adamw(w[0], gr, m[0], v[0], "adamw_" + name)
        return gr[None], delta[None], m2[None], v2[None]

    big = {
        "w_in": update(G_in, w_in, m_w_in, v_w_in, "w_in"),
        "w_out": update(G_out, w_out, m_w_out, v_w_out, "w_out"),
        "w_up": update(G_up, w_up, m_w_up, v_w_up, "w_up"),
        "w_down": update(G_down, w_down, m_w_down, v_w_down, "w_down"),
        "w_ple_gate": update(G_pg, w_ple_gate, m_w_ple_gate, v_w_ple_gate, "w_pg"),
        "w_ple_proj": update(G_pp, w_ple_proj, m_w_ple_proj, v_w_ple_proj, "w_pp"),
    }

    small_grads = [d_gmix[0:1], dcw_a[:cw_a.shape[0]], jnp.concatenate([dcw_q, dcw_k, dcw_v], axis=1)[:cw_qkv.shape[0]],
                   d_ab[0:1, :H], d_ab[1:2, :H], d_gdn[0:1], d_gffn[0:1],
                   jnp.concatenate([dcw_fg, dcw_fv], axis=1)[:cw_ffn.shape[0]], d_gple[0:1], d_gfin]
    small_shapes = [v.shape for v in small_grads]
    gpack = _pack_rows(small_grads)
    gsum = _sum_stack(_gather_all(gpack, "ag_small"), "sum_small")
    (g_gmix, g_cwa, g_cwqkv, g_alog, g_dt, g_gdn, g_gffn, g_cwffn, g_gple, g_gfin) = _unpack_rows(gsum, small_shapes)

    def my_cols(v):
        Cc = v.shape[1] // 4
        return lax.dynamic_slice_in_dim(v, chip * Cc, Cc, axis=1)

    g_small = [g_gmix, my_cols(g_cwa), my_cols(g_cwqkv), g_alog, g_dt, g_gdn, g_gffn, my_cols(g_cwffn), g_gple, g_gfin]
    w_small = [norm_mix_g, conv_a_w[0], conv_qkv_w[0], a_log, dt_bias, dn_norm_g, norm_ffn_g, conv_ffn_w[0], norm_ple_g, gfin]
    m_small = [m_norm_mix_g, m_conv_a_w[0], m_conv_qkv_w[0], m_a_log, m_dt_bias, m_dn_norm_g, m_norm_ffn_g, m_conv_ffn_w[0],
               m_norm_ple_g, m_final_norm_g.reshape(1, D)]
    v_small = [v_norm_mix_g, v_conv_a_w[0], v_conv_qkv_w[0], v_a_log, v_dt_bias, v_dn_norm_g, v_norm_ffn_g, v_conv_ffn_w[0],
               v_norm_ple_g, v_final_norm_g.reshape(1, D)]
    shp = [v.shape for v in w_small]
    ds_, ms_, vs_ = _adamw(_pack_rows(w_small), _pack_rows(g_small), _pack_rows(m_small), _pack_rows(v_small), "adamw_small")
    out_shapes = [norm_mix_g.shape, conv_a_w.shape, conv_qkv_w.shape, a_log.shape, dt_bias.shape, dn_norm_g.shape,
                  norm_ffn_g.shape, conv_ffn_w.shape, norm_ple_g.shape, final_norm_g.shape]
    rs = lambda vals: [v.reshape(s) for v, s in zip(vals, out_shapes)]
    sg, sd_, sm_, sv_ = rs(g_small), rs(_unpack_rows(ds_, shp)), rs(_unpack_rows(ms_, shp)), rs(_unpack_rows(vs_, shp))
    names_small = ["norm_mix_g", "conv_a_w", "conv_qkv_w", "a_log", "dt_bias", "dn_norm_g", "norm_ffn_g", "conv_ffn_w",
                   "norm_ple_g", "final_norm_g"]
    res = {n: (sg[i], sd_[i], sm_[i], sv_[i]) for i, n in enumerate(names_small)}
    res.update(big)
    order = ["norm_mix_g", "w_in", "conv_a_w", "conv_qkv_w", "a_log", "dt_bias", "dn_norm_g", "w_out", "norm_ffn_g", "w_up",
             "conv_ffn_w", "w_down", "norm_ple_g", "w_ple_gate", "w_ple_proj", "final_norm_g"]
    return (loss, dx[None], *[res[n][0] for n in order], *[res[n][1] for n in order], *[res[n][2] for n in order],
            *[res[n][3] for n in order])
```

```python
import functools

import jax
import jax.numpy as jnp
from jax import lax
from jax.experimental import pallas as pl
from jax.experimental.pallas import tpu as pltpu

F32 = jnp.float32
BF16 = jnp.bfloat16
LANES = 128
HALO = 8
HEAD_DIM = 128
CHUNK = 64
EPS = 1e-6
VMEM_LIMIT = 56 * 1024 * 1024
MM_VMEM_BUDGET = 36 * 1024 * 1024
MESH = pl.DeviceIdType.MESH

ADAM_LR, ADAM_B1, ADAM_B2, ADAM_EPS, ADAM_WD, ADAM_STEP = 0.001, 0.9, 0.999, 1e-08, 0.01, 10


def _tile(n, cap, unit):
    if n <= cap:
        return n
    d = (cap // unit) * unit
    while d >= unit:
        if n % d == 0:
            return d
        d -= unit
    raise ValueError(f"no tile for {n} (cap {cap}, unit {unit})")


def _sigmoid(x):
    return 1.0 / (1.0 + jnp.exp(-x))


def _mm_tiles(M, N, K, a_bytes, n_blocks_mn):
    tm, tk = _tile(M, 1024, LANES), _tile(K, 2048, LANES)
    for cap in (1024, 512, 256, 128):
        tn = _tile(N, cap, LANES)
        acc = 4 * tm * tn if K // tk > 1 else 0
        if 2 * tm * tk * a_bytes + 2 * tk * tn * 2 + 2 * 4 * tm * tn * n_blocks_mn + acc <= MM_VMEM_BUDGET:
            break
    return tm, tn, tk


def _mm(a, b, *, mode, out_dtypes, name, epi=None, extras=(), comm=None):
    if mode == "nn":
        (M, K), N = a.shape, b.shape[1]
    elif mode == "nt":
        (M, K), N = a.shape, b.shape[0]
    else:
        (K, M), N = a.shape, b.shape[1]
    n_ex, n_out = len(extras), len(out_dtypes)
    tm, tn, tk = _mm_tiles(M, N, K, a.dtype.itemsize, n_ex + n_out)
    nk = K // tk
    a_spec = pl.BlockSpec((tk, tm), lambda i, j, k: (k, i)) if mode == "tn" else pl.BlockSpec((tm, tk), lambda i, j, k: (i, k))
    b_spec = pl.BlockSpec((tn, tk), lambda i, j, k: (j, k)) if mode == "nt" else pl.BlockSpec((tk, tn), lambda i, j, k: (k, j))
    mn_spec = pl.BlockSpec((tm, tn), lambda i, j, k: (i, j))
    dims = {"nn": (((1,), (0,)), ((), ())), "nt": (((1,), (1,)), ((), ())), "tn": (((0,), (0,)), ((), ()))}[mode]

    def body(*refs):
        a_ref, b_ref = refs[0], refs[1]
        ex_refs = refs[2:2 + n_ex]
        out_refs = refs[2 + n_ex:2 + n_ex + n_out]
        part = lax.dot_general(a_ref[...].astype(BF16), b_ref[...].astype(BF16), dims, preferred_element_type=F32)

        def finish(acc):
            outs = (acc,) if epi is None else epi(acc, *[r[...] for r in ex_refs])
            for r, o in zip(out_refs, outs):
                r[...] = o.astype(r.dtype)

        if nk == 1:
            finish(part)
            return
        acc_ref = refs[-1]
        k = pl.program_id(2)

        @pl.when(k == 0)
        def _():
            acc_ref[...] = part

        @pl.when(jnp.logical_and(k > 0, k < nk - 1))
        def _():
            acc_ref[...] += part

        @pl.when(k == nk - 1)
        def _():
            finish(acc_ref[...] + part)

    outs, comm_outs = _call(
        body, name=name, grid=(M // tm, N // tn, nk),
        in_specs=[a_spec, b_spec] + [mn_spec] * n_ex,
        out_specs=[mn_spec] * n_out,
        out_shape=[jax.ShapeDtypeStruct((M, N), dt) for dt in out_dtypes],
        scratch_shapes=[pltpu.VMEM((tm, tn), F32)] if nk > 1 else [],
        semantics=("parallel", "parallel", "arbitrary"), args=(a, b, *extras), comm=comm)
    res = outs[0] if n_out == 1 else outs
    return res if comm is None else (res, comm_outs)


def _tiled(fn, *, T, C, ins, out_dtypes=(), acc_rows=(), tb=256, cb=512, name):
    tb = _tile(T, tb, HALO)
    nI, nJ = T // tb, C // cb
    hb, nH = tb // HALO, T // HALO
    specs, args, kinds = [], [], []
    for kind, arr, cmap in ins:
        cm = cmap if cmap is not None else (lambda j: j)
        kinds.append(kind)
        if kind == "cur":
            specs.append(pl.BlockSpec((tb, cb), lambda j, i, cm=cm: (i, cm(j))))
            args.append(arr)
        elif kind == "ext":
            specs.append(pl.BlockSpec((HALO, cb), lambda j, i, cm=cm: (jnp.maximum(i * hb - 1, 0), cm(j))))
            specs.append(pl.BlockSpec((tb, cb), lambda j, i, cm=cm: (i, cm(j))))
            specs.append(pl.BlockSpec((HALO, cb), lambda j, i, cm=cm: (jnp.minimum((i + 1) * hb, nH - 1), cm(j))))
            args += [arr, arr, arr]
        elif kind == "row":
            specs.append(pl.BlockSpec((arr.shape[0], cb), lambda j, i, cm=cm: (0, cm(j))))
            args.append(arr)
        elif kind == "stack":
            specs.append(pl.BlockSpec((arr.shape[0], tb, cb), lambda j, i, cm=cm: (0, i, cm(j))))
            args.append(arr)
        else:
            raise ValueError(kind)
    n_in = len(args)
    n_out, n_acc = len(out_dtypes), len(acc_rows)

    def body(*refs):
        j, i = pl.program_id(0), pl.program_id(1)
        vals, r = [], 0
        for kind in kinds:
            if kind == "ext":
                prev = jnp.where(i == 0, 0.0, refs[r][...].astype(F32))
                cur = refs[r + 1][...].astype(F32)
                nxt = jnp.where(i == nI - 1, 0.0, refs[r + 2][...].astype(F32))
                vals.append(jnp.concatenate([prev, cur, nxt], axis=0))
                r += 3
            else:
                vals.append(refs[r][...])
                r += 1
        res = fn(j, i, *vals)
        for ref, o in zip(refs[n_in:n_in + n_out], res[:n_out]):
            ref[...] = o.astype(ref.dtype)
        for ref, o in zip(refs[n_in + n_out:], res[n_out:]):
            @pl.when(i == 0)
            def _(ref=ref, o=o):
                ref[...] = o

            @pl.when(i > 0)
            def _(ref=ref, o=o):
                ref[...] += o

    outs = pl.pallas_call(
        body, name=name, grid=(nJ, nI), in_specs=specs,
        out_specs=[pl.BlockSpec((tb, cb), lambda j, i: (i, j))] * n_out
        + [pl.BlockSpec((rows, cb), lambda j, i: (0, j)) for rows in acc_rows],
        out_shape=[jax.ShapeDtypeStruct((T, C), dt) for dt in out_dtypes]
        + [jax.ShapeDtypeStruct((rows, C), F32) for rows in acc_rows],
        compiler_params=pltpu.CompilerParams(dimension_semantics=("parallel", "arbitrary"),
                                             vmem_limit_bytes=VMEM_LIMIT),
    )(*args)
    return outs


def _conv_causal(xe, w):
    K = w.shape[0]
    y = xe * w[K - 1:K]
    for j in range(K - 1):
        y = y + pltpu.roll(xe, K - 1 - j, 0) * w[j:j + 1]
    return y


def _conv_anti(de, w):
    K, n = w.shape[0], de.shape[0]
    y = de * w[K - 1:K]
    for j in range(K - 1):
        y = y + pltpu.roll(de, n - (K - 1 - j), 0) * w[j:j + 1]
    return y


def _conv_dw(dce, xe, K):
    n = dce.shape[0]
    tb = n - 2 * HALO
    rows = []
    for j in range(K):
        xs = xe if j == K - 1 else pltpu.roll(xe, K - 1 - j, 0)
        rows.append(jnp.sum((dce * xs)[HALO:HALO + tb], axis=0, keepdims=True))
    rows.append(jnp.zeros((HALO - K, dce.shape[1]), F32))
    return jnp.concatenate(rows, axis=0)


def _own(xe):
    return xe[HALO:xe.shape[0] - HALO]


def _row0(v):
    return jnp.concatenate([v, jnp.zeros((HALO - 1, v.shape[1]), F32)], axis=0)


def _per_head(fn, *xs):
    n = xs[0].shape[1] // HEAD_DIM
    outs = [fn(*[x[:, g * HEAD_DIM:(g + 1) * HEAD_DIM] for x in xs]) for g in range(n)]
    return outs[0] if n == 1 else jnp.concatenate(outs, axis=1)


def _rms_fwd(x, g, name):
    T, D = x.shape

    def fn(j, i, xv, gv):
        r = lax.rsqrt(jnp.mean(xv * xv, axis=1, keepdims=True) + EPS)
        return (xv * r * gv,)

    return _tiled(fn, T=T, C=D, ins=[("cur", x, None), ("row", g, None)], out_dtypes=[BF16], cb=D, name=name)[0]


def _rms_bwd_math(dy, xv, gv):
    r = lax.rsqrt(jnp.mean(xv * xv, axis=1, keepdims=True) + EPS)
    xh = xv * r
    dxh = dy * gv
    dx = r * (dxh - xh * jnp.mean(dxh * xh, axis=1, keepdims=True))
    dg = jnp.sum(dy * xh, axis=0, keepdims=True)
    return dx, dg


def _rms_bwd(dh, x, g, dres, name):
    T, D = x.shape

    def fn(j, i, dhv, xv, gv, dr):
        dx, dg = _rms_bwd_math(dhv, xv, gv)
        return dr + dx, dr + dx, _row0(dg)

    return _tiled(fn, T=T, C=D, ins=[("cur", dh, None), ("cur", x, None), ("row", g, None), ("cur", dres, None)],
                  out_dtypes=[F32, BF16], acc_rows=[HALO], cb=D, name=name)


def _final_fb(x3, tgt, g):
    T, D = x3.shape

    def fn(j, i, xv, tv, gv):
        r = lax.rsqrt(jnp.mean(xv * xv, axis=1, keepdims=True) + EPS)
        xh = xv * r
        e = xh * gv - tv
        dy = e * (1.0 / D)
        dxh = dy * gv
        dx = r * (dxh - xh * jnp.mean(dxh * xh, axis=1, keepdims=True))
        dg = jnp.sum(dy * xh, axis=0, keepdims=True)
        ls = jnp.sum(e * e, axis=0, keepdims=True) * (0.5 / D)
        return dx, jnp.concatenate([dg, ls, jnp.zeros((HALO - 2, D), F32)], axis=0)

    return _tiled(fn, T=T, C=D, ins=[("cur", x3, None), ("cur", tgt, None), ("row", g, None)],
                  out_dtypes=[F32], acc_rows=[HALO], cb=D, name="final_fb")


def _ga_fwd(proj, w_a, CW, cb):
    T = proj.shape[0]
    n = CW // cb

    def fn(j, i, ax, ab, ac, w):
        c = _conv_causal(ac * ax, w)
        return (ab * _own(c),)

    return _tiled(fn, T=T, C=CW, ins=[("ext", proj, None), ("cur", proj, lambda j: j + n), ("ext", proj, lambda j: j + 2 * n),
                                       ("row", w_a, None)], out_dtypes=[BF16], cb=cb, name="ga_fwd")[0]


def _ga_bwd(dymix, proj, w_a, CW, cb):
    T = proj.shape[0]
    n = CW // cb
    K = w_a.shape[0]

    def fn(j, i, dy, ax, ab, ac, w):
        u = ac * ax
        c = _conv_causal(u, w)
        dc = dy * ab
        du = _conv_anti(dc, w)
        return _own(du * ac), _own(dy * c), _own(du * ax), _conv_dw(dc, u, K)

    return _tiled(fn, T=T, C=CW, ins=[("ext", dymix, None), ("ext", proj, None), ("ext", proj, lambda j: j + n),
                                       ("ext", proj, lambda j: j + 2 * n), ("row", w_a, None)],
                  out_dtypes=[BF16, BF16, BF16], acc_rows=[HALO], cb=cb, name="ga_bwd")


def _l2n(s):
    return s * lax.rsqrt(jnp.sum(s * s, axis=1, keepdims=True) + EPS)


def _qkv_fwd(proj, w_sec, coff, normalize, DNW, cb, name):
    T = proj.shape[0]

    def fn(j, i, pre, w):
        c = _own(_conv_causal(pre, w))
        s = c * _sigmoid(c)
        return (_per_head(_l2n, s) if normalize else s,)

    return _tiled(fn, T=T, C=DNW, ins=[("ext", proj, lambda j: j + coff), ("row", w_sec, None)],
                  out_dtypes=[F32], cb=cb, name=name)[0]


def _qkv_bwd(dsec, proj, w_sec, coff, normalize, DNW, cb, name):
    T = proj.shape[0]
    K = w_sec.shape[0]

    def l2n_bwd(s, dn):
        r = lax.rsqrt(jnp.sum(s * s, axis=1, keepdims=True) + EPS)
        nrm = s * r
        return r * (dn - nrm * jnp.sum(dn * nrm, axis=1, keepdims=True))

    def fn(j, i, dn, pre, w):
        c = _conv_causal(pre, w)
        sg = _sigmoid(c)
        s = c * sg
        ds = _per_head(l2n_bwd, s, dn) if normalize else dn
        dc = ds * (sg * (1.0 + c * (1.0 - sg)))
        return _own(_conv_anti(dc, w)), _conv_dw(dc, pre, K)

    return _tiled(fn, T=T, C=DNW, ins=[("ext", dsec, None), ("ext", proj, lambda j: j + coff), ("row", w_sec, None)],
                  out_dtypes=[BF16], acc_rows=[HALO], cb=cb, name=name)


def _gb_fwd(small, a_log_row, dt_row, H):
    T = small.shape[0]

    def fn(j, i, sm, al, dt):
        z = sm + dt
        sp = jnp.maximum(z, 0.0) + jnp.log(1.0 + jnp.exp(-jnp.abs(z)))
        g = -jnp.exp(al) * sp
        beta = _sigmoid(pltpu.roll(sm, LANES - H, 1))
        return g, beta

    return _tiled(fn, T=T, C=LANES, ins=[("cur", small, None), ("row", a_log_row, None), ("row", dt_row, None)],
                  out_dtypes=[F32, F32], cb=LANES, name="gb_fwd")


def _gb_bwd(dgB, dbB, small, g, beta, a_log_row, dt_row, H):
    T = small.shape[0]

    def fn(j, i, dgv, dbv, sm, gv, bv, al, dt):
        lane = lax.broadcasted_iota(jnp.int32, sm.shape, 1)
        dg = jnp.zeros(sm.shape, F32)
        db = jnp.zeros(sm.shape, F32)
        for h in range(H):
            dg = jnp.where(lane == h, jnp.sum(dgv[h], axis=1, keepdims=True), dg)
            db = jnp.where(lane == h, jnp.sum(dbv[h], axis=1, keepdims=True), db)
        da = dg * (-jnp.exp(al)) * _sigmoid(sm + dt)
        dbb = db * bv * (1.0 - bv)
        dsm = jnp.where(lane < H, da, 0.0) + pltpu.roll(jnp.where(lane < H, dbb, 0.0), H, 1)
        d_alog = jnp.sum(jnp.where(lane < H, dg * gv, 0.0), axis=0, keepdims=True)
        d_dt = jnp.sum(jnp.where(lane < H, da, 0.0), axis=0, keepdims=True)
        return dsm, jnp.concatenate([d_alog, d_dt, jnp.zeros((HALO - 2, LANES), F32)], axis=0)

    return _tiled(fn, T=T, C=LANES, ins=[("stack", dgB, None), ("stack", dbB, None), ("cur", small, None), ("cur", g, None),
                                          ("cur", beta, None), ("row", a_log_row, None), ("row", dt_row, None)],
                  out_dtypes=[BF16], acc_rows=[HALO], cb=LANES, name="gb_bwd")


_DIMS = {"nn": (((1,), (0,)), ((), ())), "nt": (((1,), (1,)), ((), ())), "tn": (((0,), (0,)), ((), ()))}
_DOT_BWD = {"nn": (("nt", "gb"), ("tn", "ag")), "nt": (("nn", "gb"), ("tn", "ga")), "tn": (("nt", "bg"), ("nn", "ag"))}


def _split(a):
    hi = a.astype(BF16)
    return hi, (a - hi.astype(F32)).astype(BF16)


def _raw_dot(a, b, kind, passes):
    dg = lambda x, y: lax.dot_general(x, y, _DIMS[kind], preferred_element_type=F32)
    if passes == 1:
        return dg(a.astype(BF16), b.astype(BF16))
    ah, al = _split(a)
    bh, bl = _split(b)
    return dg(ah, bh) + (dg(ah, bl) + dg(al, bh))


@functools.lru_cache(maxsize=None)
def _dotf(kind, passes):
    @jax.custom_vjp
    def f(a, b):
        return _raw_dot(a, b, kind, passes)

    def fwd(a, b):
        return _raw_dot(a, b, kind, passes), (a, b)

    def bwd(res, g):
        ops = {"a": res[0], "b": res[1], "g": g}
        (ka, oa), (kb, ob) = _DOT_BWD[kind]
        return (_raw_dot(ops[oa[0]], ops[oa[1]], ka, passes), _raw_dot(ops[ob[0]], ops[ob[1]], kb, passes))

    f.defvjp(fwd, bwd)
    return f


def _chunk_fn(q, k, v, gB, bB, S):
    C = CHUNK
    d3, d3nt = _dotf("nn", 3), _dotf("nt", 3)
    d1, d1nt, d1tn = _dotf("nn", 1), _dotf("nt", 1), _dotf("tn", 1)
    each = lambda f, *ls: tuple(f(*xs) for xs in zip(*ls))
    row = lax.broadcasted_iota(jnp.int32, (C, C), 0)
    col = lax.broadcasted_iota(jnp.int32, (C, C), 1)
    causal = row >= col
    strict = row > col
    tril = jnp.where(causal, 1.0, 0.0).astype(F32)
    eye = jnp.where(row == col, 1.0, 0.0).astype(F32)
    avg = jnp.full((C, HEAD_DIM), 1.0 / HEAD_DIM, F32)
    gc = each(lambda g: d3(tril, g), gB)
    R = each(lambda g: d3nt(avg, g), gc)
    decay = each(lambda g, r: jnp.where(causal, jnp.exp(jnp.where(causal, g[:, :C] - r, 0.0)), 0.0), gc, R)
    kk = each(lambda x: d1nt(x, x), k)
    L = each(lambda a, d, b: jnp.where(strict, a * d * b[:, :C], 0.0), kk, decay, bB)
    inv = each(lambda l: eye - l, L)
    P = L
    for _ in range(5):
        P = each(lambda p: d3(p, p), P)
        inv = each(lambda a, p: d3(a, eye + p), inv, P)
    eg = each(jnp.exp, gc)
    u = each(lambda a, x, b: d3(a, x * b), inv, v, bB)
    w = each(lambda a, x, b, e: d3(a, x * b * e), inv, k, bB, eg)
    qs = each(lambda x: x * (HEAD_DIM ** -0.5), q)
    qk = each(lambda a, x, d: d1nt(a, x) * d, qs, k, decay)
    gl = each(lambda g: g[C - 1:C, :], gc)
    v_new = each(lambda a, b, s: a - d1(b, s), u, w, S)
    o1 = each(lambda a, e, s: d1(a * e, s), qs, eg, S)
    o = each(lambda a, b, c: a + d1(b, c), o1, qk, v_new)
    kv = each(lambda x, a, g, vn: d1tn(x * jnp.exp(a - g), vn), k, gl, gc, v_new)
    S_new = each(lambda s, a, b: s * jnp.exp(a) + b, S, gl, kv)
    return o, S_new


def _sel_lane(x, h):
    lane = lax.broadcasted_iota(jnp.int32, x.shape, 1)
    return jnp.broadcast_to(jnp.sum(jnp.where(lane == h, x, 0.0), axis=1, keepdims=True), x.shape)


def _head(ref, h):
    return ref[:, h * HEAD_DIM:(h + 1) * HEAD_DIM]


def _delta_fwd(q, k, v, g, beta, comm=None):
    T = q.shape[0]
    H, N = q.shape[1] // HEAD_DIM, T // CHUNK

    def body(q_ref, k_ref, v_ref, g_ref, b_ref, o_ref, s_ref, S):
        @pl.when(pl.program_id(0) == 0)
        def _():
            S[...] = jnp.zeros_like(S)

        gv, bv = g_ref[...], b_ref[...]
        heads = lambda f: tuple(f(h) for h in range(H))
        S_in = heads(lambda h: S[h])
        for h in range(H):
            s_ref[h, 0] = S_in[h]
        o, S_new = _chunk_fn(heads(lambda h: _head(q_ref, h)), heads(lambda h: _head(k_ref, h)), heads(lambda h: _head(v_ref, h)),
                             heads(lambda h: _sel_lane(gv, h)), heads(lambda h: _sel_lane(bv, h)), S_in)
        for h in range(H):
            o_ref[:, h * HEAD_DIM:(h + 1) * HEAD_DIM] = o[h]
            S[h] = S_new[h]

    blk = pl.BlockSpec((CHUNK, H * HEAD_DIM), lambda n: (n, 0))
    gblk = pl.BlockSpec((CHUNK, LANES), lambda n: (n, 0))
    outs, comm_outs = _call(
        body, name="delta_fwd", grid=(N,), in_specs=[blk, blk, blk, gblk, gblk],
        out_specs=[blk, pl.BlockSpec((H, 1, HEAD_DIM, HEAD_DIM), lambda n: (0, n, 0, 0))],
        out_shape=[jax.ShapeDtypeStruct((T, H * HEAD_DIM), F32), jax.ShapeDtypeStruct((H, N, HEAD_DIM, HEAD_DIM), F32)],
        scratch_shapes=[pltpu.VMEM((H, HEAD_DIM, HEAD_DIM), F32)],
        semantics=("arbitrary",), args=(q, k, v, g, beta), comm=comm)
    return outs[0], outs[1], comm_outs


def _delta_bwd(q, k, v, g, beta, S0, do):
    T = q.shape[0]
    H, N = q.shape[1] // HEAD_DIM, T // CHUNK

    def body(q_ref, k_ref, v_ref, g_ref, b_ref, s_ref, do_ref, dq_ref, dk_ref, dv_ref, dg_ref, db_ref, dS):
        @pl.when(pl.program_id(0) == 0)
        def _():
            dS[...] = jnp.zeros_like(dS)

        gv, bv = g_ref[...], b_ref[...]
        heads = lambda f: tuple(f(h) for h in range(H))
        _, vjp = jax.vjp(_chunk_fn, heads(lambda h: _head(q_ref, h)), heads(lambda h: _head(k_ref, h)),
                         heads(lambda h: _head(v_ref, h)), heads(lambda h: _sel_lane(gv, h)), heads(lambda h: _sel_lane(bv, h)),
                         heads(lambda h: s_ref[h, 0]))
        dq, dk, dv, dgB, dbB, dS_prev = vjp((heads(lambda h: _head(do_ref, h)), heads(lambda h: dS[h])))
        for h in range(H):
            sl = slice(h * HEAD_DIM, (h + 1) * HEAD_DIM)
            dq_ref[:, sl] = dq[h]
            dk_ref[:, sl] = dk[h]
            dv_ref[:, sl] = dv[h]
            dg_ref[h] = dgB[h]
            db_ref[h] = dbB[h]
            dS[h] = dS_prev[h]

    blk = pl.BlockSpec((CHUNK, H * HEAD_DIM), lambda n: (N - 1 - n, 0))
    gblk = pl.BlockSpec((CHUNK, LANES), lambda n: (N - 1 - n, 0))
    hblk = pl.BlockSpec((H, CHUNK, LANES), lambda n: (0, N - 1 - n, 0))
    sd = jax.ShapeDtypeStruct
    return pl.pallas_call(
        body, name="delta_bwd", grid=(N,),
        in_specs=[blk, blk, blk, gblk, gblk, pl.BlockSpec((H, 1, HEAD_DIM, HEAD_DIM), lambda n: (0, N - 1 - n, 0, 0)), blk],
        out_specs=[blk, blk, blk, hblk, hblk],
        out_shape=[sd((T, H * HEAD_DIM), F32)] * 3 + [sd((H, T, LANES), F32)] * 2,
        scratch_shapes=[pltpu.VMEM((H, HEAD_DIM, HEAD_DIM), F32)],
        compiler_params=pltpu.CompilerParams(dimension_semantics=("arbitrary",)),
    )(q, k, v, g, beta, S0, do)


def _gnorm_fwd(o, proj, z_coff, gdn_t, DNW):
    T = o.shape[0]

    def fn(j, i, ov, zv, gv):
        def one(oh, zh, gh):
            r = lax.rsqrt(jnp.mean(oh * oh, axis=1, keepdims=True) + EPS)
            return oh * r * gh * (zh * _sigmoid(zh))
        return (_per_head(one, ov, zv, jnp.broadcast_to(gv, ov.shape)),)

    return _tiled(fn, T=T, C=DNW, ins=[("cur", o, None), ("cur", proj, lambda j: j + z_coff), ("row", gdn_t, None)],
                  out_dtypes=[BF16], cb=DNW, name="gnorm_fwd")[0]


def _gnorm_bwd(dymix, y_coff, o, proj, z_coff, gdn_t, DNW):
    T = o.shape[0]
    nh = DNW // HEAD_DIM

    def fn(j, i, dy, ov, zv, gv):
        dos, dzs, dgs = [], [], jnp.zeros((1, HEAD_DIM), F32)
        for h in range(nh):
            sl = slice(h * HEAD_DIM, (h + 1) * HEAD_DIM)
            dyh, oh, zh, gh = dy[:, sl].astype(F32), ov[:, sl], zv[:, sl], gv[:, sl]
            r = lax.rsqrt(jnp.mean(oh * oh, axis=1, keepdims=True) + EPS)
            on = oh * r
            sg = _sigmoid(zh)
            sz = zh * sg
            dzs.append(dyh * on * gh * (sg * (1.0 + zh * (1.0 - sg))))
            don = dyh * gh * sz
            dos.append(r * (don - on * jnp.mean(don * on, axis=1, keepdims=True)))
            dgs = dgs + jnp.sum(dyh * on * sz, axis=0, keepdims=True)
        cat = (lambda xs: xs[0] if nh == 1 else jnp.concatenate(xs, axis=1))
        return cat(dos), cat(dzs), _row0(dgs)

    T_ = T
    nI = T_ // _tile(T_, 256, HALO)
    tb = T_ // nI
    specs_cb = DNW

    def body_wrap():
        def body(dy_ref, o_ref, z_ref, g_ref, do_ref, dz_ref, dg_ref):
            i = pl.program_id(0)
            d_o, d_z, d_g = fn(0, i, dy_ref[...], o_ref[...], z_ref[...], g_ref[...])
            do_ref[...] = d_o
            dz_ref[...] = d_z.astype(dz_ref.dtype)

            @pl.when(i == 0)
            def _():
                dg_ref[...] = d_g

            @pl.when(i > 0)
            def _():
                dg_ref[...] += d_g

        return pl.pallas_call(
            body, name="gnorm_bwd", grid=(nI,),
            in_specs=[pl.BlockSpec((tb, specs_cb), lambda i: (i, y_coff)), pl.BlockSpec((tb, specs_cb), lambda i: (i, 0)),
                      pl.BlockSpec((tb, specs_cb), lambda i: (i, z_coff)), pl.BlockSpec((1, specs_cb), lambda i: (0, 0))],
            out_specs=[pl.BlockSpec((tb, specs_cb), lambda i: (i, 0)), pl.BlockSpec((tb, specs_cb), lambda i: (i, 0)),
                       pl.BlockSpec((HALO, HEAD_DIM), lambda i: (0, 0))],
            out_shape=[jax.ShapeDtypeStruct((T_, DNW), F32), jax.ShapeDtypeStruct((T_, DNW), BF16),
                       jax.ShapeDtypeStruct((HALO, HEAD_DIM), F32)],
            compiler_params=pltpu.CompilerParams(dimension_semantics=("arbitrary",), vmem_limit_bytes=VMEM_LIMIT),
        )(dymix, o, proj, gdn_t)

    return body_wrap()


def _ffn_fwd(up_g, up_v, w_g, w_v, cb):
    T, F = up_g.shape

    def fn(j, i, ug, uv, wg, wv):
        cg = _own(_conv_causal(ug, wg))
        cv = _own(_conv_causal(uv, wv))
        return (cg * _sigmoid(cg) * cv,)

    return _tiled(fn, T=T, C=F, ins=[("ext", up_g, None), ("ext", up_v, None), ("row", w_g, None), ("row", w_v, None)],
                  out_dtypes=[BF16], cb=cb, name="ffn_fwd")[0]


def _ffn_bwd(dact, up_g, up_v, w_g, w_v, cb):
    T, F = up_g.shape
    K = w_g.shape[0]

    def fn(j, i, da, ug, uv, wg, wv):
        cg = _conv_causal(ug, wg)
        cv = _conv_causal(uv, wv)
        sg = _sigmoid(cg)
        dgate = da * cv * (sg * (1.0 + cg * (1.0 - sg)))
        dval = da * (cg * sg)
        return (_own(_conv_anti(dgate, wg)), _own(_conv_anti(dval, wv)), _conv_dw(dgate, ug, K), _conv_dw(dval, uv, K))

    return _tiled(fn, T=T, C=F, ins=[("ext", dact, None), ("ext", up_g, None), ("ext", up_v, None), ("row", w_g, None),
                                      ("row", w_v, None)], out_dtypes=[BF16, BF16], acc_rows=[HALO, HALO], cb=cb, name="ffn_bwd")


def _ple_bwd(dx3, pp, pg):
    T, D = dx3.shape

    def fn(j, i, d, ppv, pgv):
        return d * ppv * pgv * (1.0 - pgv), d * pgv

    return _tiled(fn, T=T, C=D, ins=[("cur", dx3, None), ("cur", pp, None), ("cur", pg, None)],
                  out_dtypes=[BF16, BF16], cb=_tile(D, 512, LANES), name="ple_bwd")


def _adamw(w, g, m, v, name):
    R, Cc = w.shape
    cb = _tile(Cc, 512, LANES) if Cc % LANES == 0 else Cc
    c1 = 1.0 / (1.0 - ADAM_B1 ** ADAM_STEP)
    c2 = 1.0 / (1.0 - ADAM_B2 ** ADAM_STEP)

    def fn(j, i, wv, gv, mv, vv):
        m2 = ADAM_B1 * mv + (1.0 - ADAM_B1) * gv
        v2 = ADAM_B2 * vv + (1.0 - ADAM_B2) * (gv * gv)
        delta = -ADAM_LR * ((m2 * c1) / (jnp.sqrt(v2 * c2) + ADAM_EPS) + ADAM_WD * wv)
        return delta, m2, v2

    return _tiled(fn, T=R, C=Cc, ins=[("cur", w, None), ("cur", g, None), ("cur", m, None), ("cur", v, None)],
                  out_dtypes=[F32, F32, F32], cb=cb, name=name)


def _sum_stack(st, name):
    S, R, Cc = st.shape
    cb = _tile(Cc, 512, LANES) if Cc % LANES == 0 else Cc

    def fn(j, i, sv):
        t = sv[0]
        for s in range(1, S):
            t = t + sv[s]
        return (t,)

    return _tiled(fn, T=R, C=Cc, ins=[("stack", st, None)], out_dtypes=[F32], cb=cb, name=name)[0]


ANY = pl.BlockSpec(memory_space=pl.ANY)


def _place():
    x, y, c = lax.axis_index("x"), lax.axis_index("y"), lax.axis_index("c")
    return x, y, c, 2 * x + y


def _chip_dev(s, c):
    return (s // 2, s % 2, c)


class _Comm:
    def __init__(self, ins, out_shapes, sems, start, wait, aliases=None):
        self.ins, self.out_shapes, self.sems = list(ins), list(out_shapes), list(sems)
        self.start, self.wait, self.aliases = start, wait, dict(aliases or {})


def _merge(*comms):
    offs, i, o, s = [], 0, 0, 0
    for cm in comms:
        offs.append((i, o, s))
        i, o, s = i + len(cm.ins), o + len(cm.out_shapes), s + len(cm.sems)

    def part(refs, k, cm):
        i0, o0, s0 = offs[k]
        return refs[0][i0:i0 + len(cm.ins)], refs[1][o0:o0 + len(cm.out_shapes)], refs[2][s0:s0 + len(cm.sems)]

    def start(*refs):
        for k, cm in enumerate(comms):
            cm.start(*part(refs, k, cm))

    def wait(*refs):
        for k, cm in enumerate(comms):
            cm.wait(*part(refs, k, cm))

    aliases = {}
    for k, cm in enumerate(comms):
        for a, b in cm.aliases.items():
            aliases[offs[k][0] + a] = offs[k][1] + b
    return _Comm([a for cm in comms for a in cm.ins], [a for cm in comms for a in cm.out_shapes],
                 [a for cm in comms for a in cm.sems], start, wait, aliases)


def _call(body, *, name, grid, in_specs, out_specs, out_shape, scratch_shapes, semantics, args, comm=None):
    if comm is None:
        outs = pl.pallas_call(
            body, name=name, grid=grid, in_specs=in_specs, out_specs=out_specs, out_shape=out_shape,
            scratch_shapes=list(scratch_shapes),
            compiler_params=pltpu.CompilerParams(dimension_semantics=semantics, vmem_limit_bytes=VMEM_LIMIT))(*args)
        return list(outs), []
    n_in, n_out, n_scr = len(in_specs), len(out_specs), len(scratch_shapes)
    ci, co = len(comm.ins), len(comm.out_shapes)

    def wrapped(*refs):
        r = 0
        ins, r = refs[r:r + n_in], r + n_in
        cins, r = refs[r:r + ci], r + ci
        outs, r = refs[r:r + n_out], r + n_out
        couts, r = refs[r:r + co], r + co
        scr, r = refs[r:r + n_scr], r + n_scr
        csems = refs[r:]
        ids = [pl.program_id(a) for a in range(len(grid))]
        first, last = ids[0] == 0, ids[0] == grid[0] - 1
        for a in range(1, len(grid)):
            first = jnp.logical_and(first, ids[a] == 0)
            last = jnp.logical_and(last, ids[a] == grid[a] - 1)

        @pl.when(first)
        def _():
            comm.start(cins, couts, csems)

        body(*ins, *outs, *scr)

        @pl.when(last)
        def _():
            comm.wait(cins, couts, csems)

    outs = pl.pallas_call(
        wrapped, name=name, grid=grid, in_specs=list(in_specs) + [ANY] * ci, out_specs=list(out_specs) + [ANY] * co,
        out_shape=list(out_shape) + comm.out_shapes, scratch_shapes=list(scratch_shapes) + comm.sems,
        input_output_aliases={n_in + a: n_out + b for a, b in comm.aliases.items()},
        compiler_params=pltpu.CompilerParams(dimension_semantics=("arbitrary",) * len(grid), vmem_limit_bytes=VMEM_LIMIT),
    )(*args, *comm.ins)
    return list(outs[:n_out]), list(outs[n_out:])


def _run_comm(comm, name):
    ci, co = len(comm.ins), len(comm.out_shapes)

    def body(*refs):
        cins, couts, csems = refs[:ci], refs[ci:ci + co], refs[ci + co:]
        comm.start(cins, couts, csems)
        comm.wait(cins, couts, csems)

    outs = pl.pallas_call(body, name=name, in_specs=[ANY] * ci, out_specs=[ANY] * co, out_shape=comm.out_shapes,
                          scratch_shapes=comm.sems, input_output_aliases=comm.aliases)(*comm.ins)
    return list(outs)


def _ag_comm(shard, land=None, q=0, nq=1):
    two, R2, Cc = shard.shape
    rows = pl.ds(q * (R2 // nq), R2 // nq)
    DMA = pltpu.SemaphoreType.DMA

    def copies(ins, outs, sems, which):
        sh, out = ins[0], outs[0]
        send1, recv1, send2, recv2, send0, recv0 = sems
        x, y, c, s = _place()
        sib = (x, y, 1 - c)
        rc = pltpu.make_async_remote_copy
        if which == "first":
            return [rc(sh.at[c, rows], out.at[s, c, rows], send1.at[m - 1], recv1.at[m - 1],
                       device_id=_chip_dev(s ^ m, c), device_id_type=MESH) for m in range(1, 4)]
        if which == "own":
            return [rc(sh.at[h, rows], out.at[s, h, rows], send0.at[h], recv0.at[h], device_id=sib, device_id_type=MESH)
                    for h in range(2)]
        if which == "landed":
            return [rc(sh.at[c, rows], out.at[s ^ m, c, rows], send1.at[m - 1], recv1.at[m - 1], device_id=sib,
                       device_id_type=MESH) for m in range(1, 4)]
        half = c if which == "passed" else 1 - c
        return [rc(out.at[s ^ m, half, rows], out.at[s ^ m, half, rows], send2.at[m - 1], recv2.at[m - 1], device_id=sib,
                   device_id_type=MESH) for m in range(1, 4)]

    def start(ins, outs, sems):
        for cp in copies(ins, outs, sems, "first") + copies(ins, outs, sems, "own"):
            cp.start()

    def wait(ins, outs, sems):
        passed = copies(ins, outs, sems, "passed")
        for lan, pas in zip(copies(ins, outs, sems, "landed"), passed):
            lan.wait_recv()
            pas.start()
        for cp in copies(ins, outs, sems, "handed"):
            cp.wait_recv()
        for cp in copies(ins, outs, sems, "own"):
            cp.wait()
        for cp in copies(ins, outs, sems, "first") + passed:
            cp.wait_send()

    return _Comm([shard] + ([land] if land is not None else []), [jax.ShapeDtypeStruct((4, two, R2, Cc), shard.dtype)],
                 [DMA((3,)), DMA((3,)), DMA((3,)), DMA((3,)), DMA((2,)), DMA((2,))], start, wait,
                 {1: 0} if land is not None else None)


def _a2a_comm(S1, q=0, nq=1, land=None):
    S4, R2, Cc = S1.shape
    rows = pl.ds(q * (R2 // nq), R2 // nq)
    DMA = pltpu.SemaphoreType.DMA

    def copies(ins, outs, sems):
        x, y, c, s = _place()
        return [pltpu.make_async_remote_copy(ins[0].at[s ^ m, rows], outs[0].at[m - 1, rows], sems[0].at[m - 1],
                                             sems[1].at[m - 1], device_id=_chip_dev(s ^ m, c), device_id_type=MESH)
                for m in range(1, 4)]

    def start(ins, outs, sems):
        for cp in copies(ins, outs, sems):
            cp.start()

    def wait(ins, outs, sems):
        for cp in copies(ins, outs, sems):
            cp.wait()

    return _Comm([S1] + ([land] if land is not None else []), [jax.ShapeDtypeStruct((3, R2, Cc), S1.dtype)],
                 [DMA((3,)), DMA((3,))], start, wait, {1: 0} if land is not None else None)


def _sibling_swap(G, name):
    S4, two, R2, Cc = G.shape

    def body(g_ref, out_ref, send, recv):
        x, y, c, s = _place()
        cps = []
        for t in range(S4):
            cp = pltpu.make_async_remote_copy(g_ref.at[t, 1 - c], out_ref.at[t], send.at[t], recv.at[t],
                                              device_id=(x, y, 1 - c), device_id_type=MESH)
            cp.start()
            cps.append(cp)
        for cp in cps:
            cp.wait()

    return pl.pallas_call(
        body, name=name, in_specs=[ANY], out_specs=ANY, out_shape=jax.ShapeDtypeStruct((S4, R2, Cc), G.dtype),
        scratch_shapes=[pltpu.SemaphoreType.DMA((S4,)), pltpu.SemaphoreType.DMA((S4,))],
    )(G)


def _add_half(G, A, cidx, name):
    S4, two, R2, Cc = G.shape
    tb = _tile(R2, 256, 2 * HALO)
    cb = _tile(Cc, 512, LANES) if Cc % LANES == 0 else Cc
    nI, nJ = R2 // tb, Cc // cb

    def body(c_ref, g_ref, a_ref, o_ref):
        o_ref[...] = (g_ref[0, 0] + a_ref[0]).astype(BF16)

    grid_spec = pltpu.PrefetchScalarGridSpec(
        num_scalar_prefetch=1, grid=(S4, nI, nJ),
        in_specs=[pl.BlockSpec((1, 1, tb, cb), lambda t, i, j, c_ref: (t, c_ref[0], i, j)),
                  pl.BlockSpec((1, tb, cb), lambda t, i, j, c_ref: (t, i, j))],
        out_specs=pl.BlockSpec((tb, cb), lambda t, i, j, c_ref: (t * nI + i, j)))
    return pl.pallas_call(body, name=name, grid_spec=grid_spec, out_shape=jax.ShapeDtypeStruct((S4 * R2, Cc), BF16),
                          compiler_params=pltpu.CompilerParams(dimension_semantics=("parallel", "parallel", "parallel")),
                          )(cidx, G, A)


def _add_own(S1, B, chip_idx, name):
    S4, R2, Cc = S1.shape
    tb = _tile(R2, 256, 2 * HALO)
    cb = _tile(Cc, 512, LANES) if Cc % LANES == 0 else Cc

    def body(s_idx, s_ref, b_ref, o_ref):
        o_ref[...] = ((s_ref[0].astype(F32) + b_ref[0].astype(F32)) + b_ref[1].astype(F32)) + b_ref[2].astype(F32)

    grid_spec = pltpu.PrefetchScalarGridSpec(
        num_scalar_prefetch=1, grid=(R2 // tb, Cc // cb),
        in_specs=[pl.BlockSpec((1, tb, cb), lambda i, j, s_idx: (s_idx[0], i, j)),
                  pl.BlockSpec((3, tb, cb), lambda i, j, s_idx: (0, i, j))],
        out_specs=pl.BlockSpec((tb, cb), lambda i, j, s_idx: (i, j)))
    return pl.pallas_call(body, name=name, grid_spec=grid_spec, out_shape=jax.ShapeDtypeStruct((R2, Cc), F32),
                          compiler_params=pltpu.CompilerParams(dimension_semantics=("parallel", "parallel")))(chip_idx, S1, B)


def _sibling_send(Hs, name):
    R2, Cc = Hs.shape

    def body(h_ref, out_ref, send, recv):
        x, y, c, s = _place()
        cp = pltpu.make_async_remote_copy(h_ref, out_ref, send, recv, device_id=(x, y, 1 - c), device_id_type=MESH)
        cp.start()
        cp.wait()

    return pl.pallas_call(
        body, name=name, in_specs=[ANY], out_specs=ANY, out_shape=jax.ShapeDtypeStruct((R2, Cc), Hs.dtype),
        scratch_shapes=[pltpu.SemaphoreType.DMA, pltpu.SemaphoreType.DMA],
    )(Hs)


def _gather_all(buf, name):
    R, Cc = buf.shape

    def body(b_ref, out_ref, send, recv, local):
        x, y, c, s = _place()
        d = 2 * s + c
        mine = pltpu.make_async_copy(b_ref, out_ref.at[d], local)
        mine.start()
        cps = []
        for m in range(1, 8):
            t = d ^ m
            cp = pltpu.make_async_remote_copy(b_ref, out_ref.at[d], send.at[m - 1], recv.at[m - 1],
                                              device_id=(t // 4, (t // 2) % 2, t % 2), device_id_type=MESH)
            cp.start()
            cps.append(cp)
        for cp in cps:
            cp.wait()
        mine.wait()

    return pl.pallas_call(
        body, name=name, in_specs=[ANY], out_specs=ANY, out_shape=jax.ShapeDtypeStruct((8, R, Cc), buf.dtype),
        scratch_shapes=[pltpu.SemaphoreType.DMA((7,)), pltpu.SemaphoreType.DMA((7,)), pltpu.SemaphoreType.DMA],
    )(buf)


def _chip_sums(G, cidx, name):
    S4, R, Cc = G.shape
    G4 = G.reshape(S4, 2, R // 2, Cc)
    A = _sibling_swap(G4, name + "_swap")
    return _add_half(G4, A, cidx, name + "_add").reshape(S4, R // 2, Cc)


def _finish_shard(S1, B, cidx, chip_idx, name):
    Hs = _add_own(S1, B, chip_idx, name + "_sum")
    Ho = _sibling_send(Hs, name + "_gather")
    lo = jnp.where(cidx[0] == 0, Hs, Ho)
    hi = jnp.where(cidx[0] == 0, Ho, Hs)
    return jnp.concatenate([lo, hi], axis=0)


def _pack_rows(vs):
    flat = jnp.concatenate([v.reshape(-1) for v in vs])
    n = flat.shape[0]
    rows = -(-n // (LANES * 2 * HALO)) * 2 * HALO
    return jnp.pad(flat, (0, rows * LANES - n)).reshape(rows, LANES)


def _unpack_rows(buf, shapes):
    flat = buf.reshape(-1)
    outs, o = [], 0
    for shp in shapes:
        n = 1
        for d in shp:
            n *= d
        outs.append(flat[o:o + n].reshape(shp))
        o += n
    return outs


def kernel(x, p, norm_mix_g, w_in, conv_a_w, conv_qkv_w, a_log, dt_bias, dn_norm_g, w_out, norm_ffn_g, w_up, conv_ffn_w, w_down, norm_ple_g, w_ple_gate, w_ple_proj, final_norm_g, loss_target, m_norm_mix_g, m_w_in, m_conv_a_w, m_conv_qkv_w, m_a_log, m_dt_bias, m_dn_norm_g, m_w_out, m_norm_ffn_g, m_w_up, m_conv_ffn_w, m_w_down, m_norm_ple_g, m_w_ple_gate, m_w_ple_proj, m_final_norm_g, v_norm_mix_g, v_w_in, v_conv_a_w, v_conv_qkv_w, v_a_log, v_dt_bias, v_dn_norm_g, v_w_out, v_norm_ffn_g, v_w_up, v_conv_ffn_w, v_w_down, v_norm_ple_g, v_w_ple_gate, v_w_ple_proj, v_final_norm_g):
    xs = x[0]
    ps = p[0, 0]
    tgt = loss_target[0]
    T, D = xs.shape
    H = a_log.shape[-1]
    DNW = H * HEAD_DIM
    CW = conv_a_w.shape[-1] * 4
    F = w_down.shape[1] * 4
    PD = ps.shape[-1]
    IN_MAIN = 3 * CW + 4 * DNW
    IN_COLS = IN_MAIN + 2 * H
    assert w_in.shape[-1] * 4 == IN_COLS and CW + DNW == D and 2 * H <= LANES
    cb = _tile(min(CW, DNW), 512, LANES)
    while F % cb:
        cb -= LANES
    cidx = lax.axis_index("c").astype(jnp.int32).reshape(1)
    chip = 2 * lax.axis_index("x") + lax.axis_index("y")

    def halves(w):
        sh = w[0].astype(BF16)
        return sh.reshape(2, sh.shape[0] // 2, sh.shape[1])

    def whole(land):
        return land.reshape(4, 2 * land.shape[2], land.shape[3])

    def cols(g4):
        return jnp.transpose(g4, (1, 0, 2)).reshape(g4.shape[1], 4 * g4.shape[2])

    def rows(g4):
        return g4.reshape(4 * g4.shape[1], g4.shape[2])

    conv_shapes = [conv_a_w[0].shape, conv_qkv_w[0].shape, conv_ffn_w[0].shape]
    cpack = _pack_rows([conv_a_w[0], conv_qkv_w[0], conv_ffn_w[0]])
    sh_in, sh_out, sh_up, sh_down, sh_pg, sh_pp = (halves(w) for w in (w_in, w_out, w_up, w_down, w_ple_gate, w_ple_proj))
    l_in, cg = _run_comm(_merge(_ag_comm(sh_in), _ag_comm(cpack.reshape(2, cpack.shape[0] // 2, LANES))), "ag_w_in_conv")
    w_in_f = cols(whole(l_in))
    w_in_main = w_in_f[:, :IN_MAIN]
    w_in_small = jnp.pad(w_in_f[:, IN_MAIN:], ((0, 0), (0, LANES - 2 * H)))
    cg = cg.reshape(4, cpack.shape[0], LANES)
    parts = [_unpack_rows(cg[t], conv_shapes) for t in range(4)]
    cw_a = jnp.concatenate([parts[t][0] for t in range(4)], axis=1)
    cw_qkv = jnp.concatenate([parts[t][1] for t in range(4)], axis=1)
    cw_ffn = jnp.concatenate([parts[t][2] for t in range(4)], axis=1)
    cw_q, cw_k, cw_v = cw_qkv[:, :DNW], cw_qkv[:, DNW:2 * DNW], cw_qkv[:, 2 * DNW:]
    cw_fg, cw_fv = cw_ffn[:, :F], cw_ffn[:, F:]
    pad_row = lambda v: jnp.pad(v, ((0, 0), (0, LANES - v.shape[1])))
    a_log_row, dt_row = pad_row(a_log), pad_row(dt_bias)
    gdn_t = jnp.tile(dn_norm_g, (1, H))
    gfin = final_norm_g.reshape(1, D)

    h1 = _rms_fwd(xs, norm_mix_g, "rms1")
    proj, (l_up,) = _mm(h1, w_in_main, mode="nn", out_dtypes=[F32], name="mm_proj", comm=_ag_comm(sh_up, q=0, nq=2))
    small = _mm(h1, w_in_small, mode="nn", out_dtypes=[F32], name="mm_small")
    ya = _ga_fwd(proj, cw_a, CW, cb)
    nq = 3 * CW // cb
    nd = DNW // cb
    qn = _qkv_fwd(proj, cw_q, nq, True, DNW, cb, "q_fwd")
    kn = _qkv_fwd(proj, cw_k, nq + nd, True, DNW, cb, "k_fwd")
    vs = _qkv_fwd(proj, cw_v, nq + 2 * nd, False, DNW, cb, "v_fwd")
    g, beta = _gb_fwd(small, a_log_row, dt_row, H)
    o, S0, (l_up, l_out) = _delta_fwd(qn, kn, vs, g, beta, comm=_merge(_ag_comm(sh_up, l_up, q=1, nq=2), _ag_comm(sh_out)))
    w_out_f = rows(whole(l_out))
    w_out_a, w_out_b = w_out_f[:CW], w_out_f[CW:]
    w_up_f = cols(whole(l_up))
    w_up_g, w_up_v = w_up_f[:, :F], w_up_f[:, F:]
    z_coff = (3 * CW + 3 * DNW) // DNW
    assert (3 * CW + 3 * DNW) % DNW == 0 and CW % DNW == 0
    yb = _gnorm_fwd(o, proj, z_coff, gdn_t, DNW)
    add = lambda acc, r: (r + acc,)
    x1 = _mm(ya, w_out_a, mode="nn", out_dtypes=[F32], epi=add, extras=[xs], name="mm_out_a")
    x1 = _mm(yb, w_out_b, mode="nn", out_dtypes=[F32], epi=add, extras=[x1], name="mm_out_b")
    h2 = _rms_fwd(x1, norm_ffn_g, "rms2")
    up_g, (l_down,) = _mm(h2, w_up_g, mode="nn", out_dtypes=[F32], name="mm_up_g", comm=_ag_comm(sh_down, q=0, nq=2))
    up_v, (l_down,) = _mm(h2, w_up_v, mode="nn", out_dtypes=[F32], name="mm_up_v", comm=_ag_comm(sh_down, l_down, q=1, nq=2))
    w_down_f = rows(whole(l_down))
    act = _ffn_fwd(up_g, up_v, cw_fg, cw_fv, cb)
    x2, (l_pg, l_pp) = _mm(act, w_down_f, mode="nn", out_dtypes=[F32], epi=add, extras=[x1], name="mm_down",
                           comm=_merge(_ag_comm(sh_pg), _ag_comm(sh_pp)))
    w_pg_f = rows(whole(l_pg))
    w_pp_f = cols(whole(l_pp))
    h3 = _rms_fwd(x2, norm_ple_g, "rms3")
    pp = _mm(ps, w_pp_f, mode="nn", out_dtypes=[F32], name="mm_pp")

    def ple_epi(acc, x2v, ppv):
        pg = _sigmoid(acc)
        return x2v + pg * ppv, pg

    x3, pg = _mm(h3, w_pg_f, mode="nn", out_dtypes=[F32, F32], epi=ple_epi, extras=[x2, pp], name="mm_pg")

    dx3, fin = _final_fb(x3, tgt, gfin)
    loss = lax.psum(jnp.sum(fin[1]), ("x", "y", "c"))
    d_gfin = fin[0:1]
    dpg, dpp = _ple_bwd(dx3, pp, pg)
    def split_cols(dW):
        R, C4 = dW.shape
        return jnp.transpose(dW.reshape(R, 4, C4 // 4), (1, 0, 2))

    def split_rows(dW):
        return dW.reshape(4, dW.shape[0] // 4, dW.shape[1])

    dW_pp = _mm(ps, dpp, mode="tn", out_dtypes=[F32], name="mm_dw_pp")
    dW_pg = _mm(h3, dpg, mode="tn", out_dtypes=[F32], name="mm_dw_pg")
    S_pp = _chip_sums(split_cols(dW_pp), cidx, "rs_w_pp")
    S_pg = _chip_sums(split_rows(dW_pg), cidx, "rs_w_pg")
    dh3 = _mm(dpg, w_pg_f, mode="nt", out_dtypes=[F32], name="mm_dh3")
    dx2, dx2_b, d_gple = _rms_bwd(dh3, x2, norm_ple_g, dx3, "rms3_bwd")
    dW_down, (B_pp, B_pg) = _mm(act, dx2_b, mode="tn", out_dtypes=[F32], name="mm_dw_down",
                                comm=_merge(_a2a_comm(S_pp), _a2a_comm(S_pg)))
    S_down = _chip_sums(split_rows(dW_down), cidx, "rs_w_down")
    dact, (B_down,) = _mm(dx2_b, w_down_f, mode="nt", out_dtypes=[F32], name="mm_dact", comm=_a2a_comm(S_down))
    dup_g, dup_v, dcw_fg, dcw_fv = _ffn_bwd(dact, up_g, up_v, cw_fg, cw_fv, cb)
    dW_up_g = _mm(h2, dup_g, mode="tn", out_dtypes=[F32], name="mm_dw_up_g")
    dW_up_v = _mm(h2, dup_v, mode="tn", out_dtypes=[F32], name="mm_dw_up_v")
    S_up = _chip_sums(split_cols(jnp.concatenate([dW_up_g, dW_up_v], axis=1)), cidx, "rs_w_up")
    dh2, (B_up,) = _mm(dup_g, w_up_g, mode="nt", out_dtypes=[F32], name="mm_dh2_g", comm=_a2a_comm(S_up, 0, 2))
    dh2, (B_up,) = _mm(dup_v, w_up_v, mode="nt", out_dtypes=[F32], epi=add, extras=[dh2], name="mm_dh2_v",
                       comm=_a2a_comm(S_up, 1, 2, B_up))
    dx1, dx1_b, d_gffn = _rms_bwd(dh2, x1, norm_ffn_g, dx2, "rms2_bwd")
    dW_out_a = _mm(ya, dx1_b, mode="tn", out_dtypes=[F32], name="mm_dw_out_a")
    dW_out_b = _mm(yb, dx1_b, mode="tn", out_dtypes=[F32], name="mm_dw_out_b")
    S_out = _chip_sums(split_rows(jnp.concatenate([dW_out_a, dW_out_b], axis=0)), cidx, "rs_w_out")
    dymix, (B_out,) = _mm(dx1_b, w_out_f, mode="nt", out_dtypes=[F32], name="mm_dymix", comm=_a2a_comm(S_out))
    dax, dab, dac, dcw_a = _ga_bwd(dymix, proj, cw_a, CW, cb)
    do, dz, d_gdn = _gnorm_bwd(dymix, CW // DNW, o, proj, z_coff, gdn_t, DNW)
    dqn, dkn, dvs, dgB, dbB = _delta_bwd(qn, kn, vs, g, beta, S0, do)
    dq_pre, dcw_q = _qkv_bwd(dqn, proj, cw_q, nq, True, DNW, cb, "q_bwd")
    dk_pre, dcw_k = _qkv_bwd(dkn, proj, cw_k, nq + nd, True, DNW, cb, "k_bwd")
    dv_pre, dcw_v = _qkv_bwd(dvs, proj, cw_v, nq + 2 * nd, False, DNW, cb, "v_bwd")
    dsmall, d_ab = _gb_bwd(dgB, dbB, small, g, beta, a_log_row, dt_row, H)
    dproj = jnp.concatenate([dax, dab, dac, dq_pre, dk_pre, dv_pre, dz], axis=1)
    dW_in_main = _mm(h1, dproj, mode="tn", out_dtypes=[F32], name="mm_dw_in")
    dW_in_small = _mm(h1, dsmall, mode="tn", out_dtypes=[F32], name="mm_dw_in_small")
    S_in = _chip_sums(split_cols(jnp.concatenate([dW_in_main, dW_in_small[:, :2 * H]], axis=1)), cidx, "rs_w_in")
    dh1, (B_in,) = _mm(dproj, w_in_main, mode="nt", out_dtypes=[F32], name="mm_dh1", comm=_a2a_comm(S_in))
    dh1 = _mm(dsmall, w_in_small, mode="nt", out_dtypes=[F32], epi=add, extras=[dh1], name="mm_dh1_small")
    dx, _, d_gmix = _rms_bwd(dh1, xs, norm_mix_g, dx1, "rms1_bwd")

    chip_idx = chip.astype(jnp.int32).reshape(1)

    def update(S1, B, w, m, v, name):
        gr = _finish_shard(S1, B, cidx, chip_idx, "rs_" + name)
        delta, m2, v2 = _adamw(w[0], gr, m[0], v[0], "adamw_" + name)
        return gr[None], delta[None], m2[None], v2[None]

    big = {
        "w_in": update(S_in, B_in, w_in, m_w_in, v_w_in, "w_in"),
        "w_out": update(S_out, B_out, w_out, m_w_out, v_w_out, "w_out"),
        "w_up": update(S_up, B_up, w_up, m_w_up, v_w_up, "w_up"),
        "w_down": update(S_down, B_down, w_down, m_w_down, v_w_down, "w_down"),
        "w_ple_gate": update(S_pg, B_pg, w_ple_gate, m_w_ple_gate, v_w_ple_gate, "w_pg"),
        "w_ple_proj": update(S_pp, B_pp, w_ple_proj, m_w_ple_proj, v_w_ple_proj, "w_pp"),
    }

    small_grads = [d_gmix[0:1], dcw_a[:cw_a.shape[0]], jnp.concatenate([dcw_q, dcw_k, dcw_v], axis=1)[:cw_qkv.shape[0]],
                   d_ab[0:1, :H], d_ab[1:2, :H], d_gdn[0:1], d_gffn[0:1],
                   jnp.concatenate([dcw_fg, dcw_fv], axis=1)[:cw_ffn.shape[0]], d_gple[0:1], d_gfin]
    small_shapes = [v.shape for v in small_grads]
    gpack = _pack_rows(small_grads)
    gsum = _sum_stack(_gather_all(gpack, "ag_small"), "sum_small")
    (g_gmix, g_cwa, g_cwqkv, g_alog, g_dt, g_gdn, g_gffn, g_cwffn, g_gple, g_gfin) = _unpack_rows(gsum, small_shapes)

    def my_cols(v):
        Cc = v.shape[1] // 4
        return lax.dynamic_slice_in_dim(v, chip * Cc, Cc, axis=1)

    g_small = [g_gmix, my_cols(g_cwa), my_cols(g_cwqkv), g_alog, g_dt, g_gdn, g_gffn, my_cols(g_cwffn), g_gple, g_gfin]
    w_small = [norm_mix_g, conv_a_w[0], conv_qkv_w[0], a_log, dt_bias, dn_norm_g, norm_ffn_g, conv_ffn_w[0], norm_ple_g, gfin]
    m_small = [m_norm_mix_g, m_conv_a_w[0], m_conv_qkv_w[0], m_a_log, m_dt_bias, m_dn_norm_g, m_norm_ffn_g, m_conv_ffn_w[0],
               m_norm_ple_g, m_final_norm_g.reshape(1, D)]
    v_small = [v_norm_mix_g, v_conv_a_w[0], v_conv_qkv_w[0], v_a_log, v_dt_bias, v_dn_norm_g, v_norm_ffn_g, v_conv_ffn_w[0],
               v_norm_ple_g, v_final_norm_g.reshape(1, D)]
    shp = [v.shape for v in w_small]
    ds_, ms_, vs_ = _adamw(_pack_rows(w_small), _pack_rows(g_small), _pack_rows(m_small), _pack_rows(v_small), "adamw_small")
    out_shapes = [norm_mix_g.shape, conv_a_w.shape, conv_qkv_w.shape, a_log.shape, dt_bias.shape, dn_norm_g.shape,
                  norm_ffn_g.shape, conv_ffn_w.shape, norm_ple_g.shape, final_norm_g.shape]
    rs = lambda vals: [v.reshape(s) for v, s in zip(vals, out_shapes)]
    sg, sd_, sm_, sv_ = rs(g_small), rs(_unpack_rows(ds_, shp)), rs(_unpack_rows(ms_, shp)), rs(_unpack_rows(vs_, shp))
    names_small = ["norm_mix_g", "conv_a_w", "conv_qkv_w", "a_log", "dt_bias", "dn_norm_g", "norm_ffn_g", "conv_ffn_w",
                   "norm_ple_g", "final_norm_g"]
    res = {n: (sg[i], sd_[i], sm_[i], sv_[i]) for i, n in enumerate(names_small)}
    res.update(big)
    order = ["norm_mix_g", "w_in", "conv_a_w", "conv_qkv_w", "a_log", "dt_bias", "dn_norm_g", "w_out", "norm_ffn_g", "w_up",
             "conv_ffn_w", "w_down", "norm_ple_g", "w_ple_gate", "w_ple_proj", "final_norm_g"]
    return (loss, dx[None], *[res[n][0] for n in order], *[res[n][1] for n in order], *[res[n][2] for n in order],
            *[res[n][3] for n in order])
```

```python
import functools

import jax
import jax.numpy as jnp
from jax import lax
from jax.experimental import pallas as pl
from jax.experimental.pallas import tpu as pltpu

F32 = jnp.float32
BF16 = jnp.bfloat16
LANES = 128
HALO = 8
HEAD_DIM = 128
CHUNK = 64
EPS = 1e-6
VMEM_LIMIT = 56 * 1024 * 1024
MM_VMEM_BUDGET = 40 * 1024 * 1024
EW_VMEM_BUDGET = 24 * 1024 * 1024
MESH = pl.DeviceIdType.MESH

ADAM_LR, ADAM_B1, ADAM_B2, ADAM_EPS, ADAM_WD, ADAM_STEP = 0.001, 0.9, 0.999, 1e-08, 0.01, 10


def _tile(n, cap, unit):
    if n <= cap:
        return n
    d = (cap // unit) * unit
    while d >= unit:
        if n % d == 0:
            return d
        d -= unit
    raise ValueError(f"no tile for {n} (cap {cap}, unit {unit})")


def _sigmoid(x):
    return 1.0 / (1.0 + jnp.exp(-x))


def _mm_tiles(M, K, n_unit, k_unit, a_bytes, n_blocks_mn):
    tm, tk = _tile(M, 1024, LANES), _tile(k_unit, 2048, LANES)
    for cap in (1536, 1024, 512, 256, 128):
        tn = _tile(n_unit, cap, LANES)
        acc = 4 * tm * tn if K // tk > 1 else 0
        if 2 * tm * tk * a_bytes + 2 * tk * tn * 2 + 2 * 4 * tm * tn * n_blocks_mn + acc <= MM_VMEM_BUDGET:
            break
    return tm, tn, tk


def _mm(a, b, *, mode, out_dtypes, name, epi=None, extras=(), comm=None, b_split=None, out_split=None):
    if b_split is not None:
        lo, ns = b_split
        Rb, Cb = b.shape[1], b.shape[2]
    if mode == "nn":
        (M, K), N = a.shape, (ns * Cb if b_split else b.shape[1])
    elif mode == "nt":
        (M, K), N = a.shape, (Rb if b_split else b.shape[0])
    else:
        (K, M), N = a.shape, b.shape[1]
    n_ex, n_out = len(extras), len(out_dtypes)
    n_unit = Cb if (b_split and mode == "nn") else (N // out_split if out_split else N)
    k_unit = Cb if (b_split and mode == "nt") else K
    tm, tn, tk = _mm_tiles(M, K, n_unit, k_unit, a.dtype.itemsize, n_ex + n_out)
    nk = K // tk
    a_spec = pl.BlockSpec((tk, tm), lambda i, j, k: (k, i)) if mode == "tn" else pl.BlockSpec((tm, tk), lambda i, j, k: (i, k))
    if b_split and mode == "nn":
        nb = Cb // tn
        b_spec = pl.BlockSpec((None, tk, tn), lambda i, j, k: (lo + j // nb, k, j % nb))
    elif b_split:
        nb = Cb // tk
        b_spec = pl.BlockSpec((None, tn, tk), lambda i, j, k: (lo + k // nb, j, k % nb))
    else:
        b_spec = pl.BlockSpec((tn, tk), lambda i, j, k: (j, k)) if mode == "nt" else pl.BlockSpec((tk, tn), lambda i, j, k: (k, j))
    mn_spec = pl.BlockSpec((tm, tn), lambda i, j, k: (i, j))
    out_shapes = [jax.ShapeDtypeStruct((M, N), dt) for dt in out_dtypes]
    out_specs = [mn_spec] * n_out
    if out_split:
        assert n_ex == 0 and n_out == 1
        nbo = (N // out_split) // tn
        out_specs = [pl.BlockSpec((None, tm, tn), lambda i, j, k: (j // nbo, i, j % nbo))]
        out_shapes = [jax.ShapeDtypeStruct((out_split, M, N // out_split), out_dtypes[0])]
    dims = {"nn": (((1,), (0,)), ((), ())), "nt": (((1,), (1,)), ((), ())), "tn": (((0,), (0,)), ((), ()))}[mode]

    def body(*refs):
        a_ref, b_ref = refs[0], refs[1]
        ex_refs = refs[2:2 + n_ex]
        out_refs = refs[2 + n_ex:2 + n_ex + n_out]
        part = lax.dot_general(a_ref[...].astype(BF16), b_ref[...].astype(BF16), dims, preferred_element_type=F32)

        def finish(acc):
            outs = (acc,) if epi is None else epi(acc, *[r[...] for r in ex_refs])
            for r, o in zip(out_refs, outs):
                r[...] = o.astype(r.dtype)

        if nk == 1:
            finish(part)
            return
        acc_ref = refs[-1]
        k = pl.program_id(2)

        @pl.when(k == 0)
        def _():
            acc_ref[...] = part

        @pl.when(jnp.logical_and(k > 0, k < nk - 1))
        def _():
            acc_ref[...] += part

        @pl.when(k == nk - 1)
        def _():
            finish(acc_ref[...] + part)

    outs, comm_outs = _call(
        body, name=name, grid=(M // tm, N // tn, nk),
        in_specs=[a_spec, b_spec] + [mn_spec] * n_ex,
        out_specs=out_specs,
        out_shape=out_shapes,
        scratch_shapes=[pltpu.VMEM((tm, tn), F32)] if nk > 1 else [],
        semantics=("parallel", "parallel", "arbitrary"), args=(a, b, *extras), comm=comm)
    res = outs[0] if n_out == 1 else outs
    return res if comm is None else (res, comm_outs)


def _tiled(fn, *, T, C, ins, out_dtypes=(), acc_rows=(), tb=256, cb=512, name):
    tb = _tile(T, tb, HALO)
    nI, nJ = T // tb, C // cb
    hb, nH = tb // HALO, T // HALO
    specs, args, kinds = [], [], []
    for kind, arr, cmap in ins:
        cm = cmap if cmap is not None else (lambda j: j)
        kinds.append(kind)
        if kind == "cur":
            specs.append(pl.BlockSpec((tb, cb), lambda j, i, cm=cm: (i, cm(j))))
            args.append(arr)
        elif kind == "ext":
            specs.append(pl.BlockSpec((HALO, cb), lambda j, i, cm=cm: (jnp.maximum(i * hb - 1, 0), cm(j))))
            specs.append(pl.BlockSpec((tb, cb), lambda j, i, cm=cm: (i, cm(j))))
            specs.append(pl.BlockSpec((HALO, cb), lambda j, i, cm=cm: (jnp.minimum((i + 1) * hb, nH - 1), cm(j))))
            args += [arr, arr, arr]
        elif kind == "row":
            specs.append(pl.BlockSpec((arr.shape[0], cb), lambda j, i, cm=cm: (0, cm(j))))
            args.append(arr)
        elif kind == "stack":
            specs.append(pl.BlockSpec((arr.shape[0], tb, cb), lambda j, i, cm=cm: (0, i, cm(j))))
            args.append(arr)
        else:
            raise ValueError(kind)
    n_in = len(args)
    n_out, n_acc = len(out_dtypes), len(acc_rows)

    def body(*refs):
        j, i = pl.program_id(0), pl.program_id(1)
        vals, r = [], 0
        for kind in kinds:
            if kind == "ext":
                prev = jnp.where(i == 0, 0.0, refs[r][...].astype(F32))
                cur = refs[r + 1][...].astype(F32)
                nxt = jnp.where(i == nI - 1, 0.0, refs[r + 2][...].astype(F32))
                vals.append(jnp.concatenate([prev, cur, nxt], axis=0))
                r += 3
            else:
                vals.append(refs[r][...])
                r += 1
        res = fn(j, i, *vals)
        for ref, o in zip(refs[n_in:n_in + n_out], res[:n_out]):
            ref[...] = o.astype(ref.dtype)
        for ref, o in zip(refs[n_in + n_out:], res[n_out:]):
            @pl.when(i == 0)
            def _(ref=ref, o=o):
                ref[...] = o

            @pl.when(i > 0)
            def _(ref=ref, o=o):
                ref[...] += o

    outs = pl.pallas_call(
        body, name=name, grid=(nJ, nI), in_specs=specs,
        out_specs=[pl.BlockSpec((tb, cb), lambda j, i: (i, j))] * n_out
        + [pl.BlockSpec((rows, cb), lambda j, i: (0, j)) for rows in acc_rows],
        out_shape=[jax.ShapeDtypeStruct((T, C), dt) for dt in out_dtypes]
        + [jax.ShapeDtypeStruct((rows, C), F32) for rows in acc_rows],
        compiler_params=pltpu.CompilerParams(dimension_semantics=("parallel", "arbitrary"),
                                             vmem_limit_bytes=VMEM_LIMIT),
    )(*args)
    return outs


def _conv_causal(xe, w):
    K = w.shape[0]
    y = xe * w[K - 1:K]
    for j in range(K - 1):
        y = y + pltpu.roll(xe, K - 1 - j, 0) * w[j:j + 1]
    return y


def _conv_anti(de, w):
    K, n = w.shape[0], de.shape[0]
    y = de * w[K - 1:K]
    for j in range(K - 1):
        y = y + pltpu.roll(de, n - (K - 1 - j), 0) * w[j:j + 1]
    return y


def _conv_dw(dce, xe, K):
    n = dce.shape[0]
    tb = n - 2 * HALO
    rows = []
    for j in range(K):
        xs = xe if j == K - 1 else pltpu.roll(xe, K - 1 - j, 0)
        rows.append(jnp.sum((dce * xs)[HALO:HALO + tb], axis=0, keepdims=True))
    rows.append(jnp.zeros((HALO - K, dce.shape[1]), F32))
    return jnp.concatenate(rows, axis=0)


def _own(xe):
    return xe[HALO:xe.shape[0] - HALO]


def _row0(v):
    return jnp.concatenate([v, jnp.zeros((HALO - 1, v.shape[1]), F32)], axis=0)


def _per_head(fn, *xs):
    n = xs[0].shape[1] // HEAD_DIM
    outs = [fn(*[x[:, g * HEAD_DIM:(g + 1) * HEAD_DIM] for x in xs]) for g in range(n)]
    return outs[0] if n == 1 else jnp.concatenate(outs, axis=1)


def _rms_fwd(x, g, name):
    T, D = x.shape

    def fn(j, i, xv, gv):
        r = lax.rsqrt(jnp.mean(xv * xv, axis=1, keepdims=True) + EPS)
        return (xv * r * gv,)

    return _tiled(fn, T=T, C=D, ins=[("cur", x, None), ("row", g, None)], out_dtypes=[BF16], cb=D, name=name)[0]


def _rms_bwd_math(dy, xv, gv):
    r = lax.rsqrt(jnp.mean(xv * xv, axis=1, keepdims=True) + EPS)
    xh = xv * r
    dxh = dy * gv
    dx = r * (dxh - xh * jnp.mean(dxh * xh, axis=1, keepdims=True))
    dg = jnp.sum(dy * xh, axis=0, keepdims=True)
    return dx, dg


def _rms_bwd(dh, x, g, dres, name):
    T, D = x.shape

    def fn(j, i, dhv, xv, gv, dr):
        dx, dg = _rms_bwd_math(dhv, xv, gv)
        return dr + dx, dr + dx, _row0(dg)

    return _tiled(fn, T=T, C=D, ins=[("cur", dh, None), ("cur", x, None), ("row", g, None), ("cur", dres, None)],
                  out_dtypes=[F32, BF16], acc_rows=[HALO], cb=D, name=name)


def _final_fb(x3, tgt, g):
    T, D = x3.shape

    def fn(j, i, xv, tv, gv):
        r = lax.rsqrt(jnp.mean(xv * xv, axis=1, keepdims=True) + EPS)
        xh = xv * r
        e = xh * gv - tv
        dy = e * (1.0 / D)
        dxh = dy * gv
        dx = r * (dxh - xh * jnp.mean(dxh * xh, axis=1, keepdims=True))
        dg = jnp.sum(dy * xh, axis=0, keepdims=True)
        ls = jnp.sum(e * e, axis=0, keepdims=True) * (0.5 / D)
        return dx, jnp.concatenate([dg, ls, jnp.zeros((HALO - 2, D), F32)], axis=0)

    return _tiled(fn, T=T, C=D, ins=[("cur", x3, None), ("cur", tgt, None), ("row", g, None)],
                  out_dtypes=[F32], acc_rows=[HALO], cb=D, name="final_fb")


def _ga_fwd(proj, w_a, CW, cb):
    T = proj.shape[0]
    n = CW // cb

    def fn(j, i, ax, ab, ac, w):
        c = _conv_causal(ac * ax, w)
        return (ab * _own(c),)

    return _tiled(fn, T=T, C=CW, ins=[("ext", proj, None), ("cur", proj, lambda j: j + n), ("ext", proj, lambda j: j + 2 * n),
                                       ("row", w_a, None)], out_dtypes=[BF16], cb=cb, name="ga_fwd")[0]


def _ga_bwd(dymix, proj, w_a, CW, cb):
    T = proj.shape[0]
    n = CW // cb
    K = w_a.shape[0]

    def fn(j, i, dy, ax, ab, ac, w):
        u = ac * ax
        c = _conv_causal(u, w)
        dc = dy * ab
        du = _conv_anti(dc, w)
        return _own(du * ac), _own(dy * c), _own(du * ax), _conv_dw(dc, u, K)

    return _tiled(fn, T=T, C=CW, ins=[("ext", dymix, None), ("ext", proj, None), ("ext", proj, lambda j: j + n),
                                       ("ext", proj, lambda j: j + 2 * n), ("row", w_a, None)],
                  out_dtypes=[BF16, BF16, BF16], acc_rows=[HALO], cb=cb, name="ga_bwd")


def _l2n(s):
    return s * lax.rsqrt(jnp.sum(s * s, axis=1, keepdims=True) + EPS)


def _qkv_fwd(proj, w_sec, coff, normalize, DNW, cb, name):
    T = proj.shape[0]

    def fn(j, i, pre, w):
        c = _own(_conv_causal(pre, w))
        s = c * _sigmoid(c)
        return (_per_head(_l2n, s) if normalize else s,)

    return _tiled(fn, T=T, C=DNW, ins=[("ext", proj, lambda j: j + coff), ("row", w_sec, None)],
                  out_dtypes=[F32], cb=cb, name=name)[0]


def _qkv_bwd(dsec, proj, w_sec, coff, normalize, DNW, cb, name):
    T = proj.shape[0]
    K = w_sec.shape[0]

    def l2n_bwd(s, dn):
        r = lax.rsqrt(jnp.sum(s * s, axis=1, keepdims=True) + EPS)
        nrm = s * r
        return r * (dn - nrm * jnp.sum(dn * nrm, axis=1, keepdims=True))

    def fn(j, i, dn, pre, w):
        c = _conv_causal(pre, w)
        sg = _sigmoid(c)
        s = c * sg
        ds = _per_head(l2n_bwd, s, dn) if normalize else dn
        dc = ds * (sg * (1.0 + c * (1.0 - sg)))
        return _own(_conv_anti(dc, w)), _conv_dw(dc, pre, K)

    return _tiled(fn, T=T, C=DNW, ins=[("ext", dsec, None), ("ext", proj, lambda j: j + coff), ("row", w_sec, None)],
                  out_dtypes=[BF16], acc_rows=[HALO], cb=cb, name=name)


def _gb_fwd(small, a_log_row, dt_row, H):
    T = small.shape[0]

    def fn(j, i, sm, al, dt):
        z = sm + dt
        sp = jnp.maximum(z, 0.0) + jnp.log(1.0 + jnp.exp(-jnp.abs(z)))
        g = -jnp.exp(al) * sp
        beta = _sigmoid(pltpu.roll(sm, LANES - H, 1))
        return g, beta

    return _tiled(fn, T=T, C=LANES, ins=[("cur", small, None), ("row", a_log_row, None), ("row", dt_row, None)],
                  out_dtypes=[F32, F32], cb=LANES, name="gb_fwd")


def _gb_bwd(dgB, dbB, small, g, beta, a_log_row, dt_row, H):
    T = small.shape[0]

    def fn(j, i, dgv, dbv, sm, gv, bv, al, dt):
        lane = lax.broadcasted_iota(jnp.int32, sm.shape, 1)
        dg = jnp.zeros(sm.shape, F32)
        db = jnp.zeros(sm.shape, F32)
        for h in range(H):
            dg = jnp.where(lane == h, jnp.sum(dgv[h], axis=1, keepdims=True), dg)
            db = jnp.where(lane == h, jnp.sum(dbv[h], axis=1, keepdims=True), db)
        da = dg * (-jnp.exp(al)) * _sigmoid(sm + dt)
        dbb = db * bv * (1.0 - bv)
        dsm = jnp.where(lane < H, da, 0.0) + pltpu.roll(jnp.where(lane < H, dbb, 0.0), H, 1)
        d_alog = jnp.sum(jnp.where(lane < H, dg * gv, 0.0), axis=0, keepdims=True)
        d_dt = jnp.sum(jnp.where(lane < H, da, 0.0), axis=0, keepdims=True)
        return dsm, jnp.concatenate([d_alog, d_dt, jnp.zeros((HALO - 2, LANES), F32)], axis=0)

    return _tiled(fn, T=T, C=LANES, ins=[("stack", dgB, None), ("stack", dbB, None), ("cur", small, None), ("cur", g, None),
                                          ("cur", beta, None), ("row", a_log_row, None), ("row", dt_row, None)],
                  out_dtypes=[BF16], acc_rows=[HALO], cb=LANES, name="gb_bwd")


_DIMS = {"nn": (((1,), (0,)), ((), ())), "nt": (((1,), (1,)), ((), ())), "tn": (((0,), (0,)), ((), ()))}
_DOT_BWD = {"nn": (("nt", "gb"), ("tn", "ag")), "nt": (("nn", "gb"), ("tn", "ga")), "tn": (("nt", "bg"), ("nn", "ag"))}


def _split(a):
    hi = a.astype(BF16)
    return hi, (a - hi.astype(F32)).astype(BF16)


def _raw_dot(a, b, kind, passes):
    dg = lambda x, y: lax.dot_general(x, y, _DIMS[kind], preferred_element_type=F32)
    if passes == 1:
        return dg(a.astype(BF16), b.astype(BF16))
    ah, al = _split(a)
    bh, bl = _split(b)
    return dg(ah, bh) + (dg(ah, bl) + dg(al, bh))


@functools.lru_cache(maxsize=None)
def _dotf(kind, passes):
    @jax.custom_vjp
    def f(a, b):
        return _raw_dot(a, b, kind, passes)

    def fwd(a, b):
        return _raw_dot(a, b, kind, passes), (a, b)

    def bwd(res, g):
        ops = {"a": res[0], "b": res[1], "g": g}
        (ka, oa), (kb, ob) = _DOT_BWD[kind]
        return (_raw_dot(ops[oa[0]], ops[oa[1]], ka, passes), _raw_dot(ops[ob[0]], ops[ob[1]], kb, passes))

    f.defvjp(fwd, bwd)
    return f


def _chunk_fn(q, k, v, gB, bB, S):
    C = CHUNK
    d3, d3nt = _dotf("nn", 3), _dotf("nt", 3)
    d1, d1nt, d1tn = _dotf("nn", 1), _dotf("nt", 1), _dotf("tn", 1)
    each = lambda f, *ls: tuple(f(*xs) for xs in zip(*ls))
    row = lax.broadcasted_iota(jnp.int32, (C, C), 0)
    col = lax.broadcasted_iota(jnp.int32, (C, C), 1)
    causal = row >= col
    strict = row > col
    tril = jnp.where(causal, 1.0, 0.0).astype(F32)
    eye = jnp.where(row == col, 1.0, 0.0).astype(F32)
    avg = jnp.full((C, HEAD_DIM), 1.0 / HEAD_DIM, F32)
    gc = each(lambda g: d3(tril, g), gB)
    R = each(lambda g: d3nt(avg, g), gc)
    decay = each(lambda g, r: jnp.where(causal, jnp.exp(jnp.where(causal, g[:, :C] - r, 0.0)), 0.0), gc, R)
    kk = each(lambda x: d1nt(x, x), k)
    L = each(lambda a, d, b: jnp.where(strict, a * d * b[:, :C], 0.0), kk, decay, bB)
    inv = each(lambda l: eye - l, L)
    P = L
    for _ in range(5):
        P = each(lambda p: d3(p, p), P)
        inv = each(lambda a, p: d3(a, eye + p), inv, P)
    eg = each(jnp.exp, gc)
    u = each(lambda a, x, b: d3(a, x * b), inv, v, bB)
    w = each(lambda a, x, b, e: d3(a, x * b * e), inv, k, bB, eg)
    qs = each(lambda x: x * (HEAD_DIM ** -0.5), q)
    qk = each(lambda a, x, d: d1nt(a, x) * d, qs, k, decay)
    gl = each(lambda g: g[C - 1:C, :], gc)
    v_new = each(lambda a, b, s: a - d1(b, s), u, w, S)
    o1 = each(lambda a, e, s: d1(a * e, s), qs, eg, S)
    o = each(lambda a, b, c: a + d1(b, c), o1, qk, v_new)
    kv = each(lambda x, a, g, vn: d1tn(x * jnp.exp(a - g), vn), k, gl, gc, v_new)
    S_new = each(lambda s, a, b: s * jnp.exp(a) + b, S, gl, kv)
    return o, S_new


def _sel_lane(x, h):
    lane = lax.broadcasted_iota(jnp.int32, x.shape, 1)
    return jnp.broadcast_to(jnp.sum(jnp.where(lane == h, x, 0.0), axis=1, keepdims=True), x.shape)


def _head(ref, h):
    return ref[:, h * HEAD_DIM:(h + 1) * HEAD_DIM]


def _delta_fwd(q, k, v, g, beta, comm=None):
    T = q.shape[0]
    H, N = q.shape[1] // HEAD_DIM, T // CHUNK

    def body(q_ref, k_ref, v_ref, g_ref, b_ref, o_ref, s_ref, S):
        @pl.when(pl.program_id(0) == 0)
        def _():
            S[...] = jnp.zeros_like(S)

        gv, bv = g_ref[...], b_ref[...]
        heads = lambda f: tuple(f(h) for h in range(H))
        S_in = heads(lambda h: S[h])
        for h in range(H):
            s_ref[h, 0] = S_in[h]
        o, S_new = _chunk_fn(heads(lambda h: _head(q_ref, h)), heads(lambda h: _head(k_ref, h)), heads(lambda h: _head(v_ref, h)),
                             heads(lambda h: _sel_lane(gv, h)), heads(lambda h: _sel_lane(bv, h)), S_in)
        for h in range(H):
            o_ref[:, h * HEAD_DIM:(h + 1) * HEAD_DIM] = o[h]
            S[h] = S_new[h]

    blk = pl.BlockSpec((CHUNK, H * HEAD_DIM), lambda n: (n, 0))
    gblk = pl.BlockSpec((CHUNK, LANES), lambda n: (n, 0))
    outs, comm_outs = _call(
        body, name="delta_fwd", grid=(N,), in_specs=[blk, blk, blk, gblk, gblk],
        out_specs=[blk, pl.BlockSpec((H, 1, HEAD_DIM, HEAD_DIM), lambda n: (0, n, 0, 0))],
        out_shape=[jax.ShapeDtypeStruct((T, H * HEAD_DIM), F32), jax.ShapeDtypeStruct((H, N, HEAD_DIM, HEAD_DIM), F32)],
        scratch_shapes=[pltpu.VMEM((H, HEAD_DIM, HEAD_DIM), F32)],
        semantics=("arbitrary",), args=(q, k, v, g, beta), comm=comm)
    return outs[0], outs[1], comm_outs


def _delta_bwd(q, k, v, g, beta, S0, do):
    T = q.shape[0]
    H, N = q.shape[1] // HEAD_DIM, T // CHUNK

    def body(q_ref, k_ref, v_ref, g_ref, b_ref, s_ref, do_ref, dq_ref, dk_ref, dv_ref, dg_ref, db_ref, dS):
        @pl.when(pl.program_id(0) == 0)
        def _():
            dS[...] = jnp.zeros_like(dS)

        gv, bv = g_ref[...], b_ref[...]
        heads = lambda f: tuple(f(h) for h in range(H))
        _, vjp = jax.vjp(_chunk_fn, heads(lambda h: _head(q_ref, h)), heads(lambda h: _head(k_ref, h)),
                         heads(lambda h: _head(v_ref, h)), heads(lambda h: _sel_lane(gv, h)), heads(lambda h: _sel_lane(bv, h)),
                         heads(lambda h: s_ref[h, 0]))
        dq, dk, dv, dgB, dbB, dS_prev = vjp((heads(lambda h: _head(do_ref, h)), heads(lambda h: dS[h])))
        for h in range(H):
            sl = slice(h * HEAD_DIM, (h + 1) * HEAD_DIM)
            dq_ref[:, sl] = dq[h]
            dk_ref[:, sl] = dk[h]
            dv_ref[:, sl] = dv[h]
            dg_ref[h] = dgB[h]
            db_ref[h] = dbB[h]
            dS[h] = dS_prev[h]

    blk = pl.BlockSpec((CHUNK, H * HEAD_DIM), lambda n: (N - 1 - n, 0))
    gblk = pl.BlockSpec((CHUNK, LANES), lambda n: (N - 1 - n, 0))
    hblk = pl.BlockSpec((H, CHUNK, LANES), lambda n: (0, N - 1 - n, 0))
    sd = jax.ShapeDtypeStruct
    return pl.pallas_call(
        body, name="delta_bwd", grid=(N,),
        in_specs=[blk, blk, blk, gblk, gblk, pl.BlockSpec((H, 1, HEAD_DIM, HEAD_DIM), lambda n: (0, N - 1 - n, 0, 0)), blk],
        out_specs=[blk, blk, blk, hblk, hblk],
        out_shape=[sd((T, H * HEAD_DIM), F32)] * 3 + [sd((H, T, LANES), F32)] * 2,
        scratch_shapes=[pltpu.VMEM((H, HEAD_DIM, HEAD_DIM), F32)],
        compiler_params=pltpu.CompilerParams(dimension_semantics=("arbitrary",)),
    )(q, k, v, g, beta, S0, do)


def _gnorm_fwd(o, proj, z_coff, gdn_t, DNW):
    T = o.shape[0]

    def fn(j, i, ov, zv, gv):
        def one(oh, zh, gh):
            r = lax.rsqrt(jnp.mean(oh * oh, axis=1, keepdims=True) + EPS)
            return oh * r * gh * (zh * _sigmoid(zh))
        return (_per_head(one, ov, zv, jnp.broadcast_to(gv, ov.shape)),)

    return _tiled(fn, T=T, C=DNW, ins=[("cur", o, None), ("cur", proj, lambda j: j + z_coff), ("row", gdn_t, None)],
                  out_dtypes=[BF16], cb=DNW, name="gnorm_fwd")[0]


def _gnorm_bwd(dymix, y_coff, o, proj, z_coff, gdn_t, DNW):
    T = o.shape[0]
    nh = DNW // HEAD_DIM

    def fn(j, i, dy, ov, zv, gv):
        dos, dzs, dgs = [], [], jnp.zeros((1, HEAD_DIM), F32)
        for h in range(nh):
            sl = slice(h * HEAD_DIM, (h + 1) * HEAD_DIM)
            dyh, oh, zh, gh = dy[:, sl].astype(F32), ov[:, sl], zv[:, sl], gv[:, sl]
            r = lax.rsqrt(jnp.mean(oh * oh, axis=1, keepdims=True) + EPS)
            on = oh * r
            sg = _sigmoid(zh)
            sz = zh * sg
            dzs.append(dyh * on * gh * (sg * (1.0 + zh * (1.0 - sg))))
            don = dyh * gh * sz
            dos.append(r * (don - on * jnp.mean(don * on, axis=1, keepdims=True)))
            dgs = dgs + jnp.sum(dyh * on * sz, axis=0, keepdims=True)
        cat = (lambda xs: xs[0] if nh == 1 else jnp.concatenate(xs, axis=1))
        return cat(dos), cat(dzs), _row0(dgs)

    T_ = T
    nI = T_ // _tile(T_, 256, HALO)
    tb = T_ // nI
    specs_cb = DNW

    def body_wrap():
        def body(dy_ref, o_ref, z_ref, g_ref, do_ref, dz_ref, dg_ref):
            i = pl.program_id(0)
            d_o, d_z, d_g = fn(0, i, dy_ref[...], o_ref[...], z_ref[...], g_ref[...])
            do_ref[...] = d_o
            dz_ref[...] = d_z.astype(dz_ref.dtype)

            @pl.when(i == 0)
            def _():
                dg_ref[...] = d_g

            @pl.when(i > 0)
            def _():
                dg_ref[...] += d_g

        return pl.pallas_call(
            body, name="gnorm_bwd", grid=(nI,),
            in_specs=[pl.BlockSpec((tb, specs_cb), lambda i: (i, y_coff)), pl.BlockSpec((tb, specs_cb), lambda i: (i, 0)),
                      pl.BlockSpec((tb, specs_cb), lambda i: (i, z_coff)), pl.BlockSpec((1, specs_cb), lambda i: (0, 0))],
            out_specs=[pl.BlockSpec((tb, specs_cb), lambda i: (i, 0)), pl.BlockSpec((tb, specs_cb), lambda i: (i, 0)),
                       pl.BlockSpec((HALO, HEAD_DIM), lambda i: (0, 0))],
            out_shape=[jax.ShapeDtypeStruct((T_, DNW), F32), jax.ShapeDtypeStruct((T_, DNW), BF16),
                       jax.ShapeDtypeStruct((HALO, HEAD_DIM), F32)],
            compiler_params=pltpu.CompilerParams(dimension_semantics=("arbitrary",), vmem_limit_bytes=VMEM_LIMIT),
        )(dymix, o, proj, gdn_t)

    return body_wrap()


def _ffn_fwd(up_g, up_v, w_g, w_v, cb):
    T, F = up_g.shape

    def fn(j, i, ug, uv, wg, wv):
        cg = _own(_conv_causal(ug, wg))
        cv = _own(_conv_causal(uv, wv))
        return (cg * _sigmoid(cg) * cv,)

    return _tiled(fn, T=T, C=F, ins=[("ext", up_g, None), ("ext", up_v, None), ("row", w_g, None), ("row", w_v, None)],
                  out_dtypes=[BF16], cb=cb, name="ffn_fwd")[0]


def _ffn_bwd(dact, up_g, up_v, w_g, w_v, cb):
    T, F = up_g.shape
    K = w_g.shape[0]

    def fn(j, i, da, ug, uv, wg, wv):
        cg = _conv_causal(ug, wg)
        cv = _conv_causal(uv, wv)
        sg = _sigmoid(cg)
        dgate = da * cv * (sg * (1.0 + cg * (1.0 - sg)))
        dval = da * (cg * sg)
        return (_own(_conv_anti(dgate, wg)), _own(_conv_anti(dval, wv)), _conv_dw(dgate, ug, K), _conv_dw(dval, uv, K))

    return _tiled(fn, T=T, C=F, ins=[("ext", dact, None), ("ext", up_g, None), ("ext", up_v, None), ("row", w_g, None),
                                      ("row", w_v, None)], out_dtypes=[BF16, BF16], acc_rows=[HALO, HALO], cb=cb, name="ffn_bwd")


def _ple_bwd(dx3, pp, pg):
    T, D = dx3.shape

    def fn(j, i, d, ppv, pgv):
        return d * ppv * pgv * (1.0 - pgv), d * pgv

    return _tiled(fn, T=T, C=D, ins=[("cur", dx3, None), ("cur", pp, None), ("cur", pg, None)],
                  out_dtypes=[BF16, BF16], cb=_tile(D, 512, LANES), name="ple_bwd")


def _wide(R, Cc, n_f32, unit=HALO):
    cb = Cc if (Cc % LANES or Cc <= 4096) else _tile(Cc, 2048, LANES)
    cap = max(unit, EW_VMEM_BUDGET // (2 * 4 * n_f32 * cb) // unit * unit)
    return _tile(R, cap, unit), cb


def _adamw(w, g, m, v, name):
    R, Cc = w.shape
    tb, cb = _wide(R, Cc, 7)
    c1 = 1.0 / (1.0 - ADAM_B1 ** ADAM_STEP)
    c2 = 1.0 / (1.0 - ADAM_B2 ** ADAM_STEP)

    def fn(j, i, wv, gv, mv, vv):
        m2 = ADAM_B1 * mv + (1.0 - ADAM_B1) * gv
        v2 = ADAM_B2 * vv + (1.0 - ADAM_B2) * (gv * gv)
        delta = -ADAM_LR * ((m2 * c1) / (jnp.sqrt(v2 * c2) + ADAM_EPS) + ADAM_WD * wv)
        return delta, m2, v2

    return _tiled(fn, T=R, C=Cc, ins=[("cur", w, None), ("cur", g, None), ("cur", m, None), ("cur", v, None)],
                  out_dtypes=[F32, F32, F32], tb=tb, cb=cb, name=name)


def _sum_stack(st, name):
    S, R, Cc = st.shape
    cb = _tile(Cc, 512, LANES) if Cc % LANES == 0 else Cc

    def fn(j, i, sv):
        t = sv[0]
        for s in range(1, S):
            t = t + sv[s]
        return (t,)

    return _tiled(fn, T=R, C=Cc, ins=[("stack", st, None)], out_dtypes=[F32], cb=cb, name=name)[0]


ANY = pl.BlockSpec(memory_space=pl.ANY)


def _place():
    x, y, c = lax.axis_index("x"), lax.axis_index("y"), lax.axis_index("c")
    return x, y, c, 2 * x + y


def _chip_dev(s, c):
    return (s // 2, s % 2, c)


class _Comm:
    def __init__(self, ins, out_shapes, sems, start, wait, aliases=None):
        self.ins, self.out_shapes, self.sems = list(ins), list(out_shapes), list(sems)
        self.start, self.wait, self.aliases = start, wait, dict(aliases or {})


def _merge(*comms):
    offs, i, o, s = [], 0, 0, 0
    for cm in comms:
        offs.append((i, o, s))
        i, o, s = i + len(cm.ins), o + len(cm.out_shapes), s + len(cm.sems)

    def part(refs, k, cm):
        i0, o0, s0 = offs[k]
        return refs[0][i0:i0 + len(cm.ins)], refs[1][o0:o0 + len(cm.out_shapes)], refs[2][s0:s0 + len(cm.sems)]

    def start(*refs):
        for k, cm in enumerate(comms):
            cm.start(*part(refs, k, cm))

    def wait(*refs):
        for k, cm in enumerate(comms):
            cm.wait(*part(refs, k, cm))

    aliases = {}
    for k, cm in enumerate(comms):
        for a, b in cm.aliases.items():
            aliases[offs[k][0] + a] = offs[k][1] + b
    return _Comm([a for cm in comms for a in cm.ins], [a for cm in comms for a in cm.out_shapes],
                 [a for cm in comms for a in cm.sems], start, wait, aliases)


def _call(body, *, name, grid, in_specs, out_specs, out_shape, scratch_shapes, semantics, args, comm=None):
    if comm is None:
        outs = pl.pallas_call(
            body, name=name, grid=grid, in_specs=in_specs, out_specs=out_specs, out_shape=out_shape,
            scratch_shapes=list(scratch_shapes),
            compiler_params=pltpu.CompilerParams(dimension_semantics=semantics, vmem_limit_bytes=VMEM_LIMIT))(*args)
        return list(outs), []
    n_in, n_out, n_scr = len(in_specs), len(out_specs), len(scratch_shapes)
    ci, co = len(comm.ins), len(comm.out_shapes)

    def wrapped(*refs):
        r = 0
        ins, r = refs[r:r + n_in], r + n_in
        cins, r = refs[r:r + ci], r + ci
        outs, r = refs[r:r + n_out], r + n_out
        couts, r = refs[r:r + co], r + co
        scr, r = refs[r:r + n_scr], r + n_scr
        csems = refs[r:]
        ids = [pl.program_id(a) for a in range(len(grid))]
        first, last = ids[0] == 0, ids[0] == grid[0] - 1
        for a in range(1, len(grid)):
            first = jnp.logical_and(first, ids[a] == 0)
            last = jnp.logical_and(last, ids[a] == grid[a] - 1)

        @pl.when(first)
        def _():
            comm.start(cins, couts, csems)

        body(*ins, *outs, *scr)

        @pl.when(last)
        def _():
            comm.wait(cins, couts, csems)

    outs = pl.pallas_call(
        wrapped, name=name, grid=grid, in_specs=list(in_specs) + [ANY] * ci, out_specs=list(out_specs) + [ANY] * co,
        out_shape=list(out_shape) + comm.out_shapes, scratch_shapes=list(scratch_shapes) + comm.sems,
        input_output_aliases={n_in + a: n_out + b for a, b in comm.aliases.items()},
        compiler_params=pltpu.CompilerParams(dimension_semantics=("arbitrary",) * len(grid), vmem_limit_bytes=VMEM_LIMIT),
    )(*args, *comm.ins)
    return list(outs[:n_out]), list(outs[n_out:])


def _run_comm(comm, name):
    ci, co = len(comm.ins), len(comm.out_shapes)

    def body(*refs):
        cins, couts, csems = refs[:ci], refs[ci:ci + co], refs[ci + co:]
        comm.start(cins, couts, csems)
        comm.wait(cins, couts, csems)

    outs = pl.pallas_call(body, name=name, in_specs=[ANY] * ci, out_specs=[ANY] * co, out_shape=comm.out_shapes,
                          scratch_shapes=comm.sems, input_output_aliases=comm.aliases)(*comm.ins)
    return list(outs)


def _ag_comm(shard, land=None, q=0, nq=1):
    two, R2, Cc = shard.shape
    rows = pl.ds(q * (R2 // nq), R2 // nq)
    DMA = pltpu.SemaphoreType.DMA

    def copies(ins, outs, sems, which):
        sh, out = ins[0], outs[0]
        send1, recv1, send2, recv2, send0, recv0 = sems
        x, y, c, s = _place()
        sib = (x, y, 1 - c)
        rc = pltpu.make_async_remote_copy
        if which == "first":
            return [rc(sh.at[c, rows], out.at[s, c, rows], send1.at[m - 1], recv1.at[m - 1],
                       device_id=_chip_dev(s ^ m, c), device_id_type=MESH) for m in range(1, 4)]
        if which == "own":
            return [rc(sh.at[h, rows], out.at[s, h, rows], send0.at[h], recv0.at[h], device_id=sib, device_id_type=MESH)
                    for h in range(2)]
        if which == "landed":
            return [rc(sh.at[c, rows], out.at[s ^ m, c, rows], send1.at[m - 1], recv1.at[m - 1], device_id=sib,
                       device_id_type=MESH) for m in range(1, 4)]
        half = c if which == "passed" else 1 - c
        return [rc(out.at[s ^ m, half, rows], out.at[s ^ m, half, rows], send2.at[m - 1], recv2.at[m - 1], device_id=sib,
                   device_id_type=MESH) for m in range(1, 4)]

    def start(ins, outs, sems):
        for cp in copies(ins, outs, sems, "first") + copies(ins, outs, sems, "own"):
            cp.start()

    def wait(ins, outs, sems):
        passed = copies(ins, outs, sems, "passed")
        for lan, pas in zip(copies(ins, outs, sems, "landed"), passed):
            lan.wait_recv()
            pas.start()
        for cp in copies(ins, outs, sems, "handed"):
            cp.wait_recv()
        for cp in copies(ins, outs, sems, "own"):
            cp.wait()
        for cp in copies(ins, outs, sems, "first") + passed:
            cp.wait_send()

    return _Comm([shard] + ([land] if land is not None else []), [jax.ShapeDtypeStruct((4, two, R2, Cc), shard.dtype)],
                 [DMA((3,)), DMA((3,)), DMA((3,)), DMA((3,)), DMA((2,)), DMA((2,))], start, wait,
                 {1: 0} if land is not None else None)


def _a2a_comm(S1, q=0, nq=1, land=None):
    S4, R2, Cc = S1.shape
    rows = pl.ds(q * (R2 // nq), R2 // nq)
    DMA = pltpu.SemaphoreType.DMA

    def copies(ins, outs, sems):
        x, y, c, s = _place()
        return [pltpu.make_async_remote_copy(ins[0].at[s ^ m, rows], outs[0].at[m - 1, rows], sems[0].at[m - 1],
                                             sems[1].at[m - 1], device_id=_chip_dev(s ^ m, c), device_id_type=MESH)
                for m in range(1, 4)]

    def start(ins, outs, sems):
        for cp in copies(ins, outs, sems):
            cp.start()

    def wait(ins, outs, sems):
        for cp in copies(ins, outs, sems):
            cp.wait()

    return _Comm([S1] + ([land] if land is not None else []), [jax.ShapeDtypeStruct((3, R2, Cc), S1.dtype)],
                 [DMA((3,)), DMA((3,))], start, wait, {1: 0} if land is not None else None)


def _sibling_swap(pieces, name):
    R2, Cc = pieces[0].shape[2:]
    where = [(k, tl) for k, pc in enumerate(pieces) for tl in range(pc.shape[0])]

    def body(*refs):
        out_ref, send, recv = refs[len(pieces):]
        x, y, c, s = _place()
        cps = []
        for t, (k, tl) in enumerate(where):
            cp = pltpu.make_async_remote_copy(refs[k].at[tl, 1 - c], out_ref.at[t], send.at[t], recv.at[t],
                                              device_id=(x, y, 1 - c), device_id_type=MESH)
            cp.start()
            cps.append(cp)
        for cp in cps:
            cp.wait()

    n = len(where)
    return pl.pallas_call(
        body, name=name, in_specs=[ANY] * len(pieces), out_specs=ANY, out_shape=jax.ShapeDtypeStruct((n, R2, Cc), pieces[0].dtype),
        scratch_shapes=[pltpu.SemaphoreType.DMA((n,)), pltpu.SemaphoreType.DMA((n,))],
    )(*pieces)


def _add_half(pieces, A, cidx, name):
    R2, Cc = pieces[0].shape[2:]
    S4 = A.shape[0]
    tb, cb = _wide(R2, Cc, 3, 2 * HALO)
    nI, nJ = R2 // tb, Cc // cb

    def body(c_ref, g_ref, a_ref, *rest):
        rest[-1][...] = (g_ref[0, 0] + a_ref[0]).astype(BF16)

    out, t0 = None, 0
    for k, pc in enumerate(pieces):
        grid_spec = pltpu.PrefetchScalarGridSpec(
            num_scalar_prefetch=1, grid=(pc.shape[0], nI, nJ),
            in_specs=[pl.BlockSpec((1, 1, tb, cb), lambda t, i, j, c_ref: (t, c_ref[0], i, j)),
                      pl.BlockSpec((1, tb, cb), lambda t, i, j, c_ref, t0=t0: (t0 + t, i, j))] + ([ANY] if k else []),
            out_specs=pl.BlockSpec((tb, cb), lambda t, i, j, c_ref, t0=t0: ((t0 + t) * nI + i, j)))
        out = pl.pallas_call(
            functools.partial(body), name=f"{name}{k}", grid_spec=grid_spec, out_shape=jax.ShapeDtypeStruct((S4 * R2, Cc), BF16),
            input_output_aliases={3: 0} if k else {},
            compiler_params=pltpu.CompilerParams(dimension_semantics=("parallel", "parallel", "parallel"),
                                                 vmem_limit_bytes=VMEM_LIMIT),
        )(*((cidx, pc, A) + ((out,) if k else ())))
        t0 += pc.shape[0]
    return out


def _add_own(S1, B, chip_idx, name):
    S4, R2, Cc = S1.shape
    tb, cb = _wide(R2, Cc, 3, 2 * HALO)

    def body(s_idx, s_ref, b_ref, o_ref):
        o_ref[...] = ((s_ref[0].astype(F32) + b_ref[0].astype(F32)) + b_ref[1].astype(F32)) + b_ref[2].astype(F32)

    grid_spec = pltpu.PrefetchScalarGridSpec(
        num_scalar_prefetch=1, grid=(R2 // tb, Cc // cb),
        in_specs=[pl.BlockSpec((1, tb, cb), lambda i, j, s_idx: (s_idx[0], i, j)),
                  pl.BlockSpec((3, tb, cb), lambda i, j, s_idx: (0, i, j))],
        out_specs=pl.BlockSpec((tb, cb), lambda i, j, s_idx: (i, j)))
    return pl.pallas_call(body, name=name, grid_spec=grid_spec, out_shape=jax.ShapeDtypeStruct((R2, Cc), F32),
                          compiler_params=pltpu.CompilerParams(dimension_semantics=("parallel", "parallel"),
                                                               vmem_limit_bytes=VMEM_LIMIT))(chip_idx, S1, B)


def _sibling_send(Hs, name):
    R2, Cc = Hs.shape

    def body(h_ref, out_ref, send, recv):
        x, y, c, s = _place()
        cp = pltpu.make_async_remote_copy(h_ref, out_ref, send, recv, device_id=(x, y, 1 - c), device_id_type=MESH)
        cp.start()
        cp.wait()

    return pl.pallas_call(
        body, name=name, in_specs=[ANY], out_specs=ANY, out_shape=jax.ShapeDtypeStruct((R2, Cc), Hs.dtype),
        scratch_shapes=[pltpu.SemaphoreType.DMA, pltpu.SemaphoreType.DMA],
    )(Hs)


def _gather_all(buf, name):
    R, Cc = buf.shape

    def body(b_ref, out_ref, send, recv, local):
        x, y, c, s = _place()
        d = 2 * s + c
        mine = pltpu.make_async_copy(b_ref, out_ref.at[d], local)
        mine.start()
        cps = []
        for m in range(1, 8):
            t = d ^ m
            cp = pltpu.make_async_remote_copy(b_ref, out_ref.at[d], send.at[m - 1], recv.at[m - 1],
                                              device_id=(t // 4, (t // 2) % 2, t % 2), device_id_type=MESH)
            cp.start()
            cps.append(cp)
        for cp in cps:
            cp.wait()
        mine.wait()

    return pl.pallas_call(
        body, name=name, in_specs=[ANY], out_specs=ANY, out_shape=jax.ShapeDtypeStruct((8, R, Cc), buf.dtype),
        scratch_shapes=[pltpu.SemaphoreType.DMA((7,)), pltpu.SemaphoreType.DMA((7,)), pltpu.SemaphoreType.DMA],
    )(buf)


def _chip_sums(Gs, cidx, name):
    R, Cc = Gs[0].shape[1:]
    pieces = [g.reshape(g.shape[0], 2, R // 2, Cc) for g in Gs]
    A = _sibling_swap(pieces, name + "_swap")
    return _add_half(pieces, A, cidx, name + "_add").reshape(4, R // 2, Cc)


def _finish_shard(S1, B, cidx, chip_idx, name):
    Hs = _add_own(S1, B, chip_idx, name + "_sum")
    Ho = _sibling_send(Hs, name + "_gather")
    lo = jnp.where(cidx[0] == 0, Hs, Ho)
    hi = jnp.where(cidx[0] == 0, Ho, Hs)
    return jnp.concatenate([lo, hi], axis=0)


def _pack_rows(vs):
    flat = jnp.concatenate([v.reshape(-1) for v in vs])
    n = flat.shape[0]
    rows = -(-n // (LANES * 2 * HALO)) * 2 * HALO
    return jnp.pad(flat, (0, rows * LANES - n)).reshape(rows, LANES)


def _unpack_rows(buf, shapes):
    flat = buf.reshape(-1)
    outs, o = [], 0
    for shp in shapes:
        n = 1
        for d in shp:
            n *= d
        outs.append(flat[o:o + n].reshape(shp))
        o += n
    return outs


def kernel(x, p, norm_mix_g, w_in, conv_a_w, conv_qkv_w, a_log, dt_bias, dn_norm_g, w_out, norm_ffn_g, w_up, conv_ffn_w, w_down, norm_ple_g, w_ple_gate, w_ple_proj, final_norm_g, loss_target, m_norm_mix_g, m_w_in, m_conv_a_w, m_conv_qkv_w, m_a_log, m_dt_bias, m_dn_norm_g, m_w_out, m_norm_ffn_g, m_w_up, m_conv_ffn_w, m_w_down, m_norm_ple_g, m_w_ple_gate, m_w_ple_proj, m_final_norm_g, v_norm_mix_g, v_w_in, v_conv_a_w, v_conv_qkv_w, v_a_log, v_dt_bias, v_dn_norm_g, v_w_out, v_norm_ffn_g, v_w_up, v_conv_ffn_w, v_w_down, v_norm_ple_g, v_w_ple_gate, v_w_ple_proj, v_final_norm_g):
    xs = x[0]
    ps = p[0, 0]
    tgt = loss_target[0]
    T, D = xs.shape
    H = a_log.shape[-1]
    DNW = H * HEAD_DIM
    CW = conv_a_w.shape[-1] * 4
    F = w_down.shape[1] * 4
    PD = ps.shape[-1]
    IN_MAIN = 3 * CW + 4 * DNW
    IN_COLS = IN_MAIN + 2 * H
    assert w_in.shape[-1] * 4 == IN_COLS and CW + DNW == D and 2 * H <= LANES
    cb = _tile(min(CW, DNW), 512, LANES)
    while F % cb:
        cb -= LANES
    cidx = lax.axis_index("c").astype(jnp.int32).reshape(1)
    chip = 2 * lax.axis_index("x") + lax.axis_index("y")

    def halves(w):
        sh = w[0].astype(BF16)
        return sh.reshape(2, sh.shape[0] // 2, sh.shape[1])

    def whole(land):
        return land.reshape(4, 2 * land.shape[2], land.shape[3])

    def cols(g4):
        return jnp.transpose(g4, (1, 0, 2)).reshape(g4.shape[1], 4 * g4.shape[2])

    def rows(g4):
        return g4.reshape(4 * g4.shape[1], g4.shape[2])

    conv_shapes = [conv_a_w[0].shape, conv_qkv_w[0].shape, conv_ffn_w[0].shape]
    cpack = _pack_rows([conv_a_w[0], conv_qkv_w[0], conv_ffn_w[0]])
    sh_in, sh_out, sh_up, sh_down, sh_pg, sh_pp = (halves(w) for w in (w_in, w_out, w_up, w_down, w_ple_gate, w_ple_proj))
    l_in, cg = _run_comm(_merge(_ag_comm(sh_in), _ag_comm(cpack.reshape(2, cpack.shape[0] // 2, LANES))), "ag_w_in_conv")
    w_in_f = cols(whole(l_in))
    w_in_main = w_in_f[:, :IN_MAIN]
    w_in_small = jnp.pad(w_in_f[:, IN_MAIN:], ((0, 0), (0, LANES - 2 * H)))
    cg = cg.reshape(4, cpack.shape[0], LANES)
    parts = [_unpack_rows(cg[t], conv_shapes) for t in range(4)]
    cw_a = jnp.concatenate([parts[t][0] for t in range(4)], axis=1)
    cw_qkv = jnp.concatenate([parts[t][1] for t in range(4)], axis=1)
    cw_ffn = jnp.concatenate([parts[t][2] for t in range(4)], axis=1)
    cw_q, cw_k, cw_v = cw_qkv[:, :DNW], cw_qkv[:, DNW:2 * DNW], cw_qkv[:, 2 * DNW:]
    cw_fg, cw_fv = cw_ffn[:, :F], cw_ffn[:, F:]
    pad_row = lambda v: jnp.pad(v, ((0, 0), (0, LANES - v.shape[1])))
    a_log_row, dt_row = pad_row(a_log), pad_row(dt_bias)
    gdn_t = jnp.tile(dn_norm_g, (1, H))
    gfin = final_norm_g.reshape(1, D)

    h1 = _rms_fwd(xs, norm_mix_g, "rms1")
    proj, (l_up,) = _mm(h1, w_in_main, mode="nn", out_dtypes=[F32], name="mm_proj", comm=_ag_comm(sh_up, q=0, nq=2))
    small = _mm(h1, w_in_small, mode="nn", out_dtypes=[F32], name="mm_small")
    ya = _ga_fwd(proj, cw_a, CW, cb)
    nq = 3 * CW // cb
    nd = DNW // cb
    qn = _qkv_fwd(proj, cw_q, nq, True, DNW, cb, "q_fwd")
    kn = _qkv_fwd(proj, cw_k, nq + nd, True, DNW, cb, "k_fwd")
    vs = _qkv_fwd(proj, cw_v, nq + 2 * nd, False, DNW, cb, "v_fwd")
    g, beta = _gb_fwd(small, a_log_row, dt_row, H)
    o, S0, (l_up, l_out) = _delta_fwd(qn, kn, vs, g, beta, comm=_merge(_ag_comm(sh_up, l_up, q=1, nq=2), _ag_comm(sh_out)))
    w_out_f = rows(whole(l_out))
    w_out_a, w_out_b = w_out_f[:CW], w_out_f[CW:]
    w_up_4 = whole(l_up)
    z_coff = (3 * CW + 3 * DNW) // DNW
    assert (3 * CW + 3 * DNW) % DNW == 0 and CW % DNW == 0
    yb = _gnorm_fwd(o, proj, z_coff, gdn_t, DNW)
    add = lambda acc, r: (r + acc,)
    x1 = _mm(ya, w_out_a, mode="nn", out_dtypes=[F32], epi=add, extras=[xs], name="mm_out_a")
    x1 = _mm(yb, w_out_b, mode="nn", out_dtypes=[F32], epi=add, extras=[x1], name="mm_out_b")
    h2 = _rms_fwd(x1, norm_ffn_g, "rms2")
    up_g, (l_down,) = _mm(h2, w_up_4, mode="nn", b_split=(0, 2), out_dtypes=[F32], name="mm_up_g",
                          comm=_ag_comm(sh_down, q=0, nq=2))
    up_v, (l_down,) = _mm(h2, w_up_4, mode="nn", b_split=(2, 2), out_dtypes=[F32], name="mm_up_v",
                          comm=_ag_comm(sh_down, l_down, q=1, nq=2))
    w_down_f = rows(whole(l_down))
    act = _ffn_fwd(up_g, up_v, cw_fg, cw_fv, cb)
    x2, (l_pg, l_pp) = _mm(act, w_down_f, mode="nn", out_dtypes=[F32], epi=add, extras=[x1], name="mm_down",
                           comm=_merge(_ag_comm(sh_pg), _ag_comm(sh_pp)))
    w_pg_f = rows(whole(l_pg))
    w_pp_4 = whole(l_pp)
    h3 = _rms_fwd(x2, norm_ple_g, "rms3")
    pp = _mm(ps, w_pp_4, mode="nn", b_split=(0, 4), out_dtypes=[F32], name="mm_pp")

    def ple_epi(acc, x2v, ppv):
        pg = _sigmoid(acc)
        return x2v + pg * ppv, pg

    x3, pg = _mm(h3, w_pg_f, mode="nn", out_dtypes=[F32, F32], epi=ple_epi, extras=[x2, pp], name="mm_pg")

    dx3, fin = _final_fb(x3, tgt, gfin)
    loss = lax.psum(jnp.sum(fin[1]), ("x", "y", "c"))
    d_gfin = fin[0:1]
    dpg, dpp = _ple_bwd(dx3, pp, pg)
    def split_cols(dW):
        R, C4 = dW.shape
        return jnp.transpose(dW.reshape(R, 4, C4 // 4), (1, 0, 2))

    def split_rows(dW):
        return dW.reshape(4, dW.shape[0] // 4, dW.shape[1])

    dW_pp = _mm(ps, dpp, mode="tn", out_split=4, out_dtypes=[F32], name="mm_dw_pp")
    dW_pg = _mm(h3, dpg, mode="tn", out_dtypes=[F32], name="mm_dw_pg")
    S_pp = _chip_sums([dW_pp], cidx, "rs_w_pp")
    S_pg = _chip_sums([split_rows(dW_pg)], cidx, "rs_w_pg")
    dh3 = _mm(dpg, w_pg_f, mode="nt", out_dtypes=[F32], name="mm_dh3")
    dx2, dx2_b, d_gple = _rms_bwd(dh3, x2, norm_ple_g, dx3, "rms3_bwd")
    dW_down, (B_pp, B_pg) = _mm(act, dx2_b, mode="tn", out_dtypes=[F32], name="mm_dw_down",
                                comm=_merge(_a2a_comm(S_pp), _a2a_comm(S_pg)))
    S_down = _chip_sums([split_rows(dW_down)], cidx, "rs_w_down")
    dact, (B_down,) = _mm(dx2_b, w_down_f, mode="nt", out_dtypes=[F32], name="mm_dact", comm=_a2a_comm(S_down))
    dup_g, dup_v, dcw_fg, dcw_fv = _ffn_bwd(dact, up_g, up_v, cw_fg, cw_fv, cb)
    dW_up_g = _mm(h2, dup_g, mode="tn", out_split=2, out_dtypes=[F32], name="mm_dw_up_g")
    dW_up_v = _mm(h2, dup_v, mode="tn", out_split=2, out_dtypes=[F32], name="mm_dw_up_v")
    S_up = _chip_sums([dW_up_g, dW_up_v], cidx, "rs_w_up")
    dh2, (B_up,) = _mm(dup_g, w_up_4, mode="nt", b_split=(0, 2), out_dtypes=[F32], name="mm_dh2_g",
                       comm=_a2a_comm(S_up, 0, 2))
    dh2, (B_up,) = _mm(dup_v, w_up_4, mode="nt", b_split=(2, 2), out_dtypes=[F32], epi=add, extras=[dh2], name="mm_dh2_v",
                       comm=_a2a_comm(S_up, 1, 2, B_up))
    dx1, dx1_b, d_gffn = _rms_bwd(dh2, x1, norm_ffn_g, dx2, "rms2_bwd")
    dW_out_a = _mm(ya, dx1_b, mode="tn", out_dtypes=[F32], name="mm_dw_out_a")
    dW_out_b = _mm(yb, dx1_b, mode="tn", out_dtypes=[F32], name="mm_dw_out_b")
    S_out = _chip_sums([dW_out_a.reshape(-1, D // 4, D), dW_out_b.reshape(-1, D // 4, D)], cidx, "rs_w_out")
    dymix, (B_out,) = _mm(dx1_b, w_out_f, mode="nt", out_dtypes=[F32], name="mm_dymix", comm=_a2a_comm(S_out))
    dax, dab, dac, dcw_a = _ga_bwd(dymix, proj, cw_a, CW, cb)
    do, dz, d_gdn = _gnorm_bwd(dymix, CW // DNW, o, proj, z_coff, gdn_t, DNW)
    dqn, dkn, dvs, dgB, dbB = _delta_bwd(qn, kn, vs, g, beta, S0, do)
    dq_pre, dcw_q = _qkv_bwd(dqn, proj, cw_q, nq, True, DNW, cb, "q_bwd")
    dk_pre, dcw_k = _qkv_bwd(dkn, proj, cw_k, nq + nd, True, DNW, cb, "k_bwd")
    dv_pre, dcw_v = _qkv_bwd(dvs, proj, cw_v, nq + 2 * nd, False, DNW, cb, "v_bwd")
    dsmall, d_ab = _gb_bwd(dgB, dbB, small, g, beta, a_log_row, dt_row, H)
    dproj = jnp.concatenate([dax, dab, dac, dq_pre, dk_pre, dv_pre, dz], axis=1)
    dW_in_main = _mm(h1, dproj, mode="tn", out_dtypes=[F32], name="mm_dw_in")
    dW_in_small = _mm(h1, dsmall, mode="tn", out_dtypes=[F32], name="mm_dw_in_small")
    S_in = _chip_sums([split_cols(jnp.concatenate([dW_in_main, dW_in_small[:, :2 * H]], axis=1))], cidx, "rs_w_in")
    dh1, (B_in,) = _mm(dproj, w_in_main, mode="nt", out_dtypes=[F32], name="mm_dh1", comm=_a2a_comm(S_in))
    dh1 = _mm(dsmall, w_in_small, mode="nt", out_dtypes=[F32], epi=add, extras=[dh1], name="mm_dh1_small")
    dx, _, d_gmix = _rms_bwd(dh1, xs, norm_mix_g, dx1, "rms1_bwd")

    chip_idx = chip.astype(jnp.int32).reshape(1)

    def update(S1, B, w, m, v, name):
        gr = _finish_shard(S1, B, cidx, chip_idx, "rs_" + name)
        delta, m2, v2 = _adamw(w[0], gr, m[0], v[0], "adamw_" + name)
        return gr[None], delta[None], m2[None], v2[None]

    big = {
        "w_in": update(S_in, B_in, w_in, m_w_in, v_w_in, "w_in"),
        "w_out": update(S_out, B_out, w_out, m_w_out, v_w_out, "w_out"),
        "w_up": update(S_up, B_up, w_up, m_w_up, v_w_up, "w_up"),
        "w_down": update(S_down, B_down, w_down, m_w_down, v_w_down, "w_down"),
        "w_ple_gate": update(S_pg, B_pg, w_ple_gate, m_w_ple_gate, v_w_ple_gate, "w_pg"),
        "w_ple_proj": update(S_pp, B_pp, w_ple_proj, m_w_ple_proj, v_w_ple_proj, "w_pp"),
    }

    small_grads = [d_gmix[0:1], dcw_a[:cw_a.shape[0]], jnp.concatenate([dcw_q, dcw_k, dcw_v], axis=1)[:cw_qkv.shape[0]],
                   d_ab[0:1, :H], d_ab[1:2, :H], d_gdn[0:1], d_gffn[0:1],
                   jnp.concatenate([dcw_fg, dcw_fv], axis=1)[:cw_ffn.shape[0]], d_gple[0:1], d_gfin]
    small_shapes = [v.shape for v in small_grads]
    gpack = _pack_rows(small_grads)
    gsum = _sum_stack(_gather_all(gpack, "ag_small"), "sum_small")
    (g_gmix, g_cwa, g_cwqkv, g_alog, g_dt, g_gdn, g_gffn, g_cwffn, g_gple, g_gfin) = _unpack_rows(gsum, small_shapes)

    def my_cols(v):
        Cc = v.shape[1] // 4
        return lax.dynamic_slice_in_dim(v, chip * Cc, Cc, axis=1)

    g_small = [g_gmix, my_cols(g_cwa), my_cols(g_cwqkv), g_alog, g_dt, g_gdn, g_gffn, my_cols(g_cwffn), g_gple, g_gfin]
    w_small = [norm_mix_g, conv_a_w[0], conv_qkv_w[0], a_log, dt_bias, dn_norm_g, norm_ffn_g, conv_ffn_w[0], norm_ple_g, gfin]
    m_small = [m_norm_mix_g, m_conv_a_w[0], m_conv_qkv_w[0], m_a_log, m_dt_bias, m_dn_norm_g, m_norm_ffn_g, m_conv_ffn_w[0],
               m_norm_ple_g, m_final_norm_g.reshape(1, D)]
    v_small = [v_norm_mix_g, v_conv_a_w[0], v_conv_qkv_w[0], v_a_log, v_dt_bias, v_dn_norm_g, v_norm_ffn_g, v_conv_ffn_w[0],
               v_norm_ple_g, v_final_norm_g.reshape(1, D)]
    shp = [v.shape for v in w_small]
    ds_, ms_, vs_ = _adamw(_pack_rows(w_small), _pack_rows(g_small), _pack_rows(m_small), _pack_rows(v_small), "adamw_small")
    out_shapes = [norm_mix_g.shape, conv_a_w.shape, conv_qkv_w.shape, a_log.shape, dt_bias.shape, dn_norm_g.shape,
                  norm_ffn_g.shape, conv_ffn_w.shape, norm_ple_g.shape, final_norm_g.shape]
    rs = lambda vals: [v.reshape(s) for v, s in zip(vals, out_shapes)]
    sg, sd_, sm_, sv_ = rs(g_small), rs(_unpack_rows(ds_, shp)), rs(_unpack_rows(ms_, shp)), rs(_unpack_rows(vs_, shp))
    names_small = ["norm_mix_g", "conv_a_w", "conv_qkv_w", "a_log", "dt_bias", "dn_norm_g", "norm_ffn_g", "conv_ffn_w",
                   "norm_ple_g", "final_norm_g"]
    res = {n: (sg[i], sd_[i], sm_[i], sv_[i]) for i, n in enumerate(names_small)}
    res.update(big)
    order = ["norm_mix_g", "w_in", "conv_a_w", "conv_qkv_w", "a_log", "dt_bias", "dn_norm_g", "w_out", "norm_ffn_g", "w_up",
             "conv_ffn_w", "w_down", "norm_ple_g", "w_ple_gate", "w_ple_proj", "final_norm_g"]
    return (loss, dx[None], *[res[n][0] for n in order], *[res[n][1] for n in order], *[res[n][2] for n in order],
            *[res[n][3] for n in order])
```

```python
import functools

import jax
import jax.numpy as jnp
from jax import lax
from jax.experimental import pallas as pl
from jax.experimental.pallas import tpu as pltpu

F32 = jnp.float32
BF16 = jnp.bfloat16
LANES = 128
HALO = 8
HEAD_DIM = 128
CHUNK = 64
EPS = 1e-6
VMEM_LIMIT = 56 * 1024 * 1024
MM_VMEM_BUDGET = 40 * 1024 * 1024
EW_VMEM_BUDGET = 24 * 1024 * 1024
MESH = pl.DeviceIdType.MESH

ADAM_LR, ADAM_B1, ADAM_B2, ADAM_EPS, ADAM_WD, ADAM_STEP = 0.001, 0.9, 0.999, 1e-08, 0.01, 10


def _tile(n, cap, unit):
    if n <= cap:
        return n
    d = (cap // unit) * unit
    while d >= unit:
        if n % d == 0:
            return d
        d -= unit
    raise ValueError(f"no tile for {n} (cap {cap}, unit {unit})")


def _sigmoid(x):
    return 1.0 / (1.0 + jnp.exp(-x))


def _mm_tiles(M, K, n_unit, k_unit, a_bytes, n_blocks_mn):
    tm, tk = _tile(M, 1024, LANES), _tile(k_unit, 2048, LANES)
    for cap in (1536, 1024, 512, 256, 128):
        tn = _tile(n_unit, cap, LANES)
        acc = 4 * tm * tn if K // tk > 1 else 0
        if 2 * tm * tk * a_bytes + 2 * tk * tn * 2 + 2 * 4 * tm * tn * n_blocks_mn + acc <= MM_VMEM_BUDGET:
            break
    return tm, tn, tk


def _mm(a, b, *, mode, out_dtypes, name, epi=None, extras=(), comm=None, b_split=None, out_split=None):
    if b_split is not None:
        lo, ns = b_split
        Rb, Cb = b.shape[1], b.shape[2]
    if mode == "nn":
        (M, K), N = a.shape, (ns * Cb if b_split else b.shape[1])
    elif mode == "nt":
        (M, K), N = a.shape, (Rb if b_split else b.shape[0])
    else:
        (K, M), N = a.shape, b.shape[1]
    n_ex, n_out = len(extras), len(out_dtypes)
    n_unit = Cb if (b_split and mode == "nn") else (N // out_split if out_split else N)
    k_unit = Cb if (b_split and mode == "nt") else K
    tm, tn, tk = _mm_tiles(M, K, n_unit, k_unit, a.dtype.itemsize, n_ex + n_out)
    nk = K // tk
    a_spec = pl.BlockSpec((tk, tm), lambda i, j, k: (k, i)) if mode == "tn" else pl.BlockSpec((tm, tk), lambda i, j, k: (i, k))
    if b_split and mode == "nn":
        nb = Cb // tn
        b_spec = pl.BlockSpec((None, tk, tn), lambda i, j, k: (lo + j // nb, k, j % nb))
    elif b_split:
        nb = Cb // tk
        b_spec = pl.BlockSpec((None, tn, tk), lambda i, j, k: (lo + k // nb, j, k % nb))
    else:
        b_spec = pl.BlockSpec((tn, tk), lambda i, j, k: (j, k)) if mode == "nt" else pl.BlockSpec((tk, tn), lambda i, j, k: (k, j))
    mn_spec = pl.BlockSpec((tm, tn), lambda i, j, k: (i, j))
    out_shapes = [jax.ShapeDtypeStruct((M, N), dt) for dt in out_dtypes]
    out_specs = [mn_spec] * n_out
    if out_split:
        assert n_ex == 0 and n_out == 1
        nbo = (N // out_split) // tn
        out_specs = [pl.BlockSpec((None, tm, tn), lambda i, j, k: (j // nbo, i, j % nbo))]
        out_shapes = [jax.ShapeDtypeStruct((out_split, M, N // out_split), out_dtypes[0])]
    dims = {"nn": (((1,), (0,)), ((), ())), "nt": (((1,), (1,)), ((), ())), "tn": (((0,), (0,)), ((), ()))}[mode]

    def body(*refs):
        a_ref, b_ref = refs[0], refs[1]
        ex_refs = refs[2:2 + n_ex]
        out_refs = refs[2 + n_ex:2 + n_ex + n_out]
        part = lax.dot_general(a_ref[...].astype(BF16), b_ref[...].astype(BF16), dims, preferred_element_type=F32)

        def finish(acc):
            outs = (acc,) if epi is None else epi(acc, *[r[...] for r in ex_refs])
            for r, o in zip(out_refs, outs):
                r[...] = o.astype(r.dtype)

        if nk == 1:
            finish(part)
            return
        acc_ref = refs[-1]
        k = pl.program_id(2)

        @pl.when(k == 0)
        def _():
            acc_ref[...] = part

        @pl.when(jnp.logical_and(k > 0, k < nk - 1))
        def _():
            acc_ref[...] += part

        @pl.when(k == nk - 1)
        def _():
            finish(acc_ref[...] + part)

    outs, comm_outs = _call(
        body, name=name, grid=(M // tm, N // tn, nk),
        in_specs=[a_spec, b_spec] + [mn_spec] * n_ex,
        out_specs=out_specs,
        out_shape=out_shapes,
        scratch_shapes=[pltpu.VMEM((tm, tn), F32)] if nk > 1 else [],
        semantics=("parallel", "parallel", "arbitrary"), args=(a, b, *extras), comm=comm)
    res = outs[0] if n_out == 1 else outs
    return res if comm is None else (res, comm_outs)


def _tiled(fn, *, T, C, ins, out_dtypes=(), acc_rows=(), tb=256, cb=512, name):
    tb = _tile(T, tb, HALO)
    nI, nJ = T // tb, C // cb
    hb, nH = tb // HALO, T // HALO
    specs, args, kinds = [], [], []
    for kind, arr, cmap in ins:
        cm = cmap if cmap is not None else (lambda j: j)
        kinds.append(kind)
        if kind == "cur":
            specs.append(pl.BlockSpec((tb, cb), lambda j, i, cm=cm: (i, cm(j))))
            args.append(arr)
        elif kind == "ext":
            specs.append(pl.BlockSpec((HALO, cb), lambda j, i, cm=cm: (jnp.maximum(i * hb - 1, 0), cm(j))))
            specs.append(pl.BlockSpec((tb, cb), lambda j, i, cm=cm: (i, cm(j))))
            specs.append(pl.BlockSpec((HALO, cb), lambda j, i, cm=cm: (jnp.minimum((i + 1) * hb, nH - 1), cm(j))))
            args += [arr, arr, arr]
        elif kind == "row":
            specs.append(pl.BlockSpec((arr.shape[0], cb), lambda j, i, cm=cm: (0, cm(j))))
            args.append(arr)
        elif kind == "stack":
            specs.append(pl.BlockSpec((arr.shape[0], tb, cb), lambda j, i, cm=cm: (0, i, cm(j))))
            args.append(arr)
        else:
            raise ValueError(kind)
    n_in = len(args)
    n_out, n_acc = len(out_dtypes), len(acc_rows)

    def body(*refs):
        j, i = pl.program_id(0), pl.program_id(1)
        vals, r = [], 0
        for kind in kinds:
            if kind == "ext":
                prev = jnp.where(i == 0, 0.0, refs[r][...].astype(F32))
                cur = refs[r + 1][...].astype(F32)
                nxt = jnp.where(i == nI - 1, 0.0, refs[r + 2][...].astype(F32))
                vals.append(jnp.concatenate([prev, cur, nxt], axis=0))
                r += 3
            else:
                vals.append(refs[r][...])
                r += 1
        res = fn(j, i, *vals)
        for ref, o in zip(refs[n_in:n_in + n_out], res[:n_out]):
            ref[...] = o.astype(ref.dtype)
        for ref, o in zip(refs[n_in + n_out:], res[n_out:]):
            @pl.when(i == 0)
            def _(ref=ref, o=o):
                ref[...] = o

            @pl.when(i > 0)
            def _(ref=ref, o=o):
                ref[...] += o

    outs = pl.pallas_call(
        body, name=name, grid=(nJ, nI), in_specs=specs,
        out_specs=[pl.BlockSpec((tb, cb), lambda j, i: (i, j))] * n_out
        + [pl.BlockSpec((rows, cb), lambda j, i: (0, j)) for rows in acc_rows],
        out_shape=[jax.ShapeDtypeStruct((T, C), dt) for dt in out_dtypes]
        + [jax.ShapeDtypeStruct((rows, C), F32) for rows in acc_rows],
        compiler_params=pltpu.CompilerParams(dimension_semantics=("parallel", "arbitrary"),
                                             vmem_limit_bytes=VMEM_LIMIT),
    )(*args)
    return outs


def _conv_causal(xe, w):
    K = w.shape[0]
    y = xe * w[K - 1:K]
    for j in range(K - 1):
        y = y + pltpu.roll(xe, K - 1 - j, 0) * w[j:j + 1]
    return y


def _conv_anti(de, w):
    K, n = w.shape[0], de.shape[0]
    y = de * w[K - 1:K]
    for j in range(K - 1):
        y = y + pltpu.roll(de, n - (K - 1 - j), 0) * w[j:j + 1]
    return y


def _conv_dw(dce, xe, K):
    n = dce.shape[0]
    tb = n - 2 * HALO
    rows = []
    for j in range(K):
        xs = xe if j == K - 1 else pltpu.roll(xe, K - 1 - j, 0)
        rows.append(jnp.sum((dce * xs)[HALO:HALO + tb], axis=0, keepdims=True))
    rows.append(jnp.zeros((HALO - K, dce.shape[1]), F32))
    return jnp.concatenate(rows, axis=0)


def _own(xe):
    return xe[HALO:xe.shape[0] - HALO]


def _row0(v):
    return jnp.concatenate([v, jnp.zeros((HALO - 1, v.shape[1]), F32)], axis=0)


def _per_head(fn, *xs):
    n = xs[0].shape[1] // HEAD_DIM
    outs = [fn(*[x[:, g * HEAD_DIM:(g + 1) * HEAD_DIM] for x in xs]) for g in range(n)]
    return outs[0] if n == 1 else jnp.concatenate(outs, axis=1)


def _rms_fwd(x, g, name):
    T, D = x.shape

    def fn(j, i, xv, gv):
        r = lax.rsqrt(jnp.mean(xv * xv, axis=1, keepdims=True) + EPS)
        return (xv * r * gv,)

    return _tiled(fn, T=T, C=D, ins=[("cur", x, None), ("row", g, None)], out_dtypes=[BF16], cb=D, name=name)[0]


def _rms_bwd_math(dy, xv, gv):
    r = lax.rsqrt(jnp.mean(xv * xv, axis=1, keepdims=True) + EPS)
    xh = xv * r
    dxh = dy * gv
    dx = r * (dxh - xh * jnp.mean(dxh * xh, axis=1, keepdims=True))
    dg = jnp.sum(dy * xh, axis=0, keepdims=True)
    return dx, dg


def _rms_bwd(dh, x, g, dres, name):
    T, D = x.shape

    def fn(j, i, dhv, xv, gv, dr):
        dx, dg = _rms_bwd_math(dhv, xv, gv)
        return dr + dx, dr + dx, _row0(dg)

    return _tiled(fn, T=T, C=D, ins=[("cur", dh, None), ("cur", x, None), ("row", g, None), ("cur", dres, None)],
                  out_dtypes=[F32, BF16], acc_rows=[HALO], cb=D, name=name)


def _final_fb(x3, tgt, g):
    T, D = x3.shape

    def fn(j, i, xv, tv, gv):
        r = lax.rsqrt(jnp.mean(xv * xv, axis=1, keepdims=True) + EPS)
        xh = xv * r
        e = xh * gv - tv
        dy = e * (1.0 / D)
        dxh = dy * gv
        dx = r * (dxh - xh * jnp.mean(dxh * xh, axis=1, keepdims=True))
        dg = jnp.sum(dy * xh, axis=0, keepdims=True)
        ls = jnp.sum(e * e, axis=0, keepdims=True) * (0.5 / D)
        return dx, jnp.concatenate([dg, ls, jnp.zeros((HALO - 2, D), F32)], axis=0)

    return _tiled(fn, T=T, C=D, ins=[("cur", x3, None), ("cur", tgt, None), ("row", g, None)],
                  out_dtypes=[F32], acc_rows=[HALO], cb=D, name="final_fb")


def _ga_fwd(proj, w_a, CW, cb):
    T = proj.shape[0]
    n = CW // cb

    def fn(j, i, ax, ab, ac, w):
        c = _conv_causal(ac * ax, w)
        return (ab * _own(c),)

    return _tiled(fn, T=T, C=CW, ins=[("ext", proj, None), ("cur", proj, lambda j: j + n), ("ext", proj, lambda j: j + 2 * n),
                                       ("row", w_a, None)], out_dtypes=[BF16], cb=cb, name="ga_fwd")[0]


def _ga_bwd(dymix, proj, w_a, CW, cb):
    T = proj.shape[0]
    n = CW // cb
    K = w_a.shape[0]

    def fn(j, i, dy, ax, ab, ac, w):
        u = ac * ax
        c = _conv_causal(u, w)
        dc = dy * ab
        du = _conv_anti(dc, w)
        return _own(du * ac), _own(dy * c), _own(du * ax), _conv_dw(dc, u, K)

    return _tiled(fn, T=T, C=CW, ins=[("ext", dymix, None), ("ext", proj, None), ("ext", proj, lambda j: j + n),
                                       ("ext", proj, lambda j: j + 2 * n), ("row", w_a, None)],
                  out_dtypes=[BF16, BF16, BF16], acc_rows=[HALO], cb=cb, name="ga_bwd")


def _l2n(s):
    return s * lax.rsqrt(jnp.sum(s * s, axis=1, keepdims=True) + EPS)


def _qkv_fwd(proj, w_sec, coff, normalize, DNW, cb, name):
    T = proj.shape[0]

    def fn(j, i, pre, w):
        c = _own(_conv_causal(pre, w))
        s = c * _sigmoid(c)
        return (_per_head(_l2n, s) if normalize else s,)

    return _tiled(fn, T=T, C=DNW, ins=[("ext", proj, lambda j: j + coff), ("row", w_sec, None)],
                  out_dtypes=[F32], cb=cb, name=name)[0]


def _qkv_bwd(dsec, proj, w_sec, coff, normalize, DNW, cb, name):
    T = proj.shape[0]
    K = w_sec.shape[0]

    def l2n_bwd(s, dn):
        r = lax.rsqrt(jnp.sum(s * s, axis=1, keepdims=True) + EPS)
        nrm = s * r
        return r * (dn - nrm * jnp.sum(dn * nrm, axis=1, keepdims=True))

    def fn(j, i, dn, pre, w):
        c = _conv_causal(pre, w)
        sg = _sigmoid(c)
        s = c * sg
        ds = _per_head(l2n_bwd, s, dn) if normalize else dn
        dc = ds * (sg * (1.0 + c * (1.0 - sg)))
        return _own(_conv_anti(dc, w)), _conv_dw(dc, pre, K)

    return _tiled(fn, T=T, C=DNW, ins=[("ext", dsec, None), ("ext", proj, lambda j: j + coff), ("row", w_sec, None)],
                  out_dtypes=[BF16], acc_rows=[HALO], cb=cb, name=name)


def _gb_fwd(small, a_log_row, dt_row, H):
    T = small.shape[0]

    def fn(j, i, sm, al, dt):
        z = sm + dt
        sp = jnp.maximum(z, 0.0) + jnp.log(1.0 + jnp.exp(-jnp.abs(z)))
        g = -jnp.exp(al) * sp
        beta = _sigmoid(pltpu.roll(sm, LANES - H, 1))
        return g, beta

    return _tiled(fn, T=T, C=LANES, ins=[("cur", small, None), ("row", a_log_row, None), ("row", dt_row, None)],
                  out_dtypes=[F32, F32], cb=LANES, name="gb_fwd")


def _gb_bwd(dgB, dbB, small, g, beta, a_log_row, dt_row, H):
    T = small.shape[0]

    def fn(j, i, dgv, dbv, sm, gv, bv, al, dt):
        lane = lax.broadcasted_iota(jnp.int32, sm.shape, 1)
        dg = jnp.zeros(sm.shape, F32)
        db = jnp.zeros(sm.shape, F32)
        for h in range(H):
            dg = jnp.where(lane == h, jnp.sum(dgv[h], axis=1, keepdims=True), dg)
            db = jnp.where(lane == h, jnp.sum(dbv[h], axis=1, keepdims=True), db)
        da = dg * (-jnp.exp(al)) * _sigmoid(sm + dt)
        dbb = db * bv * (1.0 - bv)
        dsm = jnp.where(lane < H, da, 0.0) + pltpu.roll(jnp.where(lane < H, dbb, 0.0), H, 1)
        d_alog = jnp.sum(jnp.where(lane < H, dg * gv, 0.0), axis=0, keepdims=True)
        d_dt = jnp.sum(jnp.where(lane < H, da, 0.0), axis=0, keepdims=True)
        return dsm, jnp.concatenate([d_alog, d_dt, jnp.zeros((HALO - 2, LANES), F32)], axis=0)

    return _tiled(fn, T=T, C=LANES, ins=[("stack", dgB, None), ("stack", dbB, None), ("cur", small, None), ("cur", g, None),
                                          ("cur", beta, None), ("row", a_log_row, None), ("row", dt_row, None)],
                  out_dtypes=[BF16], acc_rows=[HALO], cb=LANES, name="gb_bwd")


_DIMS = {"nn": (((1,), (0,)), ((), ())), "nt": (((1,), (1,)), ((), ())), "tn": (((0,), (0,)), ((), ()))}
_DOT_BWD = {"nn": (("nt", "gb"), ("tn", "ag")), "nt": (("nn", "gb"), ("tn", "ga")), "tn": (("nt", "bg"), ("nn", "ag"))}


def _split(a):
    hi = a.astype(BF16)
    return hi, (a - hi.astype(F32)).astype(BF16)


def _raw_dot(a, b, kind, passes):
    dg = lambda x, y: lax.dot_general(x, y, _DIMS[kind], preferred_element_type=F32)
    if passes == 1:
        return dg(a.astype(BF16), b.astype(BF16))
    ah, al = _split(a)
    bh, bl = _split(b)
    return dg(ah, bh) + (dg(ah, bl) + dg(al, bh))


@functools.lru_cache(maxsize=None)
def _dotf(kind, passes):
    @jax.custom_vjp
    def f(a, b):
        return _raw_dot(a, b, kind, passes)

    def fwd(a, b):
        return _raw_dot(a, b, kind, passes), (a, b)

    def bwd(res, g):
        ops = {"a": res[0], "b": res[1], "g": g}
        (ka, oa), (kb, ob) = _DOT_BWD[kind]
        return (_raw_dot(ops[oa[0]], ops[oa[1]], ka, passes), _raw_dot(ops[ob[0]], ops[ob[1]], kb, passes))

    f.defvjp(fwd, bwd)
    return f


@jax.custom_vjp
def _saved_inverse(L, inv):
    return inv


def _saved_inverse_fwd(L, inv):
    return inv, inv


def _saved_inverse_bwd(inv, g):
    d3nt, d3tn = _dotf("nt", 3), _dotf("tn", 3)
    return -d3nt(d3tn(inv, g), inv), jnp.zeros_like(inv)


_saved_inverse.defvjp(_saved_inverse_fwd, _saved_inverse_bwd)


def _chunk_fn(q, k, v, gB, bB, S, inv_saved=None):
    C = CHUNK
    d3, d3nt = _dotf("nn", 3), _dotf("nt", 3)
    d1, d1nt, d1tn = _dotf("nn", 1), _dotf("nt", 1), _dotf("tn", 1)
    each = lambda f, *ls: tuple(f(*xs) for xs in zip(*ls))
    row = lax.broadcasted_iota(jnp.int32, (C, C), 0)
    col = lax.broadcasted_iota(jnp.int32, (C, C), 1)
    causal = row >= col
    strict = row > col
    tril = jnp.where(causal, 1.0, 0.0).astype(F32)
    eye = jnp.where(row == col, 1.0, 0.0).astype(F32)
    avg = jnp.full((C, HEAD_DIM), 1.0 / HEAD_DIM, F32)
    gc = each(lambda g: d3(tril, g), gB)
    R = each(lambda g: d3nt(avg, g), gc)
    decay = each(lambda g, r: jnp.where(causal, jnp.exp(jnp.where(causal, g[:, :C] - r, 0.0)), 0.0), gc, R)
    kk = each(lambda x: d1nt(x, x), k)
    L = each(lambda a, d, b: jnp.where(strict, a * d * b[:, :C], 0.0), kk, decay, bB)
    if inv_saved is None:
        inv = each(lambda l: eye - l, L)
        P = L
        for _ in range(5):
            P = each(lambda p: d3(p, p), P)
            inv = each(lambda a, p: d3(a, eye + p), inv, P)
    else:
        inv = each(_saved_inverse, L, inv_saved)
    eg = each(jnp.exp, gc)
    u = each(lambda a, x, b: d3(a, x * b), inv, v, bB)
    w = each(lambda a, x, b, e: d3(a, x * b * e), inv, k, bB, eg)
    qs = each(lambda x: x * (HEAD_DIM ** -0.5), q)
    qk = each(lambda a, x, d: d1nt(a, x) * d, qs, k, decay)
    gl = each(lambda g: g[C - 1:C, :], gc)
    v_new = each(lambda a, b, s: a - d1(b, s), u, w, S)
    o1 = each(lambda a, e, s: d1(a * e, s), qs, eg, S)
    o = each(lambda a, b, c: a + d1(b, c), o1, qk, v_new)
    kv = each(lambda x, a, g, vn: d1tn(x * jnp.exp(a - g), vn), k, gl, gc, v_new)
    S_new = each(lambda s, a, b: s * jnp.exp(a) + b, S, gl, kv)
    return (o, S_new), inv


def _sel_lane(x, h):
    lane = lax.broadcasted_iota(jnp.int32, x.shape, 1)
    return jnp.broadcast_to(jnp.sum(jnp.where(lane == h, x, 0.0), axis=1, keepdims=True), x.shape)


def _head(ref, h):
    return ref[:, h * HEAD_DIM:(h + 1) * HEAD_DIM]


def _delta_fwd(q, k, v, g, beta, comm=None):
    T = q.shape[0]
    H, N = q.shape[1] // HEAD_DIM, T // CHUNK

    def body(q_ref, k_ref, v_ref, g_ref, b_ref, o_ref, s_ref, inv_ref, S):
        @pl.when(pl.program_id(0) == 0)
        def _():
            S[...] = jnp.zeros_like(S)

        gv, bv = g_ref[...], b_ref[...]
        heads = lambda f: tuple(f(h) for h in range(H))
        S_in = heads(lambda h: S[h])
        for h in range(H):
            s_ref[h, 0] = S_in[h]
        (o, S_new), inv = _chunk_fn(heads(lambda h: _head(q_ref, h)), heads(lambda h: _head(k_ref, h)),
                                    heads(lambda h: _head(v_ref, h)), heads(lambda h: _sel_lane(gv, h)),
                                    heads(lambda h: _sel_lane(bv, h)), S_in)
        for h in range(H):
            o_ref[:, h * HEAD_DIM:(h + 1) * HEAD_DIM] = o[h]
            inv_ref[h, 0] = inv[h]
            S[h] = S_new[h]

    blk = pl.BlockSpec((CHUNK, H * HEAD_DIM), lambda n: (n, 0))
    gblk = pl.BlockSpec((CHUNK, LANES), lambda n: (n, 0))
    outs, comm_outs = _call(
        body, name="delta_fwd", grid=(N,), in_specs=[blk, blk, blk, gblk, gblk],
        out_specs=[blk, pl.BlockSpec((H, 1, HEAD_DIM, HEAD_DIM), lambda n: (0, n, 0, 0)),
                   pl.BlockSpec((H, 1, CHUNK, CHUNK), lambda n: (0, n, 0, 0))],
        out_shape=[jax.ShapeDtypeStruct((T, H * HEAD_DIM), F32), jax.ShapeDtypeStruct((H, N, HEAD_DIM, HEAD_DIM), F32),
                   jax.ShapeDtypeStruct((H, N, CHUNK, CHUNK), F32)],
        scratch_shapes=[pltpu.VMEM((H, HEAD_DIM, HEAD_DIM), F32)],
        semantics=("arbitrary",), args=(q, k, v, g, beta), comm=comm)
    return outs[0], outs[1], outs[2], comm_outs


def _delta_bwd(q, k, v, g, beta, S0, inv, do, comm=None):
    T = q.shape[0]
    H, N = q.shape[1] // HEAD_DIM, T // CHUNK

    def body(q_ref, k_ref, v_ref, g_ref, b_ref, s_ref, inv_ref, do_ref, dq_ref, dk_ref, dv_ref, dg_ref, db_ref, dS):
        @pl.when(pl.program_id(0) == 0)
        def _():
            dS[...] = jnp.zeros_like(dS)

        gv, bv = g_ref[...], b_ref[...]
        heads = lambda f: tuple(f(h) for h in range(H))
        _, vjp, _ = jax.vjp(_chunk_fn, heads(lambda h: _head(q_ref, h)), heads(lambda h: _head(k_ref, h)),
                            heads(lambda h: _head(v_ref, h)), heads(lambda h: _sel_lane(gv, h)), heads(lambda h: _sel_lane(bv, h)),
                            heads(lambda h: s_ref[h, 0]), heads(lambda h: inv_ref[h, 0]), has_aux=True)
        dq, dk, dv, dgB, dbB, dS_prev, _ = vjp((heads(lambda h: _head(do_ref, h)), heads(lambda h: dS[h])))
        for h in range(H):
            sl = slice(h * HEAD_DIM, (h + 1) * HEAD_DIM)
            dq_ref[:, sl] = dq[h]
            dk_ref[:, sl] = dk[h]
            dv_ref[:, sl] = dv[h]
            dg_ref[h] = dgB[h]
            db_ref[h] = dbB[h]
            dS[h] = dS_prev[h]

    blk = pl.BlockSpec((CHUNK, H * HEAD_DIM), lambda n: (N - 1 - n, 0))
    gblk = pl.BlockSpec((CHUNK, LANES), lambda n: (N - 1 - n, 0))
    hblk = pl.BlockSpec((H, CHUNK, LANES), lambda n: (0, N - 1 - n, 0))
    sd = jax.ShapeDtypeStruct
    outs, comm_outs = _call(
        body, name="delta_bwd", grid=(N,),
        in_specs=[blk, blk, blk, gblk, gblk, pl.BlockSpec((H, 1, HEAD_DIM, HEAD_DIM), lambda n: (0, N - 1 - n, 0, 0)),
                  pl.BlockSpec((H, 1, CHUNK, CHUNK), lambda n: (0, N - 1 - n, 0, 0)), blk],
        out_specs=[blk, blk, blk, hblk, hblk],
        out_shape=[sd((T, H * HEAD_DIM), F32)] * 3 + [sd((H, T, LANES), F32)] * 2,
        scratch_shapes=[pltpu.VMEM((H, HEAD_DIM, HEAD_DIM), F32)],
        semantics=("arbitrary",), args=(q, k, v, g, beta, S0, inv, do), comm=comm)
    return (*outs, comm_outs)


def _gnorm_fwd(o, proj, z_coff, gdn_t, DNW):
    T = o.shape[0]

    def fn(j, i, ov, zv, gv):
        def one(oh, zh, gh):
            r = lax.rsqrt(jnp.mean(oh * oh, axis=1, keepdims=True) + EPS)
            return oh * r * gh * (zh * _sigmoid(zh))
        return (_per_head(one, ov, zv, jnp.broadcast_to(gv, ov.shape)),)

    return _tiled(fn, T=T, C=DNW, ins=[("cur", o, None), ("cur", proj, lambda j: j + z_coff), ("row", gdn_t, None)],
                  out_dtypes=[BF16], cb=DNW, name="gnorm_fwd")[0]


def _gnorm_bwd(dymix, y_coff, o, proj, z_coff, gdn_t, DNW):
    T = o.shape[0]
    nh = DNW // HEAD_DIM

    def fn(j, i, dy, ov, zv, gv):
        dos, dzs, dgs = [], [], jnp.zeros((1, HEAD_DIM), F32)
        for h in range(nh):
            sl = slice(h * HEAD_DIM, (h + 1) * HEAD_DIM)
            dyh, oh, zh, gh = dy[:, sl].astype(F32), ov[:, sl], zv[:, sl], gv[:, sl]
            r = lax.rsqrt(jnp.mean(oh * oh, axis=1, keepdims=True) + EPS)
            on = oh * r
            sg = _sigmoid(zh)
            sz = zh * sg
            dzs.append(dyh * on * gh * (sg * (1.0 + zh * (1.0 - sg))))
            don = dyh * gh * sz
            dos.append(r * (don - on * jnp.mean(don * on, axis=1, keepdims=True)))
            dgs = dgs + jnp.sum(dyh * on * sz, axis=0, keepdims=True)
        cat = (lambda xs: xs[0] if nh == 1 else jnp.concatenate(xs, axis=1))
        return cat(dos), cat(dzs), _row0(dgs)

    T_ = T
    nI = T_ // _tile(T_, 256, HALO)
    tb = T_ // nI
    specs_cb = DNW

    def body_wrap():
        def body(dy_ref, o_ref, z_ref, g_ref, do_ref, dz_ref, dg_ref):
            i = pl.program_id(0)
            d_o, d_z, d_g = fn(0, i, dy_ref[...], o_ref[...], z_ref[...], g_ref[...])
            do_ref[...] = d_o
            dz_ref[...] = d_z.astype(dz_ref.dtype)

            @pl.when(i == 0)
            def _():
                dg_ref[...] = d_g

            @pl.when(i > 0)
            def _():
                dg_ref[...] += d_g

        return pl.pallas_call(
            body, name="gnorm_bwd", grid=(nI,),
            in_specs=[pl.BlockSpec((tb, specs_cb), lambda i: (i, y_coff)), pl.BlockSpec((tb, specs_cb), lambda i: (i, 0)),
                      pl.BlockSpec((tb, specs_cb), lambda i: (i, z_coff)), pl.BlockSpec((1, specs_cb), lambda i: (0, 0))],
            out_specs=[pl.BlockSpec((tb, specs_cb), lambda i: (i, 0)), pl.BlockSpec((tb, specs_cb), lambda i: (i, 0)),
                       pl.BlockSpec((HALO, HEAD_DIM), lambda i: (0, 0))],
            out_shape=[jax.ShapeDtypeStruct((T_, DNW), F32), jax.ShapeDtypeStruct((T_, DNW), BF16),
                       jax.ShapeDtypeStruct((HALO, HEAD_DIM), F32)],
            compiler_params=pltpu.CompilerParams(dimension_semantics=("arbitrary",), vmem_limit_bytes=VMEM_LIMIT),
        )(dymix, o, proj, gdn_t)

    return body_wrap()


def _ffn_fwd(up_g, up_v, w_g, w_v, cb):
    T, F = up_g.shape

    def fn(j, i, ug, uv, wg, wv):
        cg = _own(_conv_causal(ug, wg))
        cv = _own(_conv_causal(uv, wv))
        return (cg * _sigmoid(cg) * cv,)

    return _tiled(fn, T=T, C=F, ins=[("ext", up_g, None), ("ext", up_v, None), ("row", w_g, None), ("row", w_v, None)],
                  out_dtypes=[BF16], cb=cb, name="ffn_fwd")[0]


def _ffn_bwd(dact, up_g, up_v, w_g, w_v, cb):
    T, F = up_g.shape
    K = w_g.shape[0]

    def fn(j, i, da, ug, uv, wg, wv):
        cg = _conv_causal(ug, wg)
        cv = _conv_causal(uv, wv)
        sg = _sigmoid(cg)
        dgate = da * cv * (sg * (1.0 + cg * (1.0 - sg)))
        dval = da * (cg * sg)
        return (_own(_conv_anti(dgate, wg)), _own(_conv_anti(dval, wv)), _conv_dw(dgate, ug, K), _conv_dw(dval, uv, K))

    return _tiled(fn, T=T, C=F, ins=[("ext", dact, None), ("ext", up_g, None), ("ext", up_v, None), ("row", w_g, None),
                                      ("row", w_v, None)], out_dtypes=[BF16, BF16], acc_rows=[HALO, HALO], cb=cb, name="ffn_bwd")


def _ple_bwd(dx3, pp, pg):
    T, D = dx3.shape

    def fn(j, i, d, ppv, pgv):
        return d * ppv * pgv * (1.0 - pgv), d * pgv

    return _tiled(fn, T=T, C=D, ins=[("cur", dx3, None), ("cur", pp, None), ("cur", pg, None)],
                  out_dtypes=[BF16, BF16], cb=_tile(D, 512, LANES), name="ple_bwd")


def _wide(R, Cc, n_f32, unit=HALO):
    cb = Cc if (Cc % LANES or Cc <= 4096) else _tile(Cc, 2048, LANES)
    cap = max(unit, EW_VMEM_BUDGET // (2 * 4 * n_f32 * cb) // unit * unit)
    return _tile(R, cap, unit), cb


def _adamw(w, g, m, v, name):
    R, Cc = w.shape
    tb, cb = _wide(R, Cc, 7)
    c1 = 1.0 / (1.0 - ADAM_B1 ** ADAM_STEP)
    c2 = 1.0 / (1.0 - ADAM_B2 ** ADAM_STEP)

    def fn(j, i, wv, gv, mv, vv):
        m2 = ADAM_B1 * mv + (1.0 - ADAM_B1) * gv
        v2 = ADAM_B2 * vv + (1.0 - ADAM_B2) * (gv * gv)
        delta = -ADAM_LR * ((m2 * c1) / (jnp.sqrt(v2 * c2) + ADAM_EPS) + ADAM_WD * wv)
        return delta, m2, v2

    return _tiled(fn, T=R, C=Cc, ins=[("cur", w, None), ("cur", g, None), ("cur", m, None), ("cur", v, None)],
                  out_dtypes=[F32, F32, F32], tb=tb, cb=cb, name=name)


def _sum_stack(st, name):
    S, R, Cc = st.shape
    cb = _tile(Cc, 512, LANES) if Cc % LANES == 0 else Cc

    def fn(j, i, sv):
        t = sv[0]
        for s in range(1, S):
            t = t + sv[s]
        return (t,)

    return _tiled(fn, T=R, C=Cc, ins=[("stack", st, None)], out_dtypes=[F32], cb=cb, name=name)[0]


ANY = pl.BlockSpec(memory_space=pl.ANY)


def _place():
    x, y, c = lax.axis_index("x"), lax.axis_index("y"), lax.axis_index("c")
    return x, y, c, 2 * x + y


def _chip_dev(s, c):
    return (s // 2, s % 2, c)


class _Comm:
    def __init__(self, ins, out_shapes, sems, start, wait, aliases=None):
        self.ins, self.out_shapes, self.sems = list(ins), list(out_shapes), list(sems)
        self.start, self.wait, self.aliases = start, wait, dict(aliases or {})


def _merge(*comms):
    offs, i, o, s = [], 0, 0, 0
    for cm in comms:
        offs.append((i, o, s))
        i, o, s = i + len(cm.ins), o + len(cm.out_shapes), s + len(cm.sems)

    def part(refs, k, cm):
        i0, o0, s0 = offs[k]
        return refs[0][i0:i0 + len(cm.ins)], refs[1][o0:o0 + len(cm.out_shapes)], refs[2][s0:s0 + len(cm.sems)]

    def start(*refs):
        for k, cm in enumerate(comms):
            cm.start(*part(refs, k, cm))

    def wait(*refs):
        for k, cm in enumerate(comms):
            cm.wait(*part(refs, k, cm))

    aliases = {}
    for k, cm in enumerate(comms):
        for a, b in cm.aliases.items():
            aliases[offs[k][0] + a] = offs[k][1] + b
    return _Comm([a for cm in comms for a in cm.ins], [a for cm in comms for a in cm.out_shapes],
                 [a for cm in comms for a in cm.sems], start, wait, aliases)


def _call(body, *, name, grid, in_specs, out_specs, out_shape, scratch_shapes, semantics, args, comm=None):
    if comm is None:
        outs = pl.pallas_call(
            body, name=name, grid=grid, in_specs=in_specs, out_specs=out_specs, out_shape=out_shape,
            scratch_shapes=list(scratch_shapes),
            compiler_params=pltpu.CompilerParams(dimension_semantics=semantics, vmem_limit_bytes=VMEM_LIMIT))(*args)
        return list(outs), []
    n_in, n_out, n_scr = len(in_specs), len(out_specs), len(scratch_shapes)
    ci, co = len(comm.ins), len(comm.out_shapes)

    def wrapped(*refs):
        r = 0
        ins, r = refs[r:r + n_in], r + n_in
        cins, r = refs[r:r + ci], r + ci
        outs, r = refs[r:r + n_out], r + n_out
        couts, r = refs[r:r + co], r + co
        scr, r = refs[r:r + n_scr], r + n_scr
        csems = refs[r:]
        ids = [pl.program_id(a) for a in range(len(grid))]
        first, last = ids[0] == 0, ids[0] == grid[0] - 1
        for a in range(1, len(grid)):
            first = jnp.logical_and(first, ids[a] == 0)
            last = jnp.logical_and(last, ids[a] == grid[a] - 1)

        @pl.when(first)
        def _():
            comm.start(cins, couts, csems)

        body(*ins, *outs, *scr)

        @pl.when(last)
        def _():
            comm.wait(cins, couts, csems)

    outs = pl.pallas_call(
        wrapped, name=name, grid=grid, in_specs=list(in_specs) + [ANY] * ci, out_specs=list(out_specs) + [ANY] * co,
        out_shape=list(out_shape) + comm.out_shapes, scratch_shapes=list(scratch_shapes) + comm.sems,
        input_output_aliases={n_in + a: n_out + b for a, b in comm.aliases.items()},
        compiler_params=pltpu.CompilerParams(dimension_semantics=("arbitrary",) * len(grid), vmem_limit_bytes=VMEM_LIMIT),
    )(*args, *comm.ins)
    return list(outs[:n_out]), list(outs[n_out:])


def _run_comm(comm, name):
    ci, co = len(comm.ins), len(comm.out_shapes)

    def body(*refs):
        cins, couts, csems = refs[:ci], refs[ci:ci + co], refs[ci + co:]
        comm.start(cins, couts, csems)
        comm.wait(cins, couts, csems)

    outs = pl.pallas_call(body, name=name, in_specs=[ANY] * ci, out_specs=[ANY] * co, out_shape=comm.out_shapes,
                          scratch_shapes=comm.sems, input_output_aliases=comm.aliases)(*comm.ins)
    return list(outs)


def _ag_comm(shard, land=None, q=0, nq=1):
    two, R2, Cc = shard.shape
    rows = pl.ds(q * (R2 // nq), R2 // nq)
    DMA = pltpu.SemaphoreType.DMA

    def copies(ins, outs, sems, which):
        sh, out = ins[0], outs[0]
        send1, recv1, send2, recv2, send0, recv0 = sems
        x, y, c, s = _place()
        sib = (x, y, 1 - c)
        rc = pltpu.make_async_remote_copy
        if which == "first":
            return [rc(sh.at[c, rows], out.at[s, c, rows], send1.at[m - 1], recv1.at[m - 1],
                       device_id=_chip_dev(s ^ m, c), device_id_type=MESH) for m in range(1, 4)]
        if which == "own":
            return [rc(sh.at[h, rows], out.at[s, h, rows], send0.at[h], recv0.at[h], device_id=sib, device_id_type=MESH)
                    for h in range(2)]
        if which == "landed":
            return [rc(sh.at[c, rows], out.at[s ^ m, c, rows], send1.at[m - 1], recv1.at[m - 1], device_id=sib,
                       device_id_type=MESH) for m in range(1, 4)]
        half = c if which == "passed" else 1 - c
        return [rc(out.at[s ^ m, half, rows], out.at[s ^ m, half, rows], send2.at[m - 1], recv2.at[m - 1], device_id=sib,
                   device_id_type=MESH) for m in range(1, 4)]

    def start(ins, outs, sems):
        for cp in copies(ins, outs, sems, "first") + copies(ins, outs, sems, "own"):
            cp.start()

    def wait(ins, outs, sems):
        passed = copies(ins, outs, sems, "passed")
        for lan, pas in zip(copies(ins, outs, sems, "landed"), passed):
            lan.wait_recv()
            pas.start()
        for cp in copies(ins, outs, sems, "handed"):
            cp.wait_recv()
        for cp in copies(ins, outs, sems, "own"):
            cp.wait()
        for cp in copies(ins, outs, sems, "first") + passed:
            cp.wait_send()

    return _Comm([shard] + ([land] if land is not None else []), [jax.ShapeDtypeStruct((4, two, R2, Cc), shard.dtype)],
                 [DMA((3,)), DMA((3,)), DMA((3,)), DMA((3,)), DMA((2,)), DMA((2,))], start, wait,
                 {1: 0} if land is not None else None)


def _a2a_comm(S1, q=0, nq=1, land=None):
    S4, R2, Cc = S1.shape
    rows = pl.ds(q * (R2 // nq), R2 // nq)
    DMA = pltpu.SemaphoreType.DMA

    def copies(ins, outs, sems):
        x, y, c, s = _place()
        return [pltpu.make_async_remote_copy(ins[0].at[s ^ m, rows], outs[0].at[m - 1, rows], sems[0].at[m - 1],
                                             sems[1].at[m - 1], device_id=_chip_dev(s ^ m, c), device_id_type=MESH)
                for m in range(1, 4)]

    def start(ins, outs, sems):
        for cp in copies(ins, outs, sems):
            cp.start()

    def wait(ins, outs, sems):
        for cp in copies(ins, outs, sems):
            cp.wait()

    return _Comm([S1] + ([land] if land is not None else []), [jax.ShapeDtypeStruct((3, R2, Cc), S1.dtype)],
                 [DMA((3,)), DMA((3,))], start, wait, {1: 0} if land is not None else None)


def _halves(G):
    return G.reshape(G.shape[0], 2, G.shape[1] // 2, G.shape[2])


def _swap_comm(piece):
    n, two, R2, Cc = piece.shape
    DMA = pltpu.SemaphoreType.DMA

    def copies(ins, outs, sems):
        x, y, c, s = _place()
        return [pltpu.make_async_remote_copy(ins[0].at[t, 1 - c], outs[0].at[t], sems[0].at[t], sems[1].at[t],
                                             device_id=(x, y, 1 - c), device_id_type=MESH) for t in range(n)]

    def start(ins, outs, sems):
        for cp in copies(ins, outs, sems):
            cp.start()

    def wait(ins, outs, sems):
        for cp in copies(ins, outs, sems):
            cp.wait()

    return _Comm([piece], [jax.ShapeDtypeStruct((n, R2, Cc), piece.dtype)], [DMA((n,)), DMA((n,))], start, wait)


def _add_half(pieces, As, cidx, name):
    R2, Cc = pieces[0].shape[2:]
    S4 = sum(pc.shape[0] for pc in pieces)
    tb, cb = _wide(R2, Cc, 3, 2 * HALO)
    nI, nJ = R2 // tb, Cc // cb

    def body(c_ref, g_ref, a_ref, *rest):
        rest[-1][...] = (g_ref[0, 0] + a_ref[0]).astype(BF16)

    out, t0 = None, 0
    for k, (pc, A) in enumerate(zip(pieces, As)):
        grid_spec = pltpu.PrefetchScalarGridSpec(
            num_scalar_prefetch=1, grid=(pc.shape[0], nI, nJ),
            in_specs=[pl.BlockSpec((1, 1, tb, cb), lambda t, i, j, c_ref: (t, c_ref[0], i, j)),
                      pl.BlockSpec((1, tb, cb), lambda t, i, j, c_ref: (t, i, j))] + ([ANY] if k else []),
            out_specs=pl.BlockSpec((tb, cb), lambda t, i, j, c_ref, t0=t0: ((t0 + t) * nI + i, j)))
        out = pl.pallas_call(
            functools.partial(body), name=f"{name}{k}", grid_spec=grid_spec, out_shape=jax.ShapeDtypeStruct((S4 * R2, Cc), BF16),
            input_output_aliases={3: 0} if k else {},
            compiler_params=pltpu.CompilerParams(dimension_semantics=("parallel", "parallel", "parallel"),
                                                 vmem_limit_bytes=VMEM_LIMIT),
        )(*((cidx, pc, A) + ((out,) if k else ())))
        t0 += pc.shape[0]
    return out.reshape(S4, R2, Cc)


def _add_own(S1, B, chip_idx, name):
    S4, R2, Cc = S1.shape
    tb, cb = _wide(R2, Cc, 3, 2 * HALO)

    def body(s_idx, s_ref, b_ref, o_ref):
        o_ref[...] = ((s_ref[0].astype(F32) + b_ref[0].astype(F32)) + b_ref[1].astype(F32)) + b_ref[2].astype(F32)

    grid_spec = pltpu.PrefetchScalarGridSpec(
        num_scalar_prefetch=1, grid=(R2 // tb, Cc // cb),
        in_specs=[pl.BlockSpec((1, tb, cb), lambda i, j, s_idx: (s_idx[0], i, j)),
                  pl.BlockSpec((3, tb, cb), lambda i, j, s_idx: (0, i, j))],
        out_specs=pl.BlockSpec((tb, cb), lambda i, j, s_idx: (i, j)))
    return pl.pallas_call(body, name=name, grid_spec=grid_spec, out_shape=jax.ShapeDtypeStruct((R2, Cc), F32),
                          compiler_params=pltpu.CompilerParams(dimension_semantics=("parallel", "parallel"),
                                                               vmem_limit_bytes=VMEM_LIMIT))(chip_idx, S1, B)


def _sibling_send(Hs, name):
    R2, Cc = Hs.shape

    def body(h_ref, out_ref, send, recv):
        x, y, c, s = _place()
        cp = pltpu.make_async_remote_copy(h_ref, out_ref, send, recv, device_id=(x, y, 1 - c), device_id_type=MESH)
        cp.start()
        cp.wait()

    return pl.pallas_call(
        body, name=name, in_specs=[ANY], out_specs=ANY, out_shape=jax.ShapeDtypeStruct((R2, Cc), Hs.dtype),
        scratch_shapes=[pltpu.SemaphoreType.DMA, pltpu.SemaphoreType.DMA],
    )(Hs)


def _gather_all(buf, name):
    R, Cc = buf.shape

    def body(b_ref, out_ref, send, recv, local):
        x, y, c, s = _place()
        d = 2 * s + c
        mine = pltpu.make_async_copy(b_ref, out_ref.at[d], local)
        mine.start()
        cps = []
        for m in range(1, 8):
            t = d ^ m
            cp = pltpu.make_async_remote_copy(b_ref, out_ref.at[d], send.at[m - 1], recv.at[m - 1],
                                              device_id=(t // 4, (t // 2) % 2, t % 2), device_id_type=MESH)
            cp.start()
            cps.append(cp)
        for cp in cps:
            cp.wait()
        mine.wait()

    return pl.pallas_call(
        body, name=name, in_specs=[ANY], out_specs=ANY, out_shape=jax.ShapeDtypeStruct((8, R, Cc), buf.dtype),
        scratch_shapes=[pltpu.SemaphoreType.DMA((7,)), pltpu.SemaphoreType.DMA((7,)), pltpu.SemaphoreType.DMA],
    )(buf)


def _finish_shard(S1, B, cidx, chip_idx, name):
    Hs = _add_own(S1, B, chip_idx, name + "_sum")
    Ho = _sibling_send(Hs, name + "_gather")
    lo = jnp.where(cidx[0] == 0, Hs, Ho)
    hi = jnp.where(cidx[0] == 0, Ho, Hs)
    return jnp.concatenate([lo, hi], axis=0)


def _pack_rows(vs):
    flat = jnp.concatenate([v.reshape(-1) for v in vs])
    n = flat.shape[0]
    rows = -(-n // (LANES * 2 * HALO)) * 2 * HALO
    return jnp.pad(flat, (0, rows * LANES - n)).reshape(rows, LANES)


def _unpack_rows(buf, shapes):
    flat = buf.reshape(-1)
    outs, o = [], 0
    for shp in shapes:
        n = 1
        for d in shp:
            n *= d
        outs.append(flat[o:o + n].reshape(shp))
        o += n
    return outs


def kernel(x, p, norm_mix_g, w_in, conv_a_w, conv_qkv_w, a_log, dt_bias, dn_norm_g, w_out, norm_ffn_g, w_up, conv_ffn_w, w_down, norm_ple_g, w_ple_gate, w_ple_proj, final_norm_g, loss_target, m_norm_mix_g, m_w_in, m_conv_a_w, m_conv_qkv_w, m_a_log, m_dt_bias, m_dn_norm_g, m_w_out, m_norm_ffn_g, m_w_up, m_conv_ffn_w, m_w_down, m_norm_ple_g, m_w_ple_gate, m_w_ple_proj, m_final_norm_g, v_norm_mix_g, v_w_in, v_conv_a_w, v_conv_qkv_w, v_a_log, v_dt_bias, v_dn_norm_g, v_w_out, v_norm_ffn_g, v_w_up, v_conv_ffn_w, v_w_down, v_norm_ple_g, v_w_ple_gate, v_w_ple_proj, v_final_norm_g):
    xs = x[0]
    ps = p[0, 0]
    tgt = loss_target[0]
    T, D = xs.shape
    H = a_log.shape[-1]
    DNW = H * HEAD_DIM
    CW = conv_a_w.shape[-1] * 4
    F = w_down.shape[1] * 4
    PD = ps.shape[-1]
    IN_MAIN = 3 * CW + 4 * DNW
    IN_COLS = IN_MAIN + 2 * H
    assert w_in.shape[-1] * 4 == IN_COLS and CW + DNW == D and 2 * H <= LANES
    cb = _tile(min(CW, DNW), 512, LANES)
    while F % cb:
        cb -= LANES
    cidx = lax.axis_index("c").astype(jnp.int32).reshape(1)
    chip = 2 * lax.axis_index("x") + lax.axis_index("y")

    def halves(w):
        sh = w[0].astype(BF16)
        return sh.reshape(2, sh.shape[0] // 2, sh.shape[1])

    def whole(land):
        return land.reshape(4, 2 * land.shape[2], land.shape[3])

    def cols(g4):
        return jnp.transpose(g4, (1, 0, 2)).reshape(g4.shape[1], 4 * g4.shape[2])

    def rows(g4):
        return g4.reshape(4 * g4.shape[1], g4.shape[2])

    conv_shapes = [conv_a_w[0].shape, conv_qkv_w[0].shape, conv_ffn_w[0].shape]
    cpack = _pack_rows([conv_a_w[0], conv_qkv_w[0], conv_ffn_w[0]])
    sh_in, sh_out, sh_up, sh_down, sh_pg, sh_pp = (halves(w) for w in (w_in, w_out, w_up, w_down, w_ple_gate, w_ple_proj))
    l_in, cg = _run_comm(_merge(_ag_comm(sh_in), _ag_comm(cpack.reshape(2, cpack.shape[0] // 2, LANES))), "ag_w_in_conv")
    w_in_f = cols(whole(l_in))
    w_in_main = w_in_f[:, :IN_MAIN]
    w_in_small = jnp.pad(w_in_f[:, IN_MAIN:], ((0, 0), (0, LANES - 2 * H)))
    cg = cg.reshape(4, cpack.shape[0], LANES)
    parts = [_unpack_rows(cg[t], conv_shapes) for t in range(4)]
    cw_a = jnp.concatenate([parts[t][0] for t in range(4)], axis=1)
    cw_qkv = jnp.concatenate([parts[t][1] for t in range(4)], axis=1)
    cw_ffn = jnp.concatenate([parts[t][2] for t in range(4)], axis=1)
    cw_q, cw_k, cw_v = cw_qkv[:, :DNW], cw_qkv[:, DNW:2 * DNW], cw_qkv[:, 2 * DNW:]
    cw_fg, cw_fv = cw_ffn[:, :F], cw_ffn[:, F:]
    pad_row = lambda v: jnp.pad(v, ((0, 0), (0, LANES - v.shape[1])))
    a_log_row, dt_row = pad_row(a_log), pad_row(dt_bias)
    gdn_t = jnp.tile(dn_norm_g, (1, H))
    gfin = final_norm_g.reshape(1, D)

    h1 = _rms_fwd(xs, norm_mix_g, "rms1")
    proj, (l_up,) = _mm(h1, w_in_main, mode="nn", out_dtypes=[F32], name="mm_proj", comm=_ag_comm(sh_up, q=0, nq=2))
    small = _mm(h1, w_in_small, mode="nn", out_dtypes=[F32], name="mm_small")
    ya = _ga_fwd(proj, cw_a, CW, cb)
    nq = 3 * CW // cb
    nd = DNW // cb
    qn = _qkv_fwd(proj, cw_q, nq, True, DNW, cb, "q_fwd")
    kn = _qkv_fwd(proj, cw_k, nq + nd, True, DNW, cb, "k_fwd")
    vs = _qkv_fwd(proj, cw_v, nq + 2 * nd, False, DNW, cb, "v_fwd")
    g, beta = _gb_fwd(small, a_log_row, dt_row, H)
    o, S0, inv_c, (l_up, l_out) = _delta_fwd(qn, kn, vs, g, beta, comm=_merge(_ag_comm(sh_up, l_up, q=1, nq=2), _ag_comm(sh_out)))
    w_out_f = rows(whole(l_out))
    w_out_a, w_out_b = w_out_f[:CW], w_out_f[CW:]
    w_up_4 = whole(l_up)
    z_coff = (3 * CW + 3 * DNW) // DNW
    assert (3 * CW + 3 * DNW) % DNW == 0 and CW % DNW == 0
    yb = _gnorm_fwd(o, proj, z_coff, gdn_t, DNW)
    add = lambda acc, r: (r + acc,)
    x1 = _mm(ya, w_out_a, mode="nn", out_dtypes=[F32], epi=add, extras=[xs], name="mm_out_a")
    x1 = _mm(yb, w_out_b, mode="nn", out_dtypes=[F32], epi=add, extras=[x1], name="mm_out_b")
    h2 = _rms_fwd(x1, norm_ffn_g, "rms2")
    up_g, (l_down,) = _mm(h2, w_up_4, mode="nn", b_split=(0, 2), out_dtypes=[F32], name="mm_up_g",
                          comm=_ag_comm(sh_down, q=0, nq=2))
    up_v, (l_down,) = _mm(h2, w_up_4, mode="nn", b_split=(2, 2), out_dtypes=[F32], name="mm_up_v",
                          comm=_ag_comm(sh_down, l_down, q=1, nq=2))
    w_down_f = rows(whole(l_down))
    act = _ffn_fwd(up_g, up_v, cw_fg, cw_fv, cb)
    x2, (l_pg, l_pp) = _mm(act, w_down_f, mode="nn", out_dtypes=[F32], epi=add, extras=[x1], name="mm_down",
                           comm=_merge(_ag_comm(sh_pg), _ag_comm(sh_pp)))
    w_pg_f = rows(whole(l_pg))
    w_pp_4 = whole(l_pp)
    h3 = _rms_fwd(x2, norm_ple_g, "rms3")
    pp = _mm(ps, w_pp_4, mode="nn", b_split=(0, 4), out_dtypes=[F32], name="mm_pp")

    def ple_epi(acc, x2v, ppv):
        pg = _sigmoid(acc)
        return x2v + pg * ppv, pg

    x3, pg = _mm(h3, w_pg_f, mode="nn", out_dtypes=[F32, F32], epi=ple_epi, extras=[x2, pp], name="mm_pg")

    dx3, fin = _final_fb(x3, tgt, gfin)
    loss = lax.psum(jnp.sum(fin[1]), ("x", "y", "c"))
    d_gfin = fin[0:1]
    dpg, dpp = _ple_bwd(dx3, pp, pg)
    def split_cols(dW):
        R, C4 = dW.shape
        return jnp.transpose(dW.reshape(R, 4, C4 // 4), (1, 0, 2))

    def split_rows(dW):
        return dW.reshape(4, dW.shape[0] // 4, dW.shape[1])

    dW_pp = _mm(ps, dpp, mode="tn", out_split=4, out_dtypes=[F32], name="mm_dw_pp")
    dW_pg = _mm(h3, dpg, mode="tn", out_dtypes=[F32], name="mm_dw_pg")
    P_pp, P_pg = _halves(dW_pp), _halves(split_rows(dW_pg))
    dh3, (A_pp, A_pg) = _mm(dpg, w_pg_f, mode="nt", out_dtypes=[F32], name="mm_dh3",
                            comm=_merge(_swap_comm(P_pp), _swap_comm(P_pg)))
    S_pp = _add_half([P_pp], [A_pp], cidx, "rs_w_pp_add")
    S_pg = _add_half([P_pg], [A_pg], cidx, "rs_w_pg_add")
    dx2, dx2_b, d_gple = _rms_bwd(dh3, x2, norm_ple_g, dx3, "rms3_bwd")
    dW_down, (B_pp, B_pg) = _mm(act, dx2_b, mode="tn", out_dtypes=[F32], name="mm_dw_down",
                                comm=_merge(_a2a_comm(S_pp), _a2a_comm(S_pg)))
    P_down = _halves(split_rows(dW_down))
    dact, (A_down,) = _mm(dx2_b, w_down_f, mode="nt", out_dtypes=[F32], name="mm_dact", comm=_swap_comm(P_down))
    S_down = _add_half([P_down], [A_down], cidx, "rs_w_down_add")
    dup_g, dup_v, dcw_fg, dcw_fv = _ffn_bwd(dact, up_g, up_v, cw_fg, cw_fv, cb)
    dW_up_g, (B_down,) = _mm(h2, dup_g, mode="tn", out_split=2, out_dtypes=[F32], name="mm_dw_up_g", comm=_a2a_comm(S_down))
    P_ug = _halves(dW_up_g)
    dW_up_v, (A_ug,) = _mm(h2, dup_v, mode="tn", out_split=2, out_dtypes=[F32], name="mm_dw_up_v", comm=_swap_comm(P_ug))
    P_uv = _halves(dW_up_v)
    dh2, (A_uv,) = _mm(dup_g, w_up_4, mode="nt", b_split=(0, 2), out_dtypes=[F32], name="mm_dh2_g", comm=_swap_comm(P_uv))
    S_up = _add_half([P_ug, P_uv], [A_ug, A_uv], cidx, "rs_w_up_add")
    dh2 = _mm(dup_v, w_up_4, mode="nt", b_split=(2, 2), out_dtypes=[F32], epi=add, extras=[dh2], name="mm_dh2_v")
    dx1, dx1_b, d_gffn = _rms_bwd(dh2, x1, norm_ffn_g, dx2, "rms2_bwd")
    dW_out_a = _mm(ya, dx1_b, mode="tn", out_dtypes=[F32], name="mm_dw_out_a")
    dW_out_b = _mm(yb, dx1_b, mode="tn", out_dtypes=[F32], name="mm_dw_out_b")
    P_oa, P_ob = _halves(dW_out_a.reshape(-1, D // 4, D)), _halves(dW_out_b.reshape(-1, D // 4, D))
    dymix, (A_oa, A_ob) = _mm(dx1_b, w_out_f, mode="nt", out_dtypes=[F32], name="mm_dymix",
                              comm=_merge(_swap_comm(P_oa), _swap_comm(P_ob)))
    S_out = _add_half([P_oa, P_ob], [A_oa, A_ob], cidx, "rs_w_out_add")
    dax, dab, dac, dcw_a = _ga_bwd(dymix, proj, cw_a, CW, cb)
    do, dz, d_gdn = _gnorm_bwd(dymix, CW // DNW, o, proj, z_coff, gdn_t, DNW)
    dqn, dkn, dvs, dgB, dbB, (B_up, B_out) = _delta_bwd(qn, kn, vs, g, beta, S0, inv_c, do,
                                                        comm=_merge(_a2a_comm(S_up), _a2a_comm(S_out)))
    dq_pre, dcw_q = _qkv_bwd(dqn, proj, cw_q, nq, True, DNW, cb, "q_bwd")
    dk_pre, dcw_k = _qkv_bwd(dkn, proj, cw_k, nq + nd, True, DNW, cb, "k_bwd")
    dv_pre, dcw_v = _qkv_bwd(dvs, proj, cw_v, nq + 2 * nd, False, DNW, cb, "v_bwd")
    dsmall, d_ab = _gb_bwd(dgB, dbB, small, g, beta, a_log_row, dt_row, H)
    dproj = jnp.concatenate([dax, dab, dac, dq_pre, dk_pre, dv_pre, dz], axis=1)
    dW_in_main = _mm(h1, dproj, mode="tn", out_dtypes=[F32], name="mm_dw_in")
    dW_in_small = _mm(h1, dsmall, mode="tn", out_dtypes=[F32], name="mm_dw_in_small")
    P_in = _halves(split_cols(jnp.concatenate([dW_in_main, dW_in_small[:, :2 * H]], axis=1)))
    (A_in,) = _run_comm(_swap_comm(P_in), "rs_w_in_swap")
    S_in = _add_half([P_in], [A_in], cidx, "rs_w_in_add")
    dh1, (B_in,) = _mm(dproj, w_in_main, mode="nt", out_dtypes=[F32], name="mm_dh1", comm=_a2a_comm(S_in))
    dh1 = _mm(dsmall, w_in_small, mode="nt", out_dtypes=[F32], epi=add, extras=[dh1], name="mm_dh1_small")
    dx, _, d_gmix = _rms_bwd(dh1, xs, norm_mix_g, dx1, "rms1_bwd")

    chip_idx = chip.astype(jnp.int32).reshape(1)

    def update(S1, B, w, m, v, name):
        gr = _finish_shard(S1, B, cidx, chip_idx, "rs_" + name)
        R, Cc = gr.shape
        if Cc % LANES == 0:
            delta, m2, v2 = _adamw(w[0], gr, m[0], v[0], "adamw_" + name)
            return gr[None], delta[None], m2[None], v2[None]
        flat = lambda a: jnp.transpose(a).reshape(R * Cc // LANES, LANES)
        back = lambda a: jnp.transpose(a.reshape(Cc, R))[None]
        gf = flat(gr)
        delta, m2, v2 = _adamw(flat(w[0]), gf, flat(m[0]), flat(v[0]), "adamw_" + name)
        return back(gf), back(delta), back(m2), back(v2)

    big = {
        "w_in": update(S_in, B_in, w_in, m_w_in, v_w_in, "w_in"),
        "w_out": update(S_out, B_out, w_out, m_w_out, v_w_out, "w_out"),
        "w_up": update(S_up, B_up, w_up, m_w_up, v_w_up, "w_up"),
        "w_down": update(S_down, B_down, w_down, m_w_down, v_w_down, "w_down"),
        "w_ple_gate": update(S_pg, B_pg, w_ple_gate, m_w_ple_gate, v_w_ple_gate, "w_pg"),
        "w_ple_proj": update(S_pp, B_pp, w_ple_proj, m_w_ple_proj, v_w_ple_proj, "w_pp"),
    }

    small_grads = [d_gmix[0:1], dcw_a[:cw_a.shape[0]], jnp.concatenate([dcw_q, dcw_k, dcw_v], axis=1)[:cw_qkv.shape[0]],
                   d_ab[0:1, :H], d_ab[1:2, :H], d_gdn[0:1], d_gffn[0:1],
                   jnp.concatenate([dcw_fg, dcw_fv], axis=1)[:cw_ffn.shape[0]], d_gple[0:1], d_gfin]
    small_shapes = [v.shape for v in small_grads]
    gpack = _pack_rows(small_grads)
    gsum = _sum_stack(_gather_all(gpack, "ag_small"), "sum_small")
    (g_gmix, g_cwa, g_cwqkv, g_alog, g_dt, g_gdn, g_gffn, g_cwffn, g_gple, g_gfin) = _unpack_rows(gsum, small_shapes)

    def my_cols(v):
        Cc = v.shape[1] // 4
        return lax.dynamic_slice_in_dim(v, chip * Cc, Cc, axis=1)

    g_small = [g_gmix, my_cols(g_cwa), my_cols(g_cwqkv), g_alog, g_dt, g_gdn, g_gffn, my_cols(g_cwffn), g_gple, g_gfin]
    w_small = [norm_mix_g, conv_a_w[0], conv_qkv_w[0], a_log, dt_bias, dn_norm_g, norm_ffn_g, conv_ffn_w[0], norm_ple_g, gfin]
    m_small = [m_norm_mix_g, m_conv_a_w[0], m_conv_qkv_w[0], m_a_log, m_dt_bias, m_dn_norm_g, m_norm_ffn_g, m_conv_ffn_w[0],
               m_norm_ple_g, m_final_norm_g.reshape(1, D)]
    v_small = [v_norm_mix_g, v_conv_a_w[0], v_conv_qkv_w[0], v_a_log, v_dt_bias, v_dn_norm_g, v_norm_ffn_g, v_conv_ffn_w[0],
               v_norm_ple_g, v_final_norm_g.reshape(1, D)]
    shp = [v.shape for v in w_small]
    ds_, ms_, vs_ = _adamw(_pack_rows(w_small), _pack_rows(g_small), _pack_rows(m_small), _pack_rows(v_small), "adamw_small")
    out_shapes = [norm_mix_g.shape, conv_a_w.shape, conv_qkv_w.shape, a_log.shape, dt_bias.shape, dn_norm_g.shape,
                  norm_ffn_g.shape, conv_ffn_w.shape, norm_ple_g.shape, final_norm_g.shape]
    rs = lambda vals: [v.reshape(s) for v, s in zip(vals, out_shapes)]
    sg, sd_, sm_, sv_ = rs(g_small), rs(_unpack_rows(ds_, shp)), rs(_unpack_rows(ms_, shp)), rs(_unpack_rows(vs_, shp))
    names_small = ["norm_mix_g", "conv_a_w", "conv_qkv_w", "a_log", "dt_bias", "dn_norm_g", "norm_ffn_g", "conv_ffn_w",
                   "norm_ple_g", "final_norm_g"]
    res = {n: (sg[i], sd_[i], sm_[i], sv_[i]) for i, n in enumerate(names_small)}
    res.update(big)
    order = ["norm_mix_g", "w_in", "conv_a_w", "conv_qkv_w", "a_log", "dt_bias", "dn_norm_g", "w_out", "norm_ffn_g", "w_up",
             "conv_ffn_w", "w_down", "norm_ple_g", "w_ple_gate", "w_ple_proj", "final_norm_g"]
    return (loss, dx[None], *[res[n][0] for n in order], *[res[n][1] for n in order], *[res[n][2] for n in order],
            *[res[n][3] for n in order])
```

```python
import functools

import jax
import jax.numpy as jnp
from jax import lax
from jax.experimental import pallas as pl
from jax.experimental.pallas import tpu as pltpu

F32 = jnp.float32
BF16 = jnp.bfloat16
LANES = 128
HALO = 8
HEAD_DIM = 128
CHUNK = 64
EPS = 1e-6
VMEM_LIMIT = 56 * 1024 * 1024
MM_VMEM_BUDGET = 40 * 1024 * 1024
MM_STEP_BYTES = 1 << 20
EW_VMEM_BUDGET = 28 * 1024 * 1024
MESH = pl.DeviceIdType.MESH

ADAM_LR, ADAM_B1, ADAM_B2, ADAM_EPS, ADAM_WD, ADAM_STEP = 0.001, 0.9, 0.999, 1e-08, 0.01, 10


def _tile(n, cap, unit):
    if n <= cap:
        return n
    d = (cap // unit) * unit
    while d >= unit:
        if n % d == 0:
            return d
        d -= unit
    raise ValueError(f"no tile for {n} (cap {cap}, unit {unit})")


def _sigmoid(x):
    return 1.0 / (1.0 + jnp.exp(-x))


def _divisors(n, cap):
    ds = [d for d in range(cap // LANES * LANES, 0, -LANES) if n % d == 0]
    return [n] if (n <= cap or not ds) else ds


def _mm_tiles(M, N, K, n_unit, k_unit, a_bytes, n_blocks_mn, a_transposed):
    best = None
    for tm in _divisors(M, 1536):
        for tn in _divisors(n_unit, 1536):
            for tk in _divisors(k_unit, 4096):
                nk = K // tk
                vmem = 2 * tm * tk * a_bytes + 2 * tk * tn * 2 + 2 * 4 * tm * tn * n_blocks_mn + (4 * tm * tn if nk > 1 else 0)
                if vmem > MM_VMEM_BUDGET:
                    continue
                steps = (M // tm) * (N // tn) * nk
                cost = (M * K * a_bytes * (N // tn if nk > 1 else 1) + K * N * 2 * (M // tm) + 4 * M * N * n_blocks_mn
                        + (8 * M * N * nk // 3 if nk > 1 else 0) + steps * MM_STEP_BYTES
                        + (2 * steps * tm * tk if a_transposed else 0))
                if best is None or cost < best[0]:
                    best = (cost, tm, tn, tk)
    return best[1:]


def _mm(a, b, *, mode, out_dtypes, name, epi=None, extras=(), comm=None, b_split=None, out_split=None):
    if b_split is not None:
        lo, ns = b_split
        Rb, Cb = b.shape[1], b.shape[2]
    if mode == "nn":
        (M, K), N = a.shape, (ns * Cb if b_split else b.shape[1])
    elif mode == "nt":
        (M, K), N = a.shape, (Rb if b_split else b.shape[0])
    else:
        (K, M), N = a.shape, b.shape[1]
    n_ex, n_out = len(extras), len(out_dtypes)
    n_unit = Cb if (b_split and mode == "nn") else (N // out_split if out_split else N)
    k_unit = Cb if (b_split and mode == "nt") else K
    tm, tn, tk = _mm_tiles(M, N, K, n_unit, k_unit, a.dtype.itemsize, n_ex + n_out, mode == "tn")
    nk = K // tk
    a_spec = pl.BlockSpec((tk, tm), lambda i, j, k: (k, i)) if mode == "tn" else pl.BlockSpec((tm, tk), lambda i, j, k: (i, k))
    if b_split and mode == "nn":
        nb = Cb // tn
        b_spec = pl.BlockSpec((None, tk, tn), lambda i, j, k: (lo + j // nb, k, j % nb))
    elif b_split:
        nb = Cb // tk
        b_spec = pl.BlockSpec((None, tn, tk), lambda i, j, k: (lo + k // nb, j, k % nb))
    else:
        b_spec = pl.BlockSpec((tn, tk), lambda i, j, k: (j, k)) if mode == "nt" else pl.BlockSpec((tk, tn), lambda i, j, k: (k, j))
    mn_spec = pl.BlockSpec((tm, tn), lambda i, j, k: (i, j))
    out_shapes = [jax.ShapeDtypeStruct((M, N), dt) for dt in out_dtypes]
    out_specs = [mn_spec] * n_out
    if out_split:
        assert n_ex == 0 and n_out == 1
        nbo = (N // out_split) // tn
        out_specs = [pl.BlockSpec((None, tm, tn), lambda i, j, k: (j // nbo, i, j % nbo))]
        out_shapes = [jax.ShapeDtypeStruct((out_split, M, N // out_split), out_dtypes[0])]
    dims = {"nn": (((1,), (0,)), ((), ())), "nt": (((1,), (1,)), ((), ())), "tn": (((0,), (0,)), ((), ()))}[mode]

    def body(*refs):
        a_ref, b_ref = refs[0], refs[1]
        ex_refs = refs[2:2 + n_ex]
        out_refs = refs[2 + n_ex:2 + n_ex + n_out]
        part = lax.dot_general(a_ref[...].astype(BF16), b_ref[...].astype(BF16), dims, preferred_element_type=F32)

        def finish(acc):
            outs = (acc,) if epi is None else epi(acc, *[r[...] for r in ex_refs])
            for r, o in zip(out_refs, outs):
                r[...] = o.astype(r.dtype)

        if nk == 1:
            finish(part)
            return
        acc_ref = refs[-1]
        k = pl.program_id(2)

        @pl.when(k == 0)
        def _():
            acc_ref[...] = part

        @pl.when(jnp.logical_and(k > 0, k < nk - 1))
        def _():
            acc_ref[...] += part

        @pl.when(k == nk - 1)
        def _():
            finish(acc_ref[...] + part)

    outs, comm_outs = _call(
        body, name=name, grid=(M // tm, N // tn, nk),
        in_specs=[a_spec, b_spec] + [mn_spec] * n_ex,
        out_specs=out_specs,
        out_shape=out_shapes,
        scratch_shapes=[pltpu.VMEM((tm, tn), F32)] if nk > 1 else [],
        semantics=("parallel", "parallel", "arbitrary"), args=(a, b, *extras), comm=comm)
    res = outs[0] if n_out == 1 else outs
    return res if comm is None else (res, comm_outs)


def _tiled(fn, *, T, C, ins, out_dtypes=(), acc_rows=(), tb=256, cb=512, name):
    tb = _tile(T, tb, HALO)
    nI, nJ = T // tb, C // cb
    hb, nH = tb // HALO, T // HALO
    specs, args, kinds = [], [], []
    for kind, arr, cmap in ins:
        cm = cmap if cmap is not None else (lambda j: j)
        kinds.append(kind)
        if kind == "cur":
            specs.append(pl.BlockSpec((tb, cb), lambda j, i, cm=cm: (i, cm(j))))
            args.append(arr)
        elif kind == "ext":
            specs.append(pl.BlockSpec((HALO, cb), lambda j, i, cm=cm: (jnp.maximum(i * hb - 1, 0), cm(j))))
            specs.append(pl.BlockSpec((tb, cb), lambda j, i, cm=cm: (i, cm(j))))
            specs.append(pl.BlockSpec((HALO, cb), lambda j, i, cm=cm: (jnp.minimum((i + 1) * hb, nH - 1), cm(j))))
            args += [arr, arr, arr]
        elif kind == "row":
            specs.append(pl.BlockSpec((arr.shape[0], cb), lambda j, i, cm=cm: (0, cm(j))))
            args.append(arr)
        elif kind == "stack":
            specs.append(pl.BlockSpec((arr.shape[0], tb, cb), lambda j, i, cm=cm: (0, i, cm(j))))
            args.append(arr)
        else:
            raise ValueError(kind)
    n_in = len(args)
    n_out, n_acc = len(out_dtypes), len(acc_rows)

    def body(*refs):
        j, i = pl.program_id(0), pl.program_id(1)
        vals, r = [], 0
        for kind in kinds:
            if kind == "ext":
                prev = jnp.where(i == 0, 0.0, refs[r][...].astype(F32))
                cur = refs[r + 1][...].astype(F32)
                nxt = jnp.where(i == nI - 1, 0.0, refs[r + 2][...].astype(F32))
                vals.append(jnp.concatenate([prev, cur, nxt], axis=0))
                r += 3
            else:
                vals.append(refs[r][...])
                r += 1
        res = fn(j, i, *vals)
        for ref, o in zip(refs[n_in:n_in + n_out], res[:n_out]):
            ref[...] = o.astype(ref.dtype)
        for ref, o in zip(refs[n_in + n_out:], res[n_out:]):
            @pl.when(i == 0)
            def _(ref=ref, o=o):
                ref[...] = o

            @pl.when(i > 0)
            def _(ref=ref, o=o):
                ref[...] += o

    outs = pl.pallas_call(
        body, name=name, grid=(nJ, nI), in_specs=specs,
        out_specs=[pl.BlockSpec((tb, cb), lambda j, i: (i, j))] * n_out
        + [pl.BlockSpec((rows, cb), lambda j, i: (0, j)) for rows in acc_rows],
        out_shape=[jax.ShapeDtypeStruct((T, C), dt) for dt in out_dtypes]
        + [jax.ShapeDtypeStruct((rows, C), F32) for rows in acc_rows],
        compiler_params=pltpu.CompilerParams(dimension_semantics=("parallel", "arbitrary"),
                                             vmem_limit_bytes=VMEM_LIMIT),
    )(*args)
    return outs


def _conv_causal(xe, w):
    K = w.shape[0]
    y = xe * w[K - 1:K]
    for j in range(K - 1):
        y = y + pltpu.roll(xe, K - 1 - j, 0) * w[j:j + 1]
    return y


def _conv_anti(de, w):
    K, n = w.shape[0], de.shape[0]
    y = de * w[K - 1:K]
    for j in range(K - 1):
        y = y + pltpu.roll(de, n - (K - 1 - j), 0) * w[j:j + 1]
    return y


def _conv_dw(dce, xe, K):
    n = dce.shape[0]
    tb = n - 2 * HALO
    rows = []
    for j in range(K):
        xs = xe if j == K - 1 else pltpu.roll(xe, K - 1 - j, 0)
        rows.append(jnp.sum((dce * xs)[HALO:HALO + tb], axis=0, keepdims=True))
    rows.append(jnp.zeros((HALO - K, dce.shape[1]), F32))
    return jnp.concatenate(rows, axis=0)


def _own(xe):
    return xe[HALO:xe.shape[0] - HALO]


def _row0(v):
    return jnp.concatenate([v, jnp.zeros((HALO - 1, v.shape[1]), F32)], axis=0)


def _per_head(fn, *xs):
    n = xs[0].shape[1] // HEAD_DIM
    outs = [fn(*[x[:, g * HEAD_DIM:(g + 1) * HEAD_DIM] for x in xs]) for g in range(n)]
    return outs[0] if n == 1 else jnp.concatenate(outs, axis=1)


def _rms_fwd(x, g, name):
    T, D = x.shape

    def fn(j, i, xv, gv):
        r = lax.rsqrt(jnp.mean(xv * xv, axis=1, keepdims=True) + EPS)
        return (xv * r * gv,)

    return _tiled(fn, T=T, C=D, ins=[("cur", x, None), ("row", g, None)], out_dtypes=[BF16], cb=D, name=name)[0]


def _rms_bwd_math(dy, xv, gv):
    r = lax.rsqrt(jnp.mean(xv * xv, axis=1, keepdims=True) + EPS)
    xh = xv * r
    dxh = dy * gv
    dx = r * (dxh - xh * jnp.mean(dxh * xh, axis=1, keepdims=True))
    dg = jnp.sum(dy * xh, axis=0, keepdims=True)
    return dx, dg


def _rms_bwd(dh, x, g, dres, name):
    T, D = x.shape

    def fn(j, i, dhv, xv, gv, dr):
        dx, dg = _rms_bwd_math(dhv, xv, gv)
        return dr + dx, dr + dx, _row0(dg)

    return _tiled(fn, T=T, C=D, ins=[("cur", dh, None), ("cur", x, None), ("row", g, None), ("cur", dres, None)],
                  out_dtypes=[F32, BF16], acc_rows=[HALO], cb=D, name=name)


def _final_fb(x3, tgt, g):
    T, D = x3.shape

    def fn(j, i, xv, tv, gv):
        r = lax.rsqrt(jnp.mean(xv * xv, axis=1, keepdims=True) + EPS)
        xh = xv * r
        e = xh * gv - tv
        dy = e * (1.0 / D)
        dxh = dy * gv
        dx = r * (dxh - xh * jnp.mean(dxh * xh, axis=1, keepdims=True))
        dg = jnp.sum(dy * xh, axis=0, keepdims=True)
        ls = jnp.sum(e * e, axis=0, keepdims=True) * (0.5 / D)
        return dx, jnp.concatenate([dg, ls, jnp.zeros((HALO - 2, D), F32)], axis=0)

    return _tiled(fn, T=T, C=D, ins=[("cur", x3, None), ("cur", tgt, None), ("row", g, None)],
                  out_dtypes=[F32], acc_rows=[HALO], cb=D, name="final_fb")


def _ga_fwd(proj, w_a, CW, cb):
    T = proj.shape[0]
    n = CW // cb

    def fn(j, i, ax, ab, ac, w):
        c = _conv_causal(ac * ax, w)
        return (ab * _own(c),)

    return _tiled(fn, T=T, C=CW, ins=[("ext", proj, None), ("cur", proj, lambda j: j + n), ("ext", proj, lambda j: j + 2 * n),
                                       ("row", w_a, None)], out_dtypes=[BF16], cb=cb, name="ga_fwd")[0]


def _ga_bwd(dymix, proj, w_a, CW, cb):
    T = proj.shape[0]
    n = CW // cb
    K = w_a.shape[0]

    def fn(j, i, dy, ax, ab, ac, w):
        u = ac * ax
        c = _conv_causal(u, w)
        dc = dy * ab
        du = _conv_anti(dc, w)
        return _own(du * ac), _own(dy * c), _own(du * ax), _conv_dw(dc, u, K)

    return _tiled(fn, T=T, C=CW, ins=[("ext", dymix, None), ("ext", proj, None), ("ext", proj, lambda j: j + n),
                                       ("ext", proj, lambda j: j + 2 * n), ("row", w_a, None)],
                  out_dtypes=[BF16, BF16, BF16], acc_rows=[HALO], cb=cb, name="ga_bwd")


def _l2n(s):
    return s * lax.rsqrt(jnp.sum(s * s, axis=1, keepdims=True) + EPS)


def _qkv_fwd(proj, w_sec, coff, normalize, DNW, cb, name):
    T = proj.shape[0]

    def fn(j, i, pre, w):
        c = _own(_conv_causal(pre, w))
        s = c * _sigmoid(c)
        return (_per_head(_l2n, s) if normalize else s,)

    return _tiled(fn, T=T, C=DNW, ins=[("ext", proj, lambda j: j + coff), ("row", w_sec, None)],
                  out_dtypes=[F32], cb=cb, name=name)[0]


def _qkv_bwd(dsec, proj, w_sec, coff, normalize, DNW, cb, name):
    T = proj.shape[0]
    K = w_sec.shape[0]

    def l2n_bwd(s, dn):
        r = lax.rsqrt(jnp.sum(s * s, axis=1, keepdims=True) + EPS)
        nrm = s * r
        return r * (dn - nrm * jnp.sum(dn * nrm, axis=1, keepdims=True))

    def fn(j, i, dn, pre, w):
        c = _conv_causal(pre, w)
        sg = _sigmoid(c)
        s = c * sg
        ds = _per_head(l2n_bwd, s, dn) if normalize else dn
        dc = ds * (sg * (1.0 + c * (1.0 - sg)))
        return _own(_conv_anti(dc, w)), _conv_dw(dc, pre, K)

    return _tiled(fn, T=T, C=DNW, ins=[("ext", dsec, None), ("ext", proj, lambda j: j + coff), ("row", w_sec, None)],
                  out_dtypes=[BF16], acc_rows=[HALO], cb=cb, name=name)


def _gb_fwd(small, a_log_row, dt_row, H):
    T = small.shape[0]

    def fn(j, i, sm, al, dt):
        z = sm + dt
        sp = jnp.maximum(z, 0.0) + jnp.log(1.0 + jnp.exp(-jnp.abs(z)))
        g = -jnp.exp(al) * sp
        beta = _sigmoid(pltpu.roll(sm, LANES - H, 1))
        return g, beta

    return _tiled(fn, T=T, C=LANES, ins=[("cur", small, None), ("row", a_log_row, None), ("row", dt_row, None)],
                  out_dtypes=[F32, F32], cb=LANES, name="gb_fwd")


def _gb_bwd(dgB, dbB, small, g, beta, a_log_row, dt_row, H):
    T = small.shape[0]

    def fn(j, i, dgv, dbv, sm, gv, bv, al, dt):
        lane = lax.broadcasted_iota(jnp.int32, sm.shape, 1)
        dg = jnp.zeros(sm.shape, F32)
        db = jnp.zeros(sm.shape, F32)
        for h in range(H):
            dg = jnp.where(lane == h, jnp.sum(dgv[h], axis=1, keepdims=True), dg)
            db = jnp.where(lane == h, jnp.sum(dbv[h], axis=1, keepdims=True), db)
        da = dg * (-jnp.exp(al)) * _sigmoid(sm + dt)
        dbb = db * bv * (1.0 - bv)
        dsm = jnp.where(lane < H, da, 0.0) + pltpu.roll(jnp.where(lane < H, dbb, 0.0), H, 1)
        d_alog = jnp.sum(jnp.where(lane < H, dg * gv, 0.0), axis=0, keepdims=True)
        d_dt = jnp.sum(jnp.where(lane < H, da, 0.0), axis=0, keepdims=True)
        return dsm, jnp.concatenate([d_alog, d_dt, jnp.zeros((HALO - 2, LANES), F32)], axis=0)

    return _tiled(fn, T=T, C=LANES, ins=[("stack", dgB, None), ("stack", dbB, None), ("cur", small, None), ("cur", g, None),
                                          ("cur", beta, None), ("row", a_log_row, None), ("row", dt_row, None)],
                  out_dtypes=[BF16], acc_rows=[HALO], cb=LANES, name="gb_bwd")


_DIMS = {"nn": (((1,), (0,)), ((), ())), "nt": (((1,), (1,)), ((), ())), "tn": (((0,), (0,)), ((), ()))}
_DOT_BWD = {"nn": (("nt", "gb"), ("tn", "ag")), "nt": (("nn", "gb"), ("tn", "ga")), "tn": (("nt", "bg"), ("nn", "ag"))}


def _split(a):
    hi = a.astype(BF16)
    return hi, (a - hi.astype(F32)).astype(BF16)


def _raw_dot(a, b, kind, passes):
    dg = lambda x, y: lax.dot_general(x, y, _DIMS[kind], preferred_element_type=F32)
    if passes == 1:
        return dg(a.astype(BF16), b.astype(BF16))
    ah, al = _split(a)
    bh, bl = _split(b)
    return dg(ah, bh) + (dg(ah, bl) + dg(al, bh))


@functools.lru_cache(maxsize=None)
def _dotf(kind, passes):
    @jax.custom_vjp
    def f(a, b):
        return _raw_dot(a, b, kind, passes)

    def fwd(a, b):
        return _raw_dot(a, b, kind, passes), (a, b)

    def bwd(res, g):
        ops = {"a": res[0], "b": res[1], "g": g}
        (ka, oa), (kb, ob) = _DOT_BWD[kind]
        return (_raw_dot(ops[oa[0]], ops[oa[1]], ka, passes), _raw_dot(ops[ob[0]], ops[ob[1]], kb, passes))

    f.defvjp(fwd, bwd)
    return f


@jax.custom_vjp
def _saved_inverse(L, inv):
    return inv


def _saved_inverse_fwd(L, inv):
    return inv, inv


def _saved_inverse_bwd(inv, g):
    d3nt, d3tn = _dotf("nt", 3), _dotf("tn", 3)
    return -d3nt(d3tn(inv, g), inv), jnp.zeros_like(inv)


_saved_inverse.defvjp(_saved_inverse_fwd, _saved_inverse_bwd)


def _chunk_fn(q, k, v, gB, bB, S, inv_saved=None):
    C = CHUNK
    d3, d3nt = _dotf("nn", 3), _dotf("nt", 3)
    d1, d1nt, d1tn = _dotf("nn", 1), _dotf("nt", 1), _dotf("tn", 1)
    each = lambda f, *ls: tuple(f(*xs) for xs in zip(*ls))
    row = lax.broadcasted_iota(jnp.int32, (C, C), 0)
    col = lax.broadcasted_iota(jnp.int32, (C, C), 1)
    causal = row >= col
    strict = row > col
    tril = jnp.where(causal, 1.0, 0.0).astype(F32)
    eye = jnp.where(row == col, 1.0, 0.0).astype(F32)
    avg = jnp.full((C, HEAD_DIM), 1.0 / HEAD_DIM, F32)
    gc = each(lambda g: d3(tril, g), gB)
    R = each(lambda g: d3nt(avg, g), gc)
    decay = each(lambda g, r: jnp.where(causal, jnp.exp(jnp.where(causal, g[:, :C] - r, 0.0)), 0.0), gc, R)
    kk = each(lambda x: d1nt(x, x), k)
    L = each(lambda a, d, b: jnp.where(strict, a * d * b[:, :C], 0.0), kk, decay, bB)
    if inv_saved is None:
        inv = each(lambda l: eye - l, L)
        P = L
        for _ in range(5):
            P = each(lambda p: d3(p, p), P)
            inv = each(lambda a, p: d3(a, eye + p), inv, P)
    else:
        inv = each(_saved_inverse, L, inv_saved)
    eg = each(jnp.exp, gc)
    u = each(lambda a, x, b: d3(a, x * b), inv, v, bB)
    w = each(lambda a, x, b, e: d3(a, x * b * e), inv, k, bB, eg)
    qs = each(lambda x: x * (HEAD_DIM ** -0.5), q)
    qk = each(lambda a, x, d: d1nt(a, x) * d, qs, k, decay)
    gl = each(lambda g: g[C - 1:C, :], gc)
    v_new = each(lambda a, b, s: a - d1(b, s), u, w, S)
    o1 = each(lambda a, e, s: d1(a * e, s), qs, eg, S)
    o = each(lambda a, b, c: a + d1(b, c), o1, qk, v_new)
    kv = each(lambda x, a, g, vn: d1tn(x * jnp.exp(a - g), vn), k, gl, gc, v_new)
    S_new = each(lambda s, a, b: s * jnp.exp(a) + b, S, gl, kv)
    return (o, S_new), inv


def _sel_lane(x, h):
    lane = lax.broadcasted_iota(jnp.int32, x.shape, 1)
    return jnp.broadcast_to(jnp.sum(jnp.where(lane == h, x, 0.0), axis=1, keepdims=True), x.shape)


def _head(ref, h):
    return ref[:, h * HEAD_DIM:(h + 1) * HEAD_DIM]


def _delta_fwd(q, k, v, g, beta, comm=None):
    T = q.shape[0]
    H, N = q.shape[1] // HEAD_DIM, T // CHUNK

    def body(q_ref, k_ref, v_ref, g_ref, b_ref, o_ref, s_ref, inv_ref, S):
        @pl.when(pl.program_id(0) == 0)
        def _():
            S[...] = jnp.zeros_like(S)

        gv, bv = g_ref[...], b_ref[...]
        heads = lambda f: tuple(f(h) for h in range(H))
        S_in = heads(lambda h: S[h])
        for h in range(H):
            s_ref[h, 0] = S_in[h]
        (o, S_new), inv = _chunk_fn(heads(lambda h: _head(q_ref, h)), heads(lambda h: _head(k_ref, h)),
                                    heads(lambda h: _head(v_ref, h)), heads(lambda h: _sel_lane(gv, h)),
                                    heads(lambda h: _sel_lane(bv, h)), S_in)
        for h in range(H):
            o_ref[:, h * HEAD_DIM:(h + 1) * HEAD_DIM] = o[h]
            inv_ref[h, 0] = inv[h]
            S[h] = S_new[h]

    blk = pl.BlockSpec((CHUNK, H * HEAD_DIM), lambda n: (n, 0))
    gblk = pl.BlockSpec((CHUNK, LANES), lambda n: (n, 0))
    outs, comm_outs = _call(
        body, name="delta_fwd", grid=(N,), in_specs=[blk, blk, blk, gblk, gblk],
        out_specs=[blk, pl.BlockSpec((H, 1, HEAD_DIM, HEAD_DIM), lambda n: (0, n, 0, 0)),
                   pl.BlockSpec((H, 1, CHUNK, CHUNK), lambda n: (0, n, 0, 0))],
        out_shape=[jax.ShapeDtypeStruct((T, H * HEAD_DIM), F32), jax.ShapeDtypeStruct((H, N, HEAD_DIM, HEAD_DIM), F32),
                   jax.ShapeDtypeStruct((H, N, CHUNK, CHUNK), F32)],
        scratch_shapes=[pltpu.VMEM((H, HEAD_DIM, HEAD_DIM), F32)],
        semantics=("arbitrary",), args=(q, k, v, g, beta), comm=comm)
    return outs[0], outs[1], outs[2], comm_outs


def _delta_bwd(q, k, v, g, beta, S0, inv, do, comm=None):
    T = q.shape[0]
    H, N = q.shape[1] // HEAD_DIM, T // CHUNK

    def body(q_ref, k_ref, v_ref, g_ref, b_ref, s_ref, inv_ref, do_ref, dq_ref, dk_ref, dv_ref, dg_ref, db_ref, dS):
        @pl.when(pl.program_id(0) == 0)
        def _():
            dS[...] = jnp.zeros_like(dS)

        gv, bv = g_ref[...], b_ref[...]
        heads = lambda f: tuple(f(h) for h in range(H))
        _, vjp, _ = jax.vjp(_chunk_fn, heads(lambda h: _head(q_ref, h)), heads(lambda h: _head(k_ref, h)),
                            heads(lambda h: _head(v_ref, h)), heads(lambda h: _sel_lane(gv, h)), heads(lambda h: _sel_lane(bv, h)),
                            heads(lambda h: s_ref[h, 0]), heads(lambda h: inv_ref[h, 0]), has_aux=True)
        dq, dk, dv, dgB, dbB, dS_prev, _ = vjp((heads(lambda h: _head(do_ref, h)), heads(lambda h: dS[h])))
        for h in range(H):
            sl = slice(h * HEAD_DIM, (h + 1) * HEAD_DIM)
            dq_ref[:, sl] = dq[h]
            dk_ref[:, sl] = dk[h]
            dv_ref[:, sl] = dv[h]
            dg_ref[h] = dgB[h]
            db_ref[h] = dbB[h]
            dS[h] = dS_prev[h]

    blk = pl.BlockSpec((CHUNK, H * HEAD_DIM), lambda n: (N - 1 - n, 0))
    gblk = pl.BlockSpec((CHUNK, LANES), lambda n: (N - 1 - n, 0))
    hblk = pl.BlockSpec((H, CHUNK, LANES), lambda n: (0, N - 1 - n, 0))
    sd = jax.ShapeDtypeStruct
    outs, comm_outs = _call(
        body, name="delta_bwd", grid=(N,),
        in_specs=[blk, blk, blk, gblk, gblk, pl.BlockSpec((H, 1, HEAD_DIM, HEAD_DIM), lambda n: (0, N - 1 - n, 0, 0)),
                  pl.BlockSpec((H, 1, CHUNK, CHUNK), lambda n: (0, N - 1 - n, 0, 0)), blk],
        out_specs=[blk, blk, blk, hblk, hblk],
        out_shape=[sd((T, H * HEAD_DIM), F32)] * 3 + [sd((H, T, LANES), F32)] * 2,
        scratch_shapes=[pltpu.VMEM((H, HEAD_DIM, HEAD_DIM), F32)],
        semantics=("arbitrary",), args=(q, k, v, g, beta, S0, inv, do), comm=comm)
    return (*outs, comm_outs)


def _gnorm_fwd(o, proj, z_coff, gdn_t, DNW):
    T = o.shape[0]

    def fn(j, i, ov, zv, gv):
        def one(oh, zh, gh):
            r = lax.rsqrt(jnp.mean(oh * oh, axis=1, keepdims=True) + EPS)
            return oh * r * gh * (zh * _sigmoid(zh))
        return (_per_head(one, ov, zv, jnp.broadcast_to(gv, ov.shape)),)

    return _tiled(fn, T=T, C=DNW, ins=[("cur", o, None), ("cur", proj, lambda j: j + z_coff), ("row", gdn_t, None)],
                  out_dtypes=[BF16], cb=DNW, name="gnorm_fwd")[0]


def _gnorm_bwd(dymix, y_coff, o, proj, z_coff, gdn_t, DNW):
    T = o.shape[0]
    nh = DNW // HEAD_DIM

    def fn(j, i, dy, ov, zv, gv):
        dos, dzs, dgs = [], [], jnp.zeros((1, HEAD_DIM), F32)
        for h in range(nh):
            sl = slice(h * HEAD_DIM, (h + 1) * HEAD_DIM)
            dyh, oh, zh, gh = dy[:, sl].astype(F32), ov[:, sl], zv[:, sl], gv[:, sl]
            r = lax.rsqrt(jnp.mean(oh * oh, axis=1, keepdims=True) + EPS)
            on = oh * r
            sg = _sigmoid(zh)
            sz = zh * sg
            dzs.append(dyh * on * gh * (sg * (1.0 + zh * (1.0 - sg))))
            don = dyh * gh * sz
            dos.append(r * (don - on * jnp.mean(don * on, axis=1, keepdims=True)))
            dgs = dgs + jnp.sum(dyh * on * sz, axis=0, keepdims=True)
        cat = (lambda xs: xs[0] if nh == 1 else jnp.concatenate(xs, axis=1))
        return cat(dos), cat(dzs), _row0(dgs)

    T_ = T
    nI = T_ // _tile(T_, 256, HALO)
    tb = T_ // nI
    specs_cb = DNW

    def body_wrap():
        def body(dy_ref, o_ref, z_ref, g_ref, do_ref, dz_ref, dg_ref):
            i = pl.program_id(0)
            d_o, d_z, d_g = fn(0, i, dy_ref[...], o_ref[...], z_ref[...], g_ref[...])
            do_ref[...] = d_o
            dz_ref[...] = d_z.astype(dz_ref.dtype)

            @pl.when(i == 0)
            def _():
                dg_ref[...] = d_g

            @pl.when(i > 0)
            def _():
                dg_ref[...] += d_g

        return pl.pallas_call(
            body, name="gnorm_bwd", grid=(nI,),
            in_specs=[pl.BlockSpec((tb, specs_cb), lambda i: (i, y_coff)), pl.BlockSpec((tb, specs_cb), lambda i: (i, 0)),
                      pl.BlockSpec((tb, specs_cb), lambda i: (i, z_coff)), pl.BlockSpec((1, specs_cb), lambda i: (0, 0))],
            out_specs=[pl.BlockSpec((tb, specs_cb), lambda i: (i, 0)), pl.BlockSpec((tb, specs_cb), lambda i: (i, 0)),
                       pl.BlockSpec((HALO, HEAD_DIM), lambda i: (0, 0))],
            out_shape=[jax.ShapeDtypeStruct((T_, DNW), F32), jax.ShapeDtypeStruct((T_, DNW), BF16),
                       jax.ShapeDtypeStruct((HALO, HEAD_DIM), F32)],
            compiler_params=pltpu.CompilerParams(dimension_semantics=("arbitrary",), vmem_limit_bytes=VMEM_LIMIT),
        )(dymix, o, proj, gdn_t)

    return body_wrap()


def _ffn_fwd(up_g, up_v, w_g, w_v, cb):
    T, F = up_g.shape

    def fn(j, i, ug, uv, wg, wv):
        cg = _own(_conv_causal(ug, wg))
        cv = _own(_conv_causal(uv, wv))
        return (cg * _sigmoid(cg) * cv,)

    return _tiled(fn, T=T, C=F, ins=[("ext", up_g, None), ("ext", up_v, None), ("row", w_g, None), ("row", w_v, None)],
                  out_dtypes=[BF16], cb=cb, name="ffn_fwd")[0]


def _ffn_bwd(dact, up_g, up_v, w_g, w_v, cb):
    T, F = up_g.shape
    K = w_g.shape[0]

    def fn(j, i, da, ug, uv, wg, wv):
        cg = _conv_causal(ug, wg)
        cv = _conv_causal(uv, wv)
        sg = _sigmoid(cg)
        dgate = da * cv * (sg * (1.0 + cg * (1.0 - sg)))
        dval = da * (cg * sg)
        return (_own(_conv_anti(dgate, wg)), _own(_conv_anti(dval, wv)), _conv_dw(dgate, ug, K), _conv_dw(dval, uv, K))

    return _tiled(fn, T=T, C=F, ins=[("ext", dact, None), ("ext", up_g, None), ("ext", up_v, None), ("row", w_g, None),
                                      ("row", w_v, None)], out_dtypes=[BF16, BF16], acc_rows=[HALO, HALO], cb=cb, name="ffn_bwd")


def _ple_bwd(dx3, pp, pg):
    T, D = dx3.shape

    def fn(j, i, d, ppv, pgv):
        return d * ppv * pgv * (1.0 - pgv), d * pgv

    return _tiled(fn, T=T, C=D, ins=[("cur", dx3, None), ("cur", pp, None), ("cur", pg, None)],
                  out_dtypes=[BF16, BF16], cb=_tile(D, 512, LANES), name="ple_bwd")


def _wide(R, Cc, n_f32, unit=HALO):
    cb = Cc if (Cc % LANES or Cc <= 4096) else _tile(Cc, 2048, LANES)
    cap = max(unit, EW_VMEM_BUDGET // (2 * 4 * n_f32 * cb) // unit * unit)
    return _tile(R, cap, unit), cb


def _adamw(w, g, m, v, name):
    R, Cc = w.shape
    tb, cb = _wide(R, Cc, 7)
    c1 = 1.0 / (1.0 - ADAM_B1 ** ADAM_STEP)
    c2 = 1.0 / (1.0 - ADAM_B2 ** ADAM_STEP)

    def fn(j, i, wv, gv, mv, vv):
        m2 = ADAM_B1 * mv + (1.0 - ADAM_B1) * gv
        v2 = ADAM_B2 * vv + (1.0 - ADAM_B2) * (gv * gv)
        delta = -ADAM_LR * ((m2 * c1) / (jnp.sqrt(v2 * c2) + ADAM_EPS) + ADAM_WD * wv)
        return delta, m2, v2

    return _tiled(fn, T=R, C=Cc, ins=[("cur", w, None), ("cur", g, None), ("cur", m, None), ("cur", v, None)],
                  out_dtypes=[F32, F32, F32], tb=tb, cb=cb, name=name)


def _sum_stack(st, name):
    S, R, Cc = st.shape
    cb = _tile(Cc, 512, LANES) if Cc % LANES == 0 else Cc

    def fn(j, i, sv):
        t = sv[0]
        for s in range(1, S):
            t = t + sv[s]
        return (t,)

    return _tiled(fn, T=R, C=Cc, ins=[("stack", st, None)], out_dtypes=[F32], cb=cb, name=name)[0]


ANY = pl.BlockSpec(memory_space=pl.ANY)


def _place():
    x, y, c = lax.axis_index("x"), lax.axis_index("y"), lax.axis_index("c")
    return x, y, c, 2 * x + y


def _chip_dev(s, c):
    return (s // 2, s % 2, c)


class _Comm:
    def __init__(self, ins, out_shapes, sems, start, wait, aliases=None):
        self.ins, self.out_shapes, self.sems = list(ins), list(out_shapes), list(sems)
        self.start, self.wait, self.aliases = start, wait, dict(aliases or {})


def _merge(*comms):
    offs, i, o, s = [], 0, 0, 0
    for cm in comms:
        offs.append((i, o, s))
        i, o, s = i + len(cm.ins), o + len(cm.out_shapes), s + len(cm.sems)

    def part(refs, k, cm):
        i0, o0, s0 = offs[k]
        return refs[0][i0:i0 + len(cm.ins)], refs[1][o0:o0 + len(cm.out_shapes)], refs[2][s0:s0 + len(cm.sems)]

    def start(*refs):
        for k, cm in enumerate(comms):
            cm.start(*part(refs, k, cm))

    def wait(*refs):
        for k, cm in enumerate(comms):
            cm.wait(*part(refs, k, cm))

    aliases = {}
    for k, cm in enumerate(comms):
        for a, b in cm.aliases.items():
            aliases[offs[k][0] + a] = offs[k][1] + b
    return _Comm([a for cm in comms for a in cm.ins], [a for cm in comms for a in cm.out_shapes],
                 [a for cm in comms for a in cm.sems], start, wait, aliases)


def _call(body, *, name, grid, in_specs, out_specs, out_shape, scratch_shapes, semantics, args, comm=None):
    if comm is None:
        outs = pl.pallas_call(
            body, name=name, grid=grid, in_specs=in_specs, out_specs=out_specs, out_shape=out_shape,
            scratch_shapes=list(scratch_shapes),
            compiler_params=pltpu.CompilerParams(dimension_semantics=semantics, vmem_limit_bytes=VMEM_LIMIT))(*args)
        return list(outs), []
    n_in, n_out, n_scr = len(in_specs), len(out_specs), len(scratch_shapes)
    ci, co = len(comm.ins), len(comm.out_shapes)

    def wrapped(*refs):
        r = 0
        ins, r = refs[r:r + n_in], r + n_in
        cins, r = refs[r:r + ci], r + ci
        outs, r = refs[r:r + n_out], r + n_out
        couts, r = refs[r:r + co], r + co
        scr, r = refs[r:r + n_scr], r + n_scr
        csems = refs[r:]
        ids = [pl.program_id(a) for a in range(len(grid))]
        first, last = ids[0] == 0, ids[0] == grid[0] - 1
        for a in range(1, len(grid)):
            first = jnp.logical_and(first, ids[a] == 0)
            last = jnp.logical_and(last, ids[a] == grid[a] - 1)

        @pl.when(first)
        def _():
            comm.start(cins, couts, csems)

        body(*ins, *outs, *scr)

        @pl.when(last)
        def _():
            comm.wait(cins, couts, csems)

    outs = pl.pallas_call(
        wrapped, name=name, grid=grid, in_specs=list(in_specs) + [ANY] * ci, out_specs=list(out_specs) + [ANY] * co,
        out_shape=list(out_shape) + comm.out_shapes, scratch_shapes=list(scratch_shapes) + comm.sems,
        input_output_aliases={n_in + a: n_out + b for a, b in comm.aliases.items()},
        compiler_params=pltpu.CompilerParams(dimension_semantics=("arbitrary",) * len(grid), vmem_limit_bytes=VMEM_LIMIT),
    )(*args, *comm.ins)
    return list(outs[:n_out]), list(outs[n_out:])


def _run_comm(comm, name):
    ci, co = len(comm.ins), len(comm.out_shapes)

    def body(*refs):
        cins, couts, csems = refs[:ci], refs[ci:ci + co], refs[ci + co:]
        comm.start(cins, couts, csems)
        comm.wait(cins, couts, csems)

    outs = pl.pallas_call(body, name=name, in_specs=[ANY] * ci, out_specs=[ANY] * co, out_shape=comm.out_shapes,
                          scratch_shapes=comm.sems, input_output_aliases=comm.aliases)(*comm.ins)
    return list(outs)


def _ag_comm(shard, land=None, q=0, nq=1):
    two, R2, Cc = shard.shape
    rows = pl.ds(q * (R2 // nq), R2 // nq)
    DMA = pltpu.SemaphoreType.DMA

    def copies(ins, outs, sems, which):
        sh, out = ins[0], outs[0]
        send1, recv1, send2, recv2, send0, recv0 = sems
        x, y, c, s = _place()
        sib = (x, y, 1 - c)
        rc = pltpu.make_async_remote_copy
        if which == "first":
            return [rc(sh.at[c, rows], out.at[s, c, rows], send1.at[m - 1], recv1.at[m - 1],
                       device_id=_chip_dev(s ^ m, c), device_id_type=MESH) for m in range(1, 4)]
        if which == "own":
            return [rc(sh.at[h, rows], out.at[s, h, rows], send0.at[h], recv0.at[h], device_id=sib, device_id_type=MESH)
                    for h in range(2)]
        if which == "landed":
            return [rc(sh.at[c, rows], out.at[s ^ m, c, rows], send1.at[m - 1], recv1.at[m - 1], device_id=sib,
                       device_id_type=MESH) for m in range(1, 4)]
        half = c if which == "passed" else 1 - c
        return [rc(out.at[s ^ m, half, rows], out.at[s ^ m, half, rows], send2.at[m - 1], recv2.at[m - 1], device_id=sib,
                   device_id_type=MESH) for m in range(1, 4)]

    def start(ins, outs, sems):
        for cp in copies(ins, outs, sems, "first") + copies(ins, outs, sems, "own"):
            cp.start()

    def wait(ins, outs, sems):
        passed = copies(ins, outs, sems, "passed")
        for lan, pas in zip(copies(ins, outs, sems, "landed"), passed):
            lan.wait_recv()
            pas.start()
        for cp in copies(ins, outs, sems, "handed"):
            cp.wait_recv()
        for cp in copies(ins, outs, sems, "own"):
            cp.wait()
        for cp in copies(ins, outs, sems, "first") + passed:
            cp.wait_send()

    return _Comm([shard] + ([land] if land is not None else []), [jax.ShapeDtypeStruct((4, two, R2, Cc), shard.dtype)],
                 [DMA((3,)), DMA((3,)), DMA((3,)), DMA((3,)), DMA((2,)), DMA((2,))], start, wait,
                 {1: 0} if land is not None else None)


def _a2a_comm(S1, q=0, nq=1, land=None):
    S4, R2, Cc = S1.shape
    rows = pl.ds(q * (R2 // nq), R2 // nq)
    DMA = pltpu.SemaphoreType.DMA

    def copies(ins, outs, sems):
        x, y, c, s = _place()
        return [pltpu.make_async_remote_copy(ins[0].at[s ^ m, rows], outs[0].at[m - 1, rows], sems[0].at[m - 1],
                                             sems[1].at[m - 1], device_id=_chip_dev(s ^ m, c), device_id_type=MESH)
                for m in range(1, 4)]

    def start(ins, outs, sems):
        for cp in copies(ins, outs, sems):
            cp.start()

    def wait(ins, outs, sems):
        for cp in copies(ins, outs, sems):
            cp.wait()

    return _Comm([S1] + ([land] if land is not None else []), [jax.ShapeDtypeStruct((3, R2, Cc), S1.dtype)],
                 [DMA((3,)), DMA((3,))], start, wait, {1: 0} if land is not None else None)


def _halves(G):
    return G.reshape(G.shape[0], 2, G.shape[1] // 2, G.shape[2])


def _swap_comm(piece):
    n, two, R2, Cc = piece.shape
    DMA = pltpu.SemaphoreType.DMA

    def copies(ins, outs, sems):
        x, y, c, s = _place()
        return [pltpu.make_async_remote_copy(ins[0].at[t, 1 - c], outs[0].at[t], sems[0].at[t], sems[1].at[t],
                                             device_id=(x, y, 1 - c), device_id_type=MESH) for t in range(n)]

    def start(ins, outs, sems):
        for cp in copies(ins, outs, sems):
            cp.start()

    def wait(ins, outs, sems):
        for cp in copies(ins, outs, sems):
            cp.wait()

    return _Comm([piece], [jax.ShapeDtypeStruct((n, R2, Cc), piece.dtype)], [DMA((n,)), DMA((n,))], start, wait)


def _add_half(pieces, As, cidx, name):
    R2, Cc = pieces[0].shape[2:]
    S4 = sum(pc.shape[0] for pc in pieces)
    tb, cb = _wide(R2, Cc, 3, 2 * HALO)
    nI, nJ = R2 // tb, Cc // cb

    def body(c_ref, g_ref, a_ref, *rest):
        rest[-1][...] = (g_ref[0, 0] + a_ref[0]).astype(BF16)

    out, t0 = None, 0
    for k, (pc, A) in enumerate(zip(pieces, As)):
        grid_spec = pltpu.PrefetchScalarGridSpec(
            num_scalar_prefetch=1, grid=(pc.shape[0], nI, nJ),
            in_specs=[pl.BlockSpec((1, 1, tb, cb), lambda t, i, j, c_ref: (t, c_ref[0], i, j)),
                      pl.BlockSpec((1, tb, cb), lambda t, i, j, c_ref: (t, i, j))] + ([ANY] if k else []),
            out_specs=pl.BlockSpec((tb, cb), lambda t, i, j, c_ref, t0=t0: ((t0 + t) * nI + i, j)))
        out = pl.pallas_call(
            functools.partial(body), name=f"{name}{k}", grid_spec=grid_spec, out_shape=jax.ShapeDtypeStruct((S4 * R2, Cc), BF16),
            input_output_aliases={3: 0} if k else {},
            compiler_params=pltpu.CompilerParams(dimension_semantics=("parallel", "parallel", "parallel"),
                                                 vmem_limit_bytes=VMEM_LIMIT),
        )(*((cidx, pc, A) + ((out,) if k else ())))
        t0 += pc.shape[0]
    return out.reshape(S4, R2, Cc)


def _add_own(S1, B, chip_idx, cidx, name):
    S4, R2, Cc = S1.shape
    tb, cb = _wide(R2, Cc, 3, 2 * HALO)

    def body(s_idx, c_idx, s_ref, b_ref, o_ref):
        o_ref[...] = ((s_ref[0].astype(F32) + b_ref[0].astype(F32)) + b_ref[1].astype(F32)) + b_ref[2].astype(F32)

    grid_spec = pltpu.PrefetchScalarGridSpec(
        num_scalar_prefetch=2, grid=(R2 // tb, Cc // cb),
        in_specs=[pl.BlockSpec((1, tb, cb), lambda i, j, s_idx, c_idx: (s_idx[0], i, j)),
                  pl.BlockSpec((3, tb, cb), lambda i, j, s_idx, c_idx: (0, i, j))],
        out_specs=pl.BlockSpec((None, tb, cb), lambda i, j, s_idx, c_idx: (c_idx[0], i, j)))
    return pl.pallas_call(body, name=name, grid_spec=grid_spec, out_shape=jax.ShapeDtypeStruct((2, R2, Cc), F32),
                          compiler_params=pltpu.CompilerParams(dimension_semantics=("parallel", "parallel"),
                                                               vmem_limit_bytes=VMEM_LIMIT))(chip_idx, cidx, S1, B)


def _sibling_fill(Hs, name):
    def body(h_ref, out_ref, send, recv):
        x, y, c, s = _place()
        cp = pltpu.make_async_remote_copy(h_ref.at[c], out_ref.at[c], send, recv, device_id=(x, y, 1 - c), device_id_type=MESH)
        cp.start()
        cp.wait()

    return pl.pallas_call(
        body, name=name, in_specs=[ANY], out_specs=ANY, out_shape=jax.ShapeDtypeStruct(Hs.shape, Hs.dtype),
        input_output_aliases={0: 0}, scratch_shapes=[pltpu.SemaphoreType.DMA, pltpu.SemaphoreType.DMA],
    )(Hs)


def _gather_all(buf, name):
    R, Cc = buf.shape

    def body(b_ref, out_ref, send, recv, local):
        x, y, c, s = _place()
        d = 2 * s + c
        mine = pltpu.make_async_copy(b_ref, out_ref.at[d], local)
        mine.start()
        cps = []
        for m in range(1, 8):
            t = d ^ m
            cp = pltpu.make_async_remote_copy(b_ref, out_ref.at[d], send.at[m - 1], recv.at[m - 1],
                                              device_id=(t // 4, (t // 2) % 2, t % 2), device_id_type=MESH)
            cp.start()
            cps.append(cp)
        for cp in cps:
            cp.wait()
        mine.wait()

    return pl.pallas_call(
        body, name=name, in_specs=[ANY], out_specs=ANY, out_shape=jax.ShapeDtypeStruct((8, R, Cc), buf.dtype),
        scratch_shapes=[pltpu.SemaphoreType.DMA((7,)), pltpu.SemaphoreType.DMA((7,)), pltpu.SemaphoreType.DMA],
    )(buf)


def _finish_shard(S1, B, cidx, chip_idx, name):
    Hs = _sibling_fill(_add_own(S1, B, chip_idx, cidx, name + "_sum"), name + "_gather")
    return Hs.reshape(2 * Hs.shape[1], Hs.shape[2])


def _pack_rows(vs):
    flat = jnp.concatenate([v.reshape(-1) for v in vs])
    n = flat.shape[0]
    rows = -(-n // (LANES * 2 * HALO)) * 2 * HALO
    return jnp.pad(flat, (0, rows * LANES - n)).reshape(rows, LANES)


def _unpack_rows(buf, shapes):
    flat = buf.reshape(-1)
    outs, o = [], 0
    for shp in shapes:
        n = 1
        for d in shp:
            n *= d
        outs.append(flat[o:o + n].reshape(shp))
        o += n
    return outs


def kernel(x, p, norm_mix_g, w_in, conv_a_w, conv_qkv_w, a_log, dt_bias, dn_norm_g, w_out, norm_ffn_g, w_up, conv_ffn_w, w_down, norm_ple_g, w_ple_gate, w_ple_proj, final_norm_g, loss_target, m_norm_mix_g, m_w_in, m_conv_a_w, m_conv_qkv_w, m_a_log, m_dt_bias, m_dn_norm_g, m_w_out, m_norm_ffn_g, m_w_up, m_conv_ffn_w, m_w_down, m_norm_ple_g, m_w_ple_gate, m_w_ple_proj, m_final_norm_g, v_norm_mix_g, v_w_in, v_conv_a_w, v_conv_qkv_w, v_a_log, v_dt_bias, v_dn_norm_g, v_w_out, v_norm_ffn_g, v_w_up, v_conv_ffn_w, v_w_down, v_norm_ple_g, v_w_ple_gate, v_w_ple_proj, v_final_norm_g):
    xs = x[0]
    ps = p[0, 0]
    tgt = loss_target[0]
    T, D = xs.shape
    H = a_log.shape[-1]
    DNW = H * HEAD_DIM
    CW = conv_a_w.shape[-1] * 4
    F = w_down.shape[1] * 4
    PD = ps.shape[-1]
    IN_MAIN = 3 * CW + 4 * DNW
    IN_COLS = IN_MAIN + 2 * H
    assert w_in.shape[-1] * 4 == IN_COLS and CW + DNW == D and 2 * H <= LANES
    cb = _tile(min(CW, DNW), 512, LANES)
    while F % cb:
        cb -= LANES
    cidx = lax.axis_index("c").astype(jnp.int32).reshape(1)
    chip = 2 * lax.axis_index("x") + lax.axis_index("y")

    def halves(w):
        sh = w[0].astype(BF16)
        return sh.reshape(2, sh.shape[0] // 2, sh.shape[1])

    def whole(land):
        return land.reshape(4, 2 * land.shape[2], land.shape[3])

    def cols(g4):
        return jnp.transpose(g4, (1, 0, 2)).reshape(g4.shape[1], 4 * g4.shape[2])

    def rows(g4):
        return g4.reshape(4 * g4.shape[1], g4.shape[2])

    conv_shapes = [conv_a_w[0].shape, conv_qkv_w[0].shape, conv_ffn_w[0].shape]
    cpack = _pack_rows([conv_a_w[0], conv_qkv_w[0], conv_ffn_w[0]])
    sh_in, sh_out, sh_up, sh_down, sh_pg, sh_pp = (halves(w) for w in (w_in, w_out, w_up, w_down, w_ple_gate, w_ple_proj))
    l_in, cg = _run_comm(_merge(_ag_comm(sh_in), _ag_comm(cpack.reshape(2, cpack.shape[0] // 2, LANES))), "ag_w_in_conv")
    w_in_f = cols(whole(l_in))
    w_in_main = w_in_f[:, :IN_MAIN]
    w_in_small = jnp.pad(w_in_f[:, IN_MAIN:], ((0, 0), (0, LANES - 2 * H)))
    cg = cg.reshape(4, cpack.shape[0], LANES)
    parts = [_unpack_rows(cg[t], conv_shapes) for t in range(4)]
    cw_a = jnp.concatenate([parts[t][0] for t in range(4)], axis=1)
    cw_qkv = jnp.concatenate([parts[t][1] for t in range(4)], axis=1)
    cw_ffn = jnp.concatenate([parts[t][2] for t in range(4)], axis=1)
    cw_q, cw_k, cw_v = cw_qkv[:, :DNW], cw_qkv[:, DNW:2 * DNW], cw_qkv[:, 2 * DNW:]
    cw_fg, cw_fv = cw_ffn[:, :F], cw_ffn[:, F:]
    pad_row = lambda v: jnp.pad(v, ((0, 0), (0, LANES - v.shape[1])))
    a_log_row, dt_row = pad_row(a_log), pad_row(dt_bias)
    gdn_t = jnp.tile(dn_norm_g, (1, H))
    gfin = final_norm_g.reshape(1, D)

    h1 = _rms_fwd(xs, norm_mix_g, "rms1")
    proj, (l_up,) = _mm(h1, w_in_main, mode="nn", out_dtypes=[F32], name="mm_proj", comm=_ag_comm(sh_up, q=0, nq=2))
    small = _mm(h1, w_in_small, mode="nn", out_dtypes=[F32], name="mm_small")
    ya = _ga_fwd(proj, cw_a, CW, cb)
    nq = 3 * CW // cb
    nd = DNW // cb
    qn = _qkv_fwd(proj, cw_q, nq, True, DNW, cb, "q_fwd")
    kn = _qkv_fwd(proj, cw_k, nq + nd, True, DNW, cb, "k_fwd")
    vs = _qkv_fwd(proj, cw_v, nq + 2 * nd, False, DNW, cb, "v_fwd")
    g, beta = _gb_fwd(small, a_log_row, dt_row, H)
    o, S0, inv_c, (l_up, l_out) = _delta_fwd(qn, kn, vs, g, beta, comm=_merge(_ag_comm(sh_up, l_up, q=1, nq=2), _ag_comm(sh_out)))
    w_out_f = rows(whole(l_out))
    w_out_a, w_out_b = w_out_f[:CW], w_out_f[CW:]
    w_up_4 = whole(l_up)
    z_coff = (3 * CW + 3 * DNW) // DNW
    assert (3 * CW + 3 * DNW) % DNW == 0 and CW % DNW == 0
    yb = _gnorm_fwd(o, proj, z_coff, gdn_t, DNW)
    add = lambda acc, r: (r + acc,)
    x1 = _mm(ya, w_out_a, mode="nn", out_dtypes=[F32], epi=add, extras=[xs], name="mm_out_a")
    x1 = _mm(yb, w_out_b, mode="nn", out_dtypes=[F32], epi=add, extras=[x1], name="mm_out_b")
    h2 = _rms_fwd(x1, norm_ffn_g, "rms2")
    up_g, (l_down,) = _mm(h2, w_up_4, mode="nn", b_split=(0, 2), out_dtypes=[F32], name="mm_up_g",
                          comm=_ag_comm(sh_down, q=0, nq=2))
    up_v, (l_down,) = _mm(h2, w_up_4, mode="nn", b_split=(2, 2), out_dtypes=[F32], name="mm_up_v",
                          comm=_ag_comm(sh_down, l_down, q=1, nq=2))
    w_down_f = rows(whole(l_down))
    act = _ffn_fwd(up_g, up_v, cw_fg, cw_fv, cb)
    x2, (l_pg, l_pp) = _mm(act, w_down_f, mode="nn", out_dtypes=[F32], epi=add, extras=[x1], name="mm_down",
                           comm=_merge(_ag_comm(sh_pg), _ag_comm(sh_pp)))
    w_pg_f = rows(whole(l_pg))
    w_pp_4 = whole(l_pp)
    h3 = _rms_fwd(x2, norm_ple_g, "rms3")
    pp = _mm(ps, w_pp_4, mode="nn", b_split=(0, 4), out_dtypes=[F32], name="mm_pp")

    def ple_epi(acc, x2v, ppv):
        pg = _sigmoid(acc)
        return x2v + pg * ppv, pg

    x3, pg = _mm(h3, w_pg_f, mode="nn", out_dtypes=[F32, F32], epi=ple_epi, extras=[x2, pp], name="mm_pg")

    dx3, fin = _final_fb(x3, tgt, gfin)
    loss = lax.psum(jnp.sum(fin[1]), ("x", "y", "c"))
    d_gfin = fin[0:1]
    dpg, dpp = _ple_bwd(dx3, pp, pg)
    def split_cols(dW):
        R, C4 = dW.shape
        return jnp.transpose(dW.reshape(R, 4, C4 // 4), (1, 0, 2))

    def split_rows(dW):
        return dW.reshape(4, dW.shape[0] // 4, dW.shape[1])

    dW_pp = _mm(ps, dpp, mode="tn", out_split=4, out_dtypes=[F32], name="mm_dw_pp")
    dW_pg = _mm(h3, dpg, mode="tn", out_dtypes=[F32], name="mm_dw_pg")
    P_pp, P_pg = _halves(dW_pp), _halves(split_rows(dW_pg))
    dh3, (A_pp, A_pg) = _mm(dpg, w_pg_f, mode="nt", out_dtypes=[F32], name="mm_dh3",
                            comm=_merge(_swap_comm(P_pp), _swap_comm(P_pg)))
    S_pp = _add_half([P_pp], [A_pp], cidx, "rs_w_pp_add")
    S_pg = _add_half([P_pg], [A_pg], cidx, "rs_w_pg_add")
    dx2, dx2_b, d_gple = _rms_bwd(dh3, x2, norm_ple_g, dx3, "rms3_bwd")
    dW_down, (B_pp, B_pg) = _mm(act, dx2_b, mode="tn", out_dtypes=[F32], name="mm_dw_down",
                                comm=_merge(_a2a_comm(S_pp), _a2a_comm(S_pg)))
    P_down = _halves(split_rows(dW_down))
    dact, (A_down,) = _mm(dx2_b, w_down_f, mode="nt", out_dtypes=[F32], name="mm_dact", comm=_swap_comm(P_down))
    S_down = _add_half([P_down], [A_down], cidx, "rs_w_down_add")
    dup_g, dup_v, dcw_fg, dcw_fv = _ffn_bwd(dact, up_g, up_v, cw_fg, cw_fv, cb)
    dW_up_g, (B_down,) = _mm(h2, dup_g, mode="tn", out_split=2, out_dtypes=[F32], name="mm_dw_up_g", comm=_a2a_comm(S_down))
    P_ug = _halves(dW_up_g)
    dW_up_v, (A_ug,) = _mm(h2, dup_v, mode="tn", out_split=2, out_dtypes=[F32], name="mm_dw_up_v", comm=_swap_comm(P_ug))
    P_uv = _halves(dW_up_v)
    dh2, (A_uv,) = _mm(dup_g, w_up_4, mode="nt", b_split=(0, 2), out_dtypes=[F32], name="mm_dh2_g", comm=_swap_comm(P_uv))
    S_up = _add_half([P_ug, P_uv], [A_ug, A_uv], cidx, "rs_w_up_add")
    dh2 = _mm(dup_v, w_up_4, mode="nt", b_split=(2, 2), out_dtypes=[F32], epi=add, extras=[dh2], name="mm_dh2_v")
    dx1, dx1_b, d_gffn = _rms_bwd(dh2, x1, norm_ffn_g, dx2, "rms2_bwd")
    dW_out_a = _mm(ya, dx1_b, mode="tn", out_dtypes=[F32], name="mm_dw_out_a")
    dW_out_b = _mm(yb, dx1_b, mode="tn", out_dtypes=[F32], name="mm_dw_out_b")
    P_oa, P_ob = _halves(dW_out_a.reshape(-1, D // 4, D)), _halves(dW_out_b.reshape(-1, D // 4, D))
    dymix, (A_oa, A_ob) = _mm(dx1_b, w_out_f, mode="nt", out_dtypes=[F32], name="mm_dymix",
                              comm=_merge(_swap_comm(P_oa), _swap_comm(P_ob)))
    S_out = _add_half([P_oa, P_ob], [A_oa, A_ob], cidx, "rs_w_out_add")
    dax, dab, dac, dcw_a = _ga_bwd(dymix, proj, cw_a, CW, cb)
    do, dz, d_gdn = _gnorm_bwd(dymix, CW // DNW, o, proj, z_coff, gdn_t, DNW)
    dqn, dkn, dvs, dgB, dbB, (B_up, B_out) = _delta_bwd(qn, kn, vs, g, beta, S0, inv_c, do,
                                                        comm=_merge(_a2a_comm(S_up), _a2a_comm(S_out)))
    dq_pre, dcw_q = _qkv_bwd(dqn, proj, cw_q, nq, True, DNW, cb, "q_bwd")
    dk_pre, dcw_k = _qkv_bwd(dkn, proj, cw_k, nq + nd, True, DNW, cb, "k_bwd")
    dv_pre, dcw_v = _qkv_bwd(dvs, proj, cw_v, nq + 2 * nd, False, DNW, cb, "v_bwd")
    dsmall, d_ab = _gb_bwd(dgB, dbB, small, g, beta, a_log_row, dt_row, H)
    dproj = jnp.concatenate([dax, dab, dac, dq_pre, dk_pre, dv_pre, dz], axis=1)
    dW_in_main = _mm(h1, dproj, mode="tn", out_dtypes=[F32], name="mm_dw_in")
    dW_in_small = _mm(h1, dsmall, mode="tn", out_dtypes=[F32], name="mm_dw_in_small")
    P_in = _halves(split_cols(jnp.concatenate([dW_in_main, dW_in_small[:, :2 * H]], axis=1)))
    (A_in,) = _run_comm(_swap_comm(P_in), "rs_w_in_swap")
    S_in = _add_half([P_in], [A_in], cidx, "rs_w_in_add")
    dh1, (B_in,) = _mm(dproj, w_in_main, mode="nt", out_dtypes=[F32], name="mm_dh1", comm=_a2a_comm(S_in))
    dh1 = _mm(dsmall, w_in_small, mode="nt", out_dtypes=[F32], epi=add, extras=[dh1], name="mm_dh1_small")
    dx, _, d_gmix = _rms_bwd(dh1, xs, norm_mix_g, dx1, "rms1_bwd")

    chip_idx = chip.astype(jnp.int32).reshape(1)

    def update(S1, B, w, m, v, name):
        gr = _finish_shard(S1, B, cidx, chip_idx, "rs_" + name)
        delta, m2, v2 = _adamw(w[0], gr, m[0], v[0], "adamw_" + name)
        return gr[None], delta[None], m2[None], v2[None]

    big = {
        "w_in": update(S_in, B_in, w_in, m_w_in, v_w_in, "w_in"),
        "w_out": update(S_out, B_out, w_out, m_w_out, v_w_out, "w_out"),
        "w_up": update(S_up, B_up, w_up, m_w_up, v_w_up, "w_up"),
        "w_down": update(S_down, B_down, w_down, m_w_down, v_w_down, "w_down"),
        "w_ple_gate": update(S_pg, B_pg, w_ple_gate, m_w_ple_gate, v_w_ple_gate, "w_pg"),
        "w_ple_proj": update(S_pp, B_pp, w_ple_proj, m_w_ple_proj, v_w_ple_proj, "w_pp"),
    }

    small_grads = [d_gmix[0:1], dcw_a[:cw_a.shape[0]], jnp.concatenate([dcw_q, dcw_k, dcw_v], axis=1)[:cw_qkv.shape[0]],
                   d_ab[0:1, :H], d_ab[1:2, :H], d_gdn[0:1], d_gffn[0:1],
                   jnp.concatenate([dcw_fg, dcw_fv], axis=1)[:cw_ffn.shape[0]], d_gple[0:1], d_gfin]
    small_shapes = [v.shape for v in small_grads]
    gpack = _pack_rows(small_grads)
    gsum = _sum_stack(_gather_all(gpack, "ag_small"), "sum_small")
    (g_gmix, g_cwa, g_cwqkv, g_alog, g_dt, g_gdn, g_gffn, g_cwffn, g_gple, g_gfin) = _unpack_rows(gsum, small_shapes)

    def my_cols(v):
        Cc = v.shape[1] // 4
        return lax.dynamic_slice_in_dim(v, chip * Cc, Cc, axis=1)

    g_small = [g_gmix, my_cols(g_cwa), my_cols(g_cwqkv), g_alog, g_dt, g_gdn, g_gffn, my_cols(g_cwffn), g_gple, g_gfin]
    w_small = [norm_mix_g, conv_a_w[0], conv_qkv_w[0], a_log, dt_bias, dn_norm_g, norm_ffn_g, conv_ffn_w[0], norm_ple_g, gfin]
    m_small = [m_norm_mix_g, m_conv_a_w[0], m_conv_qkv_w[0], m_a_log, m_dt_bias, m_dn_norm_g, m_norm_ffn_g, m_conv_ffn_w[0],
               m_norm_ple_g, m_final_norm_g.reshape(1, D)]
    v_small = [v_norm_mix_g, v_conv_a_w[0], v_conv_qkv_w[0], v_a_log, v_dt_bias, v_dn_norm_g, v_norm_ffn_g, v_conv_ffn_w[0],
               v_norm_ple_g, v_final_norm_g.reshape(1, D)]
    shp = [v.shape for v in w_small]
    ds_, ms_, vs_ = _adamw(_pack_rows(w_small), _pack_rows(g_small), _pack_rows(m_small), _pack_rows(v_small), "adamw_small")
    out_shapes = [norm_mix_g.shape, conv_a_w.shape, conv_qkv_w.shape, a_log.shape, dt_bias.shape, dn_norm_g.shape,
                  norm_ffn_g.shape, conv_ffn_w.shape, norm_ple_g.shape, final_norm_g.shape]
    rs = lambda vals: [v.reshape(s) for v, s in zip(vals, out_shapes)]
    sg, sd_, sm_, sv_ = rs(g_small), rs(_unpack_rows(ds_, shp)), rs(_unpack_rows(ms_, shp)), rs(_unpack_rows(vs_, shp))
    names_small = ["norm_mix_g", "conv_a_w", "conv_qkv_w", "a_log", "dt_bias", "dn_norm_g", "norm_ffn_g", "conv_ffn_w",
                   "norm_ple_g", "final_norm_g"]
    res = {n: (sg[i], sd_[i], sm_[i], sv_[i]) for i, n in enumerate(names_small)}
    res.update(big)
    order = ["norm_mix_g", "w_in", "conv_a_w", "conv_qkv_w", "a_log", "dt_bias", "dn_norm_g", "w_out", "norm_ffn_g", "w_up",
             "conv_ffn_w", "w_down", "norm_ple_g", "w_ple_gate", "w_ple_proj", "final_norm_g"]
    return (loss, dx[None], *[res[n][0] for n in order], *[res[n][1] for n in order], *[res[n][2] for n in order],
            *[res[n][3] for n in order])
```

```python
import functools

import jax
import jax.numpy as jnp
from jax import lax
from jax.experimental import pallas as pl
from jax.experimental.pallas import tpu as pltpu

F32 = jnp.float32
BF16 = jnp.bfloat16
LANES = 128
HALO = 8
HEAD_DIM = 128
CHUNK = 64
EPS = 1e-6
VMEM_LIMIT = 56 * 1024 * 1024
MM_VMEM_BUDGET = 40 * 1024 * 1024
MM_STEP_BYTES = 1 << 20
EW_VMEM_BUDGET = 28 * 1024 * 1024
MESH = pl.DeviceIdType.MESH

ADAM_LR, ADAM_B1, ADAM_B2, ADAM_EPS, ADAM_WD, ADAM_STEP = 0.001, 0.9, 0.999, 1e-08, 0.01, 10


def _tile(n, cap, unit):
    if n <= cap:
        return n
    d = (cap // unit) * unit
    while d >= unit:
        if n % d == 0:
            return d
        d -= unit
    raise ValueError(f"no tile for {n} (cap {cap}, unit {unit})")


def _sigmoid(x):
    return 1.0 / (1.0 + jnp.exp(-x))


def _divisors(n, cap):
    ds = [d for d in range(cap // LANES * LANES, 0, -LANES) if n % d == 0]
    return [n] if (n <= cap or not ds) else ds


def _mm_tiles(M, N, K, n_unit, k_unit, a_bytes, n_blocks_mn, a_transposed):
    best = None
    for tm in _divisors(M, 1536):
        for tn in _divisors(n_unit, 1536):
            for tk in _divisors(k_unit, 4096):
                nk = K // tk
                vmem = 2 * tm * tk * a_bytes + 2 * tk * tn * 2 + 2 * 4 * tm * tn * n_blocks_mn + (4 * tm * tn if nk > 1 else 0)
                if vmem > MM_VMEM_BUDGET:
                    continue
                steps = (M // tm) * (N // tn) * nk
                cost = (M * K * a_bytes * (N // tn if nk > 1 else 1) + K * N * 2 * (M // tm) + 4 * M * N * n_blocks_mn
                        + (8 * M * N * nk // 3 if nk > 1 else 0) + steps * MM_STEP_BYTES
                        + (2 * steps * tm * tk if a_transposed else 0))
                if best is None or cost < best[0]:
                    best = (cost, tm, tn, tk)
    return best[1:]


def _mm(a, b, *, mode, out_dtypes, name, epi=None, extras=(), comm=None, b_split=None, out_split=None):
    if b_split is not None:
        lo, ns = b_split
        Rb, Cb = b.shape[1], b.shape[2]
    if mode == "nn":
        (M, K), N = a.shape, (ns * Cb if b_split else b.shape[1])
    elif mode == "nt":
        (M, K), N = a.shape, (Rb if b_split else b.shape[0])
    else:
        (K, M), N = a.shape, b.shape[1]
    n_ex, n_out = len(extras), len(out_dtypes)
    n_unit = Cb if (b_split and mode == "nn") else (N // out_split if out_split else N)
    k_unit = Cb if (b_split and mode == "nt") else K
    tm, tn, tk = _mm_tiles(M, N, K, n_unit, k_unit, a.dtype.itemsize, n_ex + n_out, mode == "tn")
    nk = K // tk
    a_spec = pl.BlockSpec((tk, tm), lambda i, j, k: (k, i)) if mode == "tn" else pl.BlockSpec((tm, tk), lambda i, j, k: (i, k))
    if b_split and mode == "nn":
        nb = Cb // tn
        b_spec = pl.BlockSpec((None, tk, tn), lambda i, j, k: (lo + j // nb, k, j % nb))
    elif b_split:
        nb = Cb // tk
        b_spec = pl.BlockSpec((None, tn, tk), lambda i, j, k: (lo + k // nb, j, k % nb))
    else:
        b_spec = pl.BlockSpec((tn, tk), lambda i, j, k: (j, k)) if mode == "nt" else pl.BlockSpec((tk, tn), lambda i, j, k: (k, j))
    mn_spec = pl.BlockSpec((tm, tn), lambda i, j, k: (i, j))
    out_shapes = [jax.ShapeDtypeStruct((M, N), dt) for dt in out_dtypes]
    out_specs = [mn_spec] * n_out
    if out_split:
        assert n_ex == 0 and n_out == 1
        nbo = (N // out_split) // tn
        out_specs = [pl.BlockSpec((None, tm, tn), lambda i, j, k: (j // nbo, i, j % nbo))]
        out_shapes = [jax.ShapeDtypeStruct((out_split, M, N // out_split), out_dtypes[0])]
    dims = {"nn": (((1,), (0,)), ((), ())), "nt": (((1,), (1,)), ((), ())), "tn": (((0,), (0,)), ((), ()))}[mode]

    def body(*refs):
        a_ref, b_ref = refs[0], refs[1]
        ex_refs = refs[2:2 + n_ex]
        out_refs = refs[2 + n_ex:2 + n_ex + n_out]
        part = lax.dot_general(a_ref[...].astype(BF16), b_ref[...].astype(BF16), dims, preferred_element_type=F32)

        def finish(acc):
            outs = (acc,) if epi is None else epi(acc, *[r[...] for r in ex_refs])
            for r, o in zip(out_refs, outs):
                r[...] = o.astype(r.dtype)

        if nk == 1:
            finish(part)
            return
        acc_ref = refs[-1]
        k = pl.program_id(2)

        @pl.when(k == 0)
        def _():
            acc_ref[...] = part

        @pl.when(jnp.logical_and(k > 0, k < nk - 1))
        def _():
            acc_ref[...] += part

        @pl.when(k == nk - 1)
        def _():
            finish(acc_ref[...] + part)

    outs, comm_outs = _call(
        body, name=name, grid=(M // tm, N // tn, nk),
        in_specs=[a_spec, b_spec] + [mn_spec] * n_ex,
        out_specs=out_specs,
        out_shape=out_shapes,
        scratch_shapes=[pltpu.VMEM((tm, tn), F32)] if nk > 1 else [],
        semantics=("parallel", "parallel", "arbitrary"), args=(a, b, *extras), comm=comm)
    res = outs[0] if n_out == 1 else outs
    return res if comm is None else (res, comm_outs)


def _tiled(fn, *, T, C, ins, out_dtypes=(), acc_rows=(), tb=256, cb=512, name):
    tb = _tile(T, tb, HALO)
    nI, nJ = T // tb, C // cb
    hb, nH = tb // HALO, T // HALO
    specs, args, kinds = [], [], []
    for kind, arr, cmap in ins:
        cm = cmap if cmap is not None else (lambda j: j)
        kinds.append(kind)
        if kind == "cur":
            specs.append(pl.BlockSpec((tb, cb), lambda j, i, cm=cm: (i, cm(j))))
            args.append(arr)
        elif kind == "ext":
            specs.append(pl.BlockSpec((HALO, cb), lambda j, i, cm=cm: (jnp.maximum(i * hb - 1, 0), cm(j))))
            specs.append(pl.BlockSpec((tb, cb), lambda j, i, cm=cm: (i, cm(j))))
            specs.append(pl.BlockSpec((HALO, cb), lambda j, i, cm=cm: (jnp.minimum((i + 1) * hb, nH - 1), cm(j))))
            args += [arr, arr, arr]
        elif kind == "row":
            specs.append(pl.BlockSpec((arr.shape[0], cb), lambda j, i, cm=cm: (0, cm(j))))
            args.append(arr)
        elif kind == "stack":
            specs.append(pl.BlockSpec((arr.shape[0], tb, cb), lambda j, i, cm=cm: (0, i, cm(j))))
            args.append(arr)
        else:
            raise ValueError(kind)
    n_in = len(args)
    n_out, n_acc = len(out_dtypes), len(acc_rows)

    def body(*refs):
        j, i = pl.program_id(0), pl.program_id(1)
        vals, r = [], 0
        for kind in kinds:
            if kind == "ext":
                prev = jnp.where(i == 0, 0.0, refs[r][...].astype(F32))
                cur = refs[r + 1][...].astype(F32)
                nxt = jnp.where(i == nI - 1, 0.0, refs[r + 2][...].astype(F32))
                vals.append(jnp.concatenate([prev, cur, nxt], axis=0))
                r += 3
            else:
                vals.append(refs[r][...])
                r += 1
        res = fn(j, i, *vals)
        for ref, o in zip(refs[n_in:n_in + n_out], res[:n_out]):
            ref[...] = o.astype(ref.dtype)
        for ref, o in zip(refs[n_in + n_out:], res[n_out:]):
            @pl.when(i == 0)
            def _(ref=ref, o=o):
                ref[...] = o

            @pl.when(i > 0)
            def _(ref=ref, o=o):
                ref[...] += o

    outs = pl.pallas_call(
        body, name=name, grid=(nJ, nI), in_specs=specs,
        out_specs=[pl.BlockSpec((tb, cb), lambda j, i: (i, j))] * n_out
        + [pl.BlockSpec((rows, cb), lambda j, i: (0, j)) for rows in acc_rows],
        out_shape=[jax.ShapeDtypeStruct((T, C), dt) for dt in out_dtypes]
        + [jax.ShapeDtypeStruct((rows, C), F32) for rows in acc_rows],
        compiler_params=pltpu.CompilerParams(dimension_semantics=("parallel", "arbitrary"),
                                             vmem_limit_bytes=VMEM_LIMIT),
    )(*args)
    return outs


def _conv_causal(xe, w):
    K = w.shape[0]
    y = xe * w[K - 1:K]
    for j in range(K - 1):
        y = y + pltpu.roll(xe, K - 1 - j, 0) * w[j:j + 1]
    return y


def _conv_anti(de, w):
    K, n = w.shape[0], de.shape[0]
    y = de * w[K - 1:K]
    for j in range(K - 1):
        y = y + pltpu.roll(de, n - (K - 1 - j), 0) * w[j:j + 1]
    return y


def _conv_dw(dce, xe, K):
    n = dce.shape[0]
    tb = n - 2 * HALO
    rows = []
    for j in range(K):
        xs = xe if j == K - 1 else pltpu.roll(xe, K - 1 - j, 0)
        rows.append(jnp.sum((dce * xs)[HALO:HALO + tb], axis=0, keepdims=True))
    rows.append(jnp.zeros((HALO - K, dce.shape[1]), F32))
    return jnp.concatenate(rows, axis=0)


def _own(xe):
    return xe[HALO:xe.shape[0] - HALO]


def _row0(v):
    return jnp.concatenate([v, jnp.zeros((HALO - 1, v.shape[1]), F32)], axis=0)


def _per_head(fn, *xs):
    n = xs[0].shape[1] // HEAD_DIM
    outs = [fn(*[x[:, g * HEAD_DIM:(g + 1) * HEAD_DIM] for x in xs]) for g in range(n)]
    return outs[0] if n == 1 else jnp.concatenate(outs, axis=1)


def _rms_fwd(x, g, name):
    T, D = x.shape

    def fn(j, i, xv, gv):
        r = lax.rsqrt(jnp.mean(xv * xv, axis=1, keepdims=True) + EPS)
        return (xv * r * gv,)

    return _tiled(fn, T=T, C=D, ins=[("cur", x, None), ("row", g, None)], out_dtypes=[BF16], cb=D, name=name)[0]


def _rms_bwd_math(dy, xv, gv):
    r = lax.rsqrt(jnp.mean(xv * xv, axis=1, keepdims=True) + EPS)
    xh = xv * r
    dxh = dy * gv
    dx = r * (dxh - xh * jnp.mean(dxh * xh, axis=1, keepdims=True))
    dg = jnp.sum(dy * xh, axis=0, keepdims=True)
    return dx, dg


def _rms_bwd(dh, x, g, dres, name):
    T, D = x.shape

    def fn(j, i, dhv, xv, gv, dr):
        dx, dg = _rms_bwd_math(dhv, xv, gv)
        return dr + dx, dr + dx, _row0(dg)

    return _tiled(fn, T=T, C=D, ins=[("cur", dh, None), ("cur", x, None), ("row", g, None), ("cur", dres, None)],
                  out_dtypes=[F32, BF16], acc_rows=[HALO], cb=D, name=name)


def _final_fb(x3, tgt, g, pp, pg):
    T, D = x3.shape

    def fn(j, i, xv, tv, gv, ppv, pgv):
        r = lax.rsqrt(jnp.mean(xv * xv, axis=1, keepdims=True) + EPS)
        xh = xv * r
        e = xh * gv - tv
        dy = e * (1.0 / D)
        dxh = dy * gv
        dx = r * (dxh - xh * jnp.mean(dxh * xh, axis=1, keepdims=True))
        dg = jnp.sum(dy * xh, axis=0, keepdims=True)
        ls = jnp.sum(e * e, axis=0, keepdims=True) * (0.5 / D)
        return (dx, dx * ppv * pgv * (1.0 - pgv), dx * pgv,
                jnp.concatenate([dg, ls, jnp.zeros((HALO - 2, D), F32)], axis=0))

    return _tiled(fn, T=T, C=D, ins=[("cur", x3, None), ("cur", tgt, None), ("row", g, None), ("cur", pp, None),
                                      ("cur", pg, None)], out_dtypes=[F32, BF16, BF16], acc_rows=[HALO], cb=D, name="final_fb")


def _ga_fwd(proj, w_a, CW, cb):
    T = proj.shape[0]
    n = CW // cb

    def fn(j, i, ax, ab, ac, w):
        c = _conv_causal(ac * ax, w)
        return (ab * _own(c),)

    return _tiled(fn, T=T, C=CW, ins=[("ext", proj, None), ("cur", proj, lambda j: j + n), ("ext", proj, lambda j: j + 2 * n),
                                       ("row", w_a, None)], out_dtypes=[BF16], cb=cb, name="ga_fwd")[0]


def _ga_bwd(dymix, proj, w_a, CW, cb):
    T = proj.shape[0]
    n = CW // cb
    K = w_a.shape[0]

    def fn(j, i, dy, ax, ab, ac, w):
        u = ac * ax
        c = _conv_causal(u, w)
        dc = dy * ab
        du = _conv_anti(dc, w)
        return _own(du * ac), _own(dy * c), _own(du * ax), _conv_dw(dc, u, K)

    return _tiled(fn, T=T, C=CW, ins=[("ext", dymix, None), ("ext", proj, None), ("ext", proj, lambda j: j + n),
                                       ("ext", proj, lambda j: j + 2 * n), ("row", w_a, None)],
                  out_dtypes=[BF16, BF16, BF16], acc_rows=[HALO], cb=cb, name="ga_bwd")


def _l2n(s):
    return s * lax.rsqrt(jnp.sum(s * s, axis=1, keepdims=True) + EPS)


def _qkv_fwd(proj, w_sec, coff, normalize, DNW, cb, name):
    T = proj.shape[0]

    def fn(j, i, pre, w):
        c = _own(_conv_causal(pre, w))
        s = c * _sigmoid(c)
        return (_per_head(_l2n, s) if normalize else s,)

    return _tiled(fn, T=T, C=DNW, ins=[("ext", proj, lambda j: j + coff), ("row", w_sec, None)],
                  out_dtypes=[F32], cb=cb, name=name)[0]


def _qkv_bwd(dsec, proj, w_sec, coff, normalize, DNW, cb, name):
    T = proj.shape[0]
    K = w_sec.shape[0]

    def l2n_bwd(s, dn):
        r = lax.rsqrt(jnp.sum(s * s, axis=1, keepdims=True) + EPS)
        nrm = s * r
        return r * (dn - nrm * jnp.sum(dn * nrm, axis=1, keepdims=True))

    def fn(j, i, dn, pre, w):
        c = _conv_causal(pre, w)
        sg = _sigmoid(c)
        s = c * sg
        ds = _per_head(l2n_bwd, s, dn) if normalize else dn
        dc = ds * (sg * (1.0 + c * (1.0 - sg)))
        return _own(_conv_anti(dc, w)), _conv_dw(dc, pre, K)

    return _tiled(fn, T=T, C=DNW, ins=[("ext", dsec, None), ("ext", proj, lambda j: j + coff), ("row", w_sec, None)],
                  out_dtypes=[BF16], acc_rows=[HALO], cb=cb, name=name)


def _gb_fwd(small, a_log_row, dt_row, H):
    T = small.shape[0]

    def fn(j, i, sm, al, dt):
        z = sm + dt
        sp = jnp.maximum(z, 0.0) + jnp.log(1.0 + jnp.exp(-jnp.abs(z)))
        g = -jnp.exp(al) * sp
        beta = _sigmoid(pltpu.roll(sm, LANES - H, 1))
        return g, beta

    return _tiled(fn, T=T, C=LANES, ins=[("cur", small, None), ("row", a_log_row, None), ("row", dt_row, None)],
                  out_dtypes=[F32, F32], cb=LANES, name="gb_fwd")


def _gb_bwd(dgB, dbB, small, g, beta, a_log_row, dt_row, H):
    T = small.shape[0]

    def fn(j, i, dgv, dbv, sm, gv, bv, al, dt):
        lane = lax.broadcasted_iota(jnp.int32, sm.shape, 1)
        dg = jnp.zeros(sm.shape, F32)
        db = jnp.zeros(sm.shape, F32)
        for h in range(H):
            dg = jnp.where(lane == h, jnp.sum(dgv[h], axis=1, keepdims=True), dg)
            db = jnp.where(lane == h, jnp.sum(dbv[h], axis=1, keepdims=True), db)
        da = dg * (-jnp.exp(al)) * _sigmoid(sm + dt)
        dbb = db * bv * (1.0 - bv)
        dsm = jnp.where(lane < H, da, 0.0) + pltpu.roll(jnp.where(lane < H, dbb, 0.0), H, 1)
        d_alog = jnp.sum(jnp.where(lane < H, dg * gv, 0.0), axis=0, keepdims=True)
        d_dt = jnp.sum(jnp.where(lane < H, da, 0.0), axis=0, keepdims=True)
        return dsm, jnp.concatenate([d_alog, d_dt, jnp.zeros((HALO - 2, LANES), F32)], axis=0)

    return _tiled(fn, T=T, C=LANES, ins=[("stack", dgB, None), ("stack", dbB, None), ("cur", small, None), ("cur", g, None),
                                          ("cur", beta, None), ("row", a_log_row, None), ("row", dt_row, None)],
                  out_dtypes=[BF16], acc_rows=[HALO], cb=LANES, name="gb_bwd")


_DIMS = {"nn": (((1,), (0,)), ((), ())), "nt": (((1,), (1,)), ((), ())), "tn": (((0,), (0,)), ((), ()))}
_DOT_BWD = {"nn": (("nt", "gb"), ("tn", "ag")), "nt": (("nn", "gb"), ("tn", "ga")), "tn": (("nt", "bg"), ("nn", "ag"))}


def _split(a):
    hi = a.astype(BF16)
    return hi, (a - hi.astype(F32)).astype(BF16)


def _raw_dot(a, b, kind, passes):
    dg = lambda x, y: lax.dot_general(x, y, _DIMS[kind], preferred_element_type=F32)
    if passes == 1:
        return dg(a.astype(BF16), b.astype(BF16))
    ah, al = _split(a)
    bh, bl = _split(b)
    return dg(ah, bh) + (dg(ah, bl) + dg(al, bh))


@functools.lru_cache(maxsize=None)
def _dotf(kind, passes):
    @jax.custom_vjp
    def f(a, b):
        return _raw_dot(a, b, kind, passes)

    def fwd(a, b):
        return _raw_dot(a, b, kind, passes), (a, b)

    def bwd(res, g):
        ops = {"a": res[0], "b": res[1], "g": g}
        (ka, oa), (kb, ob) = _DOT_BWD[kind]
        return (_raw_dot(ops[oa[0]], ops[oa[1]], ka, passes), _raw_dot(ops[ob[0]], ops[ob[1]], kb, passes))

    f.defvjp(fwd, bwd)
    return f


@jax.custom_vjp
def _saved_inverse(L, inv):
    return inv


def _saved_inverse_fwd(L, inv):
    return inv, inv


def _saved_inverse_bwd(inv, g):
    d3nt, d3tn = _dotf("nt", 3), _dotf("tn", 3)
    return -d3nt(d3tn(inv, g), inv), jnp.zeros_like(inv)


_saved_inverse.defvjp(_saved_inverse_fwd, _saved_inverse_bwd)


def _chunk_fn(q, k, v, gB, bB, S, inv_saved=None):
    C = CHUNK
    d3, d3nt = _dotf("nn", 3), _dotf("nt", 3)
    d1, d1nt, d1tn = _dotf("nn", 1), _dotf("nt", 1), _dotf("tn", 1)
    each = lambda f, *ls: tuple(f(*xs) for xs in zip(*ls))
    row = lax.broadcasted_iota(jnp.int32, (C, C), 0)
    col = lax.broadcasted_iota(jnp.int32, (C, C), 1)
    causal = row >= col
    strict = row > col
    tril = jnp.where(causal, 1.0, 0.0).astype(F32)
    eye = jnp.where(row == col, 1.0, 0.0).astype(F32)
    avg = jnp.full((C, HEAD_DIM), 1.0 / HEAD_DIM, F32)
    gc = each(lambda g: d3(tril, g), gB)
    R = each(lambda g: d3nt(avg, g), gc)
    decay = each(lambda g, r: jnp.where(causal, jnp.exp(jnp.where(causal, g[:, :C] - r, 0.0)), 0.0), gc, R)
    kk = each(lambda x: d1nt(x, x), k)
    L = each(lambda a, d, b: jnp.where(strict, a * d * b[:, :C], 0.0), kk, decay, bB)
    if inv_saved is None:
        inv = each(lambda l: eye - l, L)
        P = L
        for _ in range(5):
            P = each(lambda p: d3(p, p), P)
            inv = each(lambda a, p: d3(a, eye + p), inv, P)
    else:
        inv = each(_saved_inverse, L, inv_saved)
    eg = each(jnp.exp, gc)
    u = each(lambda a, x, b: d3(a, x * b), inv, v, bB)
    w = each(lambda a, x, b, e: d3(a, x * b * e), inv, k, bB, eg)
    qs = each(lambda x: x * (HEAD_DIM ** -0.5), q)
    qk = each(lambda a, x, d: d1nt(a, x) * d, qs, k, decay)
    gl = each(lambda g: g[C - 1:C, :], gc)
    v_new = each(lambda a, b, s: a - d1(b, s), u, w, S)
    o1 = each(lambda a, e, s: d1(a * e, s), qs, eg, S)
    o = each(lambda a, b, c: a + d1(b, c), o1, qk, v_new)
    kv = each(lambda x, a, g, vn: d1tn(x * jnp.exp(a - g), vn), k, gl, gc, v_new)
    S_new = each(lambda s, a, b: s * jnp.exp(a) + b, S, gl, kv)
    return (o, S_new), inv


def _sel_lane(x, h):
    lane = lax.broadcasted_iota(jnp.int32, x.shape, 1)
    return jnp.broadcast_to(jnp.sum(jnp.where(lane == h, x, 0.0), axis=1, keepdims=True), x.shape)


def _head(ref, h):
    return ref[:, h * HEAD_DIM:(h + 1) * HEAD_DIM]


def _delta_fwd(q, k, v, g, beta, comm=None):
    T = q.shape[0]
    H, N = q.shape[1] // HEAD_DIM, T // CHUNK

    def body(q_ref, k_ref, v_ref, g_ref, b_ref, o_ref, s_ref, inv_ref, S):
        @pl.when(pl.program_id(0) == 0)
        def _():
            S[...] = jnp.zeros_like(S)

        gv, bv = g_ref[...], b_ref[...]
        heads = lambda f: tuple(f(h) for h in range(H))
        S_in = heads(lambda h: S[h])
        for h in range(H):
            s_ref[h, 0] = S_in[h]
        (o, S_new), inv = _chunk_fn(heads(lambda h: _head(q_ref, h)), heads(lambda h: _head(k_ref, h)),
                                    heads(lambda h: _head(v_ref, h)), heads(lambda h: _sel_lane(gv, h)),
                                    heads(lambda h: _sel_lane(bv, h)), S_in)
        for h in range(H):
            o_ref[:, h * HEAD_DIM:(h + 1) * HEAD_DIM] = o[h]
            inv_ref[h, 0] = inv[h]
            S[h] = S_new[h]

    blk = pl.BlockSpec((CHUNK, H * HEAD_DIM), lambda n: (n, 0))
    gblk = pl.BlockSpec((CHUNK, LANES), lambda n: (n, 0))
    outs, comm_outs = _call(
        body, name="delta_fwd", grid=(N,), in_specs=[blk, blk, blk, gblk, gblk],
        out_specs=[blk, pl.BlockSpec((H, 1, HEAD_DIM, HEAD_DIM), lambda n: (0, n, 0, 0)),
                   pl.BlockSpec((H, 1, CHUNK, CHUNK), lambda n: (0, n, 0, 0))],
        out_shape=[jax.ShapeDtypeStruct((T, H * HEAD_DIM), F32), jax.ShapeDtypeStruct((H, N, HEAD_DIM, HEAD_DIM), F32),
                   jax.ShapeDtypeStruct((H, N, CHUNK, CHUNK), F32)],
        scratch_shapes=[pltpu.VMEM((H, HEAD_DIM, HEAD_DIM), F32)],
        semantics=("arbitrary",), args=(q, k, v, g, beta), comm=comm)
    return outs[0], outs[1], outs[2], comm_outs


def _delta_bwd(q, k, v, g, beta, S0, inv, do, comm=None):
    T = q.shape[0]
    H, N = q.shape[1] // HEAD_DIM, T // CHUNK

    def body(q_ref, k_ref, v_ref, g_ref, b_ref, s_ref, inv_ref, do_ref, dq_ref, dk_ref, dv_ref, dg_ref, db_ref, dS):
        @pl.when(pl.program_id(0) == 0)
        def _():
            dS[...] = jnp.zeros_like(dS)

        gv, bv = g_ref[...], b_ref[...]
        heads = lambda f: tuple(f(h) for h in range(H))
        _, vjp, _ = jax.vjp(_chunk_fn, heads(lambda h: _head(q_ref, h)), heads(lambda h: _head(k_ref, h)),
                            heads(lambda h: _head(v_ref, h)), heads(lambda h: _sel_lane(gv, h)), heads(lambda h: _sel_lane(bv, h)),
                            heads(lambda h: s_ref[h, 0]), heads(lambda h: inv_ref[h, 0]), has_aux=True)
        dq, dk, dv, dgB, dbB, dS_prev, _ = vjp((heads(lambda h: _head(do_ref, h)), heads(lambda h: dS[h])))
        for h in range(H):
            sl = slice(h * HEAD_DIM, (h + 1) * HEAD_DIM)
            dq_ref[:, sl] = dq[h]
            dk_ref[:, sl] = dk[h]
            dv_ref[:, sl] = dv[h]
            dg_ref[h] = dgB[h]
            db_ref[h] = dbB[h]
            dS[h] = dS_prev[h]

    blk = pl.BlockSpec((CHUNK, H * HEAD_DIM), lambda n: (N - 1 - n, 0))
    gblk = pl.BlockSpec((CHUNK, LANES), lambda n: (N - 1 - n, 0))
    hblk = pl.BlockSpec((H, CHUNK, LANES), lambda n: (0, N - 1 - n, 0))
    sd = jax.ShapeDtypeStruct
    outs, comm_outs = _call(
        body, name="delta_bwd", grid=(N,),
        in_specs=[blk, blk, blk, gblk, gblk, pl.BlockSpec((H, 1, HEAD_DIM, HEAD_DIM), lambda n: (0, N - 1 - n, 0, 0)),
                  pl.BlockSpec((H, 1, CHUNK, CHUNK), lambda n: (0, N - 1 - n, 0, 0)), blk],
        out_specs=[blk, blk, blk, hblk, hblk],
        out_shape=[sd((T, H * HEAD_DIM), F32)] * 3 + [sd((H, T, LANES), F32)] * 2,
        scratch_shapes=[pltpu.VMEM((H, HEAD_DIM, HEAD_DIM), F32)],
        semantics=("arbitrary",), args=(q, k, v, g, beta, S0, inv, do), comm=comm)
    return (*outs, comm_outs)


def _gnorm_fwd(o, proj, z_coff, gdn_t, DNW):
    T = o.shape[0]

    def fn(j, i, ov, zv, gv):
        def one(oh, zh, gh):
            r = lax.rsqrt(jnp.mean(oh * oh, axis=1, keepdims=True) + EPS)
            return oh * r * gh * (zh * _sigmoid(zh))
        return (_per_head(one, ov, zv, jnp.broadcast_to(gv, ov.shape)),)

    return _tiled(fn, T=T, C=DNW, ins=[("cur", o, None), ("cur", proj, lambda j: j + z_coff), ("row", gdn_t, None)],
                  out_dtypes=[BF16], cb=DNW, name="gnorm_fwd")[0]


def _gnorm_bwd(dymix, y_coff, o, proj, z_coff, gdn_t, DNW):
    T = o.shape[0]
    nh = DNW // HEAD_DIM

    def fn(j, i, dy, ov, zv, gv):
        dos, dzs, dgs = [], [], jnp.zeros((1, HEAD_DIM), F32)
        for h in range(nh):
            sl = slice(h * HEAD_DIM, (h + 1) * HEAD_DIM)
            dyh, oh, zh, gh = dy[:, sl].astype(F32), ov[:, sl], zv[:, sl], gv[:, sl]
            r = lax.rsqrt(jnp.mean(oh * oh, axis=1, keepdims=True) + EPS)
            on = oh * r
            sg = _sigmoid(zh)
            sz = zh * sg
            dzs.append(dyh * on * gh * (sg * (1.0 + zh * (1.0 - sg))))
            don = dyh * gh * sz
            dos.append(r * (don - on * jnp.mean(don * on, axis=1, keepdims=True)))
            dgs = dgs + jnp.sum(dyh * on * sz, axis=0, keepdims=True)
        cat = (lambda xs: xs[0] if nh == 1 else jnp.concatenate(xs, axis=1))
        return cat(dos), cat(dzs), _row0(dgs)

    T_ = T
    nI = T_ // _tile(T_, 256, HALO)
    tb = T_ // nI
    specs_cb = DNW

    def body_wrap():
        def body(dy_ref, o_ref, z_ref, g_ref, do_ref, dz_ref, dg_ref):
            i = pl.program_id(0)
            d_o, d_z, d_g = fn(0, i, dy_ref[...], o_ref[...], z_ref[...], g_ref[...])
            do_ref[...] = d_o
            dz_ref[...] = d_z.astype(dz_ref.dtype)

            @pl.when(i == 0)
            def _():
                dg_ref[...] = d_g

            @pl.when(i > 0)
            def _():
                dg_ref[...] += d_g

        return pl.pallas_call(
            body, name="gnorm_bwd", grid=(nI,),
            in_specs=[pl.BlockSpec((tb, specs_cb), lambda i: (i, y_coff)), pl.BlockSpec((tb, specs_cb), lambda i: (i, 0)),
                      pl.BlockSpec((tb, specs_cb), lambda i: (i, z_coff)), pl.BlockSpec((1, specs_cb), lambda i: (0, 0))],
            out_specs=[pl.BlockSpec((tb, specs_cb), lambda i: (i, 0)), pl.BlockSpec((tb, specs_cb), lambda i: (i, 0)),
                       pl.BlockSpec((HALO, HEAD_DIM), lambda i: (0, 0))],
            out_shape=[jax.ShapeDtypeStruct((T_, DNW), F32), jax.ShapeDtypeStruct((T_, DNW), BF16),
                       jax.ShapeDtypeStruct((HALO, HEAD_DIM), F32)],
            compiler_params=pltpu.CompilerParams(dimension_semantics=("arbitrary",), vmem_limit_bytes=VMEM_LIMIT),
        )(dymix, o, proj, gdn_t)

    return body_wrap()


def _ffn_fwd(up_g, up_v, w_g, w_v, cb):
    T, F = up_g.shape

    def fn(j, i, ug, uv, wg, wv):
        cg = _own(_conv_causal(ug, wg))
        cv = _own(_conv_causal(uv, wv))
        return (cg * _sigmoid(cg) * cv,)

    return _tiled(fn, T=T, C=F, ins=[("ext", up_g, None), ("ext", up_v, None), ("row", w_g, None), ("row", w_v, None)],
                  out_dtypes=[BF16], tb=512, cb=cb, name="ffn_fwd")[0]


def _ffn_bwd(dact, up_g, up_v, w_g, w_v, cb):
    T, F = up_g.shape
    K = w_g.shape[0]

    def fn(j, i, da, ug, uv, wg, wv):
        cg = _conv_causal(ug, wg)
        cv = _conv_causal(uv, wv)
        sg = _sigmoid(cg)
        dgate = da * cv * (sg * (1.0 + cg * (1.0 - sg)))
        dval = da * (cg * sg)
        return (_own(_conv_anti(dgate, wg)), _own(_conv_anti(dval, wv)), _conv_dw(dgate, ug, K), _conv_dw(dval, uv, K))

    return _tiled(fn, T=T, C=F, ins=[("ext", dact, None), ("ext", up_g, None), ("ext", up_v, None), ("row", w_g, None),
                                      ("row", w_v, None)], out_dtypes=[BF16, BF16], acc_rows=[HALO, HALO], tb=512, cb=cb,
                  name="ffn_bwd")


def _wide(R, Cc, n_f32, unit=HALO):
    cb = Cc if (Cc % LANES or Cc <= 4096) else _tile(Cc, 2048, LANES)
    cap = max(unit, EW_VMEM_BUDGET // (2 * 4 * n_f32 * cb) // unit * unit)
    return _tile(R, cap, unit), cb


def _adamw(w, g, m, v, name):
    R, Cc = w.shape
    tb, cb = _wide(R, Cc, 7)
    c1 = 1.0 / (1.0 - ADAM_B1 ** ADAM_STEP)
    c2 = 1.0 / (1.0 - ADAM_B2 ** ADAM_STEP)

    def fn(j, i, wv, gv, mv, vv):
        m2 = ADAM_B1 * mv + (1.0 - ADAM_B1) * gv
        v2 = ADAM_B2 * vv + (1.0 - ADAM_B2) * (gv * gv)
        delta = -ADAM_LR * ((m2 * c1) / (jnp.sqrt(v2 * c2) + ADAM_EPS) + ADAM_WD * wv)
        return delta, m2, v2

    return _tiled(fn, T=R, C=Cc, ins=[("cur", w, None), ("cur", g, None), ("cur", m, None), ("cur", v, None)],
                  out_dtypes=[F32, F32, F32], tb=tb, cb=cb, name=name)


def _sum_stack(st, name):
    S, R, Cc = st.shape
    cb = _tile(Cc, 512, LANES) if Cc % LANES == 0 else Cc

    def fn(j, i, sv):
        t = sv[0]
        for s in range(1, S):
            t = t + sv[s]
        return (t,)

    return _tiled(fn, T=R, C=Cc, ins=[("stack", st, None)], out_dtypes=[F32], cb=cb, name=name)[0]


ANY = pl.BlockSpec(memory_space=pl.ANY)


def _place():
    x, y, c = lax.axis_index("x"), lax.axis_index("y"), lax.axis_index("c")
    return x, y, c, 2 * x + y


def _chip_dev(s, c):
    return (s // 2, s % 2, c)


class _Comm:
    def __init__(self, ins, out_shapes, sems, start, wait, aliases=None):
        self.ins, self.out_shapes, self.sems = list(ins), list(out_shapes), list(sems)
        self.start, self.wait, self.aliases = start, wait, dict(aliases or {})


def _merge(*comms):
    offs, i, o, s = [], 0, 0, 0
    for cm in comms:
        offs.append((i, o, s))
        i, o, s = i + len(cm.ins), o + len(cm.out_shapes), s + len(cm.sems)

    def part(refs, k, cm):
        i0, o0, s0 = offs[k]
        return refs[0][i0:i0 + len(cm.ins)], refs[1][o0:o0 + len(cm.out_shapes)], refs[2][s0:s0 + len(cm.sems)]

    def start(*refs):
        for k, cm in enumerate(comms):
            cm.start(*part(refs, k, cm))

    def wait(*refs):
        for k, cm in enumerate(comms):
            cm.wait(*part(refs, k, cm))

    aliases = {}
    for k, cm in enumerate(comms):
        for a, b in cm.aliases.items():
            aliases[offs[k][0] + a] = offs[k][1] + b
    return _Comm([a for cm in comms for a in cm.ins], [a for cm in comms for a in cm.out_shapes],
                 [a for cm in comms for a in cm.sems], start, wait, aliases)


def _call(body, *, name, grid, in_specs, out_specs, out_shape, scratch_shapes, semantics, args, comm=None):
    if comm is None:
        outs = pl.pallas_call(
            body, name=name, grid=grid, in_specs=in_specs, out_specs=out_specs, out_shape=out_shape,
            scratch_shapes=list(scratch_shapes),
            compiler_params=pltpu.CompilerParams(dimension_semantics=semantics, vmem_limit_bytes=VMEM_LIMIT))(*args)
        return list(outs), []
    n_in, n_out, n_scr = len(in_specs), len(out_specs), len(scratch_shapes)
    ci, co = len(comm.ins), len(comm.out_shapes)

    def wrapped(*refs):
        r = 0
        ins, r = refs[r:r + n_in], r + n_in
        cins, r = refs[r:r + ci], r + ci
        outs, r = refs[r:r + n_out], r + n_out
        couts, r = refs[r:r + co], r + co
        scr, r = refs[r:r + n_scr], r + n_scr
        csems = refs[r:]
        ids = [pl.program_id(a) for a in range(len(grid))]
        first, last = ids[0] == 0, ids[0] == grid[0] - 1
        for a in range(1, len(grid)):
            first = jnp.logical_and(first, ids[a] == 0)
            last = jnp.logical_and(last, ids[a] == grid[a] - 1)

        @pl.when(first)
        def _():
            comm.start(cins, couts, csems)

        body(*ins, *outs, *scr)

        @pl.when(last)
        def _():
            comm.wait(cins, couts, csems)

    outs = pl.pallas_call(
        wrapped, name=name, grid=grid, in_specs=list(in_specs) + [ANY] * ci, out_specs=list(out_specs) + [ANY] * co,
        out_shape=list(out_shape) + comm.out_shapes, scratch_shapes=list(scratch_shapes) + comm.sems,
        input_output_aliases={n_in + a: n_out + b for a, b in comm.aliases.items()},
        compiler_params=pltpu.CompilerParams(dimension_semantics=("arbitrary",) * len(grid), vmem_limit_bytes=VMEM_LIMIT),
    )(*args, *comm.ins)
    return list(outs[:n_out]), list(outs[n_out:])


def _run_comm(comm, name):
    ci, co = len(comm.ins), len(comm.out_shapes)

    def body(*refs):
        cins, couts, csems = refs[:ci], refs[ci:ci + co], refs[ci + co:]
        comm.start(cins, couts, csems)
        comm.wait(cins, couts, csems)

    outs = pl.pallas_call(body, name=name, in_specs=[ANY] * ci, out_specs=[ANY] * co, out_shape=comm.out_shapes,
                          scratch_shapes=comm.sems, input_output_aliases=comm.aliases)(*comm.ins)
    return list(outs)


def _ag_comm(shard, land=None, q=0, nq=1):
    two, R2, Cc = shard.shape
    rows = pl.ds(q * (R2 // nq), R2 // nq)
    DMA = pltpu.SemaphoreType.DMA

    def copies(ins, outs, sems, which):
        sh, out = ins[0], outs[0]
        send1, recv1, send2, recv2, send0, recv0 = sems
        x, y, c, s = _place()
        sib = (x, y, 1 - c)
        rc = pltpu.make_async_remote_copy
        if which == "first":
            return [rc(sh.at[c, rows], out.at[s, c, rows], send1.at[m - 1], recv1.at[m - 1],
                       device_id=_chip_dev(s ^ m, c), device_id_type=MESH) for m in range(1, 4)]
        if which == "own":
            return [rc(sh.at[h, rows], out.at[s, h, rows], send0.at[h], recv0.at[h], device_id=sib, device_id_type=MESH)
                    for h in range(2)]
        if which == "landed":
            return [rc(sh.at[c, rows], out.at[s ^ m, c, rows], send1.at[m - 1], recv1.at[m - 1], device_id=sib,
                       device_id_type=MESH) for m in range(1, 4)]
        half = c if which == "passed" else 1 - c
        return [rc(out.at[s ^ m, half, rows], out.at[s ^ m, half, rows], send2.at[m - 1], recv2.at[m - 1], device_id=sib,
                   device_id_type=MESH) for m in range(1, 4)]

    def start(ins, outs, sems):
        for cp in copies(ins, outs, sems, "first") + copies(ins, outs, sems, "own"):
            cp.start()

    def wait(ins, outs, sems):
        passed = copies(ins, outs, sems, "passed")
        for lan, pas in zip(copies(ins, outs, sems, "landed"), passed):
            lan.wait_recv()
            pas.start()
        for cp in copies(ins, outs, sems, "handed"):
            cp.wait_recv()
        for cp in copies(ins, outs, sems, "own"):
            cp.wait()
        for cp in copies(ins, outs, sems, "first") + passed:
            cp.wait_send()

    return _Comm([shard] + ([land] if land is not None else []), [jax.ShapeDtypeStruct((4, two, R2, Cc), shard.dtype)],
                 [DMA((3,)), DMA((3,)), DMA((3,)), DMA((3,)), DMA((2,)), DMA((2,))], start, wait,
                 {1: 0} if land is not None else None)


def _a2a_comm(S1, q=0, nq=1, land=None):
    S4, R2, Cc = S1.shape
    rows = pl.ds(q * (R2 // nq), R2 // nq)
    DMA = pltpu.SemaphoreType.DMA

    def copies(ins, outs, sems):
        x, y, c, s = _place()
        return [pltpu.make_async_remote_copy(ins[0].at[s ^ m, rows], outs[0].at[m - 1, rows], sems[0].at[m - 1],
                                             sems[1].at[m - 1], device_id=_chip_dev(s ^ m, c), device_id_type=MESH)
                for m in range(1, 4)]

    def start(ins, outs, sems):
        for cp in copies(ins, outs, sems):
            cp.start()

    def wait(ins, outs, sems):
        for cp in copies(ins, outs, sems):
            cp.wait()

    return _Comm([S1] + ([land] if land is not None else []), [jax.ShapeDtypeStruct((3, R2, Cc), S1.dtype)],
                 [DMA((3,)), DMA((3,))], start, wait, {1: 0} if land is not None else None)


def _halves(G):
    return G.reshape(G.shape[0], 2, G.shape[1] // 2, G.shape[2])


def _swap_comm(piece):
    n, two, R2, Cc = piece.shape
    DMA = pltpu.SemaphoreType.DMA

    def copies(ins, outs, sems):
        x, y, c, s = _place()
        return [pltpu.make_async_remote_copy(ins[0].at[t, 1 - c], outs[0].at[t], sems[0].at[t], sems[1].at[t],
                                             device_id=(x, y, 1 - c), device_id_type=MESH) for t in range(n)]

    def start(ins, outs, sems):
        for cp in copies(ins, outs, sems):
            cp.start()

    def wait(ins, outs, sems):
        for cp in copies(ins, outs, sems):
            cp.wait()

    return _Comm([piece], [jax.ShapeDtypeStruct((n, R2, Cc), piece.dtype)], [DMA((n,)), DMA((n,))], start, wait)


def _add_half(pieces, As, cidx, name):
    R2, Cc = pieces[0].shape[2:]
    S4 = sum(pc.shape[0] for pc in pieces)
    tb, cb = _wide(R2, Cc, 3, 2 * HALO)
    nI, nJ = R2 // tb, Cc // cb

    def body(c_ref, g_ref, a_ref, *rest):
        rest[-1][...] = (g_ref[0, 0] + a_ref[0]).astype(BF16)

    out, t0 = None, 0
    for k, (pc, A) in enumerate(zip(pieces, As)):
        grid_spec = pltpu.PrefetchScalarGridSpec(
            num_scalar_prefetch=1, grid=(pc.shape[0], nI, nJ),
            in_specs=[pl.BlockSpec((1, 1, tb, cb), lambda t, i, j, c_ref: (t, c_ref[0], i, j)),
                      pl.BlockSpec((1, tb, cb), lambda t, i, j, c_ref: (t, i, j))] + ([ANY] if k else []),
            out_specs=pl.BlockSpec((tb, cb), lambda t, i, j, c_ref, t0=t0: ((t0 + t) * nI + i, j)))
        out = pl.pallas_call(
            functools.partial(body), name=f"{name}{k}", grid_spec=grid_spec, out_shape=jax.ShapeDtypeStruct((S4 * R2, Cc), BF16),
            input_output_aliases={3: 0} if k else {},
            compiler_params=pltpu.CompilerParams(dimension_semantics=("parallel", "parallel", "parallel"),
                                                 vmem_limit_bytes=VMEM_LIMIT),
        )(*((cidx, pc, A) + ((out,) if k else ())))
        t0 += pc.shape[0]
    return out.reshape(S4, R2, Cc)


def _add_own(S1, B, chip_idx, cidx, name):
    S4, R2, Cc = S1.shape
    tb, cb = _wide(R2, Cc, 3, 2 * HALO)

    def body(s_idx, c_idx, s_ref, b_ref, o_ref):
        o_ref[...] = ((s_ref[0].astype(F32) + b_ref[0].astype(F32)) + b_ref[1].astype(F32)) + b_ref[2].astype(F32)

    grid_spec = pltpu.PrefetchScalarGridSpec(
        num_scalar_prefetch=2, grid=(R2 // tb, Cc // cb),
        in_specs=[pl.BlockSpec((1, tb, cb), lambda i, j, s_idx, c_idx: (s_idx[0], i, j)),
                  pl.BlockSpec((3, tb, cb), lambda i, j, s_idx, c_idx: (0, i, j))],
        out_specs=pl.BlockSpec((None, tb, cb), lambda i, j, s_idx, c_idx: (c_idx[0], i, j)))
    return pl.pallas_call(body, name=name, grid_spec=grid_spec, out_shape=jax.ShapeDtypeStruct((2, R2, Cc), F32),
                          compiler_params=pltpu.CompilerParams(dimension_semantics=("parallel", "parallel"),
                                                               vmem_limit_bytes=VMEM_LIMIT))(chip_idx, cidx, S1, B)


def _sibling_fill(Hs, name):
    def body(h_ref, out_ref, send, recv):
        x, y, c, s = _place()
        cp = pltpu.make_async_remote_copy(h_ref.at[c], out_ref.at[c], send, recv, device_id=(x, y, 1 - c), device_id_type=MESH)
        cp.start()
        cp.wait()

    return pl.pallas_call(
        body, name=name, in_specs=[ANY], out_specs=ANY, out_shape=jax.ShapeDtypeStruct(Hs.shape, Hs.dtype),
        input_output_aliases={0: 0}, scratch_shapes=[pltpu.SemaphoreType.DMA, pltpu.SemaphoreType.DMA],
    )(Hs)


def _gather_all(buf, name):
    R, Cc = buf.shape

    def body(b_ref, out_ref, send, recv, local):
        x, y, c, s = _place()
        d = 2 * s + c
        mine = pltpu.make_async_copy(b_ref, out_ref.at[d], local)
        mine.start()
        cps = []
        for m in range(1, 8):
            t = d ^ m
            cp = pltpu.make_async_remote_copy(b_ref, out_ref.at[d], send.at[m - 1], recv.at[m - 1],
                                              device_id=(t // 4, (t // 2) % 2, t % 2), device_id_type=MESH)
            cp.start()
            cps.append(cp)
        for cp in cps:
            cp.wait()
        mine.wait()

    return pl.pallas_call(
        body, name=name, in_specs=[ANY], out_specs=ANY, out_shape=jax.ShapeDtypeStruct((8, R, Cc), buf.dtype),
        scratch_shapes=[pltpu.SemaphoreType.DMA((7,)), pltpu.SemaphoreType.DMA((7,)), pltpu.SemaphoreType.DMA],
    )(buf)


def _finish_shard(S1, B, cidx, chip_idx, name):
    Hs = _sibling_fill(_add_own(S1, B, chip_idx, cidx, name + "_sum"), name + "_gather")
    return Hs.reshape(2 * Hs.shape[1], Hs.shape[2])


def _pack_rows(vs):
    flat = jnp.concatenate([v.reshape(-1) for v in vs])
    n = flat.shape[0]
    rows = -(-n // (LANES * 2 * HALO)) * 2 * HALO
    return jnp.pad(flat, (0, rows * LANES - n)).reshape(rows, LANES)


def _unpack_rows(buf, shapes):
    flat = buf.reshape(-1)
    outs, o = [], 0
    for shp in shapes:
        n = 1
        for d in shp:
            n *= d
        outs.append(flat[o:o + n].reshape(shp))
        o += n
    return outs


def kernel(x, p, norm_mix_g, w_in, conv_a_w, conv_qkv_w, a_log, dt_bias, dn_norm_g, w_out, norm_ffn_g, w_up, conv_ffn_w, w_down, norm_ple_g, w_ple_gate, w_ple_proj, final_norm_g, loss_target, m_norm_mix_g, m_w_in, m_conv_a_w, m_conv_qkv_w, m_a_log, m_dt_bias, m_dn_norm_g, m_w_out, m_norm_ffn_g, m_w_up, m_conv_ffn_w, m_w_down, m_norm_ple_g, m_w_ple_gate, m_w_ple_proj, m_final_norm_g, v_norm_mix_g, v_w_in, v_conv_a_w, v_conv_qkv_w, v_a_log, v_dt_bias, v_dn_norm_g, v_w_out, v_norm_ffn_g, v_w_up, v_conv_ffn_w, v_w_down, v_norm_ple_g, v_w_ple_gate, v_w_ple_proj, v_final_norm_g):
    xs = x[0]
    ps = p[0, 0]
    tgt = loss_target[0]
    T, D = xs.shape
    H = a_log.shape[-1]
    DNW = H * HEAD_DIM
    CW = conv_a_w.shape[-1] * 4
    F = w_down.shape[1] * 4
    PD = ps.shape[-1]
    IN_MAIN = 3 * CW + 4 * DNW
    IN_COLS = IN_MAIN + 2 * H
    assert w_in.shape[-1] * 4 == IN_COLS and CW + DNW == D and 2 * H <= LANES
    cb = _tile(min(CW, DNW), 512, LANES)
    while F % cb:
        cb -= LANES
    cidx = lax.axis_index("c").astype(jnp.int32).reshape(1)
    chip = 2 * lax.axis_index("x") + lax.axis_index("y")

    def halves(w):
        sh = w[0].astype(BF16)
        return sh.reshape(2, sh.shape[0] // 2, sh.shape[1])

    def whole(land):
        return land.reshape(4, 2 * land.shape[2], land.shape[3])

    def cols(g4):
        return jnp.transpose(g4, (1, 0, 2)).reshape(g4.shape[1], 4 * g4.shape[2])

    def rows(g4):
        return g4.reshape(4 * g4.shape[1], g4.shape[2])

    conv_shapes = [conv_a_w[0].shape, conv_qkv_w[0].shape, conv_ffn_w[0].shape]
    cpack = _pack_rows([conv_a_w[0], conv_qkv_w[0], conv_ffn_w[0]])
    sh_in, sh_out, sh_up, sh_down, sh_pg, sh_pp = (halves(w) for w in (w_in, w_out, w_up, w_down, w_ple_gate, w_ple_proj))
    l_in, cg = _run_comm(_merge(_ag_comm(sh_in), _ag_comm(cpack.reshape(2, cpack.shape[0] // 2, LANES))), "ag_w_in_conv")
    w_in_4 = whole(l_in)
    w_in_f = jnp.concatenate([w_in_4[t] for t in range(4)], axis=1)
    w_in_main = w_in_f[:, :IN_MAIN]
    w_in_small = jnp.pad(w_in_4[3][:, IN_MAIN - 3 * (IN_COLS // 4):], ((0, 0), (0, LANES - 2 * H)))
    cg = cg.reshape(4, cpack.shape[0], LANES)
    parts = [_unpack_rows(cg[t], conv_shapes) for t in range(4)]
    cw_a = jnp.concatenate([parts[t][0] for t in range(4)], axis=1)
    cw_qkv = jnp.concatenate([parts[t][1] for t in range(4)], axis=1)
    cw_ffn = jnp.concatenate([parts[t][2] for t in range(4)], axis=1)
    cw_q, cw_k, cw_v = cw_qkv[:, :DNW], cw_qkv[:, DNW:2 * DNW], cw_qkv[:, 2 * DNW:]
    cw_fg, cw_fv = cw_ffn[:, :F], cw_ffn[:, F:]
    pad_row = lambda v: jnp.pad(v, ((0, 0), (0, LANES - v.shape[1])))
    a_log_row, dt_row = pad_row(a_log), pad_row(dt_bias)
    gdn_t = jnp.tile(dn_norm_g, (1, H))
    gfin = final_norm_g.reshape(1, D)

    h1 = _rms_fwd(xs, norm_mix_g, "rms1")
    proj, (l_up,) = _mm(h1, w_in_main, mode="nn", out_dtypes=[F32], name="mm_proj", comm=_ag_comm(sh_up, q=0, nq=2))
    small = _mm(h1, w_in_small, mode="nn", out_dtypes=[F32], name="mm_small")
    ya = _ga_fwd(proj, cw_a, CW, cb)
    nq = 3 * CW // cb
    nd = DNW // cb
    qn = _qkv_fwd(proj, cw_q, nq, True, DNW, cb, "q_fwd")
    kn = _qkv_fwd(proj, cw_k, nq + nd, True, DNW, cb, "k_fwd")
    vs = _qkv_fwd(proj, cw_v, nq + 2 * nd, False, DNW, cb, "v_fwd")
    g, beta = _gb_fwd(small, a_log_row, dt_row, H)
    o, S0, inv_c, (l_up, l_out) = _delta_fwd(qn, kn, vs, g, beta, comm=_merge(_ag_comm(sh_up, l_up, q=1, nq=2), _ag_comm(sh_out)))
    w_out_f = rows(whole(l_out))
    w_out_a, w_out_b = w_out_f[:CW], w_out_f[CW:]
    w_up_4 = whole(l_up)
    z_coff = (3 * CW + 3 * DNW) // DNW
    assert (3 * CW + 3 * DNW) % DNW == 0 and CW % DNW == 0
    yb = _gnorm_fwd(o, proj, z_coff, gdn_t, DNW)
    add = lambda acc, r: (r + acc,)
    x1 = _mm(ya, w_out_a, mode="nn", out_dtypes=[F32], epi=add, extras=[xs], name="mm_out_a")
    x1 = _mm(yb, w_out_b, mode="nn", out_dtypes=[F32], epi=add, extras=[x1], name="mm_out_b")
    h2 = _rms_fwd(x1, norm_ffn_g, "rms2")
    up_g, (l_down,) = _mm(h2, w_up_4, mode="nn", b_split=(0, 2), out_dtypes=[F32], name="mm_up_g",
                          comm=_ag_comm(sh_down, q=0, nq=2))
    up_v, (l_down,) = _mm(h2, w_up_4, mode="nn", b_split=(2, 2), out_dtypes=[F32], name="mm_up_v",
                          comm=_ag_comm(sh_down, l_down, q=1, nq=2))
    w_down_f = rows(whole(l_down))
    act = _ffn_fwd(up_g, up_v, cw_fg, cw_fv, cb)
    x2, (l_pg, l_pp) = _mm(act, w_down_f, mode="nn", out_dtypes=[F32], epi=add, extras=[x1], name="mm_down",
                           comm=_merge(_ag_comm(sh_pg), _ag_comm(sh_pp)))
    w_pg_f = rows(whole(l_pg))
    w_pp_4 = whole(l_pp)
    h3 = _rms_fwd(x2, norm_ple_g, "rms3")
    pp = _mm(ps, w_pp_4, mode="nn", b_split=(0, 4), out_dtypes=[F32], name="mm_pp")

    def ple_epi(acc, x2v, ppv):
        pg = _sigmoid(acc)
        return x2v + pg * ppv, pg

    x3, pg = _mm(h3, w_pg_f, mode="nn", out_dtypes=[F32, F32], epi=ple_epi, extras=[x2, pp], name="mm_pg")

    dx3, dpg, dpp, fin = _final_fb(x3, tgt, gfin, pp, pg)
    loss = lax.psum(jnp.sum(fin[1]), ("x", "y", "c"))
    d_gfin = fin[0:1]
    def split_cols(dW):
        R, C4 = dW.shape
        return jnp.transpose(dW.reshape(R, 4, C4 // 4), (1, 0, 2))

    def split_rows(dW):
        return dW.reshape(4, dW.shape[0] // 4, dW.shape[1])

    dW_pp = _mm(ps, dpp, mode="tn", out_split=4, out_dtypes=[F32], name="mm_dw_pp")
    dW_pg = _mm(h3, dpg, mode="tn", out_dtypes=[F32], name="mm_dw_pg")
    P_pp, P_pg = _halves(dW_pp), _halves(split_rows(dW_pg))
    dh3, (A_pp, A_pg) = _mm(dpg, w_pg_f, mode="nt", out_dtypes=[F32], name="mm_dh3",
                            comm=_merge(_swap_comm(P_pp), _swap_comm(P_pg)))
    S_pp = _add_half([P_pp], [A_pp], cidx, "rs_w_pp_add")
    S_pg = _add_half([P_pg], [A_pg], cidx, "rs_w_pg_add")
    dx2, dx2_b, d_gple = _rms_bwd(dh3, x2, norm_ple_g, dx3, "rms3_bwd")
    dW_down, (B_pp, B_pg) = _mm(act, dx2_b, mode="tn", out_dtypes=[F32], name="mm_dw_down",
                                comm=_merge(_a2a_comm(S_pp), _a2a_comm(S_pg)))
    P_down = _halves(split_rows(dW_down))
    dact, (A_down,) = _mm(dx2_b, w_down_f, mode="nt", out_dtypes=[F32], name="mm_dact", comm=_swap_comm(P_down))
    S_down = _add_half([P_down], [A_down], cidx, "rs_w_down_add")
    dup_g, dup_v, dcw_fg, dcw_fv = _ffn_bwd(dact, up_g, up_v, cw_fg, cw_fv, cb)
    dW_up_g, (B_down,) = _mm(h2, dup_g, mode="tn", out_split=2, out_dtypes=[F32], name="mm_dw_up_g", comm=_a2a_comm(S_down))
    P_ug = _halves(dW_up_g)
    dW_up_v, (A_ug,) = _mm(h2, dup_v, mode="tn", out_split=2, out_dtypes=[F32], name="mm_dw_up_v", comm=_swap_comm(P_ug))
    P_uv = _halves(dW_up_v)
    dh2, (A_uv,) = _mm(dup_g, w_up_4, mode="nt", b_split=(0, 2), out_dtypes=[F32], name="mm_dh2_g", comm=_swap_comm(P_uv))
    S_up = _add_half([P_ug, P_uv], [A_ug, A_uv], cidx, "rs_w_up_add")
    dh2 = _mm(dup_v, w_up_4, mode="nt", b_split=(2, 2), out_dtypes=[F32], epi=add, extras=[dh2], name="mm_dh2_v")
    dx1, dx1_b, d_gffn = _rms_bwd(dh2, x1, norm_ffn_g, dx2, "rms2_bwd")
    dW_out_a = _mm(ya, dx1_b, mode="tn", out_dtypes=[F32], name="mm_dw_out_a")
    dW_out_b = _mm(yb, dx1_b, mode="tn", out_dtypes=[F32], name="mm_dw_out_b")
    P_oa, P_ob = _halves(dW_out_a.reshape(-1, D // 4, D)), _halves(dW_out_b.reshape(-1, D // 4, D))
    dymix, (A_oa, A_ob) = _mm(dx1_b, w_out_f, mode="nt", out_dtypes=[F32], name="mm_dymix",
                              comm=_merge(_swap_comm(P_oa), _swap_comm(P_ob)))
    S_out = _add_half([P_oa, P_ob], [A_oa, A_ob], cidx, "rs_w_out_add")
    dax, dab, dac, dcw_a = _ga_bwd(dymix, proj, cw_a, CW, cb)
    do, dz, d_gdn = _gnorm_bwd(dymix, CW // DNW, o, proj, z_coff, gdn_t, DNW)
    dqn, dkn, dvs, dgB, dbB, (B_up, B_out) = _delta_bwd(qn, kn, vs, g, beta, S0, inv_c, do,
                                                        comm=_merge(_a2a_comm(S_up), _a2a_comm(S_out)))
    dq_pre, dcw_q = _qkv_bwd(dqn, proj, cw_q, nq, True, DNW, cb, "q_bwd")
    dk_pre, dcw_k = _qkv_bwd(dkn, proj, cw_k, nq + nd, True, DNW, cb, "k_bwd")
    dv_pre, dcw_v = _qkv_bwd(dvs, proj, cw_v, nq + 2 * nd, False, DNW, cb, "v_bwd")
    dsmall, d_ab = _gb_bwd(dgB, dbB, small, g, beta, a_log_row, dt_row, H)
    dproj = jnp.concatenate([dax, dab, dac, dq_pre, dk_pre, dv_pre, dz], axis=1)
    dW_in_main = _mm(h1, dproj, mode="tn", out_dtypes=[F32], name="mm_dw_in")
    dW_in_small = _mm(h1, dsmall, mode="tn", out_dtypes=[F32], name="mm_dw_in_small")
    cs = IN_COLS // 4
    P_in = _halves(jnp.stack([dW_in_main[:, t * cs:(t + 1) * cs] for t in range(3)]
                             + [jnp.concatenate([dW_in_main[:, 3 * cs:], dW_in_small[:, :2 * H]], axis=1)]))
    (A_in,) = _run_comm(_swap_comm(P_in), "rs_w_in_swap")
    S_in = _add_half([P_in], [A_in], cidx, "rs_w_in_add")
    dh1, (B_in,) = _mm(dproj, w_in_main, mode="nt", out_dtypes=[F32], name="mm_dh1", comm=_a2a_comm(S_in))
    dh1 = _mm(dsmall, w_in_small, mode="nt", out_dtypes=[F32], epi=add, extras=[dh1], name="mm_dh1_small")
    dx, _, d_gmix = _rms_bwd(dh1, xs, norm_mix_g, dx1, "rms1_bwd")

    chip_idx = chip.astype(jnp.int32).reshape(1)

    def update(S1, B, w, m, v, name):
        gr = _finish_shard(S1, B, cidx, chip_idx, "rs_" + name)
        delta, m2, v2 = _adamw(w[0], gr, m[0], v[0], "adamw_" + name)
        return gr[None], delta[None], m2[None], v2[None]

    big = {
        "w_in": update(S_in, B_in, w_in, m_w_in, v_w_in, "w_in"),
        "w_out": update(S_out, B_out, w_out, m_w_out, v_w_out, "w_out"),
        "w_up": update(S_up, B_up, w_up, m_w_up, v_w_up, "w_up"),
        "w_down": update(S_down, B_down, w_down, m_w_down, v_w_down, "w_down"),
        "w_ple_gate": update(S_pg, B_pg, w_ple_gate, m_w_ple_gate, v_w_ple_gate, "w_pg"),
        "w_ple_proj": update(S_pp, B_pp, w_ple_proj, m_w_ple_proj, v_w_ple_proj, "w_pp"),
    }

    small_grads = [d_gmix[0:1], dcw_a[:cw_a.shape[0]], jnp.concatenate([dcw_q, dcw_k, dcw_v], axis=1)[:cw_qkv.shape[0]],
                   d_ab[0:1, :H], d_ab[1:2, :H], d_gdn[0:1], d_gffn[0:1],
                   jnp.concatenate([dcw_fg, dcw_fv], axis=1)[:cw_ffn.shape[0]], d_gple[0:1], d_gfin]
    small_shapes = [v.shape for v in small_grads]
    gpack = _pack_rows(small_grads)
    gsum = _sum_stack(_gather_all(gpack, "ag_small"), "sum_small")
    (g_gmix, g_cwa, g_cwqkv, g_alog, g_dt, g_gdn, g_gffn, g_cwffn, g_gple, g_gfin) = _unpack_rows(gsum, small_shapes)

    def my_cols(v):
        Cc = v.shape[1] // 4
        return lax.dynamic_slice_in_dim(v, chip * Cc, Cc, axis=1)

    g_small = [g_gmix, my_cols(g_cwa), my_cols(g_cwqkv), g_alog, g_dt, g_gdn, g_gffn, my_cols(g_cwffn), g_gple, g_gfin]
    w_small = [norm_mix_g, conv_a_w[0], conv_qkv_w[0], a_log, dt_bias, dn_norm_g, norm_ffn_g, conv_ffn_w[0], norm_ple_g, gfin]
    m_small = [m_norm_mix_g, m_conv_a_w[0], m_conv_qkv_w[0], m_a_log, m_dt_bias, m_dn_norm_g, m_norm_ffn_g, m_conv_ffn_w[0],
               m_norm_ple_g, m_final_norm_g.reshape(1, D)]
    v_small = [v_norm_mix_g, v_conv_a_w[0], v_conv_qkv_w[0], v_a_log, v_dt_bias, v_dn_norm_g, v_norm_ffn_g, v_conv_ffn_w[0],
               v_norm_ple_g, v_final_norm_g.reshape(1, D)]
    shp = [v.shape for v in w_small]
    ds_, ms_, vs_ = _adamw(_pack_rows(w_small), _pack_rows(g_small), _pack_rows(m_small), _pack_rows(v_small), "adamw_small")
    out_shapes = [norm_mix_g.shape, conv_a_w.shape, conv_qkv_w.shape, a_log.shape, dt_bias.shape, dn_norm_g.shape,
                  norm_ffn_g.shape, conv_ffn_w.shape, norm_ple_g.shape, final_norm_g.shape]
    rs = lambda vals: [v.reshape(s) for v, s in zip(vals, out_shapes)]
    sg, sd_, sm_, sv_ = rs(g_small), rs(_unpack_rows(ds_, shp)), rs(_unpack_rows(ms_, shp)), rs(_unpack_rows(vs_, shp))
    names_small = ["norm_mix_g", "conv_a_w", "conv_qkv_w", "a_log", "dt_bias", "dn_norm_g", "norm_ffn_g", "conv_ffn_w",
                   "norm_ple_g", "final_norm_g"]
    res = {n: (sg[i], sd_[i], sm_[i], sv_[i]) for i, n in enumerate(names_small)}
    res.update(big)
    order = ["norm_mix_g", "w_in", "conv_a_w", "conv_qkv_w", "a_log", "dt_bias", "dn_norm_g", "w_out", "norm_ffn_g", "w_up",
             "conv_ffn_w", "w_down", "norm_ple_g", "w_ple_gate", "w_ple_proj", "final_norm_g"]
    return (loss, dx[None], *[res[n][0] for n in order], *[res[n][1] for n in order], *[res[n][2] for n in order],
            *[res[n][3] for n in order])
```

```python
import functools

import jax
import jax.numpy as jnp
from jax import lax
from jax.experimental import pallas as pl
from jax.experimental.pallas import tpu as pltpu

F32 = jnp.float32
BF16 = jnp.bfloat16
LANES = 128
HALO = 8
HEAD_DIM = 128
CHUNK = 64
EPS = 1e-6
VMEM_LIMIT = 56 * 1024 * 1024
MM_VMEM_BUDGET = 40 * 1024 * 1024
MM_STEP_BYTES = 1 << 20
EW_VMEM_BUDGET = 28 * 1024 * 1024
MESH = pl.DeviceIdType.MESH

ADAM_LR, ADAM_B1, ADAM_B2, ADAM_EPS, ADAM_WD, ADAM_STEP = 0.001, 0.9, 0.999, 1e-08, 0.01, 10


def _tile(n, cap, unit):
    if n <= cap:
        return n
    d = (cap // unit) * unit
    while d >= unit:
        if n % d == 0:
            return d
        d -= unit
    raise ValueError(f"no tile for {n} (cap {cap}, unit {unit})")


def _sigmoid(x):
    return 1.0 / (1.0 + jnp.exp(-x))


def _divisors(n, cap):
    ds = [d for d in range(cap // LANES * LANES, 0, -LANES) if n % d == 0]
    return [n] if (n <= cap or not ds) else ds


def _mm_tiles(M, N, K, n_unit, k_unit, a_bytes, n_blocks_mn, a_transposed):
    best = None
    for tm in _divisors(M, 1536):
        for tn in _divisors(n_unit, 1536):
            for tk in _divisors(k_unit, 4096):
                nk = K // tk
                vmem = 2 * tm * tk * a_bytes + 2 * tk * tn * 2 + 2 * 4 * tm * tn * n_blocks_mn + (4 * tm * tn if nk > 1 else 0)
                if vmem > MM_VMEM_BUDGET:
                    continue
                steps = (M // tm) * (N // tn) * nk
                cost = (M * K * a_bytes * (N // tn if nk > 1 else 1) + K * N * 2 * (M // tm) + 4 * M * N * n_blocks_mn
                        + (8 * M * N * nk // 3 if nk > 1 else 0) + steps * MM_STEP_BYTES
                        + (2 * steps * tm * tk if a_transposed else 0))
                if best is None or cost < best[0]:
                    best = (cost, tm, tn, tk)
    return best[1:]


def _mm(a, b, *, mode, out_dtypes, name, epi=None, extras=(), comm=None, b_split=None, out_split=None):
    if b_split is not None:
        lo, ns = b_split
        Rb, Cb = b.shape[1], b.shape[2]
    if mode == "nn":
        (M, K), N = a.shape, (ns * Cb if b_split else b.shape[1])
    elif mode == "nt":
        (M, K), N = a.shape, (Rb if b_split else b.shape[0])
    else:
        (K, M), N = a.shape, b.shape[1]
    n_ex, n_out = len(extras), len(out_dtypes)
    n_unit = Cb if (b_split and mode == "nn") else (N // out_split if out_split else N)
    k_unit = Cb if (b_split and mode == "nt") else K
    tm, tn, tk = _mm_tiles(M, N, K, n_unit, k_unit, a.dtype.itemsize, n_ex + n_out, mode == "tn")
    nk = K // tk
    a_spec = pl.BlockSpec((tk, tm), lambda i, j, k: (k, i)) if mode == "tn" else pl.BlockSpec((tm, tk), lambda i, j, k: (i, k))
    if b_split and mode == "nn":
        nb = Cb // tn
        b_spec = pl.BlockSpec((None, tk, tn), lambda i, j, k: (lo + j // nb, k, j % nb))
    elif b_split:
        nb = Cb // tk
        b_spec = pl.BlockSpec((None, tn, tk), lambda i, j, k: (lo + k // nb, j, k % nb))
    else:
        b_spec = pl.BlockSpec((tn, tk), lambda i, j, k: (j, k)) if mode == "nt" else pl.BlockSpec((tk, tn), lambda i, j, k: (k, j))
    mn_spec = pl.BlockSpec((tm, tn), lambda i, j, k: (i, j))
    out_shapes = [jax.ShapeDtypeStruct((M, N), dt) for dt in out_dtypes]
    out_specs = [mn_spec] * n_out
    if out_split:
        assert n_ex == 0 and n_out == 1
        nbo = (N // out_split) // tn
        out_specs = [pl.BlockSpec((None, tm, tn), lambda i, j, k: (j // nbo, i, j % nbo))]
        out_shapes = [jax.ShapeDtypeStruct((out_split, M, N // out_split), out_dtypes[0])]
    dims = {"nn": (((1,), (0,)), ((), ())), "nt": (((1,), (1,)), ((), ())), "tn": (((0,), (0,)), ((), ()))}[mode]

    def body(*refs):
        a_ref, b_ref = refs[0], refs[1]
        ex_refs = refs[2:2 + n_ex]
        out_refs = refs[2 + n_ex:2 + n_ex + n_out]
        part = lax.dot_general(a_ref[...].astype(BF16), b_ref[...].astype(BF16), dims, preferred_element_type=F32)

        def finish(acc):
            outs = (acc,) if epi is None else epi(acc, *[r[...] for r in ex_refs])
            for r, o in zip(out_refs, outs):
                r[...] = o.astype(r.dtype)

        if nk == 1:
            finish(part)
            return
        acc_ref = refs[-1]
        k = pl.program_id(2)

        @pl.when(k == 0)
        def _():
            acc_ref[...] = part

        @pl.when(jnp.logical_and(k > 0, k < nk - 1))
        def _():
            acc_ref[...] += part

        @pl.when(k == nk - 1)
        def _():
            finish(acc_ref[...] + part)

    outs, comm_outs = _call(
        body, name=name, grid=(M // tm, N // tn, nk),
        in_specs=[a_spec, b_spec] + [mn_spec] * n_ex,
        out_specs=out_specs,
        out_shape=out_shapes,
        scratch_shapes=[pltpu.VMEM((tm, tn), F32)] if nk > 1 else [],
        semantics=("parallel", "parallel", "arbitrary"), args=(a, b, *extras), comm=comm)
    res = outs[0] if n_out == 1 else outs
    return res if comm is None else (res, comm_outs)


def _tiled(fn, *, T, C, ins, out_dtypes=(), acc_rows=(), tb=None, cb=512, name):
    tb = _tile(T, tb or (512 if cb <= 1024 else 256), HALO)
    nI, nJ = T // tb, C // cb
    hb, nH = tb // HALO, T // HALO
    specs, args, kinds = [], [], []
    for kind, arr, cmap in ins:
        cm = cmap if cmap is not None else (lambda j: j)
        kinds.append(kind)
        if kind == "cur":
            specs.append(pl.BlockSpec((tb, cb), lambda j, i, cm=cm: (i, cm(j))))
            args.append(arr)
        elif kind == "ext":
            specs.append(pl.BlockSpec((HALO, cb), lambda j, i, cm=cm: (jnp.maximum(i * hb - 1, 0), cm(j))))
            specs.append(pl.BlockSpec((tb, cb), lambda j, i, cm=cm: (i, cm(j))))
            specs.append(pl.BlockSpec((HALO, cb), lambda j, i, cm=cm: (jnp.minimum((i + 1) * hb, nH - 1), cm(j))))
            args += [arr, arr, arr]
        elif kind == "row":
            specs.append(pl.BlockSpec((arr.shape[0], cb), lambda j, i, cm=cm: (0, cm(j))))
            args.append(arr)
        elif kind == "stack":
            specs.append(pl.BlockSpec((arr.shape[0], tb, cb), lambda j, i, cm=cm: (0, i, cm(j))))
            args.append(arr)
        else:
            raise ValueError(kind)
    n_in = len(args)
    n_out, n_acc = len(out_dtypes), len(acc_rows)

    def body(*refs):
        j, i = pl.program_id(0), pl.program_id(1)
        vals, r = [], 0
        for kind in kinds:
            if kind == "ext":
                prev = jnp.where(i == 0, 0.0, refs[r][...].astype(F32))
                cur = refs[r + 1][...].astype(F32)
                nxt = jnp.where(i == nI - 1, 0.0, refs[r + 2][...].astype(F32))
                vals.append(jnp.concatenate([prev, cur, nxt], axis=0))
                r += 3
            else:
                vals.append(refs[r][...])
                r += 1
        res = fn(j, i, *vals)
        for ref, o in zip(refs[n_in:n_in + n_out], res[:n_out]):
            ref[...] = o.astype(ref.dtype)
        for ref, o in zip(refs[n_in + n_out:], res[n_out:]):
            @pl.when(i == 0)
            def _(ref=ref, o=o):
                ref[...] = o

            @pl.when(i > 0)
            def _(ref=ref, o=o):
                ref[...] += o

    outs = pl.pallas_call(
        body, name=name, grid=(nJ, nI), in_specs=specs,
        out_specs=[pl.BlockSpec((tb, cb), lambda j, i: (i, j))] * n_out
        + [pl.BlockSpec((rows, cb), lambda j, i: (0, j)) for rows in acc_rows],
        out_shape=[jax.ShapeDtypeStruct((T, C), dt) for dt in out_dtypes]
        + [jax.ShapeDtypeStruct((rows, C), F32) for rows in acc_rows],
        compiler_params=pltpu.CompilerParams(dimension_semantics=("parallel", "arbitrary"),
                                             vmem_limit_bytes=VMEM_LIMIT),
    )(*args)
    return outs


def _conv_causal(xe, w):
    K = w.shape[0]
    y = xe * w[K - 1:K]
    for j in range(K - 1):
        y = y + pltpu.roll(xe, K - 1 - j, 0) * w[j:j + 1]
    return y


def _conv_anti(de, w):
    K, n = w.shape[0], de.shape[0]
    y = de * w[K - 1:K]
    for j in range(K - 1):
        y = y + pltpu.roll(de, n - (K - 1 - j), 0) * w[j:j + 1]
    return y


def _conv_dw(dce, xe, K):
    n = dce.shape[0]
    tb = n - 2 * HALO
    rows = []
    for j in range(K):
        xs = xe if j == K - 1 else pltpu.roll(xe, K - 1 - j, 0)
        rows.append(jnp.sum((dce * xs)[HALO:HALO + tb], axis=0, keepdims=True))
    rows.append(jnp.zeros((HALO - K, dce.shape[1]), F32))
    return jnp.concatenate(rows, axis=0)


def _own(xe):
    return xe[HALO:xe.shape[0] - HALO]


def _row0(v):
    return jnp.concatenate([v, jnp.zeros((HALO - 1, v.shape[1]), F32)], axis=0)


def _per_head(fn, *xs):
    n = xs[0].shape[1] // HEAD_DIM
    outs = [fn(*[x[:, g * HEAD_DIM:(g + 1) * HEAD_DIM] for x in xs]) for g in range(n)]
    return outs[0] if n == 1 else jnp.concatenate(outs, axis=1)


def _rms_fwd(x, g, name):
    T, D = x.shape

    def fn(j, i, xv, gv):
        r = lax.rsqrt(jnp.mean(xv * xv, axis=1, keepdims=True) + EPS)
        return (xv * r * gv,)

    return _tiled(fn, T=T, C=D, ins=[("cur", x, None), ("row", g, None)], out_dtypes=[BF16], cb=D, name=name)[0]


def _rms_bwd_math(dy, xv, gv):
    r = lax.rsqrt(jnp.mean(xv * xv, axis=1, keepdims=True) + EPS)
    xh = xv * r
    dxh = dy * gv
    dx = r * (dxh - xh * jnp.mean(dxh * xh, axis=1, keepdims=True))
    dg = jnp.sum(dy * xh, axis=0, keepdims=True)
    return dx, dg


def _rms_bwd(dh, x, g, dres, name):
    T, D = x.shape

    def fn(j, i, dhv, xv, gv, dr):
        dx, dg = _rms_bwd_math(dhv, xv, gv)
        return dr + dx, dr + dx, _row0(dg)

    return _tiled(fn, T=T, C=D, ins=[("cur", dh, None), ("cur", x, None), ("row", g, None), ("cur", dres, None)],
                  out_dtypes=[F32, BF16], acc_rows=[HALO], cb=D, name=name)


def _final_fb(x3, tgt, g, pp, pg):
    T, D = x3.shape

    def fn(j, i, xv, tv, gv, ppv, pgv):
        r = lax.rsqrt(jnp.mean(xv * xv, axis=1, keepdims=True) + EPS)
        xh = xv * r
        e = xh * gv - tv
        dy = e * (1.0 / D)
        dxh = dy * gv
        dx = r * (dxh - xh * jnp.mean(dxh * xh, axis=1, keepdims=True))
        dg = jnp.sum(dy * xh, axis=0, keepdims=True)
        ls = jnp.sum(e * e, axis=0, keepdims=True) * (0.5 / D)
        return (dx, dx * ppv * pgv * (1.0 - pgv), dx * pgv,
                jnp.concatenate([dg, ls, jnp.zeros((HALO - 2, D), F32)], axis=0))

    return _tiled(fn, T=T, C=D, ins=[("cur", x3, None), ("cur", tgt, None), ("row", g, None), ("cur", pp, None),
                                      ("cur", pg, None)], out_dtypes=[F32, BF16, BF16], acc_rows=[HALO], cb=D, name="final_fb")


def _ga_fwd(proj, w_a, CW, cb):
    T = proj.shape[0]
    n = CW // cb

    def fn(j, i, ax, ab, ac, w):
        c = _conv_causal(ac * ax, w)
        return (ab * _own(c),)

    return _tiled(fn, T=T, C=CW, ins=[("ext", proj, None), ("cur", proj, lambda j: j + n), ("ext", proj, lambda j: j + 2 * n),
                                       ("row", w_a, None)], out_dtypes=[BF16], cb=cb, name="ga_fwd")[0]


def _ga_bwd(dymix, proj, w_a, CW, cb):
    T = proj.shape[0]
    n = CW // cb
    K = w_a.shape[0]

    def fn(j, i, dy, ax, ab, ac, w):
        u = ac * ax
        c = _conv_causal(u, w)
        dc = dy * ab
        du = _conv_anti(dc, w)
        return _own(du * ac), _own(dy * c), _own(du * ax), _conv_dw(dc, u, K)

    return _tiled(fn, T=T, C=CW, ins=[("ext", dymix, None), ("ext", proj, None), ("ext", proj, lambda j: j + n),
                                       ("ext", proj, lambda j: j + 2 * n), ("row", w_a, None)],
                  out_dtypes=[BF16, BF16, BF16], acc_rows=[HALO], cb=cb, name="ga_bwd")


def _l2n(s):
    return s * lax.rsqrt(jnp.sum(s * s, axis=1, keepdims=True) + EPS)


def _qkv_fwd(proj, w_sec, coff, normalize, DNW, cb, name):
    T = proj.shape[0]

    def fn(j, i, pre, w):
        c = _own(_conv_causal(pre, w))
        s = c * _sigmoid(c)
        return (_per_head(_l2n, s) if normalize else s,)

    return _tiled(fn, T=T, C=DNW, ins=[("ext", proj, lambda j: j + coff), ("row", w_sec, None)],
                  out_dtypes=[F32], cb=cb, name=name)[0]


def _qkv_bwd(dsec, proj, w_sec, coff, normalize, DNW, cb, name):
    T = proj.shape[0]
    K = w_sec.shape[0]

    def l2n_bwd(s, dn):
        r = lax.rsqrt(jnp.sum(s * s, axis=1, keepdims=True) + EPS)
        nrm = s * r
        return r * (dn - nrm * jnp.sum(dn * nrm, axis=1, keepdims=True))

    def fn(j, i, dn, pre, w):
        c = _conv_causal(pre, w)
        sg = _sigmoid(c)
        s = c * sg
        ds = _per_head(l2n_bwd, s, dn) if normalize else dn
        dc = ds * (sg * (1.0 + c * (1.0 - sg)))
        return _own(_conv_anti(dc, w)), _conv_dw(dc, pre, K)

    return _tiled(fn, T=T, C=DNW, ins=[("ext", dsec, None), ("ext", proj, lambda j: j + coff), ("row", w_sec, None)],
                  out_dtypes=[BF16], acc_rows=[HALO], cb=cb, name=name)


def _gb_fwd(small, a_log_row, dt_row, H):
    T = small.shape[0]

    def fn(j, i, sm, al, dt):
        z = sm + dt
        sp = jnp.maximum(z, 0.0) + jnp.log(1.0 + jnp.exp(-jnp.abs(z)))
        g = -jnp.exp(al) * sp
        beta = _sigmoid(pltpu.roll(sm, LANES - H, 1))
        return g, beta

    return _tiled(fn, T=T, C=LANES, ins=[("cur", small, None), ("row", a_log_row, None), ("row", dt_row, None)],
                  out_dtypes=[F32, F32], cb=LANES, name="gb_fwd")


def _gb_bwd(dgB, dbB, small, g, beta, a_log_row, dt_row, H):
    T = small.shape[0]

    def fn(j, i, dgv, dbv, sm, gv, bv, al, dt):
        lane = lax.broadcasted_iota(jnp.int32, sm.shape, 1)
        dg = jnp.zeros(sm.shape, F32)
        db = jnp.zeros(sm.shape, F32)
        for h in range(H):
            dg = jnp.where(lane == h, jnp.sum(dgv[h], axis=1, keepdims=True), dg)
            db = jnp.where(lane == h, jnp.sum(dbv[h], axis=1, keepdims=True), db)
        da = dg * (-jnp.exp(al)) * _sigmoid(sm + dt)
        dbb = db * bv * (1.0 - bv)
        dsm = jnp.where(lane < H, da, 0.0) + pltpu.roll(jnp.where(lane < H, dbb, 0.0), H, 1)
        d_alog = jnp.sum(jnp.where(lane < H, dg * gv, 0.0), axis=0, keepdims=True)
        d_dt = jnp.sum(jnp.where(lane < H, da, 0.0), axis=0, keepdims=True)
        return dsm, jnp.concatenate([d_alog, d_dt, jnp.zeros((HALO - 2, LANES), F32)], axis=0)

    return _tiled(fn, T=T, C=LANES, ins=[("stack", dgB, None), ("stack", dbB, None), ("cur", small, None), ("cur", g, None),
                                          ("cur", beta, None), ("row", a_log_row, None), ("row", dt_row, None)],
                  out_dtypes=[BF16], acc_rows=[HALO], cb=LANES, name="gb_bwd")


_DIMS = {"nn": (((1,), (0,)), ((), ())), "nt": (((1,), (1,)), ((), ())), "tn": (((0,), (0,)), ((), ()))}
_DOT_BWD = {"nn": (("nt", "gb"), ("tn", "ag")), "nt": (("nn", "gb"), ("tn", "ga")), "tn": (("nt", "bg"), ("nn", "ag"))}


def _split(a):
    hi = a.astype(BF16)
    return hi, (a - hi.astype(F32)).astype(BF16)


def _raw_dot(a, b, kind, passes):
    dg = lambda x, y: lax.dot_general(x, y, _DIMS[kind], preferred_element_type=F32)
    if passes == 1:
        return dg(a.astype(BF16), b.astype(BF16))
    ah, al = _split(a)
    bh, bl = _split(b)
    return dg(ah, bh) + (dg(ah, bl) + dg(al, bh))


def _raw_dot_exact(a, b, kind, exact):
    dg = lambda x, y: lax.dot_general(x, y, _DIMS[kind], preferred_element_type=F32)
    if exact == "a":
        bh, bl = _split(b)
        return dg(a.astype(BF16), bh) + dg(a.astype(BF16), bl)
    ah, al = _split(a)
    return dg(ah, b.astype(BF16)) + dg(al, b.astype(BF16))


@functools.lru_cache(maxsize=None)
def _dotc(kind):
    @jax.custom_vjp
    def f(a, b):
        return _raw_dot_exact(a, b, kind, "a")

    def fwd(a, b):
        return _raw_dot_exact(a, b, kind, "a"), a

    def bwd(a, g):
        db = _raw_dot_exact(a, g, "tn", "a") if kind == "nn" else _raw_dot_exact(g, a, "tn", "b")
        return jnp.zeros_like(a), db

    f.defvjp(fwd, bwd)
    return f


@functools.lru_cache(maxsize=None)
def _dotf(kind, passes):
    @jax.custom_vjp
    def f(a, b):
        return _raw_dot(a, b, kind, passes)

    def fwd(a, b):
        return _raw_dot(a, b, kind, passes), (a, b)

    def bwd(res, g):
        ops = {"a": res[0], "b": res[1], "g": g}
        (ka, oa), (kb, ob) = _DOT_BWD[kind]
        return (_raw_dot(ops[oa[0]], ops[oa[1]], ka, passes), _raw_dot(ops[ob[0]], ops[ob[1]], kb, passes))

    f.defvjp(fwd, bwd)
    return f


@jax.custom_vjp
def _saved_inverse(L, inv):
    return inv


def _saved_inverse_fwd(L, inv):
    return inv, inv


def _saved_inverse_bwd(inv, g):
    d3nt, d3tn = _dotf("nt", 3), _dotf("tn", 3)
    return -d3nt(d3tn(inv, g), inv), jnp.zeros_like(inv)


_saved_inverse.defvjp(_saved_inverse_fwd, _saved_inverse_bwd)


def _chunk_fn(q, k, v, gB, bB, S, inv_saved=None):
    C = CHUNK
    d3 = _dotf("nn", 3)
    d1, d1nt, d1tn = _dotf("nn", 1), _dotf("nt", 1), _dotf("tn", 1)
    each = lambda f, *ls: tuple(f(*xs) for xs in zip(*ls))
    row = lax.broadcasted_iota(jnp.int32, (C, C), 0)
    col = lax.broadcasted_iota(jnp.int32, (C, C), 1)
    causal = row >= col
    strict = row > col
    tril = jnp.where(causal, 1.0, 0.0).astype(F32)
    eye = jnp.where(row == col, 1.0, 0.0).astype(F32)
    avg = jnp.full((C, HEAD_DIM), 1.0 / HEAD_DIM, F32)
    gc = each(lambda g: _dotc("nn")(tril, g), gB)
    R = each(lambda g: _dotc("nt")(avg, g), gc)
    decay = each(lambda g, r: jnp.where(causal, jnp.exp(jnp.where(causal, g[:, :C] - r, 0.0)), 0.0), gc, R)
    kk = each(lambda x: d1nt(x, x), k)
    L = each(lambda a, d, b: jnp.where(strict, a * d * b[:, :C], 0.0), kk, decay, bB)
    if inv_saved is None:
        inv = each(lambda l: eye - l, L)
        P = L
        for _ in range(5):
            P = each(lambda p: d3(p, p), P)
            inv = each(lambda a, p: d3(a, eye + p), inv, P)
    else:
        inv = each(_saved_inverse, L, inv_saved)
    eg = each(jnp.exp, gc)
    u = each(lambda a, x, b: d3(a, x * b), inv, v, bB)
    w = each(lambda a, x, b, e: d3(a, x * b * e), inv, k, bB, eg)
    qs = each(lambda x: x * (HEAD_DIM ** -0.5), q)
    qk = each(lambda a, x, d: d1nt(a, x) * d, qs, k, decay)
    gl = each(lambda g: g[C - 1:C, :], gc)
    v_new = each(lambda a, b, s: a - d1(b, s), u, w, S)
    o1 = each(lambda a, e, s: d1(a * e, s), qs, eg, S)
    o = each(lambda a, b, c: a + d1(b, c), o1, qk, v_new)
    kv = each(lambda x, a, g, vn: d1tn(x * jnp.exp(a - g), vn), k, gl, gc, v_new)
    S_new = each(lambda s, a, b: s * jnp.exp(a) + b, S, gl, kv)
    return (o, S_new), inv


def _sel_lane(x, h):
    lane = lax.broadcasted_iota(jnp.int32, x.shape, 1)
    return jnp.broadcast_to(jnp.sum(jnp.where(lane == h, x, 0.0), axis=1, keepdims=True), x.shape)


def _head(ref, h):
    return ref[:, h * HEAD_DIM:(h + 1) * HEAD_DIM]


def _delta_fwd(q, k, v, g, beta, comm=None):
    T = q.shape[0]
    H, N = q.shape[1] // HEAD_DIM, T // CHUNK

    def body(q_ref, k_ref, v_ref, g_ref, b_ref, o_ref, s_ref, inv_ref, S):
        @pl.when(pl.program_id(0) == 0)
        def _():
            S[...] = jnp.zeros_like(S)

        gv, bv = g_ref[...], b_ref[...]
        heads = lambda f: tuple(f(h) for h in range(H))
        S_in = heads(lambda h: S[h])
        for h in range(H):
            s_ref[h, 0] = S_in[h]
        (o, S_new), inv = _chunk_fn(heads(lambda h: _head(q_ref, h)), heads(lambda h: _head(k_ref, h)),
                                    heads(lambda h: _head(v_ref, h)), heads(lambda h: _sel_lane(gv, h)),
                                    heads(lambda h: _sel_lane(bv, h)), S_in)
        for h in range(H):
            o_ref[:, h * HEAD_DIM:(h + 1) * HEAD_DIM] = o[h]
            inv_ref[h, 0] = inv[h]
            S[h] = S_new[h]

    blk = pl.BlockSpec((CHUNK, H * HEAD_DIM), lambda n: (n, 0))
    gblk = pl.BlockSpec((CHUNK, LANES), lambda n: (n, 0))
    outs, comm_outs = _call(
        body, name="delta_fwd", grid=(N,), in_specs=[blk, blk, blk, gblk, gblk],
        out_specs=[blk, pl.BlockSpec((H, 1, HEAD_DIM, HEAD_DIM), lambda n: (0, n, 0, 0)),
                   pl.BlockSpec((H, 1, CHUNK, CHUNK), lambda n: (0, n, 0, 0))],
        out_shape=[jax.ShapeDtypeStruct((T, H * HEAD_DIM), F32), jax.ShapeDtypeStruct((H, N, HEAD_DIM, HEAD_DIM), F32),
                   jax.ShapeDtypeStruct((H, N, CHUNK, CHUNK), F32)],
        scratch_shapes=[pltpu.VMEM((H, HEAD_DIM, HEAD_DIM), F32)],
        semantics=("arbitrary",), args=(q, k, v, g, beta), comm=comm)
    return outs[0], outs[1], outs[2], comm_outs


def _delta_bwd(q, k, v, g, beta, S0, inv, do, comm=None):
    T = q.shape[0]
    H, N = q.shape[1] // HEAD_DIM, T // CHUNK

    def body(q_ref, k_ref, v_ref, g_ref, b_ref, s_ref, inv_ref, do_ref, dq_ref, dk_ref, dv_ref, dg_ref, db_ref, dS):
        @pl.when(pl.program_id(0) == 0)
        def _():
            dS[...] = jnp.zeros_like(dS)

        gv, bv = g_ref[...], b_ref[...]
        heads = lambda f: tuple(f(h) for h in range(H))
        _, vjp, _ = jax.vjp(_chunk_fn, heads(lambda h: _head(q_ref, h)), heads(lambda h: _head(k_ref, h)),
                            heads(lambda h: _head(v_ref, h)), heads(lambda h: _sel_lane(gv, h)), heads(lambda h: _sel_lane(bv, h)),
                            heads(lambda h: s_ref[h, 0]), heads(lambda h: inv_ref[h, 0]), has_aux=True)
        dq, dk, dv, dgB, dbB, dS_prev, _ = vjp((heads(lambda h: _head(do_ref, h)), heads(lambda h: dS[h])))
        for h in range(H):
            sl = slice(h * HEAD_DIM, (h + 1) * HEAD_DIM)
            dq_ref[:, sl] = dq[h]
            dk_ref[:, sl] = dk[h]
            dv_ref[:, sl] = dv[h]
            dg_ref[h] = dgB[h]
            db_ref[h] = dbB[h]
            dS[h] = dS_prev[h]

    blk = pl.BlockSpec((CHUNK, H * HEAD_DIM), lambda n: (N - 1 - n, 0))
    gblk = pl.BlockSpec((CHUNK, LANES), lambda n: (N - 1 - n, 0))
    hblk = pl.BlockSpec((H, CHUNK, LANES), lambda n: (0, N - 1 - n, 0))
    sd = jax.ShapeDtypeStruct
    outs, comm_outs = _call(
        body, name="delta_bwd", grid=(N,),
        in_specs=[blk, blk, blk, gblk, gblk, pl.BlockSpec((H, 1, HEAD_DIM, HEAD_DIM), lambda n: (0, N - 1 - n, 0, 0)),
                  pl.BlockSpec((H, 1, CHUNK, CHUNK), lambda n: (0, N - 1 - n, 0, 0)), blk],
        out_specs=[blk, blk, blk, hblk, hblk],
        out_shape=[sd((T, H * HEAD_DIM), F32)] * 3 + [sd((H, T, LANES), F32)] * 2,
        scratch_shapes=[pltpu.VMEM((H, HEAD_DIM, HEAD_DIM), F32)],
        semantics=("arbitrary",), args=(q, k, v, g, beta, S0, inv, do), comm=comm)
    return (*outs, comm_outs)


def _gnorm_fwd(o, proj, z_coff, gdn_t, DNW):
    T = o.shape[0]

    def fn(j, i, ov, zv, gv):
        def one(oh, zh, gh):
            r = lax.rsqrt(jnp.mean(oh * oh, axis=1, keepdims=True) + EPS)
            return oh * r * gh * (zh * _sigmoid(zh))
        return (_per_head(one, ov, zv, jnp.broadcast_to(gv, ov.shape)),)

    return _tiled(fn, T=T, C=DNW, ins=[("cur", o, None), ("cur", proj, lambda j: j + z_coff), ("row", gdn_t, None)],
                  out_dtypes=[BF16], cb=DNW, name="gnorm_fwd")[0]


def _gnorm_bwd(dymix, y_coff, o, proj, z_coff, gdn_t, DNW):
    T = o.shape[0]
    nh = DNW // HEAD_DIM

    def fn(j, i, dy, ov, zv, gv):
        dos, dzs, dgs = [], [], jnp.zeros((1, HEAD_DIM), F32)
        for h in range(nh):
            sl = slice(h * HEAD_DIM, (h + 1) * HEAD_DIM)
            dyh, oh, zh, gh = dy[:, sl].astype(F32), ov[:, sl], zv[:, sl], gv[:, sl]
            r = lax.rsqrt(jnp.mean(oh * oh, axis=1, keepdims=True) + EPS)
            on = oh * r
            sg = _sigmoid(zh)
            sz = zh * sg
            dzs.append(dyh * on * gh * (sg * (1.0 + zh * (1.0 - sg))))
            don = dyh * gh * sz
            dos.append(r * (don - on * jnp.mean(don * on, axis=1, keepdims=True)))
            dgs = dgs + jnp.sum(dyh * on * sz, axis=0, keepdims=True)
        cat = (lambda xs: xs[0] if nh == 1 else jnp.concatenate(xs, axis=1))
        return cat(dos), cat(dzs), _row0(dgs)

    T_ = T
    nI = T_ // _tile(T_, 256, HALO)
    tb = T_ // nI
    specs_cb = DNW

    def body_wrap():
        def body(dy_ref, o_ref, z_ref, g_ref, do_ref, dz_ref, dg_ref):
            i = pl.program_id(0)
            d_o, d_z, d_g = fn(0, i, dy_ref[...], o_ref[...], z_ref[...], g_ref[...])
            do_ref[...] = d_o
            dz_ref[...] = d_z.astype(dz_ref.dtype)

            @pl.when(i == 0)
            def _():
                dg_ref[...] = d_g

            @pl.when(i > 0)
            def _():
                dg_ref[...] += d_g

        return pl.pallas_call(
            body, name="gnorm_bwd", grid=(nI,),
            in_specs=[pl.BlockSpec((tb, specs_cb), lambda i: (i, y_coff)), pl.BlockSpec((tb, specs_cb), lambda i: (i, 0)),
                      pl.BlockSpec((tb, specs_cb), lambda i: (i, z_coff)), pl.BlockSpec((1, specs_cb), lambda i: (0, 0))],
            out_specs=[pl.BlockSpec((tb, specs_cb), lambda i: (i, 0)), pl.BlockSpec((tb, specs_cb), lambda i: (i, 0)),
                       pl.BlockSpec((HALO, HEAD_DIM), lambda i: (0, 0))],
            out_shape=[jax.ShapeDtypeStruct((T_, DNW), F32), jax.ShapeDtypeStruct((T_, DNW), BF16),
                       jax.ShapeDtypeStruct((HALO, HEAD_DIM), F32)],
            compiler_params=pltpu.CompilerParams(dimension_semantics=("arbitrary",), vmem_limit_bytes=VMEM_LIMIT),
        )(dymix, o, proj, gdn_t)

    return body_wrap()


def _ffn_fwd(up_g, up_v, w_g, w_v, cb):
    T, F = up_g.shape

    def fn(j, i, ug, uv, wg, wv):
        cg = _own(_conv_causal(ug, wg))
        cv = _own(_conv_causal(uv, wv))
        return (cg * _sigmoid(cg) * cv,)

    return _tiled(fn, T=T, C=F, ins=[("ext", up_g, None), ("ext", up_v, None), ("row", w_g, None), ("row", w_v, None)],
                  out_dtypes=[BF16], tb=1024, cb=cb, name="ffn_fwd")[0]


def _ffn_bwd(dact, up_g, up_v, w_g, w_v, cb):
    T, F = up_g.shape
    K = w_g.shape[0]

    def fn(j, i, da, ug, uv, wg, wv):
        cg = _conv_causal(ug, wg)
        cv = _conv_causal(uv, wv)
        sg = _sigmoid(cg)
        dgate = da * cv * (sg * (1.0 + cg * (1.0 - sg)))
        dval = da * (cg * sg)
        return (_own(_conv_anti(dgate, wg)), _own(_conv_anti(dval, wv)), _conv_dw(dgate, ug, K), _conv_dw(dval, uv, K))

    return _tiled(fn, T=T, C=F, ins=[("ext", dact, None), ("ext", up_g, None), ("ext", up_v, None), ("row", w_g, None),
                                      ("row", w_v, None)], out_dtypes=[BF16, BF16], acc_rows=[HALO, HALO], tb=1024, cb=cb,
                  name="ffn_bwd")


def _wide(R, Cc, n_f32, unit=HALO):
    cb = Cc if (Cc % LANES or Cc <= 4096) else _tile(Cc, 2048, LANES)
    cap = max(unit, EW_VMEM_BUDGET // (2 * 4 * n_f32 * cb) // unit * unit)
    return _tile(R, cap, unit), cb


def _adamw(w, g, m, v, name):
    R, Cc = w.shape
    tb, cb = _wide(R, Cc, 7)
    c1 = 1.0 / (1.0 - ADAM_B1 ** ADAM_STEP)
    c2 = 1.0 / (1.0 - ADAM_B2 ** ADAM_STEP)

    def fn(j, i, wv, gv, mv, vv):
        m2 = ADAM_B1 * mv + (1.0 - ADAM_B1) * gv
        v2 = ADAM_B2 * vv + (1.0 - ADAM_B2) * (gv * gv)
        delta = -ADAM_LR * ((m2 * c1) / (jnp.sqrt(v2 * c2) + ADAM_EPS) + ADAM_WD * wv)
        return delta, m2, v2

    return _tiled(fn, T=R, C=Cc, ins=[("cur", w, None), ("cur", g, None), ("cur", m, None), ("cur", v, None)],
                  out_dtypes=[F32, F32, F32], tb=tb, cb=cb, name=name)


def _sum_stack(st, name):
    S, R, Cc = st.shape
    cb = _tile(Cc, 512, LANES) if Cc % LANES == 0 else Cc

    def fn(j, i, sv):
        t = sv[0]
        for s in range(1, S):
            t = t + sv[s]
        return (t,)

    return _tiled(fn, T=R, C=Cc, ins=[("stack", st, None)], out_dtypes=[F32], cb=cb, name=name)[0]


ANY = pl.BlockSpec(memory_space=pl.ANY)


def _place():
    x, y, c = lax.axis_index("x"), lax.axis_index("y"), lax.axis_index("c")
    return x, y, c, 2 * x + y


def _chip_dev(s, c):
    return (s // 2, s % 2, c)


class _Comm:
    def __init__(self, ins, out_shapes, sems, start, wait, aliases=None):
        self.ins, self.out_shapes, self.sems = list(ins), list(out_shapes), list(sems)
        self.start, self.wait, self.aliases = start, wait, dict(aliases or {})


def _merge(*comms):
    offs, i, o, s = [], 0, 0, 0
    for cm in comms:
        offs.append((i, o, s))
        i, o, s = i + len(cm.ins), o + len(cm.out_shapes), s + len(cm.sems)

    def part(refs, k, cm):
        i0, o0, s0 = offs[k]
        return refs[0][i0:i0 + len(cm.ins)], refs[1][o0:o0 + len(cm.out_shapes)], refs[2][s0:s0 + len(cm.sems)]

    def start(*refs):
        for k, cm in enumerate(comms):
            cm.start(*part(refs, k, cm))

    def wait(*refs):
        for k, cm in enumerate(comms):
            cm.wait(*part(refs, k, cm))

    aliases = {}
    for k, cm in enumerate(comms):
        for a, b in cm.aliases.items():
            aliases[offs[k][0] + a] = offs[k][1] + b
    return _Comm([a for cm in comms for a in cm.ins], [a for cm in comms for a in cm.out_shapes],
                 [a for cm in comms for a in cm.sems], start, wait, aliases)


def _call(body, *, name, grid, in_specs, out_specs, out_shape, scratch_shapes, semantics, args, comm=None):
    if comm is None:
        outs = pl.pallas_call(
            body, name=name, grid=grid, in_specs=in_specs, out_specs=out_specs, out_shape=out_shape,
            scratch_shapes=list(scratch_shapes),
            compiler_params=pltpu.CompilerParams(dimension_semantics=semantics, vmem_limit_bytes=VMEM_LIMIT))(*args)
        return list(outs), []
    n_in, n_out, n_scr = len(in_specs), len(out_specs), len(scratch_shapes)
    ci, co = len(comm.ins), len(comm.out_shapes)

    def wrapped(*refs):
        r = 0
        ins, r = refs[r:r + n_in], r + n_in
        cins, r = refs[r:r + ci], r + ci
        outs, r = refs[r:r + n_out], r + n_out
        couts, r = refs[r:r + co], r + co
        scr, r = refs[r:r + n_scr], r + n_scr
        csems = refs[r:]
        ids = [pl.program_id(a) for a in range(len(grid))]
        first, last = ids[0] == 0, ids[0] == grid[0] - 1
        for a in range(1, len(grid)):
            first = jnp.logical_and(first, ids[a] == 0)
            last = jnp.logical_and(last, ids[a] == grid[a] - 1)

        @pl.when(first)
        def _():
            comm.start(cins, couts, csems)

        body(*ins, *outs, *scr)

        @pl.when(last)
        def _():
            comm.wait(cins, couts, csems)

    outs = pl.pallas_call(
        wrapped, name=name, grid=grid, in_specs=list(in_specs) + [ANY] * ci, out_specs=list(out_specs) + [ANY] * co,
        out_shape=list(out_shape) + comm.out_shapes, scratch_shapes=list(scratch_shapes) + comm.sems,
        input_output_aliases={n_in + a: n_out + b for a, b in comm.aliases.items()},
        compiler_params=pltpu.CompilerParams(dimension_semantics=("arbitrary",) * len(grid), vmem_limit_bytes=VMEM_LIMIT),
    )(*args, *comm.ins)
    return list(outs[:n_out]), list(outs[n_out:])


def _run_comm(comm, name):
    ci, co = len(comm.ins), len(comm.out_shapes)

    def body(*refs):
        cins, couts, csems = refs[:ci], refs[ci:ci + co], refs[ci + co:]
        comm.start(cins, couts, csems)
        comm.wait(cins, couts, csems)

    outs = pl.pallas_call(body, name=name, in_specs=[ANY] * ci, out_specs=[ANY] * co, out_shape=comm.out_shapes,
                          scratch_shapes=comm.sems, input_output_aliases=comm.aliases)(*comm.ins)
    return list(outs)


def _ag_comm(shard, land=None, q=0, nq=1):
    two, R2, Cc = shard.shape
    rows = pl.ds(q * (R2 // nq), R2 // nq)
    DMA = pltpu.SemaphoreType.DMA

    def copies(ins, outs, sems, which):
        sh, out = ins[0], outs[0]
        send1, recv1, send2, recv2, send0, recv0 = sems
        x, y, c, s = _place()
        sib = (x, y, 1 - c)
        rc = pltpu.make_async_remote_copy
        if which == "first":
            return [rc(sh.at[c, rows], out.at[s, c, rows], send1.at[m - 1], recv1.at[m - 1],
                       device_id=_chip_dev(s ^ m, c), device_id_type=MESH) for m in range(1, 4)]
        if which == "own":
            return [rc(sh.at[h, rows], out.at[s, h, rows], send0.at[h], recv0.at[h], device_id=sib, device_id_type=MESH)
                    for h in range(2)]
        if which == "landed":
            return [rc(sh.at[c, rows], out.at[s ^ m, c, rows], send1.at[m - 1], recv1.at[m - 1], device_id=sib,
                       device_id_type=MESH) for m in range(1, 4)]
        half = c if which == "passed" else 1 - c
        return [rc(out.at[s ^ m, half, rows], out.at[s ^ m, half, rows], send2.at[m - 1], recv2.at[m - 1], device_id=sib,
                   device_id_type=MESH) for m in range(1, 4)]

    def start(ins, outs, sems):
        for cp in copies(ins, outs, sems, "first") + copies(ins, outs, sems, "own"):
            cp.start()

    def wait(ins, outs, sems):
        passed = copies(ins, outs, sems, "passed")
        for lan, pas in zip(copies(ins, outs, sems, "landed"), passed):
            lan.wait_recv()
            pas.start()
        for cp in copies(ins, outs, sems, "handed"):
            cp.wait_recv()
        for cp in copies(ins, outs, sems, "own"):
            cp.wait()
        for cp in copies(ins, outs, sems, "first") + passed:
            cp.wait_send()

    return _Comm([shard] + ([land] if land is not None else []), [jax.ShapeDtypeStruct((4, two, R2, Cc), shard.dtype)],
                 [DMA((3,)), DMA((3,)), DMA((3,)), DMA((3,)), DMA((2,)), DMA((2,))], start, wait,
                 {1: 0} if land is not None else None)


def _a2a_comm(S1, q=0, nq=1, land=None):
    S4, R2, Cc = S1.shape
    rows = pl.ds(q * (R2 // nq), R2 // nq)
    DMA = pltpu.SemaphoreType.DMA

    def copies(ins, outs, sems):
        x, y, c, s = _place()
        return [pltpu.make_async_remote_copy(ins[0].at[s ^ m, rows], outs[0].at[m - 1, rows], sems[0].at[m - 1],
                                             sems[1].at[m - 1], device_id=_chip_dev(s ^ m, c), device_id_type=MESH)
                for m in range(1, 4)]

    def start(ins, outs, sems):
        for cp in copies(ins, outs, sems):
            cp.start()

    def wait(ins, outs, sems):
        for cp in copies(ins, outs, sems):
            cp.wait()

    return _Comm([S1] + ([land] if land is not None else []), [jax.ShapeDtypeStruct((3, R2, Cc), S1.dtype)],
                 [DMA((3,)), DMA((3,))], start, wait, {1: 0} if land is not None else None)


def _halves(G):
    return G.reshape(G.shape[0], 2, G.shape[1] // 2, G.shape[2])


def _swap_comm(piece):
    n, two, R2, Cc = piece.shape
    DMA = pltpu.SemaphoreType.DMA

    def copies(ins, outs, sems):
        x, y, c, s = _place()
        return [pltpu.make_async_remote_copy(ins[0].at[t, 1 - c], outs[0].at[t], sems[0].at[t], sems[1].at[t],
                                             device_id=(x, y, 1 - c), device_id_type=MESH) for t in range(n)]

    def start(ins, outs, sems):
        for cp in copies(ins, outs, sems):
            cp.start()

    def wait(ins, outs, sems):
        for cp in copies(ins, outs, sems):
            cp.wait()

    return _Comm([piece], [jax.ShapeDtypeStruct((n, R2, Cc), piece.dtype)], [DMA((n,)), DMA((n,))], start, wait)


def _add_half(pieces, As, cidx, name):
    R2, Cc = pieces[0].shape[2:]
    S4 = sum(pc.shape[0] for pc in pieces)
    tb, cb = _wide(R2, Cc, 3, 2 * HALO)
    nI, nJ = R2 // tb, Cc // cb

    def body(c_ref, g_ref, a_ref, *rest):
        rest[-1][...] = (g_ref[0, 0] + a_ref[0]).astype(BF16)

    out, t0 = None, 0
    for k, (pc, A) in enumerate(zip(pieces, As)):
        grid_spec = pltpu.PrefetchScalarGridSpec(
            num_scalar_prefetch=1, grid=(pc.shape[0], nI, nJ),
            in_specs=[pl.BlockSpec((1, 1, tb, cb), lambda t, i, j, c_ref: (t, c_ref[0], i, j)),
                      pl.BlockSpec((1, tb, cb), lambda t, i, j, c_ref: (t, i, j))] + ([ANY] if k else []),
            out_specs=pl.BlockSpec((tb, cb), lambda t, i, j, c_ref, t0=t0: ((t0 + t) * nI + i, j)))
        out = pl.pallas_call(
            functools.partial(body), name=f"{name}{k}", grid_spec=grid_spec, out_shape=jax.ShapeDtypeStruct((S4 * R2, Cc), BF16),
            input_output_aliases={3: 0} if k else {},
            compiler_params=pltpu.CompilerParams(dimension_semantics=("parallel", "parallel", "parallel"),
                                                 vmem_limit_bytes=VMEM_LIMIT),
        )(*((cidx, pc, A) + ((out,) if k else ())))
        t0 += pc.shape[0]
    return out.reshape(S4, R2, Cc)


def _add_own(S1, B, chip_idx, cidx, name):
    S4, R2, Cc = S1.shape
    tb, cb = _wide(R2, Cc, 3, 2 * HALO)

    def body(s_idx, c_idx, s_ref, b_ref, o_ref):
        o_ref[...] = ((s_ref[0].astype(F32) + b_ref[0].astype(F32)) + b_ref[1].astype(F32)) + b_ref[2].astype(F32)

    grid_spec = pltpu.PrefetchScalarGridSpec(
        num_scalar_prefetch=2, grid=(R2 // tb, Cc // cb),
        in_specs=[pl.BlockSpec((1, tb, cb), lambda i, j, s_idx, c_idx: (s_idx[0], i, j)),
                  pl.BlockSpec((3, tb, cb), lambda i, j, s_idx, c_idx: (0, i, j))],
        out_specs=pl.BlockSpec((None, tb, cb), lambda i, j, s_idx, c_idx: (c_idx[0], i, j)))
    return pl.pallas_call(body, name=name, grid_spec=grid_spec, out_shape=jax.ShapeDtypeStruct((2, R2, Cc), F32),
                          compiler_params=pltpu.CompilerParams(dimension_semantics=("parallel", "parallel"),
                                                               vmem_limit_bytes=VMEM_LIMIT))(chip_idx, cidx, S1, B)


def _sibling_fill(Hs, name):
    def body(h_ref, out_ref, send, recv):
        x, y, c, s = _place()
        cp = pltpu.make_async_remote_copy(h_ref.at[c], out_ref.at[c], send, recv, device_id=(x, y, 1 - c), device_id_type=MESH)
        cp.start()
        cp.wait()

    return pl.pallas_call(
        body, name=name, in_specs=[ANY], out_specs=ANY, out_shape=jax.ShapeDtypeStruct(Hs.shape, Hs.dtype),
        input_output_aliases={0: 0}, scratch_shapes=[pltpu.SemaphoreType.DMA, pltpu.SemaphoreType.DMA],
    )(Hs)


def _gather_all(buf, name):
    R, Cc = buf.shape

    def body(b_ref, out_ref, send, recv, local):
        x, y, c, s = _place()
        d = 2 * s + c
        mine = pltpu.make_async_copy(b_ref, out_ref.at[d], local)
        mine.start()
        cps = []
        for m in range(1, 8):
            t = d ^ m
            cp = pltpu.make_async_remote_copy(b_ref, out_ref.at[d], send.at[m - 1], recv.at[m - 1],
                                              device_id=(t // 4, (t // 2) % 2, t % 2), device_id_type=MESH)
            cp.start()
            cps.append(cp)
        for cp in cps:
            cp.wait()
        mine.wait()

    return pl.pallas_call(
        body, name=name, in_specs=[ANY], out_specs=ANY, out_shape=jax.ShapeDtypeStruct((8, R, Cc), buf.dtype),
        scratch_shapes=[pltpu.SemaphoreType.DMA((7,)), pltpu.SemaphoreType.DMA((7,)), pltpu.SemaphoreType.DMA],
    )(buf)


def _finish_shard(S1, B, cidx, chip_idx, name):
    Hs = _sibling_fill(_add_own(S1, B, chip_idx, cidx, name + "_sum"), name + "_gather")
    return Hs.reshape(2 * Hs.shape[1], Hs.shape[2])


def _pack_rows(vs):
    flat = jnp.concatenate([v.reshape(-1) for v in vs])
    n = flat.shape[0]
    rows = -(-n // (LANES * 2 * HALO)) * 2 * HALO
    return jnp.pad(flat, (0, rows * LANES - n)).reshape(rows, LANES)


def _unpack_rows(buf, shapes):
    flat = buf.reshape(-1)
    outs, o = [], 0
    for shp in shapes:
        n = 1
        for d in shp:
            n *= d
        outs.append(flat[o:o + n].reshape(shp))
        o += n
    return outs


def kernel(x, p, norm_mix_g, w_in, conv_a_w, conv_qkv_w, a_log, dt_bias, dn_norm_g, w_out, norm_ffn_g, w_up, conv_ffn_w, w_down, norm_ple_g, w_ple_gate, w_ple_proj, final_norm_g, loss_target, m_norm_mix_g, m_w_in, m_conv_a_w, m_conv_qkv_w, m_a_log, m_dt_bias, m_dn_norm_g, m_w_out, m_norm_ffn_g, m_w_up, m_conv_ffn_w, m_w_down, m_norm_ple_g, m_w_ple_gate, m_w_ple_proj, m_final_norm_g, v_norm_mix_g, v_w_in, v_conv_a_w, v_conv_qkv_w, v_a_log, v_dt_bias, v_dn_norm_g, v_w_out, v_norm_ffn_g, v_w_up, v_conv_ffn_w, v_w_down, v_norm_ple_g, v_w_ple_gate, v_w_ple_proj, v_final_norm_g):
    xs = x[0]
    ps = p[0, 0]
    tgt = loss_target[0]
    T, D = xs.shape
    H = a_log.shape[-1]
    DNW = H * HEAD_DIM
    CW = conv_a_w.shape[-1] * 4
    F = w_down.shape[1] * 4
    PD = ps.shape[-1]
    IN_MAIN = 3 * CW + 4 * DNW
    IN_COLS = IN_MAIN + 2 * H
    assert w_in.shape[-1] * 4 == IN_COLS and CW + DNW == D and 2 * H <= LANES
    cb = _tile(min(CW, DNW), 512, LANES)
    while F % cb:
        cb -= LANES
    cidx = lax.axis_index("c").astype(jnp.int32).reshape(1)
    chip = 2 * lax.axis_index("x") + lax.axis_index("y")

    def halves(w):
        sh = w[0].astype(BF16)
        return sh.reshape(2, sh.shape[0] // 2, sh.shape[1])

    def whole(land):
        return land.reshape(4, 2 * land.shape[2], land.shape[3])

    def cols(g4):
        return jnp.transpose(g4, (1, 0, 2)).reshape(g4.shape[1], 4 * g4.shape[2])

    def rows(g4):
        return g4.reshape(4 * g4.shape[1], g4.shape[2])

    conv_shapes = [conv_a_w[0].shape, conv_qkv_w[0].shape, conv_ffn_w[0].shape]
    cpack = _pack_rows([conv_a_w[0], conv_qkv_w[0], conv_ffn_w[0]])
    sh_in, sh_out, sh_up, sh_down, sh_pg, sh_pp = (halves(w) for w in (w_in, w_out, w_up, w_down, w_ple_gate, w_ple_proj))
    l_in, cg = _run_comm(_merge(_ag_comm(sh_in), _ag_comm(cpack.reshape(2, cpack.shape[0] // 2, LANES))), "ag_w_in_conv")
    w_in_4 = whole(l_in)
    w_in_f = jnp.concatenate([w_in_4[t] for t in range(4)], axis=1)
    w_in_main = w_in_f[:, :IN_MAIN]
    w_in_small = jnp.pad(w_in_4[3][:, IN_MAIN - 3 * (IN_COLS // 4):], ((0, 0), (0, LANES - 2 * H)))
    cg = cg.reshape(4, cpack.shape[0], LANES)
    parts = [_unpack_rows(cg[t], conv_shapes) for t in range(4)]
    cw_a = jnp.concatenate([parts[t][0] for t in range(4)], axis=1)
    cw_qkv = jnp.concatenate([parts[t][1] for t in range(4)], axis=1)
    cw_ffn = jnp.concatenate([parts[t][2] for t in range(4)], axis=1)
    cw_q, cw_k, cw_v = cw_qkv[:, :DNW], cw_qkv[:, DNW:2 * DNW], cw_qkv[:, 2 * DNW:]
    cw_fg, cw_fv = cw_ffn[:, :F], cw_ffn[:, F:]
    pad_row = lambda v: jnp.pad(v, ((0, 0), (0, LANES - v.shape[1])))
    a_log_row, dt_row = pad_row(a_log), pad_row(dt_bias)
    gdn_t = jnp.tile(dn_norm_g, (1, H))
    gfin = final_norm_g.reshape(1, D)

    h1 = _rms_fwd(xs, norm_mix_g, "rms1")
    proj, (l_up,) = _mm(h1, w_in_main, mode="nn", out_dtypes=[F32], name="mm_proj", comm=_ag_comm(sh_up, q=0, nq=2))
    small = _mm(h1, w_in_small, mode="nn", out_dtypes=[F32], name="mm_small")
    ya = _ga_fwd(proj, cw_a, CW, cb)
    nq = 3 * CW // cb
    nd = DNW // cb
    qn = _qkv_fwd(proj, cw_q, nq, True, DNW, cb, "q_fwd")
    kn = _qkv_fwd(proj, cw_k, nq + nd, True, DNW, cb, "k_fwd")
    vs = _qkv_fwd(proj, cw_v, nq + 2 * nd, False, DNW, cb, "v_fwd")
    g, beta = _gb_fwd(small, a_log_row, dt_row, H)
    o, S0, inv_c, (l_up, l_out) = _delta_fwd(qn, kn, vs, g, beta, comm=_merge(_ag_comm(sh_up, l_up, q=1, nq=2), _ag_comm(sh_out)))
    w_out_f = rows(whole(l_out))
    w_out_a, w_out_b = w_out_f[:CW], w_out_f[CW:]
    w_up_4 = whole(l_up)
    z_coff = (3 * CW + 3 * DNW) // DNW
    assert (3 * CW + 3 * DNW) % DNW == 0 and CW % DNW == 0
    yb = _gnorm_fwd(o, proj, z_coff, gdn_t, DNW)
    add = lambda acc, r: (r + acc,)
    x1 = _mm(ya, w_out_a, mode="nn", out_dtypes=[F32], epi=add, extras=[xs], name="mm_out_a")
    x1 = _mm(yb, w_out_b, mode="nn", out_dtypes=[F32], epi=add, extras=[x1], name="mm_out_b")
    h2 = _rms_fwd(x1, norm_ffn_g, "rms2")
    up_g, (l_down,) = _mm(h2, w_up_4, mode="nn", b_split=(0, 2), out_dtypes=[F32], name="mm_up_g",
                          comm=_ag_comm(sh_down, q=0, nq=2))
    up_v, (l_down,) = _mm(h2, w_up_4, mode="nn", b_split=(2, 2), out_dtypes=[F32], name="mm_up_v",
                          comm=_ag_comm(sh_down, l_down, q=1, nq=2))
    w_down_f = rows(whole(l_down))
    act = _ffn_fwd(up_g, up_v, cw_fg, cw_fv, cb)
    x2, (l_pg, l_pp) = _mm(act, w_down_f, mode="nn", out_dtypes=[F32], epi=add, extras=[x1], name="mm_down",
                           comm=_merge(_ag_comm(sh_pg), _ag_comm(sh_pp)))
    w_pg_f = rows(whole(l_pg))
    w_pp_4 = whole(l_pp)
    h3 = _rms_fwd(x2, norm_ple_g, "rms3")
    pp = _mm(ps, w_pp_4, mode="nn", b_split=(0, 4), out_dtypes=[F32], name="mm_pp")

    def ple_epi(acc, x2v, ppv):
        pg = _sigmoid(acc)
        return x2v + pg * ppv, pg

    x3, pg = _mm(h3, w_pg_f, mode="nn", out_dtypes=[F32, F32], epi=ple_epi, extras=[x2, pp], name="mm_pg")

    dx3, dpg, dpp, fin = _final_fb(x3, tgt, gfin, pp, pg)
    loss = lax.psum(jnp.sum(fin[1]), ("x", "y", "c"))
    d_gfin = fin[0:1]
    def split_cols(dW):
        R, C4 = dW.shape
        return jnp.transpose(dW.reshape(R, 4, C4 // 4), (1, 0, 2))

    def split_rows(dW):
        return dW.reshape(4, dW.shape[0] // 4, dW.shape[1])

    dW_pp = _mm(ps, dpp, mode="tn", out_split=4, out_dtypes=[F32], name="mm_dw_pp")
    dW_pg = _mm(h3, dpg, mode="tn", out_dtypes=[F32], name="mm_dw_pg")
    P_pp, P_pg = _halves(dW_pp), _halves(split_rows(dW_pg))
    dh3, (A_pp, A_pg) = _mm(dpg, w_pg_f, mode="nt", out_dtypes=[F32], name="mm_dh3",
                            comm=_merge(_swap_comm(P_pp), _swap_comm(P_pg)))
    S_pp = _add_half([P_pp], [A_pp], cidx, "rs_w_pp_add")
    S_pg = _add_half([P_pg], [A_pg], cidx, "rs_w_pg_add")
    dx2, dx2_b, d_gple = _rms_bwd(dh3, x2, norm_ple_g, dx3, "rms3_bwd")
    dW_down, (B_pp, B_pg) = _mm(act, dx2_b, mode="tn", out_dtypes=[F32], name="mm_dw_down",
                                comm=_merge(_a2a_comm(S_pp), _a2a_comm(S_pg)))
    P_down = _halves(split_rows(dW_down))
    dact, (A_down,) = _mm(dx2_b, w_down_f, mode="nt", out_dtypes=[F32], name="mm_dact", comm=_swap_comm(P_down))
    S_down = _add_half([P_down], [A_down], cidx, "rs_w_down_add")
    dup_g, dup_v, dcw_fg, dcw_fv = _ffn_bwd(dact, up_g, up_v, cw_fg, cw_fv, cb)
    dW_up_g, (B_down,) = _mm(h2, dup_g, mode="tn", out_split=2, out_dtypes=[F32], name="mm_dw_up_g", comm=_a2a_comm(S_down))
    P_ug = _halves(dW_up_g)
    dW_up_v, (A_ug,) = _mm(h2, dup_v, mode="tn", out_split=2, out_dtypes=[F32], name="mm_dw_up_v", comm=_swap_comm(P_ug))
    P_uv = _halves(dW_up_v)
    dh2, (A_uv,) = _mm(dup_g, w_up_4, mode="nt", b_split=(0, 2), out_dtypes=[F32], name="mm_dh2_g", comm=_swap_comm(P_uv))
    S_up = _add_half([P_ug, P_uv], [A_ug, A_uv], cidx, "rs_w_up_add")
    dh2 = _mm(dup_v, w_up_4, mode="nt", b_split=(2, 2), out_dtypes=[F32], epi=add, extras=[dh2], name="mm_dh2_v")
    dx1, dx1_b, d_gffn = _rms_bwd(dh2, x1, norm_ffn_g, dx2, "rms2_bwd")
    dW_out_a = _mm(ya, dx1_b, mode="tn", out_dtypes=[F32], name="mm_dw_out_a")
    dW_out_b = _mm(yb, dx1_b, mode="tn", out_dtypes=[F32], name="mm_dw_out_b")
    P_oa, P_ob = _halves(dW_out_a.reshape(-1, D // 4, D)), _halves(dW_out_b.reshape(-1, D // 4, D))
    dymix, (A_oa, A_ob) = _mm(dx1_b, w_out_f, mode="nt", out_dtypes=[F32], name="mm_dymix",
                              comm=_merge(_swap_comm(P_oa), _swap_comm(P_ob)))
    S_out = _add_half([P_oa, P_ob], [A_oa, A_ob], cidx, "rs_w_out_add")
    dax, dab, dac, dcw_a = _ga_bwd(dymix, proj, cw_a, CW, cb)
    do, dz, d_gdn = _gnorm_bwd(dymix, CW // DNW, o, proj, z_coff, gdn_t, DNW)
    dqn, dkn, dvs, dgB, dbB, (B_up, B_out) = _delta_bwd(qn, kn, vs, g, beta, S0, inv_c, do,
                                                        comm=_merge(_a2a_comm(S_up), _a2a_comm(S_out)))
    dq_pre, dcw_q = _qkv_bwd(dqn, proj, cw_q, nq, True, DNW, cb, "q_bwd")
    dk_pre, dcw_k = _qkv_bwd(dkn, proj, cw_k, nq + nd, True, DNW, cb, "k_bwd")
    dv_pre, dcw_v = _qkv_bwd(dvs, proj, cw_v, nq + 2 * nd, False, DNW, cb, "v_bwd")
    dsmall, d_ab = _gb_bwd(dgB, dbB, small, g, beta, a_log_row, dt_row, H)
    dproj = jnp.concatenate([dax, dab, dac, dq_pre, dk_pre, dv_pre, dz], axis=1)
    dW_in_main = _mm(h1, dproj, mode="tn", out_dtypes=[F32], name="mm_dw_in")
    dW_in_small = _mm(h1, dsmall, mode="tn", out_dtypes=[F32], name="mm_dw_in_small")
    cs = IN_COLS // 4
    P_in = _halves(jnp.stack([dW_in_main[:, t * cs:(t + 1) * cs] for t in range(3)]
                             + [jnp.concatenate([dW_in_main[:, 3 * cs:], dW_in_small[:, :2 * H]], axis=1)]))
    (A_in,) = _run_comm(_swap_comm(P_in), "rs_w_in_swap")
    S_in = _add_half([P_in], [A_in], cidx, "rs_w_in_add")
    dh1, (B_in,) = _mm(dproj, w_in_main, mode="nt", out_dtypes=[F32], name="mm_dh1", comm=_a2a_comm(S_in))
    dh1 = _mm(dsmall, w_in_small, mode="nt", out_dtypes=[F32], epi=add, extras=[dh1], name="mm_dh1_small")
    dx, _, d_gmix = _rms_bwd(dh1, xs, norm_mix_g, dx1, "rms1_bwd")

    chip_idx = chip.astype(jnp.int32).reshape(1)

    def update(S1, B, w, m, v, name):
        gr = _finish_shard(S1, B, cidx, chip_idx, "rs_" + name)
        delta, m2, v2 = _adamw(w[0], gr, m[0], v[0], "adamw_" + name)
        return gr[None], delta[None], m2[None], v2[None]

    big = {
        "w_in": update(S_in, B_in, w_in, m_w_in, v_w_in, "w_in"),
        "w_out": update(S_out, B_out, w_out, m_w_out, v_w_out, "w_out"),
        "w_up": update(S_up, B_up, w_up, m_w_up, v_w_up, "w_up"),
        "w_down": update(S_down, B_down, w_down, m_w_down, v_w_down, "w_down"),
        "w_ple_gate": update(S_pg, B_pg, w_ple_gate, m_w_ple_gate, v_w_ple_gate, "w_pg"),
        "w_ple_proj": update(S_pp, B_pp, w_ple_proj, m_w_ple_proj, v_w_ple_proj, "w_pp"),
    }

    small_grads = [d_gmix[0:1], dcw_a[:cw_a.shape[0]], jnp.concatenate([dcw_q, dcw_k, dcw_v], axis=1)[:cw_qkv.shape[0]],
                   d_ab[0:1, :H], d_ab[1:2, :H], d_gdn[0:1], d_gffn[0:1],
                   jnp.concatenate([dcw_fg, dcw_fv], axis=1)[:cw_ffn.shape[0]], d_gple[0:1], d_gfin]
    small_shapes = [v.shape for v in small_grads]
    gpack = _pack_rows(small_grads)
    gsum = _sum_stack(_gather_all(gpack, "ag_small"), "sum_small")
    (g_gmix, g_cwa, g_cwqkv, g_alog, g_dt, g_gdn, g_gffn, g_cwffn, g_gple, g_gfin) = _unpack_rows(gsum, small_shapes)

    def my_cols(v):
        Cc = v.shape[1] // 4
        return lax.dynamic_slice_in_dim(v, chip * Cc, Cc, axis=1)

    g_small = [g_gmix, my_cols(g_cwa), my_cols(g_cwqkv), g_alog, g_dt, g_gdn, g_gffn, my_cols(g_cwffn), g_gple, g_gfin]
    w_small = [norm_mix_g, conv_a_w[0], conv_qkv_w[0], a_log, dt_bias, dn_norm_g, norm_ffn_g, conv_ffn_w[0], norm_ple_g, gfin]
    m_small = [m_norm_mix_g, m_conv_a_w[0], m_conv_qkv_w[0], m_a_log, m_dt_bias, m_dn_norm_g, m_norm_ffn_g, m_conv_ffn_w[0],
               m_norm_ple_g, m_final_norm_g.reshape(1, D)]
    v_small = [v_norm_mix_g, v_conv_a_w[0], v_conv_qkv_w[0], v_a_log, v_dt_bias, v_dn_norm_g, v_norm_ffn_g, v_conv_ffn_w[0],
               v_norm_ple_g, v_final_norm_g.reshape(1, D)]
    shp = [v.shape for v in w_small]
    ds_, ms_, vs_ = _adamw(_pack_rows(w_small), _pack_rows(g_small), _pack_rows(m_small), _pack_rows(v_small), "adamw_small")
    out_shapes = [norm_mix_g.shape, conv_a_w.shape, conv_qkv_w.shape, a_log.shape, dt_bias.shape, dn_norm_g.shape,
                  norm_ffn_g.shape, conv_ffn_w.shape, norm_ple_g.shape, final_norm_g.shape]
    rs = lambda vals: [v.reshape(s) for v, s in zip(vals, out_shapes)]
    sg, sd_, sm_, sv_ = rs(g_small), rs(_unpack_rows(ds_, shp)), rs(_unpack_rows(ms_, shp)), rs(_unpack_rows(vs_, shp))
    names_small = ["norm_mix_g", "conv_a_w", "conv_qkv_w", "a_log", "dt_bias", "dn_norm_g", "norm_ffn_g", "conv_ffn_w",
                   "norm_ple_g", "final_norm_g"]
    res = {n: (sg[i], sd_[i], sm_[i], sv_[i]) for i, n in enumerate(names_small)}
    res.update(big)
    order = ["norm_mix_g", "w_in", "conv_a_w", "conv_qkv_w", "a_log", "dt_bias", "dn_norm_g", "w_out", "norm_ffn_g", "w_up",
             "conv_ffn_w", "w_down", "norm_ple_g", "w_ple_gate", "w_ple_proj", "final_norm_g"]
    return (loss, dx[None], *[res[n][0] for n in order], *[res[n][1] for n in order], *[res[n][2] for n in order],
            *[res[n][3] for n in order])
```

```python
import functools

import jax
import jax.numpy as jnp
from jax import lax
from jax.experimental import pallas as pl
from jax.experimental.pallas import tpu as pltpu

F32 = jnp.float32
BF16 = jnp.bfloat16
LANES = 128
HALO = 8
HEAD_DIM = 128
CHUNK = 64
EPS = 1e-6
VMEM_LIMIT = 56 * 1024 * 1024
MM_VMEM_BUDGET = 40 * 1024 * 1024
MM_STEP_BYTES = 1 << 20
EW_VMEM_BUDGET = 28 * 1024 * 1024
MESH = pl.DeviceIdType.MESH

ADAM_LR, ADAM_B1, ADAM_B2, ADAM_EPS, ADAM_WD, ADAM_STEP = 0.001, 0.9, 0.999, 1e-08, 0.01, 10


def _tile(n, cap, unit):
    if n <= cap:
        return n
    d = (cap // unit) * unit
    while d >= unit:
        if n % d == 0:
            return d
        d -= unit
    raise ValueError(f"no tile for {n} (cap {cap}, unit {unit})")


def _sigmoid(x):
    return 1.0 / (1.0 + jnp.exp(-x))


def _divisors(n, cap):
    ds = [d for d in range(cap // LANES * LANES, 0, -LANES) if n % d == 0]
    return [n] if (n <= cap or not ds) else ds


def _mm_tiles(M, N, K, n_unit, k_unit, a_bytes, n_blocks_mn, a_transposed):
    best = None
    for tm in _divisors(M, 1536):
        for tn in _divisors(n_unit, 1536):
            for tk in _divisors(k_unit, 4096):
                nk = K // tk
                vmem = 2 * tm * tk * a_bytes + 2 * tk * tn * 2 + 2 * 4 * tm * tn * n_blocks_mn + (4 * tm * tn if nk > 1 else 0)
                if vmem > MM_VMEM_BUDGET:
                    continue
                steps = (M // tm) * (N // tn) * nk
                cost = (M * K * a_bytes * (N // tn if nk > 1 else 1) + K * N * 2 * (M // tm) + 4 * M * N * n_blocks_mn
                        + (8 * M * N * nk // 3 if nk > 1 else 0) + steps * MM_STEP_BYTES
                        + (2 * steps * tm * tk if a_transposed else 0))
                if best is None or cost < best[0]:
                    best = (cost, tm, tn, tk)
    return best[1:]


def _mm(a, b, *, mode, out_dtypes, name, epi=None, extras=(), comm=None, b_split=None, out_split=None):
    if b_split is not None:
        lo, ns = b_split
        Rb, Cb = b.shape[1], b.shape[2]
    if mode == "nn":
        (M, K), N = a.shape, (ns * Cb if b_split else b.shape[1])
    elif mode == "nt":
        (M, K), N = a.shape, (Rb if b_split else b.shape[0])
    else:
        (K, M), N = a.shape, b.shape[1]
    n_ex, n_out = len(extras), len(out_dtypes)
    n_unit = Cb if (b_split and mode == "nn") else (N // out_split if out_split else N)
    k_unit = Cb if (b_split and mode == "nt") else K
    tm, tn, tk = _mm_tiles(M, N, K, n_unit, k_unit, a.dtype.itemsize, n_ex + n_out, mode == "tn")
    nk = K // tk
    a_spec = pl.BlockSpec((tk, tm), lambda i, j, k: (k, i)) if mode == "tn" else pl.BlockSpec((tm, tk), lambda i, j, k: (i, k))
    if b_split and mode == "nn":
        nb = Cb // tn
        b_spec = pl.BlockSpec((None, tk, tn), lambda i, j, k: (lo + j // nb, k, j % nb))
    elif b_split:
        nb = Cb // tk
        b_spec = pl.BlockSpec((None, tn, tk), lambda i, j, k: (lo + k // nb, j, k % nb))
    else:
        b_spec = pl.BlockSpec((tn, tk), lambda i, j, k: (j, k)) if mode == "nt" else pl.BlockSpec((tk, tn), lambda i, j, k: (k, j))
    mn_spec = pl.BlockSpec((tm, tn), lambda i, j, k: (i, j))
    out_shapes = [jax.ShapeDtypeStruct((M, N), dt) for dt in out_dtypes]
    out_specs = [mn_spec] * n_out
    if out_split:
        assert n_ex == 0 and n_out == 1
        nbo = (N // out_split) // tn
        out_specs = [pl.BlockSpec((None, tm, tn), lambda i, j, k: (j // nbo, i, j % nbo))]
        out_shapes = [jax.ShapeDtypeStruct((out_split, M, N // out_split), out_dtypes[0])]
    dims = {"nn": (((1,), (0,)), ((), ())), "nt": (((1,), (1,)), ((), ())), "tn": (((0,), (0,)), ((), ()))}[mode]

    def body(*refs):
        a_ref, b_ref = refs[0], refs[1]
        ex_refs = refs[2:2 + n_ex]
        out_refs = refs[2 + n_ex:2 + n_ex + n_out]
        part = lax.dot_general(a_ref[...].astype(BF16), b_ref[...].astype(BF16), dims, preferred_element_type=F32)

        def finish(acc):
            outs = (acc,) if epi is None else epi(acc, *[r[...] for r in ex_refs])
            for r, o in zip(out_refs, outs):
                r[...] = o.astype(r.dtype)

        if nk == 1:
            finish(part)
            return
        acc_ref = refs[-1]
        k = pl.program_id(2)

        @pl.when(k == 0)
        def _():
            acc_ref[...] = part

        @pl.when(jnp.logical_and(k > 0, k < nk - 1))
        def _():
            acc_ref[...] += part

        @pl.when(k == nk - 1)
        def _():
            finish(acc_ref[...] + part)

    outs, comm_outs = _call(
        body, name=name, grid=(M // tm, N // tn, nk),
        in_specs=[a_spec, b_spec] + [mn_spec] * n_ex,
        out_specs=out_specs,
        out_shape=out_shapes,
        scratch_shapes=[pltpu.VMEM((tm, tn), F32)] if nk > 1 else [],
        semantics=("parallel", "parallel", "arbitrary"), args=(a, b, *extras), comm=comm)
    res = outs[0] if n_out == 1 else outs
    return res if comm is None else (res, comm_outs)


def _tiled(fn, *, T, C, ins, out_dtypes=(), acc_rows=(), tb=None, cb=512, name):
    tb = _tile(T, tb or (512 if cb <= 1024 else 256), HALO)
    nI, nJ = T // tb, C // cb
    hb, nH = tb // HALO, T // HALO
    specs, args, kinds = [], [], []
    for kind, arr, cmap in ins:
        cm = cmap if cmap is not None else (lambda j: j)
        kinds.append(kind)
        if kind == "cur":
            specs.append(pl.BlockSpec((tb, cb), lambda j, i, cm=cm: (i, cm(j))))
            args.append(arr)
        elif kind == "ext":
            specs.append(pl.BlockSpec((HALO, cb), lambda j, i, cm=cm: (jnp.maximum(i * hb - 1, 0), cm(j))))
            specs.append(pl.BlockSpec((tb, cb), lambda j, i, cm=cm: (i, cm(j))))
            specs.append(pl.BlockSpec((HALO, cb), lambda j, i, cm=cm: (jnp.minimum((i + 1) * hb, nH - 1), cm(j))))
            args += [arr, arr, arr]
        elif kind == "row":
            specs.append(pl.BlockSpec((arr.shape[0], cb), lambda j, i, cm=cm: (0, cm(j))))
            args.append(arr)
        elif kind == "stack":
            specs.append(pl.BlockSpec((arr.shape[0], tb, cb), lambda j, i, cm=cm: (0, i, cm(j))))
            args.append(arr)
        else:
            raise ValueError(kind)
    n_in = len(args)
    n_out, n_acc = len(out_dtypes), len(acc_rows)

    def body(*refs):
        j, i = pl.program_id(0), pl.program_id(1)
        vals, r = [], 0
        for kind in kinds:
            if kind == "ext":
                prev = jnp.where(i == 0, 0.0, refs[r][...].astype(F32))
                cur = refs[r + 1][...].astype(F32)
                nxt = jnp.where(i == nI - 1, 0.0, refs[r + 2][...].astype(F32))
                vals.append(jnp.concatenate([prev, cur, nxt], axis=0))
                r += 3
            else:
                vals.append(refs[r][...])
                r += 1
        res = fn(j, i, *vals)
        for ref, o in zip(refs[n_in:n_in + n_out], res[:n_out]):
            ref[...] = o.astype(ref.dtype)
        for ref, o in zip(refs[n_in + n_out:], res[n_out:]):
            @pl.when(i == 0)
            def _(ref=ref, o=o):
                ref[...] = o

            @pl.when(i > 0)
            def _(ref=ref, o=o):
                ref[...] += o

    outs = pl.pallas_call(
        body, name=name, grid=(nJ, nI), in_specs=specs,
        out_specs=[pl.BlockSpec((tb, cb), lambda j, i: (i, j))] * n_out
        + [pl.BlockSpec((rows, cb), lambda j, i: (0, j)) for rows in acc_rows],
        out_shape=[jax.ShapeDtypeStruct((T, C), dt) for dt in out_dtypes]
        + [jax.ShapeDtypeStruct((rows, C), F32) for rows in acc_rows],
        compiler_params=pltpu.CompilerParams(dimension_semantics=("parallel", "arbitrary"),
                                             vmem_limit_bytes=VMEM_LIMIT),
    )(*args)
    return outs


def _conv_causal(xe, w):
    K = w.shape[0]
    y = xe * w[K - 1:K]
    for j in range(K - 1):
        y = y + pltpu.roll(xe, K - 1 - j, 0) * w[j:j + 1]
    return y


def _conv_anti(de, w):
    K, n = w.shape[0], de.shape[0]
    y = de * w[K - 1:K]
    for j in range(K - 1):
        y = y + pltpu.roll(de, n - (K - 1 - j), 0) * w[j:j + 1]
    return y


def _conv_dw(dce, xe, K):
    n = dce.shape[0]
    tb = n - 2 * HALO
    rows = []
    for j in range(K):
        xs = xe if j == K - 1 else pltpu.roll(xe, K - 1 - j, 0)
        rows.append(jnp.sum((dce * xs)[HALO:HALO + tb], axis=0, keepdims=True))
    rows.append(jnp.zeros((HALO - K, dce.shape[1]), F32))
    return jnp.concatenate(rows, axis=0)


def _own(xe):
    return xe[HALO:xe.shape[0] - HALO]


def _row0(v):
    return jnp.concatenate([v, jnp.zeros((HALO - 1, v.shape[1]), F32)], axis=0)


def _per_head(fn, *xs):
    n = xs[0].shape[1] // HEAD_DIM
    outs = [fn(*[x[:, g * HEAD_DIM:(g + 1) * HEAD_DIM] for x in xs]) for g in range(n)]
    return outs[0] if n == 1 else jnp.concatenate(outs, axis=1)


def _rms_fwd(x, g, name):
    T, D = x.shape

    def fn(j, i, xv, gv):
        r = lax.rsqrt(jnp.mean(xv * xv, axis=1, keepdims=True) + EPS)
        return (xv * r * gv,)

    return _tiled(fn, T=T, C=D, ins=[("cur", x, None), ("row", g, None)], out_dtypes=[BF16], cb=D, name=name)[0]


def _rms_bwd_math(dy, xv, gv):
    r = lax.rsqrt(jnp.mean(xv * xv, axis=1, keepdims=True) + EPS)
    xh = xv * r
    dxh = dy * gv
    dx = r * (dxh - xh * jnp.mean(dxh * xh, axis=1, keepdims=True))
    dg = jnp.sum(dy * xh, axis=0, keepdims=True)
    return dx, dg


def _rms_bwd(dh, x, g, dres, name):
    T, D = x.shape

    def fn(j, i, dhv, xv, gv, dr):
        dx, dg = _rms_bwd_math(dhv, xv, gv)
        return dr + dx, dr + dx, _row0(dg)

    return _tiled(fn, T=T, C=D, ins=[("cur", dh, None), ("cur", x, None), ("row", g, None), ("cur", dres, None)],
                  out_dtypes=[F32, BF16], acc_rows=[HALO], cb=D, name=name)


def _final_fb(x3, tgt, g, pp, pg):
    T, D = x3.shape

    def fn(j, i, xv, tv, gv, ppv, pgv):
        r = lax.rsqrt(jnp.mean(xv * xv, axis=1, keepdims=True) + EPS)
        xh = xv * r
        e = xh * gv - tv
        dy = e * (1.0 / D)
        dxh = dy * gv
        dx = r * (dxh - xh * jnp.mean(dxh * xh, axis=1, keepdims=True))
        dg = jnp.sum(dy * xh, axis=0, keepdims=True)
        ls = jnp.sum(e * e, axis=0, keepdims=True) * (0.5 / D)
        return (dx, dx * ppv * pgv * (1.0 - pgv), dx * pgv,
                jnp.concatenate([dg, ls, jnp.zeros((HALO - 2, D), F32)], axis=0))

    return _tiled(fn, T=T, C=D, ins=[("cur", x3, None), ("cur", tgt, None), ("row", g, None), ("cur", pp, None),
                                      ("cur", pg, None)], out_dtypes=[F32, BF16, BF16], acc_rows=[HALO], cb=D, name="final_fb")


def _ga_fwd(proj, w_a, CW, cb):
    T = proj.shape[0]
    n = CW // cb

    def fn(j, i, ax, ab, ac, w):
        c = _conv_causal(ac * ax, w)
        return (ab * _own(c),)

    return _tiled(fn, T=T, C=CW, ins=[("ext", proj, None), ("cur", proj, lambda j: j + n), ("ext", proj, lambda j: j + 2 * n),
                                       ("row", w_a, None)], out_dtypes=[BF16], cb=cb, name="ga_fwd")[0]


def _ga_bwd(dymix, proj, w_a, CW, cb):
    T = proj.shape[0]
    n = CW // cb
    K = w_a.shape[0]

    def fn(j, i, dy, ax, ab, ac, w):
        u = ac * ax
        c = _conv_causal(u, w)
        dc = dy * ab
        du = _conv_anti(dc, w)
        return _own(du * ac), _own(dy * c), _own(du * ax), _conv_dw(dc, u, K)

    return _tiled(fn, T=T, C=CW, ins=[("ext", dymix, None), ("ext", proj, None), ("ext", proj, lambda j: j + n),
                                       ("ext", proj, lambda j: j + 2 * n), ("row", w_a, None)],
                  out_dtypes=[BF16, BF16, BF16], acc_rows=[HALO], cb=cb, name="ga_bwd")


def _l2n(s):
    return s * lax.rsqrt(jnp.sum(s * s, axis=1, keepdims=True) + EPS)


def _qkv_fwd(proj, w_sec, coff, normalize, DNW, cb, name):
    T = proj.shape[0]

    def fn(j, i, pre, w):
        c = _own(_conv_causal(pre, w))
        s = c * _sigmoid(c)
        return (_per_head(_l2n, s) if normalize else s,)

    return _tiled(fn, T=T, C=DNW, ins=[("ext", proj, lambda j: j + coff), ("row", w_sec, None)],
                  out_dtypes=[F32], cb=cb, name=name)[0]


def _qkv_bwd(dsec, proj, w_sec, coff, normalize, DNW, cb, name):
    T = proj.shape[0]
    K = w_sec.shape[0]

    def l2n_bwd(s, dn):
        r = lax.rsqrt(jnp.sum(s * s, axis=1, keepdims=True) + EPS)
        nrm = s * r
        return r * (dn - nrm * jnp.sum(dn * nrm, axis=1, keepdims=True))

    def fn(j, i, dn, pre, w):
        c = _conv_causal(pre, w)
        sg = _sigmoid(c)
        s = c * sg
        ds = _per_head(l2n_bwd, s, dn) if normalize else dn
        dc = ds * (sg * (1.0 + c * (1.0 - sg)))
        return _own(_conv_anti(dc, w)), _conv_dw(dc, pre, K)

    return _tiled(fn, T=T, C=DNW, ins=[("ext", dsec, None), ("ext", proj, lambda j: j + coff), ("row", w_sec, None)],
                  out_dtypes=[BF16], acc_rows=[HALO], cb=cb, name=name)


def _gb_fwd(small, a_log_row, dt_row, H):
    T = small.shape[0]

    def fn(j, i, sm, al, dt):
        z = sm + dt
        sp = jnp.maximum(z, 0.0) + jnp.log(1.0 + jnp.exp(-jnp.abs(z)))
        g = -jnp.exp(al) * sp
        beta = _sigmoid(pltpu.roll(sm, LANES - H, 1))
        return g, beta

    return _tiled(fn, T=T, C=LANES, ins=[("cur", small, None), ("row", a_log_row, None), ("row", dt_row, None)],
                  out_dtypes=[F32, F32], cb=LANES, name="gb_fwd")


def _gb_bwd(dgB, dbB, small, g, beta, a_log_row, dt_row, H):
    T = small.shape[0]

    def fn(j, i, dgv, dbv, sm, gv, bv, al, dt):
        lane = lax.broadcasted_iota(jnp.int32, sm.shape, 1)
        dg = jnp.zeros(sm.shape, F32)
        db = jnp.zeros(sm.shape, F32)
        for h in range(H):
            dg = jnp.where(lane == h, jnp.sum(dgv[h], axis=1, keepdims=True), dg)
            db = jnp.where(lane == h, jnp.sum(dbv[h], axis=1, keepdims=True), db)
        da = dg * (-jnp.exp(al)) * _sigmoid(sm + dt)
        dbb = db * bv * (1.0 - bv)
        dsm = jnp.where(lane < H, da, 0.0) + pltpu.roll(jnp.where(lane < H, dbb, 0.0), H, 1)
        d_alog = jnp.sum(jnp.where(lane < H, dg * gv, 0.0), axis=0, keepdims=True)
        d_dt = jnp.sum(jnp.where(lane < H, da, 0.0), axis=0, keepdims=True)
        return dsm, jnp.concatenate([d_alog, d_dt, jnp.zeros((HALO - 2, LANES), F32)], axis=0)

    return _tiled(fn, T=T, C=LANES, ins=[("stack", dgB, None), ("stack", dbB, None), ("cur", small, None), ("cur", g, None),
                                          ("cur", beta, None), ("row", a_log_row, None), ("row", dt_row, None)],
                  out_dtypes=[BF16], acc_rows=[HALO], cb=LANES, name="gb_bwd")


_DIMS = {"nn": (((1,), (0,)), ((), ())), "nt": (((1,), (1,)), ((), ())), "tn": (((0,), (0,)), ((), ()))}
_DOT_BWD = {"nn": (("nt", "gb"), ("tn", "ag")), "nt": (("nn", "gb"), ("tn", "ga")), "tn": (("nt", "bg"), ("nn", "ag"))}


def _split(a):
    hi = a.astype(BF16)
    return hi, (a - hi.astype(F32)).astype(BF16)


def _raw_dot(a, b, kind, passes):
    dg = lambda x, y: lax.dot_general(x, y, _DIMS[kind], preferred_element_type=F32)
    if passes == 1:
        return dg(a.astype(BF16), b.astype(BF16))
    ah, al = _split(a)
    bh, bl = _split(b)
    if kind == "tn":
        return dg(ah, bh) + (dg(ah, bl) + dg(al, bh))
    m = a.shape[0]
    top = dg(jnp.concatenate([ah, al], axis=0), bh)
    return top[:m] + (dg(ah, bl) + top[m:])


def _raw_dot_exact(a, b, kind, exact):
    dg = lambda x, y: lax.dot_general(x, y, _DIMS[kind], preferred_element_type=F32)
    if exact == "a":
        bh, bl = _split(b)
        return dg(a.astype(BF16), bh) + dg(a.astype(BF16), bl)
    ah, al = _split(a)
    return dg(ah, b.astype(BF16)) + dg(al, b.astype(BF16))


@functools.lru_cache(maxsize=None)
def _dotc(kind):
    @jax.custom_vjp
    def f(a, b):
        return _raw_dot_exact(a, b, kind, "a")

    def fwd(a, b):
        return _raw_dot_exact(a, b, kind, "a"), a

    def bwd(a, g):
        db = _raw_dot_exact(a, g, "tn", "a") if kind == "nn" else _raw_dot_exact(g, a, "tn", "b")
        return jnp.zeros_like(a), db

    f.defvjp(fwd, bwd)
    return f


@functools.lru_cache(maxsize=None)
def _dotf(kind, passes):
    @jax.custom_vjp
    def f(a, b):
        return _raw_dot(a, b, kind, passes)

    def fwd(a, b):
        return _raw_dot(a, b, kind, passes), (a, b)

    def bwd(res, g):
        ops = {"a": res[0], "b": res[1], "g": g}
        (ka, oa), (kb, ob) = _DOT_BWD[kind]
        return (_raw_dot(ops[oa[0]], ops[oa[1]], ka, passes), _raw_dot(ops[ob[0]], ops[ob[1]], kb, passes))

    f.defvjp(fwd, bwd)
    return f


@jax.custom_vjp
def _saved_inverse(L, inv):
    return inv


def _saved_inverse_fwd(L, inv):
    return inv, inv


def _saved_inverse_bwd(inv, g):
    d3nt, d3tn = _dotf("nt", 3), _dotf("tn", 3)
    return -d3nt(d3tn(inv, g), inv), jnp.zeros_like(inv)


_saved_inverse.defvjp(_saved_inverse_fwd, _saved_inverse_bwd)


def _chunk_fn(q, k, v, gB, bB, S, inv_saved=None):
    C = CHUNK
    d3 = _dotf("nn", 3)
    d1, d1nt, d1tn = _dotf("nn", 1), _dotf("nt", 1), _dotf("tn", 1)
    each = lambda f, *ls: tuple(f(*xs) for xs in zip(*ls))
    row = lax.broadcasted_iota(jnp.int32, (C, C), 0)
    col = lax.broadcasted_iota(jnp.int32, (C, C), 1)
    causal = row >= col
    strict = row > col
    tril = jnp.where(causal, 1.0, 0.0).astype(F32)
    eye = jnp.where(row == col, 1.0, 0.0).astype(F32)
    avg = jnp.full((C, HEAD_DIM), 1.0 / HEAD_DIM, F32)
    gc = each(lambda g: _dotc("nn")(tril, g), gB)
    R = each(lambda g: _dotc("nt")(avg, g), gc)
    decay = each(lambda g, r: jnp.where(causal, jnp.exp(jnp.where(causal, g[:, :C] - r, 0.0)), 0.0), gc, R)
    kk = each(lambda x: d1nt(x, x), k)
    L = each(lambda a, d, b: jnp.where(strict, a * d * b[:, :C], 0.0), kk, decay, bB)
    if inv_saved is None:
        inv = each(lambda l: eye - l, L)
        P = L
        for _ in range(5):
            P = each(lambda p: d3(p, p), P)
            inv = each(lambda a, p: d3(a, eye + p), inv, P)
    else:
        inv = each(_saved_inverse, L, inv_saved)
    eg = each(jnp.exp, gc)
    u = each(lambda a, x, b: d3(a, x * b), inv, v, bB)
    w = each(lambda a, x, b, e: d3(a, x * b * e), inv, k, bB, eg)
    qs = each(lambda x: x * (HEAD_DIM ** -0.5), q)
    qk = each(lambda a, x, d: d1nt(a, x) * d, qs, k, decay)
    gl = each(lambda g: g[C - 1:C, :], gc)
    v_new = each(lambda a, b, s: a - d1(b, s), u, w, S)
    o1 = each(lambda a, e, s: d1(a * e, s), qs, eg, S)
    o = each(lambda a, b, c: a + d1(b, c), o1, qk, v_new)
    kv = each(lambda x, a, g, vn: d1tn(x * jnp.exp(a - g), vn), k, gl, gc, v_new)
    S_new = each(lambda s, a, b: s * jnp.exp(a) + b, S, gl, kv)
    return (o, S_new), inv


def _sel_lane(x, h):
    lane = lax.broadcasted_iota(jnp.int32, x.shape, 1)
    return jnp.broadcast_to(jnp.sum(jnp.where(lane == h, x, 0.0), axis=1, keepdims=True), x.shape)


def _head(ref, h):
    return ref[:, h * HEAD_DIM:(h + 1) * HEAD_DIM]


def _delta_fwd(q, k, v, g, beta, comm=None):
    T = q.shape[0]
    H, N = q.shape[1] // HEAD_DIM, T // CHUNK

    def body(q_ref, k_ref, v_ref, g_ref, b_ref, o_ref, s_ref, inv_ref, S):
        @pl.when(pl.program_id(0) == 0)
        def _():
            S[...] = jnp.zeros_like(S)

        gv, bv = g_ref[...], b_ref[...]
        heads = lambda f: tuple(f(h) for h in range(H))
        S_in = heads(lambda h: S[h])
        for h in range(H):
            s_ref[h, 0] = S_in[h]
        (o, S_new), inv = _chunk_fn(heads(lambda h: _head(q_ref, h)), heads(lambda h: _head(k_ref, h)),
                                    heads(lambda h: _head(v_ref, h)), heads(lambda h: _sel_lane(gv, h)),
                                    heads(lambda h: _sel_lane(bv, h)), S_in)
        for h in range(H):
            o_ref[:, h * HEAD_DIM:(h + 1) * HEAD_DIM] = o[h]
            inv_ref[h, 0] = inv[h]
            S[h] = S_new[h]

    blk = pl.BlockSpec((CHUNK, H * HEAD_DIM), lambda n: (n, 0))
    gblk = pl.BlockSpec((CHUNK, LANES), lambda n: (n, 0))
    outs, comm_outs = _call(
        body, name="delta_fwd", grid=(N,), in_specs=[blk, blk, blk, gblk, gblk],
        out_specs=[blk, pl.BlockSpec((H, 1, HEAD_DIM, HEAD_DIM), lambda n: (0, n, 0, 0)),
                   pl.BlockSpec((H, 1, CHUNK, CHUNK), lambda n: (0, n, 0, 0))],
        out_shape=[jax.ShapeDtypeStruct((T, H * HEAD_DIM), F32), jax.ShapeDtypeStruct((H, N, HEAD_DIM, HEAD_DIM), F32),
                   jax.ShapeDtypeStruct((H, N, CHUNK, CHUNK), F32)],
        scratch_shapes=[pltpu.VMEM((H, HEAD_DIM, HEAD_DIM), F32)],
        semantics=("arbitrary",), args=(q, k, v, g, beta), comm=comm)
    return outs[0], outs[1], outs[2], comm_outs


def _delta_bwd(q, k, v, g, beta, S0, inv, do, comm=None):
    T = q.shape[0]
    H, N = q.shape[1] // HEAD_DIM, T // CHUNK

    def body(q_ref, k_ref, v_ref, g_ref, b_ref, s_ref, inv_ref, do_ref, dq_ref, dk_ref, dv_ref, dg_ref, db_ref, dS):
        @pl.when(pl.program_id(0) == 0)
        def _():
            dS[...] = jnp.zeros_like(dS)

        gv, bv = g_ref[...], b_ref[...]
        heads = lambda f: tuple(f(h) for h in range(H))
        _, vjp, _ = jax.vjp(_chunk_fn, heads(lambda h: _head(q_ref, h)), heads(lambda h: _head(k_ref, h)),
                            heads(lambda h: _head(v_ref, h)), heads(lambda h: _sel_lane(gv, h)), heads(lambda h: _sel_lane(bv, h)),
                            heads(lambda h: s_ref[h, 0]), heads(lambda h: inv_ref[h, 0]), has_aux=True)
        dq, dk, dv, dgB, dbB, dS_prev, _ = vjp((heads(lambda h: _head(do_ref, h)), heads(lambda h: dS[h])))
        for h in range(H):
            sl = slice(h * HEAD_DIM, (h + 1) * HEAD_DIM)
            dq_ref[:, sl] = dq[h]
            dk_ref[:, sl] = dk[h]
            dv_ref[:, sl] = dv[h]
            dg_ref[h] = dgB[h]
            db_ref[h] = dbB[h]
            dS[h] = dS_prev[h]

    blk = pl.BlockSpec((CHUNK, H * HEAD_DIM), lambda n: (N - 1 - n, 0))
    gblk = pl.BlockSpec((CHUNK, LANES), lambda n: (N - 1 - n, 0))
    hblk = pl.BlockSpec((H, CHUNK, LANES), lambda n: (0, N - 1 - n, 0))
    sd = jax.ShapeDtypeStruct
    outs, comm_outs = _call(
        body, name="delta_bwd", grid=(N,),
        in_specs=[blk, blk, blk, gblk, gblk, pl.BlockSpec((H, 1, HEAD_DIM, HEAD_DIM), lambda n: (0, N - 1 - n, 0, 0)),
                  pl.BlockSpec((H, 1, CHUNK, CHUNK), lambda n: (0, N - 1 - n, 0, 0)), blk],
        out_specs=[blk, blk, blk, hblk, hblk],
        out_shape=[sd((T, H * HEAD_DIM), F32)] * 3 + [sd((H, T, LANES), F32)] * 2,
        scratch_shapes=[pltpu.VMEM((H, HEAD_DIM, HEAD_DIM), F32)],
        semantics=("arbitrary",), args=(q, k, v, g, beta, S0, inv, do), comm=comm)
    return (*outs, comm_outs)


def _gnorm_fwd(o, proj, z_coff, gdn_t, DNW):
    T = o.shape[0]

    def fn(j, i, ov, zv, gv):
        def one(oh, zh, gh):
            r = lax.rsqrt(jnp.mean(oh * oh, axis=1, keepdims=True) + EPS)
            return oh * r * gh * (zh * _sigmoid(zh))
        return (_per_head(one, ov, zv, jnp.broadcast_to(gv, ov.shape)),)

    return _tiled(fn, T=T, C=DNW, ins=[("cur", o, None), ("cur", proj, lambda j: j + z_coff), ("row", gdn_t, None)],
                  out_dtypes=[BF16], cb=DNW, name="gnorm_fwd")[0]


def _gnorm_bwd(dymix, y_coff, o, proj, z_coff, gdn_t, DNW):
    T = o.shape[0]
    nh = DNW // HEAD_DIM

    def fn(j, i, dy, ov, zv, gv):
        dos, dzs, dgs = [], [], jnp.zeros((1, HEAD_DIM), F32)
        for h in range(nh):
            sl = slice(h * HEAD_DIM, (h + 1) * HEAD_DIM)
            dyh, oh, zh, gh = dy[:, sl].astype(F32), ov[:, sl], zv[:, sl], gv[:, sl]
            r = lax.rsqrt(jnp.mean(oh * oh, axis=1, keepdims=True) + EPS)
            on = oh * r
            sg = _sigmoid(zh)
            sz = zh * sg
            dzs.append(dyh * on * gh * (sg * (1.0 + zh * (1.0 - sg))))
            don = dyh * gh * sz
            dos.append(r * (don - on * jnp.mean(don * on, axis=1, keepdims=True)))
            dgs = dgs + jnp.sum(dyh * on * sz, axis=0, keepdims=True)
        cat = (lambda xs: xs[0] if nh == 1 else jnp.concatenate(xs, axis=1))
        return cat(dos), cat(dzs), _row0(dgs)

    T_ = T
    nI = T_ // _tile(T_, 256, HALO)
    tb = T_ // nI
    specs_cb = DNW

    def body_wrap():
        def body(dy_ref, o_ref, z_ref, g_ref, do_ref, dz_ref, dg_ref):
            i = pl.program_id(0)
            d_o, d_z, d_g = fn(0, i, dy_ref[...], o_ref[...], z_ref[...], g_ref[...])
            do_ref[...] = d_o
            dz_ref[...] = d_z.astype(dz_ref.dtype)

            @pl.when(i == 0)
            def _():
                dg_ref[...] = d_g

            @pl.when(i > 0)
            def _():
                dg_ref[...] += d_g

        return pl.pallas_call(
            body, name="gnorm_bwd", grid=(nI,),
            in_specs=[pl.BlockSpec((tb, specs_cb), lambda i: (i, y_coff)), pl.BlockSpec((tb, specs_cb), lambda i: (i, 0)),
                      pl.BlockSpec((tb, specs_cb), lambda i: (i, z_coff)), pl.BlockSpec((1, specs_cb), lambda i: (0, 0))],
            out_specs=[pl.BlockSpec((tb, specs_cb), lambda i: (i, 0)), pl.BlockSpec((tb, specs_cb), lambda i: (i, 0)),
                       pl.BlockSpec((HALO, HEAD_DIM), lambda i: (0, 0))],
            out_shape=[jax.ShapeDtypeStruct((T_, DNW), F32), jax.ShapeDtypeStruct((T_, DNW), BF16),
                       jax.ShapeDtypeStruct((HALO, HEAD_DIM), F32)],
            compiler_params=pltpu.CompilerParams(dimension_semantics=("arbitrary",), vmem_limit_bytes=VMEM_LIMIT),
        )(dymix, o, proj, gdn_t)

    return body_wrap()


def _ffn_fwd(up_g, up_v, w_g, w_v, cb):
    T, F = up_g.shape

    def fn(j, i, ug, uv, wg, wv):
        cg = _own(_conv_causal(ug, wg))
        cv = _own(_conv_causal(uv, wv))
        return (cg * _sigmoid(cg) * cv,)

    return _tiled(fn, T=T, C=F, ins=[("ext", up_g, None), ("ext", up_v, None), ("row", w_g, None), ("row", w_v, None)],
                  out_dtypes=[BF16], tb=1024, cb=cb, name="ffn_fwd")[0]


def _ffn_bwd(dact, up_g, up_v, w_g, w_v, cb):
    T, F = up_g.shape
    K = w_g.shape[0]

    def fn(j, i, da, ug, uv, wg, wv):
        cg = _conv_causal(ug, wg)
        cv = _conv_causal(uv, wv)
        sg = _sigmoid(cg)
        dgate = da * cv * (sg * (1.0 + cg * (1.0 - sg)))
        dval = da * (cg * sg)
        return (_own(_conv_anti(dgate, wg)), _own(_conv_anti(dval, wv)), _conv_dw(dgate, ug, K), _conv_dw(dval, uv, K))

    return _tiled(fn, T=T, C=F, ins=[("ext", dact, None), ("ext", up_g, None), ("ext", up_v, None), ("row", w_g, None),
                                      ("row", w_v, None)], out_dtypes=[BF16, BF16], acc_rows=[HALO, HALO], tb=1024, cb=cb,
                  name="ffn_bwd")


def _wide(R, Cc, n_f32, unit=HALO):
    cb = Cc if (Cc % LANES or Cc <= 4096) else _tile(Cc, 2048, LANES)
    cap = max(unit, EW_VMEM_BUDGET // (2 * 4 * n_f32 * cb) // unit * unit)
    return _tile(R, cap, unit), cb


def _adamw(w, g, m, v, name):
    R, Cc = w.shape
    tb, cb = _wide(R, Cc, 7)
    c1 = 1.0 / (1.0 - ADAM_B1 ** ADAM_STEP)
    c2 = 1.0 / (1.0 - ADAM_B2 ** ADAM_STEP)

    def fn(j, i, wv, gv, mv, vv):
        m2 = ADAM_B1 * mv + (1.0 - ADAM_B1) * gv
        v2 = ADAM_B2 * vv + (1.0 - ADAM_B2) * (gv * gv)
        delta = -ADAM_LR * ((m2 * c1) / (jnp.sqrt(v2 * c2) + ADAM_EPS) + ADAM_WD * wv)
        return delta, m2, v2

    return _tiled(fn, T=R, C=Cc, ins=[("cur", w, None), ("cur", g, None), ("cur", m, None), ("cur", v, None)],
                  out_dtypes=[F32, F32, F32], tb=tb, cb=cb, name=name)


def _join_shards(w4, n_main):
    S4, R, cs = w4.shape
    n_small = S4 * cs - n_main
    assert 0 < n_small <= LANES and n_small <= cs
    tb = _tile(R, 256, 2 * HALO)

    def body(w_ref, main_ref, small_ref):
        for t in range(S4 - 1):
            main_ref[:, t * cs:(t + 1) * cs] = w_ref[t]
        last = w_ref[S4 - 1]
        main_ref[:, (S4 - 1) * cs:] = last[:, :cs - n_small]
        small_ref[...] = jnp.zeros_like(small_ref)
        small_ref[:, :n_small] = last[:, cs - n_small:]

    return pl.pallas_call(
        body, name="join_w_in", grid=(R // tb,), in_specs=[pl.BlockSpec((S4, tb, cs), lambda i: (0, i, 0))],
        out_specs=[pl.BlockSpec((tb, n_main), lambda i: (i, 0)), pl.BlockSpec((tb, LANES), lambda i: (i, 0))],
        out_shape=[jax.ShapeDtypeStruct((R, n_main), w4.dtype), jax.ShapeDtypeStruct((R, LANES), w4.dtype)],
        compiler_params=pltpu.CompilerParams(dimension_semantics=("parallel",), vmem_limit_bytes=VMEM_LIMIT))(w4)


def _split_shards(main, small, cs):
    R, n_main = main.shape
    n_small = 4 * cs - n_main
    tb = _tile(R, 256, HALO)

    def body(main_ref, small_ref, out_ref):
        for t in range(3):
            out_ref[t] = main_ref[:, t * cs:(t + 1) * cs]
        out_ref[3, :, :cs - n_small] = main_ref[:, 3 * cs:]
        out_ref[3, :, cs - n_small:] = small_ref[:, :n_small]

    return pl.pallas_call(
        body, name="split_g_in", grid=(R // tb,),
        in_specs=[pl.BlockSpec((tb, n_main), lambda i: (i, 0)), pl.BlockSpec((tb, LANES), lambda i: (i, 0))],
        out_specs=pl.BlockSpec((4, tb, cs), lambda i: (0, i, 0)), out_shape=jax.ShapeDtypeStruct((4, R, cs), main.dtype),
        compiler_params=pltpu.CompilerParams(dimension_semantics=("parallel",), vmem_limit_bytes=VMEM_LIMIT))(main, small)


def _sum_stack(st, name):
    S, R, Cc = st.shape
    cb = _tile(Cc, 512, LANES) if Cc % LANES == 0 else Cc

    def fn(j, i, sv):
        t = sv[0]
        for s in range(1, S):
            t = t + sv[s]
        return (t,)

    return _tiled(fn, T=R, C=Cc, ins=[("stack", st, None)], out_dtypes=[F32], cb=cb, name=name)[0]


ANY = pl.BlockSpec(memory_space=pl.ANY)


def _place():
    x, y, c = lax.axis_index("x"), lax.axis_index("y"), lax.axis_index("c")
    return x, y, c, 2 * x + y


def _chip_dev(s, c):
    return (s // 2, s % 2, c)


class _Comm:
    def __init__(self, ins, out_shapes, sems, start, wait, aliases=None):
        self.ins, self.out_shapes, self.sems = list(ins), list(out_shapes), list(sems)
        self.start, self.wait, self.aliases = start, wait, dict(aliases or {})


def _merge(*comms):
    offs, i, o, s = [], 0, 0, 0
    for cm in comms:
        offs.append((i, o, s))
        i, o, s = i + len(cm.ins), o + len(cm.out_shapes), s + len(cm.sems)

    def part(refs, k, cm):
        i0, o0, s0 = offs[k]
        return refs[0][i0:i0 + len(cm.ins)], refs[1][o0:o0 + len(cm.out_shapes)], refs[2][s0:s0 + len(cm.sems)]

    def start(*refs):
        for k, cm in enumerate(comms):
            cm.start(*part(refs, k, cm))

    def wait(*refs):
        for k, cm in enumerate(comms):
            cm.wait(*part(refs, k, cm))

    aliases = {}
    for k, cm in enumerate(comms):
        for a, b in cm.aliases.items():
            aliases[offs[k][0] + a] = offs[k][1] + b
    return _Comm([a for cm in comms for a in cm.ins], [a for cm in comms for a in cm.out_shapes],
                 [a for cm in comms for a in cm.sems], start, wait, aliases)


def _call(body, *, name, grid, in_specs, out_specs, out_shape, scratch_shapes, semantics, args, comm=None):
    if comm is None:
        outs = pl.pallas_call(
            body, name=name, grid=grid, in_specs=in_specs, out_specs=out_specs, out_shape=out_shape,
            scratch_shapes=list(scratch_shapes),
            compiler_params=pltpu.CompilerParams(dimension_semantics=semantics, vmem_limit_bytes=VMEM_LIMIT))(*args)
        return list(outs), []
    n_in, n_out, n_scr = len(in_specs), len(out_specs), len(scratch_shapes)
    ci, co = len(comm.ins), len(comm.out_shapes)

    def wrapped(*refs):
        r = 0
        ins, r = refs[r:r + n_in], r + n_in
        cins, r = refs[r:r + ci], r + ci
        outs, r = refs[r:r + n_out], r + n_out
        couts, r = refs[r:r + co], r + co
        scr, r = refs[r:r + n_scr], r + n_scr
        csems = refs[r:]
        ids = [pl.program_id(a) for a in range(len(grid))]
        first, last = ids[0] == 0, ids[0] == grid[0] - 1
        for a in range(1, len(grid)):
            first = jnp.logical_and(first, ids[a] == 0)
            last = jnp.logical_and(last, ids[a] == grid[a] - 1)

        @pl.when(first)
        def _():
            comm.start(cins, couts, csems)

        body(*ins, *outs, *scr)

        @pl.when(last)
        def _():
            comm.wait(cins, couts, csems)

    outs = pl.pallas_call(
        wrapped, name=name, grid=grid, in_specs=list(in_specs) + [ANY] * ci, out_specs=list(out_specs) + [ANY] * co,
        out_shape=list(out_shape) + comm.out_shapes, scratch_shapes=list(scratch_shapes) + comm.sems,
        input_output_aliases={n_in + a: n_out + b for a, b in comm.aliases.items()},
        compiler_params=pltpu.CompilerParams(dimension_semantics=("arbitrary",) * len(grid), vmem_limit_bytes=VMEM_LIMIT),
    )(*args, *comm.ins)
    return list(outs[:n_out]), list(outs[n_out:])


def _run_comm(comm, name):
    ci, co = len(comm.ins), len(comm.out_shapes)

    def body(*refs):
        cins, couts, csems = refs[:ci], refs[ci:ci + co], refs[ci + co:]
        comm.start(cins, couts, csems)
        comm.wait(cins, couts, csems)

    outs = pl.pallas_call(body, name=name, in_specs=[ANY] * ci, out_specs=[ANY] * co, out_shape=comm.out_shapes,
                          scratch_shapes=comm.sems, input_output_aliases=comm.aliases)(*comm.ins)
    return list(outs)


def _ag_comm(shard, land=None, q=0, nq=1):
    two, R2, Cc = shard.shape
    rows = pl.ds(q * (R2 // nq), R2 // nq)
    DMA = pltpu.SemaphoreType.DMA

    def copies(ins, outs, sems, which):
        sh, out = ins[0], outs[0]
        send1, recv1, send2, recv2, send0, recv0 = sems
        x, y, c, s = _place()
        sib = (x, y, 1 - c)
        rc = pltpu.make_async_remote_copy
        if which == "first":
            return [rc(sh.at[c, rows], out.at[s, c, rows], send1.at[m - 1], recv1.at[m - 1],
                       device_id=_chip_dev(s ^ m, c), device_id_type=MESH) for m in range(1, 4)]
        if which == "own":
            return [rc(sh.at[h, rows], out.at[s, h, rows], send0.at[h], recv0.at[h], device_id=sib, device_id_type=MESH)
                    for h in range(2)]
        if which == "landed":
            return [rc(sh.at[c, rows], out.at[s ^ m, c, rows], send1.at[m - 1], recv1.at[m - 1], device_id=sib,
                       device_id_type=MESH) for m in range(1, 4)]
        half = c if which == "passed" else 1 - c
        return [rc(out.at[s ^ m, half, rows], out.at[s ^ m, half, rows], send2.at[m - 1], recv2.at[m - 1], device_id=sib,
                   device_id_type=MESH) for m in range(1, 4)]

    def start(ins, outs, sems):
        for cp in copies(ins, outs, sems, "first") + copies(ins, outs, sems, "own"):
            cp.start()

    def wait(ins, outs, sems):
        passed = copies(ins, outs, sems, "passed")
        for lan, pas in zip(copies(ins, outs, sems, "landed"), passed):
            lan.wait_recv()
            pas.start()
        for cp in copies(ins, outs, sems, "handed"):
            cp.wait_recv()
        for cp in copies(ins, outs, sems, "own"):
            cp.wait()
        for cp in copies(ins, outs, sems, "first") + passed:
            cp.wait_send()

    return _Comm([shard] + ([land] if land is not None else []), [jax.ShapeDtypeStruct((4, two, R2, Cc), shard.dtype)],
                 [DMA((3,)), DMA((3,)), DMA((3,)), DMA((3,)), DMA((2,)), DMA((2,))], start, wait,
                 {1: 0} if land is not None else None)


def _a2a_comm(S1, q=0, nq=1, land=None):
    S4, R2, Cc = S1.shape
    rows = pl.ds(q * (R2 // nq), R2 // nq)
    DMA = pltpu.SemaphoreType.DMA

    def copies(ins, outs, sems):
        x, y, c, s = _place()
        return [pltpu.make_async_remote_copy(ins[0].at[s ^ m, rows], outs[0].at[m - 1, rows], sems[0].at[m - 1],
                                             sems[1].at[m - 1], device_id=_chip_dev(s ^ m, c), device_id_type=MESH)
                for m in range(1, 4)]

    def start(ins, outs, sems):
        for cp in copies(ins, outs, sems):
            cp.start()

    def wait(ins, outs, sems):
        for cp in copies(ins, outs, sems):
            cp.wait()

    return _Comm([S1] + ([land] if land is not None else []), [jax.ShapeDtypeStruct((3, R2, Cc), S1.dtype)],
                 [DMA((3,)), DMA((3,))], start, wait, {1: 0} if land is not None else None)


def _halves(G):
    return G.reshape(G.shape[0], 2, G.shape[1] // 2, G.shape[2])


def _swap_comm(piece):
    n, two, R2, Cc = piece.shape
    DMA = pltpu.SemaphoreType.DMA

    def copies(ins, outs, sems):
        x, y, c, s = _place()
        return [pltpu.make_async_remote_copy(ins[0].at[t, 1 - c], outs[0].at[t], sems[0].at[t], sems[1].at[t],
                                             device_id=(x, y, 1 - c), device_id_type=MESH) for t in range(n)]

    def start(ins, outs, sems):
        for cp in copies(ins, outs, sems):
            cp.start()

    def wait(ins, outs, sems):
        for cp in copies(ins, outs, sems):
            cp.wait()

    return _Comm([piece], [jax.ShapeDtypeStruct((n, R2, Cc), piece.dtype)], [DMA((n,)), DMA((n,))], start, wait)


def _add_half(pieces, As, cidx, name):
    R2, Cc = pieces[0].shape[2:]
    S4 = sum(pc.shape[0] for pc in pieces)
    tb, cb = _wide(R2, Cc, 3, 2 * HALO)
    nI, nJ = R2 // tb, Cc // cb

    def body(c_ref, g_ref, a_ref, *rest):
        rest[-1][...] = (g_ref[0, 0] + a_ref[0]).astype(BF16)

    out, t0 = None, 0
    for k, (pc, A) in enumerate(zip(pieces, As)):
        grid_spec = pltpu.PrefetchScalarGridSpec(
            num_scalar_prefetch=1, grid=(pc.shape[0], nI, nJ),
            in_specs=[pl.BlockSpec((1, 1, tb, cb), lambda t, i, j, c_ref: (t, c_ref[0], i, j)),
                      pl.BlockSpec((1, tb, cb), lambda t, i, j, c_ref: (t, i, j))] + ([ANY] if k else []),
            out_specs=pl.BlockSpec((tb, cb), lambda t, i, j, c_ref, t0=t0: ((t0 + t) * nI + i, j)))
        out = pl.pallas_call(
            functools.partial(body), name=f"{name}{k}", grid_spec=grid_spec, out_shape=jax.ShapeDtypeStruct((S4 * R2, Cc), BF16),
            input_output_aliases={3: 0} if k else {},
            compiler_params=pltpu.CompilerParams(dimension_semantics=("parallel", "parallel", "parallel"),
                                                 vmem_limit_bytes=VMEM_LIMIT),
        )(*((cidx, pc, A) + ((out,) if k else ())))
        t0 += pc.shape[0]
    return out.reshape(S4, R2, Cc)


def _add_own(S1, B, chip_idx, cidx, name):
    S4, R2, Cc = S1.shape
    tb, cb = _wide(R2, Cc, 3, 2 * HALO)

    def body(s_idx, c_idx, s_ref, b_ref, o_ref):
        o_ref[...] = ((s_ref[0].astype(F32) + b_ref[0].astype(F32)) + b_ref[1].astype(F32)) + b_ref[2].astype(F32)

    grid_spec = pltpu.PrefetchScalarGridSpec(
        num_scalar_prefetch=2, grid=(R2 // tb, Cc // cb),
        in_specs=[pl.BlockSpec((1, tb, cb), lambda i, j, s_idx, c_idx: (s_idx[0], i, j)),
                  pl.BlockSpec((3, tb, cb), lambda i, j, s_idx, c_idx: (0, i, j))],
        out_specs=pl.BlockSpec((None, tb, cb), lambda i, j, s_idx, c_idx: (c_idx[0], i, j)))
    return pl.pallas_call(body, name=name, grid_spec=grid_spec, out_shape=jax.ShapeDtypeStruct((2, R2, Cc), F32),
                          compiler_params=pltpu.CompilerParams(dimension_semantics=("parallel", "parallel"),
                                                               vmem_limit_bytes=VMEM_LIMIT))(chip_idx, cidx, S1, B)


def _sibling_fill(Hs, name):
    def body(h_ref, out_ref, send, recv):
        x, y, c, s = _place()
        cp = pltpu.make_async_remote_copy(h_ref.at[c], out_ref.at[c], send, recv, device_id=(x, y, 1 - c), device_id_type=MESH)
        cp.start()
        cp.wait()

    return pl.pallas_call(
        body, name=name, in_specs=[ANY], out_specs=ANY, out_shape=jax.ShapeDtypeStruct(Hs.shape, Hs.dtype),
        input_output_aliases={0: 0}, scratch_shapes=[pltpu.SemaphoreType.DMA, pltpu.SemaphoreType.DMA],
    )(Hs)


def _gather_all(buf, name):
    R, Cc = buf.shape

    def body(b_ref, out_ref, send, recv, local):
        x, y, c, s = _place()
        d = 2 * s + c
        mine = pltpu.make_async_copy(b_ref, out_ref.at[d], local)
        mine.start()
        cps = []
        for m in range(1, 8):
            t = d ^ m
            cp = pltpu.make_async_remote_copy(b_ref, out_ref.at[d], send.at[m - 1], recv.at[m - 1],
                                              device_id=(t // 4, (t // 2) % 2, t % 2), device_id_type=MESH)
            cp.start()
            cps.append(cp)
        for cp in cps:
            cp.wait()
        mine.wait()

    return pl.pallas_call(
        body, name=name, in_specs=[ANY], out_specs=ANY, out_shape=jax.ShapeDtypeStruct((8, R, Cc), buf.dtype),
        scratch_shapes=[pltpu.SemaphoreType.DMA((7,)), pltpu.SemaphoreType.DMA((7,)), pltpu.SemaphoreType.DMA],
    )(buf)


def _finish_shard(S1, B, cidx, chip_idx, name):
    Hs = _sibling_fill(_add_own(S1, B, chip_idx, cidx, name + "_sum"), name + "_gather")
    return Hs.reshape(2 * Hs.shape[1], Hs.shape[2])


def _pack_rows(vs):
    flat = jnp.concatenate([v.reshape(-1) for v in vs])
    n = flat.shape[0]
    rows = -(-n // (LANES * 2 * HALO)) * 2 * HALO
    return jnp.pad(flat, (0, rows * LANES - n)).reshape(rows, LANES)


def _unpack_rows(buf, shapes):
    flat = buf.reshape(-1)
    outs, o = [], 0
    for shp in shapes:
        n = 1
        for d in shp:
            n *= d
        outs.append(flat[o:o + n].reshape(shp))
        o += n
    return outs


def kernel(x, p, norm_mix_g, w_in, conv_a_w, conv_qkv_w, a_log, dt_bias, dn_norm_g, w_out, norm_ffn_g, w_up, conv_ffn_w, w_down, norm_ple_g, w_ple_gate, w_ple_proj, final_norm_g, loss_target, m_norm_mix_g, m_w_in, m_conv_a_w, m_conv_qkv_w, m_a_log, m_dt_bias, m_dn_norm_g, m_w_out, m_norm_ffn_g, m_w_up, m_conv_ffn_w, m_w_down, m_norm_ple_g, m_w_ple_gate, m_w_ple_proj, m_final_norm_g, v_norm_mix_g, v_w_in, v_conv_a_w, v_conv_qkv_w, v_a_log, v_dt_bias, v_dn_norm_g, v_w_out, v_norm_ffn_g, v_w_up, v_conv_ffn_w, v_w_down, v_norm_ple_g, v_w_ple_gate, v_w_ple_proj, v_final_norm_g):
    xs = x[0]
    ps = p[0, 0]
    tgt = loss_target[0]
    T, D = xs.shape
    H = a_log.shape[-1]
    DNW = H * HEAD_DIM
    CW = conv_a_w.shape[-1] * 4
    F = w_down.shape[1] * 4
    PD = ps.shape[-1]
    IN_MAIN = 3 * CW + 4 * DNW
    IN_COLS = IN_MAIN + 2 * H
    assert w_in.shape[-1] * 4 == IN_COLS and CW + DNW == D and 2 * H <= LANES
    cb = _tile(min(CW, DNW), 512, LANES)
    while F % cb:
        cb -= LANES
    cidx = lax.axis_index("c").astype(jnp.int32).reshape(1)
    chip = 2 * lax.axis_index("x") + lax.axis_index("y")

    def halves(w):
        sh = w[0].astype(BF16)
        return sh.reshape(2, sh.shape[0] // 2, sh.shape[1])

    def whole(land):
        return land.reshape(4, 2 * land.shape[2], land.shape[3])

    def rows(g4):
        return g4.reshape(4 * g4.shape[1], g4.shape[2])

    conv_shapes = [conv_a_w[0].shape, conv_qkv_w[0].shape, conv_ffn_w[0].shape]
    cpack = _pack_rows([conv_a_w[0], conv_qkv_w[0], conv_ffn_w[0]])
    sh_in, sh_out, sh_up, sh_down, sh_pg, sh_pp = (halves(w) for w in (w_in, w_out, w_up, w_down, w_ple_gate, w_ple_proj))
    l_in, cg = _run_comm(_merge(_ag_comm(sh_in), _ag_comm(cpack.reshape(2, cpack.shape[0] // 2, LANES))), "ag_w_in_conv")
    w_in_main, w_in_small = _join_shards(whole(l_in), IN_MAIN)
    cg = cg.reshape(4, cpack.shape[0], LANES)
    parts = [_unpack_rows(cg[t], conv_shapes) for t in range(4)]
    cw_a = jnp.concatenate([parts[t][0] for t in range(4)], axis=1)
    cw_qkv = jnp.concatenate([parts[t][1] for t in range(4)], axis=1)
    cw_ffn = jnp.concatenate([parts[t][2] for t in range(4)], axis=1)
    cw_q, cw_k, cw_v = cw_qkv[:, :DNW], cw_qkv[:, DNW:2 * DNW], cw_qkv[:, 2 * DNW:]
    cw_fg, cw_fv = cw_ffn[:, :F], cw_ffn[:, F:]
    pad_row = lambda v: jnp.pad(v, ((0, 0), (0, LANES - v.shape[1])))
    a_log_row, dt_row = pad_row(a_log), pad_row(dt_bias)
    gdn_t = jnp.tile(dn_norm_g, (1, H))
    gfin = final_norm_g.reshape(1, D)

    h1 = _rms_fwd(xs, norm_mix_g, "rms1")
    proj, (l_up,) = _mm(h1, w_in_main, mode="nn", out_dtypes=[F32], name="mm_proj", comm=_ag_comm(sh_up, q=0, nq=2))
    small = _mm(h1, w_in_small, mode="nn", out_dtypes=[F32], name="mm_small")
    ya = _ga_fwd(proj, cw_a, CW, cb)
    nq = 3 * CW // cb
    nd = DNW // cb
    qn = _qkv_fwd(proj, cw_q, nq, True, DNW, cb, "q_fwd")
    kn = _qkv_fwd(proj, cw_k, nq + nd, True, DNW, cb, "k_fwd")
    vs = _qkv_fwd(proj, cw_v, nq + 2 * nd, False, DNW, cb, "v_fwd")
    g, beta = _gb_fwd(small, a_log_row, dt_row, H)
    o, S0, inv_c, (l_up, l_out) = _delta_fwd(qn, kn, vs, g, beta, comm=_merge(_ag_comm(sh_up, l_up, q=1, nq=2), _ag_comm(sh_out)))
    w_out_f = rows(whole(l_out))
    w_out_a, w_out_b = w_out_f[:CW], w_out_f[CW:]
    w_up_4 = whole(l_up)
    z_coff = (3 * CW + 3 * DNW) // DNW
    assert (3 * CW + 3 * DNW) % DNW == 0 and CW % DNW == 0
    yb = _gnorm_fwd(o, proj, z_coff, gdn_t, DNW)
    add = lambda acc, r: (r + acc,)
    x1 = _mm(ya, w_out_a, mode="nn", out_dtypes=[F32], epi=add, extras=[xs], name="mm_out_a")
    x1 = _mm(yb, w_out_b, mode="nn", out_dtypes=[F32], epi=add, extras=[x1], name="mm_out_b")
    h2 = _rms_fwd(x1, norm_ffn_g, "rms2")
    up_g, (l_down,) = _mm(h2, w_up_4, mode="nn", b_split=(0, 2), out_dtypes=[F32], name="mm_up_g",
                          comm=_ag_comm(sh_down, q=0, nq=2))
    up_v, (l_down,) = _mm(h2, w_up_4, mode="nn", b_split=(2, 2), out_dtypes=[F32], name="mm_up_v",
                          comm=_ag_comm(sh_down, l_down, q=1, nq=2))
    w_down_f = rows(whole(l_down))
    act = _ffn_fwd(up_g, up_v, cw_fg, cw_fv, cb)
    x2, (l_pg, l_pp) = _mm(act, w_down_f, mode="nn", out_dtypes=[F32], epi=add, extras=[x1], name="mm_down",
                           comm=_merge(_ag_comm(sh_pg), _ag_comm(sh_pp)))
    w_pg_f = rows(whole(l_pg))
    w_pp_4 = whole(l_pp)
    h3 = _rms_fwd(x2, norm_ple_g, "rms3")
    pp = _mm(ps, w_pp_4, mode="nn", b_split=(0, 4), out_dtypes=[F32], name="mm_pp")

    def ple_epi(acc, x2v, ppv):
        pg = _sigmoid(acc)
        return x2v + pg * ppv, pg

    x3, pg = _mm(h3, w_pg_f, mode="nn", out_dtypes=[F32, F32], epi=ple_epi, extras=[x2, pp], name="mm_pg")

    dx3, dpg, dpp, fin = _final_fb(x3, tgt, gfin, pp, pg)
    loss = lax.psum(jnp.sum(fin[1]), ("x", "y", "c"))
    d_gfin = fin[0:1]
    def split_rows(dW):
        return dW.reshape(4, dW.shape[0] // 4, dW.shape[1])

    dW_pp = _mm(ps, dpp, mode="tn", out_split=4, out_dtypes=[F32], name="mm_dw_pp")
    dW_pg = _mm(h3, dpg, mode="tn", out_dtypes=[F32], name="mm_dw_pg")
    P_pp, P_pg = _halves(dW_pp), _halves(split_rows(dW_pg))
    dh3, (A_pp, A_pg) = _mm(dpg, w_pg_f, mode="nt", out_dtypes=[F32], name="mm_dh3",
                            comm=_merge(_swap_comm(P_pp), _swap_comm(P_pg)))
    S_pp = _add_half([P_pp], [A_pp], cidx, "rs_w_pp_add")
    S_pg = _add_half([P_pg], [A_pg], cidx, "rs_w_pg_add")
    dx2, dx2_b, d_gple = _rms_bwd(dh3, x2, norm_ple_g, dx3, "rms3_bwd")
    dW_down, (B_pp, B_pg) = _mm(act, dx2_b, mode="tn", out_dtypes=[F32], name="mm_dw_down",
                                comm=_merge(_a2a_comm(S_pp), _a2a_comm(S_pg)))
    P_down = _halves(split_rows(dW_down))
    dact, (A_down,) = _mm(dx2_b, w_down_f, mode="nt", out_dtypes=[F32], name="mm_dact", comm=_swap_comm(P_down))
    S_down = _add_half([P_down], [A_down], cidx, "rs_w_down_add")
    dup_g, dup_v, dcw_fg, dcw_fv = _ffn_bwd(dact, up_g, up_v, cw_fg, cw_fv, cb)
    dW_up_g, (B_down,) = _mm(h2, dup_g, mode="tn", out_split=2, out_dtypes=[F32], name="mm_dw_up_g", comm=_a2a_comm(S_down))
    P_ug = _halves(dW_up_g)
    dW_up_v, (A_ug,) = _mm(h2, dup_v, mode="tn", out_split=2, out_dtypes=[F32], name="mm_dw_up_v", comm=_swap_comm(P_ug))
    P_uv = _halves(dW_up_v)
    dh2, (A_uv,) = _mm(dup_g, w_up_4, mode="nt", b_split=(0, 2), out_dtypes=[F32], name="mm_dh2_g", comm=_swap_comm(P_uv))
    S_up = _add_half([P_ug, P_uv], [A_ug, A_uv], cidx, "rs_w_up_add")
    dh2 = _mm(dup_v, w_up_4, mode="nt", b_split=(2, 2), out_dtypes=[F32], epi=add, extras=[dh2], name="mm_dh2_v")
    dx1, dx1_b, d_gffn = _rms_bwd(dh2, x1, norm_ffn_g, dx2, "rms2_bwd")
    dW_out_a = _mm(ya, dx1_b, mode="tn", out_dtypes=[F32], name="mm_dw_out_a")
    dW_out_b = _mm(yb, dx1_b, mode="tn", out_dtypes=[F32], name="mm_dw_out_b")
    P_oa, P_ob = _halves(dW_out_a.reshape(-1, D // 4, D)), _halves(dW_out_b.reshape(-1, D // 4, D))
    dymix, (A_oa, A_ob) = _mm(dx1_b, w_out_f, mode="nt", out_dtypes=[F32], name="mm_dymix",
                              comm=_merge(_swap_comm(P_oa), _swap_comm(P_ob)))
    S_out = _add_half([P_oa, P_ob], [A_oa, A_ob], cidx, "rs_w_out_add")
    dax, dab, dac, dcw_a = _ga_bwd(dymix, proj, cw_a, CW, cb)
    do, dz, d_gdn = _gnorm_bwd(dymix, CW // DNW, o, proj, z_coff, gdn_t, DNW)
    dqn, dkn, dvs, dgB, dbB, (B_up, B_out) = _delta_bwd(qn, kn, vs, g, beta, S0, inv_c, do,
                                                        comm=_merge(_a2a_comm(S_up), _a2a_comm(S_out)))
    dq_pre, dcw_q = _qkv_bwd(dqn, proj, cw_q, nq, True, DNW, cb, "q_bwd")
    dk_pre, dcw_k = _qkv_bwd(dkn, proj, cw_k, nq + nd, True, DNW, cb, "k_bwd")
    dv_pre, dcw_v = _qkv_bwd(dvs, proj, cw_v, nq + 2 * nd, False, DNW, cb, "v_bwd")
    dsmall, d_ab = _gb_bwd(dgB, dbB, small, g, beta, a_log_row, dt_row, H)
    dproj = jnp.concatenate([dax, dab, dac, dq_pre, dk_pre, dv_pre, dz], axis=1)
    dW_in_main = _mm(h1, dproj, mode="tn", out_dtypes=[F32], name="mm_dw_in")
    dW_in_small = _mm(h1, dsmall, mode="tn", out_dtypes=[F32], name="mm_dw_in_small")
    P_in = _halves(_split_shards(dW_in_main, dW_in_small, IN_COLS // 4))
    (A_in,) = _run_comm(_swap_comm(P_in), "rs_w_in_swap")
    S_in = _add_half([P_in], [A_in], cidx, "rs_w_in_add")
    dh1, (B_in,) = _mm(dproj, w_in_main, mode="nt", out_dtypes=[F32], name="mm_dh1", comm=_a2a_comm(S_in))
    dh1 = _mm(dsmall, w_in_small, mode="nt", out_dtypes=[F32], epi=add, extras=[dh1], name="mm_dh1_small")
    dx, _, d_gmix = _rms_bwd(dh1, xs, norm_mix_g, dx1, "rms1_bwd")

    chip_idx = chip.astype(jnp.int32).reshape(1)

    def update(S1, B, w, m, v, name):
        gr = _finish_shard(S1, B, cidx, chip_idx, "rs_" + name)
        delta, m2, v2 = _adamw(w[0], gr, m[0], v[0], "adamw_" + name)
        return gr[None], delta[None], m2[None], v2[None]

    big = {
        "w_in": update(S_in, B_in, w_in, m_w_in, v_w_in, "w_in"),
        "w_out": update(S_out, B_out, w_out, m_w_out, v_w_out, "w_out"),
        "w_up": update(S_up, B_up, w_up, m_w_up, v_w_up, "w_up"),
        "w_down": update(S_down, B_down, w_down, m_w_down, v_w_down, "w_down"),
        "w_ple_gate": update(S_pg, B_pg, w_ple_gate, m_w_ple_gate, v_w_ple_gate, "w_pg"),
        "w_ple_proj": update(S_pp, B_pp, w_ple_proj, m_w_ple_proj, v_w_ple_proj, "w_pp"),
    }

    small_grads = [d_gmix[0:1], dcw_a[:cw_a.shape[0]], jnp.concatenate([dcw_q, dcw_k, dcw_v], axis=1)[:cw_qkv.shape[0]],
                   d_ab[0:1, :H], d_ab[1:2, :H], d_gdn[0:1], d_gffn[0:1],
                   jnp.concatenate([dcw_fg, dcw_fv], axis=1)[:cw_ffn.shape[0]], d_gple[0:1], d_gfin]
    small_shapes = [v.shape for v in small_grads]
    gpack = _pack_rows(small_grads)
    gsum = _sum_stack(_gather_all(gpack, "ag_small"), "sum_small")
    (g_gmix, g_cwa, g_cwqkv, g_alog, g_dt, g_gdn, g_gffn, g_cwffn, g_gple, g_gfin) = _unpack_rows(gsum, small_shapes)

    def my_cols(v):
        Cc = v.shape[1] // 4
        return lax.dynamic_slice_in_dim(v, chip * Cc, Cc, axis=1)

    g_small = [g_gmix, my_cols(g_cwa), my_cols(g_cwqkv), g_alog, g_dt, g_gdn, g_gffn, my_cols(g_cwffn), g_gple, g_gfin]
    w_small = [norm_mix_g, conv_a_w[0], conv_qkv_w[0], a_log, dt_bias, dn_norm_g, norm_ffn_g, conv_ffn_w[0], norm_ple_g, gfin]
    m_small = [m_norm_mix_g, m_conv_a_w[0], m_conv_qkv_w[0], m_a_log, m_dt_bias, m_dn_norm_g, m_norm_ffn_g, m_conv_ffn_w[0],
               m_norm_ple_g, m_final_norm_g.reshape(1, D)]
    v_small = [v_norm_mix_g, v_conv_a_w[0], v_conv_qkv_w[0], v_a_log, v_dt_bias, v_dn_norm_g, v_norm_ffn_g, v_conv_ffn_w[0],
               v_norm_ple_g, v_final_norm_g.reshape(1, D)]
    shp = [v.shape for v in w_small]
    ds_, ms_, vs_ = _adamw(_pack_rows(w_small), _pack_rows(g_small), _pack_rows(m_small), _pack_rows(v_small), "adamw_small")
    out_shapes = [norm_mix_g.shape, conv_a_w.shape, conv_qkv_w.shape, a_log.shape, dt_bias.shape, dn_norm_g.shape,
                  norm_ffn_g.shape, conv_ffn_w.shape, norm_ple_g.shape, final_norm_g.shape]
    rs = lambda vals: [v.reshape(s) for v, s in zip(vals, out_shapes)]
    sg, sd_, sm_, sv_ = rs(g_small), rs(_unpack_rows(ds_, shp)), rs(_unpack_rows(ms_, shp)), rs(_unpack_rows(vs_, shp))
    names_small = ["norm_mix_g", "conv_a_w", "conv_qkv_w", "a_log", "dt_bias", "dn_norm_g", "norm_ffn_g", "conv_ffn_w",
                   "norm_ple_g", "final_norm_g"]
    res = {n: (sg[i], sd_[i], sm_[i], sv_[i]) for i, n in enumerate(names_small)}
    res.update(big)
    order = ["norm_mix_g", "w_in", "conv_a_w", "conv_qkv_w", "a_log", "dt_bias", "dn_norm_g", "w_out", "norm_ffn_g", "w_up",
             "conv_ffn_w", "w_down", "norm_ple_g", "w_ple_gate", "w_ple_proj", "final_norm_g"]
    return (loss, dx[None], *[res[n][0] for n in order], *[res[n][1] for n in order], *[res[n][2] for n in order],
            *[res[n][3] for n in order])
```

```python
import functools

import jax
import jax.numpy as jnp
from jax import lax
from jax.experimental import pallas as pl
from jax.experimental.pallas import tpu as pltpu

F32 = jnp.float32
BF16 = jnp.bfloat16
LANES = 128
HALO = 8
HEAD_DIM = 128
CHUNK = 64
EPS = 1e-6
VMEM_LIMIT = 56 * 1024 * 1024
MM_VMEM_BUDGET = 40 * 1024 * 1024
MM_STEP_BYTES = 1 << 20
EW_VMEM_BUDGET = 28 * 1024 * 1024
MESH = pl.DeviceIdType.MESH

ADAM_LR, ADAM_B1, ADAM_B2, ADAM_EPS, ADAM_WD, ADAM_STEP = 0.001, 0.9, 0.999, 1e-08, 0.01, 10


def _tile(n, cap, unit):
    if n <= cap:
        return n
    d = (cap // unit) * unit
    while d >= unit:
        if n % d == 0:
            return d
        d -= unit
    raise ValueError(f"no tile for {n} (cap {cap}, unit {unit})")


def _sigmoid(x):
    return 1.0 / (1.0 + jnp.exp(-x))


def _divisors(n, cap):
    ds = [d for d in range(cap // LANES * LANES, 0, -LANES) if n % d == 0]
    return [n] if (n <= cap or not ds) else ds


def _mm_tiles(M, N, K, n_unit, k_unit, a_bytes, n_blocks_mn, a_transposed):
    best = None
    for tm in _divisors(M, 1536):
        for tn in _divisors(n_unit, 1536):
            for tk in _divisors(k_unit, 4096):
                nk = K // tk
                vmem = 2 * tm * tk * a_bytes + 2 * tk * tn * 2 + 2 * 4 * tm * tn * n_blocks_mn + (4 * tm * tn if nk > 1 else 0)
                if vmem > MM_VMEM_BUDGET:
                    continue
                steps = (M // tm) * (N // tn) * nk
                cost = (M * K * a_bytes * (N // tn if nk > 1 else 1) + K * N * 2 * (M // tm) + 4 * M * N * n_blocks_mn
                        + (8 * M * N * nk // 3 if nk > 1 else 0) + steps * MM_STEP_BYTES
                        + (2 * steps * tm * tk if a_transposed else 0))
                if best is None or cost < best[0]:
                    best = (cost, tm, tn, tk)
    return best[1:]


def _mm(a, b, *, mode, out_dtypes, name, epi=None, extras=(), comm=None, b_split=None, out_split=None):
    if b_split is not None:
        lo, ns = b_split
        Rb, Cb = b.shape[1], b.shape[2]
    if mode == "nn":
        (M, K), N = a.shape, (ns * Cb if b_split else b.shape[1])
    elif mode == "nt":
        (M, K), N = a.shape, (Rb if b_split else b.shape[0])
    else:
        (K, M), N = a.shape, b.shape[1]
    n_ex, n_out = len(extras), len(out_dtypes)
    n_unit = Cb if (b_split and mode == "nn") else (N // out_split if out_split else N)
    k_unit = Cb if (b_split and mode == "nt") else K
    tm, tn, tk = _mm_tiles(M, N, K, n_unit, k_unit, a.dtype.itemsize, n_ex + n_out, mode == "tn")
    nk = K // tk
    a_spec = pl.BlockSpec((tk, tm), lambda i, j, k: (k, i)) if mode == "tn" else pl.BlockSpec((tm, tk), lambda i, j, k: (i, k))
    if b_split and mode == "nn":
        nb = Cb // tn
        b_spec = pl.BlockSpec((None, tk, tn), lambda i, j, k: (lo + j // nb, k, j % nb))
    elif b_split:
        nb = Cb // tk
        b_spec = pl.BlockSpec((None, tn, tk), lambda i, j, k: (lo + k // nb, j, k % nb))
    else:
        b_spec = pl.BlockSpec((tn, tk), lambda i, j, k: (j, k)) if mode == "nt" else pl.BlockSpec((tk, tn), lambda i, j, k: (k, j))
    mn_spec = pl.BlockSpec((tm, tn), lambda i, j, k: (i, j))
    out_shapes = [jax.ShapeDtypeStruct((M, N), dt) for dt in out_dtypes]
    out_specs = [mn_spec] * n_out
    if out_split:
        assert n_ex == 0 and n_out == 1
        nbo = (N // out_split) // tn
        out_specs = [pl.BlockSpec((None, tm, tn), lambda i, j, k: (j // nbo, i, j % nbo))]
        out_shapes = [jax.ShapeDtypeStruct((out_split, M, N // out_split), out_dtypes[0])]
    dims = {"nn": (((1,), (0,)), ((), ())), "nt": (((1,), (1,)), ((), ())), "tn": (((0,), (0,)), ((), ()))}[mode]

    def body(*refs):
        a_ref, b_ref = refs[0], refs[1]
        ex_refs = refs[2:2 + n_ex]
        out_refs = refs[2 + n_ex:2 + n_ex + n_out]
        part = lax.dot_general(a_ref[...].astype(BF16), b_ref[...].astype(BF16), dims, preferred_element_type=F32)

        def finish(acc):
            outs = (acc,) if epi is None else epi(acc, *[r[...] for r in ex_refs])
            for r, o in zip(out_refs, outs):
                r[...] = o.astype(r.dtype)

        if nk == 1:
            finish(part)
            return
        acc_ref = refs[-1]
        k = pl.program_id(2)

        @pl.when(k == 0)
        def _():
            acc_ref[...] = part

        @pl.when(jnp.logical_and(k > 0, k < nk - 1))
        def _():
            acc_ref[...] += part

        @pl.when(k == nk - 1)
        def _():
            finish(acc_ref[...] + part)

    outs, comm_outs = _call(
        body, name=name, grid=(M // tm, N // tn, nk),
        in_specs=[a_spec, b_spec] + [mn_spec] * n_ex,
        out_specs=out_specs,
        out_shape=out_shapes,
        scratch_shapes=[pltpu.VMEM((tm, tn), F32)] if nk > 1 else [],
        semantics=("parallel", "parallel", "arbitrary"), args=(a, b, *extras), comm=comm)
    res = outs[0] if n_out == 1 else outs
    return res if comm is None else (res, comm_outs)


def _tiled(fn, *, T, C, ins, out_dtypes=(), acc_rows=(), tb=None, cb=512, name):
    tb = _tile(T, tb or (512 if cb <= 1024 else 256), HALO)
    nI, nJ = T // tb, C // cb
    hb, nH = tb // HALO, T // HALO
    specs, args, kinds = [], [], []
    for kind, arr, cmap in ins:
        cm = cmap if cmap is not None else (lambda j: j)
        kinds.append(kind)
        if kind == "cur":
            specs.append(pl.BlockSpec((tb, cb), lambda j, i, cm=cm: (i, cm(j))))
            args.append(arr)
        elif kind == "ext":
            specs.append(pl.BlockSpec((HALO, cb), lambda j, i, cm=cm: (jnp.maximum(i * hb - 1, 0), cm(j))))
            specs.append(pl.BlockSpec((tb, cb), lambda j, i, cm=cm: (i, cm(j))))
            specs.append(pl.BlockSpec((HALO, cb), lambda j, i, cm=cm: (jnp.minimum((i + 1) * hb, nH - 1), cm(j))))
            args += [arr, arr, arr]
        elif kind == "row":
            specs.append(pl.BlockSpec((arr.shape[0], cb), lambda j, i, cm=cm: (0, cm(j))))
            args.append(arr)
        elif kind == "stack":
            specs.append(pl.BlockSpec((arr.shape[0], tb, cb), lambda j, i, cm=cm: (0, i, cm(j))))
            args.append(arr)
        else:
            raise ValueError(kind)
    n_in = len(args)
    n_out, n_acc = len(out_dtypes), len(acc_rows)

    def body(*refs):
        j, i = pl.program_id(0), pl.program_id(1)
        vals, r = [], 0
        for kind in kinds:
            if kind == "ext":
                prev = jnp.where(i == 0, 0.0, refs[r][...].astype(F32))
                cur = refs[r + 1][...].astype(F32)
                nxt = jnp.where(i == nI - 1, 0.0, refs[r + 2][...].astype(F32))
                vals.append(jnp.concatenate([prev, cur, nxt], axis=0))
                r += 3
            else:
                vals.append(refs[r][...])
                r += 1
        res = fn(j, i, *vals)
        for ref, o in zip(refs[n_in:n_in + n_out], res[:n_out]):
            ref[...] = o.astype(ref.dtype)
        for ref, o in zip(refs[n_in + n_out:], res[n_out:]):
            @pl.when(i == 0)
            def _(ref=ref, o=o):
                ref[...] = o

            @pl.when(i > 0)
            def _(ref=ref, o=o):
                ref[...] += o

    outs = pl.pallas_call(
        body, name=name, grid=(nJ, nI), in_specs=specs,
        out_specs=[pl.BlockSpec((tb, cb), lambda j, i: (i, j))] * n_out
        + [pl.BlockSpec((rows, cb), lambda j, i: (0, j)) for rows in acc_rows],
        out_shape=[jax.ShapeDtypeStruct((T, C), dt) for dt in out_dtypes]
        + [jax.ShapeDtypeStruct((rows, C), F32) for rows in acc_rows],
        compiler_params=pltpu.CompilerParams(dimension_semantics=("parallel", "arbitrary"),
                                             vmem_limit_bytes=VMEM_LIMIT),
    )(*args)
    return outs


def _conv_causal(xe, w):
    K = w.shape[0]
    y = xe * w[K - 1:K]
    for j in range(K - 1):
        y = y + pltpu.roll(xe, K - 1 - j, 0) * w[j:j + 1]
    return y


def _conv_anti(de, w):
    K, n = w.shape[0], de.shape[0]
    y = de * w[K - 1:K]
    for j in range(K - 1):
        y = y + pltpu.roll(de, n - (K - 1 - j), 0) * w[j:j + 1]
    return y


def _conv_dw(dce, xe, K):
    n = dce.shape[0]
    tb = n - 2 * HALO
    rows = []
    for j in range(K):
        xs = xe if j == K - 1 else pltpu.roll(xe, K - 1 - j, 0)
        rows.append(jnp.sum((dce * xs)[HALO:HALO + tb], axis=0, keepdims=True))
    rows.append(jnp.zeros((HALO - K, dce.shape[1]), F32))
    return jnp.concatenate(rows, axis=0)


def _own(xe):
    return xe[HALO:xe.shape[0] - HALO]


def _row0(v):
    return jnp.concatenate([v, jnp.zeros((HALO - 1, v.shape[1]), F32)], axis=0)


def _per_head(fn, *xs):
    n = xs[0].shape[1] // HEAD_DIM
    outs = [fn(*[x[:, g * HEAD_DIM:(g + 1) * HEAD_DIM] for x in xs]) for g in range(n)]
    return outs[0] if n == 1 else jnp.concatenate(outs, axis=1)


def _rms_fwd(x, g, name):
    T, D = x.shape

    def fn(j, i, xv, gv):
        r = lax.rsqrt(jnp.mean(xv * xv, axis=1, keepdims=True) + EPS)
        return (xv * r * gv,)

    return _tiled(fn, T=T, C=D, ins=[("cur", x, None), ("row", g, None)], out_dtypes=[BF16], cb=D, name=name)[0]


def _rms_bwd_math(dy, xv, gv):
    r = lax.rsqrt(jnp.mean(xv * xv, axis=1, keepdims=True) + EPS)
    xh = xv * r
    dxh = dy * gv
    dx = r * (dxh - xh * jnp.mean(dxh * xh, axis=1, keepdims=True))
    dg = jnp.sum(dy * xh, axis=0, keepdims=True)
    return dx, dg


def _rms_bwd(dh, x, g, dres, name):
    T, D = x.shape

    def fn(j, i, dhv, xv, gv, dr):
        dx, dg = _rms_bwd_math(dhv, xv, gv)
        return dr + dx, dr + dx, _row0(dg)

    return _tiled(fn, T=T, C=D, ins=[("cur", dh, None), ("cur", x, None), ("row", g, None), ("cur", dres, None)],
                  out_dtypes=[F32, BF16], acc_rows=[HALO], cb=D, name=name)


def _final_fb(x3, tgt, g, pp, pg):
    T, D = x3.shape

    def fn(j, i, xv, tv, gv, ppv, pgv):
        r = lax.rsqrt(jnp.mean(xv * xv, axis=1, keepdims=True) + EPS)
        xh = xv * r
        e = xh * gv - tv
        dy = e * (1.0 / D)
        dxh = dy * gv
        dx = r * (dxh - xh * jnp.mean(dxh * xh, axis=1, keepdims=True))
        dg = jnp.sum(dy * xh, axis=0, keepdims=True)
        ls = jnp.sum(e * e, axis=0, keepdims=True) * (0.5 / D)
        return (dx, dx * ppv * pgv * (1.0 - pgv), dx * pgv,
                jnp.concatenate([dg, ls, jnp.zeros((HALO - 2, D), F32)], axis=0))

    return _tiled(fn, T=T, C=D, ins=[("cur", x3, None), ("cur", tgt, None), ("row", g, None), ("cur", pp, None),
                                      ("cur", pg, None)], out_dtypes=[F32, BF16, BF16], acc_rows=[HALO], cb=D, name="final_fb")


def _ga_fwd(proj, w_a, CW, cb):
    T = proj.shape[0]
    n = CW // cb

    def fn(j, i, ax, ab, ac, w):
        c = _conv_causal(ac * ax, w)
        return (ab * _own(c),)

    return _tiled(fn, T=T, C=CW, ins=[("ext", proj, None), ("cur", proj, lambda j: j + n), ("ext", proj, lambda j: j + 2 * n),
                                       ("row", w_a, None)], out_dtypes=[BF16], cb=cb, name="ga_fwd")[0]


def _ga_bwd(dymix, proj, w_a, CW, cb):
    T = proj.shape[0]
    n = CW // cb
    K = w_a.shape[0]

    def fn(j, i, dy, ax, ab, ac, w):
        u = ac * ax
        c = _conv_causal(u, w)
        dc = dy * ab
        du = _conv_anti(dc, w)
        return _own(du * ac), _own(dy * c), _own(du * ax), _conv_dw(dc, u, K)

    return _tiled(fn, T=T, C=CW, ins=[("ext", dymix, None), ("ext", proj, None), ("ext", proj, lambda j: j + n),
                                       ("ext", proj, lambda j: j + 2 * n), ("row", w_a, None)],
                  out_dtypes=[BF16, BF16, BF16], acc_rows=[HALO], cb=cb, name="ga_bwd")


def _l2n(s):
    return s * lax.rsqrt(jnp.sum(s * s, axis=1, keepdims=True) + EPS)


def _qkv_fwd(proj, w_sec, coff, normalize, DNW, cb, name):
    T = proj.shape[0]

    def fn(j, i, pre, w):
        c = _own(_conv_causal(pre, w))
        s = c * _sigmoid(c)
        return (_per_head(_l2n, s) if normalize else s,)

    return _tiled(fn, T=T, C=DNW, ins=[("ext", proj, lambda j: j + coff), ("row", w_sec, None)],
                  out_dtypes=[F32], cb=cb, name=name)[0]


def _qkv_bwd(dsec, proj, w_sec, coff, normalize, DNW, cb, name):
    T = proj.shape[0]
    K = w_sec.shape[0]

    def l2n_bwd(s, dn):
        r = lax.rsqrt(jnp.sum(s * s, axis=1, keepdims=True) + EPS)
        nrm = s * r
        return r * (dn - nrm * jnp.sum(dn * nrm, axis=1, keepdims=True))

    def fn(j, i, dn, pre, w):
        c = _conv_causal(pre, w)
        sg = _sigmoid(c)
        s = c * sg
        ds = _per_head(l2n_bwd, s, dn) if normalize else dn
        dc = ds * (sg * (1.0 + c * (1.0 - sg)))
        return _own(_conv_anti(dc, w)), _conv_dw(dc, pre, K)

    return _tiled(fn, T=T, C=DNW, ins=[("ext", dsec, None), ("ext", proj, lambda j: j + coff), ("row", w_sec, None)],
                  out_dtypes=[BF16], acc_rows=[HALO], cb=cb, name=name)


def _gb_fwd(small, a_log_row, dt_row, H):
    T = small.shape[0]

    def fn(j, i, sm, al, dt):
        z = sm + dt
        sp = jnp.maximum(z, 0.0) + jnp.log(1.0 + jnp.exp(-jnp.abs(z)))
        g = -jnp.exp(al) * sp
        beta = _sigmoid(pltpu.roll(sm, LANES - H, 1))
        return g, beta

    return _tiled(fn, T=T, C=LANES, ins=[("cur", small, None), ("row", a_log_row, None), ("row", dt_row, None)],
                  out_dtypes=[F32, F32], cb=LANES, name="gb_fwd")


def _gb_bwd(dgB, dbB, small, g, beta, a_log_row, dt_row, H):
    T = small.shape[0]

    def fn(j, i, dgv, dbv, sm, gv, bv, al, dt):
        lane = lax.broadcasted_iota(jnp.int32, sm.shape, 1)
        dg = jnp.zeros(sm.shape, F32)
        db = jnp.zeros(sm.shape, F32)
        for h in range(H):
            dg = jnp.where(lane == h, jnp.sum(dgv[h], axis=1, keepdims=True), dg)
            db = jnp.where(lane == h, jnp.sum(dbv[h], axis=1, keepdims=True), db)
        da = dg * (-jnp.exp(al)) * _sigmoid(sm + dt)
        dbb = db * bv * (1.0 - bv)
        dsm = jnp.where(lane < H, da, 0.0) + pltpu.roll(jnp.where(lane < H, dbb, 0.0), H, 1)
        d_alog = jnp.sum(jnp.where(lane < H, dg * gv, 0.0), axis=0, keepdims=True)
        d_dt = jnp.sum(jnp.where(lane < H, da, 0.0), axis=0, keepdims=True)
        return dsm, jnp.concatenate([d_alog, d_dt, jnp.zeros((HALO - 2, LANES), F32)], axis=0)

    return _tiled(fn, T=T, C=LANES, ins=[("stack", dgB, None), ("stack", dbB, None), ("cur", small, None), ("cur", g, None),
                                          ("cur", beta, None), ("row", a_log_row, None), ("row", dt_row, None)],
                  out_dtypes=[BF16], acc_rows=[HALO], cb=LANES, name="gb_bwd")


_DIMS = {"nn": (((1,), (0,)), ((), ())), "nt": (((1,), (1,)), ((), ())), "tn": (((0,), (0,)), ((), ()))}
_DOT_BWD = {"nn": (("nt", "gb"), ("tn", "ag")), "nt": (("nn", "gb"), ("tn", "ga")), "tn": (("nt", "bg"), ("nn", "ag"))}


def _split(a):
    hi = a.astype(BF16)
    return hi, (a - hi.astype(F32)).astype(BF16)


def _raw_dot(a, b, kind, passes):
    dg = lambda x, y: lax.dot_general(x, y, _DIMS[kind], preferred_element_type=F32)
    if passes == 1:
        return dg(a.astype(BF16), b.astype(BF16))
    ah, al = _split(a)
    bh, bl = _split(b)
    if kind == "tn":
        return dg(ah, bh) + (dg(ah, bl) + dg(al, bh))
    m = a.shape[0]
    top = dg(jnp.concatenate([ah, al], axis=0), bh)
    return top[:m] + (dg(ah, bl) + top[m:])


def _raw_dot_exact(a, b, kind, exact):
    dg = lambda x, y: lax.dot_general(x, y, _DIMS[kind], preferred_element_type=F32)
    if exact == "a":
        bh, bl = _split(b)
        return dg(a.astype(BF16), bh) + dg(a.astype(BF16), bl)
    ah, al = _split(a)
    return dg(ah, b.astype(BF16)) + dg(al, b.astype(BF16))


@functools.lru_cache(maxsize=None)
def _dotc(kind):
    @jax.custom_vjp
    def f(a, b):
        return _raw_dot_exact(a, b, kind, "a")

    def fwd(a, b):
        return _raw_dot_exact(a, b, kind, "a"), a

    def bwd(a, g):
        db = _raw_dot_exact(a, g, "tn", "a") if kind == "nn" else _raw_dot_exact(g, a, "tn", "b")
        return jnp.zeros_like(a), db

    f.defvjp(fwd, bwd)
    return f


@functools.lru_cache(maxsize=None)
def _dotf(kind, passes):
    @jax.custom_vjp
    def f(a, b):
        return _raw_dot(a, b, kind, passes)

    def fwd(a, b):
        return _raw_dot(a, b, kind, passes), (a, b)

    def bwd(res, g):
        ops = {"a": res[0], "b": res[1], "g": g}
        (ka, oa), (kb, ob) = _DOT_BWD[kind]
        return (_raw_dot(ops[oa[0]], ops[oa[1]], ka, passes), _raw_dot(ops[ob[0]], ops[ob[1]], kb, passes))

    f.defvjp(fwd, bwd)
    return f


@jax.custom_vjp
def _saved_inverse(L, inv):
    return inv


def _saved_inverse_fwd(L, inv):
    return inv, inv


def _saved_inverse_bwd(inv, g):
    d3nt, d3tn = _dotf("nt", 3), _dotf("tn", 3)
    return -d3nt(d3tn(inv, g), inv), jnp.zeros_like(inv)


_saved_inverse.defvjp(_saved_inverse_fwd, _saved_inverse_bwd)


def _chunk_fn(q, k, v, gB, bB, S, inv_saved=None):
    C = CHUNK
    d3 = _dotf("nn", 3)
    d1, d1nt, d1tn = _dotf("nn", 1), _dotf("nt", 1), _dotf("tn", 1)
    each = lambda f, *ls: tuple(f(*xs) for xs in zip(*ls))
    row = lax.broadcasted_iota(jnp.int32, (C, C), 0)
    col = lax.broadcasted_iota(jnp.int32, (C, C), 1)
    causal = row >= col
    strict = row > col
    tril = jnp.where(causal, 1.0, 0.0).astype(F32)
    eye = jnp.where(row == col, 1.0, 0.0).astype(F32)
    avg = jnp.full((C, HEAD_DIM), 1.0 / HEAD_DIM, F32)
    gc = each(lambda g: _dotc("nn")(tril, g), gB)
    R = each(lambda g: _dotc("nt")(avg, g), gc)
    decay = each(lambda g, r: jnp.where(causal, jnp.exp(jnp.where(causal, g[:, :C] - r, 0.0)), 0.0), gc, R)
    kk = each(lambda x: d1nt(x, x), k)
    L = each(lambda a, d, b: jnp.where(strict, a * d * b[:, :C], 0.0), kk, decay, bB)
    if inv_saved is None:
        inv = each(lambda l: eye - l, L)
        P = L
        for _ in range(5):
            P = each(lambda p: d3(p, p), P)
            inv = each(lambda a, p: d3(a, eye + p), inv, P)
    else:
        inv = each(_saved_inverse, L, inv_saved)
    eg = each(jnp.exp, gc)
    u = each(lambda a, x, b: d3(a, x * b), inv, v, bB)
    w = each(lambda a, x, b, e: d3(a, x * b * e), inv, k, bB, eg)
    qs = each(lambda x: x * (HEAD_DIM ** -0.5), q)
    qk = each(lambda a, x, d: d1nt(a, x) * d, qs, k, decay)
    gl = each(lambda g: g[C - 1:C, :], gc)
    kd = each(lambda x, a, g: x * jnp.exp(a - g), k, gl, gc)
    qe = each(lambda a, e: a * e, qs, eg)
    nh = len(S)
    o = ()
    for c in range(len(q) // nh):
        sl = slice(c * nh, (c + 1) * nh)
        v_new = each(lambda a, b, s: a - d1(b, s), u[sl], w[sl], S)
        o1 = each(lambda a, s: d1(a, s), qe[sl], S)
        o += each(lambda a, b, vn: a + d1(b, vn), o1, qk[sl], v_new)
        kv = each(lambda x, vn: d1tn(x, vn), kd[sl], v_new)
        S = each(lambda s, a, b: s * jnp.exp(a) + b, S, gl[sl], kv)
    return (o, S), inv


def _sel_lane(x, h):
    lane = lax.broadcasted_iota(jnp.int32, x.shape, 1)
    return jnp.broadcast_to(jnp.sum(jnp.where(lane == h, x, 0.0), axis=1, keepdims=True), x.shape)


def _tile_of(ref, c, h):
    return ref[c * CHUNK:(c + 1) * CHUNK, h * HEAD_DIM:(h + 1) * HEAD_DIM]


def _chunks_per_step(N):
    return 4 if N % 4 == 0 else (2 if N % 2 == 0 else 1)


def _delta_fwd(q, k, v, g, beta, comm=None):
    T = q.shape[0]
    H, N = q.shape[1] // HEAD_DIM, T // CHUNK
    cps = _chunks_per_step(N)
    rows = cps * CHUNK

    def body(q_ref, k_ref, v_ref, g_ref, b_ref, o_ref, s_ref, inv_ref, S):
        @pl.when(pl.program_id(0) == 0)
        def _():
            S[...] = jnp.zeros_like(S)

        gv, bv = g_ref[...], b_ref[...]
        pairs = lambda f: tuple(f(c, h) for c in range(cps) for h in range(H))
        S_in = tuple(S[h] for h in range(H))
        for h in range(H):
            s_ref[h, 0] = S_in[h]
        (o, S_new), inv = _chunk_fn(pairs(lambda c, h: _tile_of(q_ref, c, h)), pairs(lambda c, h: _tile_of(k_ref, c, h)),
                                    pairs(lambda c, h: _tile_of(v_ref, c, h)),
                                    pairs(lambda c, h: _sel_lane(gv[c * CHUNK:(c + 1) * CHUNK], h)),
                                    pairs(lambda c, h: _sel_lane(bv[c * CHUNK:(c + 1) * CHUNK], h)), S_in)
        for c in range(cps):
            for h in range(H):
                o_ref[c * CHUNK:(c + 1) * CHUNK, h * HEAD_DIM:(h + 1) * HEAD_DIM] = o[c * H + h]
                inv_ref[h, c] = inv[c * H + h]
        for h in range(H):
            S[h] = S_new[h]

    blk = pl.BlockSpec((rows, H * HEAD_DIM), lambda n: (n, 0))
    gblk = pl.BlockSpec((rows, LANES), lambda n: (n, 0))
    outs, comm_outs = _call(
        body, name="delta_fwd", grid=(N // cps,), in_specs=[blk, blk, blk, gblk, gblk],
        out_specs=[blk, pl.BlockSpec((H, 1, HEAD_DIM, HEAD_DIM), lambda n: (0, n, 0, 0)),
                   pl.BlockSpec((H, cps, CHUNK, CHUNK), lambda n: (0, n, 0, 0))],
        out_shape=[jax.ShapeDtypeStruct((T, H * HEAD_DIM), F32), jax.ShapeDtypeStruct((H, N // cps, HEAD_DIM, HEAD_DIM), F32),
                   jax.ShapeDtypeStruct((H, N, CHUNK, CHUNK), F32)],
        scratch_shapes=[pltpu.VMEM((H, HEAD_DIM, HEAD_DIM), F32)],
        semantics=("arbitrary",), args=(q, k, v, g, beta), comm=comm)
    return outs[0], outs[1], outs[2], comm_outs


def _delta_bwd(q, k, v, g, beta, S0, inv, do, comm=None):
    T = q.shape[0]
    H, N = q.shape[1] // HEAD_DIM, T // CHUNK
    cps = _chunks_per_step(N)
    rows, NS = cps * CHUNK, N // cps

    def body(q_ref, k_ref, v_ref, g_ref, b_ref, s_ref, inv_ref, do_ref, dq_ref, dk_ref, dv_ref, dg_ref, db_ref, dS):
        @pl.when(pl.program_id(0) == 0)
        def _():
            dS[...] = jnp.zeros_like(dS)

        gv, bv = g_ref[...], b_ref[...]
        pairs = lambda f: tuple(f(c, h) for c in range(cps) for h in range(H))
        heads = lambda f: tuple(f(h) for h in range(H))
        _, vjp, _ = jax.vjp(_chunk_fn, pairs(lambda c, h: _tile_of(q_ref, c, h)), pairs(lambda c, h: _tile_of(k_ref, c, h)),
                            pairs(lambda c, h: _tile_of(v_ref, c, h)),
                            pairs(lambda c, h: _sel_lane(gv[c * CHUNK:(c + 1) * CHUNK], h)),
                            pairs(lambda c, h: _sel_lane(bv[c * CHUNK:(c + 1) * CHUNK], h)),
                            heads(lambda h: s_ref[h, 0]), pairs(lambda c, h: inv_ref[h, c]), has_aux=True)
        dq, dk, dv, dgB, dbB, dS_prev, _ = vjp((pairs(lambda c, h: _tile_of(do_ref, c, h)), heads(lambda h: dS[h])))
        for c in range(cps):
            for h in range(H):
                r, sl = slice(c * CHUNK, (c + 1) * CHUNK), slice(h * HEAD_DIM, (h + 1) * HEAD_DIM)
                dq_ref[r, sl] = dq[c * H + h]
                dk_ref[r, sl] = dk[c * H + h]
                dv_ref[r, sl] = dv[c * H + h]
                dg_ref[h, r] = dgB[c * H + h]
                db_ref[h, r] = dbB[c * H + h]
        for h in range(H):
            dS[h] = dS_prev[h]

    blk = pl.BlockSpec((rows, H * HEAD_DIM), lambda n: (NS - 1 - n, 0))
    gblk = pl.BlockSpec((rows, LANES), lambda n: (NS - 1 - n, 0))
    hblk = pl.BlockSpec((H, rows, LANES), lambda n: (0, NS - 1 - n, 0))
    sd = jax.ShapeDtypeStruct
    outs, comm_outs = _call(
        body, name="delta_bwd", grid=(NS,),
        in_specs=[blk, blk, blk, gblk, gblk, pl.BlockSpec((H, 1, HEAD_DIM, HEAD_DIM), lambda n: (0, NS - 1 - n, 0, 0)),
                  pl.BlockSpec((H, cps, CHUNK, CHUNK), lambda n: (0, NS - 1 - n, 0, 0)), blk],
        out_specs=[blk, blk, blk, hblk, hblk],
        out_shape=[sd((T, H * HEAD_DIM), F32)] * 3 + [sd((H, T, LANES), F32)] * 2,
        scratch_shapes=[pltpu.VMEM((H, HEAD_DIM, HEAD_DIM), F32)],
        semantics=("arbitrary",), args=(q, k, v, g, beta, S0, inv, do), comm=comm)
    return (*outs, comm_outs)


def _gnorm_fwd(o, proj, z_coff, gdn_t, DNW):
    T = o.shape[0]

    def fn(j, i, ov, zv, gv):
        def one(oh, zh, gh):
            r = lax.rsqrt(jnp.mean(oh * oh, axis=1, keepdims=True) + EPS)
            return oh * r * gh * (zh * _sigmoid(zh))
        return (_per_head(one, ov, zv, jnp.broadcast_to(gv, ov.shape)),)

    return _tiled(fn, T=T, C=DNW, ins=[("cur", o, None), ("cur", proj, lambda j: j + z_coff), ("row", gdn_t, None)],
                  out_dtypes=[BF16], cb=DNW, name="gnorm_fwd")[0]


def _gnorm_bwd(dymix, y_coff, o, proj, z_coff, gdn_t, DNW):
    T = o.shape[0]
    nh = DNW // HEAD_DIM

    def fn(j, i, dy, ov, zv, gv):
        dos, dzs, dgs = [], [], jnp.zeros((1, HEAD_DIM), F32)
        for h in range(nh):
            sl = slice(h * HEAD_DIM, (h + 1) * HEAD_DIM)
            dyh, oh, zh, gh = dy[:, sl].astype(F32), ov[:, sl], zv[:, sl], gv[:, sl]
            r = lax.rsqrt(jnp.mean(oh * oh, axis=1, keepdims=True) + EPS)
            on = oh * r
            sg = _sigmoid(zh)
            sz = zh * sg
            dzs.append(dyh * on * gh * (sg * (1.0 + zh * (1.0 - sg))))
            don = dyh * gh * sz
            dos.append(r * (don - on * jnp.mean(don * on, axis=1, keepdims=True)))
            dgs = dgs + jnp.sum(dyh * on * sz, axis=0, keepdims=True)
        cat = (lambda xs: xs[0] if nh == 1 else jnp.concatenate(xs, axis=1))
        return cat(dos), cat(dzs), _row0(dgs)

    T_ = T
    nI = T_ // _tile(T_, 256, HALO)
    tb = T_ // nI
    specs_cb = DNW

    def body_wrap():
        def body(dy_ref, o_ref, z_ref, g_ref, do_ref, dz_ref, dg_ref):
            i = pl.program_id(0)
            d_o, d_z, d_g = fn(0, i, dy_ref[...], o_ref[...], z_ref[...], g_ref[...])
            do_ref[...] = d_o
            dz_ref[...] = d_z.astype(dz_ref.dtype)

            @pl.when(i == 0)
            def _():
                dg_ref[...] = d_g

            @pl.when(i > 0)
            def _():
                dg_ref[...] += d_g

        return pl.pallas_call(
            body, name="gnorm_bwd", grid=(nI,),
            in_specs=[pl.BlockSpec((tb, specs_cb), lambda i: (i, y_coff)), pl.BlockSpec((tb, specs_cb), lambda i: (i, 0)),
                      pl.BlockSpec((tb, specs_cb), lambda i: (i, z_coff)), pl.BlockSpec((1, specs_cb), lambda i: (0, 0))],
            out_specs=[pl.BlockSpec((tb, specs_cb), lambda i: (i, 0)), pl.BlockSpec((tb, specs_cb), lambda i: (i, 0)),
                       pl.BlockSpec((HALO, HEAD_DIM), lambda i: (0, 0))],
            out_shape=[jax.ShapeDtypeStruct((T_, DNW), F32), jax.ShapeDtypeStruct((T_, DNW), BF16),
                       jax.ShapeDtypeStruct((HALO, HEAD_DIM), F32)],
            compiler_params=pltpu.CompilerParams(dimension_semantics=("arbitrary",), vmem_limit_bytes=VMEM_LIMIT),
        )(dymix, o, proj, gdn_t)

    return body_wrap()


def _ffn_fwd(up_g, up_v, w_g, w_v, cb):
    T, F = up_g.shape

    def fn(j, i, ug, uv, wg, wv):
        cg = _own(_conv_causal(ug, wg))
        cv = _own(_conv_causal(uv, wv))
        return (cg * _sigmoid(cg) * cv,)

    return _tiled(fn, T=T, C=F, ins=[("ext", up_g, None), ("ext", up_v, None), ("row", w_g, None), ("row", w_v, None)],
                  out_dtypes=[BF16], tb=1024, cb=cb, name="ffn_fwd")[0]


def _ffn_bwd(dact, up_g, up_v, w_g, w_v, cb):
    T, F = up_g.shape
    K = w_g.shape[0]

    def fn(j, i, da, ug, uv, wg, wv):
        cg = _conv_causal(ug, wg)
        cv = _conv_causal(uv, wv)
        sg = _sigmoid(cg)
        dgate = da * cv * (sg * (1.0 + cg * (1.0 - sg)))
        dval = da * (cg * sg)
        return (_own(_conv_anti(dgate, wg)), _own(_conv_anti(dval, wv)), _conv_dw(dgate, ug, K), _conv_dw(dval, uv, K))

    return _tiled(fn, T=T, C=F, ins=[("ext", dact, None), ("ext", up_g, None), ("ext", up_v, None), ("row", w_g, None),
                                      ("row", w_v, None)], out_dtypes=[BF16, BF16], acc_rows=[HALO, HALO], tb=1024, cb=cb,
                  name="ffn_bwd")


def _wide(R, Cc, n_f32, unit=HALO):
    cb = Cc if (Cc % LANES or Cc <= 4096) else _tile(Cc, 2048, LANES)
    cap = max(unit, EW_VMEM_BUDGET // (2 * 4 * n_f32 * cb) // unit * unit)
    return _tile(R, cap, unit), cb


def _adamw(w, g, m, v, name):
    R, Cc = w.shape
    tb, cb = _wide(R, Cc, 7)
    c1 = 1.0 / (1.0 - ADAM_B1 ** ADAM_STEP)
    c2 = 1.0 / (1.0 - ADAM_B2 ** ADAM_STEP)

    def fn(j, i, wv, gv, mv, vv):
        m2 = ADAM_B1 * mv + (1.0 - ADAM_B1) * gv
        v2 = ADAM_B2 * vv + (1.0 - ADAM_B2) * (gv * gv)
        delta = -ADAM_LR * ((m2 * c1) / (jnp.sqrt(v2 * c2) + ADAM_EPS) + ADAM_WD * wv)
        return delta, m2, v2

    return _tiled(fn, T=R, C=Cc, ins=[("cur", w, None), ("cur", g, None), ("cur", m, None), ("cur", v, None)],
                  out_dtypes=[F32, F32, F32], tb=tb, cb=cb, name=name)


def _join_shards(w4, n_main):
    S4, R, cs = w4.shape
    n_small = S4 * cs - n_main
    assert 0 < n_small <= LANES and n_small <= cs
    tb = _tile(R, 256, 2 * HALO)

    def body(w_ref, main_ref, small_ref):
        for t in range(S4 - 1):
            main_ref[:, t * cs:(t + 1) * cs] = w_ref[t]
        last = w_ref[S4 - 1]
        main_ref[:, (S4 - 1) * cs:] = last[:, :cs - n_small]
        small_ref[...] = jnp.zeros_like(small_ref)
        small_ref[:, :n_small] = last[:, cs - n_small:]

    return pl.pallas_call(
        body, name="join_w_in", grid=(R // tb,), in_specs=[pl.BlockSpec((S4, tb, cs), lambda i: (0, i, 0))],
        out_specs=[pl.BlockSpec((tb, n_main), lambda i: (i, 0)), pl.BlockSpec((tb, LANES), lambda i: (i, 0))],
        out_shape=[jax.ShapeDtypeStruct((R, n_main), w4.dtype), jax.ShapeDtypeStruct((R, LANES), w4.dtype)],
        compiler_params=pltpu.CompilerParams(dimension_semantics=("parallel",), vmem_limit_bytes=VMEM_LIMIT))(w4)


def _split_shards(main, small, cs):
    R, n_main = main.shape
    n_small = 4 * cs - n_main
    tb = _tile(R, 256, HALO)

    def body(main_ref, small_ref, out_ref):
        for t in range(3):
            out_ref[t] = main_ref[:, t * cs:(t + 1) * cs]
        out_ref[3, :, :cs - n_small] = main_ref[:, 3 * cs:]
        out_ref[3, :, cs - n_small:] = small_ref[:, :n_small]

    return pl.pallas_call(
        body, name="split_g_in", grid=(R // tb,),
        in_specs=[pl.BlockSpec((tb, n_main), lambda i: (i, 0)), pl.BlockSpec((tb, LANES), lambda i: (i, 0))],
        out_specs=pl.BlockSpec((4, tb, cs), lambda i: (0, i, 0)), out_shape=jax.ShapeDtypeStruct((4, R, cs), main.dtype),
        compiler_params=pltpu.CompilerParams(dimension_semantics=("parallel",), vmem_limit_bytes=VMEM_LIMIT))(main, small)


def _sum_stack(st, name):
    S, R, Cc = st.shape
    cb = _tile(Cc, 512, LANES) if Cc % LANES == 0 else Cc

    def fn(j, i, sv):
        t = sv[0]
        for s in range(1, S):
            t = t + sv[s]
        return (t,)

    return _tiled(fn, T=R, C=Cc, ins=[("stack", st, None)], out_dtypes=[F32], cb=cb, name=name)[0]


ANY = pl.BlockSpec(memory_space=pl.ANY)


def _place():
    x, y, c = lax.axis_index("x"), lax.axis_index("y"), lax.axis_index("c")
    return x, y, c, 2 * x + y


def _chip_dev(s, c):
    return (s // 2, s % 2, c)


class _Comm:
    def __init__(self, ins, out_shapes, sems, start, wait, aliases=None):
        self.ins, self.out_shapes, self.sems = list(ins), list(out_shapes), list(sems)
        self.start, self.wait, self.aliases = start, wait, dict(aliases or {})


def _merge(*comms):
    offs, i, o, s = [], 0, 0, 0
    for cm in comms:
        offs.append((i, o, s))
        i, o, s = i + len(cm.ins), o + len(cm.out_shapes), s + len(cm.sems)

    def part(refs, k, cm):
        i0, o0, s0 = offs[k]
        return refs[0][i0:i0 + len(cm.ins)], refs[1][o0:o0 + len(cm.out_shapes)], refs[2][s0:s0 + len(cm.sems)]

    def start(*refs):
        for k, cm in enumerate(comms):
            cm.start(*part(refs, k, cm))

    def wait(*refs):
        for k, cm in enumerate(comms):
            cm.wait(*part(refs, k, cm))

    aliases = {}
    for k, cm in enumerate(comms):
        for a, b in cm.aliases.items():
            aliases[offs[k][0] + a] = offs[k][1] + b
    return _Comm([a for cm in comms for a in cm.ins], [a for cm in comms for a in cm.out_shapes],
                 [a for cm in comms for a in cm.sems], start, wait, aliases)


def _call(body, *, name, grid, in_specs, out_specs, out_shape, scratch_shapes, semantics, args, comm=None):
    if comm is None:
        outs = pl.pallas_call(
            body, name=name, grid=grid, in_specs=in_specs, out_specs=out_specs, out_shape=out_shape,
            scratch_shapes=list(scratch_shapes),
            compiler_params=pltpu.CompilerParams(dimension_semantics=semantics, vmem_limit_bytes=VMEM_LIMIT))(*args)
        return list(outs), []
    n_in, n_out, n_scr = len(in_specs), len(out_specs), len(scratch_shapes)
    ci, co = len(comm.ins), len(comm.out_shapes)

    def wrapped(*refs):
        r = 0
        ins, r = refs[r:r + n_in], r + n_in
        cins, r = refs[r:r + ci], r + ci
        outs, r = refs[r:r + n_out], r + n_out
        couts, r = refs[r:r + co], r + co
        scr, r = refs[r:r + n_scr], r + n_scr
        csems = refs[r:]
        ids = [pl.program_id(a) for a in range(len(grid))]
        first, last = ids[0] == 0, ids[0] == grid[0] - 1
        for a in range(1, len(grid)):
            first = jnp.logical_and(first, ids[a] == 0)
            last = jnp.logical_and(last, ids[a] == grid[a] - 1)

        @pl.when(first)
        def _():
            comm.start(cins, couts, csems)

        body(*ins, *outs, *scr)

        @pl.when(last)
        def _():
            comm.wait(cins, couts, csems)

    outs = pl.pallas_call(
        wrapped, name=name, grid=grid, in_specs=list(in_specs) + [ANY] * ci, out_specs=list(out_specs) + [ANY] * co,
        out_shape=list(out_shape) + comm.out_shapes, scratch_shapes=list(scratch_shapes) + comm.sems,
        input_output_aliases={n_in + a: n_out + b for a, b in comm.aliases.items()},
        compiler_params=pltpu.CompilerParams(dimension_semantics=("arbitrary",) * len(grid), vmem_limit_bytes=VMEM_LIMIT),
    )(*args, *comm.ins)
    return list(outs[:n_out]), list(outs[n_out:])


def _run_comm(comm, name):
    ci, co = len(comm.ins), len(comm.out_shapes)

    def body(*refs):
        cins, couts, csems = refs[:ci], refs[ci:ci + co], refs[ci + co:]
        comm.start(cins, couts, csems)
        comm.wait(cins, couts, csems)

    outs = pl.pallas_call(body, name=name, in_specs=[ANY] * ci, out_specs=[ANY] * co, out_shape=comm.out_shapes,
                          scratch_shapes=comm.sems, input_output_aliases=comm.aliases)(*comm.ins)
    return list(outs)


def _ag_comm(shard, land=None, q=0, nq=1):
    two, R2, Cc = shard.shape
    rows = pl.ds(q * (R2 // nq), R2 // nq)
    DMA = pltpu.SemaphoreType.DMA

    def copies(ins, outs, sems, which):
        sh, out = ins[0], outs[0]
        send1, recv1, send2, recv2, send0, recv0 = sems
        x, y, c, s = _place()
        sib = (x, y, 1 - c)
        rc = pltpu.make_async_remote_copy
        if which == "first":
            return [rc(sh.at[c, rows], out.at[s, c, rows], send1.at[m - 1], recv1.at[m - 1],
                       device_id=_chip_dev(s ^ m, c), device_id_type=MESH) for m in range(1, 4)]
        if which == "own":
            return [rc(sh.at[h, rows], out.at[s, h, rows], send0.at[h], recv0.at[h], device_id=sib, device_id_type=MESH)
                    for h in range(2)]
        if which == "landed":
            return [rc(sh.at[c, rows], out.at[s ^ m, c, rows], send1.at[m - 1], recv1.at[m - 1], device_id=sib,
                       device_id_type=MESH) for m in range(1, 4)]
        half = c if which == "passed" else 1 - c
        return [rc(out.at[s ^ m, half, rows], out.at[s ^ m, half, rows], send2.at[m - 1], recv2.at[m - 1], device_id=sib,
                   device_id_type=MESH) for m in range(1, 4)]

    def start(ins, outs, sems):
        for cp in copies(ins, outs, sems, "first") + copies(ins, outs, sems, "own"):
            cp.start()

    def wait(ins, outs, sems):
        passed = copies(ins, outs, sems, "passed")
        for lan, pas in zip(copies(ins, outs, sems, "landed"), passed):
            lan.wait_recv()
            pas.start()
        for cp in copies(ins, outs, sems, "handed"):
            cp.wait_recv()
        for cp in copies(ins, outs, sems, "own"):
            cp.wait()
        for cp in copies(ins, outs, sems, "first") + passed:
            cp.wait_send()

    return _Comm([shard] + ([land] if land is not None else []), [jax.ShapeDtypeStruct((4, two, R2, Cc), shard.dtype)],
                 [DMA((3,)), DMA((3,)), DMA((3,)), DMA((3,)), DMA((2,)), DMA((2,))], start, wait,
                 {1: 0} if land is not None else None)


def _a2a_comm(S1, q=0, nq=1, land=None):
    S4, R2, Cc = S1.shape
    rows = pl.ds(q * (R2 // nq), R2 // nq)
    DMA = pltpu.SemaphoreType.DMA

    def copies(ins, outs, sems):
        x, y, c, s = _place()
        return [pltpu.make_async_remote_copy(ins[0].at[s ^ m, rows], outs[0].at[m - 1, rows], sems[0].at[m - 1],
                                             sems[1].at[m - 1], device_id=_chip_dev(s ^ m, c), device_id_type=MESH)
                for m in range(1, 4)]

    def start(ins, outs, sems):
        for cp in copies(ins, outs, sems):
            cp.start()

    def wait(ins, outs, sems):
        for cp in copies(ins, outs, sems):
            cp.wait()

    return _Comm([S1] + ([land] if land is not None else []), [jax.ShapeDtypeStruct((3, R2, Cc), S1.dtype)],
                 [DMA((3,)), DMA((3,))], start, wait, {1: 0} if land is not None else None)


def _halves(G):
    return G.reshape(G.shape[0], 2, G.shape[1] // 2, G.shape[2])


def _swap_comm(piece):
    n, two, R2, Cc = piece.shape
    DMA = pltpu.SemaphoreType.DMA

    def copies(ins, outs, sems):
        x, y, c, s = _place()
        return [pltpu.make_async_remote_copy(ins[0].at[t, 1 - c], outs[0].at[t], sems[0].at[t], sems[1].at[t],
                                             device_id=(x, y, 1 - c), device_id_type=MESH) for t in range(n)]

    def start(ins, outs, sems):
        for cp in copies(ins, outs, sems):
            cp.start()

    def wait(ins, outs, sems):
        for cp in copies(ins, outs, sems):
            cp.wait()

    return _Comm([piece], [jax.ShapeDtypeStruct((n, R2, Cc), piece.dtype)], [DMA((n,)), DMA((n,))], start, wait)


def _add_half(pieces, As, cidx, name):
    R2, Cc = pieces[0].shape[2:]
    S4 = sum(pc.shape[0] for pc in pieces)
    tb, cb = _wide(R2, Cc, 3, 2 * HALO)
    nI, nJ = R2 // tb, Cc // cb

    def body(c_ref, g_ref, a_ref, *rest):
        rest[-1][...] = (g_ref[0, 0] + a_ref[0]).astype(BF16)

    out, t0 = None, 0
    for k, (pc, A) in enumerate(zip(pieces, As)):
        grid_spec = pltpu.PrefetchScalarGridSpec(
            num_scalar_prefetch=1, grid=(pc.shape[0], nI, nJ),
            in_specs=[pl.BlockSpec((1, 1, tb, cb), lambda t, i, j, c_ref: (t, c_ref[0], i, j)),
                      pl.BlockSpec((1, tb, cb), lambda t, i, j, c_ref: (t, i, j))] + ([ANY] if k else []),
            out_specs=pl.BlockSpec((tb, cb), lambda t, i, j, c_ref, t0=t0: ((t0 + t) * nI + i, j)))
        out = pl.pallas_call(
            functools.partial(body), name=f"{name}{k}", grid_spec=grid_spec, out_shape=jax.ShapeDtypeStruct((S4 * R2, Cc), BF16),
            input_output_aliases={3: 0} if k else {},
            compiler_params=pltpu.CompilerParams(dimension_semantics=("parallel", "parallel", "parallel"),
                                                 vmem_limit_bytes=VMEM_LIMIT),
        )(*((cidx, pc, A) + ((out,) if k else ())))
        t0 += pc.shape[0]
    return out.reshape(S4, R2, Cc)


def _add_own(S1, B, chip_idx, cidx, name):
    S4, R2, Cc = S1.shape
    tb, cb = _wide(R2, Cc, 3, 2 * HALO)

    def body(s_idx, c_idx, s_ref, b_ref, o_ref):
        o_ref[...] = ((s_ref[0].astype(F32) + b_ref[0].astype(F32)) + b_ref[1].astype(F32)) + b_ref[2].astype(F32)

    grid_spec = pltpu.PrefetchScalarGridSpec(
        num_scalar_prefetch=2, grid=(R2 // tb, Cc // cb),
        in_specs=[pl.BlockSpec((1, tb, cb), lambda i, j, s_idx, c_idx: (s_idx[0], i, j)),
                  pl.BlockSpec((3, tb, cb), lambda i, j, s_idx, c_idx: (0, i, j))],
        out_specs=pl.BlockSpec((None, tb, cb), lambda i, j, s_idx, c_idx: (c_idx[0], i, j)))
    return pl.pallas_call(body, name=name, grid_spec=grid_spec, out_shape=jax.ShapeDtypeStruct((2, R2, Cc), F32),
                          compiler_params=pltpu.CompilerParams(dimension_semantics=("parallel", "parallel"),
                                                               vmem_limit_bytes=VMEM_LIMIT))(chip_idx, cidx, S1, B)


def _sibling_fill(Hs, name):
    def body(h_ref, out_ref, send, recv):
        x, y, c, s = _place()
        cp = pltpu.make_async_remote_copy(h_ref.at[c], out_ref.at[c], send, recv, device_id=(x, y, 1 - c), device_id_type=MESH)
        cp.start()
        cp.wait()

    return pl.pallas_call(
        body, name=name, in_specs=[ANY], out_specs=ANY, out_shape=jax.ShapeDtypeStruct(Hs.shape, Hs.dtype),
        input_output_aliases={0: 0}, scratch_shapes=[pltpu.SemaphoreType.DMA, pltpu.SemaphoreType.DMA],
    )(Hs)


def _gather_all(buf, name):
    R, Cc = buf.shape

    def body(b_ref, out_ref, send, recv, local):
        x, y, c, s = _place()
        d = 2 * s + c
        mine = pltpu.make_async_copy(b_ref, out_ref.at[d], local)
        mine.start()
        cps = []
        for m in range(1, 8):
            t = d ^ m
            cp = pltpu.make_async_remote_copy(b_ref, out_ref.at[d], send.at[m - 1], recv.at[m - 1],
                                              device_id=(t // 4, (t // 2) % 2, t % 2), device_id_type=MESH)
            cp.start()
            cps.append(cp)
        for cp in cps:
            cp.wait()
        mine.wait()

    return pl.pallas_call(
        body, name=name, in_specs=[ANY], out_specs=ANY, out_shape=jax.ShapeDtypeStruct((8, R, Cc), buf.dtype),
        scratch_shapes=[pltpu.SemaphoreType.DMA((7,)), pltpu.SemaphoreType.DMA((7,)), pltpu.SemaphoreType.DMA],
    )(buf)


def _finish_shard(S1, B, cidx, chip_idx, name):
    Hs = _sibling_fill(_add_own(S1, B, chip_idx, cidx, name + "_sum"), name + "_gather")
    return Hs.reshape(2 * Hs.shape[1], Hs.shape[2])


def _pack_rows(vs):
    flat = jnp.concatenate([v.reshape(-1) for v in vs])
    n = flat.shape[0]
    rows = -(-n // (LANES * 2 * HALO)) * 2 * HALO
    return jnp.pad(flat, (0, rows * LANES - n)).reshape(rows, LANES)


def _unpack_rows(buf, shapes):
    flat = buf.reshape(-1)
    outs, o = [], 0
    for shp in shapes:
        n = 1
        for d in shp:
            n *= d
        outs.append(flat[o:o + n].reshape(shp))
        o += n
    return outs


def kernel(x, p, norm_mix_g, w_in, conv_a_w, conv_qkv_w, a_log, dt_bias, dn_norm_g, w_out, norm_ffn_g, w_up, conv_ffn_w, w_down, norm_ple_g, w_ple_gate, w_ple_proj, final_norm_g, loss_target, m_norm_mix_g, m_w_in, m_conv_a_w, m_conv_qkv_w, m_a_log, m_dt_bias, m_dn_norm_g, m_w_out, m_norm_ffn_g, m_w_up, m_conv_ffn_w, m_w_down, m_norm_ple_g, m_w_ple_gate, m_w_ple_proj, m_final_norm_g, v_norm_mix_g, v_w_in, v_conv_a_w, v_conv_qkv_w, v_a_log, v_dt_bias, v_dn_norm_g, v_w_out, v_norm_ffn_g, v_w_up, v_conv_ffn_w, v_w_down, v_norm_ple_g, v_w_ple_gate, v_w_ple_proj, v_final_norm_g):
    xs = x[0]
    ps = p[0, 0]
    tgt = loss_target[0]
    T, D = xs.shape
    H = a_log.shape[-1]
    DNW = H * HEAD_DIM
    CW = conv_a_w.shape[-1] * 4
    F = w_down.shape[1] * 4
    PD = ps.shape[-1]
    IN_MAIN = 3 * CW + 4 * DNW
    IN_COLS = IN_MAIN + 2 * H
    assert w_in.shape[-1] * 4 == IN_COLS and CW + DNW == D and 2 * H <= LANES
    cb = _tile(min(CW, DNW), 512, LANES)
    while F % cb:
        cb -= LANES
    cidx = lax.axis_index("c").astype(jnp.int32).reshape(1)
    chip = 2 * lax.axis_index("x") + lax.axis_index("y")

    def halves(w):
        sh = w[0].astype(BF16)
        return sh.reshape(2, sh.shape[0] // 2, sh.shape[1])

    def whole(land):
        return land.reshape(4, 2 * land.shape[2], land.shape[3])

    def rows(g4):
        return g4.reshape(4 * g4.shape[1], g4.shape[2])

    conv_shapes = [conv_a_w[0].shape, conv_qkv_w[0].shape, conv_ffn_w[0].shape]
    cpack = _pack_rows([conv_a_w[0], conv_qkv_w[0], conv_ffn_w[0]])
    sh_in, sh_out, sh_up, sh_down, sh_pg, sh_pp = (halves(w) for w in (w_in, w_out, w_up, w_down, w_ple_gate, w_ple_proj))
    l_in, cg = _run_comm(_merge(_ag_comm(sh_in), _ag_comm(cpack.reshape(2, cpack.shape[0] // 2, LANES))), "ag_w_in_conv")
    w_in_main, w_in_small = _join_shards(whole(l_in), IN_MAIN)
    cg = cg.reshape(4, cpack.shape[0], LANES)
    parts = [_unpack_rows(cg[t], conv_shapes) for t in range(4)]
    cw_a = jnp.concatenate([parts[t][0] for t in range(4)], axis=1)
    cw_qkv = jnp.concatenate([parts[t][1] for t in range(4)], axis=1)
    cw_ffn = jnp.concatenate([parts[t][2] for t in range(4)], axis=1)
    cw_q, cw_k, cw_v = cw_qkv[:, :DNW], cw_qkv[:, DNW:2 * DNW], cw_qkv[:, 2 * DNW:]
    cw_fg, cw_fv = cw_ffn[:, :F], cw_ffn[:, F:]
    pad_row = lambda v: jnp.pad(v, ((0, 0), (0, LANES - v.shape[1])))
    a_log_row, dt_row = pad_row(a_log), pad_row(dt_bias)
    gdn_t = jnp.tile(dn_norm_g, (1, H))
    gfin = final_norm_g.reshape(1, D)

    h1 = _rms_fwd(xs, norm_mix_g, "rms1")
    proj, (l_up,) = _mm(h1, w_in_main, mode="nn", out_dtypes=[F32], name="mm_proj", comm=_ag_comm(sh_up, q=0, nq=2))
    small = _mm(h1, w_in_small, mode="nn", out_dtypes=[F32], name="mm_small")
    ya = _ga_fwd(proj, cw_a, CW, cb)
    nq = 3 * CW // cb
    nd = DNW // cb
    qn = _qkv_fwd(proj, cw_q, nq, True, DNW, cb, "q_fwd")
    kn = _qkv_fwd(proj, cw_k, nq + nd, True, DNW, cb, "k_fwd")
    vs = _qkv_fwd(proj, cw_v, nq + 2 * nd, False, DNW, cb, "v_fwd")
    g, beta = _gb_fwd(small, a_log_row, dt_row, H)
    o, S0, inv_c, (l_up, l_out) = _delta_fwd(qn, kn, vs, g, beta, comm=_merge(_ag_comm(sh_up, l_up, q=1, nq=2), _ag_comm(sh_out)))
    w_out_f = rows(whole(l_out))
    w_out_a, w_out_b = w_out_f[:CW], w_out_f[CW:]
    w_up_4 = whole(l_up)
    z_coff = (3 * CW + 3 * DNW) // DNW
    assert (3 * CW + 3 * DNW) % DNW == 0 and CW % DNW == 0
    yb = _gnorm_fwd(o, proj, z_coff, gdn_t, DNW)
    add = lambda acc, r: (r + acc,)
    x1 = _mm(ya, w_out_a, mode="nn", out_dtypes=[F32], epi=add, extras=[xs], name="mm_out_a")
    x1 = _mm(yb, w_out_b, mode="nn", out_dtypes=[F32], epi=add, extras=[x1], name="mm_out_b")
    h2 = _rms_fwd(x1, norm_ffn_g, "rms2")
    up_g, (l_down,) = _mm(h2, w_up_4, mode="nn", b_split=(0, 2), out_dtypes=[F32], name="mm_up_g",
                          comm=_ag_comm(sh_down, q=0, nq=2))
    up_v, (l_down,) = _mm(h2, w_up_4, mode="nn", b_split=(2, 2), out_dtypes=[F32], name="mm_up_v",
                          comm=_ag_comm(sh_down, l_down, q=1, nq=2))
    w_down_f = rows(whole(l_down))
    act = _ffn_fwd(up_g, up_v, cw_fg, cw_fv, cb)
    x2, (l_pg, l_pp) = _mm(act, w_down_f, mode="nn", out_dtypes=[F32], epi=add, extras=[x1], name="mm_down",
                           comm=_merge(_ag_comm(sh_pg), _ag_comm(sh_pp)))
    w_pg_f = rows(whole(l_pg))
    w_pp_4 = whole(l_pp)
    h3 = _rms_fwd(x2, norm_ple_g, "rms3")
    pp = _mm(ps, w_pp_4, mode="nn", b_split=(0, 4), out_dtypes=[F32], name="mm_pp")

    def ple_epi(acc, x2v, ppv):
        pg = _sigmoid(acc)
        return x2v + pg * ppv, pg

    x3, pg = _mm(h3, w_pg_f, mode="nn", out_dtypes=[F32, F32], epi=ple_epi, extras=[x2, pp], name="mm_pg")

    dx3, dpg, dpp, fin = _final_fb(x3, tgt, gfin, pp, pg)
    loss = lax.psum(jnp.sum(fin[1]), ("x", "y", "c"))
    d_gfin = fin[0:1]
    def split_rows(dW):
        return dW.reshape(4, dW.shape[0] // 4, dW.shape[1])

    dW_pp = _mm(ps, dpp, mode="tn", out_split=4, out_dtypes=[F32], name="mm_dw_pp")
    dW_pg = _mm(h3, dpg, mode="tn", out_dtypes=[F32], name="mm_dw_pg")
    P_pp, P_pg = _halves(dW_pp), _halves(split_rows(dW_pg))
    dh3, (A_pp, A_pg) = _mm(dpg, w_pg_f, mode="nt", out_dtypes=[F32], name="mm_dh3",
                            comm=_merge(_swap_comm(P_pp), _swap_comm(P_pg)))
    S_pp = _add_half([P_pp], [A_pp], cidx, "rs_w_pp_add")
    S_pg = _add_half([P_pg], [A_pg], cidx, "rs_w_pg_add")
    dx2, dx2_b, d_gple = _rms_bwd(dh3, x2, norm_ple_g, dx3, "rms3_bwd")
    dW_down, (B_pp, B_pg) = _mm(act, dx2_b, mode="tn", out_dtypes=[F32], name="mm_dw_down",
                                comm=_merge(_a2a_comm(S_pp), _a2a_comm(S_pg)))
    P_down = _halves(split_rows(dW_down))
    dact, (A_down,) = _mm(dx2_b, w_down_f, mode="nt", out_dtypes=[F32], name="mm_dact", comm=_swap_comm(P_down))
    S_down = _add_half([P_down], [A_down], cidx, "rs_w_down_add")
    dup_g, dup_v, dcw_fg, dcw_fv = _ffn_bwd(dact, up_g, up_v, cw_fg, cw_fv, cb)
    dW_up_g, (B_down,) = _mm(h2, dup_g, mode="tn", out_split=2, out_dtypes=[F32], name="mm_dw_up_g", comm=_a2a_comm(S_down))
    P_ug = _halves(dW_up_g)
    dW_up_v, (A_ug,) = _mm(h2, dup_v, mode="tn", out_split=2, out_dtypes=[F32], name="mm_dw_up_v", comm=_swap_comm(P_ug))
    P_uv = _halves(dW_up_v)
    dh2, (A_uv,) = _mm(dup_g, w_up_4, mode="nt", b_split=(0, 2), out_dtypes=[F32], name="mm_dh2_g", comm=_swap_comm(P_uv))
    S_up = _add_half([P_ug, P_uv], [A_ug, A_uv], cidx, "rs_w_up_add")
    dh2 = _mm(dup_v, w_up_4, mode="nt", b_split=(2, 2), out_dtypes=[F32], epi=add, extras=[dh2], name="mm_dh2_v")
    dx1, dx1_b, d_gffn = _rms_bwd(dh2, x1, norm_ffn_g, dx2, "rms2_bwd")
    dW_out_a = _mm(ya, dx1_b, mode="tn", out_dtypes=[F32], name="mm_dw_out_a")
    dW_out_b = _mm(yb, dx1_b, mode="tn", out_dtypes=[F32], name="mm_dw_out_b")
    P_oa, P_ob = _halves(dW_out_a.reshape(-1, D // 4, D)), _halves(dW_out_b.reshape(-1, D // 4, D))
    dymix, (A_oa, A_ob) = _mm(dx1_b, w_out_f, mode="nt", out_dtypes=[F32], name="mm_dymix",
                              comm=_merge(_swap_comm(P_oa), _swap_comm(P_ob)))
    S_out = _add_half([P_oa, P_ob], [A_oa, A_ob], cidx, "rs_w_out_add")
    dax, dab, dac, dcw_a = _ga_bwd(dymix, proj, cw_a, CW, cb)
    do, dz, d_gdn = _gnorm_bwd(dymix, CW // DNW, o, proj, z_coff, gdn_t, DNW)
    dqn, dkn, dvs, dgB, dbB, (B_up, B_out) = _delta_bwd(qn, kn, vs, g, beta, S0, inv_c, do,
                                                        comm=_merge(_a2a_comm(S_up), _a2a_comm(S_out)))
    dq_pre, dcw_q = _qkv_bwd(dqn, proj, cw_q, nq, True, DNW, cb, "q_bwd")
    dk_pre, dcw_k = _qkv_bwd(dkn, proj, cw_k, nq + nd, True, DNW, cb, "k_bwd")
    dv_pre, dcw_v = _qkv_bwd(dvs, proj, cw_v, nq + 2 * nd, False, DNW, cb, "v_bwd")
    dsmall, d_ab = _gb_bwd(dgB, dbB, small, g, beta, a_log_row, dt_row, H)
    dproj = jnp.concatenate([dax, dab, dac, dq_pre, dk_pre, dv_pre, dz], axis=1)
    dW_in_main = _mm(h1, dproj, mode="tn", out_dtypes=[F32], name="mm_dw_in")
    dW_in_small = _mm(h1, dsmall, mode="tn", out_dtypes=[F32], name="mm_dw_in_small")
    P_in = _halves(_split_shards(dW_in_main, dW_in_small, IN_COLS // 4))
    (A_in,) = _run_comm(_swap_comm(P_in), "rs_w_in_swap")
    S_in = _add_half([P_in], [A_in], cidx, "rs_w_in_add")
    dh1, (B_in,) = _mm(dproj, w_in_main, mode="nt", out_dtypes=[F32], name="mm_dh1", comm=_a2a_comm(S_in))
    dh1 = _mm(dsmall, w_in_small, mode="nt", out_dtypes=[F32], epi=add, extras=[dh1], name="mm_dh1_small")
    dx, _, d_gmix = _rms_bwd(dh1, xs, norm_mix_g, dx1, "rms1_bwd")

    chip_idx = chip.astype(jnp.int32).reshape(1)

    def update(S1, B, w, m, v, name):
        gr = _finish_shard(S1, B, cidx, chip_idx, "rs_" + name)
        delta, m2, v2 = _adamw(w[0], gr, m[0], v[0], "adamw_" + name)
        return gr[None], delta[None], m2[None], v2[None]

    big = {
        "w_in": update(S_in, B_in, w_in, m_w_in, v_w_in, "w_in"),
        "w_out": update(S_out, B_out, w_out, m_w_out, v_w_out, "w_out"),
        "w_up": update(S_up, B_up, w_up, m_w_up, v_w_up, "w_up"),
        "w_down": update(S_down, B_down, w_down, m_w_down, v_w_down, "w_down"),
        "w_ple_gate": update(S_pg, B_pg, w_ple_gate, m_w_ple_gate, v_w_ple_gate, "w_pg"),
        "w_ple_proj": update(S_pp, B_pp, w_ple_proj, m_w_ple_proj, v_w_ple_proj, "w_pp"),
    }

    small_grads = [d_gmix[0:1], dcw_a[:cw_a.shape[0]], jnp.concatenate([dcw_q, dcw_k, dcw_v], axis=1)[:cw_qkv.shape[0]],
                   d_ab[0:1, :H], d_ab[1:2, :H], d_gdn[0:1], d_gffn[0:1],
                   jnp.concatenate([dcw_fg, dcw_fv], axis=1)[:cw_ffn.shape[0]], d_gple[0:1], d_gfin]
    small_shapes = [v.shape for v in small_grads]
    gpack = _pack_rows(small_grads)
    gsum = _sum_stack(_gather_all(gpack, "ag_small"), "sum_small")
    (g_gmix, g_cwa, g_cwqkv, g_alog, g_dt, g_gdn, g_gffn, g_cwffn, g_gple, g_gfin) = _unpack_rows(gsum, small_shapes)

    def my_cols(v):
        Cc = v.shape[1] // 4
        return lax.dynamic_slice_in_dim(v, chip * Cc, Cc, axis=1)

    g_small = [g_gmix, my_cols(g_cwa), my_cols(g_cwqkv), g_alog, g_dt, g_gdn, g_gffn, my_cols(g_cwffn), g_gple, g_gfin]
    w_small = [norm_mix_g, conv_a_w[0], conv_qkv_w[0], a_log, dt_bias, dn_norm_g, norm_ffn_g, conv_ffn_w[0], norm_ple_g, gfin]
    m_small = [m_norm_mix_g, m_conv_a_w[0], m_conv_qkv_w[0], m_a_log, m_dt_bias, m_dn_norm_g, m_norm_ffn_g, m_conv_ffn_w[0],
               m_norm_ple_g, m_final_norm_g.reshape(1, D)]
    v_small = [v_norm_mix_g, v_conv_a_w[0], v_conv_qkv_w[0], v_a_log, v_dt_bias, v_dn_norm_g, v_norm_ffn_g, v_conv_ffn_w[0],
               v_norm_ple_g, v_final_norm_g.reshape(1, D)]
    shp = [v.shape for v in w_small]
    ds_, ms_, vs_ = _adamw(_pack_rows(w_small), _pack_rows(g_small), _pack_rows(m_small), _pack_rows(v_small), "adamw_small")
    out_shapes = [norm_mix_g.shape, conv_a_w.shape, conv_qkv_w.shape, a_log.shape, dt_bias.shape, dn_norm_g.shape,
                  norm_ffn_g.shape, conv_ffn_w.shape, norm_ple_g.shape, final_norm_g.shape]
    rs = lambda vals: [v.reshape(s) for v, s in zip(vals, out_shapes)]
    sg, sd_, sm_, sv_ = rs(g_small), rs(_unpack_rows(ds_, shp)), rs(_unpack_rows(ms_, shp)), rs(_unpack_rows(vs_, shp))
    names_small = ["norm_mix_g", "conv_a_w", "conv_qkv_w", "a_log", "dt_bias", "dn_norm_g", "norm_ffn_g", "conv_ffn_w",
                   "norm_ple_g", "final_norm_g"]
    res = {n: (sg[i], sd_[i], sm_[i], sv_[i]) for i, n in enumerate(names_small)}
    res.update(big)
    order = ["norm_mix_g", "w_in", "conv_a_w", "conv_qkv_w", "a_log", "dt_bias", "dn_norm_g", "w_out", "norm_ffn_g", "w_up",
             "conv_ffn_w", "w_down", "norm_ple_g", "w_ple_gate", "w_ple_proj", "final_norm_g"]
    return (loss, dx[None], *[res[n][0] for n in order], *[res[n][1] for n in order], *[res[n][2] for n in order],
            *[res[n][3] for n in order])
```

```python
import functools

import jax
import jax.numpy as jnp
from jax import lax
from jax.experimental import pallas as pl
from jax.experimental.pallas import tpu as pltpu

F32 = jnp.float32
BF16 = jnp.bfloat16
LANES = 128
HALO = 8
HEAD_DIM = 128
CHUNK = 64
EPS = 1e-6
VMEM_LIMIT = 56 * 1024 * 1024
MM_VMEM_BUDGET = 40 * 1024 * 1024
MM_STEP_BYTES = 1 << 20
EW_VMEM_BUDGET = 28 * 1024 * 1024
MESH = pl.DeviceIdType.MESH

ADAM_LR, ADAM_B1, ADAM_B2, ADAM_EPS, ADAM_WD, ADAM_STEP = 0.001, 0.9, 0.999, 1e-08, 0.01, 10


def _tile(n, cap, unit):
    if n <= cap:
        return n
    d = (cap // unit) * unit
    while d >= unit:
        if n % d == 0:
            return d
        d -= unit
    raise ValueError(f"no tile for {n} (cap {cap}, unit {unit})")


def _sigmoid(x):
    return 1.0 / (1.0 + jnp.exp(-x))


def _divisors(n, cap):
    ds = [d for d in range(cap // LANES * LANES, 0, -LANES) if n % d == 0]
    return [n] if (n <= cap or not ds) else ds


def _mm_tiles(M, N, K, n_unit, k_unit, a_bytes, n_blocks_mn, a_transposed):
    best = None
    for tm in _divisors(M, 1536):
        for tn in _divisors(n_unit, 1536):
            for tk in _divisors(k_unit, 4096):
                nk = K // tk
                vmem = 2 * tm * tk * a_bytes + 2 * tk * tn * 2 + 2 * 4 * tm * tn * n_blocks_mn + (4 * tm * tn if nk > 1 else 0)
                if vmem > MM_VMEM_BUDGET:
                    continue
                steps = (M // tm) * (N // tn) * nk
                cost = (M * K * a_bytes * (N // tn if nk > 1 else 1) + K * N * 2 * (M // tm) + 4 * M * N * n_blocks_mn
                        + (8 * M * N * nk // 3 if nk > 1 else 0) + steps * MM_STEP_BYTES
                        + (2 * steps * tm * tk if a_transposed else 0))
                if best is None or cost < best[0]:
                    best = (cost, tm, tn, tk)
    return best[1:]


def _mm(a, b, *, mode, out_dtypes, name, epi=None, extras=(), comm=None, b_split=None, out_split=None):
    if b_split is not None:
        lo, ns = b_split
        Rb, Cb = b.shape[1], b.shape[2]
    if mode == "nn":
        (M, K), N = a.shape, (ns * Cb if b_split else b.shape[1])
    elif mode == "nt":
        (M, K), N = a.shape, (Rb if b_split else b.shape[0])
    else:
        (K, M), N = a.shape, b.shape[1]
    n_ex, n_out = len(extras), len(out_dtypes)
    n_unit = Cb if (b_split and mode == "nn") else (N // out_split if out_split else N)
    k_unit = Cb if (b_split and mode == "nt") else K
    tm, tn, tk = _mm_tiles(M, N, K, n_unit, k_unit, a.dtype.itemsize, n_ex + n_out, mode == "tn")
    nk = K // tk
    a_spec = pl.BlockSpec((tk, tm), lambda i, j, k: (k, i)) if mode == "tn" else pl.BlockSpec((tm, tk), lambda i, j, k: (i, k))
    if b_split and mode == "nn":
        nb = Cb // tn
        b_spec = pl.BlockSpec((None, tk, tn), lambda i, j, k: (lo + j // nb, k, j % nb))
    elif b_split:
        nb = Cb // tk
        b_spec = pl.BlockSpec((None, tn, tk), lambda i, j, k: (lo + k // nb, j, k % nb))
    else:
        b_spec = pl.BlockSpec((tn, tk), lambda i, j, k: (j, k)) if mode == "nt" else pl.BlockSpec((tk, tn), lambda i, j, k: (k, j))
    mn_spec = pl.BlockSpec((tm, tn), lambda i, j, k: (i, j))
    out_shapes = [jax.ShapeDtypeStruct((M, N), dt) for dt in out_dtypes]
    out_specs = [mn_spec] * n_out
    if out_split:
        assert n_ex == 0 and n_out == 1
        nbo = (N // out_split) // tn
        out_specs = [pl.BlockSpec((None, tm, tn), lambda i, j, k: (j // nbo, i, j % nbo))]
        out_shapes = [jax.ShapeDtypeStruct((out_split, M, N // out_split), out_dtypes[0])]
    dims = {"nn": (((1,), (0,)), ((), ())), "nt": (((1,), (1,)), ((), ())), "tn": (((0,), (0,)), ((), ()))}[mode]

    def body(*refs):
        a_ref, b_ref = refs[0], refs[1]
        ex_refs = refs[2:2 + n_ex]
        out_refs = refs[2 + n_ex:2 + n_ex + n_out]
        part = lax.dot_general(a_ref[...].astype(BF16), b_ref[...].astype(BF16), dims, preferred_element_type=F32)

        def finish(acc):
            outs = (acc,) if epi is None else epi(acc, *[r[...] for r in ex_refs])
            for r, o in zip(out_refs, outs):
                r[...] = o.astype(r.dtype)

        if nk == 1:
            finish(part)
            return
        acc_ref = refs[-1]
        k = pl.program_id(2)

        @pl.when(k == 0)
        def _():
            acc_ref[...] = part

        @pl.when(jnp.logical_and(k > 0, k < nk - 1))
        def _():
            acc_ref[...] += part

        @pl.when(k == nk - 1)
        def _():
            finish(acc_ref[...] + part)

    outs, comm_outs = _call(
        body, name=name, grid=(M // tm, N // tn, nk),
        in_specs=[a_spec, b_spec] + [mn_spec] * n_ex,
        out_specs=out_specs,
        out_shape=out_shapes,
        scratch_shapes=[pltpu.VMEM((tm, tn), F32)] if nk > 1 else [],
        semantics=("parallel", "parallel", "arbitrary"), args=(a, b, *extras), comm=comm)
    res = outs[0] if n_out == 1 else outs
    return res if comm is None else (res, comm_outs)


def _tiled(fn, *, T, C, ins, out_dtypes=(), acc_rows=(), tb=None, cb=512, name, comm=None):
    tb = _tile(T, tb or (512 if cb <= 1024 else 256), HALO)
    nI, nJ = T // tb, C // cb
    hb, nH = tb // HALO, T // HALO
    specs, args, kinds = [], [], []
    for kind, arr, cmap in ins:
        cm = cmap if cmap is not None else (lambda j: j)
        kinds.append(kind)
        if kind == "cur":
            specs.append(pl.BlockSpec((tb, cb), lambda j, i, cm=cm: (i, cm(j))))
            args.append(arr)
        elif kind == "ext":
            specs.append(pl.BlockSpec((HALO, cb), lambda j, i, cm=cm: (jnp.maximum(i * hb - 1, 0), cm(j))))
            specs.append(pl.BlockSpec((tb, cb), lambda j, i, cm=cm: (i, cm(j))))
            specs.append(pl.BlockSpec((HALO, cb), lambda j, i, cm=cm: (jnp.minimum((i + 1) * hb, nH - 1), cm(j))))
            args += [arr, arr, arr]
        elif kind == "row":
            specs.append(pl.BlockSpec((arr.shape[0], cb), lambda j, i, cm=cm: (0, cm(j))))
            args.append(arr)
        elif kind == "stack":
            specs.append(pl.BlockSpec((arr.shape[0], tb, cb), lambda j, i, cm=cm: (0, i, cm(j))))
            args.append(arr)
        else:
            raise ValueError(kind)
    n_in = len(args)
    n_out, n_acc = len(out_dtypes), len(acc_rows)

    def body(*refs):
        j, i = pl.program_id(0), pl.program_id(1)
        vals, r = [], 0
        for kind in kinds:
            if kind == "ext":
                prev = jnp.where(i == 0, 0.0, refs[r][...].astype(F32))
                cur = refs[r + 1][...].astype(F32)
                nxt = jnp.where(i == nI - 1, 0.0, refs[r + 2][...].astype(F32))
                vals.append(jnp.concatenate([prev, cur, nxt], axis=0))
                r += 3
            else:
                vals.append(refs[r][...])
                r += 1
        res = fn(j, i, *vals)
        for ref, o in zip(refs[n_in:n_in + n_out], res[:n_out]):
            ref[...] = o.astype(ref.dtype)
        for ref, o in zip(refs[n_in + n_out:], res[n_out:]):
            @pl.when(i == 0)
            def _(ref=ref, o=o):
                ref[...] = o

            @pl.when(i > 0)
            def _(ref=ref, o=o):
                ref[...] += o

    outs, comm_outs = _call(
        body, name=name, grid=(nJ, nI), in_specs=specs,
        out_specs=[pl.BlockSpec((tb, cb), lambda j, i: (i, j))] * n_out
        + [pl.BlockSpec((rows, cb), lambda j, i: (0, j)) for rows in acc_rows],
        out_shape=[jax.ShapeDtypeStruct((T, C), dt) for dt in out_dtypes]
        + [jax.ShapeDtypeStruct((rows, C), F32) for rows in acc_rows],
        scratch_shapes=[], semantics=("parallel", "arbitrary"), args=args, comm=comm)
    return outs if comm is None else (outs, comm_outs)


def _conv_causal(xe, w):
    K = w.shape[0]
    y = xe * w[K - 1:K]
    for j in range(K - 1):
        y = y + pltpu.roll(xe, K - 1 - j, 0) * w[j:j + 1]
    return y


def _conv_anti(de, w):
    K, n = w.shape[0], de.shape[0]
    y = de * w[K - 1:K]
    for j in range(K - 1):
        y = y + pltpu.roll(de, n - (K - 1 - j), 0) * w[j:j + 1]
    return y


def _conv_dw(dce, xe, K):
    n = dce.shape[0]
    tb = n - 2 * HALO
    rows = []
    for j in range(K):
        xs = xe if j == K - 1 else pltpu.roll(xe, K - 1 - j, 0)
        rows.append(jnp.sum((dce * xs)[HALO:HALO + tb], axis=0, keepdims=True))
    rows.append(jnp.zeros((HALO - K, dce.shape[1]), F32))
    return jnp.concatenate(rows, axis=0)


def _own(xe):
    return xe[HALO:xe.shape[0] - HALO]


def _row0(v):
    return jnp.concatenate([v, jnp.zeros((HALO - 1, v.shape[1]), F32)], axis=0)


def _per_head(fn, *xs):
    n = xs[0].shape[1] // HEAD_DIM
    outs = [fn(*[x[:, g * HEAD_DIM:(g + 1) * HEAD_DIM] for x in xs]) for g in range(n)]
    return outs[0] if n == 1 else jnp.concatenate(outs, axis=1)


def _rms_fwd(x, g, name):
    T, D = x.shape

    def fn(j, i, xv, gv):
        r = lax.rsqrt(jnp.mean(xv * xv, axis=1, keepdims=True) + EPS)
        return (xv * r * gv,)

    return _tiled(fn, T=T, C=D, ins=[("cur", x, None), ("row", g, None)], out_dtypes=[BF16], cb=D, name=name)[0]


def _rms_bwd_math(dy, xv, gv):
    r = lax.rsqrt(jnp.mean(xv * xv, axis=1, keepdims=True) + EPS)
    xh = xv * r
    dxh = dy * gv
    dx = r * (dxh - xh * jnp.mean(dxh * xh, axis=1, keepdims=True))
    dg = jnp.sum(dy * xh, axis=0, keepdims=True)
    return dx, dg


def _rms_bwd(dh, x, g, dres, name):
    T, D = x.shape

    def fn(j, i, dhv, xv, gv, dr):
        dx, dg = _rms_bwd_math(dhv, xv, gv)
        return dr + dx, dr + dx, _row0(dg)

    return _tiled(fn, T=T, C=D, ins=[("cur", dh, None), ("cur", x, None), ("row", g, None), ("cur", dres, None)],
                  out_dtypes=[F32, BF16], acc_rows=[HALO], cb=D, name=name)


def _final_fb(x3, tgt, g, pp, pg):
    T, D = x3.shape

    def fn(j, i, xv, tv, gv, ppv, pgv):
        r = lax.rsqrt(jnp.mean(xv * xv, axis=1, keepdims=True) + EPS)
        xh = xv * r
        e = xh * gv - tv
        dy = e * (1.0 / D)
        dxh = dy * gv
        dx = r * (dxh - xh * jnp.mean(dxh * xh, axis=1, keepdims=True))
        dg = jnp.sum(dy * xh, axis=0, keepdims=True)
        ls = jnp.sum(e * e, axis=0, keepdims=True) * (0.5 / D)
        return (dx, dx * ppv * pgv * (1.0 - pgv), dx * pgv,
                jnp.concatenate([dg, ls, jnp.zeros((HALO - 2, D), F32)], axis=0))

    return _tiled(fn, T=T, C=D, ins=[("cur", x3, None), ("cur", tgt, None), ("row", g, None), ("cur", pp, None),
                                      ("cur", pg, None)], out_dtypes=[F32, BF16, BF16], acc_rows=[HALO], cb=D, name="final_fb")


def _ga_fwd(proj, w_a, CW, cb):
    T = proj.shape[0]
    n = CW // cb

    def fn(j, i, ax, ab, ac, w):
        c = _conv_causal(ac * ax, w)
        return (ab * _own(c),)

    return _tiled(fn, T=T, C=CW, ins=[("ext", proj, None), ("cur", proj, lambda j: j + n), ("ext", proj, lambda j: j + 2 * n),
                                       ("row", w_a, None)], out_dtypes=[BF16], cb=cb, name="ga_fwd")[0]


def _ga_bwd(dymix, proj, w_a, CW, cb):
    T = proj.shape[0]
    n = CW // cb
    K = w_a.shape[0]

    def fn(j, i, dy, ax, ab, ac, w):
        u = ac * ax
        c = _conv_causal(u, w)
        dc = dy * ab
        du = _conv_anti(dc, w)
        return _own(du * ac), _own(dy * c), _own(du * ax), _conv_dw(dc, u, K)

    return _tiled(fn, T=T, C=CW, ins=[("ext", dymix, None), ("ext", proj, None), ("ext", proj, lambda j: j + n),
                                       ("ext", proj, lambda j: j + 2 * n), ("row", w_a, None)],
                  out_dtypes=[BF16, BF16, BF16], acc_rows=[HALO], cb=cb, name="ga_bwd")


def _l2n(s):
    return s * lax.rsqrt(jnp.sum(s * s, axis=1, keepdims=True) + EPS)


def _qkv_fwd(proj, w_sec, coff, normalize, DNW, cb, name, comm=None):
    T = proj.shape[0]

    def fn(j, i, pre, w):
        c = _own(_conv_causal(pre, w))
        s = c * _sigmoid(c)
        return (_per_head(_l2n, s) if normalize else s,)

    res = _tiled(fn, T=T, C=DNW, ins=[("ext", proj, lambda j: j + coff), ("row", w_sec, None)],
                 out_dtypes=[F32], cb=cb, name=name, comm=comm)
    return res[0] if comm is None else (res[0][0], res[1])


def _qkv_bwd(dsec, proj, w_sec, coff, normalize, DNW, cb, name):
    T = proj.shape[0]
    K = w_sec.shape[0]

    def l2n_bwd(s, dn):
        r = lax.rsqrt(jnp.sum(s * s, axis=1, keepdims=True) + EPS)
        nrm = s * r
        return r * (dn - nrm * jnp.sum(dn * nrm, axis=1, keepdims=True))

    def fn(j, i, dn, pre, w):
        c = _conv_causal(pre, w)
        sg = _sigmoid(c)
        s = c * sg
        ds = _per_head(l2n_bwd, s, dn) if normalize else dn
        dc = ds * (sg * (1.0 + c * (1.0 - sg)))
        return _own(_conv_anti(dc, w)), _conv_dw(dc, pre, K)

    return _tiled(fn, T=T, C=DNW, ins=[("ext", dsec, None), ("ext", proj, lambda j: j + coff), ("row", w_sec, None)],
                  out_dtypes=[BF16], acc_rows=[HALO], cb=cb, name=name)


def _gb_fwd(small, a_log_row, dt_row, H):
    T = small.shape[0]

    def fn(j, i, sm, al, dt):
        z = sm + dt
        sp = jnp.maximum(z, 0.0) + jnp.log(1.0 + jnp.exp(-jnp.abs(z)))
        g = -jnp.exp(al) * sp
        beta = _sigmoid(pltpu.roll(sm, LANES - H, 1))
        return g, beta

    return _tiled(fn, T=T, C=LANES, ins=[("cur", small, None), ("row", a_log_row, None), ("row", dt_row, None)],
                  out_dtypes=[F32, F32], cb=LANES, name="gb_fwd")


def _gb_bwd(dgB, dbB, small, g, beta, a_log_row, dt_row, H):
    T = small.shape[0]

    def fn(j, i, dgv, dbv, sm, gv, bv, al, dt):
        lane = lax.broadcasted_iota(jnp.int32, sm.shape, 1)
        dg = jnp.zeros(sm.shape, F32)
        db = jnp.zeros(sm.shape, F32)
        for h in range(H):
            dg = jnp.where(lane == h, jnp.sum(dgv[h], axis=1, keepdims=True), dg)
            db = jnp.where(lane == h, jnp.sum(dbv[h], axis=1, keepdims=True), db)
        da = dg * (-jnp.exp(al)) * _sigmoid(sm + dt)
        dbb = db * bv * (1.0 - bv)
        dsm = jnp.where(lane < H, da, 0.0) + pltpu.roll(jnp.where(lane < H, dbb, 0.0), H, 1)
        d_alog = jnp.sum(jnp.where(lane < H, dg * gv, 0.0), axis=0, keepdims=True)
        d_dt = jnp.sum(jnp.where(lane < H, da, 0.0), axis=0, keepdims=True)
        return dsm, jnp.concatenate([d_alog, d_dt, jnp.zeros((HALO - 2, LANES), F32)], axis=0)

    return _tiled(fn, T=T, C=LANES, ins=[("stack", dgB, None), ("stack", dbB, None), ("cur", small, None), ("cur", g, None),
                                          ("cur", beta, None), ("row", a_log_row, None), ("row", dt_row, None)],
                  out_dtypes=[BF16], acc_rows=[HALO], cb=LANES, name="gb_bwd")


_DIMS = {"nn": (((1,), (0,)), ((), ())), "nt": (((1,), (1,)), ((), ())), "tn": (((0,), (0,)), ((), ()))}
_DOT_BWD = {"nn": (("nt", "gb"), ("tn", "ag")), "nt": (("nn", "gb"), ("tn", "ga")), "tn": (("nt", "bg"), ("nn", "ag"))}


def _split(a):
    hi = a.astype(BF16)
    return hi, (a - hi.astype(F32)).astype(BF16)


def _raw_dot(a, b, kind, passes):
    dg = lambda x, y: lax.dot_general(x, y, _DIMS[kind], preferred_element_type=F32)
    if passes == 1:
        return dg(a.astype(BF16), b.astype(BF16))
    ah, al = _split(a)
    bh, bl = _split(b)
    if kind == "tn":
        return dg(ah, bh) + (dg(ah, bl) + dg(al, bh))
    m = a.shape[0]
    top = dg(jnp.concatenate([ah, al], axis=0), bh)
    return top[:m] + (dg(ah, bl) + top[m:])


def _raw_dot_exact(a, b, kind, exact):
    dg = lambda x, y: lax.dot_general(x, y, _DIMS[kind], preferred_element_type=F32)
    if exact == "a":
        bh, bl = _split(b)
        return dg(a.astype(BF16), bh) + dg(a.astype(BF16), bl)
    ah, al = _split(a)
    return dg(ah, b.astype(BF16)) + dg(al, b.astype(BF16))


@functools.lru_cache(maxsize=None)
def _dotc(kind):
    @jax.custom_vjp
    def f(a, b):
        return _raw_dot_exact(a, b, kind, "a")

    def fwd(a, b):
        return _raw_dot_exact(a, b, kind, "a"), a

    def bwd(a, g):
        db = _raw_dot_exact(a, g, "tn", "a") if kind == "nn" else _raw_dot_exact(g, a, "tn", "b")
        return jnp.zeros_like(a), db

    f.defvjp(fwd, bwd)
    return f


@functools.lru_cache(maxsize=None)
def _dotf(kind, passes):
    @jax.custom_vjp
    def f(a, b):
        return _raw_dot(a, b, kind, passes)

    def fwd(a, b):
        return _raw_dot(a, b, kind, passes), (a, b)

    def bwd(res, g):
        ops = {"a": res[0], "b": res[1], "g": g}
        (ka, oa), (kb, ob) = _DOT_BWD[kind]
        return (_raw_dot(ops[oa[0]], ops[oa[1]], ka, passes), _raw_dot(ops[ob[0]], ops[ob[1]], kb, passes))

    f.defvjp(fwd, bwd)
    return f


@jax.custom_vjp
def _saved_inverse(L, inv):
    return inv


def _saved_inverse_fwd(L, inv):
    return inv, inv


def _saved_inverse_bwd(inv, g):
    d3nt, d3tn = _dotf("nt", 3), _dotf("tn", 3)
    return -d3nt(d3tn(inv, g), inv), jnp.zeros_like(inv)


_saved_inverse.defvjp(_saved_inverse_fwd, _saved_inverse_bwd)


def _chunk_fn(q, k, v, gB, bB, S, inv_saved=None):
    C = CHUNK
    d3 = _dotf("nn", 3)
    d1, d1nt, d1tn = _dotf("nn", 1), _dotf("nt", 1), _dotf("tn", 1)
    each = lambda f, *ls: tuple(f(*xs) for xs in zip(*ls))
    row = lax.broadcasted_iota(jnp.int32, (C, C), 0)
    col = lax.broadcasted_iota(jnp.int32, (C, C), 1)
    causal = row >= col
    strict = row > col
    tril = jnp.where(causal, 1.0, 0.0).astype(F32)
    eye = jnp.where(row == col, 1.0, 0.0).astype(F32)
    avg = jnp.full((C, HEAD_DIM), 1.0 / HEAD_DIM, F32)
    gc = each(lambda g: _dotc("nn")(tril, g), gB)
    R = each(lambda g: _dotc("nt")(avg, g), gc)
    decay = each(lambda g, r: jnp.where(causal, jnp.exp(jnp.where(causal, g[:, :C] - r, 0.0)), 0.0), gc, R)
    kk = each(lambda x: d1nt(x, x), k)
    L = each(lambda a, d, b: jnp.where(strict, a * d * b[:, :C], 0.0), kk, decay, bB)
    if inv_saved is None:
        inv = each(lambda l: eye - l, L)
        P = L
        for _ in range(5):
            P = each(lambda p: d3(p, p), P)
            inv = each(lambda a, p: d3(a, eye + p), inv, P)
    else:
        inv = each(_saved_inverse, L, inv_saved)
    eg = each(jnp.exp, gc)
    u = each(lambda a, x, b: d3(a, x * b), inv, v, bB)
    w = each(lambda a, x, b, e: d3(a, x * b * e), inv, k, bB, eg)
    qs = each(lambda x: x * (HEAD_DIM ** -0.5), q)
    qk = each(lambda a, x, d: d1nt(a, x) * d, qs, k, decay)
    gl = each(lambda g: g[C - 1:C, :], gc)
    kd = each(lambda x, a, g: x * jnp.exp(a - g), k, gl, gc)
    qe = each(lambda a, e: a * e, qs, eg)
    nh = len(S)
    o = ()
    for c in range(len(q) // nh):
        sl = slice(c * nh, (c + 1) * nh)
        v_new = each(lambda a, b, s: a - d1(b, s), u[sl], w[sl], S)
        o1 = each(lambda a, s: d1(a, s), qe[sl], S)
        o += each(lambda a, b, vn: a + d1(b, vn), o1, qk[sl], v_new)
        kv = each(lambda x, vn: d1tn(x, vn), kd[sl], v_new)
        S = each(lambda s, a, b: s * jnp.exp(a) + b, S, gl[sl], kv)
    return (o, S), inv


def _sel_lane(x, h):
    lane = lax.broadcasted_iota(jnp.int32, x.shape, 1)
    return jnp.broadcast_to(jnp.sum(jnp.where(lane == h, x, 0.0), axis=1, keepdims=True), x.shape)


def _tile_of(ref, c, h):
    return ref[c * CHUNK:(c + 1) * CHUNK, h * HEAD_DIM:(h + 1) * HEAD_DIM]


def _chunks_per_step(N):
    return 4 if N % 4 == 0 else (2 if N % 2 == 0 else 1)


def _delta_fwd(q, k, v, g, beta, comm=None):
    T = q.shape[0]
    H, N = q.shape[1] // HEAD_DIM, T // CHUNK
    cps = _chunks_per_step(N)
    rows = cps * CHUNK

    def body(q_ref, k_ref, v_ref, g_ref, b_ref, o_ref, s_ref, inv_ref, S):
        @pl.when(pl.program_id(0) == 0)
        def _():
            S[...] = jnp.zeros_like(S)

        gv, bv = g_ref[...], b_ref[...]
        pairs = lambda f: tuple(f(c, h) for c in range(cps) for h in range(H))
        S_in = tuple(S[h] for h in range(H))
        for h in range(H):
            s_ref[h, 0] = S_in[h]
        (o, S_new), inv = _chunk_fn(pairs(lambda c, h: _tile_of(q_ref, c, h)), pairs(lambda c, h: _tile_of(k_ref, c, h)),
                                    pairs(lambda c, h: _tile_of(v_ref, c, h)),
                                    pairs(lambda c, h: _sel_lane(gv[c * CHUNK:(c + 1) * CHUNK], h)),
                                    pairs(lambda c, h: _sel_lane(bv[c * CHUNK:(c + 1) * CHUNK], h)), S_in)
        for c in range(cps):
            for h in range(H):
                o_ref[c * CHUNK:(c + 1) * CHUNK, h * HEAD_DIM:(h + 1) * HEAD_DIM] = o[c * H + h]
                inv_ref[h, c] = inv[c * H + h]
        for h in range(H):
            S[h] = S_new[h]

    blk = pl.BlockSpec((rows, H * HEAD_DIM), lambda n: (n, 0))
    gblk = pl.BlockSpec((rows, LANES), lambda n: (n, 0))
    outs, comm_outs = _call(
        body, name="delta_fwd", grid=(N // cps,), in_specs=[blk, blk, blk, gblk, gblk],
        out_specs=[blk, pl.BlockSpec((H, 1, HEAD_DIM, HEAD_DIM), lambda n: (0, n, 0, 0)),
                   pl.BlockSpec((H, cps, CHUNK, CHUNK), lambda n: (0, n, 0, 0))],
        out_shape=[jax.ShapeDtypeStruct((T, H * HEAD_DIM), F32), jax.ShapeDtypeStruct((H, N // cps, HEAD_DIM, HEAD_DIM), F32),
                   jax.ShapeDtypeStruct((H, N, CHUNK, CHUNK), F32)],
        scratch_shapes=[pltpu.VMEM((H, HEAD_DIM, HEAD_DIM), F32)],
        semantics=("arbitrary",), args=(q, k, v, g, beta), comm=comm)
    return outs[0], outs[1], outs[2], comm_outs


def _delta_bwd(q, k, v, g, beta, S0, inv, do, comm=None):
    T = q.shape[0]
    H, N = q.shape[1] // HEAD_DIM, T // CHUNK
    cps = _chunks_per_step(N)
    rows, NS = cps * CHUNK, N // cps

    def body(q_ref, k_ref, v_ref, g_ref, b_ref, s_ref, inv_ref, do_ref, dq_ref, dk_ref, dv_ref, dg_ref, db_ref, dS):
        @pl.when(pl.program_id(0) == 0)
        def _():
            dS[...] = jnp.zeros_like(dS)

        gv, bv = g_ref[...], b_ref[...]
        pairs = lambda f: tuple(f(c, h) for c in range(cps) for h in range(H))
        heads = lambda f: tuple(f(h) for h in range(H))
        _, vjp, _ = jax.vjp(_chunk_fn, pairs(lambda c, h: _tile_of(q_ref, c, h)), pairs(lambda c, h: _tile_of(k_ref, c, h)),
                            pairs(lambda c, h: _tile_of(v_ref, c, h)),
                            pairs(lambda c, h: _sel_lane(gv[c * CHUNK:(c + 1) * CHUNK], h)),
                            pairs(lambda c, h: _sel_lane(bv[c * CHUNK:(c + 1) * CHUNK], h)),
                            heads(lambda h: s_ref[h, 0]), pairs(lambda c, h: inv_ref[h, c]), has_aux=True)
        dq, dk, dv, dgB, dbB, dS_prev, _ = vjp((pairs(lambda c, h: _tile_of(do_ref, c, h)), heads(lambda h: dS[h])))
        for c in range(cps):
            for h in range(H):
                r, sl = slice(c * CHUNK, (c + 1) * CHUNK), slice(h * HEAD_DIM, (h + 1) * HEAD_DIM)
                dq_ref[r, sl] = dq[c * H + h]
                dk_ref[r, sl] = dk[c * H + h]
                dv_ref[r, sl] = dv[c * H + h]
                dg_ref[h, r] = dgB[c * H + h]
                db_ref[h, r] = dbB[c * H + h]
        for h in range(H):
            dS[h] = dS_prev[h]

    blk = pl.BlockSpec((rows, H * HEAD_DIM), lambda n: (NS - 1 - n, 0))
    gblk = pl.BlockSpec((rows, LANES), lambda n: (NS - 1 - n, 0))
    hblk = pl.BlockSpec((H, rows, LANES), lambda n: (0, NS - 1 - n, 0))
    sd = jax.ShapeDtypeStruct
    outs, comm_outs = _call(
        body, name="delta_bwd", grid=(NS,),
        in_specs=[blk, blk, blk, gblk, gblk, pl.BlockSpec((H, 1, HEAD_DIM, HEAD_DIM), lambda n: (0, NS - 1 - n, 0, 0)),
                  pl.BlockSpec((H, cps, CHUNK, CHUNK), lambda n: (0, NS - 1 - n, 0, 0)), blk],
        out_specs=[blk, blk, blk, hblk, hblk],
        out_shape=[sd((T, H * HEAD_DIM), F32)] * 3 + [sd((H, T, LANES), F32)] * 2,
        scratch_shapes=[pltpu.VMEM((H, HEAD_DIM, HEAD_DIM), F32)],
        semantics=("arbitrary",), args=(q, k, v, g, beta, S0, inv, do), comm=comm)
    return (*outs, comm_outs)


def _gnorm_fwd(o, proj, z_coff, gdn_t, DNW):
    T = o.shape[0]

    def fn(j, i, ov, zv, gv):
        def one(oh, zh, gh):
            r = lax.rsqrt(jnp.mean(oh * oh, axis=1, keepdims=True) + EPS)
            return oh * r * gh * (zh * _sigmoid(zh))
        return (_per_head(one, ov, zv, jnp.broadcast_to(gv, ov.shape)),)

    return _tiled(fn, T=T, C=DNW, ins=[("cur", o, None), ("cur", proj, lambda j: j + z_coff), ("row", gdn_t, None)],
                  out_dtypes=[BF16], cb=DNW, name="gnorm_fwd")[0]


def _gnorm_bwd(dymix, y_coff, o, proj, z_coff, gdn_t, DNW):
    T = o.shape[0]
    nh = DNW // HEAD_DIM

    def fn(j, i, dy, ov, zv, gv):
        dos, dzs, dgs = [], [], jnp.zeros((1, HEAD_DIM), F32)
        for h in range(nh):
            sl = slice(h * HEAD_DIM, (h + 1) * HEAD_DIM)
            dyh, oh, zh, gh = dy[:, sl].astype(F32), ov[:, sl], zv[:, sl], gv[:, sl]
            r = lax.rsqrt(jnp.mean(oh * oh, axis=1, keepdims=True) + EPS)
            on = oh * r
            sg = _sigmoid(zh)
            sz = zh * sg
            dzs.append(dyh * on * gh * (sg * (1.0 + zh * (1.0 - sg))))
            don = dyh * gh * sz
            dos.append(r * (don - on * jnp.mean(don * on, axis=1, keepdims=True)))
            dgs = dgs + jnp.sum(dyh * on * sz, axis=0, keepdims=True)
        cat = (lambda xs: xs[0] if nh == 1 else jnp.concatenate(xs, axis=1))
        return cat(dos), cat(dzs), _row0(dgs)

    T_ = T
    nI = T_ // _tile(T_, 256, HALO)
    tb = T_ // nI
    specs_cb = DNW

    def body_wrap():
        def body(dy_ref, o_ref, z_ref, g_ref, do_ref, dz_ref, dg_ref):
            i = pl.program_id(0)
            d_o, d_z, d_g = fn(0, i, dy_ref[...], o_ref[...], z_ref[...], g_ref[...])
            do_ref[...] = d_o
            dz_ref[...] = d_z.astype(dz_ref.dtype)

            @pl.when(i == 0)
            def _():
                dg_ref[...] = d_g

            @pl.when(i > 0)
            def _():
                dg_ref[...] += d_g

        return pl.pallas_call(
            body, name="gnorm_bwd", grid=(nI,),
            in_specs=[pl.BlockSpec((tb, specs_cb), lambda i: (i, y_coff)), pl.BlockSpec((tb, specs_cb), lambda i: (i, 0)),
                      pl.BlockSpec((tb, specs_cb), lambda i: (i, z_coff)), pl.BlockSpec((1, specs_cb), lambda i: (0, 0))],
            out_specs=[pl.BlockSpec((tb, specs_cb), lambda i: (i, 0)), pl.BlockSpec((tb, specs_cb), lambda i: (i, 0)),
                       pl.BlockSpec((HALO, HEAD_DIM), lambda i: (0, 0))],
            out_shape=[jax.ShapeDtypeStruct((T_, DNW), F32), jax.ShapeDtypeStruct((T_, DNW), BF16),
                       jax.ShapeDtypeStruct((HALO, HEAD_DIM), F32)],
            compiler_params=pltpu.CompilerParams(dimension_semantics=("arbitrary",), vmem_limit_bytes=VMEM_LIMIT),
        )(dymix, o, proj, gdn_t)

    return body_wrap()


def _ffn_fwd(up_g, up_v, w_g, w_v, cb):
    T, F = up_g.shape

    def fn(j, i, ug, uv, wg, wv):
        cg = _own(_conv_causal(ug, wg))
        cv = _own(_conv_causal(uv, wv))
        return (cg * _sigmoid(cg) * cv,)

    return _tiled(fn, T=T, C=F, ins=[("ext", up_g, None), ("ext", up_v, None), ("row", w_g, None), ("row", w_v, None)],
                  out_dtypes=[BF16], tb=1024, cb=cb, name="ffn_fwd")[0]


def _ffn_bwd(dact, up_g, up_v, w_g, w_v, cb):
    T, F = up_g.shape
    K = w_g.shape[0]

    def fn(j, i, da, ug, uv, wg, wv):
        cg = _conv_causal(ug, wg)
        cv = _conv_causal(uv, wv)
        sg = _sigmoid(cg)
        dgate = da * cv * (sg * (1.0 + cg * (1.0 - sg)))
        dval = da * (cg * sg)
        return (_own(_conv_anti(dgate, wg)), _own(_conv_anti(dval, wv)), _conv_dw(dgate, ug, K), _conv_dw(dval, uv, K))

    return _tiled(fn, T=T, C=F, ins=[("ext", dact, None), ("ext", up_g, None), ("ext", up_v, None), ("row", w_g, None),
                                      ("row", w_v, None)], out_dtypes=[BF16, BF16], acc_rows=[HALO, HALO], tb=1024, cb=cb,
                  name="ffn_bwd")


def _wide(R, Cc, n_f32, unit=HALO):
    cb = Cc if (Cc % LANES or Cc <= 4096) else _tile(Cc, 2048, LANES)
    cap = max(unit, EW_VMEM_BUDGET // (2 * 4 * n_f32 * cb) // unit * unit)
    return _tile(R, cap, unit), cb


def _adamw(w, g, m, v, name):
    R, Cc = w.shape
    tb, cb = _wide(R, Cc, 7)
    c1 = 1.0 / (1.0 - ADAM_B1 ** ADAM_STEP)
    c2 = 1.0 / (1.0 - ADAM_B2 ** ADAM_STEP)

    def fn(j, i, wv, gv, mv, vv):
        m2 = ADAM_B1 * mv + (1.0 - ADAM_B1) * gv
        v2 = ADAM_B2 * vv + (1.0 - ADAM_B2) * (gv * gv)
        delta = -ADAM_LR * ((m2 * c1) / (jnp.sqrt(v2 * c2) + ADAM_EPS) + ADAM_WD * wv)
        return delta, m2, v2

    return _tiled(fn, T=R, C=Cc, ins=[("cur", w, None), ("cur", g, None), ("cur", m, None), ("cur", v, None)],
                  out_dtypes=[F32, F32, F32], tb=tb, cb=cb, name=name)


def _join_shards(w4, n_main):
    S4, R, cs = w4.shape
    n_small = S4 * cs - n_main
    assert 0 < n_small <= LANES and n_small <= cs
    tb = _tile(R, 256, 2 * HALO)

    def body(w_ref, main_ref, small_ref):
        for t in range(S4 - 1):
            main_ref[:, t * cs:(t + 1) * cs] = w_ref[t]
        last = w_ref[S4 - 1]
        main_ref[:, (S4 - 1) * cs:] = last[:, :cs - n_small]
        small_ref[...] = jnp.zeros_like(small_ref)
        small_ref[:, :n_small] = last[:, cs - n_small:]

    return pl.pallas_call(
        body, name="join_w_in", grid=(R // tb,), in_specs=[pl.BlockSpec((S4, tb, cs), lambda i: (0, i, 0))],
        out_specs=[pl.BlockSpec((tb, n_main), lambda i: (i, 0)), pl.BlockSpec((tb, LANES), lambda i: (i, 0))],
        out_shape=[jax.ShapeDtypeStruct((R, n_main), w4.dtype), jax.ShapeDtypeStruct((R, LANES), w4.dtype)],
        compiler_params=pltpu.CompilerParams(dimension_semantics=("parallel",), vmem_limit_bytes=VMEM_LIMIT))(w4)


def _split_shards(main, small, cs):
    R, n_main = main.shape
    n_small = 4 * cs - n_main
    tb = _tile(R, 256, HALO)

    def body(main_ref, small_ref, out_ref):
        for t in range(3):
            out_ref[t] = main_ref[:, t * cs:(t + 1) * cs]
        out_ref[3, :, :cs - n_small] = main_ref[:, 3 * cs:]
        out_ref[3, :, cs - n_small:] = small_ref[:, :n_small]

    return pl.pallas_call(
        body, name="split_g_in", grid=(R // tb,),
        in_specs=[pl.BlockSpec((tb, n_main), lambda i: (i, 0)), pl.BlockSpec((tb, LANES), lambda i: (i, 0))],
        out_specs=pl.BlockSpec((4, tb, cs), lambda i: (0, i, 0)), out_shape=jax.ShapeDtypeStruct((4, R, cs), main.dtype),
        compiler_params=pltpu.CompilerParams(dimension_semantics=("parallel",), vmem_limit_bytes=VMEM_LIMIT))(main, small)


def _sum_stack(st, name):
    S, R, Cc = st.shape
    cb = _tile(Cc, 512, LANES) if Cc % LANES == 0 else Cc

    def fn(j, i, sv):
        t = sv[0]
        for s in range(1, S):
            t = t + sv[s]
        return (t,)

    return _tiled(fn, T=R, C=Cc, ins=[("stack", st, None)], out_dtypes=[F32], cb=cb, name=name)[0]


ANY = pl.BlockSpec(memory_space=pl.ANY)


def _place():
    x, y, c = lax.axis_index("x"), lax.axis_index("y"), lax.axis_index("c")
    return x, y, c, 2 * x + y


def _chip_dev(s, c):
    return (s // 2, s % 2, c)


class _Comm:
    def __init__(self, ins, out_shapes, sems, start, wait, aliases=None):
        self.ins, self.out_shapes, self.sems = list(ins), list(out_shapes), list(sems)
        self.start, self.wait, self.aliases = start, wait, dict(aliases or {})


def _merge(*comms):
    offs, i, o, s = [], 0, 0, 0
    for cm in comms:
        offs.append((i, o, s))
        i, o, s = i + len(cm.ins), o + len(cm.out_shapes), s + len(cm.sems)

    def part(refs, k, cm):
        i0, o0, s0 = offs[k]
        return refs[0][i0:i0 + len(cm.ins)], refs[1][o0:o0 + len(cm.out_shapes)], refs[2][s0:s0 + len(cm.sems)]

    def start(*refs):
        for k, cm in enumerate(comms):
            cm.start(*part(refs, k, cm))

    def wait(*refs):
        for k, cm in enumerate(comms):
            cm.wait(*part(refs, k, cm))

    aliases = {}
    for k, cm in enumerate(comms):
        for a, b in cm.aliases.items():
            aliases[offs[k][0] + a] = offs[k][1] + b
    return _Comm([a for cm in comms for a in cm.ins], [a for cm in comms for a in cm.out_shapes],
                 [a for cm in comms for a in cm.sems], start, wait, aliases)


def _call(body, *, name, grid, in_specs, out_specs, out_shape, scratch_shapes, semantics, args, comm=None):
    if comm is None:
        outs = pl.pallas_call(
            body, name=name, grid=grid, in_specs=in_specs, out_specs=out_specs, out_shape=out_shape,
            scratch_shapes=list(scratch_shapes),
            compiler_params=pltpu.CompilerParams(dimension_semantics=semantics, vmem_limit_bytes=VMEM_LIMIT))(*args)
        return list(outs), []
    n_in, n_out, n_scr = len(in_specs), len(out_specs), len(scratch_shapes)
    ci, co = len(comm.ins), len(comm.out_shapes)

    def wrapped(*refs):
        r = 0
        ins, r = refs[r:r + n_in], r + n_in
        cins, r = refs[r:r + ci], r + ci
        outs, r = refs[r:r + n_out], r + n_out
        couts, r = refs[r:r + co], r + co
        scr, r = refs[r:r + n_scr], r + n_scr
        csems = refs[r:]
        ids = [pl.program_id(a) for a in range(len(grid))]
        first, last = ids[0] == 0, ids[0] == grid[0] - 1
        for a in range(1, len(grid)):
            first = jnp.logical_and(first, ids[a] == 0)
            last = jnp.logical_and(last, ids[a] == grid[a] - 1)

        @pl.when(first)
        def _():
            comm.start(cins, couts, csems)

        body(*ins, *outs, *scr)

        @pl.when(last)
        def _():
            comm.wait(cins, couts, csems)

    outs = pl.pallas_call(
        wrapped, name=name, grid=grid, in_specs=list(in_specs) + [ANY] * ci, out_specs=list(out_specs) + [ANY] * co,
        out_shape=list(out_shape) + comm.out_shapes, scratch_shapes=list(scratch_shapes) + comm.sems,
        input_output_aliases={n_in + a: n_out + b for a, b in comm.aliases.items()},
        compiler_params=pltpu.CompilerParams(dimension_semantics=("arbitrary",) * len(grid), vmem_limit_bytes=VMEM_LIMIT),
    )(*args, *comm.ins)
    return list(outs[:n_out]), list(outs[n_out:])


def _run_comm(comm, name):
    ci, co = len(comm.ins), len(comm.out_shapes)

    def body(*refs):
        cins, couts, csems = refs[:ci], refs[ci:ci + co], refs[ci + co:]
        comm.start(cins, couts, csems)
        comm.wait(cins, couts, csems)

    outs = pl.pallas_call(body, name=name, in_specs=[ANY] * ci, out_specs=[ANY] * co, out_shape=comm.out_shapes,
                          scratch_shapes=comm.sems, input_output_aliases=comm.aliases)(*comm.ins)
    return list(outs)


def _ag_comm(shard, land=None, q=0, nq=1):
    two, R2, Cc = shard.shape
    rows = pl.ds(q * (R2 // nq), R2 // nq)
    DMA = pltpu.SemaphoreType.DMA

    def copies(ins, outs, sems, which):
        sh, out = ins[0], outs[0]
        send1, recv1, send2, recv2, send0, recv0 = sems
        x, y, c, s = _place()
        sib = (x, y, 1 - c)
        rc = pltpu.make_async_remote_copy
        if which == "first":
            return [rc(sh.at[c, rows], out.at[s, c, rows], send1.at[m - 1], recv1.at[m - 1],
                       device_id=_chip_dev(s ^ m, c), device_id_type=MESH) for m in range(1, 4)]
        if which == "own":
            return [rc(sh.at[h, rows], out.at[s, h, rows], send0.at[h], recv0.at[h], device_id=sib, device_id_type=MESH)
                    for h in range(2)]
        if which == "landed":
            return [rc(sh.at[c, rows], out.at[s ^ m, c, rows], send1.at[m - 1], recv1.at[m - 1], device_id=sib,
                       device_id_type=MESH) for m in range(1, 4)]
        half = c if which == "passed" else 1 - c
        return [rc(out.at[s ^ m, half, rows], out.at[s ^ m, half, rows], send2.at[m - 1], recv2.at[m - 1], device_id=sib,
                   device_id_type=MESH) for m in range(1, 4)]

    def start(ins, outs, sems):
        for cp in copies(ins, outs, sems, "first") + copies(ins, outs, sems, "own"):
            cp.start()

    def wait(ins, outs, sems):
        passed = copies(ins, outs, sems, "passed")
        for lan, pas in zip(copies(ins, outs, sems, "landed"), passed):
            lan.wait_recv()
            pas.start()
        for cp in copies(ins, outs, sems, "handed"):
            cp.wait_recv()
        for cp in copies(ins, outs, sems, "own"):
            cp.wait()
        for cp in copies(ins, outs, sems, "first") + passed:
            cp.wait_send()

    return _Comm([shard] + ([land] if land is not None else []), [jax.ShapeDtypeStruct((4, two, R2, Cc), shard.dtype)],
                 [DMA((3,)), DMA((3,)), DMA((3,)), DMA((3,)), DMA((2,)), DMA((2,))], start, wait,
                 {1: 0} if land is not None else None)


def _a2a_comm(S1, q=0, nq=1, land=None):
    S4, R2, Cc = S1.shape
    rows = pl.ds(q * (R2 // nq), R2 // nq)
    DMA = pltpu.SemaphoreType.DMA

    def copies(ins, outs, sems):
        x, y, c, s = _place()
        return [pltpu.make_async_remote_copy(ins[0].at[s ^ m, rows], outs[0].at[m - 1, rows], sems[0].at[m - 1],
                                             sems[1].at[m - 1], device_id=_chip_dev(s ^ m, c), device_id_type=MESH)
                for m in range(1, 4)]

    def start(ins, outs, sems):
        for cp in copies(ins, outs, sems):
            cp.start()

    def wait(ins, outs, sems):
        for cp in copies(ins, outs, sems):
            cp.wait()

    return _Comm([S1] + ([land] if land is not None else []), [jax.ShapeDtypeStruct((3, R2, Cc), S1.dtype)],
                 [DMA((3,)), DMA((3,))], start, wait, {1: 0} if land is not None else None)


def _halves(G):
    return G.reshape(G.shape[0], 2, G.shape[1] // 2, G.shape[2])


def _swap_comm(piece):
    n, two, R2, Cc = piece.shape
    DMA = pltpu.SemaphoreType.DMA

    def copies(ins, outs, sems):
        x, y, c, s = _place()
        return [pltpu.make_async_remote_copy(ins[0].at[t, 1 - c], outs[0].at[t], sems[0].at[t], sems[1].at[t],
                                             device_id=(x, y, 1 - c), device_id_type=MESH) for t in range(n)]

    def start(ins, outs, sems):
        for cp in copies(ins, outs, sems):
            cp.start()

    def wait(ins, outs, sems):
        for cp in copies(ins, outs, sems):
            cp.wait()

    return _Comm([piece], [jax.ShapeDtypeStruct((n, R2, Cc), piece.dtype)], [DMA((n,)), DMA((n,))], start, wait)


def _add_half(pieces, As, cidx, name):
    R2, Cc = pieces[0].shape[2:]
    S4 = sum(pc.shape[0] for pc in pieces)
    tb, cb = _wide(R2, Cc, 3, 2 * HALO)
    nI, nJ = R2 // tb, Cc // cb

    def body(c_ref, g_ref, a_ref, *rest):
        rest[-1][...] = (g_ref[0, 0] + a_ref[0]).astype(BF16)

    out, t0 = None, 0
    for k, (pc, A) in enumerate(zip(pieces, As)):
        grid_spec = pltpu.PrefetchScalarGridSpec(
            num_scalar_prefetch=1, grid=(pc.shape[0], nI, nJ),
            in_specs=[pl.BlockSpec((1, 1, tb, cb), lambda t, i, j, c_ref: (t, c_ref[0], i, j)),
                      pl.BlockSpec((1, tb, cb), lambda t, i, j, c_ref: (t, i, j))] + ([ANY] if k else []),
            out_specs=pl.BlockSpec((tb, cb), lambda t, i, j, c_ref, t0=t0: ((t0 + t) * nI + i, j)))
        out = pl.pallas_call(
            functools.partial(body), name=f"{name}{k}", grid_spec=grid_spec, out_shape=jax.ShapeDtypeStruct((S4 * R2, Cc), BF16),
            input_output_aliases={3: 0} if k else {},
            compiler_params=pltpu.CompilerParams(dimension_semantics=("parallel", "parallel", "parallel"),
                                                 vmem_limit_bytes=VMEM_LIMIT),
        )(*((cidx, pc, A) + ((out,) if k else ())))
        t0 += pc.shape[0]
    return out.reshape(S4, R2, Cc)


def _add_own(S1, B, chip_idx, cidx, name):
    S4, R2, Cc = S1.shape
    tb, cb = _wide(R2, Cc, 3, 2 * HALO)

    def body(s_idx, c_idx, s_ref, b_ref, o_ref):
        o_ref[...] = ((s_ref[0].astype(F32) + b_ref[0].astype(F32)) + b_ref[1].astype(F32)) + b_ref[2].astype(F32)

    grid_spec = pltpu.PrefetchScalarGridSpec(
        num_scalar_prefetch=2, grid=(R2 // tb, Cc // cb),
        in_specs=[pl.BlockSpec((1, tb, cb), lambda i, j, s_idx, c_idx: (s_idx[0], i, j)),
                  pl.BlockSpec((3, tb, cb), lambda i, j, s_idx, c_idx: (0, i, j))],
        out_specs=pl.BlockSpec((None, tb, cb), lambda i, j, s_idx, c_idx: (c_idx[0], i, j)))
    return pl.pallas_call(body, name=name, grid_spec=grid_spec, out_shape=jax.ShapeDtypeStruct((2, R2, Cc), F32),
                          compiler_params=pltpu.CompilerParams(dimension_semantics=("parallel", "parallel"),
                                                               vmem_limit_bytes=VMEM_LIMIT))(chip_idx, cidx, S1, B)


def _sibling_fill(Hs, name):
    def body(h_ref, out_ref, send, recv):
        x, y, c, s = _place()
        cp = pltpu.make_async_remote_copy(h_ref.at[c], out_ref.at[c], send, recv, device_id=(x, y, 1 - c), device_id_type=MESH)
        cp.start()
        cp.wait()

    return pl.pallas_call(
        body, name=name, in_specs=[ANY], out_specs=ANY, out_shape=jax.ShapeDtypeStruct(Hs.shape, Hs.dtype),
        input_output_aliases={0: 0}, scratch_shapes=[pltpu.SemaphoreType.DMA, pltpu.SemaphoreType.DMA],
    )(Hs)


def _gather_all(buf, name):
    R, Cc = buf.shape

    def body(b_ref, out_ref, send, recv, local):
        x, y, c, s = _place()
        d = 2 * s + c
        mine = pltpu.make_async_copy(b_ref, out_ref.at[d], local)
        mine.start()
        cps = []
        for m in range(1, 8):
            t = d ^ m
            cp = pltpu.make_async_remote_copy(b_ref, out_ref.at[d], send.at[m - 1], recv.at[m - 1],
                                              device_id=(t // 4, (t // 2) % 2, t % 2), device_id_type=MESH)
            cp.start()
            cps.append(cp)
        for cp in cps:
            cp.wait()
        mine.wait()

    return pl.pallas_call(
        body, name=name, in_specs=[ANY], out_specs=ANY, out_shape=jax.ShapeDtypeStruct((8, R, Cc), buf.dtype),
        scratch_shapes=[pltpu.SemaphoreType.DMA((7,)), pltpu.SemaphoreType.DMA((7,)), pltpu.SemaphoreType.DMA],
    )(buf)


def _finish_shard(S1, B, cidx, chip_idx, name):
    Hs = _sibling_fill(_add_own(S1, B, chip_idx, cidx, name + "_sum"), name + "_gather")
    return Hs.reshape(2 * Hs.shape[1], Hs.shape[2])


def _pack_rows(vs):
    flat = jnp.concatenate([v.reshape(-1) for v in vs])
    n = flat.shape[0]
    rows = -(-n // (LANES * 2 * HALO)) * 2 * HALO
    return jnp.pad(flat, (0, rows * LANES - n)).reshape(rows, LANES)


def _unpack_rows(buf, shapes):
    flat = buf.reshape(-1)
    outs, o = [], 0
    for shp in shapes:
        n = 1
        for d in shp:
            n *= d
        outs.append(flat[o:o + n].reshape(shp))
        o += n
    return outs


def kernel(x, p, norm_mix_g, w_in, conv_a_w, conv_qkv_w, a_log, dt_bias, dn_norm_g, w_out, norm_ffn_g, w_up, conv_ffn_w, w_down, norm_ple_g, w_ple_gate, w_ple_proj, final_norm_g, loss_target, m_norm_mix_g, m_w_in, m_conv_a_w, m_conv_qkv_w, m_a_log, m_dt_bias, m_dn_norm_g, m_w_out, m_norm_ffn_g, m_w_up, m_conv_ffn_w, m_w_down, m_norm_ple_g, m_w_ple_gate, m_w_ple_proj, m_final_norm_g, v_norm_mix_g, v_w_in, v_conv_a_w, v_conv_qkv_w, v_a_log, v_dt_bias, v_dn_norm_g, v_w_out, v_norm_ffn_g, v_w_up, v_conv_ffn_w, v_w_down, v_norm_ple_g, v_w_ple_gate, v_w_ple_proj, v_final_norm_g):
    xs = x[0]
    ps = p[0, 0]
    tgt = loss_target[0]
    T, D = xs.shape
    H = a_log.shape[-1]
    DNW = H * HEAD_DIM
    CW = conv_a_w.shape[-1] * 4
    F = w_down.shape[1] * 4
    PD = ps.shape[-1]
    IN_MAIN = 3 * CW + 4 * DNW
    IN_COLS = IN_MAIN + 2 * H
    assert w_in.shape[-1] * 4 == IN_COLS and CW + DNW == D and 2 * H <= LANES
    cb = _tile(min(CW, DNW), 512, LANES)
    while F % cb:
        cb -= LANES
    cidx = lax.axis_index("c").astype(jnp.int32).reshape(1)
    chip = 2 * lax.axis_index("x") + lax.axis_index("y")

    def halves(w):
        sh = w[0].astype(BF16)
        return sh.reshape(2, sh.shape[0] // 2, sh.shape[1])

    def whole(land):
        return land.reshape(4, 2 * land.shape[2], land.shape[3])

    def rows(g4):
        return g4.reshape(4 * g4.shape[1], g4.shape[2])

    conv_shapes = [conv_a_w[0].shape, conv_qkv_w[0].shape, conv_ffn_w[0].shape]
    cpack = _pack_rows([conv_a_w[0], conv_qkv_w[0], conv_ffn_w[0]])
    sh_in, sh_out, sh_up, sh_down, sh_pg, sh_pp = (halves(w) for w in (w_in, w_out, w_up, w_down, w_ple_gate, w_ple_proj))
    l_in, cg = _run_comm(_merge(_ag_comm(sh_in), _ag_comm(cpack.reshape(2, cpack.shape[0] // 2, LANES))), "ag_w_in_conv")
    w_in_main, w_in_small = _join_shards(whole(l_in), IN_MAIN)
    cg = cg.reshape(4, cpack.shape[0], LANES)
    parts = [_unpack_rows(cg[t], conv_shapes) for t in range(4)]
    cw_a = jnp.concatenate([parts[t][0] for t in range(4)], axis=1)
    cw_qkv = jnp.concatenate([parts[t][1] for t in range(4)], axis=1)
    cw_ffn = jnp.concatenate([parts[t][2] for t in range(4)], axis=1)
    cw_q, cw_k, cw_v = cw_qkv[:, :DNW], cw_qkv[:, DNW:2 * DNW], cw_qkv[:, 2 * DNW:]
    cw_fg, cw_fv = cw_ffn[:, :F], cw_ffn[:, F:]
    pad_row = lambda v: jnp.pad(v, ((0, 0), (0, LANES - v.shape[1])))
    a_log_row, dt_row = pad_row(a_log), pad_row(dt_bias)
    gdn_t = jnp.tile(dn_norm_g, (1, H))
    gfin = final_norm_g.reshape(1, D)

    h1 = _rms_fwd(xs, norm_mix_g, "rms1")
    proj, (l_up,) = _mm(h1, w_in_main, mode="nn", out_dtypes=[F32], name="mm_proj", comm=_ag_comm(sh_up, q=0, nq=2))
    small = _mm(h1, w_in_small, mode="nn", out_dtypes=[F32], name="mm_small")
    ya = _ga_fwd(proj, cw_a, CW, cb)
    nq = 3 * CW // cb
    nd = DNW // cb
    qn, (l_out,) = _qkv_fwd(proj, cw_q, nq, True, DNW, cb, "q_fwd", comm=_ag_comm(sh_out, q=0, nq=2))
    kn, (l_out,) = _qkv_fwd(proj, cw_k, nq + nd, True, DNW, cb, "k_fwd", comm=_ag_comm(sh_out, l_out, q=1, nq=2))
    vs = _qkv_fwd(proj, cw_v, nq + 2 * nd, False, DNW, cb, "v_fwd")
    g, beta = _gb_fwd(small, a_log_row, dt_row, H)
    o, S0, inv_c, (l_up,) = _delta_fwd(qn, kn, vs, g, beta, comm=_ag_comm(sh_up, l_up, q=1, nq=2))
    w_out_f = rows(whole(l_out))
    w_out_a, w_out_b = w_out_f[:CW], w_out_f[CW:]
    w_up_4 = whole(l_up)
    z_coff = (3 * CW + 3 * DNW) // DNW
    assert (3 * CW + 3 * DNW) % DNW == 0 and CW % DNW == 0
    yb = _gnorm_fwd(o, proj, z_coff, gdn_t, DNW)
    add = lambda acc, r: (r + acc,)
    x1 = _mm(ya, w_out_a, mode="nn", out_dtypes=[F32], epi=add, extras=[xs], name="mm_out_a")
    x1 = _mm(yb, w_out_b, mode="nn", out_dtypes=[F32], epi=add, extras=[x1], name="mm_out_b")
    h2 = _rms_fwd(x1, norm_ffn_g, "rms2")
    up_g, (l_down,) = _mm(h2, w_up_4, mode="nn", b_split=(0, 2), out_dtypes=[F32], name="mm_up_g",
                          comm=_ag_comm(sh_down, q=0, nq=2))
    up_v, (l_down,) = _mm(h2, w_up_4, mode="nn", b_split=(2, 2), out_dtypes=[F32], name="mm_up_v",
                          comm=_ag_comm(sh_down, l_down, q=1, nq=2))
    w_down_f = rows(whole(l_down))
    act = _ffn_fwd(up_g, up_v, cw_fg, cw_fv, cb)
    x2, (l_pg, l_pp) = _mm(act, w_down_f, mode="nn", out_dtypes=[F32], epi=add, extras=[x1], name="mm_down",
                           comm=_merge(_ag_comm(sh_pg), _ag_comm(sh_pp)))
    w_pg_f = rows(whole(l_pg))
    w_pp_4 = whole(l_pp)
    h3 = _rms_fwd(x2, norm_ple_g, "rms3")
    pp = _mm(ps, w_pp_4, mode="nn", b_split=(0, 4), out_dtypes=[F32], name="mm_pp")

    def ple_epi(acc, x2v, ppv):
        pg = _sigmoid(acc)
        return x2v + pg * ppv, pg

    x3, pg = _mm(h3, w_pg_f, mode="nn", out_dtypes=[F32, F32], epi=ple_epi, extras=[x2, pp], name="mm_pg")

    dx3, dpg, dpp, fin = _final_fb(x3, tgt, gfin, pp, pg)
    loss = lax.psum(jnp.sum(fin[1]), ("x", "y", "c"))
    d_gfin = fin[0:1]
    def split_rows(dW):
        return dW.reshape(4, dW.shape[0] // 4, dW.shape[1])

    dW_pp = _mm(ps, dpp, mode="tn", out_split=4, out_dtypes=[F32], name="mm_dw_pp")
    dW_pg = _mm(h3, dpg, mode="tn", out_dtypes=[F32], name="mm_dw_pg")
    P_pp, P_pg = _halves(dW_pp), _halves(split_rows(dW_pg))
    dh3, (A_pp, A_pg) = _mm(dpg, w_pg_f, mode="nt", out_dtypes=[F32], name="mm_dh3",
                            comm=_merge(_swap_comm(P_pp), _swap_comm(P_pg)))
    S_pp = _add_half([P_pp], [A_pp], cidx, "rs_w_pp_add")
    S_pg = _add_half([P_pg], [A_pg], cidx, "rs_w_pg_add")
    dx2, dx2_b, d_gple = _rms_bwd(dh3, x2, norm_ple_g, dx3, "rms3_bwd")
    dW_down, (B_pp, B_pg) = _mm(act, dx2_b, mode="tn", out_dtypes=[F32], name="mm_dw_down",
                                comm=_merge(_a2a_comm(S_pp), _a2a_comm(S_pg)))
    P_down = _halves(split_rows(dW_down))
    dact, (A_down,) = _mm(dx2_b, w_down_f, mode="nt", out_dtypes=[F32], name="mm_dact", comm=_swap_comm(P_down))
    S_down = _add_half([P_down], [A_down], cidx, "rs_w_down_add")
    dup_g, dup_v, dcw_fg, dcw_fv = _ffn_bwd(dact, up_g, up_v, cw_fg, cw_fv, cb)
    dW_up_g, (B_down,) = _mm(h2, dup_g, mode="tn", out_split=2, out_dtypes=[F32], name="mm_dw_up_g", comm=_a2a_comm(S_down))
    P_ug = _halves(dW_up_g)
    dW_up_v, (A_ug,) = _mm(h2, dup_v, mode="tn", out_split=2, out_dtypes=[F32], name="mm_dw_up_v", comm=_swap_comm(P_ug))
    P_uv = _halves(dW_up_v)
    dh2, (A_uv,) = _mm(dup_g, w_up_4, mode="nt", b_split=(0, 2), out_dtypes=[F32], name="mm_dh2_g", comm=_swap_comm(P_uv))
    S_up = _add_half([P_ug, P_uv], [A_ug, A_uv], cidx, "rs_w_up_add")
    dh2, (B_up,) = _mm(dup_v, w_up_4, mode="nt", b_split=(2, 2), out_dtypes=[F32], epi=add, extras=[dh2], name="mm_dh2_v",
                       comm=_a2a_comm(S_up, 0, 2))
    dx1, dx1_b, d_gffn = _rms_bwd(dh2, x1, norm_ffn_g, dx2, "rms2_bwd")
    dW_out_a = _mm(ya, dx1_b, mode="tn", out_dtypes=[F32], name="mm_dw_out_a")
    dW_out_b = _mm(yb, dx1_b, mode="tn", out_dtypes=[F32], name="mm_dw_out_b")
    P_oa, P_ob = _halves(dW_out_a.reshape(-1, D // 4, D)), _halves(dW_out_b.reshape(-1, D // 4, D))
    dymix, (A_oa, A_ob) = _mm(dx1_b, w_out_f, mode="nt", out_dtypes=[F32], name="mm_dymix",
                              comm=_merge(_swap_comm(P_oa), _swap_comm(P_ob)))
    S_out = _add_half([P_oa, P_ob], [A_oa, A_ob], cidx, "rs_w_out_add")
    dax, dab, dac, dcw_a = _ga_bwd(dymix, proj, cw_a, CW, cb)
    do, dz, d_gdn = _gnorm_bwd(dymix, CW // DNW, o, proj, z_coff, gdn_t, DNW)
    dqn, dkn, dvs, dgB, dbB, (B_up, B_out) = _delta_bwd(qn, kn, vs, g, beta, S0, inv_c, do,
                                                        comm=_merge(_a2a_comm(S_up, 1, 2, B_up), _a2a_comm(S_out)))
    dq_pre, dcw_q = _qkv_bwd(dqn, proj, cw_q, nq, True, DNW, cb, "q_bwd")
    dk_pre, dcw_k = _qkv_bwd(dkn, proj, cw_k, nq + nd, True, DNW, cb, "k_bwd")
    dv_pre, dcw_v = _qkv_bwd(dvs, proj, cw_v, nq + 2 * nd, False, DNW, cb, "v_bwd")
    dsmall, d_ab = _gb_bwd(dgB, dbB, small, g, beta, a_log_row, dt_row, H)
    dproj = jnp.concatenate([dax, dab, dac, dq_pre, dk_pre, dv_pre, dz], axis=1)
    dW_in_main = _mm(h1, dproj, mode="tn", out_dtypes=[F32], name="mm_dw_in")
    dW_in_small = _mm(h1, dsmall, mode="tn", out_dtypes=[F32], name="mm_dw_in_small")
    P_in = _halves(_split_shards(dW_in_main, dW_in_small, IN_COLS // 4))
    (A_in,) = _run_comm(_swap_comm(P_in), "rs_w_in_swap")
    S_in = _add_half([P_in], [A_in], cidx, "rs_w_in_add")
    dh1, (B_in,) = _mm(dproj, w_in_main, mode="nt", out_dtypes=[F32], name="mm_dh1", comm=_a2a_comm(S_in))
    dh1 = _mm(dsmall, w_in_small, mode="nt", out_dtypes=[F32], epi=add, extras=[dh1], name="mm_dh1_small")
    dx, _, d_gmix = _rms_bwd(dh1, xs, norm_mix_g, dx1, "rms1_bwd")

    chip_idx = chip.astype(jnp.int32).reshape(1)

    def update(S1, B, w, m, v, name):
        gr = _finish_shard(S1, B, cidx, chip_idx, "rs_" + name)
        delta, m2, v2 = _adamw(w[0], gr, m[0], v[0], "adamw_" + name)
        return gr[None], delta[None], m2[None], v2[None]

    big = {
        "w_in": update(S_in, B_in, w_in, m_w_in, v_w_in, "w_in"),
        "w_out": update(S_out, B_out, w_out, m_w_out, v_w_out, "w_out"),
        "w_up": update(S_up, B_up, w_up, m_w_up, v_w_up, "w_up"),
        "w_down": update(S_down, B_down, w_down, m_w_down, v_w_down, "w_down"),
        "w_ple_gate": update(S_pg, B_pg, w_ple_gate, m_w_ple_gate, v_w_ple_gate, "w_pg"),
        "w_ple_proj": update(S_pp, B_pp, w_ple_proj, m_w_ple_proj, v_w_ple_proj, "w_pp"),
    }

    small_grads = [d_gmix[0:1], dcw_a[:cw_a.shape[0]], jnp.concatenate([dcw_q, dcw_k, dcw_v], axis=1)[:cw_qkv.shape[0]],
                   d_ab[0:1, :H], d_ab[1:2, :H], d_gdn[0:1], d_gffn[0:1],
                   jnp.concatenate([dcw_fg, dcw_fv], axis=1)[:cw_ffn.shape[0]], d_gple[0:1], d_gfin]
    small_shapes = [v.shape for v in small_grads]
    gpack = _pack_rows(small_grads)
    gsum = _sum_stack(_gather_all(gpack, "ag_small"), "sum_small")
    (g_gmix, g_cwa, g_cwqkv, g_alog, g_dt, g_gdn, g_gffn, g_cwffn, g_gple, g_gfin) = _unpack_rows(gsum, small_shapes)

    def my_cols(v):
        Cc = v.shape[1] // 4
        return lax.dynamic_slice_in_dim(v, chip * Cc, Cc, axis=1)

    g_small = [g_gmix, my_cols(g_cwa), my_cols(g_cwqkv), g_alog, g_dt, g_gdn, g_gffn, my_cols(g_cwffn), g_gple, g_gfin]
    w_small = [norm_mix_g, conv_a_w[0], conv_qkv_w[0], a_log, dt_bias, dn_norm_g, norm_ffn_g, conv_ffn_w[0], norm_ple_g, gfin]
    m_small = [m_norm_mix_g, m_conv_a_w[0], m_conv_qkv_w[0], m_a_log, m_dt_bias, m_dn_norm_g, m_norm_ffn_g, m_conv_ffn_w[0],
               m_norm_ple_g, m_final_norm_g.reshape(1, D)]
    v_small = [v_norm_mix_g, v_conv_a_w[0], v_conv_qkv_w[0], v_a_log, v_dt_bias, v_dn_norm_g, v_norm_ffn_g, v_conv_ffn_w[0],
               v_norm_ple_g, v_final_norm_g.reshape(1, D)]
    shp = [v.shape for v in w_small]
    ds_, ms_, vs_ = _adamw(_pack_rows(w_small), _pack_rows(g_small), _pack_rows(m_small), _pack_rows(v_small), "adamw_small")
    out_shapes = [norm_mix_g.shape, conv_a_w.shape, conv_qkv_w.shape, a_log.shape, dt_bias.shape, dn_norm_g.shape,
                  norm_ffn_g.shape, conv_ffn_w.shape, norm_ple_g.shape, final_norm_g.shape]
    rs = lambda vals: [v.reshape(s) for v, s in zip(vals, out_shapes)]
    sg, sd_, sm_, sv_ = rs(g_small), rs(_unpack_rows(ds_, shp)), rs(_unpack_rows(ms_, shp)), rs(_unpack_rows(vs_, shp))
    names_small = ["norm_mix_g", "conv_a_w", "conv_qkv_w", "a_log", "dt_bias", "dn_norm_g", "norm_ffn_g", "conv_ffn_w",
                   "norm_ple_g", "final_norm_g"]
    res = {n: (sg[i], sd_[i], sm_[i], sv_[i]) for i, n in enumerate(names_small)}
    res.update(big)
    order = ["norm_mix_g", "w_in", "conv_a_w", "conv_qkv_w", "a_log", "dt_bias", "dn_norm_g", "w_out", "norm_ffn_g", "w_up",
             "conv_ffn_w", "w_down", "norm_ple_g", "w_ple_gate", "w_ple_proj", "final_norm_g"]
    return (loss, dx[None], *[res[n][0] for n in order], *[res[n][1] for n in order], *[res[n][2] for n in order],
            *[res[n][3] for n in order])
```

```python
import functools

import jax
import jax.numpy as jnp
from jax import lax
from jax.experimental import pallas as pl
from jax.experimental.pallas import tpu as pltpu

F32 = jnp.float32
BF16 = jnp.bfloat16
LANES = 128
HALO = 8
HEAD_DIM = 128
CHUNK = 64
EPS = 1e-6
VMEM_LIMIT = 56 * 1024 * 1024
MM_VMEM_BUDGET = 40 * 1024 * 1024
MM_STEP_BYTES = 1 << 20
EW_VMEM_BUDGET = 28 * 1024 * 1024
MESH = pl.DeviceIdType.MESH

ADAM_LR, ADAM_B1, ADAM_B2, ADAM_EPS, ADAM_WD, ADAM_STEP = 0.001, 0.9, 0.999, 1e-08, 0.01, 10


def _tile(n, cap, unit):
    if n <= cap:
        return n
    d = (cap // unit) * unit
    while d >= unit:
        if n % d == 0:
            return d
        d -= unit
    raise ValueError(f"no tile for {n} (cap {cap}, unit {unit})")


def _sigmoid(x):
    return 1.0 / (1.0 + jnp.exp(-x))


def _divisors(n, cap):
    ds = [d for d in range(cap // LANES * LANES, 0, -LANES) if n % d == 0]
    return [n] if (n <= cap or not ds) else ds


def _mm_tiles(M, N, K, n_unit, k_unit, a_bytes, n_blocks_mn, a_transposed):
    best = None
    for tm in _divisors(M, 1536):
        for tn in _divisors(n_unit, 1536):
            for tk in _divisors(k_unit, 4096):
                nk = K // tk
                vmem = 2 * tm * tk * a_bytes + 2 * tk * tn * 2 + 2 * 4 * tm * tn * n_blocks_mn + (4 * tm * tn if nk > 1 else 0)
                if vmem > MM_VMEM_BUDGET:
                    continue
                steps = (M // tm) * (N // tn) * nk
                cost = (M * K * a_bytes * (N // tn if nk > 1 else 1) + K * N * 2 * (M // tm) + 4 * M * N * n_blocks_mn
                        + (8 * M * N * nk // 3 if nk > 1 else 0) + steps * MM_STEP_BYTES
                        + (2 * steps * tm * tk if a_transposed else 0))
                if best is None or cost < best[0]:
                    best = (cost, tm, tn, tk)
    return best[1:]


def _mm(a, b, *, mode, out_dtypes, name, epi=None, extras=(), comm=None, b_split=None, out_split=None):
    if b_split is not None:
        lo, ns = b_split
        Rb, Cb = b.shape[1], b.shape[2]
    if mode == "nn":
        (M, K), N = a.shape, (ns * Cb if b_split else b.shape[1])
    elif mode == "nt":
        (M, K), N = a.shape, (Rb if b_split else b.shape[0])
    else:
        (K, M), N = a.shape, b.shape[1]
    n_ex, n_out = len(extras), len(out_dtypes)
    n_unit = Cb if (b_split and mode == "nn") else (N // out_split if out_split else N)
    k_unit = Cb if (b_split and mode == "nt") else K
    tm, tn, tk = _mm_tiles(M, N, K, n_unit, k_unit, a.dtype.itemsize, n_ex + n_out, mode == "tn")
    nk = K // tk
    a_spec = pl.BlockSpec((tk, tm), lambda i, j, k: (k, i)) if mode == "tn" else pl.BlockSpec((tm, tk), lambda i, j, k: (i, k))
    if b_split and mode == "nn":
        nb = Cb // tn
        b_spec = pl.BlockSpec((None, tk, tn), lambda i, j, k: (lo + j // nb, k, j % nb))
    elif b_split:
        nb = Cb // tk
        b_spec = pl.BlockSpec((None, tn, tk), lambda i, j, k: (lo + k // nb, j, k % nb))
    else:
        b_spec = pl.BlockSpec((tn, tk), lambda i, j, k: (j, k)) if mode == "nt" else pl.BlockSpec((tk, tn), lambda i, j, k: (k, j))
    mn_spec = pl.BlockSpec((tm, tn), lambda i, j, k: (i, j))
    out_shapes = [jax.ShapeDtypeStruct((M, N), dt) for dt in out_dtypes]
    out_specs = [mn_spec] * n_out
    if out_split:
        assert n_ex == 0 and n_out == 1
        nbo = (N // out_split) // tn
        out_specs = [pl.BlockSpec((None, tm, tn), lambda i, j, k: (j // nbo, i, j % nbo))]
        out_shapes = [jax.ShapeDtypeStruct((out_split, M, N // out_split), out_dtypes[0])]
    dims = {"nn": (((1,), (0,)), ((), ())), "nt": (((1,), (1,)), ((), ())), "tn": (((0,), (0,)), ((), ()))}[mode]

    def body(*refs):
        a_ref, b_ref = refs[0], refs[1]
        ex_refs = refs[2:2 + n_ex]
        out_refs = refs[2 + n_ex:2 + n_ex + n_out]
        part = lax.dot_general(a_ref[...].astype(BF16), b_ref[...].astype(BF16), dims, preferred_element_type=F32)

        def finish(acc):
            outs = (acc,) if epi is None else epi(acc, *[r[...] for r in ex_refs])
            for r, o in zip(out_refs, outs):
                r[...] = o.astype(r.dtype)

        if nk == 1:
            finish(part)
            return
        acc_ref = refs[-1]
        k = pl.program_id(2)

        @pl.when(k == 0)
        def _():
            acc_ref[...] = part

        @pl.when(jnp.logical_and(k > 0, k < nk - 1))
        def _():
            acc_ref[...] += part

        @pl.when(k == nk - 1)
        def _():
            finish(acc_ref[...] + part)

    outs, comm_outs = _call(
        body, name=name, grid=(M // tm, N // tn, nk),
        in_specs=[a_spec, b_spec] + [mn_spec] * n_ex,
        out_specs=out_specs,
        out_shape=out_shapes,
        scratch_shapes=[pltpu.VMEM((tm, tn), F32)] if nk > 1 else [],
        semantics=("parallel", "parallel", "arbitrary"), args=(a, b, *extras), comm=comm)
    res = outs[0] if n_out == 1 else outs
    return res if comm is None else (res, comm_outs)


def _tiled(fn, *, T, C, ins, out_dtypes=(), acc_rows=(), tb=None, cb=512, name, comm=None):
    tb = _tile(T, tb or (512 if cb <= 1024 else 256), HALO)
    nI, nJ = T // tb, C // cb
    hb, nH = tb // HALO, T // HALO
    specs, args, kinds = [], [], []
    for kind, arr, cmap in ins:
        cm = cmap if cmap is not None else (lambda j: j)
        kinds.append(kind)
        if kind == "cur":
            specs.append(pl.BlockSpec((tb, cb), lambda j, i, cm=cm: (i, cm(j))))
            args.append(arr)
        elif kind == "ext":
            specs.append(pl.BlockSpec((HALO, cb), lambda j, i, cm=cm: (jnp.maximum(i * hb - 1, 0), cm(j))))
            specs.append(pl.BlockSpec((tb, cb), lambda j, i, cm=cm: (i, cm(j))))
            specs.append(pl.BlockSpec((HALO, cb), lambda j, i, cm=cm: (jnp.minimum((i + 1) * hb, nH - 1), cm(j))))
            args += [arr, arr, arr]
        elif kind == "row":
            specs.append(pl.BlockSpec((arr.shape[0], cb), lambda j, i, cm=cm: (0, cm(j))))
            args.append(arr)
        elif kind == "stack":
            specs.append(pl.BlockSpec((arr.shape[0], tb, cb), lambda j, i, cm=cm: (0, i, cm(j))))
            args.append(arr)
        else:
            raise ValueError(kind)
    n_in = len(args)
    n_out, n_acc = len(out_dtypes), len(acc_rows)

    def body(*refs):
        j, i = pl.program_id(0), pl.program_id(1)
        vals, r = [], 0
        for kind in kinds:
            if kind == "ext":
                prev = jnp.where(i == 0, 0.0, refs[r][...].astype(F32))
                cur = refs[r + 1][...].astype(F32)
                nxt = jnp.where(i == nI - 1, 0.0, refs[r + 2][...].astype(F32))
                vals.append(jnp.concatenate([prev, cur, nxt], axis=0))
                r += 3
            else:
                vals.append(refs[r][...])
                r += 1
        res = fn(j, i, *vals)
        for ref, o in zip(refs[n_in:n_in + n_out], res[:n_out]):
            ref[...] = o.astype(ref.dtype)
        for ref, o in zip(refs[n_in + n_out:], res[n_out:]):
            @pl.when(i == 0)
            def _(ref=ref, o=o):
                ref[...] = o

            @pl.when(i > 0)
            def _(ref=ref, o=o):
                ref[...] += o

    outs, comm_outs = _call(
        body, name=name, grid=(nJ, nI), in_specs=specs,
        out_specs=[pl.BlockSpec((tb, cb), lambda j, i: (i, j))] * n_out
        + [pl.BlockSpec((rows, cb), lambda j, i: (0, j)) for rows in acc_rows],
        out_shape=[jax.ShapeDtypeStruct((T, C), dt) for dt in out_dtypes]
        + [jax.ShapeDtypeStruct((rows, C), F32) for rows in acc_rows],
        scratch_shapes=[], semantics=("parallel", "arbitrary"), args=args, comm=comm)
    return outs if comm is None else (outs, comm_outs)


def _conv_causal(xe, w):
    K = w.shape[0]
    y = xe * w[K - 1:K]
    for j in range(K - 1):
        y = y + pltpu.roll(xe, K - 1 - j, 0) * w[j:j + 1]
    return y


def _conv_anti(de, w):
    K, n = w.shape[0], de.shape[0]
    y = de * w[K - 1:K]
    for j in range(K - 1):
        y = y + pltpu.roll(de, n - (K - 1 - j), 0) * w[j:j + 1]
    return y


def _conv_dw(dce, xe, K):
    n = dce.shape[0]
    tb = n - 2 * HALO
    rows = []
    for j in range(K):
        xs = xe if j == K - 1 else pltpu.roll(xe, K - 1 - j, 0)
        rows.append(jnp.sum((dce * xs)[HALO:HALO + tb], axis=0, keepdims=True))
    rows.append(jnp.zeros((HALO - K, dce.shape[1]), F32))
    return jnp.concatenate(rows, axis=0)


def _own(xe):
    return xe[HALO:xe.shape[0] - HALO]


def _row0(v):
    return jnp.concatenate([v, jnp.zeros((HALO - 1, v.shape[1]), F32)], axis=0)


def _per_head(fn, *xs):
    n = xs[0].shape[1] // HEAD_DIM
    outs = [fn(*[x[:, g * HEAD_DIM:(g + 1) * HEAD_DIM] for x in xs]) for g in range(n)]
    return outs[0] if n == 1 else jnp.concatenate(outs, axis=1)


def _rms_fwd(x, g, name):
    T, D = x.shape

    def fn(j, i, xv, gv):
        r = lax.rsqrt(jnp.mean(xv * xv, axis=1, keepdims=True) + EPS)
        return (xv * r * gv,)

    return _tiled(fn, T=T, C=D, ins=[("cur", x, None), ("row", g, None)], out_dtypes=[BF16], cb=D, name=name)[0]


def _rms_bwd_math(dy, xv, gv):
    r = lax.rsqrt(jnp.mean(xv * xv, axis=1, keepdims=True) + EPS)
    xh = xv * r
    dxh = dy * gv
    dx = r * (dxh - xh * jnp.mean(dxh * xh, axis=1, keepdims=True))
    dg = jnp.sum(dy * xh, axis=0, keepdims=True)
    return dx, dg


def _rms_bwd(dh, x, g, dres, name, comm=None):
    T, D = x.shape

    def fn(j, i, dhv, xv, gv, dr):
        dx, dg = _rms_bwd_math(dhv, xv, gv)
        return dr + dx, dr + dx, _row0(dg)

    return _tiled(fn, T=T, C=D, ins=[("cur", dh, None), ("cur", x, None), ("row", g, None), ("cur", dres, None)],
                  out_dtypes=[F32, BF16], acc_rows=[HALO], cb=D, name=name, comm=comm)


def _final_fb(x3, tgt, g, pp, pg):
    T, D = x3.shape

    def fn(j, i, xv, tv, gv, ppv, pgv):
        r = lax.rsqrt(jnp.mean(xv * xv, axis=1, keepdims=True) + EPS)
        xh = xv * r
        e = xh * gv - tv
        dy = e * (1.0 / D)
        dxh = dy * gv
        dx = r * (dxh - xh * jnp.mean(dxh * xh, axis=1, keepdims=True))
        dg = jnp.sum(dy * xh, axis=0, keepdims=True)
        ls = jnp.sum(e * e, axis=0, keepdims=True) * (0.5 / D)
        return (dx, dx * ppv * pgv * (1.0 - pgv), dx * pgv,
                jnp.concatenate([dg, ls, jnp.zeros((HALO - 2, D), F32)], axis=0))

    return _tiled(fn, T=T, C=D, ins=[("cur", x3, None), ("cur", tgt, None), ("row", g, None), ("cur", pp, None),
                                      ("cur", pg, None)], out_dtypes=[F32, BF16, BF16], acc_rows=[HALO], cb=D, name="final_fb")


def _ga_fwd(proj, w_a, CW, cb):
    T = proj.shape[0]
    n = CW // cb

    def fn(j, i, ax, ab, ac, w):
        c = _conv_causal(ac * ax, w)
        return (ab * _own(c),)

    return _tiled(fn, T=T, C=CW, ins=[("ext", proj, None), ("cur", proj, lambda j: j + n), ("ext", proj, lambda j: j + 2 * n),
                                       ("row", w_a, None)], out_dtypes=[BF16], cb=cb, name="ga_fwd")[0]


def _ga_bwd(dymix, proj, w_a, CW, cb):
    T = proj.shape[0]
    n = CW // cb
    K = w_a.shape[0]

    def fn(j, i, dy, ax, ab, ac, w):
        u = ac * ax
        c = _conv_causal(u, w)
        dc = dy * ab
        du = _conv_anti(dc, w)
        return _own(du * ac), _own(dy * c), _own(du * ax), _conv_dw(dc, u, K)

    return _tiled(fn, T=T, C=CW, ins=[("ext", dymix, None), ("ext", proj, None), ("ext", proj, lambda j: j + n),
                                       ("ext", proj, lambda j: j + 2 * n), ("row", w_a, None)],
                  out_dtypes=[BF16, BF16, BF16], acc_rows=[HALO], cb=cb, name="ga_bwd")


def _l2n(s):
    return s * lax.rsqrt(jnp.sum(s * s, axis=1, keepdims=True) + EPS)


def _qkv_fwd(proj, w_sec, coff, normalize, DNW, cb, name, comm=None):
    T = proj.shape[0]

    def fn(j, i, pre, w):
        c = _own(_conv_causal(pre, w))
        s = c * _sigmoid(c)
        return (_per_head(_l2n, s) if normalize else s,)

    res = _tiled(fn, T=T, C=DNW, ins=[("ext", proj, lambda j: j + coff), ("row", w_sec, None)],
                 out_dtypes=[F32], cb=cb, name=name, comm=comm)
    return res[0] if comm is None else (res[0][0], res[1])


def _qkv_bwd(dsec, proj, w_sec, coff, normalize, DNW, cb, name):
    T = proj.shape[0]
    K = w_sec.shape[0]

    def l2n_bwd(s, dn):
        r = lax.rsqrt(jnp.sum(s * s, axis=1, keepdims=True) + EPS)
        nrm = s * r
        return r * (dn - nrm * jnp.sum(dn * nrm, axis=1, keepdims=True))

    def fn(j, i, dn, pre, w):
        c = _conv_causal(pre, w)
        sg = _sigmoid(c)
        s = c * sg
        ds = _per_head(l2n_bwd, s, dn) if normalize else dn
        dc = ds * (sg * (1.0 + c * (1.0 - sg)))
        return _own(_conv_anti(dc, w)), _conv_dw(dc, pre, K)

    return _tiled(fn, T=T, C=DNW, ins=[("ext", dsec, None), ("ext", proj, lambda j: j + coff), ("row", w_sec, None)],
                  out_dtypes=[BF16], acc_rows=[HALO], cb=cb, name=name)


def _gb_fwd(small, a_log_row, dt_row, H):
    T = small.shape[0]

    def fn(j, i, sm, al, dt):
        z = sm + dt
        sp = jnp.maximum(z, 0.0) + jnp.log(1.0 + jnp.exp(-jnp.abs(z)))
        g = -jnp.exp(al) * sp
        beta = _sigmoid(pltpu.roll(sm, LANES - H, 1))
        return g, beta

    return _tiled(fn, T=T, C=LANES, ins=[("cur", small, None), ("row", a_log_row, None), ("row", dt_row, None)],
                  out_dtypes=[F32, F32], cb=LANES, name="gb_fwd")


def _gb_bwd(dgB, dbB, small, g, beta, a_log_row, dt_row, H):
    T = small.shape[0]

    def fn(j, i, dgv, dbv, sm, gv, bv, al, dt):
        lane = lax.broadcasted_iota(jnp.int32, sm.shape, 1)
        dg = jnp.zeros(sm.shape, F32)
        db = jnp.zeros(sm.shape, F32)
        for h in range(H):
            dg = jnp.where(lane == h, jnp.sum(dgv[h], axis=1, keepdims=True), dg)
            db = jnp.where(lane == h, jnp.sum(dbv[h], axis=1, keepdims=True), db)
        da = dg * (-jnp.exp(al)) * _sigmoid(sm + dt)
        dbb = db * bv * (1.0 - bv)
        dsm = jnp.where(lane < H, da, 0.0) + pltpu.roll(jnp.where(lane < H, dbb, 0.0), H, 1)
        d_alog = jnp.sum(jnp.where(lane < H, dg * gv, 0.0), axis=0, keepdims=True)
        d_dt = jnp.sum(jnp.where(lane < H, da, 0.0), axis=0, keepdims=True)
        return dsm, jnp.concatenate([d_alog, d_dt, jnp.zeros((HALO - 2, LANES), F32)], axis=0)

    return _tiled(fn, T=T, C=LANES, ins=[("stack", dgB, None), ("stack", dbB, None), ("cur", small, None), ("cur", g, None),
                                          ("cur", beta, None), ("row", a_log_row, None), ("row", dt_row, None)],
                  out_dtypes=[BF16], acc_rows=[HALO], cb=LANES, name="gb_bwd")


_DIMS = {"nn": (((1,), (0,)), ((), ())), "nt": (((1,), (1,)), ((), ())), "tn": (((0,), (0,)), ((), ()))}
_DOT_BWD = {"nn": (("nt", "gb"), ("tn", "ag")), "nt": (("nn", "gb"), ("tn", "ga")), "tn": (("nt", "bg"), ("nn", "ag"))}


def _split(a):
    hi = a.astype(BF16)
    return hi, (a - hi.astype(F32)).astype(BF16)


def _raw_dot(a, b, kind, passes):
    dg = lambda x, y: lax.dot_general(x, y, _DIMS[kind], preferred_element_type=F32)
    if passes == 1:
        return dg(a.astype(BF16), b.astype(BF16))
    ah, al = _split(a)
    bh, bl = _split(b)
    if kind == "tn":
        return dg(ah, bh) + (dg(ah, bl) + dg(al, bh))
    m = a.shape[0]
    top = dg(jnp.concatenate([ah, al], axis=0), bh)
    return top[:m] + (dg(ah, bl) + top[m:])


def _raw_dot_exact(a, b, kind, exact):
    dg = lambda x, y: lax.dot_general(x, y, _DIMS[kind], preferred_element_type=F32)
    if exact == "a":
        bh, bl = _split(b)
        return dg(a.astype(BF16), bh) + dg(a.astype(BF16), bl)
    ah, al = _split(a)
    return dg(ah, b.astype(BF16)) + dg(al, b.astype(BF16))


@functools.lru_cache(maxsize=None)
def _dotc(kind):
    @jax.custom_vjp
    def f(a, b):
        return _raw_dot_exact(a, b, kind, "a")

    def fwd(a, b):
        return _raw_dot_exact(a, b, kind, "a"), a

    def bwd(a, g):
        db = _raw_dot_exact(a, g, "tn", "a") if kind == "nn" else _raw_dot_exact(g, a, "tn", "b")
        return jnp.zeros_like(a), db

    f.defvjp(fwd, bwd)
    return f


@functools.lru_cache(maxsize=None)
def _dotf(kind, passes):
    @jax.custom_vjp
    def f(a, b):
        return _raw_dot(a, b, kind, passes)

    def fwd(a, b):
        return _raw_dot(a, b, kind, passes), (a, b)

    def bwd(res, g):
        ops = {"a": res[0], "b": res[1], "g": g}
        (ka, oa), (kb, ob) = _DOT_BWD[kind]
        return (_raw_dot(ops[oa[0]], ops[oa[1]], ka, passes), _raw_dot(ops[ob[0]], ops[ob[1]], kb, passes))

    f.defvjp(fwd, bwd)
    return f


@jax.custom_vjp
def _saved_inverse(L, inv):
    return inv


def _saved_inverse_fwd(L, inv):
    return inv, inv


def _saved_inverse_bwd(inv, g):
    d3nt, d3tn = _dotf("nt", 3), _dotf("tn", 3)
    return -d3nt(d3tn(inv, g), inv), jnp.zeros_like(inv)


_saved_inverse.defvjp(_saved_inverse_fwd, _saved_inverse_bwd)


def _chunk_fn(q, k, v, gB, bB, S, inv_saved=None):
    C = CHUNK
    d3 = _dotf("nn", 3)
    d1, d1nt, d1tn = _dotf("nn", 1), _dotf("nt", 1), _dotf("tn", 1)
    each = lambda f, *ls: tuple(f(*xs) for xs in zip(*ls))
    row = lax.broadcasted_iota(jnp.int32, (C, C), 0)
    col = lax.broadcasted_iota(jnp.int32, (C, C), 1)
    causal = row >= col
    strict = row > col
    tril = jnp.where(causal, 1.0, 0.0).astype(F32)
    eye = jnp.where(row == col, 1.0, 0.0).astype(F32)
    avg = jnp.full((C, HEAD_DIM), 1.0 / HEAD_DIM, F32)
    gc = each(lambda g: _dotc("nn")(tril, g), gB)
    R = each(lambda g: _dotc("nt")(avg, g), gc)
    decay = each(lambda g, r: jnp.where(causal, jnp.exp(jnp.where(causal, g[:, :C] - r, 0.0)), 0.0), gc, R)
    kk = each(lambda x: d1nt(x, x), k)
    L = each(lambda a, d, b: jnp.where(strict, a * d * b[:, :C], 0.0), kk, decay, bB)
    if inv_saved is None:
        inv = each(lambda l: eye - l, L)
        P = L
        for _ in range(5):
            P = each(lambda p: d3(p, p), P)
            inv = each(lambda a, p: d3(a, eye + p), inv, P)
    else:
        inv = each(_saved_inverse, L, inv_saved)
    eg = each(jnp.exp, gc)
    u = each(lambda a, x, b: d3(a, x * b), inv, v, bB)
    w = each(lambda a, x, b, e: d3(a, x * b * e), inv, k, bB, eg)
    qs = each(lambda x: x * (HEAD_DIM ** -0.5), q)
    qk = each(lambda a, x, d: d1nt(a, x) * d, qs, k, decay)
    gl = each(lambda g: g[C - 1:C, :], gc)
    kd = each(lambda x, a, g: x * jnp.exp(a - g), k, gl, gc)
    qe = each(lambda a, e: a * e, qs, eg)
    nh = len(S)
    o = ()
    for c in range(len(q) // nh):
        sl = slice(c * nh, (c + 1) * nh)
        v_new = each(lambda a, b, s: a - d1(b, s), u[sl], w[sl], S)
        o1 = each(lambda a, s: d1(a, s), qe[sl], S)
        o += each(lambda a, b, vn: a + d1(b, vn), o1, qk[sl], v_new)
        kv = each(lambda x, vn: d1tn(x, vn), kd[sl], v_new)
        S = each(lambda s, a, b: s * jnp.exp(a) + b, S, gl[sl], kv)
    return (o, S), inv


def _sel_lane(x, h):
    lane = lax.broadcasted_iota(jnp.int32, x.shape, 1)
    return jnp.broadcast_to(jnp.sum(jnp.where(lane == h, x, 0.0), axis=1, keepdims=True), x.shape)


def _tile_of(ref, c, h):
    return ref[c * CHUNK:(c + 1) * CHUNK, h * HEAD_DIM:(h + 1) * HEAD_DIM]


def _chunks_per_step(N):
    return 4 if N % 4 == 0 else (2 if N % 2 == 0 else 1)


def _delta_fwd(q, k, v, g, beta, comm=None):
    T = q.shape[0]
    H, N = q.shape[1] // HEAD_DIM, T // CHUNK
    cps = _chunks_per_step(N)
    rows = cps * CHUNK

    def body(q_ref, k_ref, v_ref, g_ref, b_ref, o_ref, s_ref, inv_ref, S):
        @pl.when(pl.program_id(0) == 0)
        def _():
            S[...] = jnp.zeros_like(S)

        gv, bv = g_ref[...], b_ref[...]
        pairs = lambda f: tuple(f(c, h) for c in range(cps) for h in range(H))
        S_in = tuple(S[h] for h in range(H))
        for h in range(H):
            s_ref[h, 0] = S_in[h]
        (o, S_new), inv = _chunk_fn(pairs(lambda c, h: _tile_of(q_ref, c, h)), pairs(lambda c, h: _tile_of(k_ref, c, h)),
                                    pairs(lambda c, h: _tile_of(v_ref, c, h)),
                                    pairs(lambda c, h: _sel_lane(gv[c * CHUNK:(c + 1) * CHUNK], h)),
                                    pairs(lambda c, h: _sel_lane(bv[c * CHUNK:(c + 1) * CHUNK], h)), S_in)
        for c in range(cps):
            for h in range(H):
                o_ref[c * CHUNK:(c + 1) * CHUNK, h * HEAD_DIM:(h + 1) * HEAD_DIM] = o[c * H + h]
                inv_ref[h, c] = inv[c * H + h]
        for h in range(H):
            S[h] = S_new[h]

    blk = pl.BlockSpec((rows, H * HEAD_DIM), lambda n: (n, 0))
    gblk = pl.BlockSpec((rows, LANES), lambda n: (n, 0))
    outs, comm_outs = _call(
        body, name="delta_fwd", grid=(N // cps,), in_specs=[blk, blk, blk, gblk, gblk],
        out_specs=[blk, pl.BlockSpec((H, 1, HEAD_DIM, HEAD_DIM), lambda n: (0, n, 0, 0)),
                   pl.BlockSpec((H, cps, CHUNK, CHUNK), lambda n: (0, n, 0, 0))],
        out_shape=[jax.ShapeDtypeStruct((T, H * HEAD_DIM), F32), jax.ShapeDtypeStruct((H, N // cps, HEAD_DIM, HEAD_DIM), F32),
                   jax.ShapeDtypeStruct((H, N, CHUNK, CHUNK), F32)],
        scratch_shapes=[pltpu.VMEM((H, HEAD_DIM, HEAD_DIM), F32)],
        semantics=("arbitrary",), args=(q, k, v, g, beta), comm=comm)
    return outs[0], outs[1], outs[2], comm_outs


def _delta_bwd(q, k, v, g, beta, S0, inv, do, comm=None):
    T = q.shape[0]
    H, N = q.shape[1] // HEAD_DIM, T // CHUNK
    cps = _chunks_per_step(N)
    rows, NS = cps * CHUNK, N // cps

    def body(q_ref, k_ref, v_ref, g_ref, b_ref, s_ref, inv_ref, do_ref, dq_ref, dk_ref, dv_ref, dg_ref, db_ref, dS):
        @pl.when(pl.program_id(0) == 0)
        def _():
            dS[...] = jnp.zeros_like(dS)

        gv, bv = g_ref[...], b_ref[...]
        pairs = lambda f: tuple(f(c, h) for c in range(cps) for h in range(H))
        heads = lambda f: tuple(f(h) for h in range(H))
        _, vjp, _ = jax.vjp(_chunk_fn, pairs(lambda c, h: _tile_of(q_ref, c, h)), pairs(lambda c, h: _tile_of(k_ref, c, h)),
                            pairs(lambda c, h: _tile_of(v_ref, c, h)),
                            pairs(lambda c, h: _sel_lane(gv[c * CHUNK:(c + 1) * CHUNK], h)),
                            pairs(lambda c, h: _sel_lane(bv[c * CHUNK:(c + 1) * CHUNK], h)),
                            heads(lambda h: s_ref[h, 0]), pairs(lambda c, h: inv_ref[h, c]), has_aux=True)
        dq, dk, dv, dgB, dbB, dS_prev, _ = vjp((pairs(lambda c, h: _tile_of(do_ref, c, h)), heads(lambda h: dS[h])))
        for c in range(cps):
            for h in range(H):
                r, sl = slice(c * CHUNK, (c + 1) * CHUNK), slice(h * HEAD_DIM, (h + 1) * HEAD_DIM)
                dq_ref[r, sl] = dq[c * H + h]
                dk_ref[r, sl] = dk[c * H + h]
                dv_ref[r, sl] = dv[c * H + h]
                dg_ref[h, r] = dgB[c * H + h]
                db_ref[h, r] = dbB[c * H + h]
        for h in range(H):
            dS[h] = dS_prev[h]

    blk = pl.BlockSpec((rows, H * HEAD_DIM), lambda n: (NS - 1 - n, 0))
    gblk = pl.BlockSpec((rows, LANES), lambda n: (NS - 1 - n, 0))
    hblk = pl.BlockSpec((H, rows, LANES), lambda n: (0, NS - 1 - n, 0))
    sd = jax.ShapeDtypeStruct
    outs, comm_outs = _call(
        body, name="delta_bwd", grid=(NS,),
        in_specs=[blk, blk, blk, gblk, gblk, pl.BlockSpec((H, 1, HEAD_DIM, HEAD_DIM), lambda n: (0, NS - 1 - n, 0, 0)),
                  pl.BlockSpec((H, cps, CHUNK, CHUNK), lambda n: (0, NS - 1 - n, 0, 0)), blk],
        out_specs=[blk, blk, blk, hblk, hblk],
        out_shape=[sd((T, H * HEAD_DIM), F32)] * 3 + [sd((H, T, LANES), F32)] * 2,
        scratch_shapes=[pltpu.VMEM((H, HEAD_DIM, HEAD_DIM), F32)],
        semantics=("arbitrary",), args=(q, k, v, g, beta, S0, inv, do), comm=comm)
    return (*outs, comm_outs)


def _gnorm_fwd(o, proj, z_coff, gdn_t, DNW):
    T = o.shape[0]

    def fn(j, i, ov, zv, gv):
        def one(oh, zh, gh):
            r = lax.rsqrt(jnp.mean(oh * oh, axis=1, keepdims=True) + EPS)
            return oh * r * gh * (zh * _sigmoid(zh))
        return (_per_head(one, ov, zv, jnp.broadcast_to(gv, ov.shape)),)

    return _tiled(fn, T=T, C=DNW, ins=[("cur", o, None), ("cur", proj, lambda j: j + z_coff), ("row", gdn_t, None)],
                  out_dtypes=[BF16], cb=DNW, name="gnorm_fwd")[0]


def _gnorm_bwd(dymix, y_coff, o, proj, z_coff, gdn_t, DNW):
    T = o.shape[0]
    nh = DNW // HEAD_DIM

    def fn(j, i, dy, ov, zv, gv):
        dos, dzs, dgs = [], [], jnp.zeros((1, HEAD_DIM), F32)
        for h in range(nh):
            sl = slice(h * HEAD_DIM, (h + 1) * HEAD_DIM)
            dyh, oh, zh, gh = dy[:, sl].astype(F32), ov[:, sl], zv[:, sl], gv[:, sl]
            r = lax.rsqrt(jnp.mean(oh * oh, axis=1, keepdims=True) + EPS)
            on = oh * r
            sg = _sigmoid(zh)
            sz = zh * sg
            dzs.append(dyh * on * gh * (sg * (1.0 + zh * (1.0 - sg))))
            don = dyh * gh * sz
            dos.append(r * (don - on * jnp.mean(don * on, axis=1, keepdims=True)))
            dgs = dgs + jnp.sum(dyh * on * sz, axis=0, keepdims=True)
        cat = (lambda xs: xs[0] if nh == 1 else jnp.concatenate(xs, axis=1))
        return cat(dos), cat(dzs), _row0(dgs)

    T_ = T
    nI = T_ // _tile(T_, 256, HALO)
    tb = T_ // nI
    specs_cb = DNW

    def body_wrap():
        def body(dy_ref, o_ref, z_ref, g_ref, do_ref, dz_ref, dg_ref):
            i = pl.program_id(0)
            d_o, d_z, d_g = fn(0, i, dy_ref[...], o_ref[...], z_ref[...], g_ref[...])
            do_ref[...] = d_o
            dz_ref[...] = d_z.astype(dz_ref.dtype)

            @pl.when(i == 0)
            def _():
                dg_ref[...] = d_g

            @pl.when(i > 0)
            def _():
                dg_ref[...] += d_g

        return pl.pallas_call(
            body, name="gnorm_bwd", grid=(nI,),
            in_specs=[pl.BlockSpec((tb, specs_cb), lambda i: (i, y_coff)), pl.BlockSpec((tb, specs_cb), lambda i: (i, 0)),
                      pl.BlockSpec((tb, specs_cb), lambda i: (i, z_coff)), pl.BlockSpec((1, specs_cb), lambda i: (0, 0))],
            out_specs=[pl.BlockSpec((tb, specs_cb), lambda i: (i, 0)), pl.BlockSpec((tb, specs_cb), lambda i: (i, 0)),
                       pl.BlockSpec((HALO, HEAD_DIM), lambda i: (0, 0))],
            out_shape=[jax.ShapeDtypeStruct((T_, DNW), F32), jax.ShapeDtypeStruct((T_, DNW), BF16),
                       jax.ShapeDtypeStruct((HALO, HEAD_DIM), F32)],
            compiler_params=pltpu.CompilerParams(dimension_semantics=("arbitrary",), vmem_limit_bytes=VMEM_LIMIT),
        )(dymix, o, proj, gdn_t)

    return body_wrap()


def _ffn_fwd(up_g, up_v, w_g, w_v, cb):
    T, F = up_g.shape

    def fn(j, i, ug, uv, wg, wv):
        cg = _own(_conv_causal(ug, wg))
        cv = _own(_conv_causal(uv, wv))
        return (cg * _sigmoid(cg) * cv,)

    return _tiled(fn, T=T, C=F, ins=[("ext", up_g, None), ("ext", up_v, None), ("row", w_g, None), ("row", w_v, None)],
                  out_dtypes=[BF16], tb=1024, cb=cb, name="ffn_fwd")[0]


def _ffn_bwd(dact, up_g, up_v, w_g, w_v, cb):
    T, F = up_g.shape
    K = w_g.shape[0]

    def fn(j, i, da, ug, uv, wg, wv):
        cg = _conv_causal(ug, wg)
        cv = _conv_causal(uv, wv)
        sg = _sigmoid(cg)
        dgate = da * cv * (sg * (1.0 + cg * (1.0 - sg)))
        dval = da * (cg * sg)
        return (_own(_conv_anti(dgate, wg)), _own(_conv_anti(dval, wv)), _conv_dw(dgate, ug, K), _conv_dw(dval, uv, K))

    return _tiled(fn, T=T, C=F, ins=[("ext", dact, None), ("ext", up_g, None), ("ext", up_v, None), ("row", w_g, None),
                                      ("row", w_v, None)], out_dtypes=[BF16, BF16], acc_rows=[HALO, HALO], tb=1024, cb=cb,
                  name="ffn_bwd")


def _wide(R, Cc, n_f32, unit=HALO):
    cb = Cc if (Cc % LANES or Cc <= 4096) else _tile(Cc, 2048, LANES)
    cap = max(unit, EW_VMEM_BUDGET // (2 * 4 * n_f32 * cb) // unit * unit)
    return _tile(R, cap, unit), cb


def _adamw(w, g, m, v, name, comm=None):
    R, Cc = w.shape
    tb, cb = _wide(R, Cc, 7) if R % HALO == 0 else (R, _tile(Cc, EW_VMEM_BUDGET // (2 * 4 * 7 * R) // LANES * LANES, LANES))
    c1 = 1.0 / (1.0 - ADAM_B1 ** ADAM_STEP)
    c2 = 1.0 / (1.0 - ADAM_B2 ** ADAM_STEP)

    def fn(j, i, wv, gv, mv, vv):
        m2 = ADAM_B1 * mv + (1.0 - ADAM_B1) * gv
        v2 = ADAM_B2 * vv + (1.0 - ADAM_B2) * (gv * gv)
        delta = -ADAM_LR * ((m2 * c1) / (jnp.sqrt(v2 * c2) + ADAM_EPS) + ADAM_WD * wv)
        return delta, m2, v2

    return _tiled(fn, T=R, C=Cc, ins=[("cur", w, None), ("cur", g, None), ("cur", m, None), ("cur", v, None)],
                  out_dtypes=[F32, F32, F32], tb=tb, cb=cb, name=name, comm=comm)


def _join_shards(w4, n_main):
    S4, R, cs = w4.shape
    n_small = S4 * cs - n_main
    assert 0 < n_small <= LANES and n_small <= cs
    tb = _tile(R, 256, 2 * HALO)

    def body(w_ref, main_ref, small_ref):
        for t in range(S4 - 1):
            main_ref[:, t * cs:(t + 1) * cs] = w_ref[t]
        last = w_ref[S4 - 1]
        main_ref[:, (S4 - 1) * cs:] = last[:, :cs - n_small]
        small_ref[...] = jnp.zeros_like(small_ref)
        small_ref[:, :n_small] = last[:, cs - n_small:]

    return pl.pallas_call(
        body, name="join_w_in", grid=(R // tb,), in_specs=[pl.BlockSpec((S4, tb, cs), lambda i: (0, i, 0))],
        out_specs=[pl.BlockSpec((tb, n_main), lambda i: (i, 0)), pl.BlockSpec((tb, LANES), lambda i: (i, 0))],
        out_shape=[jax.ShapeDtypeStruct((R, n_main), w4.dtype), jax.ShapeDtypeStruct((R, LANES), w4.dtype)],
        compiler_params=pltpu.CompilerParams(dimension_semantics=("parallel",), vmem_limit_bytes=VMEM_LIMIT))(w4)


def _split_shards(main, small, cs):
    R, n_main = main.shape
    n_small = 4 * cs - n_main
    tb = _tile(R, 256, HALO)

    def body(main_ref, small_ref, out_ref):
        for t in range(3):
            out_ref[t] = main_ref[:, t * cs:(t + 1) * cs]
        out_ref[3, :, :cs - n_small] = main_ref[:, 3 * cs:]
        out_ref[3, :, cs - n_small:] = small_ref[:, :n_small]

    return pl.pallas_call(
        body, name="split_g_in", grid=(R // tb,),
        in_specs=[pl.BlockSpec((tb, n_main), lambda i: (i, 0)), pl.BlockSpec((tb, LANES), lambda i: (i, 0))],
        out_specs=pl.BlockSpec((4, tb, cs), lambda i: (0, i, 0)), out_shape=jax.ShapeDtypeStruct((4, R, cs), main.dtype),
        compiler_params=pltpu.CompilerParams(dimension_semantics=("parallel",), vmem_limit_bytes=VMEM_LIMIT))(main, small)


def _sum_stack(st, name):
    S, R, Cc = st.shape
    cb = _tile(Cc, 512, LANES) if Cc % LANES == 0 else Cc

    def fn(j, i, sv):
        t = sv[0]
        for s in range(1, S):
            t = t + sv[s]
        return (t,)

    return _tiled(fn, T=R, C=Cc, ins=[("stack", st, None)], out_dtypes=[F32], cb=cb, name=name)[0]


ANY = pl.BlockSpec(memory_space=pl.ANY)


def _place():
    x, y, c = lax.axis_index("x"), lax.axis_index("y"), lax.axis_index("c")
    return x, y, c, 2 * x + y


def _chip_dev(s, c):
    return (s // 2, s % 2, c)


class _Comm:
    def __init__(self, ins, out_shapes, sems, start, wait, aliases=None):
        self.ins, self.out_shapes, self.sems = list(ins), list(out_shapes), list(sems)
        self.start, self.wait, self.aliases = start, wait, dict(aliases or {})


def _merge(*comms):
    offs, i, o, s = [], 0, 0, 0
    for cm in comms:
        offs.append((i, o, s))
        i, o, s = i + len(cm.ins), o + len(cm.out_shapes), s + len(cm.sems)

    def part(refs, k, cm):
        i0, o0, s0 = offs[k]
        return refs[0][i0:i0 + len(cm.ins)], refs[1][o0:o0 + len(cm.out_shapes)], refs[2][s0:s0 + len(cm.sems)]

    def start(*refs):
        for k, cm in enumerate(comms):
            cm.start(*part(refs, k, cm))

    def wait(*refs):
        for k, cm in enumerate(comms):
            cm.wait(*part(refs, k, cm))

    aliases = {}
    for k, cm in enumerate(comms):
        for a, b in cm.aliases.items():
            aliases[offs[k][0] + a] = offs[k][1] + b
    return _Comm([a for cm in comms for a in cm.ins], [a for cm in comms for a in cm.out_shapes],
                 [a for cm in comms for a in cm.sems], start, wait, aliases)


def _call(body, *, name, grid, in_specs, out_specs, out_shape, scratch_shapes, semantics, args, comm=None):
    if comm is None:
        outs = pl.pallas_call(
            body, name=name, grid=grid, in_specs=in_specs, out_specs=out_specs, out_shape=out_shape,
            scratch_shapes=list(scratch_shapes),
            compiler_params=pltpu.CompilerParams(dimension_semantics=semantics, vmem_limit_bytes=VMEM_LIMIT))(*args)
        return list(outs), []
    n_in, n_out, n_scr = len(in_specs), len(out_specs), len(scratch_shapes)
    ci, co = len(comm.ins), len(comm.out_shapes)

    def wrapped(*refs):
        r = 0
        ins, r = refs[r:r + n_in], r + n_in
        cins, r = refs[r:r + ci], r + ci
        outs, r = refs[r:r + n_out], r + n_out
        couts, r = refs[r:r + co], r + co
        scr, r = refs[r:r + n_scr], r + n_scr
        csems = refs[r:]
        ids = [pl.program_id(a) for a in range(len(grid))]
        first, last = ids[0] == 0, ids[0] == grid[0] - 1
        for a in range(1, len(grid)):
            first = jnp.logical_and(first, ids[a] == 0)
            last = jnp.logical_and(last, ids[a] == grid[a] - 1)

        @pl.when(first)
        def _():
            comm.start(cins, couts, csems)

        body(*ins, *outs, *scr)

        @pl.when(last)
        def _():
            comm.wait(cins, couts, csems)

    outs = pl.pallas_call(
        wrapped, name=name, grid=grid, in_specs=list(in_specs) + [ANY] * ci, out_specs=list(out_specs) + [ANY] * co,
        out_shape=list(out_shape) + comm.out_shapes, scratch_shapes=list(scratch_shapes) + comm.sems,
        input_output_aliases={n_in + a: n_out + b for a, b in comm.aliases.items()},
        compiler_params=pltpu.CompilerParams(dimension_semantics=("arbitrary",) * len(grid), vmem_limit_bytes=VMEM_LIMIT),
    )(*args, *comm.ins)
    return list(outs[:n_out]), list(outs[n_out:])


def _run_comm(comm, name):
    ci, co = len(comm.ins), len(comm.out_shapes)

    def body(*refs):
        cins, couts, csems = refs[:ci], refs[ci:ci + co], refs[ci + co:]
        comm.start(cins, couts, csems)
        comm.wait(cins, couts, csems)

    outs = pl.pallas_call(body, name=name, in_specs=[ANY] * ci, out_specs=[ANY] * co, out_shape=comm.out_shapes,
                          scratch_shapes=comm.sems, input_output_aliases=comm.aliases)(*comm.ins)
    return list(outs)


def _ag_comm(shard, land=None, q=0, nq=1):
    two, R2, Cc = shard.shape
    rows = pl.ds(q * (R2 // nq), R2 // nq)
    DMA = pltpu.SemaphoreType.DMA

    def copies(ins, outs, sems, which):
        sh, out = ins[0], outs[0]
        send1, recv1, send2, recv2, send0, recv0 = sems
        x, y, c, s = _place()
        sib = (x, y, 1 - c)
        rc = pltpu.make_async_remote_copy
        if which == "first":
            return [rc(sh.at[c, rows], out.at[s, c, rows], send1.at[m - 1], recv1.at[m - 1],
                       device_id=_chip_dev(s ^ m, c), device_id_type=MESH) for m in range(1, 4)]
        if which == "own":
            return [rc(sh.at[h, rows], out.at[s, h, rows], send0.at[h], recv0.at[h], device_id=sib, device_id_type=MESH)
                    for h in range(2)]
        if which == "landed":
            return [rc(sh.at[c, rows], out.at[s ^ m, c, rows], send1.at[m - 1], recv1.at[m - 1], device_id=sib,
                       device_id_type=MESH) for m in range(1, 4)]
        half = c if which == "passed" else 1 - c
        return [rc(out.at[s ^ m, half, rows], out.at[s ^ m, half, rows], send2.at[m - 1], recv2.at[m - 1], device_id=sib,
                   device_id_type=MESH) for m in range(1, 4)]

    def start(ins, outs, sems):
        for cp in copies(ins, outs, sems, "first") + copies(ins, outs, sems, "own"):
            cp.start()

    def wait(ins, outs, sems):
        passed = copies(ins, outs, sems, "passed")
        for lan, pas in zip(copies(ins, outs, sems, "landed"), passed):
            lan.wait_recv()
            pas.start()
        for cp in copies(ins, outs, sems, "handed"):
            cp.wait_recv()
        for cp in copies(ins, outs, sems, "own"):
            cp.wait()
        for cp in copies(ins, outs, sems, "first") + passed:
            cp.wait_send()

    return _Comm([shard] + ([land] if land is not None else []), [jax.ShapeDtypeStruct((4, two, R2, Cc), shard.dtype)],
                 [DMA((3,)), DMA((3,)), DMA((3,)), DMA((3,)), DMA((2,)), DMA((2,))], start, wait,
                 {1: 0} if land is not None else None)


def _a2a_comm(S1, q=0, nq=1, land=None, cnt=1):
    S4, R2, Cc = S1.shape
    rows = pl.ds(q * (R2 // nq), cnt * (R2 // nq))
    DMA = pltpu.SemaphoreType.DMA

    def copies(ins, outs, sems):
        x, y, c, s = _place()
        return [pltpu.make_async_remote_copy(ins[0].at[s ^ m, rows], outs[0].at[m - 1, rows], sems[0].at[m - 1],
                                             sems[1].at[m - 1], device_id=_chip_dev(s ^ m, c), device_id_type=MESH)
                for m in range(1, 4)]

    def start(ins, outs, sems):
        for cp in copies(ins, outs, sems):
            cp.start()

    def wait(ins, outs, sems):
        for cp in copies(ins, outs, sems):
            cp.wait()

    return _Comm([S1] + ([land] if land is not None else []), [jax.ShapeDtypeStruct((3, R2, Cc), S1.dtype)],
                 [DMA((3,)), DMA((3,))], start, wait, {1: 0} if land is not None else None)


def _halves(G):
    return G.reshape(G.shape[0], 2, G.shape[1] // 2, G.shape[2])


def _swap_comm(piece):
    n, two, R2, Cc = piece.shape
    DMA = pltpu.SemaphoreType.DMA

    def copies(ins, outs, sems):
        x, y, c, s = _place()
        return [pltpu.make_async_remote_copy(ins[0].at[t, 1 - c], outs[0].at[t], sems[0].at[t], sems[1].at[t],
                                             device_id=(x, y, 1 - c), device_id_type=MESH) for t in range(n)]

    def start(ins, outs, sems):
        for cp in copies(ins, outs, sems):
            cp.start()

    def wait(ins, outs, sems):
        for cp in copies(ins, outs, sems):
            cp.wait()

    return _Comm([piece], [jax.ShapeDtypeStruct((n, R2, Cc), piece.dtype)], [DMA((n,)), DMA((n,))], start, wait)


def _add_half(pieces, As, cidx, name):
    R2, Cc = pieces[0].shape[2:]
    S4 = sum(pc.shape[0] for pc in pieces)
    tb, cb = _wide(R2, Cc, 3, 2 * HALO)
    nI, nJ = R2 // tb, Cc // cb

    def body(c_ref, g_ref, a_ref, *rest):
        rest[-1][...] = (g_ref[0, 0] + a_ref[0]).astype(BF16)

    out, t0 = None, 0
    for k, (pc, A) in enumerate(zip(pieces, As)):
        grid_spec = pltpu.PrefetchScalarGridSpec(
            num_scalar_prefetch=1, grid=(pc.shape[0], nI, nJ),
            in_specs=[pl.BlockSpec((1, 1, tb, cb), lambda t, i, j, c_ref: (t, c_ref[0], i, j)),
                      pl.BlockSpec((1, tb, cb), lambda t, i, j, c_ref: (t, i, j))] + ([ANY] if k else []),
            out_specs=pl.BlockSpec((tb, cb), lambda t, i, j, c_ref, t0=t0: ((t0 + t) * nI + i, j)))
        out = pl.pallas_call(
            functools.partial(body), name=f"{name}{k}", grid_spec=grid_spec, out_shape=jax.ShapeDtypeStruct((S4 * R2, Cc), BF16),
            input_output_aliases={3: 0} if k else {},
            compiler_params=pltpu.CompilerParams(dimension_semantics=("parallel", "parallel", "parallel"),
                                                 vmem_limit_bytes=VMEM_LIMIT),
        )(*((cidx, pc, A) + ((out,) if k else ())))
        t0 += pc.shape[0]
    return out.reshape(S4, R2, Cc)


def _add_own(S1, B, chip_idx, cidx, name):
    S4, R2, Cc = S1.shape
    tb, cb = _wide(R2, Cc, 3, 2 * HALO)

    def body(s_idx, c_idx, s_ref, b_ref, o_ref):
        o_ref[...] = ((s_ref[0].astype(F32) + b_ref[0].astype(F32)) + b_ref[1].astype(F32)) + b_ref[2].astype(F32)

    grid_spec = pltpu.PrefetchScalarGridSpec(
        num_scalar_prefetch=2, grid=(R2 // tb, Cc // cb),
        in_specs=[pl.BlockSpec((1, tb, cb), lambda i, j, s_idx, c_idx: (s_idx[0], i, j)),
                  pl.BlockSpec((3, tb, cb), lambda i, j, s_idx, c_idx: (0, i, j))],
        out_specs=pl.BlockSpec((None, tb, cb), lambda i, j, s_idx, c_idx: (c_idx[0], i, j)))
    return pl.pallas_call(body, name=name, grid_spec=grid_spec, out_shape=jax.ShapeDtypeStruct((2, R2, Cc), F32),
                          compiler_params=pltpu.CompilerParams(dimension_semantics=("parallel", "parallel"),
                                                               vmem_limit_bytes=VMEM_LIMIT))(chip_idx, cidx, S1, B)


def _sibling_fill(Hs, name):
    def body(h_ref, out_ref, send, recv):
        x, y, c, s = _place()
        cp = pltpu.make_async_remote_copy(h_ref.at[c], out_ref.at[c], send, recv, device_id=(x, y, 1 - c), device_id_type=MESH)
        cp.start()
        cp.wait()

    return pl.pallas_call(
        body, name=name, in_specs=[ANY], out_specs=ANY, out_shape=jax.ShapeDtypeStruct(Hs.shape, Hs.dtype),
        input_output_aliases={0: 0}, scratch_shapes=[pltpu.SemaphoreType.DMA, pltpu.SemaphoreType.DMA],
    )(Hs)


def _gather_all(buf, name):
    R, Cc = buf.shape

    def body(b_ref, out_ref, send, recv, local):
        x, y, c, s = _place()
        d = 2 * s + c
        mine = pltpu.make_async_copy(b_ref, out_ref.at[d], local)
        mine.start()
        cps = []
        for m in range(1, 8):
            t = d ^ m
            cp = pltpu.make_async_remote_copy(b_ref, out_ref.at[d], send.at[m - 1], recv.at[m - 1],
                                              device_id=(t // 4, (t // 2) % 2, t % 2), device_id_type=MESH)
            cp.start()
            cps.append(cp)
        for cp in cps:
            cp.wait()
        mine.wait()

    return pl.pallas_call(
        body, name=name, in_specs=[ANY], out_specs=ANY, out_shape=jax.ShapeDtypeStruct((8, R, Cc), buf.dtype),
        scratch_shapes=[pltpu.SemaphoreType.DMA((7,)), pltpu.SemaphoreType.DMA((7,)), pltpu.SemaphoreType.DMA],
    )(buf)


def _finish_shard(S1, B, cidx, chip_idx, name):
    Hs = _sibling_fill(_add_own(S1, B, chip_idx, cidx, name + "_sum"), name + "_gather")
    return Hs.reshape(2 * Hs.shape[1], Hs.shape[2])


def _pack_rows(vs):
    flat = jnp.concatenate([v.reshape(-1) for v in vs])
    n = flat.shape[0]
    rows = -(-n // (LANES * 2 * HALO)) * 2 * HALO
    return jnp.pad(flat, (0, rows * LANES - n)).reshape(rows, LANES)


def _unpack_rows(buf, shapes):
    flat = buf.reshape(-1)
    outs, o = [], 0
    for shp in shapes:
        n = 1
        for d in shp:
            n *= d
        outs.append(flat[o:o + n].reshape(shp))
        o += n
    return outs


def kernel(x, p, norm_mix_g, w_in, conv_a_w, conv_qkv_w, a_log, dt_bias, dn_norm_g, w_out, norm_ffn_g, w_up, conv_ffn_w, w_down, norm_ple_g, w_ple_gate, w_ple_proj, final_norm_g, loss_target, m_norm_mix_g, m_w_in, m_conv_a_w, m_conv_qkv_w, m_a_log, m_dt_bias, m_dn_norm_g, m_w_out, m_norm_ffn_g, m_w_up, m_conv_ffn_w, m_w_down, m_norm_ple_g, m_w_ple_gate, m_w_ple_proj, m_final_norm_g, v_norm_mix_g, v_w_in, v_conv_a_w, v_conv_qkv_w, v_a_log, v_dt_bias, v_dn_norm_g, v_w_out, v_norm_ffn_g, v_w_up, v_conv_ffn_w, v_w_down, v_norm_ple_g, v_w_ple_gate, v_w_ple_proj, v_final_norm_g):
    xs = x[0]
    ps = p[0, 0]
    tgt = loss_target[0]
    T, D = xs.shape
    H = a_log.shape[-1]
    DNW = H * HEAD_DIM
    CW = conv_a_w.shape[-1] * 4
    F = w_down.shape[1] * 4
    PD = ps.shape[-1]
    IN_MAIN = 3 * CW + 4 * DNW
    IN_COLS = IN_MAIN + 2 * H
    assert w_in.shape[-1] * 4 == IN_COLS and CW + DNW == D and 2 * H <= LANES
    cb = _tile(min(CW, DNW), 512, LANES)
    while F % cb:
        cb -= LANES
    cidx = lax.axis_index("c").astype(jnp.int32).reshape(1)
    chip = 2 * lax.axis_index("x") + lax.axis_index("y")

    def halves(w):
        sh = w[0].astype(BF16)
        return sh.reshape(2, sh.shape[0] // 2, sh.shape[1])

    def whole(land):
        return land.reshape(4, 2 * land.shape[2], land.shape[3])

    def rows(g4):
        return g4.reshape(4 * g4.shape[1], g4.shape[2])

    conv_shapes = [conv_a_w[0].shape, conv_qkv_w[0].shape, conv_ffn_w[0].shape]
    cpack = _pack_rows([conv_a_w[0], conv_qkv_w[0], conv_ffn_w[0]])
    sh_in, sh_out, sh_up, sh_down, sh_pg, sh_pp = (halves(w) for w in (w_in, w_out, w_up, w_down, w_ple_gate, w_ple_proj))
    l_in, cg = _run_comm(_merge(_ag_comm(sh_in), _ag_comm(cpack.reshape(2, cpack.shape[0] // 2, LANES))), "ag_w_in_conv")
    w_in_main, w_in_small = _join_shards(whole(l_in), IN_MAIN)
    cg = cg.reshape(4, cpack.shape[0], LANES)
    parts = [_unpack_rows(cg[t], conv_shapes) for t in range(4)]
    cw_a = jnp.concatenate([parts[t][0] for t in range(4)], axis=1)
    cw_qkv = jnp.concatenate([parts[t][1] for t in range(4)], axis=1)
    cw_ffn = jnp.concatenate([parts[t][2] for t in range(4)], axis=1)
    cw_q, cw_k, cw_v = cw_qkv[:, :DNW], cw_qkv[:, DNW:2 * DNW], cw_qkv[:, 2 * DNW:]
    cw_fg, cw_fv = cw_ffn[:, :F], cw_ffn[:, F:]
    pad_row = lambda v: jnp.pad(v, ((0, 0), (0, LANES - v.shape[1])))
    a_log_row, dt_row = pad_row(a_log), pad_row(dt_bias)
    gdn_t = jnp.tile(dn_norm_g, (1, H))
    gfin = final_norm_g.reshape(1, D)

    h1 = _rms_fwd(xs, norm_mix_g, "rms1")
    proj, (l_up,) = _mm(h1, w_in_main, mode="nn", out_dtypes=[F32], name="mm_proj", comm=_ag_comm(sh_up, q=0, nq=2))
    small = _mm(h1, w_in_small, mode="nn", out_dtypes=[F32], name="mm_small")
    ya = _ga_fwd(proj, cw_a, CW, cb)
    nq = 3 * CW // cb
    nd = DNW // cb
    qn, (l_out,) = _qkv_fwd(proj, cw_q, nq, True, DNW, cb, "q_fwd", comm=_ag_comm(sh_out, q=0, nq=2))
    kn, (l_out,) = _qkv_fwd(proj, cw_k, nq + nd, True, DNW, cb, "k_fwd", comm=_ag_comm(sh_out, l_out, q=1, nq=2))
    vs = _qkv_fwd(proj, cw_v, nq + 2 * nd, False, DNW, cb, "v_fwd")
    g, beta = _gb_fwd(small, a_log_row, dt_row, H)
    o, S0, inv_c, (l_up,) = _delta_fwd(qn, kn, vs, g, beta, comm=_ag_comm(sh_up, l_up, q=1, nq=2))
    w_out_f = rows(whole(l_out))
    w_out_a, w_out_b = w_out_f[:CW], w_out_f[CW:]
    w_up_4 = whole(l_up)
    z_coff = (3 * CW + 3 * DNW) // DNW
    assert (3 * CW + 3 * DNW) % DNW == 0 and CW % DNW == 0
    yb = _gnorm_fwd(o, proj, z_coff, gdn_t, DNW)
    add = lambda acc, r: (r + acc,)
    x1 = _mm(ya, w_out_a, mode="nn", out_dtypes=[F32], epi=add, extras=[xs], name="mm_out_a")
    x1 = _mm(yb, w_out_b, mode="nn", out_dtypes=[F32], epi=add, extras=[x1], name="mm_out_b")
    h2 = _rms_fwd(x1, norm_ffn_g, "rms2")
    up_g, (l_down,) = _mm(h2, w_up_4, mode="nn", b_split=(0, 2), out_dtypes=[F32], name="mm_up_g",
                          comm=_ag_comm(sh_down, q=0, nq=2))
    up_v, (l_down,) = _mm(h2, w_up_4, mode="nn", b_split=(2, 2), out_dtypes=[F32], name="mm_up_v",
                          comm=_ag_comm(sh_down, l_down, q=1, nq=2))
    w_down_f = rows(whole(l_down))
    act = _ffn_fwd(up_g, up_v, cw_fg, cw_fv, cb)
    x2, (l_pg, l_pp) = _mm(act, w_down_f, mode="nn", out_dtypes=[F32], epi=add, extras=[x1], name="mm_down",
                           comm=_merge(_ag_comm(sh_pg), _ag_comm(sh_pp)))
    w_pg_f = rows(whole(l_pg))
    w_pp_4 = whole(l_pp)
    h3 = _rms_fwd(x2, norm_ple_g, "rms3")
    pp = _mm(ps, w_pp_4, mode="nn", b_split=(0, 4), out_dtypes=[F32], name="mm_pp")

    def ple_epi(acc, x2v, ppv):
        pg = _sigmoid(acc)
        return x2v + pg * ppv, pg

    x3, pg = _mm(h3, w_pg_f, mode="nn", out_dtypes=[F32, F32], epi=ple_epi, extras=[x2, pp], name="mm_pg")

    dx3, dpg, dpp, fin = _final_fb(x3, tgt, gfin, pp, pg)
    loss = lax.psum(jnp.sum(fin[1]), ("x", "y", "c"))
    d_gfin = fin[0:1]
    def split_rows(dW):
        return dW.reshape(4, dW.shape[0] // 4, dW.shape[1])

    dW_pp = _mm(ps, dpp, mode="tn", out_split=4, out_dtypes=[F32], name="mm_dw_pp")
    dW_pg = _mm(h3, dpg, mode="tn", out_dtypes=[F32], name="mm_dw_pg")
    P_pp, P_pg = _halves(dW_pp), _halves(split_rows(dW_pg))
    dh3, (A_pp, A_pg) = _mm(dpg, w_pg_f, mode="nt", out_dtypes=[F32], name="mm_dh3",
                            comm=_merge(_swap_comm(P_pp), _swap_comm(P_pg)))
    S_pp = _add_half([P_pp], [A_pp], cidx, "rs_w_pp_add")
    S_pg = _add_half([P_pg], [A_pg], cidx, "rs_w_pg_add")
    dx2, dx2_b, d_gple = _rms_bwd(dh3, x2, norm_ple_g, dx3, "rms3_bwd")
    dW_down, (B_pp, B_pg) = _mm(act, dx2_b, mode="tn", out_dtypes=[F32], name="mm_dw_down",
                                comm=_merge(_a2a_comm(S_pp), _a2a_comm(S_pg)))
    P_down = _halves(split_rows(dW_down))
    dact, (A_down,) = _mm(dx2_b, w_down_f, mode="nt", out_dtypes=[F32], name="mm_dact", comm=_swap_comm(P_down))
    S_down = _add_half([P_down], [A_down], cidx, "rs_w_down_add")
    dup_g, dup_v, dcw_fg, dcw_fv = _ffn_bwd(dact, up_g, up_v, cw_fg, cw_fv, cb)
    dW_up_g, (B_down,) = _mm(h2, dup_g, mode="tn", out_split=2, out_dtypes=[F32], name="mm_dw_up_g", comm=_a2a_comm(S_down))
    P_ug = _halves(dW_up_g)
    dW_up_v, (A_ug,) = _mm(h2, dup_v, mode="tn", out_split=2, out_dtypes=[F32], name="mm_dw_up_v", comm=_swap_comm(P_ug))
    P_uv = _halves(dW_up_v)
    dh2, (A_uv,) = _mm(dup_g, w_up_4, mode="nt", b_split=(0, 2), out_dtypes=[F32], name="mm_dh2_g", comm=_swap_comm(P_uv))
    S_up = _add_half([P_ug, P_uv], [A_ug, A_uv], cidx, "rs_w_up_add")
    dh2, (B_up,) = _mm(dup_v, w_up_4, mode="nt", b_split=(2, 2), out_dtypes=[F32], epi=add, extras=[dh2], name="mm_dh2_v",
                       comm=_a2a_comm(S_up, 0, 2))
    dx1, dx1_b, d_gffn = _rms_bwd(dh2, x1, norm_ffn_g, dx2, "rms2_bwd")
    dW_out_a = _mm(ya, dx1_b, mode="tn", out_dtypes=[F32], name="mm_dw_out_a")
    dW_out_b = _mm(yb, dx1_b, mode="tn", out_dtypes=[F32], name="mm_dw_out_b")
    P_oa, P_ob = _halves(dW_out_a.reshape(-1, D // 4, D)), _halves(dW_out_b.reshape(-1, D // 4, D))
    dymix, (A_oa, A_ob) = _mm(dx1_b, w_out_f, mode="nt", out_dtypes=[F32], name="mm_dymix",
                              comm=_merge(_swap_comm(P_oa), _swap_comm(P_ob)))
    S_out = _add_half([P_oa, P_ob], [A_oa, A_ob], cidx, "rs_w_out_add")
    dax, dab, dac, dcw_a = _ga_bwd(dymix, proj, cw_a, CW, cb)
    do, dz, d_gdn = _gnorm_bwd(dymix, CW // DNW, o, proj, z_coff, gdn_t, DNW)
    dqn, dkn, dvs, dgB, dbB, (B_up, B_out) = _delta_bwd(qn, kn, vs, g, beta, S0, inv_c, do,
                                                        comm=_merge(_a2a_comm(S_up, 1, 2, B_up), _a2a_comm(S_out)))
    dq_pre, dcw_q = _qkv_bwd(dqn, proj, cw_q, nq, True, DNW, cb, "q_bwd")
    dk_pre, dcw_k = _qkv_bwd(dkn, proj, cw_k, nq + nd, True, DNW, cb, "k_bwd")
    dv_pre, dcw_v = _qkv_bwd(dvs, proj, cw_v, nq + 2 * nd, False, DNW, cb, "v_bwd")
    dsmall, d_ab = _gb_bwd(dgB, dbB, small, g, beta, a_log_row, dt_row, H)
    dproj = jnp.concatenate([dax, dab, dac, dq_pre, dk_pre, dv_pre, dz], axis=1)
    dW_in_main = _mm(h1, dproj, mode="tn", out_dtypes=[F32], name="mm_dw_in")
    dW_in_small = _mm(h1, dsmall, mode="tn", out_dtypes=[F32], name="mm_dw_in_small")
    chip_idx = chip.astype(jnp.int32).reshape(1)

    def update(S1, B, w, m, v, name, comm=None):
        gr = _finish_shard(S1, B, cidx, chip_idx, "rs_" + name)
        if gr.shape[1] % LANES == 0:
            res = _adamw(w[0], gr, m[0], v[0], "adamw_" + name, comm=comm)
            (delta, m2, v2), extra = (res, None) if comm is None else res
            out = (gr[None], delta[None], m2[None], v2[None])
            return out if comm is None else (out, extra)
        tr = jnp.transpose
        grt = tr(gr)
        delta, m2, v2 = _adamw(tr(w[0]), grt, tr(m[0]), tr(v[0]), "adamw_" + name)
        return tr(grt)[None], tr(delta)[None], tr(m2)[None], tr(v2)[None]

    P_in = _halves(_split_shards(dW_in_main, dW_in_small, IN_COLS // 4))
    up_res, (A_in,) = update(S_up, B_up, w_up, m_w_up, v_w_up, "w_up", comm=_swap_comm(P_in))
    S_in = _add_half([P_in], [A_in], cidx, "rs_w_in_add")
    dh1, (B_in,) = _mm(dproj, w_in_main, mode="nt", out_dtypes=[F32], name="mm_dh1", comm=_a2a_comm(S_in, 0, 4, cnt=3))
    dh1 = _mm(dsmall, w_in_small, mode="nt", out_dtypes=[F32], epi=add, extras=[dh1], name="mm_dh1_small")
    (dx, _, d_gmix), (B_in,) = _rms_bwd(dh1, xs, norm_mix_g, dx1, "rms1_bwd", comm=_a2a_comm(S_in, 3, 4, B_in))

    big = {
        "w_in": update(S_in, B_in, w_in, m_w_in, v_w_in, "w_in"),
        "w_out": update(S_out, B_out, w_out, m_w_out, v_w_out, "w_out"),
        "w_up": up_res,
        "w_down": update(S_down, B_down, w_down, m_w_down, v_w_down, "w_down"),
        "w_ple_gate": update(S_pg, B_pg, w_ple_gate, m_w_ple_gate, v_w_ple_gate, "w_pg"),
        "w_ple_proj": update(S_pp, B_pp, w_ple_proj, m_w_ple_proj, v_w_ple_proj, "w_pp"),
    }

    small_grads = [d_gmix[0:1], dcw_a[:cw_a.shape[0]], jnp.concatenate([dcw_q, dcw_k, dcw_v], axis=1)[:cw_qkv.shape[0]],
                   d_ab[0:1, :H], d_ab[1:2, :H], d_gdn[0:1], d_gffn[0:1],
                   jnp.concatenate([dcw_fg, dcw_fv], axis=1)[:cw_ffn.shape[0]], d_gple[0:1], d_gfin]
    small_shapes = [v.shape for v in small_grads]
    gpack = _pack_rows(small_grads)
    gsum = _sum_stack(_gather_all(gpack, "ag_small"), "sum_small")
    (g_gmix, g_cwa, g_cwqkv, g_alog, g_dt, g_gdn, g_gffn, g_cwffn, g_gple, g_gfin) = _unpack_rows(gsum, small_shapes)

    def my_cols(v):
        Cc = v.shape[1] // 4
        return lax.dynamic_slice_in_dim(v, chip * Cc, Cc, axis=1)

    g_small = [g_gmix, my_cols(g_cwa), my_cols(g_cwqkv), g_alog, g_dt, g_gdn, g_gffn, my_cols(g_cwffn), g_gple, g_gfin]
    w_small = [norm_mix_g, conv_a_w[0], conv_qkv_w[0], a_log, dt_bias, dn_norm_g, norm_ffn_g, conv_ffn_w[0], norm_ple_g, gfin]
    m_small = [m_norm_mix_g, m_conv_a_w[0], m_conv_qkv_w[0], m_a_log, m_dt_bias, m_dn_norm_g, m_norm_ffn_g, m_conv_ffn_w[0],
               m_norm_ple_g, m_final_norm_g.reshape(1, D)]
    v_small = [v_norm_mix_g, v_conv_a_w[0], v_conv_qkv_w[0], v_a_log, v_dt_bias, v_dn_norm_g, v_norm_ffn_g, v_conv_ffn_w[0],
               v_norm_ple_g, v_final_norm_g.reshape(1, D)]
    shp = [v.shape for v in w_small]
    ds_, ms_, vs_ = _adamw(_pack_rows(w_small), _pack_rows(g_small), _pack_rows(m_small), _pack_rows(v_small), "adamw_small")
    out_shapes = [norm_mix_g.shape, conv_a_w.shape, conv_qkv_w.shape, a_log.shape, dt_bias.shape, dn_norm_g.shape,
                  norm_ffn_g.shape, conv_ffn_w.shape, norm_ple_g.shape, final_norm_g.shape]
    rs = lambda vals: [v.reshape(s) for v, s in zip(vals, out_shapes)]
    sg, sd_, sm_, sv_ = rs(g_small), rs(_unpack_rows(ds_, shp)), rs(_unpack_rows(ms_, shp)), rs(_unpack_rows(vs_, shp))
    names_small = ["norm_mix_g", "conv_a_w", "conv_qkv_w", "a_log", "dt_bias", "dn_norm_g", "norm_ffn_g", "conv_ffn_w",
                   "norm_ple_g", "final_norm_g"]
    res = {n: (sg[i], sd_[i], sm_[i], sv_[i]) for i, n in enumerate(names_small)}
    res.update(big)
    order = ["norm_mix_g", "w_in", "conv_a_w", "conv_qkv_w", "a_log", "dt_bias", "dn_norm_g", "w_out", "norm_ffn_g", "w_up",
             "conv_ffn_w", "w_down", "norm_ple_g", "w_ple_gate", "w_ple_proj", "final_norm_g"]
    return (loss, dx[None], *[res[n][0] for n in order], *[res[n][1] for n in order], *[res[n][2] for n in order],
            *[res[n][3] for n in order])
```

```python
import functools

import jax
import jax.numpy as jnp
from jax import lax
from jax.experimental import pallas as pl
from jax.experimental.pallas import tpu as pltpu

F32 = jnp.float32
BF16 = jnp.bfloat16
LANES = 128
HALO = 8
HEAD_DIM = 128
CHUNK = 64
EPS = 1e-6
VMEM_LIMIT = 56 * 1024 * 1024
MM_VMEM_BUDGET = 40 * 1024 * 1024
MM_STEP_BYTES = 1 << 20
EW_VMEM_BUDGET = 28 * 1024 * 1024
MESH = pl.DeviceIdType.MESH

ADAM_LR, ADAM_B1, ADAM_B2, ADAM_EPS, ADAM_WD, ADAM_STEP = 0.001, 0.9, 0.999, 1e-08, 0.01, 10


def _tile(n, cap, unit):
    if n <= cap:
        return n
    d = (cap // unit) * unit
    while d >= unit:
        if n % d == 0:
            return d
        d -= unit
    raise ValueError(f"no tile for {n} (cap {cap}, unit {unit})")


def _sigmoid(x):
    return 1.0 / (1.0 + jnp.exp(-x))


def _divisors(n, cap):
    ds = [d for d in range(cap // LANES * LANES, 0, -LANES) if n % d == 0]
    return [n] if (n <= cap or not ds) else ds


def _mm_tiles(M, N, K, n_unit, k_unit, a_bytes, n_blocks_mn, a_transposed):
    best = None
    for tm in _divisors(M, 1536):
        for tn in _divisors(n_unit, 1536):
            for tk in _divisors(k_unit, 4096):
                nk = K // tk
                vmem = 2 * tm * tk * a_bytes + 2 * tk * tn * 2 + 2 * 4 * tm * tn * n_blocks_mn + (4 * tm * tn if nk > 1 else 0)
                if vmem > MM_VMEM_BUDGET:
                    continue
                steps = (M // tm) * (N // tn) * nk
                cost = (M * K * a_bytes * (N // tn if nk > 1 else 1) + K * N * 2 * (M // tm) + 4 * M * N * n_blocks_mn
                        + (8 * M * N * nk // 3 if nk > 1 else 0) + steps * MM_STEP_BYTES
                        + (2 * steps * tm * tk if a_transposed else 0))
                if best is None or cost < best[0]:
                    best = (cost, tm, tn, tk)
    return best[1:]


def _mm(a, b, *, mode, out_dtypes, name, epi=None, extras=(), comm=None, b_split=None, out_split=None):
    if b_split is not None:
        lo, ns = b_split
        Rb, Cb = b.shape[1], b.shape[2]
    if mode == "nn":
        (M, K), N = a.shape, (ns * Cb if b_split else b.shape[1])
    elif mode == "nt":
        (M, K), N = a.shape, (Rb if b_split else b.shape[0])
    else:
        (K, M), N = a.shape, b.shape[1]
    n_ex, n_out = len(extras), len(out_dtypes)
    n_unit = Cb if (b_split and mode == "nn") else (N // out_split if out_split else N)
    k_unit = Cb if (b_split and mode == "nt") else K
    tm, tn, tk = _mm_tiles(M, N, K, n_unit, k_unit, a.dtype.itemsize, n_ex + n_out, mode == "tn")
    nk = K // tk
    a_spec = pl.BlockSpec((tk, tm), lambda i, j, k: (k, i)) if mode == "tn" else pl.BlockSpec((tm, tk), lambda i, j, k: (i, k))
    if b_split and mode == "nn":
        nb = Cb // tn
        b_spec = pl.BlockSpec((None, tk, tn), lambda i, j, k: (lo + j // nb, k, j % nb))
    elif b_split:
        nb = Cb // tk
        b_spec = pl.BlockSpec((None, tn, tk), lambda i, j, k: (lo + k // nb, j, k % nb))
    else:
        b_spec = pl.BlockSpec((tn, tk), lambda i, j, k: (j, k)) if mode == "nt" else pl.BlockSpec((tk, tn), lambda i, j, k: (k, j))
    mn_spec = pl.BlockSpec((tm, tn), lambda i, j, k: (i, j))
    out_shapes = [jax.ShapeDtypeStruct((M, N), dt) for dt in out_dtypes]
    out_specs = [mn_spec] * n_out
    if out_split:
        assert n_ex == 0 and n_out == 1
        nbo = (N // out_split) // tn
        out_specs = [pl.BlockSpec((None, tm, tn), lambda i, j, k: (j // nbo, i, j % nbo))]
        out_shapes = [jax.ShapeDtypeStruct((out_split, M, N // out_split), out_dtypes[0])]
    dims = {"nn": (((1,), (0,)), ((), ())), "nt": (((1,), (1,)), ((), ())), "tn": (((0,), (0,)), ((), ()))}[mode]

    def body(*refs):
        a_ref, b_ref = refs[0], refs[1]
        ex_refs = refs[2:2 + n_ex]
        out_refs = refs[2 + n_ex:2 + n_ex + n_out]
        part = lax.dot_general(a_ref[...].astype(BF16), b_ref[...].astype(BF16), dims, preferred_element_type=F32)

        def finish(acc):
            outs = (acc,) if epi is None else epi(acc, *[r[...] for r in ex_refs])
            for r, o in zip(out_refs, outs):
                r[...] = o.astype(r.dtype)

        if nk == 1:
            finish(part)
            return
        acc_ref = refs[-1]
        k = pl.program_id(2)

        @pl.when(k == 0)
        def _():
            acc_ref[...] = part

        @pl.when(jnp.logical_and(k > 0, k < nk - 1))
        def _():
            acc_ref[...] += part

        @pl.when(k == nk - 1)
        def _():
            finish(acc_ref[...] + part)

    outs, comm_outs = _call(
        body, name=name, grid=(M // tm, N // tn, nk),
        in_specs=[a_spec, b_spec] + [mn_spec] * n_ex,
        out_specs=out_specs,
        out_shape=out_shapes,
        scratch_shapes=[pltpu.VMEM((tm, tn), F32)] if nk > 1 else [],
        semantics=("parallel", "parallel", "arbitrary"), args=(a, b, *extras), comm=comm)
    res = outs[0] if n_out == 1 else outs
    return res if comm is None else (res, comm_outs)


def _tiled(fn, *, T, C, ins, out_dtypes=(), acc_rows=(), tb=None, cb=512, name, comm=None):
    tb = _tile(T, tb or (512 if cb <= 1024 else 256), HALO)
    nI, nJ = T // tb, C // cb
    hb, nH = tb // HALO, T // HALO
    specs, args, kinds = [], [], []
    for kind, arr, cmap in ins:
        cm = cmap if cmap is not None else (lambda j: j)
        kinds.append(kind)
        if kind == "cur":
            specs.append(pl.BlockSpec((tb, cb), lambda j, i, cm=cm: (i, cm(j))))
            args.append(arr)
        elif kind == "ext":
            specs.append(pl.BlockSpec((HALO, cb), lambda j, i, cm=cm: (jnp.maximum(i * hb - 1, 0), cm(j))))
            specs.append(pl.BlockSpec((tb, cb), lambda j, i, cm=cm: (i, cm(j))))
            specs.append(pl.BlockSpec((HALO, cb), lambda j, i, cm=cm: (jnp.minimum((i + 1) * hb, nH - 1), cm(j))))
            args += [arr, arr, arr]
        elif kind == "row":
            specs.append(pl.BlockSpec((arr.shape[0], cb), lambda j, i, cm=cm: (0, cm(j))))
            args.append(arr)
        elif kind == "stack":
            specs.append(pl.BlockSpec((arr.shape[0], tb, cb), lambda j, i, cm=cm: (0, i, cm(j))))
            args.append(arr)
        else:
            raise ValueError(kind)
    n_in = len(args)
    n_out, n_acc = len(out_dtypes), len(acc_rows)

    def body(*refs):
        j, i = pl.program_id(0), pl.program_id(1)
        vals, r = [], 0
        for kind in kinds:
            if kind == "ext":
                prev = jnp.where(i == 0, 0.0, refs[r][...].astype(F32))
                cur = refs[r + 1][...].astype(F32)
                nxt = jnp.where(i == nI - 1, 0.0, refs[r + 2][...].astype(F32))
                vals.append(jnp.concatenate([prev, cur, nxt], axis=0))
                r += 3
            else:
                vals.append(refs[r][...])
                r += 1
        res = fn(j, i, *vals)
        for ref, o in zip(refs[n_in:n_in + n_out], res[:n_out]):
            ref[...] = o.astype(ref.dtype)
        for ref, o in zip(refs[n_in + n_out:], res[n_out:]):
            @pl.when(i == 0)
            def _(ref=ref, o=o):
                ref[...] = o

            @pl.when(i > 0)
            def _(ref=ref, o=o):
                ref[...] += o

    outs, comm_outs = _call(
        body, name=name, grid=(nJ, nI), in_specs=specs,
        out_specs=[pl.BlockSpec((tb, cb), lambda j, i: (i, j))] * n_out
        + [pl.BlockSpec((rows, cb), lambda j, i: (0, j)) for rows in acc_rows],
        out_shape=[jax.ShapeDtypeStruct((T, C), dt) for dt in out_dtypes]
        + [jax.ShapeDtypeStruct((rows, C), F32) for rows in acc_rows],
        scratch_shapes=[], semantics=("parallel", "arbitrary"), args=args, comm=comm)
    return outs if comm is None else (outs, comm_outs)


def _conv_causal(xe, w):
    K = w.shape[0]
    y = xe * w[K - 1:K]
    for j in range(K - 1):
        y = y + pltpu.roll(xe, K - 1 - j, 0) * w[j:j + 1]
    return y


def _conv_anti(de, w):
    K, n = w.shape[0], de.shape[0]
    y = de * w[K - 1:K]
    for j in range(K - 1):
        y = y + pltpu.roll(de, n - (K - 1 - j), 0) * w[j:j + 1]
    return y


def _conv_dw(dce, xe, K):
    n = dce.shape[0]
    tb = n - 2 * HALO
    rows = []
    for j in range(K):
        xs = xe if j == K - 1 else pltpu.roll(xe, K - 1 - j, 0)
        rows.append(jnp.sum((dce * xs)[HALO:HALO + tb], axis=0, keepdims=True))
    rows.append(jnp.zeros((HALO - K, dce.shape[1]), F32))
    return jnp.concatenate(rows, axis=0)


def _own(xe):
    return xe[HALO:xe.shape[0] - HALO]


def _row0(v):
    return jnp.concatenate([v, jnp.zeros((HALO - 1, v.shape[1]), F32)], axis=0)


def _per_head(fn, *xs):
    n = xs[0].shape[1] // HEAD_DIM
    outs = [fn(*[x[:, g * HEAD_DIM:(g + 1) * HEAD_DIM] for x in xs]) for g in range(n)]
    return outs[0] if n == 1 else jnp.concatenate(outs, axis=1)


def _rms_fwd(x, g, name):
    T, D = x.shape

    def fn(j, i, xv, gv):
        r = lax.rsqrt(jnp.mean(xv * xv, axis=1, keepdims=True) + EPS)
        return (xv * r * gv,)

    return _tiled(fn, T=T, C=D, ins=[("cur", x, None), ("row", g, None)], out_dtypes=[BF16], cb=D, name=name)[0]


def _rms_bwd_math(dy, xv, gv):
    r = lax.rsqrt(jnp.mean(xv * xv, axis=1, keepdims=True) + EPS)
    xh = xv * r
    dxh = dy * gv
    dx = r * (dxh - xh * jnp.mean(dxh * xh, axis=1, keepdims=True))
    dg = jnp.sum(dy * xh, axis=0, keepdims=True)
    return dx, dg


def _rms_bwd(dh, x, g, dres, name, comm=None):
    T, D = x.shape

    def fn(j, i, dhv, xv, gv, dr):
        dx, dg = _rms_bwd_math(dhv, xv, gv)
        return dr + dx, dr + dx, _row0(dg)

    return _tiled(fn, T=T, C=D, ins=[("cur", dh, None), ("cur", x, None), ("row", g, None), ("cur", dres, None)],
                  out_dtypes=[F32, BF16], acc_rows=[HALO], cb=D, name=name, comm=comm)


def _final_fb(x3, tgt, g, pp, pg):
    T, D = x3.shape

    def fn(j, i, xv, tv, gv, ppv, pgv):
        r = lax.rsqrt(jnp.mean(xv * xv, axis=1, keepdims=True) + EPS)
        xh = xv * r
        e = xh * gv - tv
        dy = e * (1.0 / D)
        dxh = dy * gv
        dx = r * (dxh - xh * jnp.mean(dxh * xh, axis=1, keepdims=True))
        dg = jnp.sum(dy * xh, axis=0, keepdims=True)
        ls = jnp.sum(e * e, axis=0, keepdims=True) * (0.5 / D)
        return (dx, dx * ppv * pgv * (1.0 - pgv), dx * pgv,
                jnp.concatenate([dg, ls, jnp.zeros((HALO - 2, D), F32)], axis=0))

    return _tiled(fn, T=T, C=D, ins=[("cur", x3, None), ("cur", tgt, None), ("row", g, None), ("cur", pp, None),
                                      ("cur", pg, None)], out_dtypes=[F32, BF16, BF16], acc_rows=[HALO], cb=D, name="final_fb")


def _ga_fwd(proj, w_a, CW, cb):
    T = proj.shape[0]
    n = CW // cb

    def fn(j, i, ax, ab, ac, w):
        c = _conv_causal(ac * ax, w)
        return (ab * _own(c),)

    return _tiled(fn, T=T, C=CW, ins=[("ext", proj, None), ("cur", proj, lambda j: j + n), ("ext", proj, lambda j: j + 2 * n),
                                       ("row", w_a, None)], out_dtypes=[BF16], cb=cb, name="ga_fwd")[0]


def _ga_bwd(dymix, proj, w_a, CW, cb):
    T = proj.shape[0]
    n = CW // cb
    K = w_a.shape[0]

    def fn(j, i, dy, ax, ab, ac, w):
        u = ac * ax
        c = _conv_causal(u, w)
        dc = dy * ab
        du = _conv_anti(dc, w)
        return _own(du * ac), _own(dy * c), _own(du * ax), _conv_dw(dc, u, K)

    return _tiled(fn, T=T, C=CW, ins=[("ext", dymix, None), ("ext", proj, None), ("ext", proj, lambda j: j + n),
                                       ("ext", proj, lambda j: j + 2 * n), ("row", w_a, None)],
                  out_dtypes=[BF16, BF16, BF16], acc_rows=[HALO], cb=cb, name="ga_bwd")


def _l2n(s):
    return s * lax.rsqrt(jnp.sum(s * s, axis=1, keepdims=True) + EPS)


def _qkv_fwd(proj, w_sec, coff, normalize, DNW, cb, name, comm=None):
    T = proj.shape[0]

    def fn(j, i, pre, w):
        c = _own(_conv_causal(pre, w))
        s = c * _sigmoid(c)
        return (_per_head(_l2n, s) if normalize else s,)

    res = _tiled(fn, T=T, C=DNW, ins=[("ext", proj, lambda j: j + coff), ("row", w_sec, None)],
                 out_dtypes=[F32], cb=cb, name=name, comm=comm)
    return res[0] if comm is None else (res[0][0], res[1])


def _qkv_bwd(dsec, proj, w_sec, coff, normalize, DNW, cb, name):
    T = proj.shape[0]
    K = w_sec.shape[0]

    def l2n_bwd(s, dn):
        r = lax.rsqrt(jnp.sum(s * s, axis=1, keepdims=True) + EPS)
        nrm = s * r
        return r * (dn - nrm * jnp.sum(dn * nrm, axis=1, keepdims=True))

    def fn(j, i, dn, pre, w):
        c = _conv_causal(pre, w)
        sg = _sigmoid(c)
        s = c * sg
        ds = _per_head(l2n_bwd, s, dn) if normalize else dn
        dc = ds * (sg * (1.0 + c * (1.0 - sg)))
        return _own(_conv_anti(dc, w)), _conv_dw(dc, pre, K)

    return _tiled(fn, T=T, C=DNW, ins=[("ext", dsec, None), ("ext", proj, lambda j: j + coff), ("row", w_sec, None)],
                  out_dtypes=[BF16], acc_rows=[HALO], cb=cb, name=name)


def _gb_fwd(small, a_log_row, dt_row, H):
    T = small.shape[0]

    def fn(j, i, sm, al, dt):
        z = sm + dt
        sp = jnp.maximum(z, 0.0) + jnp.log(1.0 + jnp.exp(-jnp.abs(z)))
        g = -jnp.exp(al) * sp
        beta = _sigmoid(pltpu.roll(sm, LANES - H, 1))
        return g, beta

    return _tiled(fn, T=T, C=LANES, ins=[("cur", small, None), ("row", a_log_row, None), ("row", dt_row, None)],
                  out_dtypes=[F32, F32], cb=LANES, name="gb_fwd")


def _gb_bwd(dgB, dbB, small, g, beta, a_log_row, dt_row, H):
    T = small.shape[0]

    def fn(j, i, dgv, dbv, sm, gv, bv, al, dt):
        lane = lax.broadcasted_iota(jnp.int32, sm.shape, 1)
        dg = jnp.zeros(sm.shape, F32)
        db = jnp.zeros(sm.shape, F32)
        for h in range(H):
            dg = jnp.where(lane == h, jnp.sum(dgv[h], axis=1, keepdims=True), dg)
            db = jnp.where(lane == h, jnp.sum(dbv[h], axis=1, keepdims=True), db)
        da = dg * (-jnp.exp(al)) * _sigmoid(sm + dt)
        dbb = db * bv * (1.0 - bv)
        dsm = jnp.where(lane < H, da, 0.0) + pltpu.roll(jnp.where(lane < H, dbb, 0.0), H, 1)
        d_alog = jnp.sum(jnp.where(lane < H, dg * gv, 0.0), axis=0, keepdims=True)
        d_dt = jnp.sum(jnp.where(lane < H, da, 0.0), axis=0, keepdims=True)
        return dsm, jnp.concatenate([d_alog, d_dt, jnp.zeros((HALO - 2, LANES), F32)], axis=0)

    return _tiled(fn, T=T, C=LANES, ins=[("stack", dgB, None), ("stack", dbB, None), ("cur", small, None), ("cur", g, None),
                                          ("cur", beta, None), ("row", a_log_row, None), ("row", dt_row, None)],
                  out_dtypes=[BF16], acc_rows=[HALO], cb=LANES, name="gb_bwd")


_DIMS = {"nn": (((1,), (0,)), ((), ())), "nt": (((1,), (1,)), ((), ())), "tn": (((0,), (0,)), ((), ()))}
_DOT_BWD = {"nn": (("nt", "gb"), ("tn", "ag")), "nt": (("nn", "gb"), ("tn", "ga")), "tn": (("nt", "bg"), ("nn", "ag"))}


def _split(a):
    hi = a.astype(BF16)
    return hi, (a - hi.astype(F32)).astype(BF16)


def _raw_dot(a, b, kind, passes):
    dg = lambda x, y: lax.dot_general(x, y, _DIMS[kind], preferred_element_type=F32)
    if passes == 1:
        return dg(a.astype(BF16), b.astype(BF16))
    ah, al = _split(a)
    bh, bl = _split(b)
    if kind == "tn":
        return dg(ah, bh) + (dg(ah, bl) + dg(al, bh))
    m = a.shape[0]
    top = dg(jnp.concatenate([ah, al], axis=0), bh)
    return top[:m] + (dg(ah, bl) + top[m:])


def _raw_dot_exact(a, b, kind, exact):
    dg = lambda x, y: lax.dot_general(x, y, _DIMS[kind], preferred_element_type=F32)
    if exact == "a":
        bh, bl = _split(b)
        return dg(a.astype(BF16), bh) + dg(a.astype(BF16), bl)
    ah, al = _split(a)
    return dg(ah, b.astype(BF16)) + dg(al, b.astype(BF16))


@functools.lru_cache(maxsize=None)
def _dotc(kind):
    @jax.custom_vjp
    def f(a, b):
        return _raw_dot_exact(a, b, kind, "a")

    def fwd(a, b):
        return _raw_dot_exact(a, b, kind, "a"), a

    def bwd(a, g):
        db = _raw_dot_exact(a, g, "tn", "a") if kind == "nn" else _raw_dot_exact(g, a, "tn", "b")
        return jnp.zeros_like(a), db

    f.defvjp(fwd, bwd)
    return f


@functools.lru_cache(maxsize=None)
def _dotf(kind, passes):
    @jax.custom_vjp
    def f(a, b):
        return _raw_dot(a, b, kind, passes)

    def fwd(a, b):
        return _raw_dot(a, b, kind, passes), (a, b)

    def bwd(res, g):
        ops = {"a": res[0], "b": res[1], "g": g}
        (ka, oa), (kb, ob) = _DOT_BWD[kind]
        return (_raw_dot(ops[oa[0]], ops[oa[1]], ka, passes), _raw_dot(ops[ob[0]], ops[ob[1]], kb, passes))

    f.defvjp(fwd, bwd)
    return f


@jax.custom_vjp
def _saved_inverse(L, inv):
    return inv


def _saved_inverse_fwd(L, inv):
    return inv, inv


def _saved_inverse_bwd(inv, g):
    d3nt, d3tn = _dotf("nt", 3), _dotf("tn", 3)
    return -d3nt(d3tn(inv, g), inv), jnp.zeros_like(inv)


_saved_inverse.defvjp(_saved_inverse_fwd, _saved_inverse_bwd)


def _chunk_fn(q, k, v, gB, bB, S, inv_saved=None):
    C = CHUNK
    d3 = _dotf("nn", 3)
    d1, d1nt, d1tn = _dotf("nn", 1), _dotf("nt", 1), _dotf("tn", 1)
    each = lambda f, *ls: tuple(f(*xs) for xs in zip(*ls))
    row = lax.broadcasted_iota(jnp.int32, (C, C), 0)
    col = lax.broadcasted_iota(jnp.int32, (C, C), 1)
    causal = row >= col
    strict = row > col
    tril = jnp.where(causal, 1.0, 0.0).astype(F32)
    eye = jnp.where(row == col, 1.0, 0.0).astype(F32)
    avg = jnp.full((C, HEAD_DIM), 1.0 / HEAD_DIM, F32)
    gc = each(lambda g: _dotc("nn")(tril, g), gB)
    R = each(lambda g: _dotc("nt")(avg, g), gc)
    decay = each(lambda g, r: jnp.where(causal, jnp.exp(jnp.where(causal, g[:, :C] - r, 0.0)), 0.0), gc, R)
    kk = each(lambda x: d1nt(x, x), k)
    L = each(lambda a, d, b: jnp.where(strict, a * d * b[:, :C], 0.0), kk, decay, bB)
    if inv_saved is None:
        inv = each(lambda l: eye - l, L)
        P = L
        for _ in range(5):
            P = each(lambda p: d3(p, p), P)
            inv = each(lambda a, p: d3(a, eye + p), inv, P)
    else:
        inv = each(_saved_inverse, L, inv_saved)
    eg = each(jnp.exp, gc)
    u = each(lambda a, x, b: d3(a, x * b), inv, v, bB)
    w = each(lambda a, x, b, e: d3(a, x * b * e), inv, k, bB, eg)
    qs = each(lambda x: x * (HEAD_DIM ** -0.5), q)
    qk = each(lambda a, x, d: d1nt(a, x) * d, qs, k, decay)
    gl = each(lambda g: g[C - 1:C, :], gc)
    kd = each(lambda x, a, g: x * jnp.exp(a - g), k, gl, gc)
    qe = each(lambda a, e: a * e, qs, eg)
    nh = len(S)
    o = ()
    for c in range(len(q) // nh):
        sl = slice(c * nh, (c + 1) * nh)
        v_new = each(lambda a, b, s: a - d1(b, s), u[sl], w[sl], S)
        o1 = each(lambda a, s: d1(a, s), qe[sl], S)
        o += each(lambda a, b, vn: a + d1(b, vn), o1, qk[sl], v_new)
        kv = each(lambda x, vn: d1tn(x, vn), kd[sl], v_new)
        S = each(lambda s, a, b: s * jnp.exp(a) + b, S, gl[sl], kv)
    return (o, S), inv


def _sel_lane(x, h):
    lane = lax.broadcasted_iota(jnp.int32, x.shape, 1)
    return jnp.broadcast_to(jnp.sum(jnp.where(lane == h, x, 0.0), axis=1, keepdims=True), x.shape)


def _tile_of(ref, c, h):
    return ref[c * CHUNK:(c + 1) * CHUNK, h * HEAD_DIM:(h + 1) * HEAD_DIM]


def _chunks_per_step(N):
    return 4 if N % 4 == 0 else (2 if N % 2 == 0 else 1)


def _delta_fwd(q, k, v, g, beta, comm=None):
    T = q.shape[0]
    H, N = q.shape[1] // HEAD_DIM, T // CHUNK
    cps = _chunks_per_step(N)
    rows = cps * CHUNK

    def body(q_ref, k_ref, v_ref, g_ref, b_ref, o_ref, s_ref, inv_ref, S):
        @pl.when(pl.program_id(0) == 0)
        def _():
            S[...] = jnp.zeros_like(S)

        gv, bv = g_ref[...], b_ref[...]
        pairs = lambda f: tuple(f(c, h) for c in range(cps) for h in range(H))
        S_in = tuple(S[h] for h in range(H))
        for h in range(H):
            s_ref[h, 0] = S_in[h]
        (o, S_new), inv = _chunk_fn(pairs(lambda c, h: _tile_of(q_ref, c, h)), pairs(lambda c, h: _tile_of(k_ref, c, h)),
                                    pairs(lambda c, h: _tile_of(v_ref, c, h)),
                                    pairs(lambda c, h: _sel_lane(gv[c * CHUNK:(c + 1) * CHUNK], h)),
                                    pairs(lambda c, h: _sel_lane(bv[c * CHUNK:(c + 1) * CHUNK], h)), S_in)
        for c in range(cps):
            for h in range(H):
                o_ref[c * CHUNK:(c + 1) * CHUNK, h * HEAD_DIM:(h + 1) * HEAD_DIM] = o[c * H + h]
                inv_ref[h, c] = inv[c * H + h]
        for h in range(H):
            S[h] = S_new[h]

    blk = pl.BlockSpec((rows, H * HEAD_DIM), lambda n: (n, 0))
    gblk = pl.BlockSpec((rows, LANES), lambda n: (n, 0))
    outs, comm_outs = _call(
        body, name="delta_fwd", grid=(N // cps,), in_specs=[blk, blk, blk, gblk, gblk],
        out_specs=[blk, pl.BlockSpec((H, 1, HEAD_DIM, HEAD_DIM), lambda n: (0, n, 0, 0)),
                   pl.BlockSpec((H, cps, CHUNK, CHUNK), lambda n: (0, n, 0, 0))],
        out_shape=[jax.ShapeDtypeStruct((T, H * HEAD_DIM), F32), jax.ShapeDtypeStruct((H, N // cps, HEAD_DIM, HEAD_DIM), F32),
                   jax.ShapeDtypeStruct((H, N, CHUNK, CHUNK), F32)],
        scratch_shapes=[pltpu.VMEM((H, HEAD_DIM, HEAD_DIM), F32)],
        semantics=("arbitrary",), args=(q, k, v, g, beta), comm=comm)
    return outs[0], outs[1], outs[2], comm_outs


def _delta_bwd(q, k, v, g, beta, S0, inv, do, comm=None):
    T = q.shape[0]
    H, N = q.shape[1] // HEAD_DIM, T // CHUNK
    cps = _chunks_per_step(N)
    rows, NS = cps * CHUNK, N // cps

    def body(q_ref, k_ref, v_ref, g_ref, b_ref, s_ref, inv_ref, do_ref, dq_ref, dk_ref, dv_ref, dg_ref, db_ref, dS):
        @pl.when(pl.program_id(0) == 0)
        def _():
            dS[...] = jnp.zeros_like(dS)

        gv, bv = g_ref[...], b_ref[...]
        pairs = lambda f: tuple(f(c, h) for c in range(cps) for h in range(H))
        heads = lambda f: tuple(f(h) for h in range(H))
        _, vjp, _ = jax.vjp(_chunk_fn, pairs(lambda c, h: _tile_of(q_ref, c, h)), pairs(lambda c, h: _tile_of(k_ref, c, h)),
                            pairs(lambda c, h: _tile_of(v_ref, c, h)),
                            pairs(lambda c, h: _sel_lane(gv[c * CHUNK:(c + 1) * CHUNK], h)),
                            pairs(lambda c, h: _sel_lane(bv[c * CHUNK:(c + 1) * CHUNK], h)),
                            heads(lambda h: s_ref[h, 0]), pairs(lambda c, h: inv_ref[h, c]), has_aux=True)
        dq, dk, dv, dgB, dbB, dS_prev, _ = vjp((pairs(lambda c, h: _tile_of(do_ref, c, h)), heads(lambda h: dS[h])))
        for c in range(cps):
            for h in range(H):
                r, sl = slice(c * CHUNK, (c + 1) * CHUNK), slice(h * HEAD_DIM, (h + 1) * HEAD_DIM)
                dq_ref[r, sl] = dq[c * H + h]
                dk_ref[r, sl] = dk[c * H + h]
                dv_ref[r, sl] = dv[c * H + h]
                dg_ref[h, r] = dgB[c * H + h]
                db_ref[h, r] = dbB[c * H + h]
        for h in range(H):
            dS[h] = dS_prev[h]

    blk = pl.BlockSpec((rows, H * HEAD_DIM), lambda n: (NS - 1 - n, 0))
    gblk = pl.BlockSpec((rows, LANES), lambda n: (NS - 1 - n, 0))
    hblk = pl.BlockSpec((H, rows, LANES), lambda n: (0, NS - 1 - n, 0))
    sd = jax.ShapeDtypeStruct
    outs, comm_outs = _call(
        body, name="delta_bwd", grid=(NS,),
        in_specs=[blk, blk, blk, gblk, gblk, pl.BlockSpec((H, 1, HEAD_DIM, HEAD_DIM), lambda n: (0, NS - 1 - n, 0, 0)),
                  pl.BlockSpec((H, cps, CHUNK, CHUNK), lambda n: (0, NS - 1 - n, 0, 0)), blk],
        out_specs=[blk, blk, blk, hblk, hblk],
        out_shape=[sd((T, H * HEAD_DIM), F32)] * 3 + [sd((H, T, LANES), F32)] * 2,
        scratch_shapes=[pltpu.VMEM((H, HEAD_DIM, HEAD_DIM), F32)],
        semantics=("arbitrary",), args=(q, k, v, g, beta, S0, inv, do), comm=comm)
    return (*outs, comm_outs)


def _gnorm_fwd(o, proj, z_coff, gdn_t, DNW):
    T = o.shape[0]

    def fn(j, i, ov, zv, gv):
        def one(oh, zh, gh):
            r = lax.rsqrt(jnp.mean(oh * oh, axis=1, keepdims=True) + EPS)
            return oh * r * gh * (zh * _sigmoid(zh))
        return (_per_head(one, ov, zv, jnp.broadcast_to(gv, ov.shape)),)

    return _tiled(fn, T=T, C=DNW, ins=[("cur", o, None), ("cur", proj, lambda j: j + z_coff), ("row", gdn_t, None)],
                  out_dtypes=[BF16], cb=DNW, name="gnorm_fwd")[0]


def _gnorm_bwd(dymix, y_coff, o, proj, z_coff, gdn_t, DNW):
    T = o.shape[0]
    nh = DNW // HEAD_DIM

    def fn(j, i, dy, ov, zv, gv):
        dos, dzs, dgs = [], [], jnp.zeros((1, HEAD_DIM), F32)
        for h in range(nh):
            sl = slice(h * HEAD_DIM, (h + 1) * HEAD_DIM)
            dyh, oh, zh, gh = dy[:, sl].astype(F32), ov[:, sl], zv[:, sl], gv[:, sl]
            r = lax.rsqrt(jnp.mean(oh * oh, axis=1, keepdims=True) + EPS)
            on = oh * r
            sg = _sigmoid(zh)
            sz = zh * sg
            dzs.append(dyh * on * gh * (sg * (1.0 + zh * (1.0 - sg))))
            don = dyh * gh * sz
            dos.append(r * (don - on * jnp.mean(don * on, axis=1, keepdims=True)))
            dgs = dgs + jnp.sum(dyh * on * sz, axis=0, keepdims=True)
        cat = (lambda xs: xs[0] if nh == 1 else jnp.concatenate(xs, axis=1))
        return cat(dos), cat(dzs), _row0(dgs)

    T_ = T
    nI = T_ // _tile(T_, 256, HALO)
    tb = T_ // nI
    specs_cb = DNW

    def body_wrap():
        def body(dy_ref, o_ref, z_ref, g_ref, do_ref, dz_ref, dg_ref):
            i = pl.program_id(0)
            d_o, d_z, d_g = fn(0, i, dy_ref[...], o_ref[...], z_ref[...], g_ref[...])
            do_ref[...] = d_o
            dz_ref[...] = d_z.astype(dz_ref.dtype)

            @pl.when(i == 0)
            def _():
                dg_ref[...] = d_g

            @pl.when(i > 0)
            def _():
                dg_ref[...] += d_g

        return pl.pallas_call(
            body, name="gnorm_bwd", grid=(nI,),
            in_specs=[pl.BlockSpec((tb, specs_cb), lambda i: (i, y_coff)), pl.BlockSpec((tb, specs_cb), lambda i: (i, 0)),
                      pl.BlockSpec((tb, specs_cb), lambda i: (i, z_coff)), pl.BlockSpec((1, specs_cb), lambda i: (0, 0))],
            out_specs=[pl.BlockSpec((tb, specs_cb), lambda i: (i, 0)), pl.BlockSpec((tb, specs_cb), lambda i: (i, 0)),
                       pl.BlockSpec((HALO, HEAD_DIM), lambda i: (0, 0))],
            out_shape=[jax.ShapeDtypeStruct((T_, DNW), F32), jax.ShapeDtypeStruct((T_, DNW), BF16),
                       jax.ShapeDtypeStruct((HALO, HEAD_DIM), F32)],
            compiler_params=pltpu.CompilerParams(dimension_semantics=("arbitrary",), vmem_limit_bytes=VMEM_LIMIT),
        )(dymix, o, proj, gdn_t)

    return body_wrap()


def _ffn_fwd(up_g, up_v, w_g, w_v, cb):
    T, F = up_g.shape

    def fn(j, i, ug, uv, wg, wv):
        cg = _own(_conv_causal(ug, wg))
        cv = _own(_conv_causal(uv, wv))
        return (cg * _sigmoid(cg) * cv,)

    return _tiled(fn, T=T, C=F, ins=[("ext", up_g, None), ("ext", up_v, None), ("row", w_g, None), ("row", w_v, None)],
                  out_dtypes=[BF16], tb=1024, cb=cb, name="ffn_fwd")[0]


def _ffn_bwd(dact, up_g, up_v, w_g, w_v, cb):
    T, F = up_g.shape
    K = w_g.shape[0]

    def fn(j, i, da, ug, uv, wg, wv):
        cg = _conv_causal(ug, wg)
        cv = _conv_causal(uv, wv)
        sg = _sigmoid(cg)
        dgate = da * cv * (sg * (1.0 + cg * (1.0 - sg)))
        dval = da * (cg * sg)
        return (_own(_conv_anti(dgate, wg)), _own(_conv_anti(dval, wv)), _conv_dw(dgate, ug, K), _conv_dw(dval, uv, K))

    return _tiled(fn, T=T, C=F, ins=[("ext", dact, None), ("ext", up_g, None), ("ext", up_v, None), ("row", w_g, None),
                                      ("row", w_v, None)], out_dtypes=[BF16, BF16], acc_rows=[HALO, HALO], tb=1024, cb=cb,
                  name="ffn_bwd")


def _wide(R, Cc, n_f32, unit=HALO):
    cb = Cc if (Cc % LANES or Cc <= 4096) else _tile(Cc, 2048, LANES)
    cap = max(unit, EW_VMEM_BUDGET // (2 * 4 * n_f32 * cb) // unit * unit)
    return _tile(R, cap, unit), cb


def _adamw(w, g, m, v, name, comm=None):
    R, Cc = w.shape
    tb, cb = _wide(R, Cc, 7) if R % HALO == 0 else (R, _tile(Cc, EW_VMEM_BUDGET // (2 * 4 * 7 * R) // LANES * LANES, LANES))
    c1 = 1.0 / (1.0 - ADAM_B1 ** ADAM_STEP)
    c2 = 1.0 / (1.0 - ADAM_B2 ** ADAM_STEP)

    def fn(j, i, wv, gv, mv, vv):
        m2 = ADAM_B1 * mv + (1.0 - ADAM_B1) * gv
        v2 = ADAM_B2 * vv + (1.0 - ADAM_B2) * (gv * gv)
        delta = -ADAM_LR * ((m2 * c1) / (jnp.sqrt(v2 * c2) + ADAM_EPS) + ADAM_WD * wv)
        return delta, m2, v2

    return _tiled(fn, T=R, C=Cc, ins=[("cur", w, None), ("cur", g, None), ("cur", m, None), ("cur", v, None)],
                  out_dtypes=[F32, F32, F32], tb=tb, cb=cb, name=name, comm=comm)


def _join_shards(w4, n_main):
    S4, R, cs = w4.shape
    n_small = S4 * cs - n_main
    assert 0 < n_small <= LANES and n_small <= cs
    tb = _tile(R, 256, 2 * HALO)

    def body(w_ref, main_ref, small_ref):
        for t in range(S4 - 1):
            main_ref[:, t * cs:(t + 1) * cs] = w_ref[t]
        last = w_ref[S4 - 1]
        main_ref[:, (S4 - 1) * cs:] = last[:, :cs - n_small]
        small_ref[...] = jnp.zeros_like(small_ref)
        small_ref[:, :n_small] = last[:, cs - n_small:]

    return pl.pallas_call(
        body, name="join_w_in", grid=(R // tb,), in_specs=[pl.BlockSpec((S4, tb, cs), lambda i: (0, i, 0))],
        out_specs=[pl.BlockSpec((tb, n_main), lambda i: (i, 0)), pl.BlockSpec((tb, LANES), lambda i: (i, 0))],
        out_shape=[jax.ShapeDtypeStruct((R, n_main), w4.dtype), jax.ShapeDtypeStruct((R, LANES), w4.dtype)],
        compiler_params=pltpu.CompilerParams(dimension_semantics=("parallel",), vmem_limit_bytes=VMEM_LIMIT))(w4)


def _split_shards(main, small, cs):
    R, n_main = main.shape
    n_small = 4 * cs - n_main
    tb = _tile(R, 256, HALO)

    def body(main_ref, small_ref, out_ref):
        for t in range(3):
            out_ref[t] = main_ref[:, t * cs:(t + 1) * cs]
        out_ref[3, :, :cs - n_small] = main_ref[:, 3 * cs:]
        out_ref[3, :, cs - n_small:] = small_ref[:, :n_small]

    return pl.pallas_call(
        body, name="split_g_in", grid=(R // tb,),
        in_specs=[pl.BlockSpec((tb, n_main), lambda i: (i, 0)), pl.BlockSpec((tb, LANES), lambda i: (i, 0))],
        out_specs=pl.BlockSpec((4, tb, cs), lambda i: (0, i, 0)), out_shape=jax.ShapeDtypeStruct((4, R, cs), main.dtype),
        compiler_params=pltpu.CompilerParams(dimension_semantics=("parallel",), vmem_limit_bytes=VMEM_LIMIT))(main, small)


def _sum_stack(st, name):
    S, R, Cc = st.shape
    cb = _tile(Cc, 512, LANES) if Cc % LANES == 0 else Cc

    def fn(j, i, sv):
        t = sv[0]
        for s in range(1, S):
            t = t + sv[s]
        return (t,)

    return _tiled(fn, T=R, C=Cc, ins=[("stack", st, None)], out_dtypes=[F32], cb=cb, name=name)[0]


ANY = pl.BlockSpec(memory_space=pl.ANY)


def _place():
    x, y, c = lax.axis_index("x"), lax.axis_index("y"), lax.axis_index("c")
    return x, y, c, 2 * x + y


def _chip_dev(s, c):
    return (s // 2, s % 2, c)


class _Comm:
    def __init__(self, ins, out_shapes, sems, start, wait, aliases=None):
        self.ins, self.out_shapes, self.sems = list(ins), list(out_shapes), list(sems)
        self.start, self.wait, self.aliases = start, wait, dict(aliases or {})


def _merge(*comms):
    offs, i, o, s = [], 0, 0, 0
    for cm in comms:
        offs.append((i, o, s))
        i, o, s = i + len(cm.ins), o + len(cm.out_shapes), s + len(cm.sems)

    def part(refs, k, cm):
        i0, o0, s0 = offs[k]
        return refs[0][i0:i0 + len(cm.ins)], refs[1][o0:o0 + len(cm.out_shapes)], refs[2][s0:s0 + len(cm.sems)]

    def start(*refs):
        for k, cm in enumerate(comms):
            cm.start(*part(refs, k, cm))

    def wait(*refs):
        for k, cm in enumerate(comms):
            cm.wait(*part(refs, k, cm))

    aliases = {}
    for k, cm in enumerate(comms):
        for a, b in cm.aliases.items():
            aliases[offs[k][0] + a] = offs[k][1] + b
    return _Comm([a for cm in comms for a in cm.ins], [a for cm in comms for a in cm.out_shapes],
                 [a for cm in comms for a in cm.sems], start, wait, aliases)


def _call(body, *, name, grid, in_specs, out_specs, out_shape, scratch_shapes, semantics, args, comm=None):
    if comm is None:
        outs = pl.pallas_call(
            body, name=name, grid=grid, in_specs=in_specs, out_specs=out_specs, out_shape=out_shape,
            scratch_shapes=list(scratch_shapes),
            compiler_params=pltpu.CompilerParams(dimension_semantics=semantics, vmem_limit_bytes=VMEM_LIMIT))(*args)
        return list(outs), []
    n_in, n_out, n_scr = len(in_specs), len(out_specs), len(scratch_shapes)
    ci, co = len(comm.ins), len(comm.out_shapes)

    def wrapped(*refs):
        r = 0
        ins, r = refs[r:r + n_in], r + n_in
        cins, r = refs[r:r + ci], r + ci
        outs, r = refs[r:r + n_out], r + n_out
        couts, r = refs[r:r + co], r + co
        scr, r = refs[r:r + n_scr], r + n_scr
        csems = refs[r:]
        ids = [pl.program_id(a) for a in range(len(grid))]
        first, last = ids[0] == 0, ids[0] == grid[0] - 1
        for a in range(1, len(grid)):
            first = jnp.logical_and(first, ids[a] == 0)
            last = jnp.logical_and(last, ids[a] == grid[a] - 1)

        @pl.when(first)
        def _():
            comm.start(cins, couts, csems)

        body(*ins, *outs, *scr)

        @pl.when(last)
        def _():
            comm.wait(cins, couts, csems)

    outs = pl.pallas_call(
        wrapped, name=name, grid=grid, in_specs=list(in_specs) + [ANY] * ci, out_specs=list(out_specs) + [ANY] * co,
        out_shape=list(out_shape) + comm.out_shapes, scratch_shapes=list(scratch_shapes) + comm.sems,
        input_output_aliases={n_in + a: n_out + b for a, b in comm.aliases.items()},
        compiler_params=pltpu.CompilerParams(dimension_semantics=("arbitrary",) * len(grid), vmem_limit_bytes=VMEM_LIMIT),
    )(*args, *comm.ins)
    return list(outs[:n_out]), list(outs[n_out:])


def _run_comm(comm, name):
    ci, co = len(comm.ins), len(comm.out_shapes)

    def body(*refs):
        cins, couts, csems = refs[:ci], refs[ci:ci + co], refs[ci + co:]
        comm.start(cins, couts, csems)
        comm.wait(cins, couts, csems)

    outs = pl.pallas_call(body, name=name, in_specs=[ANY] * ci, out_specs=[ANY] * co, out_shape=comm.out_shapes,
                          scratch_shapes=comm.sems, input_output_aliases=comm.aliases)(*comm.ins)
    return list(outs)


def _ag_comm(shard, land=None, q=0, nq=1):
    two, R2, Cc = shard.shape
    rows = pl.ds(q * (R2 // nq), R2 // nq)
    DMA = pltpu.SemaphoreType.DMA

    def copies(ins, outs, sems, which):
        sh, out = ins[0], outs[0]
        send1, recv1, send2, recv2, send0, recv0 = sems
        x, y, c, s = _place()
        sib = (x, y, 1 - c)
        rc = pltpu.make_async_remote_copy
        if which == "first":
            return [rc(sh.at[c, rows], out.at[s, c, rows], send1.at[m - 1], recv1.at[m - 1],
                       device_id=_chip_dev(s ^ m, c), device_id_type=MESH) for m in range(1, 4)]
        if which == "own":
            return [rc(sh.at[h, rows], out.at[s, h, rows], send0.at[h], recv0.at[h], device_id=sib, device_id_type=MESH)
                    for h in range(2)]
        if which == "landed":
            return [rc(sh.at[c, rows], out.at[s ^ m, c, rows], send1.at[m - 1], recv1.at[m - 1], device_id=sib,
                       device_id_type=MESH) for m in range(1, 4)]
        half = c if which == "passed" else 1 - c
        return [rc(out.at[s ^ m, half, rows], out.at[s ^ m, half, rows], send2.at[m - 1], recv2.at[m - 1], device_id=sib,
                   device_id_type=MESH) for m in range(1, 4)]

    def start(ins, outs, sems):
        for cp in copies(ins, outs, sems, "first") + copies(ins, outs, sems, "own"):
            cp.start()

    def wait(ins, outs, sems):
        passed = copies(ins, outs, sems, "passed")
        for lan, pas in zip(copies(ins, outs, sems, "landed"), passed):
            lan.wait_recv()
            pas.start()
        for cp in copies(ins, outs, sems, "handed"):
            cp.wait_recv()
        for cp in copies(ins, outs, sems, "own"):
            cp.wait()
        for cp in copies(ins, outs, sems, "first") + passed:
            cp.wait_send()

    return _Comm([shard] + ([land] if land is not None else []), [jax.ShapeDtypeStruct((4, two, R2, Cc), shard.dtype)],
                 [DMA((3,)), DMA((3,)), DMA((3,)), DMA((3,)), DMA((2,)), DMA((2,))], start, wait,
                 {1: 0} if land is not None else None)


def _a2a_comm(S1, q=0, nq=1, land=None, cnt=1):
    S4, R2, Cc = S1.shape
    rows = pl.ds(q * (R2 // nq), cnt * (R2 // nq))
    DMA = pltpu.SemaphoreType.DMA

    def copies(ins, outs, sems):
        x, y, c, s = _place()
        return [pltpu.make_async_remote_copy(ins[0].at[s ^ m, rows], outs[0].at[m - 1, rows], sems[0].at[m - 1],
                                             sems[1].at[m - 1], device_id=_chip_dev(s ^ m, c), device_id_type=MESH)
                for m in range(1, 4)]

    def start(ins, outs, sems):
        for cp in copies(ins, outs, sems):
            cp.start()

    def wait(ins, outs, sems):
        for cp in copies(ins, outs, sems):
            cp.wait()

    return _Comm([S1] + ([land] if land is not None else []), [jax.ShapeDtypeStruct((3, R2, Cc), S1.dtype)],
                 [DMA((3,)), DMA((3,))], start, wait, {1: 0} if land is not None else None)


def _halves(G):
    return G.reshape(G.shape[0], 2, G.shape[1] // 2, G.shape[2])


def _swap_comm(piece):
    n, two, R2, Cc = piece.shape
    DMA = pltpu.SemaphoreType.DMA

    def copies(ins, outs, sems):
        x, y, c, s = _place()
        return [pltpu.make_async_remote_copy(ins[0].at[t, 1 - c], outs[0].at[t], sems[0].at[t], sems[1].at[t],
                                             device_id=(x, y, 1 - c), device_id_type=MESH) for t in range(n)]

    def start(ins, outs, sems):
        for cp in copies(ins, outs, sems):
            cp.start()

    def wait(ins, outs, sems):
        for cp in copies(ins, outs, sems):
            cp.wait()

    return _Comm([piece], [jax.ShapeDtypeStruct((n, R2, Cc), piece.dtype)], [DMA((n,)), DMA((n,))], start, wait)


def _add_half(pieces, As, cidx, name):
    R2, Cc = pieces[0].shape[2:]
    S4 = sum(pc.shape[0] for pc in pieces)
    tb, cb = _wide(R2, Cc, 3, 2 * HALO)
    nI, nJ = R2 // tb, Cc // cb

    def body(c_ref, g_ref, a_ref, *rest):
        rest[-1][...] = (g_ref[0, 0] + a_ref[0]).astype(BF16)

    out, t0 = None, 0
    for k, (pc, A) in enumerate(zip(pieces, As)):
        grid_spec = pltpu.PrefetchScalarGridSpec(
            num_scalar_prefetch=1, grid=(pc.shape[0], nI, nJ),
            in_specs=[pl.BlockSpec((1, 1, tb, cb), lambda t, i, j, c_ref: (t, c_ref[0], i, j)),
                      pl.BlockSpec((1, tb, cb), lambda t, i, j, c_ref: (t, i, j))] + ([ANY] if k else []),
            out_specs=pl.BlockSpec((tb, cb), lambda t, i, j, c_ref, t0=t0: ((t0 + t) * nI + i, j)))
        out = pl.pallas_call(
            functools.partial(body), name=f"{name}{k}", grid_spec=grid_spec, out_shape=jax.ShapeDtypeStruct((S4 * R2, Cc), BF16),
            input_output_aliases={3: 0} if k else {},
            compiler_params=pltpu.CompilerParams(dimension_semantics=("parallel", "parallel", "parallel"),
                                                 vmem_limit_bytes=VMEM_LIMIT),
        )(*((cidx, pc, A) + ((out,) if k else ())))
        t0 += pc.shape[0]
    return out.reshape(S4, R2, Cc)


def _add_own(S1, B, chip_idx, cidx, name):
    S4, R2, Cc = S1.shape
    tb, cb = _wide(R2, Cc, 3, 2 * HALO)

    def body(s_idx, c_idx, s_ref, b_ref, o_ref):
        o_ref[...] = ((s_ref[0].astype(F32) + b_ref[0].astype(F32)) + b_ref[1].astype(F32)) + b_ref[2].astype(F32)

    grid_spec = pltpu.PrefetchScalarGridSpec(
        num_scalar_prefetch=2, grid=(R2 // tb, Cc // cb),
        in_specs=[pl.BlockSpec((1, tb, cb), lambda i, j, s_idx, c_idx: (s_idx[0], i, j)),
                  pl.BlockSpec((3, tb, cb), lambda i, j, s_idx, c_idx: (0, i, j))],
        out_specs=pl.BlockSpec((None, tb, cb), lambda i, j, s_idx, c_idx: (c_idx[0], i, j)))
    return pl.pallas_call(body, name=name, grid_spec=grid_spec, out_shape=jax.ShapeDtypeStruct((2, R2, Cc), F32),
                          compiler_params=pltpu.CompilerParams(dimension_semantics=("parallel", "parallel"),
                                                               vmem_limit_bytes=VMEM_LIMIT))(chip_idx, cidx, S1, B)


def _sibling_fill(Hs, name):
    def body(h_ref, out_ref, send, recv):
        x, y, c, s = _place()
        cp = pltpu.make_async_remote_copy(h_ref.at[c], out_ref.at[c], send, recv, device_id=(x, y, 1 - c), device_id_type=MESH)
        cp.start()
        cp.wait()

    return pl.pallas_call(
        body, name=name, in_specs=[ANY], out_specs=ANY, out_shape=jax.ShapeDtypeStruct(Hs.shape, Hs.dtype),
        input_output_aliases={0: 0}, scratch_shapes=[pltpu.SemaphoreType.DMA, pltpu.SemaphoreType.DMA],
    )(Hs)


def _gather_all(buf, name):
    R, Cc = buf.shape

    def body(b_ref, out_ref, send, recv, local):
        x, y, c, s = _place()
        d = 2 * s + c
        mine = pltpu.make_async_copy(b_ref, out_ref.at[d], local)
        mine.start()
        cps = []
        for m in range(1, 8):
            t = d ^ m
            cp = pltpu.make_async_remote_copy(b_ref, out_ref.at[d], send.at[m - 1], recv.at[m - 1],
                                              device_id=(t // 4, (t // 2) % 2, t % 2), device_id_type=MESH)
            cp.start()
            cps.append(cp)
        for cp in cps:
            cp.wait()
        mine.wait()

    return pl.pallas_call(
        body, name=name, in_specs=[ANY], out_specs=ANY, out_shape=jax.ShapeDtypeStruct((8, R, Cc), buf.dtype),
        scratch_shapes=[pltpu.SemaphoreType.DMA((7,)), pltpu.SemaphoreType.DMA((7,)), pltpu.SemaphoreType.DMA],
    )(buf)


def _finish_shard(S1, B, cidx, chip_idx, name):
    Hs = _sibling_fill(_add_own(S1, B, chip_idx, cidx, name + "_sum"), name + "_gather")
    return Hs.reshape(2 * Hs.shape[1], Hs.shape[2])


def _pack_rows(vs):
    flat = jnp.concatenate([v.reshape(-1) for v in vs])
    n = flat.shape[0]
    rows = -(-n // (LANES * 2 * HALO)) * 2 * HALO
    return jnp.pad(flat, (0, rows * LANES - n)).reshape(rows, LANES)


def _unpack_rows(buf, shapes):
    flat = buf.reshape(-1)
    outs, o = [], 0
    for shp in shapes:
        n = 1
        for d in shp:
            n *= d
        outs.append(flat[o:o + n].reshape(shp))
        o += n
    return outs


def kernel(x, p, norm_mix_g, w_in, conv_a_w, conv_qkv_w, a_log, dt_bias, dn_norm_g, w_out, norm_ffn_g, w_up, conv_ffn_w, w_down, norm_ple_g, w_ple_gate, w_ple_proj, final_norm_g, loss_target, m_norm_mix_g, m_w_in, m_conv_a_w, m_conv_qkv_w, m_a_log, m_dt_bias, m_dn_norm_g, m_w_out, m_norm_ffn_g, m_w_up, m_conv_ffn_w, m_w_down, m_norm_ple_g, m_w_ple_gate, m_w_ple_proj, m_final_norm_g, v_norm_mix_g, v_w_in, v_conv_a_w, v_conv_qkv_w, v_a_log, v_dt_bias, v_dn_norm_g, v_w_out, v_norm_ffn_g, v_w_up, v_conv_ffn_w, v_w_down, v_norm_ple_g, v_w_ple_gate, v_w_ple_proj, v_final_norm_g):
    xs = x[0]
    ps = p[0, 0]
    tgt = loss_target[0]
    T, D = xs.shape
    H = a_log.shape[-1]
    DNW = H * HEAD_DIM
    CW = conv_a_w.shape[-1] * 4
    F = w_down.shape[1] * 4
    PD = ps.shape[-1]
    IN_MAIN = 3 * CW + 4 * DNW
    IN_COLS = IN_MAIN + 2 * H
    assert w_in.shape[-1] * 4 == IN_COLS and CW + DNW == D and 2 * H <= LANES
    cb = _tile(min(CW, DNW), 512, LANES)
    while F % cb:
        cb -= LANES
    cidx = lax.axis_index("c").astype(jnp.int32).reshape(1)
    chip = 2 * lax.axis_index("x") + lax.axis_index("y")

    def halves(w):
        sh = w[0].astype(BF16)
        return sh.reshape(2, sh.shape[0] // 2, sh.shape[1])

    def whole(land):
        return land.reshape(4, 2 * land.shape[2], land.shape[3])

    def rows(g4):
        return g4.reshape(4 * g4.shape[1], g4.shape[2])

    conv_shapes = [conv_a_w[0].shape, conv_qkv_w[0].shape, conv_ffn_w[0].shape]
    cpack = _pack_rows([conv_a_w[0], conv_qkv_w[0], conv_ffn_w[0]])
    sh_in, sh_out, sh_up, sh_down, sh_pg, sh_pp = (halves(w) for w in (w_in, w_out, w_up, w_down, w_ple_gate, w_ple_proj))
    l_in, cg = _run_comm(_merge(_ag_comm(sh_in), _ag_comm(cpack.reshape(2, cpack.shape[0] // 2, LANES))), "ag_w_in_conv")
    w_in_main, w_in_small = _join_shards(whole(l_in), IN_MAIN)
    cg = cg.reshape(4, cpack.shape[0], LANES)
    parts = [_unpack_rows(cg[t], conv_shapes) for t in range(4)]
    cw_a = jnp.concatenate([parts[t][0] for t in range(4)], axis=1)
    cw_qkv = jnp.concatenate([parts[t][1] for t in range(4)], axis=1)
    cw_ffn = jnp.concatenate([parts[t][2] for t in range(4)], axis=1)
    cw_q, cw_k, cw_v = cw_qkv[:, :DNW], cw_qkv[:, DNW:2 * DNW], cw_qkv[:, 2 * DNW:]
    cw_fg, cw_fv = cw_ffn[:, :F], cw_ffn[:, F:]
    pad_row = lambda v: jnp.pad(v, ((0, 0), (0, LANES - v.shape[1])))
    a_log_row, dt_row = pad_row(a_log), pad_row(dt_bias)
    gdn_t = jnp.tile(dn_norm_g, (1, H))
    gfin = final_norm_g.reshape(1, D)

    h1 = _rms_fwd(xs, norm_mix_g, "rms1")
    proj, (l_up,) = _mm(h1, w_in_main, mode="nn", out_dtypes=[F32], name="mm_proj", comm=_ag_comm(sh_up, q=0, nq=2))
    small = _mm(h1, w_in_small, mode="nn", out_dtypes=[F32], name="mm_small")
    ya = _ga_fwd(proj, cw_a, CW, cb)
    nq = 3 * CW // cb
    nd = DNW // cb
    qn, (l_out,) = _qkv_fwd(proj, cw_q, nq, True, DNW, cb, "q_fwd", comm=_ag_comm(sh_out, q=0, nq=2))
    kn, (l_out,) = _qkv_fwd(proj, cw_k, nq + nd, True, DNW, cb, "k_fwd", comm=_ag_comm(sh_out, l_out, q=1, nq=2))
    vs = _qkv_fwd(proj, cw_v, nq + 2 * nd, False, DNW, cb, "v_fwd")
    g, beta = _gb_fwd(small, a_log_row, dt_row, H)
    o, S0, inv_c, (l_up,) = _delta_fwd(qn, kn, vs, g, beta, comm=_ag_comm(sh_up, l_up, q=1, nq=2))
    w_out_f = rows(whole(l_out))
    w_out_a, w_out_b = w_out_f[:CW], w_out_f[CW:]
    w_up_4 = whole(l_up)
    z_coff = (3 * CW + 3 * DNW) // DNW
    assert (3 * CW + 3 * DNW) % DNW == 0 and CW % DNW == 0
    yb = _gnorm_fwd(o, proj, z_coff, gdn_t, DNW)
    add = lambda acc, r: (r + acc,)
    x1 = _mm(ya, w_out_a, mode="nn", out_dtypes=[F32], epi=add, extras=[xs], name="mm_out_a")
    x1 = _mm(yb, w_out_b, mode="nn", out_dtypes=[F32], epi=add, extras=[x1], name="mm_out_b")
    h2 = _rms_fwd(x1, norm_ffn_g, "rms2")
    up_g, (l_down,) = _mm(h2, w_up_4, mode="nn", b_split=(0, 2), out_dtypes=[F32], name="mm_up_g",
                          comm=_ag_comm(sh_down, q=0, nq=2))
    up_v, (l_down,) = _mm(h2, w_up_4, mode="nn", b_split=(2, 2), out_dtypes=[F32], name="mm_up_v",
                          comm=_ag_comm(sh_down, l_down, q=1, nq=2))
    w_down_f = rows(whole(l_down))
    act = _ffn_fwd(up_g, up_v, cw_fg, cw_fv, cb)
    x2, (l_pg, l_pp) = _mm(act, w_down_f, mode="nn", out_dtypes=[F32], epi=add, extras=[x1], name="mm_down",
                           comm=_merge(_ag_comm(sh_pg), _ag_comm(sh_pp)))
    w_pg_f = rows(whole(l_pg))
    w_pp_4 = whole(l_pp)
    h3 = _rms_fwd(x2, norm_ple_g, "rms3")
    pp = _mm(ps, w_pp_4, mode="nn", b_split=(0, 4), out_dtypes=[F32], name="mm_pp")

    def ple_epi(acc, x2v, ppv):
        pg = _sigmoid(acc)
        return x2v + pg * ppv, pg

    x3, pg = _mm(h3, w_pg_f, mode="nn", out_dtypes=[F32, F32], epi=ple_epi, extras=[x2, pp], name="mm_pg")

    dx3, dpg, dpp, fin = _final_fb(x3, tgt, gfin, pp, pg)
    loss = lax.psum(jnp.sum(fin[1]), ("x", "y", "c"))
    d_gfin = fin[0:1]
    def split_rows(dW):
        return dW.reshape(4, dW.shape[0] // 4, dW.shape[1])

    dW_pp = _mm(ps, dpp, mode="tn", out_split=4, out_dtypes=[F32], name="mm_dw_pp")
    dW_pg = _mm(h3, dpg, mode="tn", out_dtypes=[F32], name="mm_dw_pg")
    P_pp, P_pg = _halves(dW_pp), _halves(split_rows(dW_pg))
    dh3, (A_pp, A_pg) = _mm(dpg, w_pg_f, mode="nt", out_dtypes=[F32], name="mm_dh3",
                            comm=_merge(_swap_comm(P_pp), _swap_comm(P_pg)))
    S_pp = _add_half([P_pp], [A_pp], cidx, "rs_w_pp_add")
    S_pg = _add_half([P_pg], [A_pg], cidx, "rs_w_pg_add")
    dx2, dx2_b, d_gple = _rms_bwd(dh3, x2, norm_ple_g, dx3, "rms3_bwd")
    dW_down, (B_pp, B_pg) = _mm(act, dx2_b, mode="tn", out_dtypes=[F32], name="mm_dw_down",
                                comm=_merge(_a2a_comm(S_pp), _a2a_comm(S_pg)))
    P_down = _halves(split_rows(dW_down))
    dact, (A_down,) = _mm(dx2_b, w_down_f, mode="nt", out_dtypes=[F32], name="mm_dact", comm=_swap_comm(P_down))
    S_down = _add_half([P_down], [A_down], cidx, "rs_w_down_add")
    dup_g, dup_v, dcw_fg, dcw_fv = _ffn_bwd(dact, up_g, up_v, cw_fg, cw_fv, cb)
    dW_up_g, (B_down,) = _mm(h2, dup_g, mode="tn", out_split=2, out_dtypes=[F32], name="mm_dw_up_g", comm=_a2a_comm(S_down))
    P_ug = _halves(dW_up_g)
    dW_up_v, (A_ug,) = _mm(h2, dup_v, mode="tn", out_split=2, out_dtypes=[F32], name="mm_dw_up_v", comm=_swap_comm(P_ug))
    P_uv = _halves(dW_up_v)
    dh2, (A_uv,) = _mm(dup_g, w_up_4, mode="nt", b_split=(0, 2), out_dtypes=[F32], name="mm_dh2_g", comm=_swap_comm(P_uv))
    S_up = _add_half([P_ug, P_uv], [A_ug, A_uv], cidx, "rs_w_up_add")
    dh2, (B_up,) = _mm(dup_v, w_up_4, mode="nt", b_split=(2, 2), out_dtypes=[F32], epi=add, extras=[dh2], name="mm_dh2_v",
                       comm=_a2a_comm(S_up, 0, 2))
    dx1, dx1_b, d_gffn = _rms_bwd(dh2, x1, norm_ffn_g, dx2, "rms2_bwd")
    dW_out_a = _mm(ya, dx1_b, mode="tn", out_dtypes=[F32], name="mm_dw_out_a")
    dW_out_b = _mm(yb, dx1_b, mode="tn", out_dtypes=[F32], name="mm_dw_out_b")
    P_oa, P_ob = _halves(dW_out_a.reshape(-1, D // 4, D)), _halves(dW_out_b.reshape(-1, D // 4, D))
    dymix, (A_oa, A_ob) = _mm(dx1_b, w_out_f, mode="nt", out_dtypes=[F32], name="mm_dymix",
                              comm=_merge(_swap_comm(P_oa), _swap_comm(P_ob)))
    S_out = _add_half([P_oa, P_ob], [A_oa, A_ob], cidx, "rs_w_out_add")
    dax, dab, dac, dcw_a = _ga_bwd(dymix, proj, cw_a, CW, cb)
    do, dz, d_gdn = _gnorm_bwd(dymix, CW // DNW, o, proj, z_coff, gdn_t, DNW)
    dqn, dkn, dvs, dgB, dbB, (B_up, B_out) = _delta_bwd(qn, kn, vs, g, beta, S0, inv_c, do,
                                                        comm=_merge(_a2a_comm(S_up, 1, 2, B_up), _a2a_comm(S_out)))
    dq_pre, dcw_q = _qkv_bwd(dqn, proj, cw_q, nq, True, DNW, cb, "q_bwd")
    dk_pre, dcw_k = _qkv_bwd(dkn, proj, cw_k, nq + nd, True, DNW, cb, "k_bwd")
    dv_pre, dcw_v = _qkv_bwd(dvs, proj, cw_v, nq + 2 * nd, False, DNW, cb, "v_bwd")
    dsmall, d_ab = _gb_bwd(dgB, dbB, small, g, beta, a_log_row, dt_row, H)
    dproj = jnp.concatenate([dax, dab, dac, dq_pre, dk_pre, dv_pre, dz], axis=1)
    dW_in_main = _mm(h1, dproj, mode="tn", out_dtypes=[F32], name="mm_dw_in")
    dW_in_small = _mm(h1, dsmall, mode="tn", out_dtypes=[F32], name="mm_dw_in_small")
    chip_idx = chip.astype(jnp.int32).reshape(1)

    def update(S1, B, w, m, v, name, comm=None):
        gr = _finish_shard(S1, B, cidx, chip_idx, "rs_" + name)
        if gr.shape[1] % LANES == 0:
            res = _adamw(w[0], gr, m[0], v[0], "adamw_" + name, comm=comm)
            (delta, m2, v2), extra = (res, None) if comm is None else res
            out = (gr[None], delta[None], m2[None], v2[None])
            return out if comm is None else (out, extra)
        tr = jnp.transpose
        grt = tr(gr)
        delta, m2, v2 = _adamw(tr(w[0]), grt, tr(m[0]), tr(v[0]), "adamw_" + name)
        return tr(grt)[None], tr(delta)[None], tr(m2)[None], tr(v2)[None]

    P_in = _halves(_split_shards(dW_in_main, dW_in_small, IN_COLS // 4))
    (A_in,) = _run_comm(_swap_comm(P_in), "rs_w_in_swap")
    S_in = _add_half([P_in], [A_in], cidx, "rs_w_in_add")
    dh1, (B_in,) = _mm(dproj, w_in_main, mode="nt", out_dtypes=[F32], name="mm_dh1", comm=_a2a_comm(S_in, 0, 8, cnt=7))
    dh1, (B_in,) = _mm(dsmall, w_in_small, mode="nt", out_dtypes=[F32], epi=add, extras=[dh1], name="mm_dh1_small",
                       comm=_a2a_comm(S_in, 7, 8, B_in))
    dx, _, d_gmix = _rms_bwd(dh1, xs, norm_mix_g, dx1, "rms1_bwd")

    big = {
        "w_in": update(S_in, B_in, w_in, m_w_in, v_w_in, "w_in"),
        "w_out": update(S_out, B_out, w_out, m_w_out, v_w_out, "w_out"),
        "w_up": update(S_up, B_up, w_up, m_w_up, v_w_up, "w_up"),
        "w_down": update(S_down, B_down, w_down, m_w_down, v_w_down, "w_down"),
        "w_ple_gate": update(S_pg, B_pg, w_ple_gate, m_w_ple_gate, v_w_ple_gate, "w_pg"),
        "w_ple_proj": update(S_pp, B_pp, w_ple_proj, m_w_ple_proj, v_w_ple_proj, "w_pp"),
    }

    small_grads = [d_gmix[0:1], dcw_a[:cw_a.shape[0]], jnp.concatenate([dcw_q, dcw_k, dcw_v], axis=1)[:cw_qkv.shape[0]],
                   d_ab[0:1, :H], d_ab[1:2, :H], d_gdn[0:1], d_gffn[0:1],
                   jnp.concatenate([dcw_fg, dcw_fv], axis=1)[:cw_ffn.shape[0]], d_gple[0:1], d_gfin]
    small_shapes = [v.shape for v in small_grads]
    gpack = _pack_rows(small_grads)
    gsum = _sum_stack(_gather_all(gpack, "ag_small"), "sum_small")
    (g_gmix, g_cwa, g_cwqkv, g_alog, g_dt, g_gdn, g_gffn, g_cwffn, g_gple, g_gfin) = _unpack_rows(gsum, small_shapes)

    def my_cols(v):
        Cc = v.shape[1] // 4
        return lax.dynamic_slice_in_dim(v, chip * Cc, Cc, axis=1)

    g_small = [g_gmix, my_cols(g_cwa), my_cols(g_cwqkv), g_alog, g_dt, g_gdn, g_gffn, my_cols(g_cwffn), g_gple, g_gfin]
    w_small = [norm_mix_g, conv_a_w[0], conv_qkv_w[0], a_log, dt_bias, dn_norm_g, norm_ffn_g, conv_ffn_w[0], norm_ple_g, gfin]
    m_small = [m_norm_mix_g, m_conv_a_w[0], m_conv_qkv_w[0], m_a_log, m_dt_bias, m_dn_norm_g, m_norm_ffn_g, m_conv_ffn_w[0],
               m_norm_ple_g, m_final_norm_g.reshape(1, D)]
    v_small = [v_norm_mix_g, v_conv_a_w[0], v_conv_qkv_w[0], v_a_log, v_dt_bias, v_dn_norm_g, v_norm_ffn_g, v_conv_ffn_w[0],
               v_norm_ple_g, v_final_norm_g.reshape(1, D)]
    shp = [v.shape for v in w_small]
    ds_, ms_, vs_ = _adamw(_pack_rows(w_small), _pack_rows(g_small), _pack_rows(m_small), _pack_rows(v_small), "adamw_small")
    out_shapes = [norm_mix_g.shape, conv_a_w.shape, conv_qkv_w.shape, a_log.shape, dt_bias.shape, dn_norm_g.shape,
                  norm_ffn_g.shape, conv_ffn_w.shape, norm_ple_g.shape, final_norm_g.shape]
    rs = lambda vals: [v.reshape(s) for v, s in zip(vals, out_shapes)]
    sg, sd_, sm_, sv_ = rs(g_small), rs(_unpack_rows(ds_, shp)), rs(_unpack_rows(ms_, shp)), rs(_unpack_rows(vs_, shp))
    names_small = ["norm_mix_g", "conv_a_w", "conv_qkv_w", "a_log", "dt_bias", "dn_norm_g", "norm_ffn_g", "conv_ffn_w",
                   "norm_ple_g", "final_norm_g"]
    res = {n: (sg[i], sd_[i], sm_[i], sv_[i]) for i, n in enumerate(names_small)}
    res.update(big)
    order = ["norm_mix_g", "w_in", "conv_a_w", "conv_qkv_w", "a_log", "dt_bias", "dn_norm_g", "w_out", "norm_ffn_g", "w_up",
             "conv_ffn_w", "w_down", "norm_ple_g", "w_ple_gate", "w_ple_proj", "final_norm_g"]
    return (loss, dx[None], *[res[n][0] for n in order], *[res[n][1] for n in order], *[res[n][2] for n in order],
            *[res[n][3] for n in order])
```

```python
import functools

import jax
import jax.numpy as jnp
from jax import lax
from jax.experimental import pallas as pl
from jax.experimental.pallas import tpu as pltpu

F32 = jnp.float32
BF16 = jnp.bfloat16
LANES = 128
HALO = 8
HEAD_DIM = 128
CHUNK = 64
EPS = 1e-6
VMEM_LIMIT = 56 * 1024 * 1024
MM_VMEM_BUDGET = 40 * 1024 * 1024
MM_STEP_BYTES = 1 << 20
EW_VMEM_BUDGET = 28 * 1024 * 1024
MESH = pl.DeviceIdType.MESH

ADAM_LR, ADAM_B1, ADAM_B2, ADAM_EPS, ADAM_WD, ADAM_STEP = 0.001, 0.9, 0.999, 1e-08, 0.01, 10


def _tile(n, cap, unit):
    if n <= cap:
        return n
    d = (cap // unit) * unit
    while d >= unit:
        if n % d == 0:
            return d
        d -= unit
    raise ValueError(f"no tile for {n} (cap {cap}, unit {unit})")


def _sigmoid(x):
    return 1.0 / (1.0 + jnp.exp(-x))


def _divisors(n, cap):
    ds = [d for d in range(cap // LANES * LANES, 0, -LANES) if n % d == 0]
    return [n] if (n <= cap or not ds) else ds


def _mm_tiles(M, N, K, n_unit, k_unit, a_bytes, n_blocks_mn, a_transposed):
    best = None
    for tm in _divisors(M, 1536):
        for tn in _divisors(n_unit, 1536):
            for tk in _divisors(k_unit, 4096):
                nk = K // tk
                vmem = 2 * tm * tk * a_bytes + 2 * tk * tn * 2 + 2 * 4 * tm * tn * n_blocks_mn + (4 * tm * tn if nk > 1 else 0)
                if vmem > MM_VMEM_BUDGET:
                    continue
                steps = (M // tm) * (N // tn) * nk
                cost = (M * K * a_bytes * (N // tn if nk > 1 else 1) + K * N * 2 * (M // tm) + 4 * M * N * n_blocks_mn
                        + (8 * M * N * nk // 3 if nk > 1 else 0) + steps * MM_STEP_BYTES
                        + (2 * steps * tm * tk if a_transposed else 0))
                if best is None or cost < best[0]:
                    best = (cost, tm, tn, tk)
    return best[1:]


def _mm(a, b, *, mode, out_dtypes, name, epi=None, extras=(), comm=None, b_split=None, out_split=None):
    if b_split is not None:
        lo, ns = b_split
        Rb, Cb = b.shape[1], b.shape[2]
    if mode == "nn":
        (M, K), N = a.shape, (ns * Cb if b_split else b.shape[1])
    elif mode == "nt":
        (M, K), N = a.shape, (Rb if b_split else b.shape[0])
    else:
        (K, M), N = a.shape, b.shape[1]
    n_ex, n_out = len(extras), len(out_dtypes)
    n_unit = Cb if (b_split and mode == "nn") else (N // out_split if out_split else N)
    k_unit = Cb if (b_split and mode == "nt") else K
    tm, tn, tk = _mm_tiles(M, N, K, n_unit, k_unit, a.dtype.itemsize, n_ex + n_out, mode == "tn")
    nk = K // tk
    a_spec = pl.BlockSpec((tk, tm), lambda i, j, k: (k, i)) if mode == "tn" else pl.BlockSpec((tm, tk), lambda i, j, k: (i, k))
    if b_split and mode == "nn":
        nb = Cb // tn
        b_spec = pl.BlockSpec((None, tk, tn), lambda i, j, k: (lo + j // nb, k, j % nb))
    elif b_split:
        nb = Cb // tk
        b_spec = pl.BlockSpec((None, tn, tk), lambda i, j, k: (lo + k // nb, j, k % nb))
    else:
        b_spec = pl.BlockSpec((tn, tk), lambda i, j, k: (j, k)) if mode == "nt" else pl.BlockSpec((tk, tn), lambda i, j, k: (k, j))
    mn_spec = pl.BlockSpec((tm, tn), lambda i, j, k: (i, j))
    out_shapes = [jax.ShapeDtypeStruct((M, N), dt) for dt in out_dtypes]
    out_specs = [mn_spec] * n_out
    if out_split:
        assert n_ex == 0 and n_out == 1
        nbo = (N // out_split) // tn
        out_specs = [pl.BlockSpec((None, tm, tn), lambda i, j, k: (j // nbo, i, j % nbo))]
        out_shapes = [jax.ShapeDtypeStruct((out_split, M, N // out_split), out_dtypes[0])]
    dims = {"nn": (((1,), (0,)), ((), ())), "nt": (((1,), (1,)), ((), ())), "tn": (((0,), (0,)), ((), ()))}[mode]

    def body(*refs):
        a_ref, b_ref = refs[0], refs[1]
        ex_refs = refs[2:2 + n_ex]
        out_refs = refs[2 + n_ex:2 + n_ex + n_out]
        part = lax.dot_general(a_ref[...].astype(BF16), b_ref[...].astype(BF16), dims, preferred_element_type=F32)

        def finish(acc):
            outs = (acc,) if epi is None else epi(acc, *[r[...] for r in ex_refs])
            for r, o in zip(out_refs, outs):
                r[...] = o.astype(r.dtype)

        if nk == 1:
            finish(part)
            return
        acc_ref = refs[-1]
        k = pl.program_id(2)

        @pl.when(k == 0)
        def _():
            acc_ref[...] = part

        @pl.when(jnp.logical_and(k > 0, k < nk - 1))
        def _():
            acc_ref[...] += part

        @pl.when(k == nk - 1)
        def _():
            finish(acc_ref[...] + part)

    outs, comm_outs = _call(
        body, name=name, grid=(M // tm, N // tn, nk),
        in_specs=[a_spec, b_spec] + [mn_spec] * n_ex,
        out_specs=out_specs,
        out_shape=out_shapes,
        scratch_shapes=[pltpu.VMEM((tm, tn), F32)] if nk > 1 else [],
        semantics=("parallel", "parallel", "arbitrary"), args=(a, b, *extras), comm=comm)
    res = outs[0] if n_out == 1 else outs
    return res if comm is None else (res, comm_outs)


def _tiled(fn, *, T, C, ins, out_dtypes=(), acc_rows=(), tb=None, cb=512, name, comm=None):
    tb = _tile(T, tb or (512 if cb <= 1024 else 256), HALO)
    nI, nJ = T // tb, C // cb
    hb, nH = tb // HALO, T // HALO
    specs, args, kinds = [], [], []
    for kind, arr, cmap in ins:
        cm = cmap if cmap is not None else (lambda j: j)
        kinds.append(kind)
        if kind == "cur":
            specs.append(pl.BlockSpec((tb, cb), lambda j, i, cm=cm: (i, cm(j))))
            args.append(arr)
        elif kind == "ext":
            specs.append(pl.BlockSpec((HALO, cb), lambda j, i, cm=cm: (jnp.maximum(i * hb - 1, 0), cm(j))))
            specs.append(pl.BlockSpec((tb, cb), lambda j, i, cm=cm: (i, cm(j))))
            specs.append(pl.BlockSpec((HALO, cb), lambda j, i, cm=cm: (jnp.minimum((i + 1) * hb, nH - 1), cm(j))))
            args += [arr, arr, arr]
        elif kind == "row":
            specs.append(pl.BlockSpec((arr.shape[0], cb), lambda j, i, cm=cm: (0, cm(j))))
            args.append(arr)
        elif kind == "stack":
            specs.append(pl.BlockSpec((arr.shape[0], tb, cb), lambda j, i, cm=cm: (0, i, cm(j))))
            args.append(arr)
        else:
            raise ValueError(kind)
    n_in = len(args)
    n_out, n_acc = len(out_dtypes), len(acc_rows)

    def body(*refs):
        j, i = pl.program_id(0), pl.program_id(1)
        vals, r = [], 0
        for kind in kinds:
            if kind == "ext":
                prev = jnp.where(i == 0, 0.0, refs[r][...].astype(F32))
                cur = refs[r + 1][...].astype(F32)
                nxt = jnp.where(i == nI - 1, 0.0, refs[r + 2][...].astype(F32))
                vals.append(jnp.concatenate([prev, cur, nxt], axis=0))
                r += 3
            else:
                vals.append(refs[r][...])
                r += 1
        res = fn(j, i, *vals)
        for ref, o in zip(refs[n_in:n_in + n_out], res[:n_out]):
            ref[...] = o.astype(ref.dtype)
        for ref, o in zip(refs[n_in + n_out:], res[n_out:]):
            @pl.when(i == 0)
            def _(ref=ref, o=o):
                ref[...] = o

            @pl.when(i > 0)
            def _(ref=ref, o=o):
                ref[...] += o

    outs, comm_outs = _call(
        body, name=name, grid=(nJ, nI), in_specs=specs,
        out_specs=[pl.BlockSpec((tb, cb), lambda j, i: (i, j))] * n_out
        + [pl.BlockSpec((rows, cb), lambda j, i: (0, j)) for rows in acc_rows],
        out_shape=[jax.ShapeDtypeStruct((T, C), dt) for dt in out_dtypes]
        + [jax.ShapeDtypeStruct((rows, C), F32) for rows in acc_rows],
        scratch_shapes=[], semantics=("parallel", "arbitrary"), args=args, comm=comm)
    return outs if comm is None else (outs, comm_outs)


def _conv_causal(xe, w):
    K = w.shape[0]
    y = xe * w[K - 1:K]
    for j in range(K - 1):
        y = y + pltpu.roll(xe, K - 1 - j, 0) * w[j:j + 1]
    return y


def _conv_anti(de, w):
    K, n = w.shape[0], de.shape[0]
    y = de * w[K - 1:K]
    for j in range(K - 1):
        y = y + pltpu.roll(de, n - (K - 1 - j), 0) * w[j:j + 1]
    return y


def _conv_dw(dce, xe, K):
    n = dce.shape[0]
    tb = n - 2 * HALO
    rows = []
    for j in range(K):
        xs = xe if j == K - 1 else pltpu.roll(xe, K - 1 - j, 0)
        rows.append(jnp.sum((dce * xs)[HALO:HALO + tb], axis=0, keepdims=True))
    rows.append(jnp.zeros((HALO - K, dce.shape[1]), F32))
    return jnp.concatenate(rows, axis=0)


def _own(xe):
    return xe[HALO:xe.shape[0] - HALO]


def _row0(v):
    return jnp.concatenate([v, jnp.zeros((HALO - 1, v.shape[1]), F32)], axis=0)


def _per_head(fn, *xs):
    n = xs[0].shape[1] // HEAD_DIM
    outs = [fn(*[x[:, g * HEAD_DIM:(g + 1) * HEAD_DIM] for x in xs]) for g in range(n)]
    return outs[0] if n == 1 else jnp.concatenate(outs, axis=1)


def _rms_fwd(x, g, name):
    T, D = x.shape

    def fn(j, i, xv, gv):
        r = lax.rsqrt(jnp.mean(xv * xv, axis=1, keepdims=True) + EPS)
        return (xv * r * gv,)

    return _tiled(fn, T=T, C=D, ins=[("cur", x, None), ("row", g, None)], out_dtypes=[BF16], cb=D, name=name)[0]


def _rms_bwd_math(dy, xv, gv):
    r = lax.rsqrt(jnp.mean(xv * xv, axis=1, keepdims=True) + EPS)
    xh = xv * r
    dxh = dy * gv
    dx = r * (dxh - xh * jnp.mean(dxh * xh, axis=1, keepdims=True))
    dg = jnp.sum(dy * xh, axis=0, keepdims=True)
    return dx, dg


def _rms_bwd(dh, x, g, dres, name, comm=None):
    T, D = x.shape

    def fn(j, i, dhv, xv, gv, dr):
        dx, dg = _rms_bwd_math(dhv, xv, gv)
        return dr + dx, dr + dx, _row0(dg)

    return _tiled(fn, T=T, C=D, ins=[("cur", dh, None), ("cur", x, None), ("row", g, None), ("cur", dres, None)],
                  out_dtypes=[F32, BF16], acc_rows=[HALO], cb=D, name=name, comm=comm)


def _final_fb(x3, tgt, g, pp, pg):
    T, D = x3.shape

    def fn(j, i, xv, tv, gv, ppv, pgv):
        r = lax.rsqrt(jnp.mean(xv * xv, axis=1, keepdims=True) + EPS)
        xh = xv * r
        e = xh * gv - tv
        dy = e * (1.0 / D)
        dxh = dy * gv
        dx = r * (dxh - xh * jnp.mean(dxh * xh, axis=1, keepdims=True))
        dg = jnp.sum(dy * xh, axis=0, keepdims=True)
        ls = jnp.sum(e * e, axis=0, keepdims=True) * (0.5 / D)
        return (dx, dx * ppv * pgv * (1.0 - pgv), dx * pgv,
                jnp.concatenate([dg, ls, jnp.zeros((HALO - 2, D), F32)], axis=0))

    return _tiled(fn, T=T, C=D, ins=[("cur", x3, None), ("cur", tgt, None), ("row", g, None), ("cur", pp, None),
                                      ("cur", pg, None)], out_dtypes=[F32, BF16, BF16], acc_rows=[HALO], cb=D, name="final_fb")


def _ga_fwd(proj, w_a, CW, cb):
    T = proj.shape[0]
    n = CW // cb

    def fn(j, i, ax, ab, ac, w):
        c = _conv_causal(ac * ax, w)
        return (ab * _own(c),)

    return _tiled(fn, T=T, C=CW, ins=[("ext", proj, None), ("cur", proj, lambda j: j + n), ("ext", proj, lambda j: j + 2 * n),
                                       ("row", w_a, None)], out_dtypes=[BF16], cb=cb, name="ga_fwd")[0]


def _ga_bwd(dymix, proj, w_a, CW, cb):
    T = proj.shape[0]
    n = CW // cb
    K = w_a.shape[0]

    def fn(j, i, dy, ax, ab, ac, w):
        u = ac * ax
        c = _conv_causal(u, w)
        dc = dy * ab
        du = _conv_anti(dc, w)
        return _own(du * ac), _own(dy * c), _own(du * ax), _conv_dw(dc, u, K)

    return _tiled(fn, T=T, C=CW, ins=[("ext", dymix, None), ("ext", proj, None), ("ext", proj, lambda j: j + n),
                                       ("ext", proj, lambda j: j + 2 * n), ("row", w_a, None)],
                  out_dtypes=[BF16, BF16, BF16], acc_rows=[HALO], cb=cb, name="ga_bwd")


def _l2n(s):
    return s * lax.rsqrt(jnp.sum(s * s, axis=1, keepdims=True) + EPS)


def _qkv_fwd(proj, w_sec, coff, normalize, DNW, cb, name, comm=None):
    T = proj.shape[0]

    def fn(j, i, pre, w):
        c = _own(_conv_causal(pre, w))
        s = c * _sigmoid(c)
        return (_per_head(_l2n, s) if normalize else s,)

    res = _tiled(fn, T=T, C=DNW, ins=[("ext", proj, lambda j: j + coff), ("row", w_sec, None)],
                 out_dtypes=[F32], cb=cb, name=name, comm=comm)
    return res[0] if comm is None else (res[0][0], res[1])


def _qkv_bwd(dsec, proj, w_sec, coff, normalize, DNW, cb, name):
    T = proj.shape[0]
    K = w_sec.shape[0]

    def l2n_bwd(s, dn):
        r = lax.rsqrt(jnp.sum(s * s, axis=1, keepdims=True) + EPS)
        nrm = s * r
        return r * (dn - nrm * jnp.sum(dn * nrm, axis=1, keepdims=True))

    def fn(j, i, dn, pre, w):
        c = _conv_causal(pre, w)
        sg = _sigmoid(c)
        s = c * sg
        ds = _per_head(l2n_bwd, s, dn) if normalize else dn
        dc = ds * (sg * (1.0 + c * (1.0 - sg)))
        return _own(_conv_anti(dc, w)), _conv_dw(dc, pre, K)

    return _tiled(fn, T=T, C=DNW, ins=[("ext", dsec, None), ("ext", proj, lambda j: j + coff), ("row", w_sec, None)],
                  out_dtypes=[BF16], acc_rows=[HALO], cb=cb, name=name)


def _gb_fwd(small, a_log_row, dt_row, H):
    T = small.shape[0]

    def fn(j, i, sm, al, dt):
        z = sm + dt
        sp = jnp.maximum(z, 0.0) + jnp.log(1.0 + jnp.exp(-jnp.abs(z)))
        g = -jnp.exp(al) * sp
        beta = _sigmoid(pltpu.roll(sm, LANES - H, 1))
        return g, beta

    return _tiled(fn, T=T, C=LANES, ins=[("cur", small, None), ("row", a_log_row, None), ("row", dt_row, None)],
                  out_dtypes=[F32, F32], cb=LANES, name="gb_fwd")


def _gb_bwd(dgB, dbB, small, g, beta, a_log_row, dt_row, H):
    T = small.shape[0]

    def fn(j, i, dgv, dbv, sm, gv, bv, al, dt):
        lane = lax.broadcasted_iota(jnp.int32, sm.shape, 1)
        dg = jnp.zeros(sm.shape, F32)
        db = jnp.zeros(sm.shape, F32)
        for h in range(H):
            dg = jnp.where(lane == h, jnp.sum(dgv[h], axis=1, keepdims=True), dg)
            db = jnp.where(lane == h, jnp.sum(dbv[h], axis=1, keepdims=True), db)
        da = dg * (-jnp.exp(al)) * _sigmoid(sm + dt)
        dbb = db * bv * (1.0 - bv)
        dsm = jnp.where(lane < H, da, 0.0) + pltpu.roll(jnp.where(lane < H, dbb, 0.0), H, 1)
        d_alog = jnp.sum(jnp.where(lane < H, dg * gv, 0.0), axis=0, keepdims=True)
        d_dt = jnp.sum(jnp.where(lane < H, da, 0.0), axis=0, keepdims=True)
        return dsm, jnp.concatenate([d_alog, d_dt, jnp.zeros((HALO - 2, LANES), F32)], axis=0)

    return _tiled(fn, T=T, C=LANES, ins=[("stack", dgB, None), ("stack", dbB, None), ("cur", small, None), ("cur", g, None),
                                          ("cur", beta, None), ("row", a_log_row, None), ("row", dt_row, None)],
                  out_dtypes=[BF16], acc_rows=[HALO], cb=LANES, name="gb_bwd")


_DIMS = {"nn": (((1,), (0,)), ((), ())), "nt": (((1,), (1,)), ((), ())), "tn": (((0,), (0,)), ((), ()))}
_DOT_BWD = {"nn": (("nt", "gb"), ("tn", "ag")), "nt": (("nn", "gb"), ("tn", "ga")), "tn": (("nt", "bg"), ("nn", "ag"))}


def _split(a):
    hi = a.astype(BF16)
    return hi, (a - hi.astype(F32)).astype(BF16)


def _raw_dot(a, b, kind, passes):
    dg = lambda x, y: lax.dot_general(x, y, _DIMS[kind], preferred_element_type=F32)
    if passes == 1:
        return dg(a.astype(BF16), b.astype(BF16))
    ah, al = _split(a)
    bh, bl = _split(b)
    if kind == "tn":
        return dg(ah, bh) + (dg(ah, bl) + dg(al, bh))
    m = a.shape[0]
    top = dg(jnp.concatenate([ah, al], axis=0), bh)
    return top[:m] + (dg(ah, bl) + top[m:])


def _raw_dot_exact(a, b, kind, exact):
    dg = lambda x, y: lax.dot_general(x, y, _DIMS[kind], preferred_element_type=F32)
    if exact == "a":
        bh, bl = _split(b)
        return dg(a.astype(BF16), bh) + dg(a.astype(BF16), bl)
    ah, al = _split(a)
    return dg(ah, b.astype(BF16)) + dg(al, b.astype(BF16))


@functools.lru_cache(maxsize=None)
def _dotc(kind):
    @jax.custom_vjp
    def f(a, b):
        return _raw_dot_exact(a, b, kind, "a")

    def fwd(a, b):
        return _raw_dot_exact(a, b, kind, "a"), a

    def bwd(a, g):
        db = _raw_dot_exact(a, g, "tn", "a") if kind == "nn" else _raw_dot_exact(g, a, "tn", "b")
        return jnp.zeros_like(a), db

    f.defvjp(fwd, bwd)
    return f


@functools.lru_cache(maxsize=None)
def _dotf(kind, passes):
    @jax.custom_vjp
    def f(a, b):
        return _raw_dot(a, b, kind, passes)

    def fwd(a, b):
        return _raw_dot(a, b, kind, passes), (a, b)

    def bwd(res, g):
        ops = {"a": res[0], "b": res[1], "g": g}
        (ka, oa), (kb, ob) = _DOT_BWD[kind]
        return (_raw_dot(ops[oa[0]], ops[oa[1]], ka, passes), _raw_dot(ops[ob[0]], ops[ob[1]], kb, passes))

    f.defvjp(fwd, bwd)
    return f


@jax.custom_vjp
def _saved_inverse(L, inv):
    return inv


def _saved_inverse_fwd(L, inv):
    return inv, inv


def _saved_inverse_bwd(inv, g):
    d3nt, d3tn = _dotf("nt", 3), _dotf("tn", 3)
    return -d3nt(d3tn(inv, g), inv), jnp.zeros_like(inv)


_saved_inverse.defvjp(_saved_inverse_fwd, _saved_inverse_bwd)


def _chunk_fn(q, k, v, gB, bB, S, inv_saved=None):
    C = CHUNK
    d3 = _dotf("nn", 3)
    d1, d1nt, d1tn = _dotf("nn", 1), _dotf("nt", 1), _dotf("tn", 1)
    each = lambda f, *ls: tuple(f(*xs) for xs in zip(*ls))
    row = lax.broadcasted_iota(jnp.int32, (C, C), 0)
    col = lax.broadcasted_iota(jnp.int32, (C, C), 1)
    causal = row >= col
    strict = row > col
    tril = jnp.where(causal, 1.0, 0.0).astype(F32)
    eye = jnp.where(row == col, 1.0, 0.0).astype(F32)
    avg = jnp.full((C, HEAD_DIM), 1.0 / HEAD_DIM, F32)
    gc = each(lambda g: _dotc("nn")(tril, g), gB)
    R = each(lambda g: _dotc("nt")(avg, g), gc)
    decay = each(lambda g, r: jnp.where(causal, jnp.exp(jnp.where(causal, g[:, :C] - r, 0.0)), 0.0), gc, R)
    kk = each(lambda x: d1nt(x, x), k)
    L = each(lambda a, d, b: jnp.where(strict, a * d * b[:, :C], 0.0), kk, decay, bB)
    if inv_saved is None:
        inv = each(lambda l: eye - l, L)
        P = L
        for _ in range(5):
            P = each(lambda p: d3(p, p), P)
            inv = each(lambda a, p: d3(a, eye + p), inv, P)
    else:
        inv = each(_saved_inverse, L, inv_saved)
    eg = each(jnp.exp, gc)
    u = each(lambda a, x, b: d3(a, x * b), inv, v, bB)
    w = each(lambda a, x, b, e: d3(a, x * b * e), inv, k, bB, eg)
    qs = each(lambda x: x * (HEAD_DIM ** -0.5), q)
    qk = each(lambda a, x, d: d1nt(a, x) * d, qs, k, decay)
    gl = each(lambda g: g[C - 1:C, :], gc)
    kd = each(lambda x, a, g: x * jnp.exp(a - g), k, gl, gc)
    qe = each(lambda a, e: a * e, qs, eg)
    nh = len(S)
    o = ()
    for c in range(len(q) // nh):
        sl = slice(c * nh, (c + 1) * nh)
        v_new = each(lambda a, b, s: a - d1(b, s), u[sl], w[sl], S)
        o1 = each(lambda a, s: d1(a, s), qe[sl], S)
        o += each(lambda a, b, vn: a + d1(b, vn), o1, qk[sl], v_new)
        kv = each(lambda x, vn: d1tn(x, vn), kd[sl], v_new)
        S = each(lambda s, a, b: s * jnp.exp(a) + b, S, gl[sl], kv)
    return (o, S), inv


def _sel_lane(x, h):
    lane = lax.broadcasted_iota(jnp.int32, x.shape, 1)
    return jnp.broadcast_to(jnp.sum(jnp.where(lane == h, x, 0.0), axis=1, keepdims=True), x.shape)


def _tile_of(ref, c, h):
    return ref[c * CHUNK:(c + 1) * CHUNK, h * HEAD_DIM:(h + 1) * HEAD_DIM]


def _chunks_per_step(N):
    return 4 if N % 4 == 0 else (2 if N % 2 == 0 else 1)


def _delta_fwd(q, k, v, g, beta, comm=None):
    T = q.shape[0]
    H, N = q.shape[1] // HEAD_DIM, T // CHUNK
    cps = _chunks_per_step(N)
    rows = cps * CHUNK

    def body(q_ref, k_ref, v_ref, g_ref, b_ref, o_ref, s_ref, inv_ref, S):
        @pl.when(pl.program_id(0) == 0)
        def _():
            S[...] = jnp.zeros_like(S)

        gv, bv = g_ref[...], b_ref[...]
        pairs = lambda f: tuple(f(c, h) for c in range(cps) for h in range(H))
        S_in = tuple(S[h] for h in range(H))
        for h in range(H):
            s_ref[h, 0] = S_in[h]
        (o, S_new), inv = _chunk_fn(pairs(lambda c, h: _tile_of(q_ref, c, h)), pairs(lambda c, h: _tile_of(k_ref, c, h)),
                                    pairs(lambda c, h: _tile_of(v_ref, c, h)),
                                    pairs(lambda c, h: _sel_lane(gv[c * CHUNK:(c + 1) * CHUNK], h)),
                                    pairs(lambda c, h: _sel_lane(bv[c * CHUNK:(c + 1) * CHUNK], h)), S_in)
        for c in range(cps):
            for h in range(H):
                o_ref[c * CHUNK:(c + 1) * CHUNK, h * HEAD_DIM:(h + 1) * HEAD_DIM] = o[c * H + h]
                inv_ref[h, c] = inv[c * H + h]
        for h in range(H):
            S[h] = S_new[h]

    blk = pl.BlockSpec((rows, H * HEAD_DIM), lambda n: (n, 0))
    gblk = pl.BlockSpec((rows, LANES), lambda n: (n, 0))
    outs, comm_outs = _call(
        body, name="delta_fwd", grid=(N // cps,), in_specs=[blk, blk, blk, gblk, gblk],
        out_specs=[blk, pl.BlockSpec((H, 1, HEAD_DIM, HEAD_DIM), lambda n: (0, n, 0, 0)),
                   pl.BlockSpec((H, cps, CHUNK, CHUNK), lambda n: (0, n, 0, 0))],
        out_shape=[jax.ShapeDtypeStruct((T, H * HEAD_DIM), F32), jax.ShapeDtypeStruct((H, N // cps, HEAD_DIM, HEAD_DIM), F32),
                   jax.ShapeDtypeStruct((H, N, CHUNK, CHUNK), F32)],
        scratch_shapes=[pltpu.VMEM((H, HEAD_DIM, HEAD_DIM), F32)],
        semantics=("arbitrary",), args=(q, k, v, g, beta), comm=comm)
    return outs[0], outs[1], outs[2], comm_outs


def _delta_bwd(q, k, v, g, beta, S0, inv, do, comm=None):
    T = q.shape[0]
    H, N = q.shape[1] // HEAD_DIM, T // CHUNK
    cps = _chunks_per_step(N)
    rows, NS = cps * CHUNK, N // cps

    def body(q_ref, k_ref, v_ref, g_ref, b_ref, s_ref, inv_ref, do_ref, dq_ref, dk_ref, dv_ref, dg_ref, db_ref, dS):
        @pl.when(pl.program_id(0) == 0)
        def _():
            dS[...] = jnp.zeros_like(dS)

        gv, bv = g_ref[...], b_ref[...]
        pairs = lambda f: tuple(f(c, h) for c in range(cps) for h in range(H))
        heads = lambda f: tuple(f(h) for h in range(H))
        _, vjp, _ = jax.vjp(_chunk_fn, pairs(lambda c, h: _tile_of(q_ref, c, h)), pairs(lambda c, h: _tile_of(k_ref, c, h)),
                            pairs(lambda c, h: _tile_of(v_ref, c, h)),
                            pairs(lambda c, h: _sel_lane(gv[c * CHUNK:(c + 1) * CHUNK], h)),
                            pairs(lambda c, h: _sel_lane(bv[c * CHUNK:(c + 1) * CHUNK], h)),
                            heads(lambda h: s_ref[h, 0]), pairs(lambda c, h: inv_ref[h, c]), has_aux=True)
        dq, dk, dv, dgB, dbB, dS_prev, _ = vjp((pairs(lambda c, h: _tile_of(do_ref, c, h)), heads(lambda h: dS[h])))
        for c in range(cps):
            for h in range(H):
                r, sl = slice(c * CHUNK, (c + 1) * CHUNK), slice(h * HEAD_DIM, (h + 1) * HEAD_DIM)
                dq_ref[r, sl] = dq[c * H + h]
                dk_ref[r, sl] = dk[c * H + h]
                dv_ref[r, sl] = dv[c * H + h]
                dg_ref[h, r] = dgB[c * H + h]
                db_ref[h, r] = dbB[c * H + h]
        for h in range(H):
            dS[h] = dS_prev[h]

    blk = pl.BlockSpec((rows, H * HEAD_DIM), lambda n: (NS - 1 - n, 0))
    gblk = pl.BlockSpec((rows, LANES), lambda n: (NS - 1 - n, 0))
    hblk = pl.BlockSpec((H, rows, LANES), lambda n: (0, NS - 1 - n, 0))
    sd = jax.ShapeDtypeStruct
    outs, comm_outs = _call(
        body, name="delta_bwd", grid=(NS,),
        in_specs=[blk, blk, blk, gblk, gblk, pl.BlockSpec((H, 1, HEAD_DIM, HEAD_DIM), lambda n: (0, NS - 1 - n, 0, 0)),
                  pl.BlockSpec((H, cps, CHUNK, CHUNK), lambda n: (0, NS - 1 - n, 0, 0)), blk],
        out_specs=[blk, blk, blk, hblk, hblk],
        out_shape=[sd((T, H * HEAD_DIM), F32)] * 3 + [sd((H, T, LANES), F32)] * 2,
        scratch_shapes=[pltpu.VMEM((H, HEAD_DIM, HEAD_DIM), F32)],
        semantics=("arbitrary",), args=(q, k, v, g, beta, S0, inv, do), comm=comm)
    return (*outs, comm_outs)


def _gnorm_fwd(o, proj, z_coff, gdn_t, DNW):
    T = o.shape[0]

    def fn(j, i, ov, zv, gv):
        def one(oh, zh, gh):
            r = lax.rsqrt(jnp.mean(oh * oh, axis=1, keepdims=True) + EPS)
            return oh * r * gh * (zh * _sigmoid(zh))
        return (_per_head(one, ov, zv, jnp.broadcast_to(gv, ov.shape)),)

    return _tiled(fn, T=T, C=DNW, ins=[("cur", o, None), ("cur", proj, lambda j: j + z_coff), ("row", gdn_t, None)],
                  out_dtypes=[BF16], cb=DNW, name="gnorm_fwd")[0]


def _gnorm_bwd(dymix, y_coff, o, proj, z_coff, gdn_t, DNW):
    T = o.shape[0]
    nh = DNW // HEAD_DIM

    def fn(j, i, dy, ov, zv, gv):
        dos, dzs, dgs = [], [], jnp.zeros((1, HEAD_DIM), F32)
        for h in range(nh):
            sl = slice(h * HEAD_DIM, (h + 1) * HEAD_DIM)
            dyh, oh, zh, gh = dy[:, sl].astype(F32), ov[:, sl], zv[:, sl], gv[:, sl]
            r = lax.rsqrt(jnp.mean(oh * oh, axis=1, keepdims=True) + EPS)
            on = oh * r
            sg = _sigmoid(zh)
            sz = zh * sg
            dzs.append(dyh * on * gh * (sg * (1.0 + zh * (1.0 - sg))))
            don = dyh * gh * sz
            dos.append(r * (don - on * jnp.mean(don * on, axis=1, keepdims=True)))
            dgs = dgs + jnp.sum(dyh * on * sz, axis=0, keepdims=True)
        cat = (lambda xs: xs[0] if nh == 1 else jnp.concatenate(xs, axis=1))
        return cat(dos), cat(dzs), _row0(dgs)

    T_ = T
    nI = T_ // _tile(T_, 256, HALO)
    tb = T_ // nI
    specs_cb = DNW

    def body_wrap():
        def body(dy_ref, o_ref, z_ref, g_ref, do_ref, dz_ref, dg_ref):
            i = pl.program_id(0)
            d_o, d_z, d_g = fn(0, i, dy_ref[...], o_ref[...], z_ref[...], g_ref[...])
            do_ref[...] = d_o
            dz_ref[...] = d_z.astype(dz_ref.dtype)

            @pl.when(i == 0)
            def _():
                dg_ref[...] = d_g

            @pl.when(i > 0)
            def _():
                dg_ref[...] += d_g

        return pl.pallas_call(
            body, name="gnorm_bwd", grid=(nI,),
            in_specs=[pl.BlockSpec((tb, specs_cb), lambda i: (i, y_coff)), pl.BlockSpec((tb, specs_cb), lambda i: (i, 0)),
                      pl.BlockSpec((tb, specs_cb), lambda i: (i, z_coff)), pl.BlockSpec((1, specs_cb), lambda i: (0, 0))],
            out_specs=[pl.BlockSpec((tb, specs_cb), lambda i: (i, 0)), pl.BlockSpec((tb, specs_cb), lambda i: (i, 0)),
                       pl.BlockSpec((HALO, HEAD_DIM), lambda i: (0, 0))],
            out_shape=[jax.ShapeDtypeStruct((T_, DNW), F32), jax.ShapeDtypeStruct((T_, DNW), BF16),
                       jax.ShapeDtypeStruct((HALO, HEAD_DIM), F32)],
            compiler_params=pltpu.CompilerParams(dimension_semantics=("arbitrary",), vmem_limit_bytes=VMEM_LIMIT),
        )(dymix, o, proj, gdn_t)

    return body_wrap()


def _ffn_fwd(up_g, up_v, w_g, w_v, cb):
    T, F = up_g.shape

    def fn(j, i, ug, uv, wg, wv):
        cg = _own(_conv_causal(ug, wg))
        cv = _own(_conv_causal(uv, wv))
        return (cg * _sigmoid(cg) * cv,)

    return _tiled(fn, T=T, C=F, ins=[("ext", up_g, None), ("ext", up_v, None), ("row", w_g, None), ("row", w_v, None)],
                  out_dtypes=[BF16], tb=1024, cb=cb, name="ffn_fwd")[0]


def _ffn_bwd(dact, up_g, up_v, w_g, w_v, cb):
    T, F = up_g.shape
    K = w_g.shape[0]

    def fn(j, i, da, ug, uv, wg, wv):
        cg = _conv_causal(ug, wg)
        cv = _conv_causal(uv, wv)
        sg = _sigmoid(cg)
        dgate = da * cv * (sg * (1.0 + cg * (1.0 - sg)))
        dval = da * (cg * sg)
        return (_own(_conv_anti(dgate, wg)), _own(_conv_anti(dval, wv)), _conv_dw(dgate, ug, K), _conv_dw(dval, uv, K))

    return _tiled(fn, T=T, C=F, ins=[("ext", dact, None), ("ext", up_g, None), ("ext", up_v, None), ("row", w_g, None),
                                      ("row", w_v, None)], out_dtypes=[BF16, BF16], acc_rows=[HALO, HALO], tb=1024, cb=cb,
                  name="ffn_bwd")


def _wide(R, Cc, n_f32, unit=HALO):
    cb = Cc if (Cc % LANES or Cc <= 4096) else _tile(Cc, 2048, LANES)
    cap = max(unit, EW_VMEM_BUDGET // (2 * 4 * n_f32 * cb) // unit * unit)
    return _tile(R, cap, unit), cb


def _adamw(w, g, m, v, name, comm=None):
    R, Cc = w.shape
    tb, cb = _wide(R, Cc, 7) if R % HALO == 0 else (R, _tile(Cc, EW_VMEM_BUDGET // (2 * 4 * 7 * R) // LANES * LANES, LANES))
    c1 = 1.0 / (1.0 - ADAM_B1 ** ADAM_STEP)
    c2 = 1.0 / (1.0 - ADAM_B2 ** ADAM_STEP)

    def fn(j, i, wv, gv, mv, vv):
        m2 = ADAM_B1 * mv + (1.0 - ADAM_B1) * gv
        v2 = ADAM_B2 * vv + (1.0 - ADAM_B2) * (gv * gv)
        delta = -ADAM_LR * ((m2 * c1) / (jnp.sqrt(v2 * c2) + ADAM_EPS) + ADAM_WD * wv)
        return delta, m2, v2

    return _tiled(fn, T=R, C=Cc, ins=[("cur", w, None), ("cur", g, None), ("cur", m, None), ("cur", v, None)],
                  out_dtypes=[F32, F32, F32], tb=tb, cb=cb, name=name, comm=comm)


def _join_shards(w4, n_main):
    S4, R, cs = w4.shape
    n_small = S4 * cs - n_main
    assert 0 < n_small <= LANES and n_small <= cs
    tb = _tile(R, 256, 2 * HALO)

    def body(w_ref, main_ref, small_ref):
        for t in range(S4 - 1):
            main_ref[:, t * cs:(t + 1) * cs] = w_ref[t]
        last = w_ref[S4 - 1]
        main_ref[:, (S4 - 1) * cs:] = last[:, :cs - n_small]
        small_ref[...] = jnp.zeros_like(small_ref)
        small_ref[:, :n_small] = last[:, cs - n_small:]

    return pl.pallas_call(
        body, name="join_w_in", grid=(R // tb,), in_specs=[pl.BlockSpec((S4, tb, cs), lambda i: (0, i, 0))],
        out_specs=[pl.BlockSpec((tb, n_main), lambda i: (i, 0)), pl.BlockSpec((tb, LANES), lambda i: (i, 0))],
        out_shape=[jax.ShapeDtypeStruct((R, n_main), w4.dtype), jax.ShapeDtypeStruct((R, LANES), w4.dtype)],
        compiler_params=pltpu.CompilerParams(dimension_semantics=("parallel",), vmem_limit_bytes=VMEM_LIMIT))(w4)


def _split_shards(main, small, cs):
    R, n_main = main.shape
    n_small = 4 * cs - n_main
    tb = _tile(R, 256, HALO)

    def body(main_ref, small_ref, out_ref):
        for t in range(3):
            out_ref[t] = main_ref[:, t * cs:(t + 1) * cs]
        out_ref[3, :, :cs - n_small] = main_ref[:, 3 * cs:]
        out_ref[3, :, cs - n_small:] = small_ref[:, :n_small]

    return pl.pallas_call(
        body, name="split_g_in", grid=(R // tb,),
        in_specs=[pl.BlockSpec((tb, n_main), lambda i: (i, 0)), pl.BlockSpec((tb, LANES), lambda i: (i, 0))],
        out_specs=pl.BlockSpec((4, tb, cs), lambda i: (0, i, 0)), out_shape=jax.ShapeDtypeStruct((4, R, cs), main.dtype),
        compiler_params=pltpu.CompilerParams(dimension_semantics=("parallel",), vmem_limit_bytes=VMEM_LIMIT))(main, small)


def _sum_stack(st, name):
    S, R, Cc = st.shape
    cb = _tile(Cc, 512, LANES) if Cc % LANES == 0 else Cc

    def fn(j, i, sv):
        t = sv[0]
        for s in range(1, S):
            t = t + sv[s]
        return (t,)

    return _tiled(fn, T=R, C=Cc, ins=[("stack", st, None)], out_dtypes=[F32], cb=cb, name=name)[0]


ANY = pl.BlockSpec(memory_space=pl.ANY)


def _place():
    x, y, c = lax.axis_index("x"), lax.axis_index("y"), lax.axis_index("c")
    return x, y, c, 2 * x + y


def _chip_dev(s, c):
    return (s // 2, s % 2, c)


class _Comm:
    def __init__(self, ins, out_shapes, sems, start, wait, aliases=None):
        self.ins, self.out_shapes, self.sems = list(ins), list(out_shapes), list(sems)
        self.start, self.wait, self.aliases = start, wait, dict(aliases or {})


def _merge(*comms):
    offs, i, o, s = [], 0, 0, 0
    for cm in comms:
        offs.append((i, o, s))
        i, o, s = i + len(cm.ins), o + len(cm.out_shapes), s + len(cm.sems)

    def part(refs, k, cm):
        i0, o0, s0 = offs[k]
        return refs[0][i0:i0 + len(cm.ins)], refs[1][o0:o0 + len(cm.out_shapes)], refs[2][s0:s0 + len(cm.sems)]

    def start(*refs):
        for k, cm in enumerate(comms):
            cm.start(*part(refs, k, cm))

    def wait(*refs):
        for k, cm in enumerate(comms):
            cm.wait(*part(refs, k, cm))

    aliases = {}
    for k, cm in enumerate(comms):
        for a, b in cm.aliases.items():
            aliases[offs[k][0] + a] = offs[k][1] + b
    return _Comm([a for cm in comms for a in cm.ins], [a for cm in comms for a in cm.out_shapes],
                 [a for cm in comms for a in cm.sems], start, wait, aliases)


def _call(body, *, name, grid, in_specs, out_specs, out_shape, scratch_shapes, semantics, args, comm=None):
    if comm is None:
        outs = pl.pallas_call(
            body, name=name, grid=grid, in_specs=in_specs, out_specs=out_specs, out_shape=out_shape,
            scratch_shapes=list(scratch_shapes),
            compiler_params=pltpu.CompilerParams(dimension_semantics=semantics, vmem_limit_bytes=VMEM_LIMIT))(*args)
        return list(outs), []
    n_in, n_out, n_scr = len(in_specs), len(out_specs), len(scratch_shapes)
    ci, co = len(comm.ins), len(comm.out_shapes)

    def wrapped(*refs):
        r = 0
        ins, r = refs[r:r + n_in], r + n_in
        cins, r = refs[r:r + ci], r + ci
        outs, r = refs[r:r + n_out], r + n_out
        couts, r = refs[r:r + co], r + co
        scr, r = refs[r:r + n_scr], r + n_scr
        csems = refs[r:]
        ids = [pl.program_id(a) for a in range(len(grid))]
        first, last = ids[0] == 0, ids[0] == grid[0] - 1
        for a in range(1, len(grid)):
            first = jnp.logical_and(first, ids[a] == 0)
            last = jnp.logical_and(last, ids[a] == grid[a] - 1)

        @pl.when(first)
        def _():
            comm.start(cins, couts, csems)

        body(*ins, *outs, *scr)

        @pl.when(last)
        def _():
            comm.wait(cins, couts, csems)

    outs = pl.pallas_call(
        wrapped, name=name, grid=grid, in_specs=list(in_specs) + [ANY] * ci, out_specs=list(out_specs) + [ANY] * co,
        out_shape=list(out_shape) + comm.out_shapes, scratch_shapes=list(scratch_shapes) + comm.sems,
        input_output_aliases={n_in + a: n_out + b for a, b in comm.aliases.items()},
        compiler_params=pltpu.CompilerParams(dimension_semantics=("arbitrary",) * len(grid), vmem_limit_bytes=VMEM_LIMIT),
    )(*args, *comm.ins)
    return list(outs[:n_out]), list(outs[n_out:])


def _run_comm(comm, name):
    ci, co = len(comm.ins), len(comm.out_shapes)

    def body(*refs):
        cins, couts, csems = refs[:ci], refs[ci:ci + co], refs[ci + co:]
        comm.start(cins, couts, csems)
        comm.wait(cins, couts, csems)

    outs = pl.pallas_call(body, name=name, in_specs=[ANY] * ci, out_specs=[ANY] * co, out_shape=comm.out_shapes,
                          scratch_shapes=comm.sems, input_output_aliases=comm.aliases)(*comm.ins)
    return list(outs)


def _ag_comm(shard, land=None, q=0, nq=1):
    two, R2, Cc = shard.shape
    rows = pl.ds(q * (R2 // nq), R2 // nq)
    DMA = pltpu.SemaphoreType.DMA

    def copies(ins, outs, sems, which):
        sh, out = ins[0], outs[0]
        send1, recv1, send2, recv2, send0, recv0 = sems
        x, y, c, s = _place()
        sib = (x, y, 1 - c)
        rc = pltpu.make_async_remote_copy
        if which == "first":
            return [rc(sh.at[c, rows], out.at[s, c, rows], send1.at[m - 1], recv1.at[m - 1],
                       device_id=_chip_dev(s ^ m, c), device_id_type=MESH) for m in range(1, 4)]
        if which == "own":
            return [rc(sh.at[h, rows], out.at[s, h, rows], send0.at[h], recv0.at[h], device_id=sib, device_id_type=MESH)
                    for h in range(2)]
        if which == "landed":
            return [rc(sh.at[c, rows], out.at[s ^ m, c, rows], send1.at[m - 1], recv1.at[m - 1], device_id=sib,
                       device_id_type=MESH) for m in range(1, 4)]
        half = c if which == "passed" else 1 - c
        return [rc(out.at[s ^ m, half, rows], out.at[s ^ m, half, rows], send2.at[m - 1], recv2.at[m - 1], device_id=sib,
                   device_id_type=MESH) for m in range(1, 4)]

    def start(ins, outs, sems):
        for cp in copies(ins, outs, sems, "first") + copies(ins, outs, sems, "own"):
            cp.start()

    def wait(ins, outs, sems):
        passed = copies(ins, outs, sems, "passed")
        for lan, pas in zip(copies(ins, outs, sems, "landed"), passed):
            lan.wait_recv()
            pas.start()
        for cp in copies(ins, outs, sems, "handed"):
            cp.wait_recv()
        for cp in copies(ins, outs, sems, "own"):
            cp.wait()
        for cp in copies(ins, outs, sems, "first") + passed:
            cp.wait_send()

    return _Comm([shard] + ([land] if land is not None else []), [jax.ShapeDtypeStruct((4, two, R2, Cc), shard.dtype)],
                 [DMA((3,)), DMA((3,)), DMA((3,)), DMA((3,)), DMA((2,)), DMA((2,))], start, wait,
                 {1: 0} if land is not None else None)


def _a2a_comm(S1, q=0, nq=1, land=None, cnt=1):
    S4, R2, Cc = S1.shape
    rows = pl.ds(q * (R2 // nq), cnt * (R2 // nq))
    DMA = pltpu.SemaphoreType.DMA

    def copies(ins, outs, sems):
        x, y, c, s = _place()
        return [pltpu.make_async_remote_copy(ins[0].at[s ^ m, rows], outs[0].at[m - 1, rows], sems[0].at[m - 1],
                                             sems[1].at[m - 1], device_id=_chip_dev(s ^ m, c), device_id_type=MESH)
                for m in range(1, 4)]

    def start(ins, outs, sems):
        for cp in copies(ins, outs, sems):
            cp.start()

    def wait(ins, outs, sems):
        for cp in copies(ins, outs, sems):
            cp.wait()

    return _Comm([S1] + ([land] if land is not None else []), [jax.ShapeDtypeStruct((3, R2, Cc), S1.dtype)],
                 [DMA((3,)), DMA((3,))], start, wait, {1: 0} if land is not None else None)


def _halves(G):
    return G.reshape(G.shape[0], 2, G.shape[1] // 2, G.shape[2])


def _swap_comm(piece):
    n, two, R2, Cc = piece.shape
    DMA = pltpu.SemaphoreType.DMA

    def copies(ins, outs, sems):
        x, y, c, s = _place()
        return [pltpu.make_async_remote_copy(ins[0].at[t, 1 - c], outs[0].at[t], sems[0].at[t], sems[1].at[t],
                                             device_id=(x, y, 1 - c), device_id_type=MESH) for t in range(n)]

    def start(ins, outs, sems):
        for cp in copies(ins, outs, sems):
            cp.start()

    def wait(ins, outs, sems):
        for cp in copies(ins, outs, sems):
            cp.wait()

    return _Comm([piece], [jax.ShapeDtypeStruct((n, R2, Cc), piece.dtype)], [DMA((n,)), DMA((n,))], start, wait)


def _add_half(pieces, As, cidx, name):
    R2, Cc = pieces[0].shape[2:]
    S4 = sum(pc.shape[0] for pc in pieces)
    tb, cb = _wide(R2, Cc, 3, 2 * HALO)
    nI, nJ = R2 // tb, Cc // cb

    def body(c_ref, g_ref, a_ref, *rest):
        rest[-1][...] = (g_ref[0, 0] + a_ref[0]).astype(BF16)

    out, t0 = None, 0
    for k, (pc, A) in enumerate(zip(pieces, As)):
        grid_spec = pltpu.PrefetchScalarGridSpec(
            num_scalar_prefetch=1, grid=(pc.shape[0], nI, nJ),
            in_specs=[pl.BlockSpec((1, 1, tb, cb), lambda t, i, j, c_ref: (t, c_ref[0], i, j)),
                      pl.BlockSpec((1, tb, cb), lambda t, i, j, c_ref: (t, i, j))] + ([ANY] if k else []),
            out_specs=pl.BlockSpec((tb, cb), lambda t, i, j, c_ref, t0=t0: ((t0 + t) * nI + i, j)))
        out = pl.pallas_call(
            functools.partial(body), name=f"{name}{k}", grid_spec=grid_spec, out_shape=jax.ShapeDtypeStruct((S4 * R2, Cc), BF16),
            input_output_aliases={3: 0} if k else {},
            compiler_params=pltpu.CompilerParams(dimension_semantics=("parallel", "parallel", "parallel"),
                                                 vmem_limit_bytes=VMEM_LIMIT),
        )(*((cidx, pc, A) + ((out,) if k else ())))
        t0 += pc.shape[0]
    return out.reshape(S4, R2, Cc)


def _add_own(S1, B, chip_idx, cidx, name):
    S4, R2, Cc = S1.shape
    tb, cb = _wide(R2, Cc, 3, 2 * HALO)

    def body(s_idx, c_idx, s_ref, b_ref, o_ref):
        o_ref[...] = ((s_ref[0].astype(F32) + b_ref[0].astype(F32)) + b_ref[1].astype(F32)) + b_ref[2].astype(F32)

    grid_spec = pltpu.PrefetchScalarGridSpec(
        num_scalar_prefetch=2, grid=(R2 // tb, Cc // cb),
        in_specs=[pl.BlockSpec((1, tb, cb), lambda i, j, s_idx, c_idx: (s_idx[0], i, j)),
                  pl.BlockSpec((3, tb, cb), lambda i, j, s_idx, c_idx: (0, i, j))],
        out_specs=pl.BlockSpec((None, tb, cb), lambda i, j, s_idx, c_idx: (c_idx[0], i, j)))
    return pl.pallas_call(body, name=name, grid_spec=grid_spec, out_shape=jax.ShapeDtypeStruct((2, R2, Cc), F32),
                          compiler_params=pltpu.CompilerParams(dimension_semantics=("parallel", "parallel"),
                                                               vmem_limit_bytes=VMEM_LIMIT))(chip_idx, cidx, S1, B)


def _fill_comm(Hs):
    def copy(ins, outs, sems):
        x, y, c, s = _place()
        return pltpu.make_async_remote_copy(ins[0].at[c], outs[0].at[c], sems[0], sems[1], device_id=(x, y, 1 - c),
                                            device_id_type=MESH)

    return _Comm([Hs], [jax.ShapeDtypeStruct(Hs.shape, Hs.dtype)], [pltpu.SemaphoreType.DMA, pltpu.SemaphoreType.DMA],
                 lambda *r: copy(*r).start(), lambda *r: copy(*r).wait(), {0: 0})


def _gather_all(buf, name):
    R, Cc = buf.shape

    def body(b_ref, out_ref, send, recv, local):
        x, y, c, s = _place()
        d = 2 * s + c
        mine = pltpu.make_async_copy(b_ref, out_ref.at[d], local)
        mine.start()
        cps = []
        for m in range(1, 8):
            t = d ^ m
            cp = pltpu.make_async_remote_copy(b_ref, out_ref.at[d], send.at[m - 1], recv.at[m - 1],
                                              device_id=(t // 4, (t // 2) % 2, t % 2), device_id_type=MESH)
            cp.start()
            cps.append(cp)
        for cp in cps:
            cp.wait()
        mine.wait()

    return pl.pallas_call(
        body, name=name, in_specs=[ANY], out_specs=ANY, out_shape=jax.ShapeDtypeStruct((8, R, Cc), buf.dtype),
        scratch_shapes=[pltpu.SemaphoreType.DMA((7,)), pltpu.SemaphoreType.DMA((7,)), pltpu.SemaphoreType.DMA],
    )(buf)


def _pack_rows(vs):
    flat = jnp.concatenate([v.reshape(-1) for v in vs])
    n = flat.shape[0]
    rows = -(-n // (LANES * 2 * HALO)) * 2 * HALO
    return jnp.pad(flat, (0, rows * LANES - n)).reshape(rows, LANES)


def _unpack_rows(buf, shapes):
    flat = buf.reshape(-1)
    outs, o = [], 0
    for shp in shapes:
        n = 1
        for d in shp:
            n *= d
        outs.append(flat[o:o + n].reshape(shp))
        o += n
    return outs


def kernel(x, p, norm_mix_g, w_in, conv_a_w, conv_qkv_w, a_log, dt_bias, dn_norm_g, w_out, norm_ffn_g, w_up, conv_ffn_w, w_down, norm_ple_g, w_ple_gate, w_ple_proj, final_norm_g, loss_target, m_norm_mix_g, m_w_in, m_conv_a_w, m_conv_qkv_w, m_a_log, m_dt_bias, m_dn_norm_g, m_w_out, m_norm_ffn_g, m_w_up, m_conv_ffn_w, m_w_down, m_norm_ple_g, m_w_ple_gate, m_w_ple_proj, m_final_norm_g, v_norm_mix_g, v_w_in, v_conv_a_w, v_conv_qkv_w, v_a_log, v_dt_bias, v_dn_norm_g, v_w_out, v_norm_ffn_g, v_w_up, v_conv_ffn_w, v_w_down, v_norm_ple_g, v_w_ple_gate, v_w_ple_proj, v_final_norm_g):
    xs = x[0]
    ps = p[0, 0]
    tgt = loss_target[0]
    T, D = xs.shape
    H = a_log.shape[-1]
    DNW = H * HEAD_DIM
    CW = conv_a_w.shape[-1] * 4
    F = w_down.shape[1] * 4
    PD = ps.shape[-1]
    IN_MAIN = 3 * CW + 4 * DNW
    IN_COLS = IN_MAIN + 2 * H
    assert w_in.shape[-1] * 4 == IN_COLS and CW + DNW == D and 2 * H <= LANES
    cb = _tile(min(CW, DNW), 512, LANES)
    while F % cb:
        cb -= LANES
    cidx = lax.axis_index("c").astype(jnp.int32).reshape(1)
    chip = 2 * lax.axis_index("x") + lax.axis_index("y")

    def halves(w):
        sh = w[0].astype(BF16)
        return sh.reshape(2, sh.shape[0] // 2, sh.shape[1])

    def whole(land):
        return land.reshape(4, 2 * land.shape[2], land.shape[3])

    def rows(g4):
        return g4.reshape(4 * g4.shape[1], g4.shape[2])

    conv_shapes = [conv_a_w[0].shape, conv_qkv_w[0].shape, conv_ffn_w[0].shape]
    cpack = _pack_rows([conv_a_w[0], conv_qkv_w[0], conv_ffn_w[0]])
    sh_in, sh_out, sh_up, sh_down, sh_pg, sh_pp = (halves(w) for w in (w_in, w_out, w_up, w_down, w_ple_gate, w_ple_proj))
    l_in, cg = _run_comm(_merge(_ag_comm(sh_in), _ag_comm(cpack.reshape(2, cpack.shape[0] // 2, LANES))), "ag_w_in_conv")
    w_in_main, w_in_small = _join_shards(whole(l_in), IN_MAIN)
    cg = cg.reshape(4, cpack.shape[0], LANES)
    parts = [_unpack_rows(cg[t], conv_shapes) for t in range(4)]
    cw_a = jnp.concatenate([parts[t][0] for t in range(4)], axis=1)
    cw_qkv = jnp.concatenate([parts[t][1] for t in range(4)], axis=1)
    cw_ffn = jnp.concatenate([parts[t][2] for t in range(4)], axis=1)
    cw_q, cw_k, cw_v = cw_qkv[:, :DNW], cw_qkv[:, DNW:2 * DNW], cw_qkv[:, 2 * DNW:]
    cw_fg, cw_fv = cw_ffn[:, :F], cw_ffn[:, F:]
    pad_row = lambda v: jnp.pad(v, ((0, 0), (0, LANES - v.shape[1])))
    a_log_row, dt_row = pad_row(a_log), pad_row(dt_bias)
    gdn_t = jnp.tile(dn_norm_g, (1, H))
    gfin = final_norm_g.reshape(1, D)

    h1 = _rms_fwd(xs, norm_mix_g, "rms1")
    proj, (l_up,) = _mm(h1, w_in_main, mode="nn", out_dtypes=[F32], name="mm_proj", comm=_ag_comm(sh_up, q=0, nq=2))
    small = _mm(h1, w_in_small, mode="nn", out_dtypes=[F32], name="mm_small")
    ya = _ga_fwd(proj, cw_a, CW, cb)
    nq = 3 * CW // cb
    nd = DNW // cb
    qn, (l_out,) = _qkv_fwd(proj, cw_q, nq, True, DNW, cb, "q_fwd", comm=_ag_comm(sh_out, q=0, nq=2))
    kn, (l_out,) = _qkv_fwd(proj, cw_k, nq + nd, True, DNW, cb, "k_fwd", comm=_ag_comm(sh_out, l_out, q=1, nq=2))
    vs = _qkv_fwd(proj, cw_v, nq + 2 * nd, False, DNW, cb, "v_fwd")
    g, beta = _gb_fwd(small, a_log_row, dt_row, H)
    o, S0, inv_c, (l_up,) = _delta_fwd(qn, kn, vs, g, beta, comm=_ag_comm(sh_up, l_up, q=1, nq=2))
    w_out_f = rows(whole(l_out))
    w_out_a, w_out_b = w_out_f[:CW], w_out_f[CW:]
    w_up_4 = whole(l_up)
    z_coff = (3 * CW + 3 * DNW) // DNW
    assert (3 * CW + 3 * DNW) % DNW == 0 and CW % DNW == 0
    yb = _gnorm_fwd(o, proj, z_coff, gdn_t, DNW)
    add = lambda acc, r: (r + acc,)
    x1 = _mm(ya, w_out_a, mode="nn", out_dtypes=[F32], epi=add, extras=[xs], name="mm_out_a")
    x1 = _mm(yb, w_out_b, mode="nn", out_dtypes=[F32], epi=add, extras=[x1], name="mm_out_b")
    h2 = _rms_fwd(x1, norm_ffn_g, "rms2")
    up_g, (l_down,) = _mm(h2, w_up_4, mode="nn", b_split=(0, 2), out_dtypes=[F32], name="mm_up_g",
                          comm=_ag_comm(sh_down, q=0, nq=2))
    up_v, (l_down,) = _mm(h2, w_up_4, mode="nn", b_split=(2, 2), out_dtypes=[F32], name="mm_up_v",
                          comm=_ag_comm(sh_down, l_down, q=1, nq=2))
    w_down_f = rows(whole(l_down))
    act = _ffn_fwd(up_g, up_v, cw_fg, cw_fv, cb)
    x2, (l_pg, l_pp) = _mm(act, w_down_f, mode="nn", out_dtypes=[F32], epi=add, extras=[x1], name="mm_down",
                           comm=_merge(_ag_comm(sh_pg), _ag_comm(sh_pp)))
    w_pg_f = rows(whole(l_pg))
    w_pp_4 = whole(l_pp)
    h3 = _rms_fwd(x2, norm_ple_g, "rms3")
    pp = _mm(ps, w_pp_4, mode="nn", b_split=(0, 4), out_dtypes=[F32], name="mm_pp")

    def ple_epi(acc, x2v, ppv):
        pg = _sigmoid(acc)
        return x2v + pg * ppv, pg

    x3, pg = _mm(h3, w_pg_f, mode="nn", out_dtypes=[F32, F32], epi=ple_epi, extras=[x2, pp], name="mm_pg")

    dx3, dpg, dpp, fin = _final_fb(x3, tgt, gfin, pp, pg)
    loss = lax.psum(jnp.sum(fin[1]), ("x", "y", "c"))
    d_gfin = fin[0:1]
    def split_rows(dW):
        return dW.reshape(4, dW.shape[0] // 4, dW.shape[1])

    chip_idx = chip.astype(jnp.int32).reshape(1)
    own_sum = lambda S1, B, name: _add_own(S1, B, chip_idx, cidx, "rs_" + name + "_sum")

    dW_pp = _mm(ps, dpp, mode="tn", out_split=4, out_dtypes=[F32], name="mm_dw_pp")
    dW_pg = _mm(h3, dpg, mode="tn", out_dtypes=[F32], name="mm_dw_pg")
    P_pp, P_pg = _halves(dW_pp), _halves(split_rows(dW_pg))
    dh3, (A_pp, A_pg) = _mm(dpg, w_pg_f, mode="nt", out_dtypes=[F32], name="mm_dh3",
                            comm=_merge(_swap_comm(P_pp), _swap_comm(P_pg)))
    S_pp = _add_half([P_pp], [A_pp], cidx, "rs_w_pp_add")
    S_pg = _add_half([P_pg], [A_pg], cidx, "rs_w_pg_add")
    dx2, dx2_b, d_gple = _rms_bwd(dh3, x2, norm_ple_g, dx3, "rms3_bwd")
    dW_down, (B_pp, B_pg) = _mm(act, dx2_b, mode="tn", out_dtypes=[F32], name="mm_dw_down",
                                comm=_merge(_a2a_comm(S_pp), _a2a_comm(S_pg)))
    P_down = _halves(split_rows(dW_down))
    dact, (A_down, F_pp, F_pg) = _mm(dx2_b, w_down_f, mode="nt", out_dtypes=[F32], name="mm_dact", comm=_merge(
        _swap_comm(P_down), _fill_comm(own_sum(S_pp, B_pp, "w_pp")), _fill_comm(own_sum(S_pg, B_pg, "w_pg"))))
    S_down = _add_half([P_down], [A_down], cidx, "rs_w_down_add")
    dup_g, dup_v, dcw_fg, dcw_fv = _ffn_bwd(dact, up_g, up_v, cw_fg, cw_fv, cb)
    dW_up_g, (B_down,) = _mm(h2, dup_g, mode="tn", out_split=2, out_dtypes=[F32], name="mm_dw_up_g", comm=_a2a_comm(S_down))
    P_ug = _halves(dW_up_g)
    dW_up_v, (A_ug, F_down) = _mm(h2, dup_v, mode="tn", out_split=2, out_dtypes=[F32], name="mm_dw_up_v",
                                  comm=_merge(_swap_comm(P_ug), _fill_comm(own_sum(S_down, B_down, "w_down"))))
    P_uv = _halves(dW_up_v)
    dh2, (A_uv,) = _mm(dup_g, w_up_4, mode="nt", b_split=(0, 2), out_dtypes=[F32], name="mm_dh2_g", comm=_swap_comm(P_uv))
    S_up = _add_half([P_ug, P_uv], [A_ug, A_uv], cidx, "rs_w_up_add")
    dh2, (B_up,) = _mm(dup_v, w_up_4, mode="nt", b_split=(2, 2), out_dtypes=[F32], epi=add, extras=[dh2], name="mm_dh2_v",
                       comm=_a2a_comm(S_up, 0, 2))
    dx1, dx1_b, d_gffn = _rms_bwd(dh2, x1, norm_ffn_g, dx2, "rms2_bwd")
    dW_out_a = _mm(ya, dx1_b, mode="tn", out_dtypes=[F32], name="mm_dw_out_a")
    dW_out_b = _mm(yb, dx1_b, mode="tn", out_dtypes=[F32], name="mm_dw_out_b")
    P_oa, P_ob = _halves(dW_out_a.reshape(-1, D // 4, D)), _halves(dW_out_b.reshape(-1, D // 4, D))
    dymix, (A_oa, A_ob) = _mm(dx1_b, w_out_f, mode="nt", out_dtypes=[F32], name="mm_dymix",
                              comm=_merge(_swap_comm(P_oa), _swap_comm(P_ob)))
    S_out = _add_half([P_oa, P_ob], [A_oa, A_ob], cidx, "rs_w_out_add")
    dax, dab, dac, dcw_a = _ga_bwd(dymix, proj, cw_a, CW, cb)
    do, dz, d_gdn = _gnorm_bwd(dymix, CW // DNW, o, proj, z_coff, gdn_t, DNW)
    dqn, dkn, dvs, dgB, dbB, (B_up, B_out) = _delta_bwd(qn, kn, vs, g, beta, S0, inv_c, do,
                                                        comm=_merge(_a2a_comm(S_up, 1, 2, B_up), _a2a_comm(S_out)))
    dq_pre, dcw_q = _qkv_bwd(dqn, proj, cw_q, nq, True, DNW, cb, "q_bwd")
    dk_pre, dcw_k = _qkv_bwd(dkn, proj, cw_k, nq + nd, True, DNW, cb, "k_bwd")
    dv_pre, dcw_v = _qkv_bwd(dvs, proj, cw_v, nq + 2 * nd, False, DNW, cb, "v_bwd")
    dsmall, d_ab = _gb_bwd(dgB, dbB, small, g, beta, a_log_row, dt_row, H)
    dproj = jnp.concatenate([dax, dab, dac, dq_pre, dk_pre, dv_pre, dz], axis=1)
    dW_in_main, (F_up, F_out) = _mm(h1, dproj, mode="tn", out_dtypes=[F32], name="mm_dw_in", comm=_merge(
        _fill_comm(own_sum(S_up, B_up, "w_up")), _fill_comm(own_sum(S_out, B_out, "w_out"))))
    dW_in_small = _mm(h1, dsmall, mode="tn", out_dtypes=[F32], name="mm_dw_in_small")
    def update(Hf, w, m, v, name):
        gr = Hf.reshape(2 * Hf.shape[1], Hf.shape[2])
        if gr.shape[1] % LANES == 0:
            delta, m2, v2 = _adamw(w[0], gr, m[0], v[0], "adamw_" + name)
            return gr[None], delta[None], m2[None], v2[None]
        tr = jnp.transpose
        grt = tr(gr)
        delta, m2, v2 = _adamw(tr(w[0]), grt, tr(m[0]), tr(v[0]), "adamw_" + name)
        return tr(grt)[None], tr(delta)[None], tr(m2)[None], tr(v2)[None]

    P_in = _halves(_split_shards(dW_in_main, dW_in_small, IN_COLS // 4))
    (A_in,) = _run_comm(_swap_comm(P_in), "rs_w_in_swap")
    S_in = _add_half([P_in], [A_in], cidx, "rs_w_in_add")
    dh1, (B_in,) = _mm(dproj, w_in_main, mode="nt", out_dtypes=[F32], name="mm_dh1", comm=_a2a_comm(S_in, 0, 8, cnt=7))
    dh1, (B_in,) = _mm(dsmall, w_in_small, mode="nt", out_dtypes=[F32], epi=add, extras=[dh1], name="mm_dh1_small",
                       comm=_a2a_comm(S_in, 7, 8, B_in))
    dx, _, d_gmix = _rms_bwd(dh1, xs, norm_mix_g, dx1, "rms1_bwd")

    (F_in,) = _run_comm(_fill_comm(own_sum(S_in, B_in, "w_in")), "rs_w_in_gather")
    big = {
        "w_in": update(F_in, w_in, m_w_in, v_w_in, "w_in"),
        "w_out": update(F_out, w_out, m_w_out, v_w_out, "w_out"),
        "w_up": update(F_up, w_up, m_w_up, v_w_up, "w_up"),
        "w_down": update(F_down, w_down, m_w_down, v_w_down, "w_down"),
        "w_ple_gate": update(F_pg, w_ple_gate, m_w_ple_gate, v_w_ple_gate, "w_pg"),
        "w_ple_proj": update(F_pp, w_ple_proj, m_w_ple_proj, v_w_ple_proj, "w_pp"),
    }

    small_grads = [d_gmix[0:1], dcw_a[:cw_a.shape[0]], jnp.concatenate([dcw_q, dcw_k, dcw_v], axis=1)[:cw_qkv.shape[0]],
                   d_ab[0:1, :H], d_ab[1:2, :H], d_gdn[0:1], d_gffn[0:1],
                   jnp.concatenate([dcw_fg, dcw_fv], axis=1)[:cw_ffn.shape[0]], d_gple[0:1], d_gfin]
    small_shapes = [v.shape for v in small_grads]
    gpack = _pack_rows(small_grads)
    gsum = _sum_stack(_gather_all(gpack, "ag_small"), "sum_small")
    (g_gmix, g_cwa, g_cwqkv, g_alog, g_dt, g_gdn, g_gffn, g_cwffn, g_gple, g_gfin) = _unpack_rows(gsum, small_shapes)

    def my_cols(v):
        Cc = v.shape[1] // 4
        return lax.dynamic_slice_in_dim(v, chip * Cc, Cc, axis=1)

    g_small = [g_gmix, my_cols(g_cwa), my_cols(g_cwqkv), g_alog, g_dt, g_gdn, g_gffn, my_cols(g_cwffn), g_gple, g_gfin]
    w_small = [norm_mix_g, conv_a_w[0], conv_qkv_w[0], a_log, dt_bias, dn_norm_g, norm_ffn_g, conv_ffn_w[0], norm_ple_g, gfin]
    m_small = [m_norm_mix_g, m_conv_a_w[0], m_conv_qkv_w[0], m_a_log, m_dt_bias, m_dn_norm_g, m_norm_ffn_g, m_conv_ffn_w[0],
               m_norm_ple_g, m_final_norm_g.reshape(1, D)]
    v_small = [v_norm_mix_g, v_conv_a_w[0], v_conv_qkv_w[0], v_a_log, v_dt_bias, v_dn_norm_g, v_norm_ffn_g, v_conv_ffn_w[0],
               v_norm_ple_g, v_final_norm_g.reshape(1, D)]
    shp = [v.shape for v in w_small]
    ds_, ms_, vs_ = _adamw(_pack_rows(w_small), _pack_rows(g_small), _pack_rows(m_small), _pack_rows(v_small), "adamw_small")
    out_shapes = [norm_mix_g.shape, conv_a_w.shape, conv_qkv_w.shape, a_log.shape, dt_bias.shape, dn_norm_g.shape,
                  norm_ffn_g.shape, conv_ffn_w.shape, norm_ple_g.shape, final_norm_g.shape]
    rs = lambda vals: [v.reshape(s) for v, s in zip(vals, out_shapes)]
    sg, sd_, sm_, sv_ = rs(g_small), rs(_unpack_rows(ds_, shp)), rs(_unpack_rows(ms_, shp)), rs(_unpack_rows(vs_, shp))
    names_small = ["norm_mix_g", "conv_a_w", "conv_qkv_w", "a_log", "dt_bias", "dn_norm_g", "norm_ffn_g", "conv_ffn_w",
                   "norm_ple_g", "final_norm_g"]
    res = {n: (sg[i], sd_[i], sm_[i], sv_[i]) for i, n in enumerate(names_small)}
    res.update(big)
    order = ["norm_mix_g", "w_in", "conv_a_w", "conv_qkv_w", "a_log", "dt_bias", "dn_norm_g", "w_out", "norm_ffn_g", "w_up",
             "conv_ffn_w", "w_down", "norm_ple_g", "w_ple_gate", "w_ple_proj", "final_norm_g"]
    return (loss, dx[None], *[res[n][0] for n in order], *[res[n][1] for n in order], *[res[n][2] for n in order],
            *[res[n][3] for n in order])
```

```python
import functools

import jax
import jax.numpy as jnp
from jax import lax
from jax.experimental import pallas as pl
from jax.experimental.pallas import tpu as pltpu

F32 = jnp.float32
BF16 = jnp.bfloat16
LANES = 128
HALO = 8
HEAD_DIM = 128
CHUNK = 64
EPS = 1e-6
VMEM_LIMIT = 56 * 1024 * 1024
MM_VMEM_BUDGET = 40 * 1024 * 1024
MM_STEP_BYTES = 1 << 20
EW_VMEM_BUDGET = 28 * 1024 * 1024
MESH = pl.DeviceIdType.MESH

ADAM_LR, ADAM_B1, ADAM_B2, ADAM_EPS, ADAM_WD, ADAM_STEP = 0.001, 0.9, 0.999, 1e-08, 0.01, 10


def _tile(n, cap, unit):
    if n <= cap:
        return n
    d = (cap // unit) * unit
    while d >= unit:
        if n % d == 0:
            return d
        d -= unit
    raise ValueError(f"no tile for {n} (cap {cap}, unit {unit})")


def _sigmoid(x):
    return 1.0 / (1.0 + jnp.exp(-x))


def _divisors(n, cap):
    ds = [d for d in range(cap // LANES * LANES, 0, -LANES) if n % d == 0]
    return [n] if (n <= cap or not ds) else ds


def _mm_tiles(M, N, K, n_unit, k_unit, a_bytes, n_blocks_mn, a_transposed, tn_full=False):
    best = None
    for tm in _divisors(M, 1536):
        for tn in ([N] if tn_full else _divisors(n_unit, 1536)):
            for tk in _divisors(k_unit, 4096):
                nk = K // tk
                vmem = 2 * tm * tk * a_bytes + 2 * tk * tn * 2 + 2 * 4 * tm * tn * n_blocks_mn + (4 * tm * tn if nk > 1 else 0)
                if vmem > MM_VMEM_BUDGET:
                    continue
                steps = (M // tm) * (N // tn) * nk
                cost = (M * K * a_bytes * (N // tn if nk > 1 else 1) + K * N * 2 * (M // tm) + 4 * M * N * n_blocks_mn
                        + (8 * M * N * nk // 3 if nk > 1 else 0) + steps * MM_STEP_BYTES
                        + (2 * steps * tm * tk if a_transposed else 0))
                if best is None or cost < best[0]:
                    best = (cost, tm, tn, tk)
    return best[1:]


def _mm(a, b, *, mode, out_dtypes, name, epi=None, extras=(), comm=None, b_split=None, out_split=None, rows=()):
    if b_split is not None:
        lo, ns = b_split
        Rb, Cb = b.shape[1], b.shape[2]
    if mode == "nn":
        (M, K), N = a.shape, (ns * Cb if b_split else b.shape[1])
    elif mode == "nt":
        (M, K), N = a.shape, (Rb if b_split else b.shape[0])
    else:
        (K, M), N = a.shape, b.shape[1]
    n_ex, n_out = len(extras), len(out_dtypes)
    n_unit = Cb if (b_split and mode == "nn") else (N // out_split if out_split else N)
    n_rows = len(rows)
    assert not (n_rows and (b_split or out_split))
    k_unit = Cb if (b_split and mode == "nt") else K
    tm, tn, tk = _mm_tiles(M, N, K, n_unit, k_unit, a.dtype.itemsize, n_ex + n_out, mode == "tn", tn_full=bool(n_rows))
    nk = K // tk
    a_spec = pl.BlockSpec((tk, tm), lambda i, j, k: (k, i)) if mode == "tn" else pl.BlockSpec((tm, tk), lambda i, j, k: (i, k))
    if b_split and mode == "nn":
        nb = Cb // tn
        b_spec = pl.BlockSpec((None, tk, tn), lambda i, j, k: (lo + j // nb, k, j % nb))
    elif b_split:
        nb = Cb // tk
        b_spec = pl.BlockSpec((None, tn, tk), lambda i, j, k: (lo + k // nb, j, k % nb))
    else:
        b_spec = pl.BlockSpec((tn, tk), lambda i, j, k: (j, k)) if mode == "nt" else pl.BlockSpec((tk, tn), lambda i, j, k: (k, j))
    mn_spec = pl.BlockSpec((tm, tn), lambda i, j, k: (i, j))
    out_shapes = [jax.ShapeDtypeStruct((M, N), dt) for dt in out_dtypes]
    out_specs = [mn_spec] * n_out
    if out_split:
        assert n_ex == 0 and n_out == 1
        nbo = (N // out_split) // tn
        out_specs = [pl.BlockSpec((None, tm, tn), lambda i, j, k: (j // nbo, i, j % nbo))]
        out_shapes = [jax.ShapeDtypeStruct((out_split, M, N // out_split), out_dtypes[0])]
    dims = {"nn": (((1,), (0,)), ((), ())), "nt": (((1,), (1,)), ((), ())), "tn": (((0,), (0,)), ((), ()))}[mode]

    def body(*refs):
        a_ref, b_ref = refs[0], refs[1]
        ex_refs = refs[2:2 + n_ex + n_rows]
        out_refs = refs[2 + n_ex + n_rows:2 + n_ex + n_rows + n_out]
        part = lax.dot_general(a_ref[...].astype(BF16), b_ref[...].astype(BF16), dims, preferred_element_type=F32)

        def finish(acc):
            outs = (acc,) if epi is None else epi(acc, *[r[...] for r in ex_refs])
            for r, o in zip(out_refs, outs):
                r[...] = o.astype(r.dtype)

        if nk == 1:
            finish(part)
            return
        acc_ref = refs[-1]
        k = pl.program_id(2)

        @pl.when(k == 0)
        def _():
            acc_ref[...] = part

        @pl.when(jnp.logical_and(k > 0, k < nk - 1))
        def _():
            acc_ref[...] += part

        @pl.when(k == nk - 1)
        def _():
            finish(acc_ref[...] + part)

    outs, comm_outs = _call(
        body, name=name, grid=(M // tm, N // tn, nk),
        in_specs=[a_spec, b_spec] + [mn_spec] * n_ex + [pl.BlockSpec((1, tn), lambda i, j, k: (0, j))] * n_rows,
        out_specs=out_specs,
        out_shape=out_shapes,
        scratch_shapes=[pltpu.VMEM((tm, tn), F32)] if nk > 1 else [],
        semantics=("parallel", "parallel", "arbitrary"), args=(a, b, *extras, *rows), comm=comm)
    res = outs[0] if n_out == 1 else outs
    return res if comm is None else (res, comm_outs)


def _tiled(fn, *, T, C, ins, out_dtypes=(), acc_rows=(), tb=None, cb=512, name, comm=None, into=None):
    tb = _tile(T, tb or (512 if cb <= 1024 else 256), HALO)
    nI, nJ = T // tb, C // cb
    hb, nH = tb // HALO, T // HALO
    specs, args, kinds = [], [], []
    for kind, arr, cmap in ins:
        cm = cmap if cmap is not None else (lambda j: j)
        kinds.append(kind)
        if kind == "cur":
            specs.append(pl.BlockSpec((tb, cb), lambda j, i, cm=cm: (i, cm(j))))
            args.append(arr)
        elif kind == "ext":
            specs.append(pl.BlockSpec((HALO, cb), lambda j, i, cm=cm: (jnp.maximum(i * hb - 1, 0), cm(j))))
            specs.append(pl.BlockSpec((tb, cb), lambda j, i, cm=cm: (i, cm(j))))
            specs.append(pl.BlockSpec((HALO, cb), lambda j, i, cm=cm: (jnp.minimum((i + 1) * hb, nH - 1), cm(j))))
            args += [arr, arr, arr]
        elif kind == "row":
            specs.append(pl.BlockSpec((arr.shape[0], cb), lambda j, i, cm=cm: (0, cm(j))))
            args.append(arr)
        elif kind == "stack":
            specs.append(pl.BlockSpec((arr.shape[0], tb, cb), lambda j, i, cm=cm: (0, i, cm(j))))
            args.append(arr)
        else:
            raise ValueError(kind)
    n_in = len(args)
    n_out, n_acc = len(out_dtypes), len(acc_rows)

    def body(*refs):
        j, i = pl.program_id(0), pl.program_id(1)
        vals, r = [], 0
        for kind in kinds:
            if kind == "ext":
                prev = jnp.where(i == 0, 0.0, refs[r][...].astype(F32))
                cur = refs[r + 1][...].astype(F32)
                nxt = jnp.where(i == nI - 1, 0.0, refs[r + 2][...].astype(F32))
                vals.append(jnp.concatenate([prev, cur, nxt], axis=0))
                r += 3
            else:
                vals.append(refs[r][...])
                r += 1
        res = fn(j, i, *vals)
        for ref, o in zip(refs[n_in:n_in + n_out], res[:n_out]):
            ref[...] = o.astype(ref.dtype)
        for ref, o in zip(refs[n_in + n_out:], res[n_out:]):
            @pl.when(i == 0)
            def _(ref=ref, o=o):
                ref[...] = o

            @pl.when(i > 0)
            def _(ref=ref, o=o):
                ref[...] += o

    out_specs = [pl.BlockSpec((tb, cb), lambda j, i: (i, j))] * n_out
    out_shape = [jax.ShapeDtypeStruct((T, C), dt) for dt in out_dtypes]
    io_aliases = None
    if into is not None:
        buf, total, off = into
        assert n_out == 1 and comm is None
        out_specs = [pl.BlockSpec((tb, cb), lambda j, i: (i, j + off))]
        out_shape = [jax.ShapeDtypeStruct((T, total), out_dtypes[0])]
        if buf is not None:
            specs, args, io_aliases = specs + [ANY], args + [buf], {n_in: 0}
            n_in += 1
    outs, comm_outs = _call(
        body, name=name, grid=(nJ, nI), in_specs=specs,
        out_specs=out_specs + [pl.BlockSpec((rows, cb), lambda j, i: (0, j)) for rows in acc_rows],
        out_shape=out_shape + [jax.ShapeDtypeStruct((rows, C), F32) for rows in acc_rows],
        scratch_shapes=[], semantics=("parallel", "arbitrary"), args=args, comm=comm, io_aliases=io_aliases)
    return outs if comm is None else (outs, comm_outs)


def _conv_causal(xe, w):
    K = w.shape[0]
    y = xe * w[K - 1:K]
    for j in range(K - 1):
        y = y + pltpu.roll(xe, K - 1 - j, 0) * w[j:j + 1]
    return y


def _conv_anti(de, w):
    K, n = w.shape[0], de.shape[0]
    y = de * w[K - 1:K]
    for j in range(K - 1):
        y = y + pltpu.roll(de, n - (K - 1 - j), 0) * w[j:j + 1]
    return y


def _conv_dw(dce, xe, K):
    n = dce.shape[0]
    tb = n - 2 * HALO
    rows = []
    for j in range(K):
        xs = xe if j == K - 1 else pltpu.roll(xe, K - 1 - j, 0)
        rows.append(jnp.sum((dce * xs)[HALO:HALO + tb], axis=0, keepdims=True))
    rows.append(jnp.zeros((HALO - K, dce.shape[1]), F32))
    return jnp.concatenate(rows, axis=0)


def _own(xe):
    return xe[HALO:xe.shape[0] - HALO]


def _row0(v):
    return jnp.concatenate([v, jnp.zeros((HALO - 1, v.shape[1]), F32)], axis=0)


def _per_head(fn, *xs):
    n = xs[0].shape[1] // HEAD_DIM
    outs = [fn(*[x[:, g * HEAD_DIM:(g + 1) * HEAD_DIM] for x in xs]) for g in range(n)]
    return outs[0] if n == 1 else jnp.concatenate(outs, axis=1)


def _rms_fwd(x, g, name):
    T, D = x.shape

    def fn(j, i, xv, gv):
        r = lax.rsqrt(jnp.mean(xv * xv, axis=1, keepdims=True) + EPS)
        return (xv * r * gv,)

    return _tiled(fn, T=T, C=D, ins=[("cur", x, None), ("row", g, None)], out_dtypes=[BF16], cb=D, name=name)[0]


def _rms_bwd_math(dy, xv, gv):
    r = lax.rsqrt(jnp.mean(xv * xv, axis=1, keepdims=True) + EPS)
    xh = xv * r
    dxh = dy * gv
    dx = r * (dxh - xh * jnp.mean(dxh * xh, axis=1, keepdims=True))
    dg = jnp.sum(dy * xh, axis=0, keepdims=True)
    return dx, dg


def _rms_bwd(dh, x, g, dres, name, comm=None):
    T, D = x.shape

    def fn(j, i, dhv, xv, gv, dr):
        dx, dg = _rms_bwd_math(dhv, xv, gv)
        return dr + dx, dr + dx, _row0(dg)

    return _tiled(fn, T=T, C=D, ins=[("cur", dh, None), ("cur", x, None), ("row", g, None), ("cur", dres, None)],
                  out_dtypes=[F32, BF16], acc_rows=[HALO], cb=D, name=name, comm=comm)


def _final_fb(x3, tgt, g, pp, pg):
    T, D = x3.shape

    def fn(j, i, xv, tv, gv, ppv, pgv):
        r = lax.rsqrt(jnp.mean(xv * xv, axis=1, keepdims=True) + EPS)
        xh = xv * r
        e = xh * gv - tv
        dy = e * (1.0 / D)
        dxh = dy * gv
        dx = r * (dxh - xh * jnp.mean(dxh * xh, axis=1, keepdims=True))
        dg = jnp.sum(dy * xh, axis=0, keepdims=True)
        ls = jnp.sum(e * e, axis=0, keepdims=True) * (0.5 / D)
        return (dx, dx * ppv * pgv * (1.0 - pgv), dx * pgv,
                jnp.concatenate([dg, ls, jnp.zeros((HALO - 2, D), F32)], axis=0))

    return _tiled(fn, T=T, C=D, ins=[("cur", x3, None), ("cur", tgt, None), ("row", g, None), ("cur", pp, None),
                                      ("cur", pg, None)], out_dtypes=[F32, BF16, BF16], acc_rows=[HALO], cb=D, name="final_fb")


def _ga_fwd(proj, w_a, CW, cb, total):
    T = proj.shape[0]
    n = CW // cb

    def fn(j, i, ax, ab, ac, w):
        c = _conv_causal(ac * ax, w)
        return (ab * _own(c),)

    return _tiled(fn, T=T, C=CW, ins=[("ext", proj, None), ("cur", proj, lambda j: j + n), ("ext", proj, lambda j: j + 2 * n),
                                       ("row", w_a, None)], out_dtypes=[BF16], cb=cb, name="ga_fwd", into=(None, total, 0))[0]


def _ga_bwd(dymix, proj, w_a, CW, cb):
    T = proj.shape[0]
    n = CW // cb
    K = w_a.shape[0]

    def fn(j, i, dy, ax, ab, ac, w):
        u = ac * ax
        c = _conv_causal(u, w)
        dc = dy * ab
        du = _conv_anti(dc, w)
        return _own(du * ac), _own(dy * c), _own(du * ax), _conv_dw(dc, u, K)

    return _tiled(fn, T=T, C=CW, ins=[("ext", dymix, None), ("ext", proj, None), ("ext", proj, lambda j: j + n),
                                       ("ext", proj, lambda j: j + 2 * n), ("row", w_a, None)],
                  out_dtypes=[BF16, BF16, BF16], acc_rows=[HALO], cb=cb, name="ga_bwd")


def _l2n(s):
    return s * lax.rsqrt(jnp.sum(s * s, axis=1, keepdims=True) + EPS)


def _qkv_fwd(proj, w_sec, coff, normalize, DNW, cb, name, comm=None):
    T = proj.shape[0]

    def fn(j, i, pre, w):
        c = _own(_conv_causal(pre, w))
        s = c * _sigmoid(c)
        return (_per_head(_l2n, s) if normalize else s,)

    res = _tiled(fn, T=T, C=DNW, ins=[("ext", proj, lambda j: j + coff), ("row", w_sec, None)],
                 out_dtypes=[F32], cb=cb, name=name, comm=comm)
    return res[0] if comm is None else (res[0][0], res[1])


def _qkv_bwd(dsec, proj, w_sec, coff, normalize, DNW, cb, name):
    T = proj.shape[0]
    K = w_sec.shape[0]

    def l2n_bwd(s, dn):
        r = lax.rsqrt(jnp.sum(s * s, axis=1, keepdims=True) + EPS)
        nrm = s * r
        return r * (dn - nrm * jnp.sum(dn * nrm, axis=1, keepdims=True))

    def fn(j, i, dn, pre, w):
        c = _conv_causal(pre, w)
        sg = _sigmoid(c)
        s = c * sg
        ds = _per_head(l2n_bwd, s, dn) if normalize else dn
        dc = ds * (sg * (1.0 + c * (1.0 - sg)))
        return _own(_conv_anti(dc, w)), _conv_dw(dc, pre, K)

    return _tiled(fn, T=T, C=DNW, ins=[("ext", dsec, None), ("ext", proj, lambda j: j + coff), ("row", w_sec, None)],
                  out_dtypes=[BF16], acc_rows=[HALO], cb=cb, name=name)


def _gb_fwd(small, a_log_row, dt_row, H):
    T = small.shape[0]

    def fn(j, i, sm, al, dt):
        z = sm + dt
        sp = jnp.maximum(z, 0.0) + jnp.log(1.0 + jnp.exp(-jnp.abs(z)))
        g = -jnp.exp(al) * sp
        beta = _sigmoid(pltpu.roll(sm, LANES - H, 1))
        return g, beta

    return _tiled(fn, T=T, C=LANES, ins=[("cur", small, None), ("row", a_log_row, None), ("row", dt_row, None)],
                  out_dtypes=[F32, F32], cb=LANES, name="gb_fwd")


def _gb_bwd(dgB, dbB, small, g, beta, a_log_row, dt_row, H):
    T = small.shape[0]

    def fn(j, i, dgv, dbv, sm, gv, bv, al, dt):
        lane = lax.broadcasted_iota(jnp.int32, sm.shape, 1)
        dg = jnp.zeros(sm.shape, F32)
        db = jnp.zeros(sm.shape, F32)
        for h in range(H):
            dg = jnp.where(lane == h, jnp.sum(dgv[h], axis=1, keepdims=True), dg)
            db = jnp.where(lane == h, jnp.sum(dbv[h], axis=1, keepdims=True), db)
        da = dg * (-jnp.exp(al)) * _sigmoid(sm + dt)
        dbb = db * bv * (1.0 - bv)
        dsm = jnp.where(lane < H, da, 0.0) + pltpu.roll(jnp.where(lane < H, dbb, 0.0), H, 1)
        d_alog = jnp.sum(jnp.where(lane < H, dg * gv, 0.0), axis=0, keepdims=True)
        d_dt = jnp.sum(jnp.where(lane < H, da, 0.0), axis=0, keepdims=True)
        return dsm, jnp.concatenate([d_alog, d_dt, jnp.zeros((HALO - 2, LANES), F32)], axis=0)

    return _tiled(fn, T=T, C=LANES, ins=[("stack", dgB, None), ("stack", dbB, None), ("cur", small, None), ("cur", g, None),
                                          ("cur", beta, None), ("row", a_log_row, None), ("row", dt_row, None)],
                  out_dtypes=[BF16], acc_rows=[HALO], cb=LANES, name="gb_bwd")


_DIMS = {"nn": (((1,), (0,)), ((), ())), "nt": (((1,), (1,)), ((), ())), "tn": (((0,), (0,)), ((), ()))}
_DOT_BWD = {"nn": (("nt", "gb"), ("tn", "ag")), "nt": (("nn", "gb"), ("tn", "ga")), "tn": (("nt", "bg"), ("nn", "ag"))}


def _split(a):
    hi = a.astype(BF16)
    return hi, (a - hi.astype(F32)).astype(BF16)


def _raw_dot(a, b, kind, passes):
    dg = lambda x, y: lax.dot_general(x, y, _DIMS[kind], preferred_element_type=F32)
    if passes == 1:
        return dg(a.astype(BF16), b.astype(BF16))
    ah, al = _split(a)
    bh, bl = _split(b)
    if kind == "tn":
        return dg(ah, bh) + (dg(ah, bl) + dg(al, bh))
    m = a.shape[0]
    top = dg(jnp.concatenate([ah, al], axis=0), bh)
    return top[:m] + (dg(ah, bl) + top[m:])


def _raw_dot_exact(a, b, kind, exact):
    dg = lambda x, y: lax.dot_general(x, y, _DIMS[kind], preferred_element_type=F32)
    if exact == "a":
        bh, bl = _split(b)
        return dg(a.astype(BF16), bh) + dg(a.astype(BF16), bl)
    ah, al = _split(a)
    return dg(ah, b.astype(BF16)) + dg(al, b.astype(BF16))


@functools.lru_cache(maxsize=None)
def _dotc(kind):
    @jax.custom_vjp
    def f(a, b):
        return _raw_dot_exact(a, b, kind, "a")

    def fwd(a, b):
        return _raw_dot_exact(a, b, kind, "a"), a

    def bwd(a, g):
        db = _raw_dot_exact(a, g, "tn", "a") if kind == "nn" else _raw_dot_exact(g, a, "tn", "b")
        return jnp.zeros_like(a), db

    f.defvjp(fwd, bwd)
    return f


@functools.lru_cache(maxsize=None)
def _dotf(kind, passes):
    @jax.custom_vjp
    def f(a, b):
        return _raw_dot(a, b, kind, passes)

    def fwd(a, b):
        return _raw_dot(a, b, kind, passes), (a, b)

    def bwd(res, g):
        ops = {"a": res[0], "b": res[1], "g": g}
        (ka, oa), (kb, ob) = _DOT_BWD[kind]
        return (_raw_dot(ops[oa[0]], ops[oa[1]], ka, passes), _raw_dot(ops[ob[0]], ops[ob[1]], kb, passes))

    f.defvjp(fwd, bwd)
    return f


@jax.custom_vjp
def _saved_inverse(L, inv):
    return inv


def _saved_inverse_fwd(L, inv):
    return inv, inv


def _saved_inverse_bwd(inv, g):
    d3nt, d3tn = _dotf("nt", 3), _dotf("tn", 3)
    return -d3nt(d3tn(inv, g), inv), jnp.zeros_like(inv)


_saved_inverse.defvjp(_saved_inverse_fwd, _saved_inverse_bwd)


def _chunk_fn(q, k, v, gB, bB, S, inv_saved=None):
    C = CHUNK
    d3 = _dotf("nn", 3)
    d1, d1nt, d1tn = _dotf("nn", 1), _dotf("nt", 1), _dotf("tn", 1)
    each = lambda f, *ls: tuple(f(*xs) for xs in zip(*ls))
    row = lax.broadcasted_iota(jnp.int32, (C, C), 0)
    col = lax.broadcasted_iota(jnp.int32, (C, C), 1)
    causal = row >= col
    strict = row > col
    tril = jnp.where(causal, 1.0, 0.0).astype(F32)
    eye = jnp.where(row == col, 1.0, 0.0).astype(F32)
    avg = jnp.full((C, HEAD_DIM), 1.0 / HEAD_DIM, F32)
    gc = each(lambda g: _dotc("nn")(tril, g), gB)
    R = each(lambda g: _dotc("nt")(avg, g), gc)
    decay = each(lambda g, r: jnp.where(causal, jnp.exp(jnp.where(causal, g[:, :C] - r, 0.0)), 0.0), gc, R)
    kk = each(lambda x: d1nt(x, x), k)
    L = each(lambda a, d, b: jnp.where(strict, a * d * b[:, :C], 0.0), kk, decay, bB)
    if inv_saved is None:
        inv = each(lambda l: eye - l, L)
        P = L
        for _ in range(5):
            P = each(lambda p: d3(p, p), P)
            inv = each(lambda a, p: d3(a, eye + p), inv, P)
    else:
        inv = each(_saved_inverse, L, inv_saved)
    eg = each(jnp.exp, gc)
    u = each(lambda a, x, b: d3(a, x * b), inv, v, bB)
    w = each(lambda a, x, b, e: d3(a, x * b * e), inv, k, bB, eg)
    qs = each(lambda x: x * (HEAD_DIM ** -0.5), q)
    qk = each(lambda a, x, d: d1nt(a, x) * d, qs, k, decay)
    gl = each(lambda g: g[C - 1:C, :], gc)
    kd = each(lambda x, a, g: x * jnp.exp(a - g), k, gl, gc)
    qe = each(lambda a, e: a * e, qs, eg)
    nh = len(S)
    o = ()
    for c in range(len(q) // nh):
        sl = slice(c * nh, (c + 1) * nh)
        v_new = each(lambda a, b, s: a - d1(b, s), u[sl], w[sl], S)
        o1 = each(lambda a, s: d1(a, s), qe[sl], S)
        o += each(lambda a, b, vn: a + d1(b, vn), o1, qk[sl], v_new)
        kv = each(lambda x, vn: d1tn(x, vn), kd[sl], v_new)
        S = each(lambda s, a, b: s * jnp.exp(a) + b, S, gl[sl], kv)
    return (o, S), inv


def _sel_lane(x, h):
    lane = lax.broadcasted_iota(jnp.int32, x.shape, 1)
    return jnp.broadcast_to(jnp.sum(jnp.where(lane == h, x, 0.0), axis=1, keepdims=True), x.shape)


def _tile_of(ref, c, h):
    return ref[c * CHUNK:(c + 1) * CHUNK, h * HEAD_DIM:(h + 1) * HEAD_DIM]


def _chunks_per_step(N):
    return 4 if N % 4 == 0 else (2 if N % 2 == 0 else 1)


def _delta_fwd(q, k, v, g, beta, comm=None):
    T = q.shape[0]
    H, N = q.shape[1] // HEAD_DIM, T // CHUNK
    cps = _chunks_per_step(N)
    rows = cps * CHUNK

    def body(q_ref, k_ref, v_ref, g_ref, b_ref, o_ref, s_ref, inv_ref, S):
        @pl.when(pl.program_id(0) == 0)
        def _():
            S[...] = jnp.zeros_like(S)

        gv, bv = g_ref[...], b_ref[...]
        pairs = lambda f: tuple(f(c, h) for c in range(cps) for h in range(H))
        S_in = tuple(S[h] for h in range(H))
        for h in range(H):
            s_ref[h, 0] = S_in[h]
        (o, S_new), inv = _chunk_fn(pairs(lambda c, h: _tile_of(q_ref, c, h)), pairs(lambda c, h: _tile_of(k_ref, c, h)),
                                    pairs(lambda c, h: _tile_of(v_ref, c, h)),
                                    pairs(lambda c, h: _sel_lane(gv[c * CHUNK:(c + 1) * CHUNK], h)),
                                    pairs(lambda c, h: _sel_lane(bv[c * CHUNK:(c + 1) * CHUNK], h)), S_in)
        for c in range(cps):
            for h in range(H):
                o_ref[c * CHUNK:(c + 1) * CHUNK, h * HEAD_DIM:(h + 1) * HEAD_DIM] = o[c * H + h]
                inv_ref[h, c] = inv[c * H + h]
        for h in range(H):
            S[h] = S_new[h]

    blk = pl.BlockSpec((rows, H * HEAD_DIM), lambda n: (n, 0))
    gblk = pl.BlockSpec((rows, LANES), lambda n: (n, 0))
    outs, comm_outs = _call(
        body, name="delta_fwd", grid=(N // cps,), in_specs=[blk, blk, blk, gblk, gblk],
        out_specs=[blk, pl.BlockSpec((H, 1, HEAD_DIM, HEAD_DIM), lambda n: (0, n, 0, 0)),
                   pl.BlockSpec((H, cps, CHUNK, CHUNK), lambda n: (0, n, 0, 0))],
        out_shape=[jax.ShapeDtypeStruct((T, H * HEAD_DIM), F32), jax.ShapeDtypeStruct((H, N // cps, HEAD_DIM, HEAD_DIM), F32),
                   jax.ShapeDtypeStruct((H, N, CHUNK, CHUNK), F32)],
        scratch_shapes=[pltpu.VMEM((H, HEAD_DIM, HEAD_DIM), F32)],
        semantics=("arbitrary",), args=(q, k, v, g, beta), comm=comm)
    return outs[0], outs[1], outs[2], comm_outs


def _delta_bwd(q, k, v, g, beta, S0, inv, do, comm=None):
    T = q.shape[0]
    H, N = q.shape[1] // HEAD_DIM, T // CHUNK
    cps = _chunks_per_step(N)
    rows, NS = cps * CHUNK, N // cps

    def body(q_ref, k_ref, v_ref, g_ref, b_ref, s_ref, inv_ref, do_ref, dq_ref, dk_ref, dv_ref, dg_ref, db_ref, dS):
        @pl.when(pl.program_id(0) == 0)
        def _():
            dS[...] = jnp.zeros_like(dS)

        gv, bv = g_ref[...], b_ref[...]
        pairs = lambda f: tuple(f(c, h) for c in range(cps) for h in range(H))
        heads = lambda f: tuple(f(h) for h in range(H))
        _, vjp, _ = jax.vjp(_chunk_fn, pairs(lambda c, h: _tile_of(q_ref, c, h)), pairs(lambda c, h: _tile_of(k_ref, c, h)),
                            pairs(lambda c, h: _tile_of(v_ref, c, h)),
                            pairs(lambda c, h: _sel_lane(gv[c * CHUNK:(c + 1) * CHUNK], h)),
                            pairs(lambda c, h: _sel_lane(bv[c * CHUNK:(c + 1) * CHUNK], h)),
                            heads(lambda h: s_ref[h, 0]), pairs(lambda c, h: inv_ref[h, c]), has_aux=True)
        dq, dk, dv, dgB, dbB, dS_prev, _ = vjp((pairs(lambda c, h: _tile_of(do_ref, c, h)), heads(lambda h: dS[h])))
        for c in range(cps):
            for h in range(H):
                r, sl = slice(c * CHUNK, (c + 1) * CHUNK), slice(h * HEAD_DIM, (h + 1) * HEAD_DIM)
                dq_ref[r, sl] = dq[c * H + h]
                dk_ref[r, sl] = dk[c * H + h]
                dv_ref[r, sl] = dv[c * H + h]
                dg_ref[h, r] = dgB[c * H + h]
                db_ref[h, r] = dbB[c * H + h]
        for h in range(H):
            dS[h] = dS_prev[h]

    blk = pl.BlockSpec((rows, H * HEAD_DIM), lambda n: (NS - 1 - n, 0))
    gblk = pl.BlockSpec((rows, LANES), lambda n: (NS - 1 - n, 0))
    hblk = pl.BlockSpec((H, rows, LANES), lambda n: (0, NS - 1 - n, 0))
    sd = jax.ShapeDtypeStruct
    outs, comm_outs = _call(
        body, name="delta_bwd", grid=(NS,),
        in_specs=[blk, blk, blk, gblk, gblk, pl.BlockSpec((H, 1, HEAD_DIM, HEAD_DIM), lambda n: (0, NS - 1 - n, 0, 0)),
                  pl.BlockSpec((H, cps, CHUNK, CHUNK), lambda n: (0, NS - 1 - n, 0, 0)), blk],
        out_specs=[blk, blk, blk, hblk, hblk],
        out_shape=[sd((T, H * HEAD_DIM), F32)] * 3 + [sd((H, T, LANES), F32)] * 2,
        scratch_shapes=[pltpu.VMEM((H, HEAD_DIM, HEAD_DIM), F32)],
        semantics=("arbitrary",), args=(q, k, v, g, beta, S0, inv, do), comm=comm)
    return (*outs, comm_outs)


def _gnorm_fwd(o, proj, z_coff, gdn_t, DNW, buf, coff):
    T = o.shape[0]

    def fn(j, i, ov, zv, gv):
        def one(oh, zh, gh):
            r = lax.rsqrt(jnp.mean(oh * oh, axis=1, keepdims=True) + EPS)
            return oh * r * gh * (zh * _sigmoid(zh))
        return (_per_head(one, ov, zv, jnp.broadcast_to(gv, ov.shape)),)

    return _tiled(fn, T=T, C=DNW, ins=[("cur", o, None), ("cur", proj, lambda j: j + z_coff), ("row", gdn_t, None)],
                  out_dtypes=[BF16], cb=DNW, name="gnorm_fwd", into=(buf, buf.shape[1], coff))[0]


def _gnorm_bwd(dymix, y_coff, o, proj, z_coff, gdn_t, DNW):
    T = o.shape[0]
    nh = DNW // HEAD_DIM

    def fn(j, i, dy, ov, zv, gv):
        dos, dzs, dgs = [], [], jnp.zeros((1, HEAD_DIM), F32)
        for h in range(nh):
            sl = slice(h * HEAD_DIM, (h + 1) * HEAD_DIM)
            dyh, oh, zh, gh = dy[:, sl].astype(F32), ov[:, sl], zv[:, sl], gv[:, sl]
            r = lax.rsqrt(jnp.mean(oh * oh, axis=1, keepdims=True) + EPS)
            on = oh * r
            sg = _sigmoid(zh)
            sz = zh * sg
            dzs.append(dyh * on * gh * (sg * (1.0 + zh * (1.0 - sg))))
            don = dyh * gh * sz
            dos.append(r * (don - on * jnp.mean(don * on, axis=1, keepdims=True)))
            dgs = dgs + jnp.sum(dyh * on * sz, axis=0, keepdims=True)
        cat = (lambda xs: xs[0] if nh == 1 else jnp.concatenate(xs, axis=1))
        return cat(dos), cat(dzs), _row0(dgs)

    T_ = T
    nI = T_ // _tile(T_, 256, HALO)
    tb = T_ // nI
    specs_cb = DNW

    def body_wrap():
        def body(dy_ref, o_ref, z_ref, g_ref, do_ref, dz_ref, dg_ref):
            i = pl.program_id(0)
            d_o, d_z, d_g = fn(0, i, dy_ref[...], o_ref[...], z_ref[...], g_ref[...])
            do_ref[...] = d_o
            dz_ref[...] = d_z.astype(dz_ref.dtype)

            @pl.when(i == 0)
            def _():
                dg_ref[...] = d_g

            @pl.when(i > 0)
            def _():
                dg_ref[...] += d_g

        return pl.pallas_call(
            body, name="gnorm_bwd", grid=(nI,),
            in_specs=[pl.BlockSpec((tb, specs_cb), lambda i: (i, y_coff)), pl.BlockSpec((tb, specs_cb), lambda i: (i, 0)),
                      pl.BlockSpec((tb, specs_cb), lambda i: (i, z_coff)), pl.BlockSpec((1, specs_cb), lambda i: (0, 0))],
            out_specs=[pl.BlockSpec((tb, specs_cb), lambda i: (i, 0)), pl.BlockSpec((tb, specs_cb), lambda i: (i, 0)),
                       pl.BlockSpec((HALO, HEAD_DIM), lambda i: (0, 0))],
            out_shape=[jax.ShapeDtypeStruct((T_, DNW), F32), jax.ShapeDtypeStruct((T_, DNW), BF16),
                       jax.ShapeDtypeStruct((HALO, HEAD_DIM), F32)],
            compiler_params=pltpu.CompilerParams(dimension_semantics=("arbitrary",), vmem_limit_bytes=VMEM_LIMIT),
        )(dymix, o, proj, gdn_t)

    return body_wrap()


def _ffn_fwd(up_g, up_v, w_g, w_v, cb):
    T, F = up_g.shape

    def fn(j, i, ug, uv, wg, wv):
        cg = _own(_conv_causal(ug, wg))
        cv = _own(_conv_causal(uv, wv))
        return (cg * _sigmoid(cg) * cv,)

    return _tiled(fn, T=T, C=F, ins=[("ext", up_g, None), ("ext", up_v, None), ("row", w_g, None), ("row", w_v, None)],
                  out_dtypes=[BF16], tb=1024, cb=cb, name="ffn_fwd")[0]


def _ffn_bwd(dact, up_g, up_v, w_g, w_v, cb):
    T, F = up_g.shape
    K = w_g.shape[0]

    def fn(j, i, da, ug, uv, wg, wv):
        cg = _conv_causal(ug, wg)
        cv = _conv_causal(uv, wv)
        sg = _sigmoid(cg)
        dgate = da * cv * (sg * (1.0 + cg * (1.0 - sg)))
        dval = da * (cg * sg)
        return (_own(_conv_anti(dgate, wg)), _own(_conv_anti(dval, wv)), _conv_dw(dgate, ug, K), _conv_dw(dval, uv, K))

    return _tiled(fn, T=T, C=F, ins=[("ext", dact, None), ("ext", up_g, None), ("ext", up_v, None), ("row", w_g, None),
                                      ("row", w_v, None)], out_dtypes=[BF16, BF16], acc_rows=[HALO, HALO], tb=1024, cb=cb,
                  name="ffn_bwd")


def _wide(R, Cc, n_f32, unit=HALO):
    cb = Cc if (Cc % LANES or Cc <= 4096) else _tile(Cc, 2048, LANES)
    cap = max(unit, EW_VMEM_BUDGET // (2 * 4 * n_f32 * cb) // unit * unit)
    return _tile(R, cap, unit), cb


def _adamw(w, g, m, v, name, comm=None):
    R, Cc = w.shape
    tb, cb = _wide(R, Cc, 7) if R % HALO == 0 else (R, _tile(Cc, EW_VMEM_BUDGET // (2 * 4 * 7 * R) // LANES * LANES, LANES))
    c1 = 1.0 / (1.0 - ADAM_B1 ** ADAM_STEP)
    c2 = 1.0 / (1.0 - ADAM_B2 ** ADAM_STEP)

    def fn(j, i, wv, gv, mv, vv):
        m2 = ADAM_B1 * mv + (1.0 - ADAM_B1) * gv
        v2 = ADAM_B2 * vv + (1.0 - ADAM_B2) * (gv * gv)
        delta = -ADAM_LR * ((m2 * c1) / (jnp.sqrt(v2 * c2) + ADAM_EPS) + ADAM_WD * wv)
        return delta, m2, v2

    return _tiled(fn, T=R, C=Cc, ins=[("cur", w, None), ("cur", g, None), ("cur", m, None), ("cur", v, None)],
                  out_dtypes=[F32, F32, F32], tb=tb, cb=cb, name=name, comm=comm)


def _join_shards(w4, n_main):
    S4, R, cs = w4.shape
    n_small = S4 * cs - n_main
    assert 0 < n_small <= LANES and n_small <= cs
    tb = _tile(R, 256, 2 * HALO)

    def body(w_ref, main_ref, small_ref):
        for t in range(S4 - 1):
            main_ref[:, t * cs:(t + 1) * cs] = w_ref[t]
        last = w_ref[S4 - 1]
        main_ref[:, (S4 - 1) * cs:] = last[:, :cs - n_small]
        small_ref[...] = jnp.zeros_like(small_ref)
        small_ref[:, :n_small] = last[:, cs - n_small:]

    return pl.pallas_call(
        body, name="join_w_in", grid=(R // tb,), in_specs=[pl.BlockSpec((S4, tb, cs), lambda i: (0, i, 0))],
        out_specs=[pl.BlockSpec((tb, n_main), lambda i: (i, 0)), pl.BlockSpec((tb, LANES), lambda i: (i, 0))],
        out_shape=[jax.ShapeDtypeStruct((R, n_main), w4.dtype), jax.ShapeDtypeStruct((R, LANES), w4.dtype)],
        compiler_params=pltpu.CompilerParams(dimension_semantics=("parallel",), vmem_limit_bytes=VMEM_LIMIT))(w4)


def _split_shards(main, small, cs):
    R, n_main = main.shape
    n_small = 4 * cs - n_main
    tb = _tile(R, 256, HALO)

    def body(main_ref, small_ref, out_ref):
        for t in range(3):
            out_ref[t] = main_ref[:, t * cs:(t + 1) * cs]
        out_ref[3, :, :cs - n_small] = main_ref[:, 3 * cs:]
        out_ref[3, :, cs - n_small:] = small_ref[:, :n_small]

    return pl.pallas_call(
        body, name="split_g_in", grid=(R // tb,),
        in_specs=[pl.BlockSpec((tb, n_main), lambda i: (i, 0)), pl.BlockSpec((tb, LANES), lambda i: (i, 0))],
        out_specs=pl.BlockSpec((4, tb, cs), lambda i: (0, i, 0)), out_shape=jax.ShapeDtypeStruct((4, R, cs), main.dtype),
        compiler_params=pltpu.CompilerParams(dimension_semantics=("parallel",), vmem_limit_bytes=VMEM_LIMIT))(main, small)


def _sum_stack(st, name):
    S, R, Cc = st.shape
    cb = _tile(Cc, 512, LANES) if Cc % LANES == 0 else Cc

    def fn(j, i, sv):
        t = sv[0]
        for s in range(1, S):
            t = t + sv[s]
        return (t,)

    return _tiled(fn, T=R, C=Cc, ins=[("stack", st, None)], out_dtypes=[F32], cb=cb, name=name)[0]


ANY = pl.BlockSpec(memory_space=pl.ANY)


def _place():
    x, y, c = lax.axis_index("x"), lax.axis_index("y"), lax.axis_index("c")
    return x, y, c, 2 * x + y


def _chip_dev(s, c):
    return (s // 2, s % 2, c)


class _Comm:
    def __init__(self, ins, out_shapes, sems, start, wait, aliases=None):
        self.ins, self.out_shapes, self.sems = list(ins), list(out_shapes), list(sems)
        self.start, self.wait, self.aliases = start, wait, dict(aliases or {})


def _merge(*comms):
    offs, i, o, s = [], 0, 0, 0
    for cm in comms:
        offs.append((i, o, s))
        i, o, s = i + len(cm.ins), o + len(cm.out_shapes), s + len(cm.sems)

    def part(refs, k, cm):
        i0, o0, s0 = offs[k]
        return refs[0][i0:i0 + len(cm.ins)], refs[1][o0:o0 + len(cm.out_shapes)], refs[2][s0:s0 + len(cm.sems)]

    def start(*refs):
        for k, cm in enumerate(comms):
            cm.start(*part(refs, k, cm))

    def wait(*refs):
        for k, cm in enumerate(comms):
            cm.wait(*part(refs, k, cm))

    aliases = {}
    for k, cm in enumerate(comms):
        for a, b in cm.aliases.items():
            aliases[offs[k][0] + a] = offs[k][1] + b
    return _Comm([a for cm in comms for a in cm.ins], [a for cm in comms for a in cm.out_shapes],
                 [a for cm in comms for a in cm.sems], start, wait, aliases)


def _call(body, *, name, grid, in_specs, out_specs, out_shape, scratch_shapes, semantics, args, comm=None, io_aliases=None):
    if comm is None:
        outs = pl.pallas_call(
            body, name=name, grid=grid, in_specs=in_specs, out_specs=out_specs, out_shape=out_shape,
            scratch_shapes=list(scratch_shapes), input_output_aliases=dict(io_aliases or {}),
            compiler_params=pltpu.CompilerParams(dimension_semantics=semantics, vmem_limit_bytes=VMEM_LIMIT))(*args)
        return list(outs), []
    assert not io_aliases
    n_in, n_out, n_scr = len(in_specs), len(out_specs), len(scratch_shapes)
    ci, co = len(comm.ins), len(comm.out_shapes)

    def wrapped(*refs):
        r = 0
        ins, r = refs[r:r + n_in], r + n_in
        cins, r = refs[r:r + ci], r + ci
        outs, r = refs[r:r + n_out], r + n_out
        couts, r = refs[r:r + co], r + co
        scr, r = refs[r:r + n_scr], r + n_scr
        csems = refs[r:]
        ids = [pl.program_id(a) for a in range(len(grid))]
        first, last = ids[0] == 0, ids[0] == grid[0] - 1
        for a in range(1, len(grid)):
            first = jnp.logical_and(first, ids[a] == 0)
            last = jnp.logical_and(last, ids[a] == grid[a] - 1)

        @pl.when(first)
        def _():
            comm.start(cins, couts, csems)

        body(*ins, *outs, *scr)

        @pl.when(last)
        def _():
            comm.wait(cins, couts, csems)

    outs = pl.pallas_call(
        wrapped, name=name, grid=grid, in_specs=list(in_specs) + [ANY] * ci, out_specs=list(out_specs) + [ANY] * co,
        out_shape=list(out_shape) + comm.out_shapes, scratch_shapes=list(scratch_shapes) + comm.sems,
        input_output_aliases={n_in + a: n_out + b for a, b in comm.aliases.items()},
        compiler_params=pltpu.CompilerParams(dimension_semantics=("arbitrary",) * len(grid), vmem_limit_bytes=VMEM_LIMIT),
    )(*args, *comm.ins)
    return list(outs[:n_out]), list(outs[n_out:])


def _run_comm(comm, name):
    ci, co = len(comm.ins), len(comm.out_shapes)

    def body(*refs):
        cins, couts, csems = refs[:ci], refs[ci:ci + co], refs[ci + co:]
        comm.start(cins, couts, csems)
        comm.wait(cins, couts, csems)

    outs = pl.pallas_call(body, name=name, in_specs=[ANY] * ci, out_specs=[ANY] * co, out_shape=comm.out_shapes,
                          scratch_shapes=comm.sems, input_output_aliases=comm.aliases)(*comm.ins)
    return list(outs)


def _ag_comm(shard, land=None, q=0, nq=1):
    two, R2, Cc = shard.shape
    rows = pl.ds(q * (R2 // nq), R2 // nq)
    DMA = pltpu.SemaphoreType.DMA

    def copies(ins, outs, sems, which):
        sh, out = ins[0], outs[0]
        send1, recv1, send2, recv2, send0, recv0 = sems
        x, y, c, s = _place()
        sib = (x, y, 1 - c)
        rc = pltpu.make_async_remote_copy
        if which == "first":
            return [rc(sh.at[c, rows], out.at[s, c, rows], send1.at[m - 1], recv1.at[m - 1],
                       device_id=_chip_dev(s ^ m, c), device_id_type=MESH) for m in range(1, 4)]
        if which == "own":
            return [rc(sh.at[h, rows], out.at[s, h, rows], send0.at[h], recv0.at[h], device_id=sib, device_id_type=MESH)
                    for h in range(2)]
        if which == "landed":
            return [rc(sh.at[c, rows], out.at[s ^ m, c, rows], send1.at[m - 1], recv1.at[m - 1], device_id=sib,
                       device_id_type=MESH) for m in range(1, 4)]
        half = c if which == "passed" else 1 - c
        return [rc(out.at[s ^ m, half, rows], out.at[s ^ m, half, rows], send2.at[m - 1], recv2.at[m - 1], device_id=sib,
                   device_id_type=MESH) for m in range(1, 4)]

    def start(ins, outs, sems):
        for cp in copies(ins, outs, sems, "first") + copies(ins, outs, sems, "own"):
            cp.start()

    def wait(ins, outs, sems):
        passed = copies(ins, outs, sems, "passed")
        for lan, pas in zip(copies(ins, outs, sems, "landed"), passed):
            lan.wait_recv()
            pas.start()
        for cp in copies(ins, outs, sems, "handed"):
            cp.wait_recv()
        for cp in copies(ins, outs, sems, "own"):
            cp.wait()
        for cp in copies(ins, outs, sems, "first") + passed:
            cp.wait_send()

    return _Comm([shard] + ([land] if land is not None else []), [jax.ShapeDtypeStruct((4, two, R2, Cc), shard.dtype)],
                 [DMA((3,)), DMA((3,)), DMA((3,)), DMA((3,)), DMA((2,)), DMA((2,))], start, wait,
                 {1: 0} if land is not None else None)


def _a2a_comm(S1, q=0, nq=1, land=None, cnt=1):
    S4, R2, Cc = S1.shape
    rows = pl.ds(q * (R2 // nq), cnt * (R2 // nq))
    DMA = pltpu.SemaphoreType.DMA

    def copies(ins, outs, sems):
        x, y, c, s = _place()
        return [pltpu.make_async_remote_copy(ins[0].at[s ^ m, rows], outs[0].at[m - 1, rows], sems[0].at[m - 1],
                                             sems[1].at[m - 1], device_id=_chip_dev(s ^ m, c), device_id_type=MESH)
                for m in range(1, 4)]

    def start(ins, outs, sems):
        for cp in copies(ins, outs, sems):
            cp.start()

    def wait(ins, outs, sems):
        for cp in copies(ins, outs, sems):
            cp.wait()

    return _Comm([S1] + ([land] if land is not None else []), [jax.ShapeDtypeStruct((3, R2, Cc), S1.dtype)],
                 [DMA((3,)), DMA((3,))], start, wait, {1: 0} if land is not None else None)


def _halves(G):
    return G.reshape(G.shape[0], 2, G.shape[1] // 2, G.shape[2])


def _swap_comm(piece):
    n, two, R2, Cc = piece.shape
    DMA = pltpu.SemaphoreType.DMA

    def copies(ins, outs, sems):
        x, y, c, s = _place()
        return [pltpu.make_async_remote_copy(ins[0].at[t, 1 - c], outs[0].at[t], sems[0].at[t], sems[1].at[t],
                                             device_id=(x, y, 1 - c), device_id_type=MESH) for t in range(n)]

    def start(ins, outs, sems):
        for cp in copies(ins, outs, sems):
            cp.start()

    def wait(ins, outs, sems):
        for cp in copies(ins, outs, sems):
            cp.wait()

    return _Comm([piece], [jax.ShapeDtypeStruct((n, R2, Cc), piece.dtype)], [DMA((n,)), DMA((n,))], start, wait)


def _add_half(pieces, As, cidx, name):
    R2, Cc = pieces[0].shape[2:]
    S4 = sum(pc.shape[0] for pc in pieces)
    tb, cb = _wide(R2, Cc, 3, 2 * HALO)
    nI, nJ = R2 // tb, Cc // cb

    def body(c_ref, g_ref, a_ref, *rest):
        rest[-1][...] = (g_ref[0, 0] + a_ref[0]).astype(BF16)

    out, t0 = None, 0
    for k, (pc, A) in enumerate(zip(pieces, As)):
        grid_spec = pltpu.PrefetchScalarGridSpec(
            num_scalar_prefetch=1, grid=(pc.shape[0], nI, nJ),
            in_specs=[pl.BlockSpec((1, 1, tb, cb), lambda t, i, j, c_ref: (t, c_ref[0], i, j)),
                      pl.BlockSpec((1, tb, cb), lambda t, i, j, c_ref: (t, i, j))] + ([ANY] if k else []),
            out_specs=pl.BlockSpec((tb, cb), lambda t, i, j, c_ref, t0=t0: ((t0 + t) * nI + i, j)))
        out = pl.pallas_call(
            functools.partial(body), name=f"{name}{k}", grid_spec=grid_spec, out_shape=jax.ShapeDtypeStruct((S4 * R2, Cc), BF16),
            input_output_aliases={3: 0} if k else {},
            compiler_params=pltpu.CompilerParams(dimension_semantics=("parallel", "parallel", "parallel"),
                                                 vmem_limit_bytes=VMEM_LIMIT),
        )(*((cidx, pc, A) + ((out,) if k else ())))
        t0 += pc.shape[0]
    return out.reshape(S4, R2, Cc)


def _add_own(S1, B, chip_idx, cidx, name):
    S4, R2, Cc = S1.shape
    tb, cb = _wide(R2, Cc, 3, 2 * HALO)

    def body(s_idx, c_idx, s_ref, b_ref, o_ref):
        o_ref[...] = ((s_ref[0].astype(F32) + b_ref[0].astype(F32)) + b_ref[1].astype(F32)) + b_ref[2].astype(F32)

    grid_spec = pltpu.PrefetchScalarGridSpec(
        num_scalar_prefetch=2, grid=(R2 // tb, Cc // cb),
        in_specs=[pl.BlockSpec((1, tb, cb), lambda i, j, s_idx, c_idx: (s_idx[0], i, j)),
                  pl.BlockSpec((3, tb, cb), lambda i, j, s_idx, c_idx: (0, i, j))],
        out_specs=pl.BlockSpec((None, tb, cb), lambda i, j, s_idx, c_idx: (c_idx[0], i, j)))
    return pl.pallas_call(body, name=name, grid_spec=grid_spec, out_shape=jax.ShapeDtypeStruct((2, R2, Cc), F32),
                          compiler_params=pltpu.CompilerParams(dimension_semantics=("parallel", "parallel"),
                                                               vmem_limit_bytes=VMEM_LIMIT))(chip_idx, cidx, S1, B)


def _fill_comm(Hs):
    def copy(ins, outs, sems):
        x, y, c, s = _place()
        return pltpu.make_async_remote_copy(ins[0].at[c], outs[0].at[c], sems[0], sems[1], device_id=(x, y, 1 - c),
                                            device_id_type=MESH)

    return _Comm([Hs], [jax.ShapeDtypeStruct(Hs.shape, Hs.dtype)], [pltpu.SemaphoreType.DMA, pltpu.SemaphoreType.DMA],
                 lambda *r: copy(*r).start(), lambda *r: copy(*r).wait(), {0: 0})


def _gather_all(buf, name):
    R, Cc = buf.shape

    def body(b_ref, out_ref, send, recv, local):
        x, y, c, s = _place()
        d = 2 * s + c
        mine = pltpu.make_async_copy(b_ref, out_ref.at[d], local)
        mine.start()
        cps = []
        for m in range(1, 8):
            t = d ^ m
            cp = pltpu.make_async_remote_copy(b_ref, out_ref.at[d], send.at[m - 1], recv.at[m - 1],
                                              device_id=(t // 4, (t // 2) % 2, t % 2), device_id_type=MESH)
            cp.start()
            cps.append(cp)
        for cp in cps:
            cp.wait()
        mine.wait()

    return pl.pallas_call(
        body, name=name, in_specs=[ANY], out_specs=ANY, out_shape=jax.ShapeDtypeStruct((8, R, Cc), buf.dtype),
        scratch_shapes=[pltpu.SemaphoreType.DMA((7,)), pltpu.SemaphoreType.DMA((7,)), pltpu.SemaphoreType.DMA],
    )(buf)


def _pack_rows(vs):
    flat = jnp.concatenate([v.reshape(-1) for v in vs])
    n = flat.shape[0]
    rows = -(-n // (LANES * 2 * HALO)) * 2 * HALO
    return jnp.pad(flat, (0, rows * LANES - n)).reshape(rows, LANES)


def _unpack_rows(buf, shapes):
    flat = buf.reshape(-1)
    outs, o = [], 0
    for shp in shapes:
        n = 1
        for d in shp:
            n *= d
        outs.append(flat[o:o + n].reshape(shp))
        o += n
    return outs


def kernel(x, p, norm_mix_g, w_in, conv_a_w, conv_qkv_w, a_log, dt_bias, dn_norm_g, w_out, norm_ffn_g, w_up, conv_ffn_w, w_down, norm_ple_g, w_ple_gate, w_ple_proj, final_norm_g, loss_target, m_norm_mix_g, m_w_in, m_conv_a_w, m_conv_qkv_w, m_a_log, m_dt_bias, m_dn_norm_g, m_w_out, m_norm_ffn_g, m_w_up, m_conv_ffn_w, m_w_down, m_norm_ple_g, m_w_ple_gate, m_w_ple_proj, m_final_norm_g, v_norm_mix_g, v_w_in, v_conv_a_w, v_conv_qkv_w, v_a_log, v_dt_bias, v_dn_norm_g, v_w_out, v_norm_ffn_g, v_w_up, v_conv_ffn_w, v_w_down, v_norm_ple_g, v_w_ple_gate, v_w_ple_proj, v_final_norm_g):
    xs = x[0]
    ps = p[0, 0]
    tgt = loss_target[0]
    T, D = xs.shape
    H = a_log.shape[-1]
    DNW = H * HEAD_DIM
    CW = conv_a_w.shape[-1] * 4
    F = w_down.shape[1] * 4
    PD = ps.shape[-1]
    IN_MAIN = 3 * CW + 4 * DNW
    IN_COLS = IN_MAIN + 2 * H
    assert w_in.shape[-1] * 4 == IN_COLS and CW + DNW == D and 2 * H <= LANES
    cb = _tile(min(CW, DNW), 512, LANES)
    while F % cb:
        cb -= LANES
    cidx = lax.axis_index("c").astype(jnp.int32).reshape(1)
    chip = 2 * lax.axis_index("x") + lax.axis_index("y")

    def halves(w):
        sh = w[0].astype(BF16)
        return sh.reshape(2, sh.shape[0] // 2, sh.shape[1])

    def whole(land):
        return land.reshape(4, 2 * land.shape[2], land.shape[3])

    def rows(g4):
        return g4.reshape(4 * g4.shape[1], g4.shape[2])

    conv_shapes = [conv_a_w[0].shape, conv_qkv_w[0].shape, conv_ffn_w[0].shape]
    cpack = _pack_rows([conv_a_w[0], conv_qkv_w[0], conv_ffn_w[0]])
    sh_in, sh_out, sh_up, sh_down, sh_pg, sh_pp = (halves(w) for w in (w_in, w_out, w_up, w_down, w_ple_gate, w_ple_proj))
    l_in, cg = _run_comm(_merge(_ag_comm(sh_in), _ag_comm(cpack.reshape(2, cpack.shape[0] // 2, LANES))), "ag_w_in_conv")
    w_in_main, w_in_small = _join_shards(whole(l_in), IN_MAIN)
    cg = cg.reshape(4, cpack.shape[0], LANES)
    parts = [_unpack_rows(cg[t], conv_shapes) for t in range(4)]
    cw_a = jnp.concatenate([parts[t][0] for t in range(4)], axis=1)
    cw_qkv = jnp.concatenate([parts[t][1] for t in range(4)], axis=1)
    cw_ffn = jnp.concatenate([parts[t][2] for t in range(4)], axis=1)
    cw_q, cw_k, cw_v = cw_qkv[:, :DNW], cw_qkv[:, DNW:2 * DNW], cw_qkv[:, 2 * DNW:]
    cw_fg, cw_fv = cw_ffn[:, :F], cw_ffn[:, F:]
    pad_row = lambda v: jnp.pad(v, ((0, 0), (0, LANES - v.shape[1])))
    a_log_row, dt_row = pad_row(a_log), pad_row(dt_bias)
    gdn_t = jnp.tile(dn_norm_g, (1, H))
    gfin = final_norm_g.reshape(1, D)

    h1 = _rms_fwd(xs, norm_mix_g, "rms1")
    proj, (l_up,) = _mm(h1, w_in_main, mode="nn", out_dtypes=[F32], name="mm_proj", comm=_ag_comm(sh_up, q=0, nq=2))
    small = _mm(h1, w_in_small, mode="nn", out_dtypes=[F32], name="mm_small")
    ymix = _ga_fwd(proj, cw_a, CW, cb, D)
    nq = 3 * CW // cb
    nd = DNW // cb
    qn, (l_out,) = _qkv_fwd(proj, cw_q, nq, True, DNW, cb, "q_fwd", comm=_ag_comm(sh_out, q=0, nq=2))
    kn, (l_out,) = _qkv_fwd(proj, cw_k, nq + nd, True, DNW, cb, "k_fwd", comm=_ag_comm(sh_out, l_out, q=1, nq=2))
    vs = _qkv_fwd(proj, cw_v, nq + 2 * nd, False, DNW, cb, "v_fwd")
    g, beta = _gb_fwd(small, a_log_row, dt_row, H)
    o, S0, inv_c, (l_up,) = _delta_fwd(qn, kn, vs, g, beta, comm=_ag_comm(sh_up, l_up, q=1, nq=2))
    w_out_f = rows(whole(l_out))
    w_up_4 = whole(l_up)
    z_coff = (3 * CW + 3 * DNW) // DNW
    assert (3 * CW + 3 * DNW) % DNW == 0 and CW % DNW == 0
    ymix = _gnorm_fwd(o, proj, z_coff, gdn_t, DNW, ymix, CW // DNW)
    add = lambda acc, r: (r + acc,)

    def out_epi(acc, xv, gv):
        x1v = xv + acc
        return x1v, x1v * lax.rsqrt(jnp.mean(x1v * x1v, axis=1, keepdims=True) + EPS) * gv

    x1, h2 = _mm(ymix, w_out_f, mode="nn", out_dtypes=[F32, BF16], epi=out_epi, extras=[xs], rows=[norm_ffn_g], name="mm_out")
    up_g, (l_down,) = _mm(h2, w_up_4, mode="nn", b_split=(0, 2), out_dtypes=[F32], name="mm_up_g",
                          comm=_ag_comm(sh_down, q=0, nq=2))
    up_v, (l_down,) = _mm(h2, w_up_4, mode="nn", b_split=(2, 2), out_dtypes=[F32], name="mm_up_v",
                          comm=_ag_comm(sh_down, l_down, q=1, nq=2))
    w_down_f = rows(whole(l_down))
    act = _ffn_fwd(up_g, up_v, cw_fg, cw_fv, cb)
    x2, (l_pg, l_pp) = _mm(act, w_down_f, mode="nn", out_dtypes=[F32], epi=add, extras=[x1], name="mm_down",
                           comm=_merge(_ag_comm(sh_pg), _ag_comm(sh_pp)))
    w_pg_f = rows(whole(l_pg))
    w_pp_4 = whole(l_pp)
    h3 = _rms_fwd(x2, norm_ple_g, "rms3")
    pp = _mm(ps, w_pp_4, mode="nn", b_split=(0, 4), out_dtypes=[F32], name="mm_pp")

    def ple_epi(acc, x2v, ppv):
        pg = _sigmoid(acc)
        return x2v + pg * ppv, pg

    x3, pg = _mm(h3, w_pg_f, mode="nn", out_dtypes=[F32, F32], epi=ple_epi, extras=[x2, pp], name="mm_pg")

    dx3, dpg, dpp, fin = _final_fb(x3, tgt, gfin, pp, pg)
    loss = lax.psum(jnp.sum(fin[1]), ("x", "y", "c"))
    d_gfin = fin[0:1]
    def split_rows(dW):
        return dW.reshape(4, dW.shape[0] // 4, dW.shape[1])

    chip_idx = chip.astype(jnp.int32).reshape(1)
    own_sum = lambda S1, B, name: _add_own(S1, B, chip_idx, cidx, "rs_" + name + "_sum")

    dW_pp = _mm(ps, dpp, mode="tn", out_split=4, out_dtypes=[F32], name="mm_dw_pp")
    dW_pg = _mm(h3, dpg, mode="tn", out_dtypes=[F32], name="mm_dw_pg")
    P_pp, P_pg = _halves(dW_pp), _halves(split_rows(dW_pg))
    dh3, (A_pp, A_pg) = _mm(dpg, w_pg_f, mode="nt", out_dtypes=[F32], name="mm_dh3",
                            comm=_merge(_swap_comm(P_pp), _swap_comm(P_pg)))
    S_pp = _add_half([P_pp], [A_pp], cidx, "rs_w_pp_add")
    S_pg = _add_half([P_pg], [A_pg], cidx, "rs_w_pg_add")
    dx2, dx2_b, d_gple = _rms_bwd(dh3, x2, norm_ple_g, dx3, "rms3_bwd")
    dW_down, (B_pp, B_pg) = _mm(act, dx2_b, mode="tn", out_dtypes=[F32], name="mm_dw_down",
                                comm=_merge(_a2a_comm(S_pp), _a2a_comm(S_pg)))
    P_down = _halves(split_rows(dW_down))
    dact, (A_down, F_pp, F_pg) = _mm(dx2_b, w_down_f, mode="nt", out_dtypes=[F32], name="mm_dact", comm=_merge(
        _swap_comm(P_down), _fill_comm(own_sum(S_pp, B_pp, "w_pp")), _fill_comm(own_sum(S_pg, B_pg, "w_pg"))))
    S_down = _add_half([P_down], [A_down], cidx, "rs_w_down_add")
    dup_g, dup_v, dcw_fg, dcw_fv = _ffn_bwd(dact, up_g, up_v, cw_fg, cw_fv, cb)
    dW_up_g, (B_down,) = _mm(h2, dup_g, mode="tn", out_split=2, out_dtypes=[F32], name="mm_dw_up_g", comm=_a2a_comm(S_down))
    P_ug = _halves(dW_up_g)
    dW_up_v, (A_ug, F_down) = _mm(h2, dup_v, mode="tn", out_split=2, out_dtypes=[F32], name="mm_dw_up_v",
                                  comm=_merge(_swap_comm(P_ug), _fill_comm(own_sum(S_down, B_down, "w_down"))))
    P_uv = _halves(dW_up_v)
    dh2, (A_uv,) = _mm(dup_g, w_up_4, mode="nt", b_split=(0, 2), out_dtypes=[F32], name="mm_dh2_g", comm=_swap_comm(P_uv))
    S_up = _add_half([P_ug, P_uv], [A_ug, A_uv], cidx, "rs_w_up_add")
    dh2, (B_up,) = _mm(dup_v, w_up_4, mode="nt", b_split=(2, 2), out_dtypes=[F32], epi=add, extras=[dh2], name="mm_dh2_v",
                       comm=_a2a_comm(S_up, 0, 2))
    dx1, dx1_b, d_gffn = _rms_bwd(dh2, x1, norm_ffn_g, dx2, "rms2_bwd")
    P_out = _halves(split_rows(_mm(ymix, dx1_b, mode="tn", out_dtypes=[F32], name="mm_dw_out")))
    dymix, (A_out,) = _mm(dx1_b, w_out_f, mode="nt", out_dtypes=[F32], name="mm_dymix", comm=_swap_comm(P_out))
    S_out = _add_half([P_out], [A_out], cidx, "rs_w_out_add")
    dax, dab, dac, dcw_a = _ga_bwd(dymix, proj, cw_a, CW, cb)
    do, dz, d_gdn = _gnorm_bwd(dymix, CW // DNW, o, proj, z_coff, gdn_t, DNW)
    dqn, dkn, dvs, dgB, dbB, (B_up, B_out) = _delta_bwd(qn, kn, vs, g, beta, S0, inv_c, do,
                                                        comm=_merge(_a2a_comm(S_up, 1, 2, B_up), _a2a_comm(S_out)))
    dq_pre, dcw_q = _qkv_bwd(dqn, proj, cw_q, nq, True, DNW, cb, "q_bwd")
    dk_pre, dcw_k = _qkv_bwd(dkn, proj, cw_k, nq + nd, True, DNW, cb, "k_bwd")
    dv_pre, dcw_v = _qkv_bwd(dvs, proj, cw_v, nq + 2 * nd, False, DNW, cb, "v_bwd")
    dsmall, d_ab = _gb_bwd(dgB, dbB, small, g, beta, a_log_row, dt_row, H)
    dproj = jnp.concatenate([dax, dab, dac, dq_pre, dk_pre, dv_pre, dz], axis=1)
    dW_in_main, (F_up, F_out) = _mm(h1, dproj, mode="tn", out_dtypes=[F32], name="mm_dw_in", comm=_merge(
        _fill_comm(own_sum(S_up, B_up, "w_up")), _fill_comm(own_sum(S_out, B_out, "w_out"))))
    dW_in_small = _mm(h1, dsmall, mode="tn", out_dtypes=[F32], name="mm_dw_in_small")
    def update(Hf, w, m, v, name):
        gr = Hf.reshape(2 * Hf.shape[1], Hf.shape[2])
        if gr.shape[1] % LANES == 0:
            delta, m2, v2 = _adamw(w[0], gr, m[0], v[0], "adamw_" + name)
            return gr[None], delta[None], m2[None], v2[None]
        tr = jnp.transpose
        grt = tr(gr)
        delta, m2, v2 = _adamw(tr(w[0]), grt, tr(m[0]), tr(v[0]), "adamw_" + name)
        return tr(grt)[None], tr(delta)[None], tr(m2)[None], tr(v2)[None]

    P_in = _halves(_split_shards(dW_in_main, dW_in_small, IN_COLS // 4))
    (A_in,) = _run_comm(_swap_comm(P_in), "rs_w_in_swap")
    S_in = _add_half([P_in], [A_in], cidx, "rs_w_in_add")
    dh1, (B_in,) = _mm(dproj, w_in_main, mode="nt", out_dtypes=[F32], name="mm_dh1", comm=_a2a_comm(S_in, 0, 8, cnt=7))
    dh1, (B_in,) = _mm(dsmall, w_in_small, mode="nt", out_dtypes=[F32], epi=add, extras=[dh1], name="mm_dh1_small",
                       comm=_a2a_comm(S_in, 7, 8, B_in))
    dx, _, d_gmix = _rms_bwd(dh1, xs, norm_mix_g, dx1, "rms1_bwd")

    (F_in,) = _run_comm(_fill_comm(own_sum(S_in, B_in, "w_in")), "rs_w_in_gather")
    big = {
        "w_in": update(F_in, w_in, m_w_in, v_w_in, "w_in"),
        "w_out": update(F_out, w_out, m_w_out, v_w_out, "w_out"),
        "w_up": update(F_up, w_up, m_w_up, v_w_up, "w_up"),
        "w_down": update(F_down, w_down, m_w_down, v_w_down, "w_down"),
        "w_ple_gate": update(F_pg, w_ple_gate, m_w_ple_gate, v_w_ple_gate, "w_pg"),
        "w_ple_proj": update(F_pp, w_ple_proj, m_w_ple_proj, v_w_ple_proj, "w_pp"),
    }

    small_grads = [d_gmix[0:1], dcw_a[:cw_a.shape[0]], jnp.concatenate([dcw_q, dcw_k, dcw_v], axis=1)[:cw_qkv.shape[0]],
                   d_ab[0:1, :H], d_ab[1:2, :H], d_gdn[0:1], d_gffn[0:1],
                   jnp.concatenate([dcw_fg, dcw_fv], axis=1)[:cw_ffn.shape[0]], d_gple[0:1], d_gfin]
    small_shapes = [v.shape for v in small_grads]
    gpack = _pack_rows(small_grads)
    gsum = _sum_stack(_gather_all(gpack, "ag_small"), "sum_small")
    (g_gmix, g_cwa, g_cwqkv, g_alog, g_dt, g_gdn, g_gffn, g_cwffn, g_gple, g_gfin) = _unpack_rows(gsum, small_shapes)

    def my_cols(v):
        Cc = v.shape[1] // 4
        return lax.dynamic_slice_in_dim(v, chip * Cc, Cc, axis=1)

    g_small = [g_gmix, my_cols(g_cwa), my_cols(g_cwqkv), g_alog, g_dt, g_gdn, g_gffn, my_cols(g_cwffn), g_gple, g_gfin]
    w_small = [norm_mix_g, conv_a_w[0], conv_qkv_w[0], a_log, dt_bias, dn_norm_g, norm_ffn_g, conv_ffn_w[0], norm_ple_g, gfin]
    m_small = [m_norm_mix_g, m_conv_a_w[0], m_conv_qkv_w[0], m_a_log, m_dt_bias, m_dn_norm_g, m_norm_ffn_g, m_conv_ffn_w[0],
               m_norm_ple_g, m_final_norm_g.reshape(1, D)]
    v_small = [v_norm_mix_g, v_conv_a_w[0], v_conv_qkv_w[0], v_a_log, v_dt_bias, v_dn_norm_g, v_norm_ffn_g, v_conv_ffn_w[0],
               v_norm_ple_g, v_final_norm_g.reshape(1, D)]
    shp = [v.shape for v in w_small]
    ds_, ms_, vs_ = _adamw(_pack_rows(w_small), _pack_rows(g_small), _pack_rows(m_small), _pack_rows(v_small), "adamw_small")
    out_shapes = [norm_mix_g.shape, conv_a_w.shape, conv_qkv_w.shape, a_log.shape, dt_bias.shape, dn_norm_g.shape,
                  norm_ffn_g.shape, conv_ffn_w.shape, norm_ple_g.shape, final_norm_g.shape]
    rs = lambda vals: [v.reshape(s) for v, s in zip(vals, out_shapes)]
    sg, sd_, sm_, sv_ = rs(g_small), rs(_unpack_rows(ds_, shp)), rs(_unpack_rows(ms_, shp)), rs(_unpack_rows(vs_, shp))
    names_small = ["norm_mix_g", "conv_a_w", "conv_qkv_w", "a_log", "dt_bias", "dn_norm_g", "norm_ffn_g", "conv_ffn_w",
                   "norm_ple_g", "final_norm_g"]
    res = {n: (sg[i], sd_[i], sm_[i], sv_[i]) for i, n in enumerate(names_small)}
    res.update(big)
    order = ["norm_mix_g", "w_in", "conv_a_w", "conv_qkv_w", "a_log", "dt_bias", "dn_norm_g", "w_out", "norm_ffn_g", "w_up",
             "conv_ffn_w", "w_down", "norm_ple_g", "w_ple_gate", "w_ple_proj", "final_norm_g"]
    return (loss, dx[None], *[res[n][0] for n in order], *[res[n][1] for n in order], *[res[n][2] for n in order],
            *[res[n][3] for n in order])
```

```python
import functools

import jax
import jax.numpy as jnp
from jax import lax
from jax.experimental import pallas as pl
from jax.experimental.pallas import tpu as pltpu

F32 = jnp.float32
BF16 = jnp.bfloat16
LANES = 128
HALO = 8
HEAD_DIM = 128
CHUNK = 64
EPS = 1e-6
VMEM_LIMIT = 56 * 1024 * 1024
MM_VMEM_BUDGET = 40 * 1024 * 1024
MM_STEP_BYTES = 1 << 20
EW_VMEM_BUDGET = 28 * 1024 * 1024
MESH = pl.DeviceIdType.MESH

ADAM_LR, ADAM_B1, ADAM_B2, ADAM_EPS, ADAM_WD, ADAM_STEP = 0.001, 0.9, 0.999, 1e-08, 0.01, 10


def _tile(n, cap, unit):
    if n <= cap:
        return n
    d = (cap // unit) * unit
    while d >= unit:
        if n % d == 0:
            return d
        d -= unit
    raise ValueError(f"no tile for {n} (cap {cap}, unit {unit})")


def _sigmoid(x):
    return 1.0 / (1.0 + jnp.exp(-x))


def _divisors(n, cap):
    ds = [d for d in range(cap // LANES * LANES, 0, -LANES) if n % d == 0]
    return [n] if (n <= cap or not ds) else ds


def _mm_tiles(M, N, K, n_unit, k_unit, a_bytes, n_blocks_mn, a_transposed, tn_full=False):
    best = None
    for tm in _divisors(M, 1536):
        for tn in ([N] if tn_full else _divisors(n_unit, 1536)):
            for tk in _divisors(k_unit, 4096):
                nk = K // tk
                vmem = 2 * tm * tk * a_bytes + 2 * tk * tn * 2 + 2 * 4 * tm * tn * n_blocks_mn + (4 * tm * tn if nk > 1 else 0)
                if vmem > MM_VMEM_BUDGET:
                    continue
                steps = (M // tm) * (N // tn) * nk
                b_reads = 1 if (nk == 1 and N == tn) else M // tm
                cost = (M * K * a_bytes * (N // tn if nk > 1 else 1) + K * N * 2 * b_reads + 4 * M * N * n_blocks_mn
                        + (8 * M * N * nk // 3 if nk > 1 else 0) + steps * MM_STEP_BYTES
                        + (2 * steps * tm * tk if a_transposed else 0))
                if best is None or cost < best[0]:
                    best = (cost, tm, tn, tk)
    return best[1:]


def _mm(a, b, *, mode, out_dtypes, name, epi=None, extras=(), comm=None, b_split=None, out_split=None, rows=(), parts=0):
    if b_split is not None:
        lo, ns = b_split
        Rb, Cb = b.shape[1], b.shape[2]
    if mode == "nn":
        (M, K), N = a.shape, (ns * Cb if b_split else b.shape[1])
    elif mode == "nt":
        (M, K), N = a.shape, (Rb if b_split else b.shape[0])
    else:
        (K, M), N = a.shape, b.shape[1]
    n_ex, n_out = len(extras), len(out_dtypes)
    n_unit = Cb if (b_split and mode == "nn") else (N // out_split if out_split else N)
    n_rows = len(rows)
    assert not (n_rows and (b_split or out_split))
    k_unit = Cb if (b_split and mode == "nt") else K
    mn_blocks = (sum(e.dtype.itemsize for e in extras) + sum(jnp.dtype(d).itemsize for d in out_dtypes)) / 4
    tm, tn, tk = _mm_tiles(M, N, K, n_unit, k_unit, a.dtype.itemsize, mn_blocks, mode == "tn", tn_full=bool(n_rows))
    nk = K // tk
    a_spec = pl.BlockSpec((tk, tm), lambda i, j, k: (k, i)) if mode == "tn" else pl.BlockSpec((tm, tk), lambda i, j, k: (i, k))
    if b_split and mode == "nn":
        nb = Cb // tn
        b_spec = pl.BlockSpec((None, tk, tn), lambda i, j, k: (lo + j // nb, k, j % nb))
    elif b_split:
        nb = Cb // tk
        b_spec = pl.BlockSpec((None, tn, tk), lambda i, j, k: (lo + k // nb, j, k % nb))
    else:
        b_spec = pl.BlockSpec((tn, tk), lambda i, j, k: (j, k)) if mode == "nt" else pl.BlockSpec((tk, tn), lambda i, j, k: (k, j))
    mn_spec = pl.BlockSpec((tm, tn), lambda i, j, k: (i, j))
    out_shapes = [jax.ShapeDtypeStruct((M, N), dt) for dt in out_dtypes] + [jax.ShapeDtypeStruct((M // tm * HALO, N), F32)] * parts
    out_specs = [mn_spec] * n_out + [pl.BlockSpec((HALO, tn), lambda i, j, k: (i, j))] * parts
    if out_split:
        assert n_ex == 0 and n_out == 1
        nbo = (N // out_split) // tn
        out_specs = [pl.BlockSpec((None, tm, tn), lambda i, j, k: (j // nbo, i, j % nbo))]
        out_shapes = [jax.ShapeDtypeStruct((out_split, M, N // out_split), out_dtypes[0])]
    dims = {"nn": (((1,), (0,)), ((), ())), "nt": (((1,), (1,)), ((), ())), "tn": (((0,), (0,)), ((), ()))}[mode]

    def body(*refs):
        a_ref, b_ref = refs[0], refs[1]
        ex_refs = refs[2:2 + n_ex + n_rows]
        out_refs = refs[2 + n_ex + n_rows:2 + n_ex + n_rows + n_out + parts]
        part = lax.dot_general(a_ref[...].astype(BF16), b_ref[...].astype(BF16), dims, preferred_element_type=F32)

        def finish(acc):
            outs = (acc,) if epi is None else epi(acc, *[r[...] for r in ex_refs])
            for r, o in zip(out_refs, outs):
                r[...] = o.astype(r.dtype)

        if nk == 1:
            finish(part)
            return
        acc_ref = refs[-1]
        k = pl.program_id(2)

        @pl.when(k == 0)
        def _():
            acc_ref[...] = part

        @pl.when(jnp.logical_and(k > 0, k < nk - 1))
        def _():
            acc_ref[...] += part

        @pl.when(k == nk - 1)
        def _():
            finish(acc_ref[...] + part)

    outs, comm_outs = _call(
        body, name=name, grid=(M // tm, N // tn, nk),
        in_specs=[a_spec, b_spec] + [mn_spec] * n_ex + [pl.BlockSpec((1, tn), lambda i, j, k: (0, j))] * n_rows,
        out_specs=out_specs,
        out_shape=out_shapes,
        scratch_shapes=[pltpu.VMEM((tm, tn), F32)] if nk > 1 else [],
        semantics=("parallel", "parallel", "arbitrary"), args=(a, b, *extras, *rows), comm=comm)
    res = outs[0] if n_out + parts == 1 else outs
    return res if comm is None else (res, comm_outs)


def _tiled(fn, *, T, C, ins, out_dtypes=(), acc_rows=(), tb=None, cb=512, name, comm=None, into=None):
    tb = _tile(T, tb or (512 if cb <= 1024 else 256), HALO)
    nI, nJ = T // tb, C // cb
    hb, nH = tb // HALO, T // HALO
    specs, args, kinds = [], [], []
    for kind, arr, cmap in ins:
        cm = cmap if cmap is not None else (lambda j: j)
        kinds.append(kind)
        if kind == "cur":
            specs.append(pl.BlockSpec((tb, cb), lambda j, i, cm=cm: (i, cm(j))))
            args.append(arr)
        elif kind == "ext":
            specs.append(pl.BlockSpec((HALO, cb), lambda j, i, cm=cm: (jnp.maximum(i * hb - 1, 0), cm(j))))
            specs.append(pl.BlockSpec((tb, cb), lambda j, i, cm=cm: (i, cm(j))))
            specs.append(pl.BlockSpec((HALO, cb), lambda j, i, cm=cm: (jnp.minimum((i + 1) * hb, nH - 1), cm(j))))
            args += [arr, arr, arr]
        elif kind == "row":
            specs.append(pl.BlockSpec((arr.shape[0], cb), lambda j, i, cm=cm: (0, cm(j))))
            args.append(arr)
        elif kind == "stack":
            specs.append(pl.BlockSpec((arr.shape[0], tb, cb), lambda j, i, cm=cm: (0, i, cm(j))))
            args.append(arr)
        else:
            raise ValueError(kind)
    n_in = len(args)
    n_out, n_acc = len(out_dtypes), len(acc_rows)

    def body(*refs):
        j, i = pl.program_id(0), pl.program_id(1)
        vals, r = [], 0
        for kind in kinds:
            if kind == "ext":
                prev = jnp.where(i == 0, 0.0, refs[r][...].astype(F32))
                cur = refs[r + 1][...].astype(F32)
                nxt = jnp.where(i == nI - 1, 0.0, refs[r + 2][...].astype(F32))
                vals.append(jnp.concatenate([prev, cur, nxt], axis=0))
                r += 3
            else:
                vals.append(refs[r][...])
                r += 1
        res = fn(j, i, *vals)
        for ref, o in zip(refs[n_in:n_in + n_out], res[:n_out]):
            ref[...] = o.astype(ref.dtype)
        for ref, o in zip(refs[n_in + n_out:], res[n_out:]):
            @pl.when(i == 0)
            def _(ref=ref, o=o):
                ref[...] = o

            @pl.when(i > 0)
            def _(ref=ref, o=o):
                ref[...] += o

    out_specs = [pl.BlockSpec((tb, cb), lambda j, i: (i, j))] * n_out
    out_shape = [jax.ShapeDtypeStruct((T, C), dt) for dt in out_dtypes]
    io_aliases = None
    if into is not None:
        buf, total, off = into
        assert n_out == 1 and comm is None
        out_specs = [pl.BlockSpec((tb, cb), lambda j, i: (i, j + off))]
        out_shape = [jax.ShapeDtypeStruct((T, total), out_dtypes[0])]
        if buf is not None:
            specs, args, io_aliases = specs + [ANY], args + [buf], {n_in: 0}
            n_in += 1
    outs, comm_outs = _call(
        body, name=name, grid=(nJ, nI), in_specs=specs,
        out_specs=out_specs + [pl.BlockSpec((rows, cb), lambda j, i: (0, j)) for rows in acc_rows],
        out_shape=out_shape + [jax.ShapeDtypeStruct((rows, C), F32) for rows in acc_rows],
        scratch_shapes=[], semantics=("parallel", "arbitrary"), args=args, comm=comm, io_aliases=io_aliases)
    return outs if comm is None else (outs, comm_outs)


def _conv_causal(xe, w):
    K = w.shape[0]
    y = xe * w[K - 1:K]
    for j in range(K - 1):
        y = y + pltpu.roll(xe, K - 1 - j, 0) * w[j:j + 1]
    return y


def _conv_anti(de, w):
    K, n = w.shape[0], de.shape[0]
    y = de * w[K - 1:K]
    for j in range(K - 1):
        y = y + pltpu.roll(de, n - (K - 1 - j), 0) * w[j:j + 1]
    return y


def _conv_dw(dce, xe, K):
    n = dce.shape[0]
    tb = n - 2 * HALO
    rows = []
    for j in range(K):
        xs = xe if j == K - 1 else pltpu.roll(xe, K - 1 - j, 0)
        rows.append(jnp.sum((dce * xs)[HALO:HALO + tb], axis=0, keepdims=True))
    rows.append(jnp.zeros((HALO - K, dce.shape[1]), F32))
    return jnp.concatenate(rows, axis=0)


def _own(xe):
    return xe[HALO:xe.shape[0] - HALO]


def _row0(v):
    return jnp.concatenate([v, jnp.zeros((HALO - 1, v.shape[1]), F32)], axis=0)


def _per_head(fn, *xs):
    n = xs[0].shape[1] // HEAD_DIM
    outs = [fn(*[x[:, g * HEAD_DIM:(g + 1) * HEAD_DIM] for x in xs]) for g in range(n)]
    return outs[0] if n == 1 else jnp.concatenate(outs, axis=1)


def _rms_fwd(x, g, name):
    T, D = x.shape

    def fn(j, i, xv, gv):
        r = lax.rsqrt(jnp.mean(xv * xv, axis=1, keepdims=True) + EPS)
        return (xv * r * gv,)

    return _tiled(fn, T=T, C=D, ins=[("cur", x, None), ("row", g, None)], out_dtypes=[BF16], cb=D, name=name)[0]


def _rms_bwd_math(dy, xv, gv):
    r = lax.rsqrt(jnp.mean(xv * xv, axis=1, keepdims=True) + EPS)
    xh = xv * r
    dxh = dy * gv
    dx = r * (dxh - xh * jnp.mean(dxh * xh, axis=1, keepdims=True))
    dg = jnp.sum(dy * xh, axis=0, keepdims=True)
    return dx, dg


def _rms_bwd(dh, x, g, dres, name, comm=None):
    T, D = x.shape

    def fn(j, i, dhv, xv, gv, dr):
        dx, dg = _rms_bwd_math(dhv, xv, gv)
        return dr + dx, dr + dx, _row0(dg)

    return _tiled(fn, T=T, C=D, ins=[("cur", dh, None), ("cur", x, None), ("row", g, None), ("cur", dres, None)],
                  out_dtypes=[F32, BF16], acc_rows=[HALO], cb=D, name=name, comm=comm)


def _ga_fwd(proj, w_a, CW, cb, total):
    T = proj.shape[0]
    n = CW // cb

    def fn(j, i, ax, ab, ac, w):
        c = _conv_causal(ac * ax, w)
        return (ab * _own(c),)

    return _tiled(fn, T=T, C=CW, ins=[("ext", proj, None), ("cur", proj, lambda j: j + n), ("ext", proj, lambda j: j + 2 * n),
                                       ("row", w_a, None)], out_dtypes=[BF16], cb=cb, name="ga_fwd", into=(None, total, 0))[0]


def _ga_bwd(dymix, proj, w_a, CW, cb):
    T = proj.shape[0]
    n = CW // cb
    K = w_a.shape[0]

    def fn(j, i, dy, ax, ab, ac, w):
        u = ac * ax
        c = _conv_causal(u, w)
        dc = dy * ab
        du = _conv_anti(dc, w)
        return _own(du * ac), _own(dy * c), _own(du * ax), _conv_dw(dc, u, K)

    return _tiled(fn, T=T, C=CW, ins=[("ext", dymix, None), ("ext", proj, None), ("ext", proj, lambda j: j + n),
                                       ("ext", proj, lambda j: j + 2 * n), ("row", w_a, None)],
                  out_dtypes=[BF16, BF16, BF16], acc_rows=[HALO], cb=cb, name="ga_bwd")


def _l2n(s):
    return s * lax.rsqrt(jnp.sum(s * s, axis=1, keepdims=True) + EPS)


def _qkv_fwd(proj, w_sec, coff, normalize, DNW, cb, name, comm=None):
    T = proj.shape[0]

    def fn(j, i, pre, w):
        c = _own(_conv_causal(pre, w))
        s = c * _sigmoid(c)
        return (_per_head(_l2n, s) if normalize else s,)

    res = _tiled(fn, T=T, C=DNW, ins=[("ext", proj, lambda j: j + coff), ("row", w_sec, None)],
                 out_dtypes=[F32], cb=cb, name=name, comm=comm)
    return res[0] if comm is None else (res[0][0], res[1])


def _qkv_bwd(dsec, proj, w_sec, coff, normalize, DNW, cb, name):
    T = proj.shape[0]
    K = w_sec.shape[0]

    def l2n_bwd(s, dn):
        r = lax.rsqrt(jnp.sum(s * s, axis=1, keepdims=True) + EPS)
        nrm = s * r
        return r * (dn - nrm * jnp.sum(dn * nrm, axis=1, keepdims=True))

    def fn(j, i, dn, pre, w):
        c = _conv_causal(pre, w)
        sg = _sigmoid(c)
        s = c * sg
        ds = _per_head(l2n_bwd, s, dn) if normalize else dn
        dc = ds * (sg * (1.0 + c * (1.0 - sg)))
        return _own(_conv_anti(dc, w)), _conv_dw(dc, pre, K)

    return _tiled(fn, T=T, C=DNW, ins=[("ext", dsec, None), ("ext", proj, lambda j: j + coff), ("row", w_sec, None)],
                  out_dtypes=[BF16], acc_rows=[HALO], cb=cb, name=name)


def _gb_fwd(small, a_log_row, dt_row, H):
    T = small.shape[0]

    def fn(j, i, sm, al, dt):
        z = sm + dt
        sp = jnp.maximum(z, 0.0) + jnp.log(1.0 + jnp.exp(-jnp.abs(z)))
        g = -jnp.exp(al) * sp
        beta = _sigmoid(pltpu.roll(sm, LANES - H, 1))
        return g, beta

    return _tiled(fn, T=T, C=LANES, ins=[("cur", small, None), ("row", a_log_row, None), ("row", dt_row, None)],
                  out_dtypes=[F32, F32], cb=LANES, name="gb_fwd")


def _gb_bwd(dgB, dbB, small, g, beta, a_log_row, dt_row, H):
    T = small.shape[0]

    def fn(j, i, dgv, dbv, sm, gv, bv, al, dt):
        lane = lax.broadcasted_iota(jnp.int32, sm.shape, 1)
        dg = jnp.zeros(sm.shape, F32)
        db = jnp.zeros(sm.shape, F32)
        for h in range(H):
            dg = jnp.where(lane == h, jnp.sum(dgv[h], axis=1, keepdims=True), dg)
            db = jnp.where(lane == h, jnp.sum(dbv[h], axis=1, keepdims=True), db)
        da = dg * (-jnp.exp(al)) * _sigmoid(sm + dt)
        dbb = db * bv * (1.0 - bv)
        dsm = jnp.where(lane < H, da, 0.0) + pltpu.roll(jnp.where(lane < H, dbb, 0.0), H, 1)
        d_alog = jnp.sum(jnp.where(lane < H, dg * gv, 0.0), axis=0, keepdims=True)
        d_dt = jnp.sum(jnp.where(lane < H, da, 0.0), axis=0, keepdims=True)
        return dsm, jnp.concatenate([d_alog, d_dt, jnp.zeros((HALO - 2, LANES), F32)], axis=0)

    return _tiled(fn, T=T, C=LANES, ins=[("stack", dgB, None), ("stack", dbB, None), ("cur", small, None), ("cur", g, None),
                                          ("cur", beta, None), ("row", a_log_row, None), ("row", dt_row, None)],
                  out_dtypes=[BF16], acc_rows=[HALO], cb=LANES, name="gb_bwd")


_DIMS = {"nn": (((1,), (0,)), ((), ())), "nt": (((1,), (1,)), ((), ())), "tn": (((0,), (0,)), ((), ()))}
_DOT_BWD = {"nn": (("nt", "gb"), ("tn", "ag")), "nt": (("nn", "gb"), ("tn", "ga")), "tn": (("nt", "bg"), ("nn", "ag"))}


def _split(a):
    hi = a.astype(BF16)
    return hi, (a - hi.astype(F32)).astype(BF16)


def _raw_dot(a, b, kind, passes):
    dg = lambda x, y: lax.dot_general(x, y, _DIMS[kind], preferred_element_type=F32)
    if passes == 1:
        return dg(a.astype(BF16), b.astype(BF16))
    ah, al = _split(a)
    bh, bl = _split(b)
    if kind == "tn":
        return dg(ah, bh) + (dg(ah, bl) + dg(al, bh))
    m = a.shape[0]
    top = dg(jnp.concatenate([ah, al], axis=0), bh)
    return top[:m] + (dg(ah, bl) + top[m:])


def _raw_dot_exact(a, b, kind, exact):
    dg = lambda x, y: lax.dot_general(x, y, _DIMS[kind], preferred_element_type=F32)
    if exact == "a":
        bh, bl = _split(b)
        return dg(a.astype(BF16), bh) + dg(a.astype(BF16), bl)
    ah, al = _split(a)
    return dg(ah, b.astype(BF16)) + dg(al, b.astype(BF16))


@functools.lru_cache(maxsize=None)
def _dotc(kind):
    @jax.custom_vjp
    def f(a, b):
        return _raw_dot_exact(a, b, kind, "a")

    def fwd(a, b):
        return _raw_dot_exact(a, b, kind, "a"), a

    def bwd(a, g):
        db = _raw_dot_exact(a, g, "tn", "a") if kind == "nn" else _raw_dot_exact(g, a, "tn", "b")
        return jnp.zeros_like(a), db

    f.defvjp(fwd, bwd)
    return f


@functools.lru_cache(maxsize=None)
def _dotf(kind, passes):
    @jax.custom_vjp
    def f(a, b):
        return _raw_dot(a, b, kind, passes)

    def fwd(a, b):
        return _raw_dot(a, b, kind, passes), (a, b)

    def bwd(res, g):
        ops = {"a": res[0], "b": res[1], "g": g}
        (ka, oa), (kb, ob) = _DOT_BWD[kind]
        return (_raw_dot(ops[oa[0]], ops[oa[1]], ka, passes), _raw_dot(ops[ob[0]], ops[ob[1]], kb, passes))

    f.defvjp(fwd, bwd)
    return f


@jax.custom_vjp
def _saved_inverse(L, inv):
    return inv


def _saved_inverse_fwd(L, inv):
    return inv, inv


def _saved_inverse_bwd(inv, g):
    d3nt, d3tn = _dotf("nt", 3), _dotf("tn", 3)
    return -d3nt(d3tn(inv, g), inv), jnp.zeros_like(inv)


_saved_inverse.defvjp(_saved_inverse_fwd, _saved_inverse_bwd)


def _chunk_fn(q, k, v, gB, bB, S, inv_saved=None):
    C = CHUNK
    d3 = _dotf("nn", 3)
    d1, d1nt, d1tn = _dotf("nn", 1), _dotf("nt", 1), _dotf("tn", 1)
    each = lambda f, *ls: tuple(f(*xs) for xs in zip(*ls))
    row = lax.broadcasted_iota(jnp.int32, (C, C), 0)
    col = lax.broadcasted_iota(jnp.int32, (C, C), 1)
    causal = row >= col
    strict = row > col
    tril = jnp.where(causal, 1.0, 0.0).astype(F32)
    eye = jnp.where(row == col, 1.0, 0.0).astype(F32)
    avg = jnp.full((C, HEAD_DIM), 1.0 / HEAD_DIM, F32)
    gc = each(lambda g: _dotc("nn")(tril, g), gB)
    R = each(lambda g: _dotc("nt")(avg, g), gc)
    decay = each(lambda g, r: jnp.where(causal, jnp.exp(jnp.where(causal, g[:, :C] - r, 0.0)), 0.0), gc, R)
    kk = each(lambda x: d1nt(x, x), k)
    L = each(lambda a, d, b: jnp.where(strict, a * d * b[:, :C], 0.0), kk, decay, bB)
    if inv_saved is None:
        inv = each(lambda l: eye - l, L)
        P = L
        for _ in range(5):
            P = each(lambda p: d3(p, p), P)
            inv = each(lambda a, p: d3(a, eye + p), inv, P)
    else:
        inv = each(_saved_inverse, L, inv_saved)
    eg = each(jnp.exp, gc)
    u = each(lambda a, x, b: d3(a, x * b), inv, v, bB)
    w = each(lambda a, x, b, e: d3(a, x * b * e), inv, k, bB, eg)
    qs = each(lambda x: x * (HEAD_DIM ** -0.5), q)
    qk = each(lambda a, x, d: d1nt(a, x) * d, qs, k, decay)
    gl = each(lambda g: g[C - 1:C, :], gc)
    kd = each(lambda x, a, g: x * jnp.exp(a - g), k, gl, gc)
    qe = each(lambda a, e: a * e, qs, eg)
    nh = len(S)
    o = ()
    for c in range(len(q) // nh):
        sl = slice(c * nh, (c + 1) * nh)
        v_new = each(lambda a, b, s: a - d1(b, s), u[sl], w[sl], S)
        o1 = each(lambda a, s: d1(a, s), qe[sl], S)
        o += each(lambda a, b, vn: a + d1(b, vn), o1, qk[sl], v_new)
        kv = each(lambda x, vn: d1tn(x, vn), kd[sl], v_new)
        S = each(lambda s, a, b: s * jnp.exp(a) + b, S, gl[sl], kv)
    return (o, S), inv


def _sel_lane(x, h):
    lane = lax.broadcasted_iota(jnp.int32, x.shape, 1)
    return jnp.broadcast_to(jnp.sum(jnp.where(lane == h, x, 0.0), axis=1, keepdims=True), x.shape)


def _tile_of(ref, c, h):
    return ref[c * CHUNK:(c + 1) * CHUNK, h * HEAD_DIM:(h + 1) * HEAD_DIM]


def _chunks_per_step(N):
    return 4 if N % 4 == 0 else (2 if N % 2 == 0 else 1)


def _delta_fwd(q, k, v, g, beta, comm=None):
    T = q.shape[0]
    H, N = q.shape[1] // HEAD_DIM, T // CHUNK
    cps = _chunks_per_step(N)
    rows = cps * CHUNK

    def body(q_ref, k_ref, v_ref, g_ref, b_ref, o_ref, s_ref, inv_ref, S):
        @pl.when(pl.program_id(0) == 0)
        def _():
            S[...] = jnp.zeros_like(S)

        gv, bv = g_ref[...], b_ref[...]
        pairs = lambda f: tuple(f(c, h) for c in range(cps) for h in range(H))
        S_in = tuple(S[h] for h in range(H))
        for h in range(H):
            s_ref[h, 0] = S_in[h]
        (o, S_new), inv = _chunk_fn(pairs(lambda c, h: _tile_of(q_ref, c, h)), pairs(lambda c, h: _tile_of(k_ref, c, h)),
                                    pairs(lambda c, h: _tile_of(v_ref, c, h)),
                                    pairs(lambda c, h: _sel_lane(gv[c * CHUNK:(c + 1) * CHUNK], h)),
                                    pairs(lambda c, h: _sel_lane(bv[c * CHUNK:(c + 1) * CHUNK], h)), S_in)
        for c in range(cps):
            for h in range(H):
                o_ref[c * CHUNK:(c + 1) * CHUNK, h * HEAD_DIM:(h + 1) * HEAD_DIM] = o[c * H + h]
                inv_ref[h, c] = inv[c * H + h]
        for h in range(H):
            S[h] = S_new[h]

    blk = pl.BlockSpec((rows, H * HEAD_DIM), lambda n: (n, 0))
    gblk = pl.BlockSpec((rows, LANES), lambda n: (n, 0))
    outs, comm_outs = _call(
        body, name="delta_fwd", grid=(N // cps,), in_specs=[blk, blk, blk, gblk, gblk],
        out_specs=[blk, pl.BlockSpec((H, 1, HEAD_DIM, HEAD_DIM), lambda n: (0, n, 0, 0)),
                   pl.BlockSpec((H, cps, CHUNK, CHUNK), lambda n: (0, n, 0, 0))],
        out_shape=[jax.ShapeDtypeStruct((T, H * HEAD_DIM), F32), jax.ShapeDtypeStruct((H, N // cps, HEAD_DIM, HEAD_DIM), F32),
                   jax.ShapeDtypeStruct((H, N, CHUNK, CHUNK), F32)],
        scratch_shapes=[pltpu.VMEM((H, HEAD_DIM, HEAD_DIM), F32)],
        semantics=("arbitrary",), args=(q, k, v, g, beta), comm=comm)
    return outs[0], outs[1], outs[2], comm_outs


def _delta_bwd(q, k, v, g, beta, S0, inv, do, comm=None):
    T = q.shape[0]
    H, N = q.shape[1] // HEAD_DIM, T // CHUNK
    cps = _chunks_per_step(N)
    rows, NS = cps * CHUNK, N // cps

    def body(q_ref, k_ref, v_ref, g_ref, b_ref, s_ref, inv_ref, do_ref, dq_ref, dk_ref, dv_ref, dg_ref, db_ref, dS):
        @pl.when(pl.program_id(0) == 0)
        def _():
            dS[...] = jnp.zeros_like(dS)

        gv, bv = g_ref[...], b_ref[...]
        pairs = lambda f: tuple(f(c, h) for c in range(cps) for h in range(H))
        heads = lambda f: tuple(f(h) for h in range(H))
        _, vjp, _ = jax.vjp(_chunk_fn, pairs(lambda c, h: _tile_of(q_ref, c, h)), pairs(lambda c, h: _tile_of(k_ref, c, h)),
                            pairs(lambda c, h: _tile_of(v_ref, c, h)),
                            pairs(lambda c, h: _sel_lane(gv[c * CHUNK:(c + 1) * CHUNK], h)),
                            pairs(lambda c, h: _sel_lane(bv[c * CHUNK:(c + 1) * CHUNK], h)),
                            heads(lambda h: s_ref[h, 0]), pairs(lambda c, h: inv_ref[h, c]), has_aux=True)
        dq, dk, dv, dgB, dbB, dS_prev, _ = vjp((pairs(lambda c, h: _tile_of(do_ref, c, h)), heads(lambda h: dS[h])))
        for c in range(cps):
            for h in range(H):
                r, sl = slice(c * CHUNK, (c + 1) * CHUNK), slice(h * HEAD_DIM, (h + 1) * HEAD_DIM)
                dq_ref[r, sl] = dq[c * H + h]
                dk_ref[r, sl] = dk[c * H + h]
                dv_ref[r, sl] = dv[c * H + h]
                dg_ref[h, r] = dgB[c * H + h]
                db_ref[h, r] = dbB[c * H + h]
        for h in range(H):
            dS[h] = dS_prev[h]

    blk = pl.BlockSpec((rows, H * HEAD_DIM), lambda n: (NS - 1 - n, 0))
    gblk = pl.BlockSpec((rows, LANES), lambda n: (NS - 1 - n, 0))
    hblk = pl.BlockSpec((H, rows, LANES), lambda n: (0, NS - 1 - n, 0))
    sd = jax.ShapeDtypeStruct
    outs, comm_outs = _call(
        body, name="delta_bwd", grid=(NS,),
        in_specs=[blk, blk, blk, gblk, gblk, pl.BlockSpec((H, 1, HEAD_DIM, HEAD_DIM), lambda n: (0, NS - 1 - n, 0, 0)),
                  pl.BlockSpec((H, cps, CHUNK, CHUNK), lambda n: (0, NS - 1 - n, 0, 0)), blk],
        out_specs=[blk, blk, blk, hblk, hblk],
        out_shape=[sd((T, H * HEAD_DIM), F32)] * 3 + [sd((H, T, LANES), F32)] * 2,
        scratch_shapes=[pltpu.VMEM((H, HEAD_DIM, HEAD_DIM), F32)],
        semantics=("arbitrary",), args=(q, k, v, g, beta, S0, inv, do), comm=comm)
    return (*outs, comm_outs)


def _gnorm_fwd(o, proj, z_coff, gdn_t, DNW, buf, coff):
    T = o.shape[0]

    def fn(j, i, ov, zv, gv):
        def one(oh, zh, gh):
            r = lax.rsqrt(jnp.mean(oh * oh, axis=1, keepdims=True) + EPS)
            return oh * r * gh * (zh * _sigmoid(zh))
        return (_per_head(one, ov, zv, jnp.broadcast_to(gv, ov.shape)),)

    return _tiled(fn, T=T, C=DNW, ins=[("cur", o, None), ("cur", proj, lambda j: j + z_coff), ("row", gdn_t, None)],
                  out_dtypes=[BF16], cb=DNW, name="gnorm_fwd", into=(buf, buf.shape[1], coff))[0]


def _gnorm_bwd(dymix, y_coff, o, proj, z_coff, gdn_t, DNW):
    T = o.shape[0]
    nh = DNW // HEAD_DIM

    def fn(j, i, dy, ov, zv, gv):
        dos, dzs, dgs = [], [], jnp.zeros((1, HEAD_DIM), F32)
        for h in range(nh):
            sl = slice(h * HEAD_DIM, (h + 1) * HEAD_DIM)
            dyh, oh, zh, gh = dy[:, sl].astype(F32), ov[:, sl], zv[:, sl], gv[:, sl]
            r = lax.rsqrt(jnp.mean(oh * oh, axis=1, keepdims=True) + EPS)
            on = oh * r
            sg = _sigmoid(zh)
            sz = zh * sg
            dzs.append(dyh * on * gh * (sg * (1.0 + zh * (1.0 - sg))))
            don = dyh * gh * sz
            dos.append(r * (don - on * jnp.mean(don * on, axis=1, keepdims=True)))
            dgs = dgs + jnp.sum(dyh * on * sz, axis=0, keepdims=True)
        cat = (lambda xs: xs[0] if nh == 1 else jnp.concatenate(xs, axis=1))
        return cat(dos), cat(dzs), _row0(dgs)

    T_ = T
    nI = T_ // _tile(T_, 256, HALO)
    tb = T_ // nI
    specs_cb = DNW

    def body_wrap():
        def body(dy_ref, o_ref, z_ref, g_ref, do_ref, dz_ref, dg_ref):
            i = pl.program_id(0)
            d_o, d_z, d_g = fn(0, i, dy_ref[...], o_ref[...], z_ref[...], g_ref[...])
            do_ref[...] = d_o
            dz_ref[...] = d_z.astype(dz_ref.dtype)

            @pl.when(i == 0)
            def _():
                dg_ref[...] = d_g

            @pl.when(i > 0)
            def _():
                dg_ref[...] += d_g

        return pl.pallas_call(
            body, name="gnorm_bwd", grid=(nI,),
            in_specs=[pl.BlockSpec((tb, specs_cb), lambda i: (i, y_coff)), pl.BlockSpec((tb, specs_cb), lambda i: (i, 0)),
                      pl.BlockSpec((tb, specs_cb), lambda i: (i, z_coff)), pl.BlockSpec((1, specs_cb), lambda i: (0, 0))],
            out_specs=[pl.BlockSpec((tb, specs_cb), lambda i: (i, 0)), pl.BlockSpec((tb, specs_cb), lambda i: (i, 0)),
                       pl.BlockSpec((HALO, HEAD_DIM), lambda i: (0, 0))],
            out_shape=[jax.ShapeDtypeStruct((T_, DNW), F32), jax.ShapeDtypeStruct((T_, DNW), BF16),
                       jax.ShapeDtypeStruct((HALO, HEAD_DIM), F32)],
            compiler_params=pltpu.CompilerParams(dimension_semantics=("arbitrary",), vmem_limit_bytes=VMEM_LIMIT),
        )(dymix, o, proj, gdn_t)

    return body_wrap()


def _ffn_fwd(up_g, up_v, w_g, w_v, cb):
    T, F = up_g.shape

    def fn(j, i, ug, uv, wg, wv):
        cg = _own(_conv_causal(ug, wg))
        cv = _own(_conv_causal(uv, wv))
        return (cg * _sigmoid(cg) * cv,)

    return _tiled(fn, T=T, C=F, ins=[("ext", up_g, None), ("ext", up_v, None), ("row", w_g, None), ("row", w_v, None)],
                  out_dtypes=[BF16], tb=1024, cb=cb, name="ffn_fwd")[0]


def _ffn_bwd(dact, up_g, up_v, w_g, w_v, cb):
    T, F = up_g.shape
    K = w_g.shape[0]

    def fn(j, i, da, ug, uv, wg, wv):
        cg = _conv_causal(ug, wg)
        cv = _conv_causal(uv, wv)
        sg = _sigmoid(cg)
        dgate = da * cv * (sg * (1.0 + cg * (1.0 - sg)))
        dval = da * (cg * sg)
        return (_own(_conv_anti(dgate, wg)), _own(_conv_anti(dval, wv)), _conv_dw(dgate, ug, K), _conv_dw(dval, uv, K))

    return _tiled(fn, T=T, C=F, ins=[("ext", dact, None), ("ext", up_g, None), ("ext", up_v, None), ("row", w_g, None),
                                      ("row", w_v, None)], out_dtypes=[BF16, BF16], acc_rows=[HALO, HALO], tb=1024, cb=cb,
                  name="ffn_bwd")


def _wide(R, Cc, n_f32, unit=HALO):
    cb = Cc if (Cc % LANES or Cc <= 4096) else _tile(Cc, 2048, LANES)
    cap = max(unit, EW_VMEM_BUDGET // (2 * 4 * n_f32 * cb) // unit * unit)
    return _tile(R, cap, unit), cb


def _adamw(w, g, m, v, name, comm=None):
    R, Cc = w.shape
    tb, cb = _wide(R, Cc, 7) if R % HALO == 0 else (R, _tile(Cc, EW_VMEM_BUDGET // (2 * 4 * 7 * R) // LANES * LANES, LANES))
    c1 = 1.0 / (1.0 - ADAM_B1 ** ADAM_STEP)
    c2 = 1.0 / (1.0 - ADAM_B2 ** ADAM_STEP)

    def fn(j, i, wv, gv, mv, vv):
        m2 = ADAM_B1 * mv + (1.0 - ADAM_B1) * gv
        v2 = ADAM_B2 * vv + (1.0 - ADAM_B2) * (gv * gv)
        delta = -ADAM_LR * ((m2 * c1) / (jnp.sqrt(v2 * c2) + ADAM_EPS) + ADAM_WD * wv)
        return delta, m2, v2

    return _tiled(fn, T=R, C=Cc, ins=[("cur", w, None), ("cur", g, None), ("cur", m, None), ("cur", v, None)],
                  out_dtypes=[F32, F32, F32], tb=tb, cb=cb, name=name, comm=comm)


def _join_shards(w4, n_main):
    S4, R, cs = w4.shape
    n_small = S4 * cs - n_main
    assert 0 < n_small <= LANES and n_small <= cs
    tb = _tile(R, 256, 2 * HALO)

    def body(w_ref, main_ref, small_ref):
        for t in range(S4 - 1):
            main_ref[:, t * cs:(t + 1) * cs] = w_ref[t]
        last = w_ref[S4 - 1]
        main_ref[:, (S4 - 1) * cs:] = last[:, :cs - n_small]
        small_ref[...] = jnp.zeros_like(small_ref)
        small_ref[:, :n_small] = last[:, cs - n_small:]

    return pl.pallas_call(
        body, name="join_w_in", grid=(R // tb,), in_specs=[pl.BlockSpec((S4, tb, cs), lambda i: (0, i, 0))],
        out_specs=[pl.BlockSpec((tb, n_main), lambda i: (i, 0)), pl.BlockSpec((tb, LANES), lambda i: (i, 0))],
        out_shape=[jax.ShapeDtypeStruct((R, n_main), w4.dtype), jax.ShapeDtypeStruct((R, LANES), w4.dtype)],
        compiler_params=pltpu.CompilerParams(dimension_semantics=("parallel",), vmem_limit_bytes=VMEM_LIMIT))(w4)


def _split_shards(main, small, cs):
    R, n_main = main.shape
    n_small = 4 * cs - n_main
    tb = _tile(R, 256, HALO)

    def body(main_ref, small_ref, out_ref):
        for t in range(3):
            out_ref[t] = main_ref[:, t * cs:(t + 1) * cs]
        out_ref[3, :, :cs - n_small] = main_ref[:, 3 * cs:]
        out_ref[3, :, cs - n_small:] = small_ref[:, :n_small]

    return pl.pallas_call(
        body, name="split_g_in", grid=(R // tb,),
        in_specs=[pl.BlockSpec((tb, n_main), lambda i: (i, 0)), pl.BlockSpec((tb, LANES), lambda i: (i, 0))],
        out_specs=pl.BlockSpec((4, tb, cs), lambda i: (0, i, 0)), out_shape=jax.ShapeDtypeStruct((4, R, cs), main.dtype),
        compiler_params=pltpu.CompilerParams(dimension_semantics=("parallel",), vmem_limit_bytes=VMEM_LIMIT))(main, small)


def _sum_stack(st, name):
    S, R, Cc = st.shape
    cb = _tile(Cc, 512, LANES) if Cc % LANES == 0 else Cc

    def fn(j, i, sv):
        t = sv[0]
        for s in range(1, S):
            t = t + sv[s]
        return (t,)

    return _tiled(fn, T=R, C=Cc, ins=[("stack", st, None)], out_dtypes=[F32], cb=cb, name=name)[0]


ANY = pl.BlockSpec(memory_space=pl.ANY)


def _place():
    x, y, c = lax.axis_index("x"), lax.axis_index("y"), lax.axis_index("c")
    return x, y, c, 2 * x + y


def _chip_dev(s, c):
    return (s // 2, s % 2, c)


class _Comm:
    def __init__(self, ins, out_shapes, sems, start, wait, aliases=None):
        self.ins, self.out_shapes, self.sems = list(ins), list(out_shapes), list(sems)
        self.start, self.wait, self.aliases = start, wait, dict(aliases or {})


def _merge(*comms):
    offs, i, o, s = [], 0, 0, 0
    for cm in comms:
        offs.append((i, o, s))
        i, o, s = i + len(cm.ins), o + len(cm.out_shapes), s + len(cm.sems)

    def part(refs, k, cm):
        i0, o0, s0 = offs[k]
        return refs[0][i0:i0 + len(cm.ins)], refs[1][o0:o0 + len(cm.out_shapes)], refs[2][s0:s0 + len(cm.sems)]

    def start(*refs):
        for k, cm in enumerate(comms):
            cm.start(*part(refs, k, cm))

    def wait(*refs):
        for k, cm in enumerate(comms):
            cm.wait(*part(refs, k, cm))

    aliases = {}
    for k, cm in enumerate(comms):
        for a, b in cm.aliases.items():
            aliases[offs[k][0] + a] = offs[k][1] + b
    return _Comm([a for cm in comms for a in cm.ins], [a for cm in comms for a in cm.out_shapes],
                 [a for cm in comms for a in cm.sems], start, wait, aliases)


def _call(body, *, name, grid, in_specs, out_specs, out_shape, scratch_shapes, semantics, args, comm=None, io_aliases=None):
    if comm is None:
        outs = pl.pallas_call(
            body, name=name, grid=grid, in_specs=in_specs, out_specs=out_specs, out_shape=out_shape,
            scratch_shapes=list(scratch_shapes), input_output_aliases=dict(io_aliases or {}),
            compiler_params=pltpu.CompilerParams(dimension_semantics=semantics, vmem_limit_bytes=VMEM_LIMIT))(*args)
        return list(outs), []
    assert not io_aliases
    n_in, n_out, n_scr = len(in_specs), len(out_specs), len(scratch_shapes)
    ci, co = len(comm.ins), len(comm.out_shapes)

    def wrapped(*refs):
        r = 0
        ins, r = refs[r:r + n_in], r + n_in
        cins, r = refs[r:r + ci], r + ci
        outs, r = refs[r:r + n_out], r + n_out
        couts, r = refs[r:r + co], r + co
        scr, r = refs[r:r + n_scr], r + n_scr
        csems = refs[r:]
        ids = [pl.program_id(a) for a in range(len(grid))]
        first, last = ids[0] == 0, ids[0] == grid[0] - 1
        for a in range(1, len(grid)):
            first = jnp.logical_and(first, ids[a] == 0)
            last = jnp.logical_and(last, ids[a] == grid[a] - 1)

        @pl.when(first)
        def _():
            comm.start(cins, couts, csems)

        body(*ins, *outs, *scr)

        @pl.when(last)
        def _():
            comm.wait(cins, couts, csems)

    outs = pl.pallas_call(
        wrapped, name=name, grid=grid, in_specs=list(in_specs) + [ANY] * ci, out_specs=list(out_specs) + [ANY] * co,
        out_shape=list(out_shape) + comm.out_shapes, scratch_shapes=list(scratch_shapes) + comm.sems,
        input_output_aliases={n_in + a: n_out + b for a, b in comm.aliases.items()},
        compiler_params=pltpu.CompilerParams(dimension_semantics=("arbitrary",) * len(grid), vmem_limit_bytes=VMEM_LIMIT),
    )(*args, *comm.ins)
    return list(outs[:n_out]), list(outs[n_out:])


def _run_comm(comm, name):
    ci, co = len(comm.ins), len(comm.out_shapes)

    def body(*refs):
        cins, couts, csems = refs[:ci], refs[ci:ci + co], refs[ci + co:]
        comm.start(cins, couts, csems)
        comm.wait(cins, couts, csems)

    outs = pl.pallas_call(body, name=name, in_specs=[ANY] * ci, out_specs=[ANY] * co, out_shape=comm.out_shapes,
                          scratch_shapes=comm.sems, input_output_aliases=comm.aliases)(*comm.ins)
    return list(outs)


def _ag_comm(shard, land=None, q=0, nq=1):
    two, R2, Cc = shard.shape
    rows = pl.ds(q * (R2 // nq), R2 // nq)
    DMA = pltpu.SemaphoreType.DMA

    def copies(ins, outs, sems, which):
        sh, out = ins[0], outs[0]
        send1, recv1, send2, recv2, send0, recv0 = sems
        x, y, c, s = _place()
        sib = (x, y, 1 - c)
        rc = pltpu.make_async_remote_copy
        if which == "first":
            return [rc(sh.at[c, rows], out.at[s, c, rows], send1.at[m - 1], recv1.at[m - 1],
                       device_id=_chip_dev(s ^ m, c), device_id_type=MESH) for m in range(1, 4)]
        if which == "own":
            return [rc(sh.at[h, rows], out.at[s, h, rows], send0.at[h], recv0.at[h], device_id=sib, device_id_type=MESH)
                    for h in range(2)]
        if which == "landed":
            return [rc(sh.at[c, rows], out.at[s ^ m, c, rows], send1.at[m - 1], recv1.at[m - 1], device_id=sib,
                       device_id_type=MESH) for m in range(1, 4)]
        half = c if which == "passed" else 1 - c
        return [rc(out.at[s ^ m, half, rows], out.at[s ^ m, half, rows], send2.at[m - 1], recv2.at[m - 1], device_id=sib,
                   device_id_type=MESH) for m in range(1, 4)]

    def start(ins, outs, sems):
        for cp in copies(ins, outs, sems, "first") + copies(ins, outs, sems, "own"):
            cp.start()

    def wait(ins, outs, sems):
        passed = copies(ins, outs, sems, "passed")
        for lan, pas in zip(copies(ins, outs, sems, "landed"), passed):
            lan.wait_recv()
            pas.start()
        for cp in copies(ins, outs, sems, "handed"):
            cp.wait_recv()
        for cp in copies(ins, outs, sems, "own"):
            cp.wait()
        for cp in copies(ins, outs, sems, "first") + passed:
            cp.wait_send()

    return _Comm([shard] + ([land] if land is not None else []), [jax.ShapeDtypeStruct((4, two, R2, Cc), shard.dtype)],
                 [DMA((3,)), DMA((3,)), DMA((3,)), DMA((3,)), DMA((2,)), DMA((2,))], start, wait,
                 {1: 0} if land is not None else None)


def _a2a_comm(S1, q=0, nq=1, land=None, cnt=1):
    S4, R2, Cc = S1.shape
    rows = pl.ds(q * (R2 // nq), cnt * (R2 // nq))
    DMA = pltpu.SemaphoreType.DMA

    def copies(ins, outs, sems):
        x, y, c, s = _place()
        return [pltpu.make_async_remote_copy(ins[0].at[s ^ m, rows], outs[0].at[m - 1, rows], sems[0].at[m - 1],
                                             sems[1].at[m - 1], device_id=_chip_dev(s ^ m, c), device_id_type=MESH)
                for m in range(1, 4)]

    def start(ins, outs, sems):
        for cp in copies(ins, outs, sems):
            cp.start()

    def wait(ins, outs, sems):
        for cp in copies(ins, outs, sems):
            cp.wait()

    return _Comm([S1] + ([land] if land is not None else []), [jax.ShapeDtypeStruct((3, R2, Cc), S1.dtype)],
                 [DMA((3,)), DMA((3,))], start, wait, {1: 0} if land is not None else None)


def _halves(G):
    return G.reshape(G.shape[0], 2, G.shape[1] // 2, G.shape[2])


def _swap_comm(piece):
    n, two, R2, Cc = piece.shape
    DMA = pltpu.SemaphoreType.DMA

    def copies(ins, outs, sems):
        x, y, c, s = _place()
        return [pltpu.make_async_remote_copy(ins[0].at[t, 1 - c], outs[0].at[t], sems[0].at[t], sems[1].at[t],
                                             device_id=(x, y, 1 - c), device_id_type=MESH) for t in range(n)]

    def start(ins, outs, sems):
        for cp in copies(ins, outs, sems):
            cp.start()

    def wait(ins, outs, sems):
        for cp in copies(ins, outs, sems):
            cp.wait()

    return _Comm([piece], [jax.ShapeDtypeStruct((n, R2, Cc), piece.dtype)], [DMA((n,)), DMA((n,))], start, wait)


def _add_half(pieces, As, cidx, name):
    R2, Cc = pieces[0].shape[2:]
    S4 = sum(pc.shape[0] for pc in pieces)
    tb, cb = _wide(R2, Cc, 3, 2 * HALO)
    nI, nJ = R2 // tb, Cc // cb

    def body(c_ref, g_ref, a_ref, *rest):
        rest[-1][...] = (g_ref[0, 0] + a_ref[0]).astype(BF16)

    out, t0 = None, 0
    for k, (pc, A) in enumerate(zip(pieces, As)):
        grid_spec = pltpu.PrefetchScalarGridSpec(
            num_scalar_prefetch=1, grid=(pc.shape[0], nI, nJ),
            in_specs=[pl.BlockSpec((1, 1, tb, cb), lambda t, i, j, c_ref: (t, c_ref[0], i, j)),
                      pl.BlockSpec((1, tb, cb), lambda t, i, j, c_ref: (t, i, j))] + ([ANY] if k else []),
            out_specs=pl.BlockSpec((tb, cb), lambda t, i, j, c_ref, t0=t0: ((t0 + t) * nI + i, j)))
        out = pl.pallas_call(
            functools.partial(body), name=f"{name}{k}", grid_spec=grid_spec, out_shape=jax.ShapeDtypeStruct((S4 * R2, Cc), BF16),
            input_output_aliases={3: 0} if k else {},
            compiler_params=pltpu.CompilerParams(dimension_semantics=("parallel", "parallel", "parallel"),
                                                 vmem_limit_bytes=VMEM_LIMIT),
        )(*((cidx, pc, A) + ((out,) if k else ())))
        t0 += pc.shape[0]
    return out.reshape(S4, R2, Cc)


def _add_own(S1, B, chip_idx, cidx, name):
    S4, R2, Cc = S1.shape
    tb, cb = _wide(R2, Cc, 3, 2 * HALO)

    def body(s_idx, c_idx, s_ref, b_ref, o_ref):
        o_ref[...] = ((s_ref[0].astype(F32) + b_ref[0].astype(F32)) + b_ref[1].astype(F32)) + b_ref[2].astype(F32)

    grid_spec = pltpu.PrefetchScalarGridSpec(
        num_scalar_prefetch=2, grid=(R2 // tb, Cc // cb),
        in_specs=[pl.BlockSpec((1, tb, cb), lambda i, j, s_idx, c_idx: (s_idx[0], i, j)),
                  pl.BlockSpec((3, tb, cb), lambda i, j, s_idx, c_idx: (0, i, j))],
        out_specs=pl.BlockSpec((None, tb, cb), lambda i, j, s_idx, c_idx: (c_idx[0], i, j)))
    return pl.pallas_call(body, name=name, grid_spec=grid_spec, out_shape=jax.ShapeDtypeStruct((2, R2, Cc), F32),
                          compiler_params=pltpu.CompilerParams(dimension_semantics=("parallel", "parallel"),
                                                               vmem_limit_bytes=VMEM_LIMIT))(chip_idx, cidx, S1, B)


def _fill_comm(Hs):
    def copy(ins, outs, sems):
        x, y, c, s = _place()
        return pltpu.make_async_remote_copy(ins[0].at[c], outs[0].at[c], sems[0], sems[1], device_id=(x, y, 1 - c),
                                            device_id_type=MESH)

    return _Comm([Hs], [jax.ShapeDtypeStruct(Hs.shape, Hs.dtype)], [pltpu.SemaphoreType.DMA, pltpu.SemaphoreType.DMA],
                 lambda *r: copy(*r).start(), lambda *r: copy(*r).wait(), {0: 0})


def _gather_all(buf, name):
    R, Cc = buf.shape

    def body(b_ref, out_ref, send, recv, local):
        x, y, c, s = _place()
        d = 2 * s + c
        mine = pltpu.make_async_copy(b_ref, out_ref.at[d], local)
        mine.start()
        cps = []
        for m in range(1, 8):
            t = d ^ m
            cp = pltpu.make_async_remote_copy(b_ref, out_ref.at[d], send.at[m - 1], recv.at[m - 1],
                                              device_id=(t // 4, (t // 2) % 2, t % 2), device_id_type=MESH)
            cp.start()
            cps.append(cp)
        for cp in cps:
            cp.wait()
        mine.wait()

    return pl.pallas_call(
        body, name=name, in_specs=[ANY], out_specs=ANY, out_shape=jax.ShapeDtypeStruct((8, R, Cc), buf.dtype),
        scratch_shapes=[pltpu.SemaphoreType.DMA((7,)), pltpu.SemaphoreType.DMA((7,)), pltpu.SemaphoreType.DMA],
    )(buf)


def _pack_rows(vs):
    flat = jnp.concatenate([v.reshape(-1) for v in vs])
    n = flat.shape[0]
    rows = -(-n // (LANES * 2 * HALO)) * 2 * HALO
    return jnp.pad(flat, (0, rows * LANES - n)).reshape(rows, LANES)


def _unpack_rows(buf, shapes):
    flat = buf.reshape(-1)
    outs, o = [], 0
    for shp in shapes:
        n = 1
        for d in shp:
            n *= d
        outs.append(flat[o:o + n].reshape(shp))
        o += n
    return outs


def kernel(x, p, norm_mix_g, w_in, conv_a_w, conv_qkv_w, a_log, dt_bias, dn_norm_g, w_out, norm_ffn_g, w_up, conv_ffn_w, w_down, norm_ple_g, w_ple_gate, w_ple_proj, final_norm_g, loss_target, m_norm_mix_g, m_w_in, m_conv_a_w, m_conv_qkv_w, m_a_log, m_dt_bias, m_dn_norm_g, m_w_out, m_norm_ffn_g, m_w_up, m_conv_ffn_w, m_w_down, m_norm_ple_g, m_w_ple_gate, m_w_ple_proj, m_final_norm_g, v_norm_mix_g, v_w_in, v_conv_a_w, v_conv_qkv_w, v_a_log, v_dt_bias, v_dn_norm_g, v_w_out, v_norm_ffn_g, v_w_up, v_conv_ffn_w, v_w_down, v_norm_ple_g, v_w_ple_gate, v_w_ple_proj, v_final_norm_g):
    xs = x[0]
    ps = p[0, 0]
    tgt = loss_target[0]
    T, D = xs.shape
    H = a_log.shape[-1]
    DNW = H * HEAD_DIM
    CW = conv_a_w.shape[-1] * 4
    F = w_down.shape[1] * 4
    PD = ps.shape[-1]
    IN_MAIN = 3 * CW + 4 * DNW
    IN_COLS = IN_MAIN + 2 * H
    assert w_in.shape[-1] * 4 == IN_COLS and CW + DNW == D and 2 * H <= LANES
    cb = _tile(min(CW, DNW), 512, LANES)
    while F % cb:
        cb -= LANES
    cidx = lax.axis_index("c").astype(jnp.int32).reshape(1)
    chip = 2 * lax.axis_index("x") + lax.axis_index("y")

    def halves(w):
        sh = w[0].astype(BF16)
        return sh.reshape(2, sh.shape[0] // 2, sh.shape[1])

    def whole(land):
        return land.reshape(4, 2 * land.shape[2], land.shape[3])

    def rows(g4):
        return g4.reshape(4 * g4.shape[1], g4.shape[2])

    conv_shapes = [conv_a_w[0].shape, conv_qkv_w[0].shape, conv_ffn_w[0].shape]
    cpack = _pack_rows([conv_a_w[0], conv_qkv_w[0], conv_ffn_w[0]])
    sh_in, sh_out, sh_up, sh_down, sh_pg, sh_pp = (halves(w) for w in (w_in, w_out, w_up, w_down, w_ple_gate, w_ple_proj))
    l_in, cg = _run_comm(_merge(_ag_comm(sh_in), _ag_comm(cpack.reshape(2, cpack.shape[0] // 2, LANES))), "ag_w_in_conv")
    w_in_main, w_in_small = _join_shards(whole(l_in), IN_MAIN)
    cg = cg.reshape(4, cpack.shape[0], LANES)
    parts = [_unpack_rows(cg[t], conv_shapes) for t in range(4)]
    cw_a = jnp.concatenate([parts[t][0] for t in range(4)], axis=1)
    cw_qkv = jnp.concatenate([parts[t][1] for t in range(4)], axis=1)
    cw_ffn = jnp.concatenate([parts[t][2] for t in range(4)], axis=1)
    cw_q, cw_k, cw_v = cw_qkv[:, :DNW], cw_qkv[:, DNW:2 * DNW], cw_qkv[:, 2 * DNW:]
    cw_fg, cw_fv = cw_ffn[:, :F], cw_ffn[:, F:]
    pad_row = lambda v: jnp.pad(v, ((0, 0), (0, LANES - v.shape[1])))
    a_log_row, dt_row = pad_row(a_log), pad_row(dt_bias)
    gdn_t = jnp.tile(dn_norm_g, (1, H))
    gfin = final_norm_g.reshape(1, D)

    h1 = _rms_fwd(xs, norm_mix_g, "rms1")
    proj, (l_up,) = _mm(h1, w_in_main, mode="nn", out_dtypes=[F32], name="mm_proj", comm=_ag_comm(sh_up, q=0, nq=2))
    small = _mm(h1, w_in_small, mode="nn", out_dtypes=[F32], name="mm_small")
    ymix = _ga_fwd(proj, cw_a, CW, cb, D)
    nq = 3 * CW // cb
    nd = DNW // cb
    qn, (l_out,) = _qkv_fwd(proj, cw_q, nq, True, DNW, cb, "q_fwd", comm=_ag_comm(sh_out, q=0, nq=2))
    kn, (l_out,) = _qkv_fwd(proj, cw_k, nq + nd, True, DNW, cb, "k_fwd", comm=_ag_comm(sh_out, l_out, q=1, nq=2))
    vs = _qkv_fwd(proj, cw_v, nq + 2 * nd, False, DNW, cb, "v_fwd")
    g, beta = _gb_fwd(small, a_log_row, dt_row, H)
    o, S0, inv_c, (l_up,) = _delta_fwd(qn, kn, vs, g, beta, comm=_ag_comm(sh_up, l_up, q=1, nq=2))
    w_out_f = rows(whole(l_out))
    w_up_4 = whole(l_up)
    z_coff = (3 * CW + 3 * DNW) // DNW
    assert (3 * CW + 3 * DNW) % DNW == 0 and CW % DNW == 0
    ymix = _gnorm_fwd(o, proj, z_coff, gdn_t, DNW, ymix, CW // DNW)
    add = lambda acc, r: (r + acc,)

    def out_epi(acc, xv, gv):
        x1v = xv + acc
        return x1v, x1v * lax.rsqrt(jnp.mean(x1v * x1v, axis=1, keepdims=True) + EPS) * gv

    x1, h2 = _mm(ymix, w_out_f, mode="nn", out_dtypes=[F32, BF16], epi=out_epi, extras=[xs], rows=[norm_ffn_g], name="mm_out")
    up_g, (l_down,) = _mm(h2, w_up_4, mode="nn", b_split=(0, 2), out_dtypes=[F32], name="mm_up_g",
                          comm=_ag_comm(sh_down, q=0, nq=2))
    up_v, (l_down,) = _mm(h2, w_up_4, mode="nn", b_split=(2, 2), out_dtypes=[F32], name="mm_up_v",
                          comm=_ag_comm(sh_down, l_down, q=1, nq=2))
    w_down_f = rows(whole(l_down))
    act = _ffn_fwd(up_g, up_v, cw_fg, cw_fv, cb)
    x2, (l_pg, l_pp) = _mm(act, w_down_f, mode="nn", out_dtypes=[F32], epi=add, extras=[x1], name="mm_down",
                           comm=_merge(_ag_comm(sh_pg), _ag_comm(sh_pp)))
    w_pg_f = rows(whole(l_pg))
    w_pp_4 = whole(l_pp)
    h3 = _rms_fwd(x2, norm_ple_g, "rms3")
    pp = _mm(ps, w_pp_4, mode="nn", b_split=(0, 4), out_dtypes=[F32], name="mm_pp")

    def ple_final_epi(acc, x2v, ppv, tv, gv):
        pg = _sigmoid(acc)
        x3v = x2v + pg * ppv
        r = lax.rsqrt(jnp.mean(x3v * x3v, axis=1, keepdims=True) + EPS)
        xh = x3v * r
        e = xh * gv - tv
        dy = e * (1.0 / D)
        dxh = dy * gv
        dx = r * (dxh - xh * jnp.mean(dxh * xh, axis=1, keepdims=True))
        dg = jnp.sum(dy * xh, axis=0, keepdims=True)
        ls = jnp.sum(e * e, axis=0, keepdims=True) * (0.5 / D)
        return dx, dx * ppv * pg * (1.0 - pg), dx * pg, jnp.concatenate([dg, ls, jnp.zeros((HALO - 2, D), F32)], axis=0)

    dx3, dpg, dpp, fin = _mm(h3, w_pg_f, mode="nn", out_dtypes=[F32, BF16, BF16], parts=1, epi=ple_final_epi,
                             extras=[x2, pp, tgt], rows=[gfin], name="mm_pg_final")
    fin = jnp.sum(fin.reshape(-1, HALO, D), axis=0)
    loss = lax.psum(jnp.sum(fin[1]), ("x", "y", "c"))
    d_gfin = fin[0:1]
    def split_rows(dW):
        return dW.reshape(4, dW.shape[0] // 4, dW.shape[1])

    chip_idx = chip.astype(jnp.int32).reshape(1)
    own_sum = lambda S1, B, name: _add_own(S1, B, chip_idx, cidx, "rs_" + name + "_sum")

    dW_pp = _mm(ps, dpp, mode="tn", out_split=4, out_dtypes=[F32], name="mm_dw_pp")
    dW_pg = _mm(h3, dpg, mode="tn", out_dtypes=[F32], name="mm_dw_pg")
    P_pp, P_pg = _halves(dW_pp), _halves(split_rows(dW_pg))
    def rms_bwd_epi(acc, xv, dr, gv):
        dxv, dg = _rms_bwd_math(acc, xv, gv)
        return dr + dxv, dr + dxv, _row0(dg)

    (dx2, dx2_b, d_gple), (A_pp, A_pg) = _mm(dpg, w_pg_f, mode="nt", out_dtypes=[F32, BF16], parts=1, epi=rms_bwd_epi,
                                             extras=[x2, dx3], rows=[norm_ple_g], name="mm_dh3_rms",
                                             comm=_merge(_swap_comm(P_pp), _swap_comm(P_pg)))
    d_gple = jnp.sum(d_gple.reshape(-1, HALO, D), axis=0)
    S_pp = _add_half([P_pp], [A_pp], cidx, "rs_w_pp_add")
    S_pg = _add_half([P_pg], [A_pg], cidx, "rs_w_pg_add")
    dW_down, (B_pp, B_pg) = _mm(act, dx2_b, mode="tn", out_dtypes=[F32], name="mm_dw_down",
                                comm=_merge(_a2a_comm(S_pp), _a2a_comm(S_pg)))
    P_down = _halves(split_rows(dW_down))
    dact, (A_down, F_pp, F_pg) = _mm(dx2_b, w_down_f, mode="nt", out_dtypes=[F32], name="mm_dact", comm=_merge(
        _swap_comm(P_down), _fill_comm(own_sum(S_pp, B_pp, "w_pp")), _fill_comm(own_sum(S_pg, B_pg, "w_pg"))))
    S_down = _add_half([P_down], [A_down], cidx, "rs_w_down_add")
    dup_g, dup_v, dcw_fg, dcw_fv = _ffn_bwd(dact, up_g, up_v, cw_fg, cw_fv, cb)
    dW_up_g, (B_down,) = _mm(h2, dup_g, mode="tn", out_split=2, out_dtypes=[F32], name="mm_dw_up_g", comm=_a2a_comm(S_down))
    P_ug = _halves(dW_up_g)
    dW_up_v, (A_ug, F_down) = _mm(h2, dup_v, mode="tn", out_split=2, out_dtypes=[F32], name="mm_dw_up_v",
                                  comm=_merge(_swap_comm(P_ug), _fill_comm(own_sum(S_down, B_down, "w_down"))))
    P_uv = _halves(dW_up_v)
    dh2, (A_uv,) = _mm(dup_g, w_up_4, mode="nt", b_split=(0, 2), out_dtypes=[F32], name="mm_dh2_g", comm=_swap_comm(P_uv))
    S_up = _add_half([P_ug, P_uv], [A_ug, A_uv], cidx, "rs_w_up_add")
    dh2, (B_up,) = _mm(dup_v, w_up_4, mode="nt", b_split=(2, 2), out_dtypes=[F32], epi=add, extras=[dh2], name="mm_dh2_v",
                       comm=_a2a_comm(S_up, 0, 2))
    dx1, dx1_b, d_gffn = _rms_bwd(dh2, x1, norm_ffn_g, dx2, "rms2_bwd")
    P_out = _halves(split_rows(_mm(ymix, dx1_b, mode="tn", out_dtypes=[F32], name="mm_dw_out")))
    dymix, (A_out,) = _mm(dx1_b, w_out_f, mode="nt", out_dtypes=[F32], name="mm_dymix", comm=_swap_comm(P_out))
    S_out = _add_half([P_out], [A_out], cidx, "rs_w_out_add")
    dax, dab, dac, dcw_a = _ga_bwd(dymix, proj, cw_a, CW, cb)
    do, dz, d_gdn = _gnorm_bwd(dymix, CW // DNW, o, proj, z_coff, gdn_t, DNW)
    dqn, dkn, dvs, dgB, dbB, (B_up, B_out) = _delta_bwd(qn, kn, vs, g, beta, S0, inv_c, do,
                                                        comm=_merge(_a2a_comm(S_up, 1, 2, B_up), _a2a_comm(S_out)))
    dq_pre, dcw_q = _qkv_bwd(dqn, proj, cw_q, nq, True, DNW, cb, "q_bwd")
    dk_pre, dcw_k = _qkv_bwd(dkn, proj, cw_k, nq + nd, True, DNW, cb, "k_bwd")
    dv_pre, dcw_v = _qkv_bwd(dvs, proj, cw_v, nq + 2 * nd, False, DNW, cb, "v_bwd")
    dsmall, d_ab = _gb_bwd(dgB, dbB, small, g, beta, a_log_row, dt_row, H)
    dproj = jnp.concatenate([dax, dab, dac, dq_pre, dk_pre, dv_pre, dz], axis=1)
    dW_in_main, (F_up, F_out) = _mm(h1, dproj, mode="tn", out_dtypes=[F32], name="mm_dw_in", comm=_merge(
        _fill_comm(own_sum(S_up, B_up, "w_up")), _fill_comm(own_sum(S_out, B_out, "w_out"))))
    dW_in_small = _mm(h1, dsmall, mode="tn", out_dtypes=[F32], name="mm_dw_in_small")
    def update(Hf, w, m, v, name):
        gr = Hf.reshape(2 * Hf.shape[1], Hf.shape[2])
        if gr.shape[1] % LANES == 0:
            delta, m2, v2 = _adamw(w[0], gr, m[0], v[0], "adamw_" + name)
            return gr[None], delta[None], m2[None], v2[None]
        tr = jnp.transpose
        grt = tr(gr)
        delta, m2, v2 = _adamw(tr(w[0]), grt, tr(m[0]), tr(v[0]), "adamw_" + name)
        return tr(grt)[None], tr(delta)[None], tr(m2)[None], tr(v2)[None]

    P_in = _halves(_split_shards(dW_in_main, dW_in_small, IN_COLS // 4))
    (A_in,) = _run_comm(_swap_comm(P_in), "rs_w_in_swap")
    S_in = _add_half([P_in], [A_in], cidx, "rs_w_in_add")
    dh1, (B_in,) = _mm(dproj, w_in_main, mode="nt", out_dtypes=[F32], name="mm_dh1", comm=_a2a_comm(S_in, 0, 8, cnt=7))
    dh1, (B_in,) = _mm(dsmall, w_in_small, mode="nt", out_dtypes=[F32], epi=add, extras=[dh1], name="mm_dh1_small",
                       comm=_a2a_comm(S_in, 7, 8, B_in))
    dx, _, d_gmix = _rms_bwd(dh1, xs, norm_mix_g, dx1, "rms1_bwd")

    (F_in,) = _run_comm(_fill_comm(own_sum(S_in, B_in, "w_in")), "rs_w_in_gather")
    big = {
        "w_in": update(F_in, w_in, m_w_in, v_w_in, "w_in"),
        "w_out": update(F_out, w_out, m_w_out, v_w_out, "w_out"),
        "w_up": update(F_up, w_up, m_w_up, v_w_up, "w_up"),
        "w_down": update(F_down, w_down, m_w_down, v_w_down, "w_down"),
        "w_ple_gate": update(F_pg, w_ple_gate, m_w_ple_gate, v_w_ple_gate, "w_pg"),
        "w_ple_proj": update(F_pp, w_ple_proj, m_w_ple_proj, v_w_ple_proj, "w_pp"),
    }

    small_grads = [d_gmix[0:1], dcw_a[:cw_a.shape[0]], jnp.concatenate([dcw_q, dcw_k, dcw_v], axis=1)[:cw_qkv.shape[0]],
                   d_ab[0:1, :H], d_ab[1:2, :H], d_gdn[0:1], d_gffn[0:1],
                   jnp.concatenate([dcw_fg, dcw_fv], axis=1)[:cw_ffn.shape[0]], d_gple[0:1], d_gfin]
    small_shapes = [v.shape for v in small_grads]
    gpack = _pack_rows(small_grads)
    gsum = _sum_stack(_gather_all(gpack, "ag_small"), "sum_small")
    (g_gmix, g_cwa, g_cwqkv, g_alog, g_dt, g_gdn, g_gffn, g_cwffn, g_gple, g_gfin) = _unpack_rows(gsum, small_shapes)

    def my_cols(v):
        Cc = v.shape[1] // 4
        return lax.dynamic_slice_in_dim(v, chip * Cc, Cc, axis=1)

    g_small = [g_gmix, my_cols(g_cwa), my_cols(g_cwqkv), g_alog, g_dt, g_gdn, g_gffn, my_cols(g_cwffn), g_gple, g_gfin]
    w_small = [norm_mix_g, conv_a_w[0], conv_qkv_w[0], a_log, dt_bias, dn_norm_g, norm_ffn_g, conv_ffn_w[0], norm_ple_g, gfin]
    m_small = [m_norm_mix_g, m_conv_a_w[0], m_conv_qkv_w[0], m_a_log, m_dt_bias, m_dn_norm_g, m_norm_ffn_g, m_conv_ffn_w[0],
               m_norm_ple_g, m_final_norm_g.reshape(1, D)]
    v_small = [v_norm_mix_g, v_conv_a_w[0], v_conv_qkv_w[0], v_a_log, v_dt_bias, v_dn_norm_g, v_norm_ffn_g, v_conv_ffn_w[0],
               v_norm_ple_g, v_final_norm_g.reshape(1, D)]
    shp = [v.shape for v in w_small]
    ds_, ms_, vs_ = _adamw(_pack_rows(w_small), _pack_rows(g_small), _pack_rows(m_small), _pack_rows(v_small), "adamw_small")
    out_shapes = [norm_mix_g.shape, conv_a_w.shape, conv_qkv_w.shape, a_log.shape, dt_bias.shape, dn_norm_g.shape,
                  norm_ffn_g.shape, conv_ffn_w.shape, norm_ple_g.shape, final_norm_g.shape]
    rs = lambda vals: [v.reshape(s) for v, s in zip(vals, out_shapes)]
    sg, sd_, sm_, sv_ = rs(g_small), rs(_unpack_rows(ds_, shp)), rs(_unpack_rows(ms_, shp)), rs(_unpack_rows(vs_, shp))
    names_small = ["norm_mix_g", "conv_a_w", "conv_qkv_w", "a_log", "dt_bias", "dn_norm_g", "norm_ffn_g", "conv_ffn_w",
                   "norm_ple_g", "final_norm_g"]
    res = {n: (sg[i], sd_[i], sm_[i], sv_[i]) for i, n in enumerate(names_small)}
    res.update(big)
    order = ["norm_mix_g", "w_in", "conv_a_w", "conv_qkv_w", "a_log", "dt_bias", "dn_norm_g", "w_out", "norm_ffn_g", "w_up",
             "conv_ffn_w", "w_down", "norm_ple_g", "w_ple_gate", "w_ple_proj", "final_norm_g"]
    return (loss, dx[None], *[res[n][0] for n in order], *[res[n][1] for n in order], *[res[n][2] for n in order],
            *[res[n][3] for n in order])
```

```python
import functools

import jax
import jax.numpy as jnp
from jax import lax
from jax.experimental import pallas as pl
from jax.experimental.pallas import tpu as pltpu

F32 = jnp.float32
BF16 = jnp.bfloat16
LANES = 128
HALO = 8
HEAD_DIM = 128
CHUNK = 64
EPS = 1e-6
VMEM_LIMIT = 56 * 1024 * 1024
MM_VMEM_BUDGET = 40 * 1024 * 1024
MM_STEP_BYTES = 1 << 20
EW_VMEM_BUDGET = 28 * 1024 * 1024
MESH = pl.DeviceIdType.MESH

ADAM_LR, ADAM_B1, ADAM_B2, ADAM_EPS, ADAM_WD, ADAM_STEP = 0.001, 0.9, 0.999, 1e-08, 0.01, 10


def _tile(n, cap, unit):
    if n <= cap:
        return n
    d = (cap // unit) * unit
    while d >= unit:
        if n % d == 0:
            return d
        d -= unit
    raise ValueError(f"no tile for {n} (cap {cap}, unit {unit})")


def _sigmoid(x):
    return 1.0 / (1.0 + jnp.exp(-x))


def _divisors(n, cap):
    ds = [d for d in range(cap // LANES * LANES, 0, -LANES) if n % d == 0]
    return [n] if (n <= cap or not ds) else ds


def _mm_tiles(M, N, K, n_unit, k_unit, a_bytes, n_blocks_mn, a_transposed, tn_full=False):
    best = None
    for tm in _divisors(M, 1536):
        for tn in ([N] if tn_full else _divisors(n_unit, 1536)):
            for tk in _divisors(k_unit, 4096):
                nk = K // tk
                vmem = 2 * tm * tk * a_bytes + 2 * tk * tn * 2 + 2 * 4 * tm * tn * n_blocks_mn + (4 * tm * tn if nk > 1 else 0)
                if vmem > MM_VMEM_BUDGET:
                    continue
                steps = (M // tm) * (N // tn) * nk
                b_reads = 1 if (nk == 1 and N == tn) else M // tm
                cost = (M * K * a_bytes * (N // tn if nk > 1 else 1) + K * N * 2 * b_reads + 4 * M * N * n_blocks_mn
                        + (8 * M * N * nk // 3 if nk > 1 else 0) + steps * MM_STEP_BYTES
                        + (2 * steps * tm * tk if a_transposed else 0))
                if best is None or cost < best[0]:
                    best = (cost, tm, tn, tk)
    return best[1:]


def _mm(a, b, *, mode, out_dtypes, name, epi=None, extras=(), comm=None, b_split=None, out_split=None, rows=(), parts=0):
    if b_split is not None:
        lo, ns = b_split
        Rb, Cb = b.shape[1], b.shape[2]
    if mode == "nn":
        (M, K), N = a.shape, (ns * Cb if b_split else b.shape[1])
    elif mode == "nt":
        (M, K), N = a.shape, (Rb if b_split else b.shape[0])
    else:
        (K, M), N = a.shape, b.shape[1]
    n_ex, n_out = len(extras), len(out_dtypes)
    n_unit = Cb if (b_split and mode == "nn") else (N // out_split if out_split else N)
    n_rows = len(rows)
    assert not (n_rows and (b_split or out_split))
    k_unit = Cb if (b_split and mode == "nt") else K
    mn_blocks = (sum(e.dtype.itemsize for e in extras) + sum(jnp.dtype(d).itemsize for d in out_dtypes)) / 4
    tm, tn, tk = _mm_tiles(M, N, K, n_unit, k_unit, a.dtype.itemsize, mn_blocks, mode == "tn", tn_full=bool(n_rows))
    nk = K // tk
    a_spec = pl.BlockSpec((tk, tm), lambda i, j, k: (k, i)) if mode == "tn" else pl.BlockSpec((tm, tk), lambda i, j, k: (i, k))
    if b_split and mode == "nn":
        nb = Cb // tn
        b_spec = pl.BlockSpec((None, tk, tn), lambda i, j, k: (lo + j // nb, k, j % nb))
    elif b_split:
        nb = Cb // tk
        b_spec = pl.BlockSpec((None, tn, tk), lambda i, j, k: (lo + k // nb, j, k % nb))
    else:
        b_spec = pl.BlockSpec((tn, tk), lambda i, j, k: (j, k)) if mode == "nt" else pl.BlockSpec((tk, tn), lambda i, j, k: (k, j))
    mn_spec = pl.BlockSpec((tm, tn), lambda i, j, k: (i, j))
    out_shapes = [jax.ShapeDtypeStruct((M, N), dt) for dt in out_dtypes] + [jax.ShapeDtypeStruct((M // tm * HALO, N), F32)] * parts
    out_specs = [mn_spec] * n_out + [pl.BlockSpec((HALO, tn), lambda i, j, k: (i, j))] * parts
    if out_split:
        assert n_ex == 0 and n_out == 1
        nbo = (N // out_split) // tn
        out_specs = [pl.BlockSpec((None, tm, tn), lambda i, j, k: (j // nbo, i, j % nbo))]
        out_shapes = [jax.ShapeDtypeStruct((out_split, M, N // out_split), out_dtypes[0])]
    dims = {"nn": (((1,), (0,)), ((), ())), "nt": (((1,), (1,)), ((), ())), "tn": (((0,), (0,)), ((), ()))}[mode]

    def body(*refs):
        a_ref, b_ref = refs[0], refs[1]
        ex_refs = refs[2:2 + n_ex + n_rows]
        out_refs = refs[2 + n_ex + n_rows:2 + n_ex + n_rows + n_out + parts]
        part = lax.dot_general(a_ref[...].astype(BF16), b_ref[...].astype(BF16), dims, preferred_element_type=F32)

        def finish(acc):
            outs = (acc,) if epi is None else epi(acc, *[r[...] for r in ex_refs])
            for r, o in zip(out_refs, outs):
                r[...] = o.astype(r.dtype)

        if nk == 1:
            finish(part)
            return
        acc_ref = refs[-1]
        k = pl.program_id(2)

        @pl.when(k == 0)
        def _():
            acc_ref[...] = part

        @pl.when(jnp.logical_and(k > 0, k < nk - 1))
        def _():
            acc_ref[...] += part

        @pl.when(k == nk - 1)
        def _():
            finish(acc_ref[...] + part)

    outs, comm_outs = _call(
        body, name=name, grid=(M // tm, N // tn, nk),
        in_specs=[a_spec, b_spec] + [mn_spec] * n_ex + [pl.BlockSpec((1, tn), lambda i, j, k: (0, j))] * n_rows,
        out_specs=out_specs,
        out_shape=out_shapes,
        scratch_shapes=[pltpu.VMEM((tm, tn), F32)] if nk > 1 else [],
        semantics=("parallel", "parallel", "arbitrary"), args=(a, b, *extras, *rows), comm=comm)
    res = outs[0] if n_out + parts == 1 else outs
    return res if comm is None else (res, comm_outs)


def _tiled(fn, *, T, C, ins, out_dtypes=(), acc_rows=(), tb=None, cb=512, name, comm=None, into=None):
    tb = _tile(T, tb or (512 if cb <= 1024 else 256), HALO)
    nI, nJ = T // tb, C // cb
    hb, nH = tb // HALO, T // HALO
    specs, args, kinds = [], [], []
    for kind, arr, cmap in ins:
        cm = cmap if cmap is not None else (lambda j: j)
        kinds.append(kind)
        if kind == "cur":
            specs.append(pl.BlockSpec((tb, cb), lambda j, i, cm=cm: (i, cm(j))))
            args.append(arr)
        elif kind == "ext":
            specs.append(pl.BlockSpec((HALO, cb), lambda j, i, cm=cm: (jnp.maximum(i * hb - 1, 0), cm(j))))
            specs.append(pl.BlockSpec((tb, cb), lambda j, i, cm=cm: (i, cm(j))))
            specs.append(pl.BlockSpec((HALO, cb), lambda j, i, cm=cm: (jnp.minimum((i + 1) * hb, nH - 1), cm(j))))
            args += [arr, arr, arr]
        elif kind == "row":
            specs.append(pl.BlockSpec((arr.shape[0], cb), lambda j, i, cm=cm: (0, cm(j))))
            args.append(arr)
        elif kind == "stack":
            specs.append(pl.BlockSpec((arr.shape[0], tb, cb), lambda j, i, cm=cm: (0, i, cm(j))))
            args.append(arr)
        else:
            raise ValueError(kind)
    n_in = len(args)
    n_out, n_acc = len(out_dtypes), len(acc_rows)

    def body(*refs):
        j, i = pl.program_id(0), pl.program_id(1)
        vals, r = [], 0
        for kind in kinds:
            if kind == "ext":
                prev = jnp.where(i == 0, 0.0, refs[r][...].astype(F32))
                cur = refs[r + 1][...].astype(F32)
                nxt = jnp.where(i == nI - 1, 0.0, refs[r + 2][...].astype(F32))
                vals.append(jnp.concatenate([prev, cur, nxt], axis=0))
                r += 3
            else:
                vals.append(refs[r][...])
                r += 1
        res = fn(j, i, *vals)
        for ref, o in zip(refs[n_in:n_in + n_out], res[:n_out]):
            ref[...] = o.astype(ref.dtype)
        for ref, o in zip(refs[n_in + n_out:], res[n_out:]):
            @pl.when(i == 0)
            def _(ref=ref, o=o):
                ref[...] = o

            @pl.when(i > 0)
            def _(ref=ref, o=o):
                ref[...] += o

    out_specs = [pl.BlockSpec((tb, cb), lambda j, i: (i, j))] * n_out
    out_shape = [jax.ShapeDtypeStruct((T, C), dt) for dt in out_dtypes]
    io_aliases = None
    if into is not None:
        buf, total, off = into
        assert n_out == 1 and comm is None
        out_specs = [pl.BlockSpec((tb, cb), lambda j, i: (i, j + off))]
        out_shape = [jax.ShapeDtypeStruct((T, total), out_dtypes[0])]
        if buf is not None:
            specs, args, io_aliases = specs + [ANY], args + [buf], {n_in: 0}
            n_in += 1
    outs, comm_outs = _call(
        body, name=name, grid=(nJ, nI), in_specs=specs,
        out_specs=out_specs + [pl.BlockSpec((rows, cb), lambda j, i: (0, j)) for rows in acc_rows],
        out_shape=out_shape + [jax.ShapeDtypeStruct((rows, C), F32) for rows in acc_rows],
        scratch_shapes=[], semantics=("parallel", "arbitrary"), args=args, comm=comm, io_aliases=io_aliases)
    return outs if comm is None else (outs, comm_outs)


def _conv_causal(xe, w):
    K = w.shape[0]
    y = xe * w[K - 1:K]
    for j in range(K - 1):
        y = y + pltpu.roll(xe, K - 1 - j, 0) * w[j:j + 1]
    return y


def _conv_anti(de, w):
    K, n = w.shape[0], de.shape[0]
    y = de * w[K - 1:K]
    for j in range(K - 1):
        y = y + pltpu.roll(de, n - (K - 1 - j), 0) * w[j:j + 1]
    return y


def _conv_dw(dce, xe, K):
    n = dce.shape[0]
    tb = n - 2 * HALO
    rows = []
    for j in range(K):
        xs = xe if j == K - 1 else pltpu.roll(xe, K - 1 - j, 0)
        rows.append(jnp.sum((dce * xs)[HALO:HALO + tb], axis=0, keepdims=True))
    rows.append(jnp.zeros((HALO - K, dce.shape[1]), F32))
    return jnp.concatenate(rows, axis=0)


def _own(xe):
    return xe[HALO:xe.shape[0] - HALO]


def _row0(v):
    return jnp.concatenate([v, jnp.zeros((HALO - 1, v.shape[1]), F32)], axis=0)


def _per_head(fn, *xs):
    n = xs[0].shape[1] // HEAD_DIM
    outs = [fn(*[x[:, g * HEAD_DIM:(g + 1) * HEAD_DIM] for x in xs]) for g in range(n)]
    return outs[0] if n == 1 else jnp.concatenate(outs, axis=1)


def _rms_fwd(x, g, name):
    T, D = x.shape

    def fn(j, i, xv, gv):
        r = lax.rsqrt(jnp.mean(xv * xv, axis=1, keepdims=True) + EPS)
        return (xv * r * gv,)

    return _tiled(fn, T=T, C=D, ins=[("cur", x, None), ("row", g, None)], out_dtypes=[BF16], cb=D, name=name)[0]


def _rms_bwd_math(dy, xv, gv):
    r = lax.rsqrt(jnp.mean(xv * xv, axis=1, keepdims=True) + EPS)
    xh = xv * r
    dxh = dy * gv
    dx = r * (dxh - xh * jnp.mean(dxh * xh, axis=1, keepdims=True))
    dg = jnp.sum(dy * xh, axis=0, keepdims=True)
    return dx, dg


def _rms_bwd(dh, x, g, dres, name, comm=None):
    T, D = x.shape

    def fn(j, i, dhv, xv, gv, dr):
        dx, dg = _rms_bwd_math(dhv, xv, gv)
        return dr + dx, dr + dx, _row0(dg)

    return _tiled(fn, T=T, C=D, ins=[("cur", dh, None), ("cur", x, None), ("row", g, None), ("cur", dres, None)],
                  out_dtypes=[F32, BF16], acc_rows=[HALO], cb=D, name=name, comm=comm)


def _ga_fwd(proj, w_a, CW, cb, total):
    T = proj.shape[0]
    n = CW // cb

    def fn(j, i, ax, ab, ac, w):
        c = _conv_causal(ac * ax, w)
        return (ab * _own(c),)

    return _tiled(fn, T=T, C=CW, ins=[("ext", proj, None), ("cur", proj, lambda j: j + n), ("ext", proj, lambda j: j + 2 * n),
                                       ("row", w_a, None)], out_dtypes=[BF16], cb=cb, name="ga_fwd", into=(None, total, 0))[0]


def _ga_bwd(dymix, proj, w_a, CW, cb):
    T = proj.shape[0]
    n = CW // cb
    K = w_a.shape[0]

    def fn(j, i, dy, ax, ab, ac, w):
        u = ac * ax
        c = _conv_causal(u, w)
        dc = dy * ab
        du = _conv_anti(dc, w)
        return _own(du * ac), _own(dy * c), _own(du * ax), _conv_dw(dc, u, K)

    return _tiled(fn, T=T, C=CW, ins=[("ext", dymix, None), ("ext", proj, None), ("ext", proj, lambda j: j + n),
                                       ("ext", proj, lambda j: j + 2 * n), ("row", w_a, None)],
                  out_dtypes=[BF16, BF16, BF16], acc_rows=[HALO], cb=cb, name="ga_bwd")


def _l2n(s):
    return s * lax.rsqrt(jnp.sum(s * s, axis=1, keepdims=True) + EPS)


def _qkv_fwd(proj, w_sec, coff, normalize, DNW, cb, name, comm=None):
    T = proj.shape[0]

    def fn(j, i, pre, w):
        c = _own(_conv_causal(pre, w))
        s = c * _sigmoid(c)
        return (_per_head(_l2n, s) if normalize else s,)

    res = _tiled(fn, T=T, C=DNW, ins=[("ext", proj, lambda j: j + coff), ("row", w_sec, None)],
                 out_dtypes=[F32], cb=cb, name=name, comm=comm)
    return res[0] if comm is None else (res[0][0], res[1])


def _qkv_bwd(dsec, proj, w_sec, coff, normalize, DNW, cb, name):
    T = proj.shape[0]
    K = w_sec.shape[0]

    def l2n_bwd(s, dn):
        r = lax.rsqrt(jnp.sum(s * s, axis=1, keepdims=True) + EPS)
        nrm = s * r
        return r * (dn - nrm * jnp.sum(dn * nrm, axis=1, keepdims=True))

    def fn(j, i, dn, pre, w):
        c = _conv_causal(pre, w)
        sg = _sigmoid(c)
        s = c * sg
        ds = _per_head(l2n_bwd, s, dn) if normalize else dn
        dc = ds * (sg * (1.0 + c * (1.0 - sg)))
        return _own(_conv_anti(dc, w)), _conv_dw(dc, pre, K)

    return _tiled(fn, T=T, C=DNW, ins=[("ext", dsec, None), ("ext", proj, lambda j: j + coff), ("row", w_sec, None)],
                  out_dtypes=[BF16], acc_rows=[HALO], cb=cb, name=name)


def _gb_fwd(small, a_log_row, dt_row, H):
    T = small.shape[0]

    def fn(j, i, sm, al, dt):
        z = sm + dt
        sp = jnp.maximum(z, 0.0) + jnp.log(1.0 + jnp.exp(-jnp.abs(z)))
        g = -jnp.exp(al) * sp
        beta = _sigmoid(pltpu.roll(sm, LANES - H, 1))
        return g, beta

    return _tiled(fn, T=T, C=LANES, ins=[("cur", small, None), ("row", a_log_row, None), ("row", dt_row, None)],
                  out_dtypes=[F32, F32], cb=LANES, name="gb_fwd")


def _gb_bwd(dgB, dbB, small, g, beta, a_log_row, dt_row, H):
    T = small.shape[0]

    def fn(j, i, dgv, dbv, sm, gv, bv, al, dt):
        lane = lax.broadcasted_iota(jnp.int32, sm.shape, 1)
        dg = jnp.zeros(sm.shape, F32)
        db = jnp.zeros(sm.shape, F32)
        for h in range(H):
            dg = jnp.where(lane == h, jnp.sum(dgv[h], axis=1, keepdims=True), dg)
            db = jnp.where(lane == h, jnp.sum(dbv[h], axis=1, keepdims=True), db)
        da = dg * (-jnp.exp(al)) * _sigmoid(sm + dt)
        dbb = db * bv * (1.0 - bv)
        dsm = jnp.where(lane < H, da, 0.0) + pltpu.roll(jnp.where(lane < H, dbb, 0.0), H, 1)
        d_alog = jnp.sum(jnp.where(lane < H, dg * gv, 0.0), axis=0, keepdims=True)
        d_dt = jnp.sum(jnp.where(lane < H, da, 0.0), axis=0, keepdims=True)
        return dsm, jnp.concatenate([d_alog, d_dt, jnp.zeros((HALO - 2, LANES), F32)], axis=0)

    return _tiled(fn, T=T, C=LANES, ins=[("stack", dgB, None), ("stack", dbB, None), ("cur", small, None), ("cur", g, None),
                                          ("cur", beta, None), ("row", a_log_row, None), ("row", dt_row, None)],
                  out_dtypes=[BF16], acc_rows=[HALO], cb=LANES, name="gb_bwd")


_DIMS = {"nn": (((1,), (0,)), ((), ())), "nt": (((1,), (1,)), ((), ())), "tn": (((0,), (0,)), ((), ()))}
_DOT_BWD = {"nn": (("nt", "gb"), ("tn", "ag")), "nt": (("nn", "gb"), ("tn", "ga")), "tn": (("nt", "bg"), ("nn", "ag"))}


def _split(a):
    hi = a.astype(BF16)
    return hi, (a - hi.astype(F32)).astype(BF16)


def _raw_dot(a, b, kind, passes):
    dg = lambda x, y: lax.dot_general(x, y, _DIMS[kind], preferred_element_type=F32)
    if passes == 1:
        return dg(a.astype(BF16), b.astype(BF16))
    ah, al = _split(a)
    bh, bl = _split(b)
    if kind == "tn":
        return dg(ah, bh) + (dg(ah, bl) + dg(al, bh))
    m = a.shape[0]
    top = dg(jnp.concatenate([ah, al], axis=0), bh)
    return top[:m] + (dg(ah, bl) + top[m:])


def _raw_dot_exact(a, b, kind, exact):
    dg = lambda x, y: lax.dot_general(x, y, _DIMS[kind], preferred_element_type=F32)
    if exact == "a":
        bh, bl = _split(b)
        return dg(a.astype(BF16), bh) + dg(a.astype(BF16), bl)
    ah, al = _split(a)
    return dg(ah, b.astype(BF16)) + dg(al, b.astype(BF16))


@functools.lru_cache(maxsize=None)
def _dotc(kind):
    @jax.custom_vjp
    def f(a, b):
        return _raw_dot_exact(a, b, kind, "a")

    def fwd(a, b):
        return _raw_dot_exact(a, b, kind, "a"), a

    def bwd(a, g):
        db = _raw_dot_exact(a, g, "tn", "a") if kind == "nn" else _raw_dot_exact(g, a, "tn", "b")
        return jnp.zeros_like(a), db

    f.defvjp(fwd, bwd)
    return f


@functools.lru_cache(maxsize=None)
def _dotf(kind, passes):
    @jax.custom_vjp
    def f(a, b):
        return _raw_dot(a, b, kind, passes)

    def fwd(a, b):
        return _raw_dot(a, b, kind, passes), (a, b)

    def bwd(res, g):
        ops = {"a": res[0], "b": res[1], "g": g}
        (ka, oa), (kb, ob) = _DOT_BWD[kind]
        return (_raw_dot(ops[oa[0]], ops[oa[1]], ka, passes), _raw_dot(ops[ob[0]], ops[ob[1]], kb, passes))

    f.defvjp(fwd, bwd)
    return f


@jax.custom_vjp
def _saved_inverse(L, inv):
    return inv


def _saved_inverse_fwd(L, inv):
    return inv, inv


def _saved_inverse_bwd(inv, g):
    d3nt, d3tn = _dotf("nt", 3), _dotf("tn", 3)
    return -d3nt(d3tn(inv, g), inv), jnp.zeros_like(inv)


_saved_inverse.defvjp(_saved_inverse_fwd, _saved_inverse_bwd)


def _chunk_fn(q, k, v, gB, bB, S, inv_saved=None):
    C = CHUNK
    d3 = _dotf("nn", 3)
    d1, d1nt, d1tn = _dotf("nn", 1), _dotf("nt", 1), _dotf("tn", 1)
    each = lambda f, *ls: tuple(f(*xs) for xs in zip(*ls))
    row = lax.broadcasted_iota(jnp.int32, (C, C), 0)
    col = lax.broadcasted_iota(jnp.int32, (C, C), 1)
    causal = row >= col
    strict = row > col
    tril = jnp.where(causal, 1.0, 0.0).astype(F32)
    eye = jnp.where(row == col, 1.0, 0.0).astype(F32)
    avg = jnp.full((C, HEAD_DIM), 1.0 / HEAD_DIM, F32)
    gc = each(lambda g: _dotc("nn")(tril, g), gB)
    R = each(lambda g: _dotc("nt")(avg, g), gc)
    decay = each(lambda g, r: jnp.where(causal, jnp.exp(jnp.where(causal, g[:, :C] - r, 0.0)), 0.0), gc, R)
    kk = each(lambda x: d1nt(x, x), k)
    L = each(lambda a, d, b: jnp.where(strict, a * d * b[:, :C], 0.0), kk, decay, bB)
    if inv_saved is None:
        inv = each(lambda l: eye - l, L)
        P = L
        for _ in range(5):
            P = each(lambda p: d3(p, p), P)
            inv = each(lambda a, p: d3(a, eye + p), inv, P)
    else:
        inv = each(_saved_inverse, L, inv_saved)
    eg = each(jnp.exp, gc)
    u = each(lambda a, x, b: d3(a, x * b), inv, v, bB)
    w = each(lambda a, x, b, e: d3(a, x * b * e), inv, k, bB, eg)
    qs = each(lambda x: x * (HEAD_DIM ** -0.5), q)
    qk = each(lambda a, x, d: d1nt(a, x) * d, qs, k, decay)
    gl = each(lambda g: g[C - 1:C, :], gc)
    kd = each(lambda x, a, g: x * jnp.exp(a - g), k, gl, gc)
    qe = each(lambda a, e: a * e, qs, eg)
    nh = len(S)
    o = ()
    for c in range(len(q) // nh):
        sl = slice(c * nh, (c + 1) * nh)
        v_new = each(lambda a, b, s: a - d1(b, s), u[sl], w[sl], S)
        o1 = each(lambda a, s: d1(a, s), qe[sl], S)
        o += each(lambda a, b, vn: a + d1(b, vn), o1, qk[sl], v_new)
        kv = each(lambda x, vn: d1tn(x, vn), kd[sl], v_new)
        S = each(lambda s, a, b: s * jnp.exp(a) + b, S, gl[sl], kv)
    return (o, S), inv


def _sel_lane(x, h):
    lane = lax.broadcasted_iota(jnp.int32, x.shape, 1)
    return jnp.broadcast_to(jnp.sum(jnp.where(lane == h, x, 0.0), axis=1, keepdims=True), x.shape)


def _tile_of(ref, c, h):
    return ref[c * CHUNK:(c + 1) * CHUNK, h * HEAD_DIM:(h + 1) * HEAD_DIM]


def _chunks_per_step(N):
    return 4 if N % 4 == 0 else (2 if N % 2 == 0 else 1)


def _delta_fwd(q, k, v, g, beta, comm=None):
    T = q.shape[0]
    H, N = q.shape[1] // HEAD_DIM, T // CHUNK
    cps = _chunks_per_step(N)
    rows = cps * CHUNK

    def body(q_ref, k_ref, v_ref, g_ref, b_ref, o_ref, s_ref, inv_ref, S):
        @pl.when(pl.program_id(0) == 0)
        def _():
            S[...] = jnp.zeros_like(S)

        gv, bv = g_ref[...], b_ref[...]
        pairs = lambda f: tuple(f(c, h) for c in range(cps) for h in range(H))
        S_in = tuple(S[h] for h in range(H))
        for h in range(H):
            s_ref[h, 0] = S_in[h]
        (o, S_new), inv = _chunk_fn(pairs(lambda c, h: _tile_of(q_ref, c, h)), pairs(lambda c, h: _tile_of(k_ref, c, h)),
                                    pairs(lambda c, h: _tile_of(v_ref, c, h)),
                                    pairs(lambda c, h: _sel_lane(gv[c * CHUNK:(c + 1) * CHUNK], h)),
                                    pairs(lambda c, h: _sel_lane(bv[c * CHUNK:(c + 1) * CHUNK], h)), S_in)
        for c in range(cps):
            for h in range(H):
                o_ref[c * CHUNK:(c + 1) * CHUNK, h * HEAD_DIM:(h + 1) * HEAD_DIM] = o[c * H + h]
                inv_ref[h, c] = inv[c * H + h]
        for h in range(H):
            S[h] = S_new[h]

    blk = pl.BlockSpec((rows, H * HEAD_DIM), lambda n: (n, 0))
    gblk = pl.BlockSpec((rows, LANES), lambda n: (n, 0))
    outs, comm_outs = _call(
        body, name="delta_fwd", grid=(N // cps,), in_specs=[blk, blk, blk, gblk, gblk],
        out_specs=[blk, pl.BlockSpec((H, 1, HEAD_DIM, HEAD_DIM), lambda n: (0, n, 0, 0)),
                   pl.BlockSpec((H, cps, CHUNK, CHUNK), lambda n: (0, n, 0, 0))],
        out_shape=[jax.ShapeDtypeStruct((T, H * HEAD_DIM), F32), jax.ShapeDtypeStruct((H, N // cps, HEAD_DIM, HEAD_DIM), F32),
                   jax.ShapeDtypeStruct((H, N, CHUNK, CHUNK), F32)],
        scratch_shapes=[pltpu.VMEM((H, HEAD_DIM, HEAD_DIM), F32)],
        semantics=("arbitrary",), args=(q, k, v, g, beta), comm=comm)
    return outs[0], outs[1], outs[2], comm_outs


def _delta_bwd(q, k, v, g, beta, S0, inv, do, comm=None):
    T = q.shape[0]
    H, N = q.shape[1] // HEAD_DIM, T // CHUNK
    cps = _chunks_per_step(N)
    rows, NS = cps * CHUNK, N // cps

    def body(q_ref, k_ref, v_ref, g_ref, b_ref, s_ref, inv_ref, do_ref, dq_ref, dk_ref, dv_ref, dg_ref, db_ref, dS):
        @pl.when(pl.program_id(0) == 0)
        def _():
            dS[...] = jnp.zeros_like(dS)

        gv, bv = g_ref[...], b_ref[...]
        pairs = lambda f: tuple(f(c, h) for c in range(cps) for h in range(H))
        heads = lambda f: tuple(f(h) for h in range(H))
        _, vjp, _ = jax.vjp(_chunk_fn, pairs(lambda c, h: _tile_of(q_ref, c, h)), pairs(lambda c, h: _tile_of(k_ref, c, h)),
                            pairs(lambda c, h: _tile_of(v_ref, c, h)),
                            pairs(lambda c, h: _sel_lane(gv[c * CHUNK:(c + 1) * CHUNK], h)),
                            pairs(lambda c, h: _sel_lane(bv[c * CHUNK:(c + 1) * CHUNK], h)),
                            heads(lambda h: s_ref[h, 0]), pairs(lambda c, h: inv_ref[h, c]), has_aux=True)
        dq, dk, dv, dgB, dbB, dS_prev, _ = vjp((pairs(lambda c, h: _tile_of(do_ref, c, h)), heads(lambda h: dS[h])))
        for c in range(cps):
            for h in range(H):
                r, sl = slice(c * CHUNK, (c + 1) * CHUNK), slice(h * HEAD_DIM, (h + 1) * HEAD_DIM)
                dq_ref[r, sl] = dq[c * H + h]
                dk_ref[r, sl] = dk[c * H + h]
                dv_ref[r, sl] = dv[c * H + h]
                dg_ref[h, r] = dgB[c * H + h]
                db_ref[h, r] = dbB[c * H + h]
        for h in range(H):
            dS[h] = dS_prev[h]

    blk = pl.BlockSpec((rows, H * HEAD_DIM), lambda n: (NS - 1 - n, 0))
    gblk = pl.BlockSpec((rows, LANES), lambda n: (NS - 1 - n, 0))
    hblk = pl.BlockSpec((H, rows, LANES), lambda n: (0, NS - 1 - n, 0))
    sd = jax.ShapeDtypeStruct
    outs, comm_outs = _call(
        body, name="delta_bwd", grid=(NS,),
        in_specs=[blk, blk, blk, gblk, gblk, pl.BlockSpec((H, 1, HEAD_DIM, HEAD_DIM), lambda n: (0, NS - 1 - n, 0, 0)),
                  pl.BlockSpec((H, cps, CHUNK, CHUNK), lambda n: (0, NS - 1 - n, 0, 0)), blk],
        out_specs=[blk, blk, blk, hblk, hblk],
        out_shape=[sd((T, H * HEAD_DIM), F32)] * 3 + [sd((H, T, LANES), F32)] * 2,
        scratch_shapes=[pltpu.VMEM((H, HEAD_DIM, HEAD_DIM), F32)],
        semantics=("arbitrary",), args=(q, k, v, g, beta, S0, inv, do), comm=comm)
    return (*outs, comm_outs)


def _gnorm_fwd(o, proj, z_coff, gdn_t, DNW, buf, coff):
    T = o.shape[0]

    def fn(j, i, ov, zv, gv):
        def one(oh, zh, gh):
            r = lax.rsqrt(jnp.mean(oh * oh, axis=1, keepdims=True) + EPS)
            return oh * r * gh * (zh * _sigmoid(zh))
        return (_per_head(one, ov, zv, jnp.broadcast_to(gv, ov.shape)),)

    return _tiled(fn, T=T, C=DNW, ins=[("cur", o, None), ("cur", proj, lambda j: j + z_coff), ("row", gdn_t, None)],
                  out_dtypes=[BF16], cb=DNW, name="gnorm_fwd", into=(buf, buf.shape[1], coff))[0]


def _gnorm_bwd(dymix, y_coff, o, proj, z_coff, gdn_t, DNW):
    T = o.shape[0]
    nh = DNW // HEAD_DIM

    def fn(j, i, dy, ov, zv, gv):
        dos, dzs, dgs = [], [], jnp.zeros((1, HEAD_DIM), F32)
        for h in range(nh):
            sl = slice(h * HEAD_DIM, (h + 1) * HEAD_DIM)
            dyh, oh, zh, gh = dy[:, sl].astype(F32), ov[:, sl], zv[:, sl], gv[:, sl]
            r = lax.rsqrt(jnp.mean(oh * oh, axis=1, keepdims=True) + EPS)
            on = oh * r
            sg = _sigmoid(zh)
            sz = zh * sg
            dzs.append(dyh * on * gh * (sg * (1.0 + zh * (1.0 - sg))))
            don = dyh * gh * sz
            dos.append(r * (don - on * jnp.mean(don * on, axis=1, keepdims=True)))
            dgs = dgs + jnp.sum(dyh * on * sz, axis=0, keepdims=True)
        cat = (lambda xs: xs[0] if nh == 1 else jnp.concatenate(xs, axis=1))
        return cat(dos), cat(dzs), _row0(dgs)

    T_ = T
    nI = T_ // _tile(T_, 256, HALO)
    tb = T_ // nI
    specs_cb = DNW

    def body_wrap():
        def body(dy_ref, o_ref, z_ref, g_ref, do_ref, dz_ref, dg_ref):
            i = pl.program_id(0)
            d_o, d_z, d_g = fn(0, i, dy_ref[...], o_ref[...], z_ref[...], g_ref[...])
            do_ref[...] = d_o
            dz_ref[...] = d_z.astype(dz_ref.dtype)

            @pl.when(i == 0)
            def _():
                dg_ref[...] = d_g

            @pl.when(i > 0)
            def _():
                dg_ref[...] += d_g

        return pl.pallas_call(
            body, name="gnorm_bwd", grid=(nI,),
            in_specs=[pl.BlockSpec((tb, specs_cb), lambda i: (i, y_coff)), pl.BlockSpec((tb, specs_cb), lambda i: (i, 0)),
                      pl.BlockSpec((tb, specs_cb), lambda i: (i, z_coff)), pl.BlockSpec((1, specs_cb), lambda i: (0, 0))],
            out_specs=[pl.BlockSpec((tb, specs_cb), lambda i: (i, 0)), pl.BlockSpec((tb, specs_cb), lambda i: (i, 0)),
                       pl.BlockSpec((HALO, HEAD_DIM), lambda i: (0, 0))],
            out_shape=[jax.ShapeDtypeStruct((T_, DNW), F32), jax.ShapeDtypeStruct((T_, DNW), BF16),
                       jax.ShapeDtypeStruct((HALO, HEAD_DIM), F32)],
            compiler_params=pltpu.CompilerParams(dimension_semantics=("arbitrary",), vmem_limit_bytes=VMEM_LIMIT),
        )(dymix, o, proj, gdn_t)

    return body_wrap()


def _ffn_fwd(up_g, up_v, w_g, w_v, cb):
    T, F = up_g.shape

    def fn(j, i, ug, uv, wg, wv):
        cg = _own(_conv_causal(ug, wg))
        cv = _own(_conv_causal(uv, wv))
        return (cg * _sigmoid(cg) * cv,)

    return _tiled(fn, T=T, C=F, ins=[("ext", up_g, None), ("ext", up_v, None), ("row", w_g, None), ("row", w_v, None)],
                  out_dtypes=[BF16], tb=1024, cb=cb, name="ffn_fwd")[0]


def _ffn_bwd(dact, up_g, up_v, w_g, w_v, cb):
    T, F = up_g.shape
    K = w_g.shape[0]

    def fn(j, i, da, ug, uv, wg, wv):
        cg = _conv_causal(ug, wg)
        cv = _conv_causal(uv, wv)
        sg = _sigmoid(cg)
        dgate = da * cv * (sg * (1.0 + cg * (1.0 - sg)))
        dval = da * (cg * sg)
        return (_own(_conv_anti(dgate, wg)), _own(_conv_anti(dval, wv)), _conv_dw(dgate, ug, K), _conv_dw(dval, uv, K))

    return _tiled(fn, T=T, C=F, ins=[("ext", dact, None), ("ext", up_g, None), ("ext", up_v, None), ("row", w_g, None),
                                      ("row", w_v, None)], out_dtypes=[BF16, BF16], acc_rows=[HALO, HALO], tb=1024, cb=cb,
                  name="ffn_bwd")


def _wide(R, Cc, n_f32, unit=HALO):
    cb = Cc if (Cc % LANES or Cc <= 4096) else _tile(Cc, 2048, LANES)
    cap = max(unit, EW_VMEM_BUDGET // (2 * 4 * n_f32 * cb) // unit * unit)
    return _tile(R, cap, unit), cb


def _adamw(w, g, m, v, name, comm=None):
    R, Cc = w.shape
    tb, cb = _wide(R, Cc, 7) if R % HALO == 0 else (R, _tile(Cc, EW_VMEM_BUDGET // (2 * 4 * 7 * R) // LANES * LANES, LANES))
    c1 = 1.0 / (1.0 - ADAM_B1 ** ADAM_STEP)
    c2 = 1.0 / (1.0 - ADAM_B2 ** ADAM_STEP)

    def fn(j, i, wv, gv, mv, vv):
        m2 = ADAM_B1 * mv + (1.0 - ADAM_B1) * gv
        v2 = ADAM_B2 * vv + (1.0 - ADAM_B2) * (gv * gv)
        delta = -ADAM_LR * ((m2 * c1) / (jnp.sqrt(v2 * c2) + ADAM_EPS) + ADAM_WD * wv)
        return delta, m2, v2

    return _tiled(fn, T=R, C=Cc, ins=[("cur", w, None), ("cur", g, None), ("cur", m, None), ("cur", v, None)],
                  out_dtypes=[F32, F32, F32], tb=tb, cb=cb, name=name, comm=comm)


def _join_shards(w4, n_main):
    S4, R, cs = w4.shape
    n_small = S4 * cs - n_main
    assert 0 < n_small <= LANES and n_small <= cs
    tb = _tile(R, 256, 2 * HALO)

    def body(w_ref, main_ref, small_ref):
        for t in range(S4 - 1):
            main_ref[:, t * cs:(t + 1) * cs] = w_ref[t]
        last = w_ref[S4 - 1]
        main_ref[:, (S4 - 1) * cs:] = last[:, :cs - n_small]
        small_ref[...] = jnp.zeros_like(small_ref)
        small_ref[:, :n_small] = last[:, cs - n_small:]

    return pl.pallas_call(
        body, name="join_w_in", grid=(R // tb,), in_specs=[pl.BlockSpec((S4, tb, cs), lambda i: (0, i, 0))],
        out_specs=[pl.BlockSpec((tb, n_main), lambda i: (i, 0)), pl.BlockSpec((tb, LANES), lambda i: (i, 0))],
        out_shape=[jax.ShapeDtypeStruct((R, n_main), w4.dtype), jax.ShapeDtypeStruct((R, LANES), w4.dtype)],
        compiler_params=pltpu.CompilerParams(dimension_semantics=("parallel",), vmem_limit_bytes=VMEM_LIMIT))(w4)


def _split_shards(main, small, cs):
    R, n_main = main.shape
    n_small = 4 * cs - n_main
    tb = _tile(R, 256, HALO)

    def body(main_ref, small_ref, out_ref):
        for t in range(3):
            out_ref[t] = main_ref[:, t * cs:(t + 1) * cs]
        out_ref[3, :, :cs - n_small] = main_ref[:, 3 * cs:]
        out_ref[3, :, cs - n_small:] = small_ref[:, :n_small]

    return pl.pallas_call(
        body, name="split_g_in", grid=(R // tb,),
        in_specs=[pl.BlockSpec((tb, n_main), lambda i: (i, 0)), pl.BlockSpec((tb, LANES), lambda i: (i, 0))],
        out_specs=pl.BlockSpec((4, tb, cs), lambda i: (0, i, 0)), out_shape=jax.ShapeDtypeStruct((4, R, cs), main.dtype),
        compiler_params=pltpu.CompilerParams(dimension_semantics=("parallel",), vmem_limit_bytes=VMEM_LIMIT))(main, small)


def _sum_stack(st, name):
    S, R, Cc = st.shape
    cb = _tile(Cc, 512, LANES) if Cc % LANES == 0 else Cc

    def fn(j, i, sv):
        t = sv[0]
        for s in range(1, S):
            t = t + sv[s]
        return (t,)

    return _tiled(fn, T=R, C=Cc, ins=[("stack", st, None)], out_dtypes=[F32], cb=cb, name=name)[0]


ANY = pl.BlockSpec(memory_space=pl.ANY)


def _place():
    x, y, c = lax.axis_index("x"), lax.axis_index("y"), lax.axis_index("c")
    return x, y, c, 2 * x + y


def _chip_dev(s, c):
    return (s // 2, s % 2, c)


class _Comm:
    def __init__(self, ins, out_shapes, sems, start, wait, aliases=None):
        self.ins, self.out_shapes, self.sems = list(ins), list(out_shapes), list(sems)
        self.start, self.wait, self.aliases = start, wait, dict(aliases or {})


def _merge(*comms):
    offs, i, o, s = [], 0, 0, 0
    for cm in comms:
        offs.append((i, o, s))
        i, o, s = i + len(cm.ins), o + len(cm.out_shapes), s + len(cm.sems)

    def part(refs, k, cm):
        i0, o0, s0 = offs[k]
        return refs[0][i0:i0 + len(cm.ins)], refs[1][o0:o0 + len(cm.out_shapes)], refs[2][s0:s0 + len(cm.sems)]

    def start(*refs):
        for k, cm in enumerate(comms):
            cm.start(*part(refs, k, cm))

    def wait(*refs):
        for k, cm in enumerate(comms):
            cm.wait(*part(refs, k, cm))

    aliases = {}
    for k, cm in enumerate(comms):
        for a, b in cm.aliases.items():
            aliases[offs[k][0] + a] = offs[k][1] + b
    return _Comm([a for cm in comms for a in cm.ins], [a for cm in comms for a in cm.out_shapes],
                 [a for cm in comms for a in cm.sems], start, wait, aliases)


def _call(body, *, name, grid, in_specs, out_specs, out_shape, scratch_shapes, semantics, args, comm=None, io_aliases=None):
    if comm is None:
        outs = pl.pallas_call(
            body, name=name, grid=grid, in_specs=in_specs, out_specs=out_specs, out_shape=out_shape,
            scratch_shapes=list(scratch_shapes), input_output_aliases=dict(io_aliases or {}),
            compiler_params=pltpu.CompilerParams(dimension_semantics=semantics, vmem_limit_bytes=VMEM_LIMIT))(*args)
        return list(outs), []
    assert not io_aliases
    n_in, n_out, n_scr = len(in_specs), len(out_specs), len(scratch_shapes)
    ci, co = len(comm.ins), len(comm.out_shapes)

    def wrapped(*refs):
        r = 0
        ins, r = refs[r:r + n_in], r + n_in
        cins, r = refs[r:r + ci], r + ci
        outs, r = refs[r:r + n_out], r + n_out
        couts, r = refs[r:r + co], r + co
        scr, r = refs[r:r + n_scr], r + n_scr
        csems = refs[r:]
        ids = [pl.program_id(a) for a in range(len(grid))]
        first, last = ids[0] == 0, ids[0] == grid[0] - 1
        for a in range(1, len(grid)):
            first = jnp.logical_and(first, ids[a] == 0)
            last = jnp.logical_and(last, ids[a] == grid[a] - 1)

        @pl.when(first)
        def _():
            comm.start(cins, couts, csems)

        body(*ins, *outs, *scr)

        @pl.when(last)
        def _():
            comm.wait(cins, couts, csems)

    outs = pl.pallas_call(
        wrapped, name=name, grid=grid, in_specs=list(in_specs) + [ANY] * ci, out_specs=list(out_specs) + [ANY] * co,
        out_shape=list(out_shape) + comm.out_shapes, scratch_shapes=list(scratch_shapes) + comm.sems,
        input_output_aliases={n_in + a: n_out + b for a, b in comm.aliases.items()},
        compiler_params=pltpu.CompilerParams(dimension_semantics=("arbitrary",) * len(grid), vmem_limit_bytes=VMEM_LIMIT),
    )(*args, *comm.ins)
    return list(outs[:n_out]), list(outs[n_out:])


def _run_comm(comm, name):
    ci, co = len(comm.ins), len(comm.out_shapes)

    def body(*refs):
        cins, couts, csems = refs[:ci], refs[ci:ci + co], refs[ci + co:]
        comm.start(cins, couts, csems)
        comm.wait(cins, couts, csems)

    outs = pl.pallas_call(body, name=name, in_specs=[ANY] * ci, out_specs=[ANY] * co, out_shape=comm.out_shapes,
                          scratch_shapes=comm.sems, input_output_aliases=comm.aliases)(*comm.ins)
    return list(outs)


def _ag_comm(shard, land=None, q=0, nq=1):
    two, R2, Cc = shard.shape
    rows = pl.ds(q * (R2 // nq), R2 // nq)
    DMA = pltpu.SemaphoreType.DMA

    def copies(ins, outs, sems, which):
        sh, out = ins[0], outs[0]
        send1, recv1, send2, recv2, send0, recv0 = sems
        x, y, c, s = _place()
        sib = (x, y, 1 - c)
        rc = pltpu.make_async_remote_copy
        if which == "first":
            return [rc(sh.at[c, rows], out.at[s, c, rows], send1.at[m - 1], recv1.at[m - 1],
                       device_id=_chip_dev(s ^ m, c), device_id_type=MESH) for m in range(1, 4)]
        if which == "own":
            return [rc(sh.at[h, rows], out.at[s, h, rows], send0.at[h], recv0.at[h], device_id=sib, device_id_type=MESH)
                    for h in range(2)]
        if which == "landed":
            return [rc(sh.at[c, rows], out.at[s ^ m, c, rows], send1.at[m - 1], recv1.at[m - 1], device_id=sib,
                       device_id_type=MESH) for m in range(1, 4)]
        half = c if which == "passed" else 1 - c
        return [rc(out.at[s ^ m, half, rows], out.at[s ^ m, half, rows], send2.at[m - 1], recv2.at[m - 1], device_id=sib,
                   device_id_type=MESH) for m in range(1, 4)]

    def start(ins, outs, sems):
        for cp in copies(ins, outs, sems, "first") + copies(ins, outs, sems, "own"):
            cp.start()

    def wait(ins, outs, sems):
        passed = copies(ins, outs, sems, "passed")
        for lan, pas in zip(copies(ins, outs, sems, "landed"), passed):
            lan.wait_recv()
            pas.start()
        for cp in copies(ins, outs, sems, "handed"):
            cp.wait_recv()
        for cp in copies(ins, outs, sems, "own"):
            cp.wait()
        for cp in copies(ins, outs, sems, "first") + passed:
            cp.wait_send()

    return _Comm([shard] + ([land] if land is not None else []), [jax.ShapeDtypeStruct((4, two, R2, Cc), shard.dtype)],
                 [DMA((3,)), DMA((3,)), DMA((3,)), DMA((3,)), DMA((2,)), DMA((2,))], start, wait,
                 {1: 0} if land is not None else None)


def _a2a_comm(S1, q=0, nq=1, land=None, cnt=1):
    S4, R2, Cc = S1.shape
    rows = pl.ds(q * (R2 // nq), cnt * (R2 // nq))
    DMA = pltpu.SemaphoreType.DMA

    def copies(ins, outs, sems):
        x, y, c, s = _place()
        return [pltpu.make_async_remote_copy(ins[0].at[s ^ m, rows], outs[0].at[m - 1, rows], sems[0].at[m - 1],
                                             sems[1].at[m - 1], device_id=_chip_dev(s ^ m, c), device_id_type=MESH)
                for m in range(1, 4)]

    def start(ins, outs, sems):
        for cp in copies(ins, outs, sems):
            cp.start()

    def wait(ins, outs, sems):
        for cp in copies(ins, outs, sems):
            cp.wait()

    return _Comm([S1] + ([land] if land is not None else []), [jax.ShapeDtypeStruct((3, R2, Cc), S1.dtype)],
                 [DMA((3,)), DMA((3,))], start, wait, {1: 0} if land is not None else None)


def _halves(G):
    return G.reshape(G.shape[0], 2, G.shape[1] // 2, G.shape[2])


def _swap_comm(piece):
    n, two, R2, Cc = piece.shape
    DMA = pltpu.SemaphoreType.DMA

    def copies(ins, outs, sems):
        x, y, c, s = _place()
        return [pltpu.make_async_remote_copy(ins[0].at[t, 1 - c], outs[0].at[t], sems[0].at[t], sems[1].at[t],
                                             device_id=(x, y, 1 - c), device_id_type=MESH) for t in range(n)]

    def start(ins, outs, sems):
        for cp in copies(ins, outs, sems):
            cp.start()

    def wait(ins, outs, sems):
        for cp in copies(ins, outs, sems):
            cp.wait()

    return _Comm([piece], [jax.ShapeDtypeStruct((n, R2, Cc), piece.dtype)], [DMA((n,)), DMA((n,))], start, wait)


def _add_half(pieces, As, cidx, name):
    R2, Cc = pieces[0].shape[2:]
    S4 = sum(pc.shape[0] for pc in pieces)
    tb, cb = _wide(R2, Cc, 3, 2 * HALO)
    nI, nJ = R2 // tb, Cc // cb

    def body(c_ref, g_ref, a_ref, *rest):
        rest[-1][...] = (g_ref[0, 0] + a_ref[0]).astype(BF16)

    out, t0 = None, 0
    for k, (pc, A) in enumerate(zip(pieces, As)):
        grid_spec = pltpu.PrefetchScalarGridSpec(
            num_scalar_prefetch=1, grid=(pc.shape[0], nI, nJ),
            in_specs=[pl.BlockSpec((1, 1, tb, cb), lambda t, i, j, c_ref: (t, c_ref[0], i, j)),
                      pl.BlockSpec((1, tb, cb), lambda t, i, j, c_ref: (t, i, j))] + ([ANY] if k else []),
            out_specs=pl.BlockSpec((tb, cb), lambda t, i, j, c_ref, t0=t0: ((t0 + t) * nI + i, j)))
        out = pl.pallas_call(
            functools.partial(body), name=f"{name}{k}", grid_spec=grid_spec, out_shape=jax.ShapeDtypeStruct((S4 * R2, Cc), BF16),
            input_output_aliases={3: 0} if k else {},
            compiler_params=pltpu.CompilerParams(dimension_semantics=("parallel", "parallel", "parallel"),
                                                 vmem_limit_bytes=VMEM_LIMIT),
        )(*((cidx, pc, A) + ((out,) if k else ())))
        t0 += pc.shape[0]
    return out.reshape(S4, R2, Cc)


def _add_own(S1, B, chip_idx, cidx, name):
    S4, R2, Cc = S1.shape
    tb, cb = _wide(R2, Cc, 3, 2 * HALO)

    def body(s_idx, c_idx, s_ref, b_ref, o_ref):
        o_ref[...] = ((s_ref[0].astype(F32) + b_ref[0].astype(F32)) + b_ref[1].astype(F32)) + b_ref[2].astype(F32)

    grid_spec = pltpu.PrefetchScalarGridSpec(
        num_scalar_prefetch=2, grid=(R2 // tb, Cc // cb),
        in_specs=[pl.BlockSpec((1, tb, cb), lambda i, j, s_idx, c_idx: (s_idx[0], i, j)),
                  pl.BlockSpec((3, tb, cb), lambda i, j, s_idx, c_idx: (0, i, j))],
        out_specs=pl.BlockSpec((None, tb, cb), lambda i, j, s_idx, c_idx: (c_idx[0], i, j)))
    return pl.pallas_call(body, name=name, grid_spec=grid_spec, out_shape=jax.ShapeDtypeStruct((2, R2, Cc), F32),
                          compiler_params=pltpu.CompilerParams(dimension_semantics=("parallel", "parallel"),
                                                               vmem_limit_bytes=VMEM_LIMIT))(chip_idx, cidx, S1, B)


def _fill_comm(Hs):
    def copy(ins, outs, sems):
        x, y, c, s = _place()
        return pltpu.make_async_remote_copy(ins[0].at[c], outs[0].at[c], sems[0], sems[1], device_id=(x, y, 1 - c),
                                            device_id_type=MESH)

    return _Comm([Hs], [jax.ShapeDtypeStruct(Hs.shape, Hs.dtype)], [pltpu.SemaphoreType.DMA, pltpu.SemaphoreType.DMA],
                 lambda *r: copy(*r).start(), lambda *r: copy(*r).wait(), {0: 0})


def _gather_all_comm(buf):
    R, Cc = buf.shape
    DMA = pltpu.SemaphoreType.DMA

    def copies(ins, outs, sems):
        x, y, c, s = _place()
        d = 2 * s + c
        return ([pltpu.make_async_remote_copy(ins[0], outs[0].at[d], sems[0].at[m - 1], sems[1].at[m - 1],
                                              device_id=((d ^ m) // 4, ((d ^ m) // 2) % 2, (d ^ m) % 2), device_id_type=MESH)
                 for m in range(1, 8)], pltpu.make_async_copy(ins[0], outs[0].at[d], sems[2]))

    def start(ins, outs, sems):
        remote, mine = copies(ins, outs, sems)
        for cp in remote + [mine]:
            cp.start()

    def wait(ins, outs, sems):
        remote, mine = copies(ins, outs, sems)
        for cp in remote + [mine]:
            cp.wait()

    return _Comm([buf], [jax.ShapeDtypeStruct((8, R, Cc), buf.dtype)], [DMA((7,)), DMA((7,)), DMA], start, wait)


def _pack_rows(vs):
    flat = jnp.concatenate([v.reshape(-1) for v in vs])
    n = flat.shape[0]
    rows = -(-n // (LANES * 2 * HALO)) * 2 * HALO
    return jnp.pad(flat, (0, rows * LANES - n)).reshape(rows, LANES)


def _unpack_rows(buf, shapes):
    flat = buf.reshape(-1)
    outs, o = [], 0
    for shp in shapes:
        n = 1
        for d in shp:
            n *= d
        outs.append(flat[o:o + n].reshape(shp))
        o += n
    return outs


def kernel(x, p, norm_mix_g, w_in, conv_a_w, conv_qkv_w, a_log, dt_bias, dn_norm_g, w_out, norm_ffn_g, w_up, conv_ffn_w, w_down, norm_ple_g, w_ple_gate, w_ple_proj, final_norm_g, loss_target, m_norm_mix_g, m_w_in, m_conv_a_w, m_conv_qkv_w, m_a_log, m_dt_bias, m_dn_norm_g, m_w_out, m_norm_ffn_g, m_w_up, m_conv_ffn_w, m_w_down, m_norm_ple_g, m_w_ple_gate, m_w_ple_proj, m_final_norm_g, v_norm_mix_g, v_w_in, v_conv_a_w, v_conv_qkv_w, v_a_log, v_dt_bias, v_dn_norm_g, v_w_out, v_norm_ffn_g, v_w_up, v_conv_ffn_w, v_w_down, v_norm_ple_g, v_w_ple_gate, v_w_ple_proj, v_final_norm_g):
    xs = x[0]
    ps = p[0, 0]
    tgt = loss_target[0]
    T, D = xs.shape
    H = a_log.shape[-1]
    DNW = H * HEAD_DIM
    CW = conv_a_w.shape[-1] * 4
    F = w_down.shape[1] * 4
    PD = ps.shape[-1]
    IN_MAIN = 3 * CW + 4 * DNW
    IN_COLS = IN_MAIN + 2 * H
    assert w_in.shape[-1] * 4 == IN_COLS and CW + DNW == D and 2 * H <= LANES
    cb = _tile(min(CW, DNW), 512, LANES)
    while F % cb:
        cb -= LANES
    cidx = lax.axis_index("c").astype(jnp.int32).reshape(1)
    chip = 2 * lax.axis_index("x") + lax.axis_index("y")

    def halves(w):
        sh = w[0].astype(BF16)
        return sh.reshape(2, sh.shape[0] // 2, sh.shape[1])

    def whole(land):
        return land.reshape(4, 2 * land.shape[2], land.shape[3])

    def rows(g4):
        return g4.reshape(4 * g4.shape[1], g4.shape[2])

    conv_shapes = [conv_a_w[0].shape, conv_qkv_w[0].shape, conv_ffn_w[0].shape]
    cpack = _pack_rows([conv_a_w[0], conv_qkv_w[0], conv_ffn_w[0]])
    sh_in, sh_out, sh_up, sh_down, sh_pg, sh_pp = (halves(w) for w in (w_in, w_out, w_up, w_down, w_ple_gate, w_ple_proj))
    l_in, cg = _run_comm(_merge(_ag_comm(sh_in), _ag_comm(cpack.reshape(2, cpack.shape[0] // 2, LANES))), "ag_w_in_conv")
    w_in_main, w_in_small = _join_shards(whole(l_in), IN_MAIN)
    cg = cg.reshape(4, cpack.shape[0], LANES)
    parts = [_unpack_rows(cg[t], conv_shapes) for t in range(4)]
    cw_a = jnp.concatenate([parts[t][0] for t in range(4)], axis=1)
    cw_qkv = jnp.concatenate([parts[t][1] for t in range(4)], axis=1)
    cw_ffn = jnp.concatenate([parts[t][2] for t in range(4)], axis=1)
    cw_q, cw_k, cw_v = cw_qkv[:, :DNW], cw_qkv[:, DNW:2 * DNW], cw_qkv[:, 2 * DNW:]
    cw_fg, cw_fv = cw_ffn[:, :F], cw_ffn[:, F:]
    pad_row = lambda v: jnp.pad(v, ((0, 0), (0, LANES - v.shape[1])))
    a_log_row, dt_row = pad_row(a_log), pad_row(dt_bias)
    gdn_t = jnp.tile(dn_norm_g, (1, H))
    gfin = final_norm_g.reshape(1, D)

    h1 = _rms_fwd(xs, norm_mix_g, "rms1")
    proj, (l_up,) = _mm(h1, w_in_main, mode="nn", out_dtypes=[F32], name="mm_proj", comm=_ag_comm(sh_up, q=0, nq=2))
    small = _mm(h1, w_in_small, mode="nn", out_dtypes=[F32], name="mm_small")
    ymix = _ga_fwd(proj, cw_a, CW, cb, D)
    nq = 3 * CW // cb
    nd = DNW // cb
    qn, (l_out,) = _qkv_fwd(proj, cw_q, nq, True, DNW, cb, "q_fwd", comm=_ag_comm(sh_out, q=0, nq=2))
    kn, (l_out,) = _qkv_fwd(proj, cw_k, nq + nd, True, DNW, cb, "k_fwd", comm=_ag_comm(sh_out, l_out, q=1, nq=2))
    vs = _qkv_fwd(proj, cw_v, nq + 2 * nd, False, DNW, cb, "v_fwd")
    g, beta = _gb_fwd(small, a_log_row, dt_row, H)
    o, S0, inv_c, (l_up,) = _delta_fwd(qn, kn, vs, g, beta, comm=_ag_comm(sh_up, l_up, q=1, nq=2))
    w_out_f = rows(whole(l_out))
    w_up_4 = whole(l_up)
    z_coff = (3 * CW + 3 * DNW) // DNW
    assert (3 * CW + 3 * DNW) % DNW == 0 and CW % DNW == 0
    ymix = _gnorm_fwd(o, proj, z_coff, gdn_t, DNW, ymix, CW // DNW)
    add = lambda acc, r: (r + acc,)

    def out_epi(acc, xv, gv):
        x1v = xv + acc
        return x1v, x1v * lax.rsqrt(jnp.mean(x1v * x1v, axis=1, keepdims=True) + EPS) * gv

    x1, h2 = _mm(ymix, w_out_f, mode="nn", out_dtypes=[F32, BF16], epi=out_epi, extras=[xs], rows=[norm_ffn_g], name="mm_out")
    up_g, (l_down,) = _mm(h2, w_up_4, mode="nn", b_split=(0, 2), out_dtypes=[F32], name="mm_up_g",
                          comm=_ag_comm(sh_down, q=0, nq=2))
    up_v, (l_down,) = _mm(h2, w_up_4, mode="nn", b_split=(2, 2), out_dtypes=[F32], name="mm_up_v",
                          comm=_ag_comm(sh_down, l_down, q=1, nq=2))
    w_down_f = rows(whole(l_down))
    act = _ffn_fwd(up_g, up_v, cw_fg, cw_fv, cb)
    x2, (l_pg, l_pp) = _mm(act, w_down_f, mode="nn", out_dtypes=[F32], epi=add, extras=[x1], name="mm_down",
                           comm=_merge(_ag_comm(sh_pg), _ag_comm(sh_pp)))
    w_pg_f = rows(whole(l_pg))
    w_pp_4 = whole(l_pp)
    h3 = _rms_fwd(x2, norm_ple_g, "rms3")
    pp = _mm(ps, w_pp_4, mode="nn", b_split=(0, 4), out_dtypes=[F32], name="mm_pp")

    def ple_final_epi(acc, x2v, ppv, tv, gv):
        pg = _sigmoid(acc)
        x3v = x2v + pg * ppv
        r = lax.rsqrt(jnp.mean(x3v * x3v, axis=1, keepdims=True) + EPS)
        xh = x3v * r
        e = xh * gv - tv
        dy = e * (1.0 / D)
        dxh = dy * gv
        dx = r * (dxh - xh * jnp.mean(dxh * xh, axis=1, keepdims=True))
        dg = jnp.sum(dy * xh, axis=0, keepdims=True)
        ls = jnp.sum(e * e, axis=0, keepdims=True) * (0.5 / D)
        return dx, dx * ppv * pg * (1.0 - pg), dx * pg, jnp.concatenate([dg, ls, jnp.zeros((HALO - 2, D), F32)], axis=0)

    dx3, dpg, dpp, fin = _mm(h3, w_pg_f, mode="nn", out_dtypes=[F32, BF16, BF16], parts=1, epi=ple_final_epi,
                             extras=[x2, pp, tgt], rows=[gfin], name="mm_pg_final")
    fin = jnp.sum(fin.reshape(-1, HALO, D), axis=0)
    loss = lax.psum(jnp.sum(fin[1]), ("x", "y", "c"))
    d_gfin = fin[0:1]
    def split_rows(dW):
        return dW.reshape(4, dW.shape[0] // 4, dW.shape[1])

    chip_idx = chip.astype(jnp.int32).reshape(1)
    own_sum = lambda S1, B, name: _add_own(S1, B, chip_idx, cidx, "rs_" + name + "_sum")

    dW_pp = _mm(ps, dpp, mode="tn", out_split=4, out_dtypes=[F32], name="mm_dw_pp")
    dW_pg = _mm(h3, dpg, mode="tn", out_dtypes=[F32], name="mm_dw_pg")
    P_pp, P_pg = _halves(dW_pp), _halves(split_rows(dW_pg))
    def rms_bwd_epi(acc, xv, dr, gv):
        dxv, dg = _rms_bwd_math(acc, xv, gv)
        return dr + dxv, dr + dxv, _row0(dg)

    (dx2, dx2_b, d_gple), (A_pp, A_pg) = _mm(dpg, w_pg_f, mode="nt", out_dtypes=[F32, BF16], parts=1, epi=rms_bwd_epi,
                                             extras=[x2, dx3], rows=[norm_ple_g], name="mm_dh3_rms",
                                             comm=_merge(_swap_comm(P_pp), _swap_comm(P_pg)))
    d_gple = jnp.sum(d_gple.reshape(-1, HALO, D), axis=0)
    S_pp = _add_half([P_pp], [A_pp], cidx, "rs_w_pp_add")
    S_pg = _add_half([P_pg], [A_pg], cidx, "rs_w_pg_add")
    dW_down, (B_pp, B_pg) = _mm(act, dx2_b, mode="tn", out_dtypes=[F32], name="mm_dw_down",
                                comm=_merge(_a2a_comm(S_pp), _a2a_comm(S_pg)))
    P_down = _halves(split_rows(dW_down))
    dact, (A_down, F_pp, F_pg) = _mm(dx2_b, w_down_f, mode="nt", out_dtypes=[F32], name="mm_dact", comm=_merge(
        _swap_comm(P_down), _fill_comm(own_sum(S_pp, B_pp, "w_pp")), _fill_comm(own_sum(S_pg, B_pg, "w_pg"))))
    S_down = _add_half([P_down], [A_down], cidx, "rs_w_down_add")
    dup_g, dup_v, dcw_fg, dcw_fv = _ffn_bwd(dact, up_g, up_v, cw_fg, cw_fv, cb)
    dW_up_g, (B_down,) = _mm(h2, dup_g, mode="tn", out_split=2, out_dtypes=[F32], name="mm_dw_up_g", comm=_a2a_comm(S_down))
    P_ug = _halves(dW_up_g)
    dW_up_v, (A_ug, F_down) = _mm(h2, dup_v, mode="tn", out_split=2, out_dtypes=[F32], name="mm_dw_up_v",
                                  comm=_merge(_swap_comm(P_ug), _fill_comm(own_sum(S_down, B_down, "w_down"))))
    P_uv = _halves(dW_up_v)
    dh2, (A_uv,) = _mm(dup_g, w_up_4, mode="nt", b_split=(0, 2), out_dtypes=[F32], name="mm_dh2_g", comm=_swap_comm(P_uv))
    S_up = _add_half([P_ug, P_uv], [A_ug, A_uv], cidx, "rs_w_up_add")
    dh2, (B_up,) = _mm(dup_v, w_up_4, mode="nt", b_split=(2, 2), out_dtypes=[F32], epi=add, extras=[dh2], name="mm_dh2_v",
                       comm=_a2a_comm(S_up, 0, 2))
    dx1, dx1_b, d_gffn = _rms_bwd(dh2, x1, norm_ffn_g, dx2, "rms2_bwd")
    P_out = _halves(split_rows(_mm(ymix, dx1_b, mode="tn", out_dtypes=[F32], name="mm_dw_out")))
    dymix, (A_out,) = _mm(dx1_b, w_out_f, mode="nt", out_dtypes=[F32], name="mm_dymix", comm=_swap_comm(P_out))
    S_out = _add_half([P_out], [A_out], cidx, "rs_w_out_add")
    dax, dab, dac, dcw_a = _ga_bwd(dymix, proj, cw_a, CW, cb)
    do, dz, d_gdn = _gnorm_bwd(dymix, CW // DNW, o, proj, z_coff, gdn_t, DNW)
    dqn, dkn, dvs, dgB, dbB, (B_up, B_out) = _delta_bwd(qn, kn, vs, g, beta, S0, inv_c, do,
                                                        comm=_merge(_a2a_comm(S_up, 1, 2, B_up), _a2a_comm(S_out)))
    dq_pre, dcw_q = _qkv_bwd(dqn, proj, cw_q, nq, True, DNW, cb, "q_bwd")
    dk_pre, dcw_k = _qkv_bwd(dkn, proj, cw_k, nq + nd, True, DNW, cb, "k_bwd")
    dv_pre, dcw_v = _qkv_bwd(dvs, proj, cw_v, nq + 2 * nd, False, DNW, cb, "v_bwd")
    dsmall, d_ab = _gb_bwd(dgB, dbB, small, g, beta, a_log_row, dt_row, H)
    dproj = jnp.concatenate([dax, dab, dac, dq_pre, dk_pre, dv_pre, dz], axis=1)
    dW_in_main, (F_up, F_out) = _mm(h1, dproj, mode="tn", out_dtypes=[F32], name="mm_dw_in", comm=_merge(
        _fill_comm(own_sum(S_up, B_up, "w_up")), _fill_comm(own_sum(S_out, B_out, "w_out"))))
    dW_in_small = _mm(h1, dsmall, mode="tn", out_dtypes=[F32], name="mm_dw_in_small")
    def update(Hf, w, m, v, name):
        gr = Hf.reshape(2 * Hf.shape[1], Hf.shape[2])
        if gr.shape[1] % LANES == 0:
            delta, m2, v2 = _adamw(w[0], gr, m[0], v[0], "adamw_" + name)
            return gr[None], delta[None], m2[None], v2[None]
        tr = jnp.transpose
        grt = tr(gr)
        delta, m2, v2 = _adamw(tr(w[0]), grt, tr(m[0]), tr(v[0]), "adamw_" + name)
        return tr(grt)[None], tr(delta)[None], tr(m2)[None], tr(v2)[None]

    P_in = _halves(_split_shards(dW_in_main, dW_in_small, IN_COLS // 4))
    (A_in,) = _run_comm(_swap_comm(P_in), "rs_w_in_swap")
    S_in = _add_half([P_in], [A_in], cidx, "rs_w_in_add")
    dh1, (B_in,) = _mm(dproj, w_in_main, mode="nt", out_dtypes=[F32], name="mm_dh1", comm=_a2a_comm(S_in))

    def rms1_epi(acc, dhv, xv, dr, gv):
        dxv, dg = _rms_bwd_math(acc + dhv, xv, gv)
        return dr + dxv, _row0(dg)

    dx, d_gmix = _mm(dsmall, w_in_small, mode="nt", out_dtypes=[F32], parts=1, epi=rms1_epi, extras=[dh1, xs, dx1],
                     rows=[norm_mix_g], name="mm_dh1_small_rms")
    d_gmix = jnp.sum(d_gmix.reshape(-1, HALO, D), axis=0)

    small_grads = [d_gmix[0:1], dcw_a[:cw_a.shape[0]], jnp.concatenate([dcw_q, dcw_k, dcw_v], axis=1)[:cw_qkv.shape[0]],
                   d_ab[0:1, :H], d_ab[1:2, :H], d_gdn[0:1], d_gffn[0:1],
                   jnp.concatenate([dcw_fg, dcw_fv], axis=1)[:cw_ffn.shape[0]], d_gple[0:1], d_gfin]
    small_shapes = [v.shape for v in small_grads]
    gpack = _pack_rows(small_grads)
    F_in, g8 = _run_comm(_merge(_fill_comm(own_sum(S_in, B_in, "w_in")), _gather_all_comm(gpack)), "rs_w_in_gather_small")
    big = {
        "w_in": update(F_in, w_in, m_w_in, v_w_in, "w_in"),
        "w_out": update(F_out, w_out, m_w_out, v_w_out, "w_out"),
        "w_up": update(F_up, w_up, m_w_up, v_w_up, "w_up"),
        "w_down": update(F_down, w_down, m_w_down, v_w_down, "w_down"),
        "w_ple_gate": update(F_pg, w_ple_gate, m_w_ple_gate, v_w_ple_gate, "w_pg"),
        "w_ple_proj": update(F_pp, w_ple_proj, m_w_ple_proj, v_w_ple_proj, "w_pp"),
    }

    gsum = _sum_stack(g8, "sum_small")
    (g_gmix, g_cwa, g_cwqkv, g_alog, g_dt, g_gdn, g_gffn, g_cwffn, g_gple, g_gfin) = _unpack_rows(gsum, small_shapes)

    def my_cols(v):
        Cc = v.shape[1] // 4
        return lax.dynamic_slice_in_dim(v, chip * Cc, Cc, axis=1)

    g_small = [g_gmix, my_cols(g_cwa), my_cols(g_cwqkv), g_alog, g_dt, g_gdn, g_gffn, my_cols(g_cwffn), g_gple, g_gfin]
    w_small = [norm_mix_g, conv_a_w[0], conv_qkv_w[0], a_log, dt_bias, dn_norm_g, norm_ffn_g, conv_ffn_w[0], norm_ple_g, gfin]
    m_small = [m_norm_mix_g, m_conv_a_w[0], m_conv_qkv_w[0], m_a_log, m_dt_bias, m_dn_norm_g, m_norm_ffn_g, m_conv_ffn_w[0],
               m_norm_ple_g, m_final_norm_g.reshape(1, D)]
    v_small = [v_norm_mix_g, v_conv_a_w[0], v_conv_qkv_w[0], v_a_log, v_dt_bias, v_dn_norm_g, v_norm_ffn_g, v_conv_ffn_w[0],
               v_norm_ple_g, v_final_norm_g.reshape(1, D)]
    shp = [v.shape for v in w_small]
    ds_, ms_, vs_ = _adamw(_pack_rows(w_small), _pack_rows(g_small), _pack_rows(m_small), _pack_rows(v_small), "adamw_small")
    out_shapes = [norm_mix_g.shape, conv_a_w.shape, conv_qkv_w.shape, a_log.shape, dt_bias.shape, dn_norm_g.shape,
                  norm_ffn_g.shape, conv_ffn_w.shape, norm_ple_g.shape, final_norm_g.shape]
    rs = lambda vals: [v.reshape(s) for v, s in zip(vals, out_shapes)]
    sg, sd_, sm_, sv_ = rs(g_small), rs(_unpack_rows(ds_, shp)), rs(_unpack_rows(ms_, shp)), rs(_unpack_rows(vs_, shp))
    names_small = ["norm_mix_g", "conv_a_w", "conv_qkv_w", "a_log", "dt_bias", "dn_norm_g", "norm_ffn_g", "conv_ffn_w",
                   "norm_ple_g", "final_norm_g"]
    res = {n: (sg[i], sd_[i], sm_[i], sv_[i]) for i, n in enumerate(names_small)}
    res.update(big)
    order = ["norm_mix_g", "w_in", "conv_a_w", "conv_qkv_w", "a_log", "dt_bias", "dn_norm_g", "w_out", "norm_ffn_g", "w_up",
             "conv_ffn_w", "w_down", "norm_ple_g", "w_ple_gate", "w_ple_proj", "final_norm_g"]
    return (loss, dx[None], *[res[n][0] for n in order], *[res[n][1] for n in order], *[res[n][2] for n in order],
            *[res[n][3] for n in order])
```

```python
import functools

import jax
import jax.numpy as jnp
from jax import lax
from jax.experimental import pallas as pl
from jax.experimental.pallas import tpu as pltpu

F32 = jnp.float32
BF16 = jnp.bfloat16
LANES = 128
HALO = 8
HEAD_DIM = 128
CHUNK = 64
EPS = 1e-6
VMEM_LIMIT = 56 * 1024 * 1024
MM_VMEM_BUDGET = 40 * 1024 * 1024
MM_STEP_BYTES = 1 << 20
EW_VMEM_BUDGET = 28 * 1024 * 1024
MESH = pl.DeviceIdType.MESH

ADAM_LR, ADAM_B1, ADAM_B2, ADAM_EPS, ADAM_WD, ADAM_STEP = 0.001, 0.9, 0.999, 1e-08, 0.01, 10


def _tile(n, cap, unit):
    if n <= cap:
        return n
    d = (cap // unit) * unit
    while d >= unit:
        if n % d == 0:
            return d
        d -= unit
    raise ValueError(f"no tile for {n} (cap {cap}, unit {unit})")


def _sigmoid(x):
    return 1.0 / (1.0 + jnp.exp(-x))


def _divisors(n, cap):
    ds = [d for d in range(cap // LANES * LANES, 0, -LANES) if n % d == 0]
    return [n] if (n <= cap or not ds) else ds


def _mm_tiles(M, N, K, n_unit, k_unit, a_bytes, n_blocks_mn, a_transposed, tn_full=False):
    best = None
    for tm in _divisors(M, 1536):
        for tn in ([N] if tn_full else _divisors(n_unit, 1536)):
            for tk in _divisors(k_unit, 4096):
                nk = K // tk
                vmem = 2 * tm * tk * a_bytes + 2 * tk * tn * 2 + 2 * 4 * tm * tn * n_blocks_mn + (4 * tm * tn if nk > 1 else 0)
                if vmem > MM_VMEM_BUDGET:
                    continue
                steps = (M // tm) * (N // tn) * nk
                b_reads = 1 if (nk == 1 and N == tn) else M // tm
                cost = (M * K * a_bytes * (N // tn if nk > 1 else 1) + K * N * 2 * b_reads + 4 * M * N * n_blocks_mn
                        + (8 * M * N * nk // 3 if nk > 1 else 0) + steps * MM_STEP_BYTES
                        + (2 * steps * tm * tk if a_transposed else 0))
                if best is None or cost < best[0]:
                    best = (cost, tm, tn, tk)
    return best[1:]


def _mm(a, b, *, mode, out_dtypes, name, epi=None, extras=(), comm=None, b_split=None, out_split=None, rows=(), parts=0):
    if b_split is not None:
        lo, ns = b_split
        Rb, Cb = b.shape[1], b.shape[2]
    if mode == "nn":
        (M, K), N = a.shape, (ns * Cb if b_split else b.shape[1])
    elif mode == "nt":
        (M, K), N = a.shape, (Rb if b_split else b.shape[0])
    else:
        (K, M), N = a.shape, b.shape[1]
    n_ex, n_out = len(extras), len(out_dtypes)
    n_unit = Cb if (b_split and mode == "nn") else (N // out_split if out_split else N)
    n_rows = len(rows)
    assert not (n_rows and (b_split or out_split))
    k_unit = Cb if (b_split and mode == "nt") else K
    mn_blocks = (sum(e.dtype.itemsize for e in extras) + sum(jnp.dtype(d).itemsize for d in out_dtypes)) / 4
    tm, tn, tk = _mm_tiles(M, N, K, n_unit, k_unit, a.dtype.itemsize, mn_blocks, mode == "tn", tn_full=bool(n_rows))
    nk = K // tk
    a_spec = pl.BlockSpec((tk, tm), lambda i, j, k: (k, i)) if mode == "tn" else pl.BlockSpec((tm, tk), lambda i, j, k: (i, k))
    if b_split and mode == "nn":
        nb = Cb // tn
        b_spec = pl.BlockSpec((None, tk, tn), lambda i, j, k: (lo + j // nb, k, j % nb))
    elif b_split:
        nb = Cb // tk
        b_spec = pl.BlockSpec((None, tn, tk), lambda i, j, k: (lo + k // nb, j, k % nb))
    else:
        b_spec = pl.BlockSpec((tn, tk), lambda i, j, k: (j, k)) if mode == "nt" else pl.BlockSpec((tk, tn), lambda i, j, k: (k, j))
    mn_spec = pl.BlockSpec((tm, tn), lambda i, j, k: (i, j))
    out_shapes = [jax.ShapeDtypeStruct((M, N), dt) for dt in out_dtypes] + [jax.ShapeDtypeStruct((M // tm * HALO, N), F32)] * parts
    out_specs = [mn_spec] * n_out + [pl.BlockSpec((HALO, tn), lambda i, j, k: (i, j))] * parts
    if out_split:
        assert n_ex == 0 and n_out == 1
        nbo = (N // out_split) // tn
        out_specs = [pl.BlockSpec((None, tm, tn), lambda i, j, k: (j // nbo, i, j % nbo))]
        out_shapes = [jax.ShapeDtypeStruct((out_split, M, N // out_split), out_dtypes[0])]
    dims = {"nn": (((1,), (0,)), ((), ())), "nt": (((1,), (1,)), ((), ())), "tn": (((0,), (0,)), ((), ()))}[mode]

    def body(*refs):
        a_ref, b_ref = refs[0], refs[1]
        ex_refs = refs[2:2 + n_ex + n_rows]
        out_refs = refs[2 + n_ex + n_rows:2 + n_ex + n_rows + n_out + parts]
        part = lax.dot_general(a_ref[...].astype(BF16), b_ref[...].astype(BF16), dims, preferred_element_type=F32)

        def finish(acc):
            outs = (acc,) if epi is None else epi(acc, *[r[...] for r in ex_refs])
            for r, o in zip(out_refs, outs):
                r[...] = o.astype(r.dtype)

        if nk == 1:
            finish(part)
            return
        acc_ref = refs[-1]
        k = pl.program_id(2)

        @pl.when(k == 0)
        def _():
            acc_ref[...] = part

        @pl.when(jnp.logical_and(k > 0, k < nk - 1))
        def _():
            acc_ref[...] += part

        @pl.when(k == nk - 1)
        def _():
            finish(acc_ref[...] + part)

    outs, comm_outs = _call(
        body, name=name, grid=(M // tm, N // tn, nk),
        in_specs=[a_spec, b_spec] + [mn_spec] * n_ex + [pl.BlockSpec((1, tn), lambda i, j, k: (0, j))] * n_rows,
        out_specs=out_specs,
        out_shape=out_shapes,
        scratch_shapes=[pltpu.VMEM((tm, tn), F32)] if nk > 1 else [],
        semantics=("parallel", "parallel", "arbitrary"), args=(a, b, *extras, *rows), comm=comm)
    res = outs[0] if n_out + parts == 1 else outs
    return res if comm is None else (res, comm_outs)


def _tiled(fn, *, T, C, ins, out_dtypes=(), acc_rows=(), tb=None, cb=512, name, comm=None, into=None):
    tb = _tile(T, tb or (512 if cb <= 1024 else 256), HALO)
    nI, nJ = T // tb, C // cb
    hb, nH = tb // HALO, T // HALO
    specs, args, kinds = [], [], []
    for kind, arr, cmap in ins:
        cm = cmap if cmap is not None else (lambda j: j)
        kinds.append(kind)
        if kind == "cur":
            specs.append(pl.BlockSpec((tb, cb), lambda j, i, cm=cm: (i, cm(j))))
            args.append(arr)
        elif kind == "ext":
            specs.append(pl.BlockSpec((HALO, cb), lambda j, i, cm=cm: (jnp.maximum(i * hb - 1, 0), cm(j))))
            specs.append(pl.BlockSpec((tb, cb), lambda j, i, cm=cm: (i, cm(j))))
            specs.append(pl.BlockSpec((HALO, cb), lambda j, i, cm=cm: (jnp.minimum((i + 1) * hb, nH - 1), cm(j))))
            args += [arr, arr, arr]
        elif kind == "row":
            specs.append(pl.BlockSpec((arr.shape[0], cb), lambda j, i, cm=cm: (0, cm(j))))
            args.append(arr)
        elif kind == "stack":
            specs.append(pl.BlockSpec((arr.shape[0], tb, cb), lambda j, i, cm=cm: (0, i, cm(j))))
            args.append(arr)
        else:
            raise ValueError(kind)
    n_in = len(args)
    n_out, n_acc = len(out_dtypes), len(acc_rows)

    def body(*refs):
        j, i = pl.program_id(0), pl.program_id(1)
        vals, r = [], 0
        for kind in kinds:
            if kind == "ext":
                prev = jnp.where(i == 0, 0.0, refs[r][...].astype(F32))
                cur = refs[r + 1][...].astype(F32)
                nxt = jnp.where(i == nI - 1, 0.0, refs[r + 2][...].astype(F32))
                vals.append(jnp.concatenate([prev, cur, nxt], axis=0))
                r += 3
            else:
                vals.append(refs[r][...])
                r += 1
        res = fn(j, i, *vals)
        for ref, o in zip(refs[n_in:n_in + n_out], res[:n_out]):
            ref[...] = o.astype(ref.dtype)
        for ref, o in zip(refs[n_in + n_out:], res[n_out:]):
            @pl.when(i == 0)
            def _(ref=ref, o=o):
                ref[...] = o

            @pl.when(i > 0)
            def _(ref=ref, o=o):
                ref[...] += o

    out_specs = [pl.BlockSpec((tb, cb), lambda j, i: (i, j))] * n_out
    out_shape = [jax.ShapeDtypeStruct((T, C), dt) for dt in out_dtypes]
    io_aliases = None
    if into is not None:
        buf, total, off = into
        assert n_out == 1 and comm is None
        out_specs = [pl.BlockSpec((tb, cb), lambda j, i: (i, j + off))]
        out_shape = [jax.ShapeDtypeStruct((T, total), out_dtypes[0])]
        if buf is not None:
            specs, args, io_aliases = specs + [ANY], args + [buf], {n_in: 0}
            n_in += 1
    outs, comm_outs = _call(
        body, name=name, grid=(nJ, nI), in_specs=specs,
        out_specs=out_specs + [pl.BlockSpec((rows, cb), lambda j, i: (0, j)) for rows in acc_rows],
        out_shape=out_shape + [jax.ShapeDtypeStruct((rows, C), F32) for rows in acc_rows],
        scratch_shapes=[], semantics=("parallel", "arbitrary"), args=args, comm=comm, io_aliases=io_aliases)
    return outs if comm is None else (outs, comm_outs)


def _conv_causal(xe, w):
    K = w.shape[0]
    y = xe * w[K - 1:K]
    for j in range(K - 1):
        y = y + pltpu.roll(xe, K - 1 - j, 0) * w[j:j + 1]
    return y


def _conv_anti(de, w):
    K, n = w.shape[0], de.shape[0]
    y = de * w[K - 1:K]
    for j in range(K - 1):
        y = y + pltpu.roll(de, n - (K - 1 - j), 0) * w[j:j + 1]
    return y


def _conv_dw(dce, xe, K):
    n = dce.shape[0]
    tb = n - 2 * HALO
    rows = []
    for j in range(K):
        xs = xe if j == K - 1 else pltpu.roll(xe, K - 1 - j, 0)
        rows.append(jnp.sum((dce * xs)[HALO:HALO + tb], axis=0, keepdims=True))
    rows.append(jnp.zeros((HALO - K, dce.shape[1]), F32))
    return jnp.concatenate(rows, axis=0)


def _own(xe):
    return xe[HALO:xe.shape[0] - HALO]


def _row0(v):
    return jnp.concatenate([v, jnp.zeros((HALO - 1, v.shape[1]), F32)], axis=0)


def _per_head(fn, *xs):
    n = xs[0].shape[1] // HEAD_DIM
    outs = [fn(*[x[:, g * HEAD_DIM:(g + 1) * HEAD_DIM] for x in xs]) for g in range(n)]
    return outs[0] if n == 1 else jnp.concatenate(outs, axis=1)


def _rms_fwd(x, g, name):
    T, D = x.shape

    def fn(j, i, xv, gv):
        r = lax.rsqrt(jnp.mean(xv * xv, axis=1, keepdims=True) + EPS)
        return (xv * r * gv,)

    return _tiled(fn, T=T, C=D, ins=[("cur", x, None), ("row", g, None)], out_dtypes=[BF16], cb=D, name=name)[0]


def _rms_bwd_math(dy, xv, gv):
    r = lax.rsqrt(jnp.mean(xv * xv, axis=1, keepdims=True) + EPS)
    xh = xv * r
    dxh = dy * gv
    dx = r * (dxh - xh * jnp.mean(dxh * xh, axis=1, keepdims=True))
    dg = jnp.sum(dy * xh, axis=0, keepdims=True)
    return dx, dg


def _rms_bwd(dh, x, g, dres, name, comm=None):
    T, D = x.shape

    def fn(j, i, dhv, xv, gv, dr):
        dx, dg = _rms_bwd_math(dhv, xv, gv)
        return dr + dx, dr + dx, _row0(dg)

    return _tiled(fn, T=T, C=D, ins=[("cur", dh, None), ("cur", x, None), ("row", g, None), ("cur", dres, None)],
                  out_dtypes=[F32, BF16], acc_rows=[HALO], cb=D, name=name, comm=comm)


def _ga_fwd(proj, w_a, CW, cb, total):
    T = proj.shape[0]
    n = CW // cb

    def fn(j, i, ax, ab, ac, w):
        c = _conv_causal(ac * ax, w)
        return (ab * _own(c),)

    return _tiled(fn, T=T, C=CW, ins=[("ext", proj, None), ("cur", proj, lambda j: j + n), ("ext", proj, lambda j: j + 2 * n),
                                       ("row", w_a, None)], out_dtypes=[BF16], cb=cb, name="ga_fwd", into=(None, total, 0))[0]


def _ga_bwd(dymix, proj, w_a, CW, cb):
    T = proj.shape[0]
    n = CW // cb
    K = w_a.shape[0]

    def fn(j, i, dy, ax, ab, ac, w):
        u = ac * ax
        c = _conv_causal(u, w)
        dc = dy * ab
        du = _conv_anti(dc, w)
        return _own(du * ac), _own(dy * c), _own(du * ax), _conv_dw(dc, u, K)

    return _tiled(fn, T=T, C=CW, ins=[("ext", dymix, None), ("ext", proj, None), ("ext", proj, lambda j: j + n),
                                       ("ext", proj, lambda j: j + 2 * n), ("row", w_a, None)],
                  out_dtypes=[BF16, BF16, BF16], acc_rows=[HALO], cb=cb, name="ga_bwd")


def _l2n(s):
    return s * lax.rsqrt(jnp.sum(s * s, axis=1, keepdims=True) + EPS)


def _qkv_fwd(proj, w_sec, coff, normalize, DNW, cb, name, comm=None):
    T = proj.shape[0]

    def fn(j, i, pre, w):
        c = _own(_conv_causal(pre, w))
        s = c * _sigmoid(c)
        return (_per_head(_l2n, s) if normalize else s,)

    res = _tiled(fn, T=T, C=DNW, ins=[("ext", proj, lambda j: j + coff), ("row", w_sec, None)],
                 out_dtypes=[F32], cb=cb, name=name, comm=comm)
    return res[0] if comm is None else (res[0][0], res[1])


def _qkv_bwd(dsec, proj, w_sec, coff, normalize, DNW, cb, name):
    T = proj.shape[0]
    K = w_sec.shape[0]

    def l2n_bwd(s, dn):
        r = lax.rsqrt(jnp.sum(s * s, axis=1, keepdims=True) + EPS)
        nrm = s * r
        return r * (dn - nrm * jnp.sum(dn * nrm, axis=1, keepdims=True))

    def fn(j, i, dn, pre, w):
        c = _conv_causal(pre, w)
        sg = _sigmoid(c)
        s = c * sg
        ds = _per_head(l2n_bwd, s, dn) if normalize else dn
        dc = ds * (sg * (1.0 + c * (1.0 - sg)))
        return _own(_conv_anti(dc, w)), _conv_dw(dc, pre, K)

    return _tiled(fn, T=T, C=DNW, ins=[("ext", dsec, None), ("ext", proj, lambda j: j + coff), ("row", w_sec, None)],
                  out_dtypes=[BF16], acc_rows=[HALO], cb=cb, name=name)


def _gb_fwd(small, a_log_row, dt_row, H):
    T = small.shape[0]

    def fn(j, i, sm, al, dt):
        z = sm + dt
        sp = jnp.maximum(z, 0.0) + jnp.log(1.0 + jnp.exp(-jnp.abs(z)))
        g = -jnp.exp(al) * sp
        beta = _sigmoid(pltpu.roll(sm, LANES - H, 1))
        return g, beta

    return _tiled(fn, T=T, C=LANES, ins=[("cur", small, None), ("row", a_log_row, None), ("row", dt_row, None)],
                  out_dtypes=[F32, F32], cb=LANES, name="gb_fwd")


def _gb_bwd(dgB, dbB, small, g, beta, a_log_row, dt_row, H):
    T = small.shape[0]

    def fn(j, i, dgv, dbv, sm, gv, bv, al, dt):
        lane = lax.broadcasted_iota(jnp.int32, sm.shape, 1)
        dg = jnp.zeros(sm.shape, F32)
        db = jnp.zeros(sm.shape, F32)
        for h in range(H):
            dg = jnp.where(lane == h, jnp.sum(dgv[h], axis=1, keepdims=True), dg)
            db = jnp.where(lane == h, jnp.sum(dbv[h], axis=1, keepdims=True), db)
        da = dg * (-jnp.exp(al)) * _sigmoid(sm + dt)
        dbb = db * bv * (1.0 - bv)
        dsm = jnp.where(lane < H, da, 0.0) + pltpu.roll(jnp.where(lane < H, dbb, 0.0), H, 1)
        d_alog = jnp.sum(jnp.where(lane < H, dg * gv, 0.0), axis=0, keepdims=True)
        d_dt = jnp.sum(jnp.where(lane < H, da, 0.0), axis=0, keepdims=True)
        return dsm, jnp.concatenate([d_alog, d_dt, jnp.zeros((HALO - 2, LANES), F32)], axis=0)

    return _tiled(fn, T=T, C=LANES, ins=[("stack", dgB, None), ("stack", dbB, None), ("cur", small, None), ("cur", g, None),
                                          ("cur", beta, None), ("row", a_log_row, None), ("row", dt_row, None)],
                  out_dtypes=[BF16], acc_rows=[HALO], cb=LANES, name="gb_bwd")


_DIMS = {"nn": (((1,), (0,)), ((), ())), "nt": (((1,), (1,)), ((), ())), "tn": (((0,), (0,)), ((), ()))}
_DOT_BWD = {"nn": (("nt", "gb"), ("tn", "ag")), "nt": (("nn", "gb"), ("tn", "ga")), "tn": (("nt", "bg"), ("nn", "ag"))}


def _split(a):
    hi = a.astype(BF16)
    return hi, (a - hi.astype(F32)).astype(BF16)


def _raw_dot(a, b, kind, passes):
    dg = lambda x, y: lax.dot_general(x, y, _DIMS[kind], preferred_element_type=F32)
    if passes == 1:
        return dg(a.astype(BF16), b.astype(BF16))
    ah, al = _split(a)
    bh, bl = _split(b)
    if kind == "tn":
        return dg(ah, bh) + (dg(ah, bl) + dg(al, bh))
    m = a.shape[0]
    top = dg(jnp.concatenate([ah, al], axis=0), bh)
    return top[:m] + (dg(ah, bl) + top[m:])


def _raw_dot_exact(a, b, kind, exact):
    dg = lambda x, y: lax.dot_general(x, y, _DIMS[kind], preferred_element_type=F32)
    if exact == "a":
        bh, bl = _split(b)
        return dg(a.astype(BF16), bh) + dg(a.astype(BF16), bl)
    ah, al = _split(a)
    return dg(ah, b.astype(BF16)) + dg(al, b.astype(BF16))


@functools.lru_cache(maxsize=None)
def _dotc(kind):
    @jax.custom_vjp
    def f(a, b):
        return _raw_dot_exact(a, b, kind, "a")

    def fwd(a, b):
        return _raw_dot_exact(a, b, kind, "a"), a

    def bwd(a, g):
        db = _raw_dot_exact(a, g, "tn", "a") if kind == "nn" else _raw_dot_exact(g, a, "tn", "b")
        return jnp.zeros_like(a), db

    f.defvjp(fwd, bwd)
    return f


@functools.lru_cache(maxsize=None)
def _dotf(kind, passes):
    @jax.custom_vjp
    def f(a, b):
        return _raw_dot(a, b, kind, passes)

    def fwd(a, b):
        return _raw_dot(a, b, kind, passes), (a, b)

    def bwd(res, g):
        ops = {"a": res[0], "b": res[1], "g": g}
        (ka, oa), (kb, ob) = _DOT_BWD[kind]
        return (_raw_dot(ops[oa[0]], ops[oa[1]], ka, passes), _raw_dot(ops[ob[0]], ops[ob[1]], kb, passes))

    f.defvjp(fwd, bwd)
    return f


@jax.custom_vjp
def _saved_inverse(L, inv):
    return inv


def _saved_inverse_fwd(L, inv):
    return inv, inv


def _saved_inverse_bwd(inv, g):
    d3nt, d3tn = _dotf("nt", 3), _dotf("tn", 3)
    return -d3nt(d3tn(inv, g), inv), jnp.zeros_like(inv)


_saved_inverse.defvjp(_saved_inverse_fwd, _saved_inverse_bwd)


def _chunk_fn(q, k, v, gB, bB, S, inv_saved=None):
    C = CHUNK
    d3 = _dotf("nn", 3)
    d1, d1nt, d1tn = _dotf("nn", 1), _dotf("nt", 1), _dotf("tn", 1)
    each = lambda f, *ls: tuple(f(*xs) for xs in zip(*ls))
    row = lax.broadcasted_iota(jnp.int32, (C, C), 0)
    col = lax.broadcasted_iota(jnp.int32, (C, C), 1)
    causal = row >= col
    strict = row > col
    tril = jnp.where(causal, 1.0, 0.0).astype(F32)
    eye = jnp.where(row == col, 1.0, 0.0).astype(F32)
    avg = jnp.full((C, HEAD_DIM), 1.0 / HEAD_DIM, F32)
    gc = each(lambda g: _dotc("nn")(tril, g), gB)
    R = each(lambda g: _dotc("nt")(avg, g), gc)
    decay = each(lambda g, r: jnp.where(causal, jnp.exp(jnp.where(causal, g[:, :C] - r, 0.0)), 0.0), gc, R)
    kk = each(lambda x: d1nt(x, x), k)
    L = each(lambda a, d, b: jnp.where(strict, a * d * b[:, :C], 0.0), kk, decay, bB)
    if inv_saved is None:
        inv = each(lambda l: eye - l, L)
        P = L
        for _ in range(5):
            P = each(lambda p: d3(p, p), P)
            inv = each(lambda a, p: d3(a, eye + p), inv, P)
    else:
        inv = each(_saved_inverse, L, inv_saved)
    eg = each(jnp.exp, gc)
    u = each(lambda a, x, b: d3(a, x * b), inv, v, bB)
    w = each(lambda a, x, b, e: d3(a, x * b * e), inv, k, bB, eg)
    qs = each(lambda x: x * (HEAD_DIM ** -0.5), q)
    qk = each(lambda a, x, d: d1nt(a, x) * d, qs, k, decay)
    gl = each(lambda g: g[C - 1:C, :], gc)
    kd = each(lambda x, a, g: x * jnp.exp(a - g), k, gl, gc)
    qe = each(lambda a, e: a * e, qs, eg)
    nh = len(S)
    o = ()
    for c in range(len(q) // nh):
        sl = slice(c * nh, (c + 1) * nh)
        v_new = each(lambda a, b, s: a - d1(b, s), u[sl], w[sl], S)
        o1 = each(lambda a, s: d1(a, s), qe[sl], S)
        o += each(lambda a, b, vn: a + d1(b, vn), o1, qk[sl], v_new)
        kv = each(lambda x, vn: d1tn(x, vn), kd[sl], v_new)
        S = each(lambda s, a, b: s * jnp.exp(a) + b, S, gl[sl], kv)
    return (o, S), inv


def _sel_lane(x, h):
    lane = lax.broadcasted_iota(jnp.int32, x.shape, 1)
    return jnp.broadcast_to(jnp.sum(jnp.where(lane == h, x, 0.0), axis=1, keepdims=True), x.shape)


def _tile_of(ref, c, h):
    return ref[c * CHUNK:(c + 1) * CHUNK, h * HEAD_DIM:(h + 1) * HEAD_DIM]


def _chunks_per_step(N):
    return 4 if N % 4 == 0 else (2 if N % 2 == 0 else 1)


def _delta_fwd(q, k, v, g, beta, comm=None):
    T = q.shape[0]
    H, N = q.shape[1] // HEAD_DIM, T // CHUNK
    cps = _chunks_per_step(N)
    rows = cps * CHUNK

    def body(q_ref, k_ref, v_ref, g_ref, b_ref, o_ref, s_ref, inv_ref, S):
        @pl.when(pl.program_id(0) == 0)
        def _():
            S[...] = jnp.zeros_like(S)

        gv, bv = g_ref[...], b_ref[...]
        pairs = lambda f: tuple(f(c, h) for c in range(cps) for h in range(H))
        S_in = tuple(S[h] for h in range(H))
        for h in range(H):
            s_ref[h, 0] = S_in[h]
        (o, S_new), inv = _chunk_fn(pairs(lambda c, h: _tile_of(q_ref, c, h)), pairs(lambda c, h: _tile_of(k_ref, c, h)),
                                    pairs(lambda c, h: _tile_of(v_ref, c, h)),
                                    pairs(lambda c, h: _sel_lane(gv[c * CHUNK:(c + 1) * CHUNK], h)),
                                    pairs(lambda c, h: _sel_lane(bv[c * CHUNK:(c + 1) * CHUNK], h)), S_in)
        for c in range(cps):
            for h in range(H):
                o_ref[c * CHUNK:(c + 1) * CHUNK, h * HEAD_DIM:(h + 1) * HEAD_DIM] = o[c * H + h]
                inv_ref[h, c] = inv[c * H + h]
        for h in range(H):
            S[h] = S_new[h]

    blk = pl.BlockSpec((rows, H * HEAD_DIM), lambda n: (n, 0))
    gblk = pl.BlockSpec((rows, LANES), lambda n: (n, 0))
    outs, comm_outs = _call(
        body, name="delta_fwd", grid=(N // cps,), in_specs=[blk, blk, blk, gblk, gblk],
        out_specs=[blk, pl.BlockSpec((H, 1, HEAD_DIM, HEAD_DIM), lambda n: (0, n, 0, 0)),
                   pl.BlockSpec((H, cps, CHUNK, CHUNK), lambda n: (0, n, 0, 0))],
        out_shape=[jax.ShapeDtypeStruct((T, H * HEAD_DIM), F32), jax.ShapeDtypeStruct((H, N // cps, HEAD_DIM, HEAD_DIM), F32),
                   jax.ShapeDtypeStruct((H, N, CHUNK, CHUNK), F32)],
        scratch_shapes=[pltpu.VMEM((H, HEAD_DIM, HEAD_DIM), F32)],
        semantics=("arbitrary",), args=(q, k, v, g, beta), comm=comm)
    return outs[0], outs[1], outs[2], comm_outs


def _delta_bwd(q, k, v, g, beta, S0, inv, do, comm=None):
    T = q.shape[0]
    H, N = q.shape[1] // HEAD_DIM, T // CHUNK
    cps = _chunks_per_step(N)
    rows, NS = cps * CHUNK, N // cps

    def body(q_ref, k_ref, v_ref, g_ref, b_ref, s_ref, inv_ref, do_ref, dq_ref, dk_ref, dv_ref, dg_ref, db_ref, dS):
        @pl.when(pl.program_id(0) == 0)
        def _():
            dS[...] = jnp.zeros_like(dS)

        gv, bv = g_ref[...], b_ref[...]
        pairs = lambda f: tuple(f(c, h) for c in range(cps) for h in range(H))
        heads = lambda f: tuple(f(h) for h in range(H))
        _, vjp, _ = jax.vjp(_chunk_fn, pairs(lambda c, h: _tile_of(q_ref, c, h)), pairs(lambda c, h: _tile_of(k_ref, c, h)),
                            pairs(lambda c, h: _tile_of(v_ref, c, h)),
                            pairs(lambda c, h: _sel_lane(gv[c * CHUNK:(c + 1) * CHUNK], h)),
                            pairs(lambda c, h: _sel_lane(bv[c * CHUNK:(c + 1) * CHUNK], h)),
                            heads(lambda h: s_ref[h, 0]), pairs(lambda c, h: inv_ref[h, c]), has_aux=True)
        dq, dk, dv, dgB, dbB, dS_prev, _ = vjp((pairs(lambda c, h: _tile_of(do_ref, c, h)), heads(lambda h: dS[h])))
        for c in range(cps):
            for h in range(H):
                r, sl = slice(c * CHUNK, (c + 1) * CHUNK), slice(h * HEAD_DIM, (h + 1) * HEAD_DIM)
                dq_ref[r, sl] = dq[c * H + h]
                dk_ref[r, sl] = dk[c * H + h]
                dv_ref[r, sl] = dv[c * H + h]
                dg_ref[h, r] = dgB[c * H + h]
                db_ref[h, r] = dbB[c * H + h]
        for h in range(H):
            dS[h] = dS_prev[h]

    blk = pl.BlockSpec((rows, H * HEAD_DIM), lambda n: (NS - 1 - n, 0))
    gblk = pl.BlockSpec((rows, LANES), lambda n: (NS - 1 - n, 0))
    hblk = pl.BlockSpec((H, rows, LANES), lambda n: (0, NS - 1 - n, 0))
    sd = jax.ShapeDtypeStruct
    outs, comm_outs = _call(
        body, name="delta_bwd", grid=(NS,),
        in_specs=[blk, blk, blk, gblk, gblk, pl.BlockSpec((H, 1, HEAD_DIM, HEAD_DIM), lambda n: (0, NS - 1 - n, 0, 0)),
                  pl.BlockSpec((H, cps, CHUNK, CHUNK), lambda n: (0, NS - 1 - n, 0, 0)), blk],
        out_specs=[blk, blk, blk, hblk, hblk],
        out_shape=[sd((T, H * HEAD_DIM), F32)] * 3 + [sd((H, T, LANES), F32)] * 2,
        scratch_shapes=[pltpu.VMEM((H, HEAD_DIM, HEAD_DIM), F32)],
        semantics=("arbitrary",), args=(q, k, v, g, beta, S0, inv, do), comm=comm)
    return (*outs, comm_outs)


def _gnorm_fwd(o, proj, z_coff, gdn_t, DNW, buf, coff):
    T = o.shape[0]

    def fn(j, i, ov, zv, gv):
        def one(oh, zh, gh):
            r = lax.rsqrt(jnp.mean(oh * oh, axis=1, keepdims=True) + EPS)
            return oh * r * gh * (zh * _sigmoid(zh))
        return (_per_head(one, ov, zv, jnp.broadcast_to(gv, ov.shape)),)

    return _tiled(fn, T=T, C=DNW, ins=[("cur", o, None), ("cur", proj, lambda j: j + z_coff), ("row", gdn_t, None)],
                  out_dtypes=[BF16], cb=DNW, name="gnorm_fwd", into=(buf, buf.shape[1], coff))[0]


def _gnorm_bwd(dymix, y_coff, o, proj, z_coff, gdn_t, DNW):
    T = o.shape[0]
    nh = DNW // HEAD_DIM

    def fn(j, i, dy, ov, zv, gv):
        dos, dzs, dgs = [], [], jnp.zeros((1, HEAD_DIM), F32)
        for h in range(nh):
            sl = slice(h * HEAD_DIM, (h + 1) * HEAD_DIM)
            dyh, oh, zh, gh = dy[:, sl].astype(F32), ov[:, sl], zv[:, sl], gv[:, sl]
            r = lax.rsqrt(jnp.mean(oh * oh, axis=1, keepdims=True) + EPS)
            on = oh * r
            sg = _sigmoid(zh)
            sz = zh * sg
            dzs.append(dyh * on * gh * (sg * (1.0 + zh * (1.0 - sg))))
            don = dyh * gh * sz
            dos.append(r * (don - on * jnp.mean(don * on, axis=1, keepdims=True)))
            dgs = dgs + jnp.sum(dyh * on * sz, axis=0, keepdims=True)
        cat = (lambda xs: xs[0] if nh == 1 else jnp.concatenate(xs, axis=1))
        return cat(dos), cat(dzs), _row0(dgs)

    T_ = T
    nI = T_ // _tile(T_, 256, HALO)
    tb = T_ // nI
    specs_cb = DNW

    def body_wrap():
        def body(dy_ref, o_ref, z_ref, g_ref, do_ref, dz_ref, dg_ref):
            i = pl.program_id(0)
            d_o, d_z, d_g = fn(0, i, dy_ref[...], o_ref[...], z_ref[...], g_ref[...])
            do_ref[...] = d_o
            dz_ref[...] = d_z.astype(dz_ref.dtype)

            @pl.when(i == 0)
            def _():
                dg_ref[...] = d_g

            @pl.when(i > 0)
            def _():
                dg_ref[...] += d_g

        return pl.pallas_call(
            body, name="gnorm_bwd", grid=(nI,),
            in_specs=[pl.BlockSpec((tb, specs_cb), lambda i: (i, y_coff)), pl.BlockSpec((tb, specs_cb), lambda i: (i, 0)),
                      pl.BlockSpec((tb, specs_cb), lambda i: (i, z_coff)), pl.BlockSpec((1, specs_cb), lambda i: (0, 0))],
            out_specs=[pl.BlockSpec((tb, specs_cb), lambda i: (i, 0)), pl.BlockSpec((tb, specs_cb), lambda i: (i, 0)),
                       pl.BlockSpec((HALO, HEAD_DIM), lambda i: (0, 0))],
            out_shape=[jax.ShapeDtypeStruct((T_, DNW), F32), jax.ShapeDtypeStruct((T_, DNW), BF16),
                       jax.ShapeDtypeStruct((HALO, HEAD_DIM), F32)],
            compiler_params=pltpu.CompilerParams(dimension_semantics=("arbitrary",), vmem_limit_bytes=VMEM_LIMIT),
        )(dymix, o, proj, gdn_t)

    return body_wrap()


def _ffn_fwd(up_g, up_v, w_g, w_v, cb):
    T, F = up_g.shape

    def fn(j, i, ug, uv, wg, wv):
        cg = _own(_conv_causal(ug, wg))
        cv = _own(_conv_causal(uv, wv))
        return (cg * _sigmoid(cg) * cv,)

    return _tiled(fn, T=T, C=F, ins=[("ext", up_g, None), ("ext", up_v, None), ("row", w_g, None), ("row", w_v, None)],
                  out_dtypes=[BF16], tb=1024, cb=cb, name="ffn_fwd")[0]


def _ffn_bwd(dact, up_g, up_v, w_g, w_v, cb):
    T, F = up_g.shape
    K = w_g.shape[0]

    def fn(j, i, da, ug, uv, wg, wv):
        cg = _conv_causal(ug, wg)
        cv = _conv_causal(uv, wv)
        sg = _sigmoid(cg)
        dgate = da * cv * (sg * (1.0 + cg * (1.0 - sg)))
        dval = da * (cg * sg)
        return (_own(_conv_anti(dgate, wg)), _own(_conv_anti(dval, wv)), _conv_dw(dgate, ug, K), _conv_dw(dval, uv, K))

    return _tiled(fn, T=T, C=F, ins=[("ext", dact, None), ("ext", up_g, None), ("ext", up_v, None), ("row", w_g, None),
                                      ("row", w_v, None)], out_dtypes=[BF16, BF16], acc_rows=[HALO, HALO], tb=1024, cb=cb,
                  name="ffn_bwd")


def _wide(R, Cc, n_f32, unit=HALO):
    cb = Cc if (Cc % LANES or Cc <= 4096) else _tile(Cc, 2048, LANES)
    cap = max(unit, EW_VMEM_BUDGET // (2 * 4 * n_f32 * cb) // unit * unit)
    return _tile(R, cap, unit), cb


def _adamw(w, g, m, v, name, comm=None):
    R, Cc = w.shape
    tb, cb = _wide(R, Cc, 7) if R % HALO == 0 else (R, _tile(Cc, EW_VMEM_BUDGET // (2 * 4 * 7 * R) // LANES * LANES, LANES))
    c1 = 1.0 / (1.0 - ADAM_B1 ** ADAM_STEP)
    c2 = 1.0 / (1.0 - ADAM_B2 ** ADAM_STEP)

    def fn(j, i, wv, gv, mv, vv):
        m2 = ADAM_B1 * mv + (1.0 - ADAM_B1) * gv
        v2 = ADAM_B2 * vv + (1.0 - ADAM_B2) * (gv * gv)
        delta = -ADAM_LR * ((m2 * c1) / (jnp.sqrt(v2 * c2) + ADAM_EPS) + ADAM_WD * wv)
        return delta, m2, v2

    return _tiled(fn, T=R, C=Cc, ins=[("cur", w, None), ("cur", g, None), ("cur", m, None), ("cur", v, None)],
                  out_dtypes=[F32, F32, F32], tb=tb, cb=cb, name=name, comm=comm)


def _join_shards(w4, n_main):
    S4, R, cs = w4.shape
    n_small = S4 * cs - n_main
    assert 0 < n_small <= LANES and n_small <= cs
    tb = _tile(R, 256, 2 * HALO)

    def body(w_ref, main_ref, small_ref):
        for t in range(S4 - 1):
            main_ref[:, t * cs:(t + 1) * cs] = w_ref[t]
        last = w_ref[S4 - 1]
        main_ref[:, (S4 - 1) * cs:] = last[:, :cs - n_small]
        small_ref[...] = jnp.zeros_like(small_ref)
        small_ref[:, :n_small] = last[:, cs - n_small:]

    return pl.pallas_call(
        body, name="join_w_in", grid=(R // tb,), in_specs=[pl.BlockSpec((S4, tb, cs), lambda i: (0, i, 0))],
        out_specs=[pl.BlockSpec((tb, n_main), lambda i: (i, 0)), pl.BlockSpec((tb, LANES), lambda i: (i, 0))],
        out_shape=[jax.ShapeDtypeStruct((R, n_main), w4.dtype), jax.ShapeDtypeStruct((R, LANES), w4.dtype)],
        compiler_params=pltpu.CompilerParams(dimension_semantics=("parallel",), vmem_limit_bytes=VMEM_LIMIT))(w4)


def _split_shards(main, small, cs):
    R, n_main = main.shape
    n_small = 4 * cs - n_main
    tb = _tile(R, 256, HALO)

    def body(main_ref, small_ref, out_ref):
        for t in range(3):
            out_ref[t] = main_ref[:, t * cs:(t + 1) * cs]
        out_ref[3, :, :cs - n_small] = main_ref[:, 3 * cs:]
        out_ref[3, :, cs - n_small:] = small_ref[:, :n_small]

    return pl.pallas_call(
        body, name="split_g_in", grid=(R // tb,),
        in_specs=[pl.BlockSpec((tb, n_main), lambda i: (i, 0)), pl.BlockSpec((tb, LANES), lambda i: (i, 0))],
        out_specs=pl.BlockSpec((4, tb, cs), lambda i: (0, i, 0)), out_shape=jax.ShapeDtypeStruct((4, R, cs), main.dtype),
        compiler_params=pltpu.CompilerParams(dimension_semantics=("parallel",), vmem_limit_bytes=VMEM_LIMIT))(main, small)


def _sum_stack(st, name):
    S, R, Cc = st.shape
    cb = _tile(Cc, 512, LANES) if Cc % LANES == 0 else Cc

    def fn(j, i, sv):
        t = sv[0]
        for s in range(1, S):
            t = t + sv[s]
        return (t,)

    return _tiled(fn, T=R, C=Cc, ins=[("stack", st, None)], out_dtypes=[F32], cb=cb, name=name)[0]


ANY = pl.BlockSpec(memory_space=pl.ANY)


def _place():
    x, y, c = lax.axis_index("x"), lax.axis_index("y"), lax.axis_index("c")
    return x, y, c, 2 * x + y


def _chip_dev(s, c):
    return (s // 2, s % 2, c)


class _Comm:
    def __init__(self, ins, out_shapes, sems, start, wait, aliases=None):
        self.ins, self.out_shapes, self.sems = list(ins), list(out_shapes), list(sems)
        self.start, self.wait, self.aliases = start, wait, dict(aliases or {})


def _merge(*comms):
    offs, i, o, s = [], 0, 0, 0
    for cm in comms:
        offs.append((i, o, s))
        i, o, s = i + len(cm.ins), o + len(cm.out_shapes), s + len(cm.sems)

    def part(refs, k, cm):
        i0, o0, s0 = offs[k]
        return refs[0][i0:i0 + len(cm.ins)], refs[1][o0:o0 + len(cm.out_shapes)], refs[2][s0:s0 + len(cm.sems)]

    def start(*refs):
        for k, cm in enumerate(comms):
            cm.start(*part(refs, k, cm))

    def wait(*refs):
        for k, cm in enumerate(comms):
            cm.wait(*part(refs, k, cm))

    aliases = {}
    for k, cm in enumerate(comms):
        for a, b in cm.aliases.items():
            aliases[offs[k][0] + a] = offs[k][1] + b
    return _Comm([a for cm in comms for a in cm.ins], [a for cm in comms for a in cm.out_shapes],
                 [a for cm in comms for a in cm.sems], start, wait, aliases)


def _call(body, *, name, grid, in_specs, out_specs, out_shape, scratch_shapes, semantics, args, comm=None, io_aliases=None):
    if comm is None:
        outs = pl.pallas_call(
            body, name=name, grid=grid, in_specs=in_specs, out_specs=out_specs, out_shape=out_shape,
            scratch_shapes=list(scratch_shapes), input_output_aliases=dict(io_aliases or {}),
            compiler_params=pltpu.CompilerParams(dimension_semantics=semantics, vmem_limit_bytes=VMEM_LIMIT))(*args)
        return list(outs), []
    assert not io_aliases
    n_in, n_out, n_scr = len(in_specs), len(out_specs), len(scratch_shapes)
    ci, co = len(comm.ins), len(comm.out_shapes)

    def wrapped(*refs):
        r = 0
        ins, r = refs[r:r + n_in], r + n_in
        cins, r = refs[r:r + ci], r + ci
        outs, r = refs[r:r + n_out], r + n_out
        couts, r = refs[r:r + co], r + co
        scr, r = refs[r:r + n_scr], r + n_scr
        csems = refs[r:]
        ids = [pl.program_id(a) for a in range(len(grid))]
        first, last = ids[0] == 0, ids[0] == grid[0] - 1
        for a in range(1, len(grid)):
            first = jnp.logical_and(first, ids[a] == 0)
            last = jnp.logical_and(last, ids[a] == grid[a] - 1)

        @pl.when(first)
        def _():
            comm.start(cins, couts, csems)

        body(*ins, *outs, *scr)

        @pl.when(last)
        def _():
            comm.wait(cins, couts, csems)

    outs = pl.pallas_call(
        wrapped, name=name, grid=grid, in_specs=list(in_specs) + [ANY] * ci, out_specs=list(out_specs) + [ANY] * co,
        out_shape=list(out_shape) + comm.out_shapes, scratch_shapes=list(scratch_shapes) + comm.sems,
        input_output_aliases={n_in + a: n_out + b for a, b in comm.aliases.items()},
        compiler_params=pltpu.CompilerParams(dimension_semantics=("arbitrary",) * len(grid), vmem_limit_bytes=VMEM_LIMIT),
    )(*args, *comm.ins)
    return list(outs[:n_out]), list(outs[n_out:])


def _run_comm(comm, name):
    ci, co = len(comm.ins), len(comm.out_shapes)

    def body(*refs):
        cins, couts, csems = refs[:ci], refs[ci:ci + co], refs[ci + co:]
        comm.start(cins, couts, csems)
        comm.wait(cins, couts, csems)

    outs = pl.pallas_call(body, name=name, in_specs=[ANY] * ci, out_specs=[ANY] * co, out_shape=comm.out_shapes,
                          scratch_shapes=comm.sems, input_output_aliases=comm.aliases)(*comm.ins)
    return list(outs)


def _ag_comm(shard, land=None, q=0, nq=1):
    two, R2, Cc = shard.shape
    rows = pl.ds(q * (R2 // nq), R2 // nq)
    DMA = pltpu.SemaphoreType.DMA

    def copies(ins, outs, sems, which):
        sh, out = ins[0], outs[0]
        send1, recv1, send2, recv2, send0, recv0 = sems
        x, y, c, s = _place()
        sib = (x, y, 1 - c)
        rc = pltpu.make_async_remote_copy
        if which == "first":
            return [rc(sh.at[c, rows], out.at[s, c, rows], send1.at[m - 1], recv1.at[m - 1],
                       device_id=_chip_dev(s ^ m, c), device_id_type=MESH) for m in range(1, 4)]
        if which == "own":
            return [rc(sh.at[h, rows], out.at[s, h, rows], send0.at[h], recv0.at[h], device_id=sib, device_id_type=MESH)
                    for h in range(2)]
        if which == "landed":
            return [rc(sh.at[c, rows], out.at[s ^ m, c, rows], send1.at[m - 1], recv1.at[m - 1], device_id=sib,
                       device_id_type=MESH) for m in range(1, 4)]
        half = c if which == "passed" else 1 - c
        return [rc(out.at[s ^ m, half, rows], out.at[s ^ m, half, rows], send2.at[m - 1], recv2.at[m - 1], device_id=sib,
                   device_id_type=MESH) for m in range(1, 4)]

    def start(ins, outs, sems):
        for cp in copies(ins, outs, sems, "first") + copies(ins, outs, sems, "own"):
            cp.start()

    def wait(ins, outs, sems):
        passed = copies(ins, outs, sems, "passed")
        for lan, pas in zip(copies(ins, outs, sems, "landed"), passed):
            lan.wait_recv()
            pas.start()
        for cp in copies(ins, outs, sems, "handed"):
            cp.wait_recv()
        for cp in copies(ins, outs, sems, "own"):
            cp.wait()
        for cp in copies(ins, outs, sems, "first") + passed:
            cp.wait_send()

    return _Comm([shard] + ([land] if land is not None else []), [jax.ShapeDtypeStruct((4, two, R2, Cc), shard.dtype)],
                 [DMA((3,)), DMA((3,)), DMA((3,)), DMA((3,)), DMA((2,)), DMA((2,))], start, wait,
                 {1: 0} if land is not None else None)


def _ag_relay_comm(shard):
    two, R2, Cc = shard.shape
    lo, hi = pl.ds(0, R2 // 2), pl.ds(R2 // 2, R2 // 2)
    DMA = pltpu.SemaphoreType.DMA

    def copies(ins, outs, sems, which):
        sh, out = ins[0], outs[0]
        send1, recv1, sendr, recvr, send2, recv2, send0, recv0 = sems
        x, y, c, s = _place()
        sib = (x, y, 1 - c)
        nbr = lambda m: _chip_dev(s ^ m, c)
        rc = functools.partial(pltpu.make_async_remote_copy, device_id_type=MESH)
        if which == "first":
            return [rc(sh.at[c], out.at[s, c], send1.at[m - 1], recv1.at[m - 1], device_id=nbr(m)) for m in (1, 2)]
        if which == "landed":
            return [rc(sh.at[c], out.at[s ^ m, c], send1.at[m - 1], recv1.at[m - 1], device_id=sib) for m in (1, 2)]
        if which == "relay":
            return [rc(out.at[s ^ 2, c, lo], out.at[s ^ 2, c, lo], sendr.at[0], recvr.at[0], device_id=nbr(1)),
                    rc(out.at[s ^ 1, c, hi], out.at[s ^ 1, c, hi], sendr.at[1], recvr.at[1], device_id=nbr(2))]
        if which == "relayed":
            return [rc(out.at[s ^ 3, c, lo], out.at[s ^ 3, c, lo], sendr.at[0], recvr.at[0], device_id=sib),
                    rc(out.at[s ^ 3, c, hi], out.at[s ^ 3, c, hi], sendr.at[1], recvr.at[1], device_id=sib)]
        if which == "own":
            return [rc(sh.at[h], out.at[s, h], send0.at[h], recv0.at[h], device_id=sib) for h in range(2)]
        half = c if which == "passed" else 1 - c
        return [rc(out.at[s ^ m, half], out.at[s ^ m, half], send2.at[m - 1], recv2.at[m - 1], device_id=sib)
                for m in range(1, 4)]

    def start(ins, outs, sems):
        for cp in copies(ins, outs, sems, "first") + copies(ins, outs, sems, "own"):
            cp.start()

    def wait(ins, outs, sems):
        landed, relay = copies(ins, outs, sems, "landed"), copies(ins, outs, sems, "relay")
        passed = copies(ins, outs, sems, "passed")
        landed[1].wait_recv()
        relay[0].start()
        passed[1].start()
        landed[0].wait_recv()
        relay[1].start()
        passed[0].start()
        for cp in copies(ins, outs, sems, "relayed"):
            cp.wait_recv()
        passed[2].start()
        for cp in copies(ins, outs, sems, "handed"):
            cp.wait_recv()
        for cp in copies(ins, outs, sems, "own"):
            cp.wait()
        for cp in copies(ins, outs, sems, "first") + relay + passed:
            cp.wait_send()

    return _Comm([shard], [jax.ShapeDtypeStruct((4, two, R2, Cc), shard.dtype)],
                 [DMA((2,)), DMA((2,)), DMA((2,)), DMA((2,)), DMA((3,)), DMA((3,)), DMA((2,)), DMA((2,))], start, wait)


def _a2a_comm(S1, q=0, nq=1, land=None, cnt=1):
    S4, R2, Cc = S1.shape
    rows = pl.ds(q * (R2 // nq), cnt * (R2 // nq))
    DMA = pltpu.SemaphoreType.DMA

    def copies(ins, outs, sems):
        x, y, c, s = _place()
        return [pltpu.make_async_remote_copy(ins[0].at[s ^ m, rows], outs[0].at[m - 1, rows], sems[0].at[m - 1],
                                             sems[1].at[m - 1], device_id=_chip_dev(s ^ m, c), device_id_type=MESH)
                for m in range(1, 4)]

    def start(ins, outs, sems):
        for cp in copies(ins, outs, sems):
            cp.start()

    def wait(ins, outs, sems):
        for cp in copies(ins, outs, sems):
            cp.wait()

    return _Comm([S1] + ([land] if land is not None else []), [jax.ShapeDtypeStruct((3, R2, Cc), S1.dtype)],
                 [DMA((3,)), DMA((3,))], start, wait, {1: 0} if land is not None else None)


def _halves(G):
    return G.reshape(G.shape[0], 2, G.shape[1] // 2, G.shape[2])


def _swap_comm(piece):
    n, two, R2, Cc = piece.shape
    DMA = pltpu.SemaphoreType.DMA

    def copies(ins, outs, sems):
        x, y, c, s = _place()
        return [pltpu.make_async_remote_copy(ins[0].at[t, 1 - c], outs[0].at[t], sems[0].at[t], sems[1].at[t],
                                             device_id=(x, y, 1 - c), device_id_type=MESH) for t in range(n)]

    def start(ins, outs, sems):
        for cp in copies(ins, outs, sems):
            cp.start()

    def wait(ins, outs, sems):
        for cp in copies(ins, outs, sems):
            cp.wait()

    return _Comm([piece], [jax.ShapeDtypeStruct((n, R2, Cc), piece.dtype)], [DMA((n,)), DMA((n,))], start, wait)


def _add_half(pieces, As, cidx, name):
    R2, Cc = pieces[0].shape[2:]
    S4 = sum(pc.shape[0] for pc in pieces)
    tb, cb = _wide(R2, Cc, 3, 2 * HALO)
    nI, nJ = R2 // tb, Cc // cb

    def body(c_ref, g_ref, a_ref, *rest):
        rest[-1][...] = (g_ref[0, 0] + a_ref[0]).astype(BF16)

    out, t0 = None, 0
    for k, (pc, A) in enumerate(zip(pieces, As)):
        grid_spec = pltpu.PrefetchScalarGridSpec(
            num_scalar_prefetch=1, grid=(pc.shape[0], nI, nJ),
            in_specs=[pl.BlockSpec((1, 1, tb, cb), lambda t, i, j, c_ref: (t, c_ref[0], i, j)),
                      pl.BlockSpec((1, tb, cb), lambda t, i, j, c_ref: (t, i, j))] + ([ANY] if k else []),
            out_specs=pl.BlockSpec((tb, cb), lambda t, i, j, c_ref, t0=t0: ((t0 + t) * nI + i, j)))
        out = pl.pallas_call(
            functools.partial(body), name=f"{name}{k}", grid_spec=grid_spec, out_shape=jax.ShapeDtypeStruct((S4 * R2, Cc), BF16),
            input_output_aliases={3: 0} if k else {},
            compiler_params=pltpu.CompilerParams(dimension_semantics=("parallel", "parallel", "parallel"),
                                                 vmem_limit_bytes=VMEM_LIMIT),
        )(*((cidx, pc, A) + ((out,) if k else ())))
        t0 += pc.shape[0]
    return out.reshape(S4, R2, Cc)


def _add_own(S1, B, chip_idx, cidx, name):
    S4, R2, Cc = S1.shape
    tb, cb = _wide(R2, Cc, 3, 2 * HALO)

    def body(s_idx, c_idx, s_ref, b_ref, o_ref):
        o_ref[...] = ((s_ref[0].astype(F32) + b_ref[0].astype(F32)) + b_ref[1].astype(F32)) + b_ref[2].astype(F32)

    grid_spec = pltpu.PrefetchScalarGridSpec(
        num_scalar_prefetch=2, grid=(R2 // tb, Cc // cb),
        in_specs=[pl.BlockSpec((1, tb, cb), lambda i, j, s_idx, c_idx: (s_idx[0], i, j)),
                  pl.BlockSpec((3, tb, cb), lambda i, j, s_idx, c_idx: (0, i, j))],
        out_specs=pl.BlockSpec((None, tb, cb), lambda i, j, s_idx, c_idx: (c_idx[0], i, j)))
    return pl.pallas_call(body, name=name, grid_spec=grid_spec, out_shape=jax.ShapeDtypeStruct((2, R2, Cc), F32),
                          compiler_params=pltpu.CompilerParams(dimension_semantics=("parallel", "parallel"),
                                                               vmem_limit_bytes=VMEM_LIMIT))(chip_idx, cidx, S1, B)


def _fill_comm(Hs):
    def copy(ins, outs, sems):
        x, y, c, s = _place()
        return pltpu.make_async_remote_copy(ins[0].at[c], outs[0].at[c], sems[0], sems[1], device_id=(x, y, 1 - c),
                                            device_id_type=MESH)

    return _Comm([Hs], [jax.ShapeDtypeStruct(Hs.shape, Hs.dtype)], [pltpu.SemaphoreType.DMA, pltpu.SemaphoreType.DMA],
                 lambda *r: copy(*r).start(), lambda *r: copy(*r).wait(), {0: 0})


def _gather_all_comm(buf):
    R, Cc = buf.shape
    DMA = pltpu.SemaphoreType.DMA

    def copies(ins, outs, sems):
        x, y, c, s = _place()
        d = 2 * s + c
        return ([pltpu.make_async_remote_copy(ins[0], outs[0].at[d], sems[0].at[m - 1], sems[1].at[m - 1],
                                              device_id=((d ^ m) // 4, ((d ^ m) // 2) % 2, (d ^ m) % 2), device_id_type=MESH)
                 for m in range(1, 8)], pltpu.make_async_copy(ins[0], outs[0].at[d], sems[2]))

    def start(ins, outs, sems):
        remote, mine = copies(ins, outs, sems)
        for cp in remote + [mine]:
            cp.start()

    def wait(ins, outs, sems):
        remote, mine = copies(ins, outs, sems)
        for cp in remote + [mine]:
            cp.wait()

    return _Comm([buf], [jax.ShapeDtypeStruct((8, R, Cc), buf.dtype)], [DMA((7,)), DMA((7,)), DMA], start, wait)


def _pack_rows(vs):
    flat = jnp.concatenate([v.reshape(-1) for v in vs])
    n = flat.shape[0]
    rows = -(-n // (LANES * 2 * HALO)) * 2 * HALO
    return jnp.pad(flat, (0, rows * LANES - n)).reshape(rows, LANES)


def _unpack_rows(buf, shapes):
    flat = buf.reshape(-1)
    outs, o = [], 0
    for shp in shapes:
        n = 1
        for d in shp:
            n *= d
        outs.append(flat[o:o + n].reshape(shp))
        o += n
    return outs


def kernel(x, p, norm_mix_g, w_in, conv_a_w, conv_qkv_w, a_log, dt_bias, dn_norm_g, w_out, norm_ffn_g, w_up, conv_ffn_w, w_down, norm_ple_g, w_ple_gate, w_ple_proj, final_norm_g, loss_target, m_norm_mix_g, m_w_in, m_conv_a_w, m_conv_qkv_w, m_a_log, m_dt_bias, m_dn_norm_g, m_w_out, m_norm_ffn_g, m_w_up, m_conv_ffn_w, m_w_down, m_norm_ple_g, m_w_ple_gate, m_w_ple_proj, m_final_norm_g, v_norm_mix_g, v_w_in, v_conv_a_w, v_conv_qkv_w, v_a_log, v_dt_bias, v_dn_norm_g, v_w_out, v_norm_ffn_g, v_w_up, v_conv_ffn_w, v_w_down, v_norm_ple_g, v_w_ple_gate, v_w_ple_proj, v_final_norm_g):
    xs = x[0]
    ps = p[0, 0]
    tgt = loss_target[0]
    T, D = xs.shape
    H = a_log.shape[-1]
    DNW = H * HEAD_DIM
    CW = conv_a_w.shape[-1] * 4
    F = w_down.shape[1] * 4
    PD = ps.shape[-1]
    IN_MAIN = 3 * CW + 4 * DNW
    IN_COLS = IN_MAIN + 2 * H
    assert w_in.shape[-1] * 4 == IN_COLS and CW + DNW == D and 2 * H <= LANES
    cb = _tile(min(CW, DNW), 512, LANES)
    while F % cb:
        cb -= LANES
    cidx = lax.axis_index("c").astype(jnp.int32).reshape(1)
    chip = 2 * lax.axis_index("x") + lax.axis_index("y")

    def halves(w):
        sh = w[0].astype(BF16)
        return sh.reshape(2, sh.shape[0] // 2, sh.shape[1])

    def whole(land):
        return land.reshape(4, 2 * land.shape[2], land.shape[3])

    def rows(g4):
        return g4.reshape(4 * g4.shape[1], g4.shape[2])

    conv_shapes = [conv_a_w[0].shape, conv_qkv_w[0].shape, conv_ffn_w[0].shape]
    cpack = _pack_rows([conv_a_w[0], conv_qkv_w[0], conv_ffn_w[0]])
    sh_in, sh_out, sh_up, sh_down, sh_pg, sh_pp = (halves(w) for w in (w_in, w_out, w_up, w_down, w_ple_gate, w_ple_proj))
    l_in, cg = _run_comm(_merge(_ag_relay_comm(sh_in), _ag_comm(cpack.reshape(2, cpack.shape[0] // 2, LANES))), "ag_w_in_conv")
    w_in_main, w_in_small = _join_shards(whole(l_in), IN_MAIN)
    cg = cg.reshape(4, cpack.shape[0], LANES)
    parts = [_unpack_rows(cg[t], conv_shapes) for t in range(4)]
    cw_a = jnp.concatenate([parts[t][0] for t in range(4)], axis=1)
    cw_qkv = jnp.concatenate([parts[t][1] for t in range(4)], axis=1)
    cw_ffn = jnp.concatenate([parts[t][2] for t in range(4)], axis=1)
    cw_q, cw_k, cw_v = cw_qkv[:, :DNW], cw_qkv[:, DNW:2 * DNW], cw_qkv[:, 2 * DNW:]
    cw_fg, cw_fv = cw_ffn[:, :F], cw_ffn[:, F:]
    pad_row = lambda v: jnp.pad(v, ((0, 0), (0, LANES - v.shape[1])))
    a_log_row, dt_row = pad_row(a_log), pad_row(dt_bias)
    gdn_t = jnp.tile(dn_norm_g, (1, H))
    gfin = final_norm_g.reshape(1, D)

    h1 = _rms_fwd(xs, norm_mix_g, "rms1")
    proj, (l_up,) = _mm(h1, w_in_main, mode="nn", out_dtypes=[F32], name="mm_proj", comm=_ag_comm(sh_up, q=0, nq=2))
    small = _mm(h1, w_in_small, mode="nn", out_dtypes=[F32], name="mm_small")
    ymix = _ga_fwd(proj, cw_a, CW, cb, D)
    nq = 3 * CW // cb
    nd = DNW // cb
    qn, (l_out,) = _qkv_fwd(proj, cw_q, nq, True, DNW, cb, "q_fwd", comm=_ag_comm(sh_out, q=0, nq=2))
    kn, (l_out,) = _qkv_fwd(proj, cw_k, nq + nd, True, DNW, cb, "k_fwd", comm=_ag_comm(sh_out, l_out, q=1, nq=2))
    vs = _qkv_fwd(proj, cw_v, nq + 2 * nd, False, DNW, cb, "v_fwd")
    g, beta = _gb_fwd(small, a_log_row, dt_row, H)
    o, S0, inv_c, (l_up,) = _delta_fwd(qn, kn, vs, g, beta, comm=_ag_comm(sh_up, l_up, q=1, nq=2))
    w_out_f = rows(whole(l_out))
    w_up_4 = whole(l_up)
    z_coff = (3 * CW + 3 * DNW) // DNW
    assert (3 * CW + 3 * DNW) % DNW == 0 and CW % DNW == 0
    ymix = _gnorm_fwd(o, proj, z_coff, gdn_t, DNW, ymix, CW // DNW)
    add = lambda acc, r: (r + acc,)

    def out_epi(acc, xv, gv):
        x1v = xv + acc
        return x1v, x1v * lax.rsqrt(jnp.mean(x1v * x1v, axis=1, keepdims=True) + EPS) * gv

    x1, h2 = _mm(ymix, w_out_f, mode="nn", out_dtypes=[F32, BF16], epi=out_epi, extras=[xs], rows=[norm_ffn_g], name="mm_out")
    up_g, (l_down,) = _mm(h2, w_up_4, mode="nn", b_split=(0, 2), out_dtypes=[F32], name="mm_up_g",
                          comm=_ag_comm(sh_down, q=0, nq=2))
    up_v, (l_down,) = _mm(h2, w_up_4, mode="nn", b_split=(2, 2), out_dtypes=[F32], name="mm_up_v",
                          comm=_ag_comm(sh_down, l_down, q=1, nq=2))
    w_down_f = rows(whole(l_down))
    act = _ffn_fwd(up_g, up_v, cw_fg, cw_fv, cb)
    x2, (l_pg, l_pp) = _mm(act, w_down_f, mode="nn", out_dtypes=[F32], epi=add, extras=[x1], name="mm_down",
                           comm=_merge(_ag_comm(sh_pg), _ag_comm(sh_pp)))
    w_pg_f = rows(whole(l_pg))
    w_pp_4 = whole(l_pp)
    h3 = _rms_fwd(x2, norm_ple_g, "rms3")
    pp = _mm(ps, w_pp_4, mode="nn", b_split=(0, 4), out_dtypes=[F32], name="mm_pp")

    def ple_final_epi(acc, x2v, ppv, tv, gv):
        pg = _sigmoid(acc)
        x3v = x2v + pg * ppv
        r = lax.rsqrt(jnp.mean(x3v * x3v, axis=1, keepdims=True) + EPS)
        xh = x3v * r
        e = xh * gv - tv
        dy = e * (1.0 / D)
        dxh = dy * gv
        dx = r * (dxh - xh * jnp.mean(dxh * xh, axis=1, keepdims=True))
        dg = jnp.sum(dy * xh, axis=0, keepdims=True)
        ls = jnp.sum(e * e, axis=0, keepdims=True) * (0.5 / D)
        return dx, dx * ppv * pg * (1.0 - pg), dx * pg, jnp.concatenate([dg, ls, jnp.zeros((HALO - 2, D), F32)], axis=0)

    dx3, dpg, dpp, fin = _mm(h3, w_pg_f, mode="nn", out_dtypes=[F32, BF16, BF16], parts=1, epi=ple_final_epi,
                             extras=[x2, pp, tgt], rows=[gfin], name="mm_pg_final")
    fin = jnp.sum(fin.reshape(-1, HALO, D), axis=0)
    loss = lax.psum(jnp.sum(fin[1]), ("x", "y", "c"))
    d_gfin = fin[0:1]
    def split_rows(dW):
        return dW.reshape(4, dW.shape[0] // 4, dW.shape[1])

    chip_idx = chip.astype(jnp.int32).reshape(1)
    own_sum = lambda S1, B, name: _add_own(S1, B, chip_idx, cidx, "rs_" + name + "_sum")

    dW_pp = _mm(ps, dpp, mode="tn", out_split=4, out_dtypes=[F32], name="mm_dw_pp")
    dW_pg = _mm(h3, dpg, mode="tn", out_dtypes=[F32], name="mm_dw_pg")
    P_pp, P_pg = _halves(dW_pp), _halves(split_rows(dW_pg))
    def rms_bwd_epi(acc, xv, dr, gv):
        dxv, dg = _rms_bwd_math(acc, xv, gv)
        return dr + dxv, dr + dxv, _row0(dg)

    (dx2, dx2_b, d_gple), (A_pp, A_pg) = _mm(dpg, w_pg_f, mode="nt", out_dtypes=[F32, BF16], parts=1, epi=rms_bwd_epi,
                                             extras=[x2, dx3], rows=[norm_ple_g], name="mm_dh3_rms",
                                             comm=_merge(_swap_comm(P_pp), _swap_comm(P_pg)))
    d_gple = jnp.sum(d_gple.reshape(-1, HALO, D), axis=0)
    S_pp = _add_half([P_pp], [A_pp], cidx, "rs_w_pp_add")
    S_pg = _add_half([P_pg], [A_pg], cidx, "rs_w_pg_add")
    dW_down, (B_pp, B_pg) = _mm(act, dx2_b, mode="tn", out_dtypes=[F32], name="mm_dw_down",
                                comm=_merge(_a2a_comm(S_pp), _a2a_comm(S_pg)))
    P_down = _halves(split_rows(dW_down))
    dact, (A_down, F_pp, F_pg) = _mm(dx2_b, w_down_f, mode="nt", out_dtypes=[F32], name="mm_dact", comm=_merge(
        _swap_comm(P_down), _fill_comm(own_sum(S_pp, B_pp, "w_pp")), _fill_comm(own_sum(S_pg, B_pg, "w_pg"))))
    S_down = _add_half([P_down], [A_down], cidx, "rs_w_down_add")
    dup_g, dup_v, dcw_fg, dcw_fv = _ffn_bwd(dact, up_g, up_v, cw_fg, cw_fv, cb)
    dW_up_g, (B_down,) = _mm(h2, dup_g, mode="tn", out_split=2, out_dtypes=[F32], name="mm_dw_up_g", comm=_a2a_comm(S_down))
    P_ug = _halves(dW_up_g)
    dW_up_v, (A_ug, F_down) = _mm(h2, dup_v, mode="tn", out_split=2, out_dtypes=[F32], name="mm_dw_up_v",
                                  comm=_merge(_swap_comm(P_ug), _fill_comm(own_sum(S_down, B_down, "w_down"))))
    P_uv = _halves(dW_up_v)
    dh2, (A_uv,) = _mm(dup_g, w_up_4, mode="nt", b_split=(0, 2), out_dtypes=[F32], name="mm_dh2_g", comm=_swap_comm(P_uv))
    S_up = _add_half([P_ug, P_uv], [A_ug, A_uv], cidx, "rs_w_up_add")
    dh2, (B_up,) = _mm(dup_v, w_up_4, mode="nt", b_split=(2, 2), out_dtypes=[F32], epi=add, extras=[dh2], name="mm_dh2_v",
                       comm=_a2a_comm(S_up, 0, 2))
    dx1, dx1_b, d_gffn = _rms_bwd(dh2, x1, norm_ffn_g, dx2, "rms2_bwd")
    P_out = _halves(split_rows(_mm(ymix, dx1_b, mode="tn", out_dtypes=[F32], name="mm_dw_out")))
    dymix, (A_out,) = _mm(dx1_b, w_out_f, mode="nt", out_dtypes=[F32], name="mm_dymix", comm=_swap_comm(P_out))
    S_out = _add_half([P_out], [A_out], cidx, "rs_w_out_add")
    dax, dab, dac, dcw_a = _ga_bwd(dymix, proj, cw_a, CW, cb)
    do, dz, d_gdn = _gnorm_bwd(dymix, CW // DNW, o, proj, z_coff, gdn_t, DNW)
    dqn, dkn, dvs, dgB, dbB, (B_up, B_out) = _delta_bwd(qn, kn, vs, g, beta, S0, inv_c, do,
                                                        comm=_merge(_a2a_comm(S_up, 1, 2, B_up), _a2a_comm(S_out)))
    dq_pre, dcw_q = _qkv_bwd(dqn, proj, cw_q, nq, True, DNW, cb, "q_bwd")
    dk_pre, dcw_k = _qkv_bwd(dkn, proj, cw_k, nq + nd, True, DNW, cb, "k_bwd")
    dv_pre, dcw_v = _qkv_bwd(dvs, proj, cw_v, nq + 2 * nd, False, DNW, cb, "v_bwd")
    dsmall, d_ab = _gb_bwd(dgB, dbB, small, g, beta, a_log_row, dt_row, H)
    dproj = jnp.concatenate([dax, dab, dac, dq_pre, dk_pre, dv_pre, dz], axis=1)
    dW_in_main, (F_up, F_out) = _mm(h1, dproj, mode="tn", out_dtypes=[F32], name="mm_dw_in", comm=_merge(
        _fill_comm(own_sum(S_up, B_up, "w_up")), _fill_comm(own_sum(S_out, B_out, "w_out"))))
    dW_in_small = _mm(h1, dsmall, mode="tn", out_dtypes=[F32], name="mm_dw_in_small")
    def update(Hf, w, m, v, name):
        gr = Hf.reshape(2 * Hf.shape[1], Hf.shape[2])
        if gr.shape[1] % LANES == 0:
            delta, m2, v2 = _adamw(w[0], gr, m[0], v[0], "adamw_" + name)
            return gr[None], delta[None], m2[None], v2[None]
        tr = jnp.transpose
        grt = tr(gr)
        delta, m2, v2 = _adamw(tr(w[0]), grt, tr(m[0]), tr(v[0]), "adamw_" + name)
        return tr(grt)[None], tr(delta)[None], tr(m2)[None], tr(v2)[None]

    P_in = _halves(_split_shards(dW_in_main, dW_in_small, IN_COLS // 4))
    (A_in,) = _run_comm(_swap_comm(P_in), "rs_w_in_swap")
    S_in = _add_half([P_in], [A_in], cidx, "rs_w_in_add")
    dh1, (B_in,) = _mm(dproj, w_in_main, mode="nt", out_dtypes=[F32], name="mm_dh1", comm=_a2a_comm(S_in))

    def rms1_epi(acc, dhv, xv, dr, gv):
        dxv, dg = _rms_bwd_math(acc + dhv, xv, gv)
        return dr + dxv, _row0(dg)

    dx, d_gmix = _mm(dsmall, w_in_small, mode="nt", out_dtypes=[F32], parts=1, epi=rms1_epi, extras=[dh1, xs, dx1],
                     rows=[norm_mix_g], name="mm_dh1_small_rms")
    d_gmix = jnp.sum(d_gmix.reshape(-1, HALO, D), axis=0)

    small_grads = [d_gmix[0:1], dcw_a[:cw_a.shape[0]], jnp.concatenate([dcw_q, dcw_k, dcw_v], axis=1)[:cw_qkv.shape[0]],
                   d_ab[0:1, :H], d_ab[1:2, :H], d_gdn[0:1], d_gffn[0:1],
                   jnp.concatenate([dcw_fg, dcw_fv], axis=1)[:cw_ffn.shape[0]], d_gple[0:1], d_gfin]
    small_shapes = [v.shape for v in small_grads]
    gpack = _pack_rows(small_grads)
    F_in, g8 = _run_comm(_merge(_fill_comm(own_sum(S_in, B_in, "w_in")), _gather_all_comm(gpack)), "rs_w_in_gather_small")
    big = {
        "w_in": update(F_in, w_in, m_w_in, v_w_in, "w_in"),
        "w_out": update(F_out, w_out, m_w_out, v_w_out, "w_out"),
        "w_up": update(F_up, w_up, m_w_up, v_w_up, "w_up"),
        "w_down": update(F_down, w_down, m_w_down, v_w_down, "w_down"),
        "w_ple_gate": update(F_pg, w_ple_gate, m_w_ple_gate, v_w_ple_gate, "w_pg"),
        "w_ple_proj": update(F_pp, w_ple_proj, m_w_ple_proj, v_w_ple_proj, "w_pp"),
    }

    gsum = _sum_stack(g8, "sum_small")
    (g_gmix, g_cwa, g_cwqkv, g_alog, g_dt, g_gdn, g_gffn, g_cwffn, g_gple, g_gfin) = _unpack_rows(gsum, small_shapes)

    def my_cols(v):
        Cc = v.shape[1] // 4
        return lax.dynamic_slice_in_dim(v, chip * Cc, Cc, axis=1)

    g_small = [g_gmix, my_cols(g_cwa), my_cols(g_cwqkv), g_alog, g_dt, g_gdn, g_gffn, my_cols(g_cwffn), g_gple, g_gfin]
    w_small = [norm_mix_g, conv_a_w[0], conv_qkv_w[0], a_log, dt_bias, dn_norm_g, norm_ffn_g, conv_ffn_w[0], norm_ple_g, gfin]
    m_small = [m_norm_mix_g, m_conv_a_w[0], m_conv_qkv_w[0], m_a_log, m_dt_bias, m_dn_norm_g, m_norm_ffn_g, m_conv_ffn_w[0],
               m_norm_ple_g, m_final_norm_g.reshape(1, D)]
    v_small = [v_norm_mix_g, v_conv_a_w[0], v_conv_qkv_w[0], v_a_log, v_dt_bias, v_dn_norm_g, v_norm_ffn_g, v_conv_ffn_w[0],
               v_norm_ple_g, v_final_norm_g.reshape(1, D)]
    shp = [v.shape for v in w_small]
    ds_, ms_, vs_ = _adamw(_pack_rows(w_small), _pack_rows(g_small), _pack_rows(m_small), _pack_rows(v_small), "adamw_small")
    out_shapes = [norm_mix_g.shape, conv_a_w.shape, conv_qkv_w.shape, a_log.shape, dt_bias.shape, dn_norm_g.shape,
                  norm_ffn_g.shape, conv_ffn_w.shape, norm_ple_g.shape, final_norm_g.shape]
    rs = lambda vals: [v.reshape(s) for v, s in zip(vals, out_shapes)]
    sg, sd_, sm_, sv_ = rs(g_small), rs(_unpack_rows(ds_, shp)), rs(_unpack_rows(ms_, shp)), rs(_unpack_rows(vs_, shp))
    names_small = ["norm_mix_g", "conv_a_w", "conv_qkv_w", "a_log", "dt_bias", "dn_norm_g", "norm_ffn_g", "conv_ffn_w",
                   "norm_ple_g", "final_norm_g"]
    res = {n: (sg[i], sd_[i], sm_[i], sv_[i]) for i, n in enumerate(names_small)}
    res.update(big)
    order = ["norm_mix_g", "w_in", "conv_a_w", "conv_qkv_w", "a_log", "dt_bias", "dn_norm_g", "w_out", "norm_ffn_g", "w_up",
             "conv_ffn_w", "w_down", "norm_ple_g", "w_ple_gate", "w_ple_proj", "final_norm_g"]
    return (loss, dx[None], *[res[n][0] for n in order], *[res[n][1] for n in order], *[res[n][2] for n in order],
            *[res[n][3] for n in order])
```

```python
import functools

import jax
import jax.numpy as jnp
from jax import lax
from jax.experimental import pallas as pl
from jax.experimental.pallas import tpu as pltpu

F32 = jnp.float32
BF16 = jnp.bfloat16
LANES = 128
HALO = 8
HEAD_DIM = 128
CHUNK = 64
EPS = 1e-6
VMEM_LIMIT = 56 * 1024 * 1024
MM_VMEM_BUDGET = 40 * 1024 * 1024
MM_STEP_BYTES = 1 << 20
EW_VMEM_BUDGET = 28 * 1024 * 1024
MESH = pl.DeviceIdType.MESH

ADAM_LR, ADAM_B1, ADAM_B2, ADAM_EPS, ADAM_WD, ADAM_STEP = 0.001, 0.9, 0.999, 1e-08, 0.01, 10


def _tile(n, cap, unit):
    if n <= cap:
        return n
    d = (cap // unit) * unit
    while d >= unit:
        if n % d == 0:
            return d
        d -= unit
    raise ValueError(f"no tile for {n} (cap {cap}, unit {unit})")


def _sigmoid(x):
    return 1.0 / (1.0 + jnp.exp(-x))


def _divisors(n, cap):
    ds = [d for d in range(cap // LANES * LANES, 0, -LANES) if n % d == 0]
    return [n] if (n <= cap or not ds) else ds


def _mm_tiles(M, N, K, n_unit, k_unit, a_bytes, n_blocks_mn, a_transposed, tn_full=False):
    best = None
    for tm in _divisors(M, 1536):
        for tn in ([N] if tn_full else _divisors(n_unit, 1536)):
            for tk in _divisors(k_unit, 4096):
                nk = K // tk
                vmem = 2 * tm * tk * a_bytes + 2 * tk * tn * 2 + 2 * 4 * tm * tn * n_blocks_mn + (4 * tm * tn if nk > 1 else 0)
                if vmem > MM_VMEM_BUDGET:
                    continue
                steps = (M // tm) * (N // tn) * nk
                b_reads = 1 if (nk == 1 and N == tn) else M // tm
                cost = (M * K * a_bytes * (N // tn if nk > 1 else 1) + K * N * 2 * b_reads + 4 * M * N * n_blocks_mn
                        + (8 * M * N * nk // 3 if nk > 1 else 0) + steps * MM_STEP_BYTES
                        + (2 * steps * tm * tk if a_transposed else 0))
                if best is None or cost < best[0]:
                    best = (cost, tm, tn, tk)
    return best[1:]


def _mm(a, b, *, mode, out_dtypes, name, epi=None, extras=(), comm=None, b_split=None, out_split=None, rows=(), parts=0):
    if b_split is not None:
        lo, ns = b_split
        Rb, Cb = b.shape[1], b.shape[2]
    if mode == "nn":
        (M, K), N = a.shape, (ns * Cb if b_split else b.shape[1])
    elif mode == "nt":
        (M, K), N = a.shape, (Rb if b_split else b.shape[0])
    else:
        (K, M), N = a.shape, b.shape[1]
    n_ex, n_out = len(extras), len(out_dtypes)
    n_unit = Cb if (b_split and mode == "nn") else (N // out_split if out_split else N)
    n_rows = len(rows)
    assert not (n_rows and (b_split or out_split))
    k_unit = Cb if (b_split and mode == "nt") else K
    mn_blocks = (sum(e.dtype.itemsize for e in extras) + sum(jnp.dtype(d).itemsize for d in out_dtypes)) / 4
    tm, tn, tk = _mm_tiles(M, N, K, n_unit, k_unit, a.dtype.itemsize, mn_blocks, mode == "tn", tn_full=bool(n_rows))
    nk = K // tk
    a_spec = pl.BlockSpec((tk, tm), lambda i, j, k: (k, i)) if mode == "tn" else pl.BlockSpec((tm, tk), lambda i, j, k: (i, k))
    if b_split and mode == "nn":
        nb = Cb // tn
        b_spec = pl.BlockSpec((None, tk, tn), lambda i, j, k: (lo + j // nb, k, j % nb))
    elif b_split:
        nb = Cb // tk
        b_spec = pl.BlockSpec((None, tn, tk), lambda i, j, k: (lo + k // nb, j, k % nb))
    else:
        b_spec = pl.BlockSpec((tn, tk), lambda i, j, k: (j, k)) if mode == "nt" else pl.BlockSpec((tk, tn), lambda i, j, k: (k, j))
    mn_spec = pl.BlockSpec((tm, tn), lambda i, j, k: (i, j))
    out_shapes = [jax.ShapeDtypeStruct((M, N), dt) for dt in out_dtypes] + [jax.ShapeDtypeStruct((M // tm * HALO, N), F32)] * parts
    out_specs = [mn_spec] * n_out + [pl.BlockSpec((HALO, tn), lambda i, j, k: (i, j))] * parts
    if out_split:
        assert n_ex == 0 and n_out == 1
        nbo = (N // out_split) // tn
        out_specs = [pl.BlockSpec((None, tm, tn), lambda i, j, k: (j // nbo, i, j % nbo))]
        out_shapes = [jax.ShapeDtypeStruct((out_split, M, N // out_split), out_dtypes[0])]
    dims = {"nn": (((1,), (0,)), ((), ())), "nt": (((1,), (1,)), ((), ())), "tn": (((0,), (0,)), ((), ()))}[mode]

    def body(*refs):
        a_ref, b_ref = refs[0], refs[1]
        ex_refs = refs[2:2 + n_ex + n_rows]
        out_refs = refs[2 + n_ex + n_rows:2 + n_ex + n_rows + n_out + parts]
        part = lax.dot_general(a_ref[...].astype(BF16), b_ref[...].astype(BF16), dims, preferred_element_type=F32)

        def finish(acc):
            outs = (acc,) if epi is None else epi(acc, *[r[...] for r in ex_refs])
            for r, o in zip(out_refs, outs):
                r[...] = o.astype(r.dtype)

        if nk == 1:
            finish(part)
            return
        acc_ref = refs[-1]
        k = pl.program_id(2)

        @pl.when(k == 0)
        def _():
            acc_ref[...] = part

        @pl.when(jnp.logical_and(k > 0, k < nk - 1))
        def _():
            acc_ref[...] += part

        @pl.when(k == nk - 1)
        def _():
            finish(acc_ref[...] + part)

    outs, comm_outs = _call(
        body, name=name, grid=(M // tm, N // tn, nk),
        in_specs=[a_spec, b_spec] + [mn_spec] * n_ex + [pl.BlockSpec((1, tn), lambda i, j, k: (0, j))] * n_rows,
        out_specs=out_specs,
        out_shape=out_shapes,
        scratch_shapes=[pltpu.VMEM((tm, tn), F32)] if nk > 1 else [],
        semantics=("parallel", "parallel", "arbitrary"), args=(a, b, *extras, *rows), comm=comm)
    res = outs[0] if n_out + parts == 1 else outs
    return res if comm is None else (res, comm_outs)


def _tiled(fn, *, T, C, ins, out_dtypes=(), acc_rows=(), tb=None, cb=512, name, comm=None, into=None):
    tb = _tile(T, tb or (1024 if cb <= 512 else 512 if cb <= 1024 else 256), HALO)
    nI, nJ = T // tb, C // cb
    hb, nH = tb // HALO, T // HALO
    specs, args, kinds = [], [], []
    for kind, arr, cmap in ins:
        cm = cmap if cmap is not None else (lambda j: j)
        kinds.append(kind)
        if kind == "cur":
            specs.append(pl.BlockSpec((tb, cb), lambda j, i, cm=cm: (i, cm(j))))
            args.append(arr)
        elif kind == "ext":
            specs.append(pl.BlockSpec((HALO, cb), lambda j, i, cm=cm: (jnp.maximum(i * hb - 1, 0), cm(j))))
            specs.append(pl.BlockSpec((tb, cb), lambda j, i, cm=cm: (i, cm(j))))
            specs.append(pl.BlockSpec((HALO, cb), lambda j, i, cm=cm: (jnp.minimum((i + 1) * hb, nH - 1), cm(j))))
            args += [arr, arr, arr]
        elif kind == "row":
            specs.append(pl.BlockSpec((arr.shape[0], cb), lambda j, i, cm=cm: (0, cm(j))))
            args.append(arr)
        elif kind == "stack":
            specs.append(pl.BlockSpec((arr.shape[0], tb, cb), lambda j, i, cm=cm: (0, i, cm(j))))
            args.append(arr)
        else:
            raise ValueError(kind)
    n_in = len(args)
    n_out, n_acc = len(out_dtypes), len(acc_rows)

    def body(*refs):
        j, i = pl.program_id(0), pl.program_id(1)
        vals, r = [], 0
        for kind in kinds:
            if kind == "ext":
                prev = jnp.where(i == 0, 0.0, refs[r][...].astype(F32))
                cur = refs[r + 1][...].astype(F32)
                nxt = jnp.where(i == nI - 1, 0.0, refs[r + 2][...].astype(F32))
                vals.append(jnp.concatenate([prev, cur, nxt], axis=0))
                r += 3
            else:
                vals.append(refs[r][...])
                r += 1
        res = fn(j, i, *vals)
        for ref, o in zip(refs[n_in:n_in + n_out], res[:n_out]):
            ref[...] = o.astype(ref.dtype)
        for ref, o in zip(refs[n_in + n_out:], res[n_out:]):
            @pl.when(i == 0)
            def _(ref=ref, o=o):
                ref[...] = o

            @pl.when(i > 0)
            def _(ref=ref, o=o):
                ref[...] += o

    out_specs = [pl.BlockSpec((tb, cb), lambda j, i: (i, j))] * n_out
    out_shape = [jax.ShapeDtypeStruct((T, C), dt) for dt in out_dtypes]
    io_aliases = None
    if into is not None:
        buf, total, off = into
        assert n_out == 1 and comm is None
        out_specs = [pl.BlockSpec((tb, cb), lambda j, i: (i, j + off))]
        out_shape = [jax.ShapeDtypeStruct((T, total), out_dtypes[0])]
        if buf is not None:
            specs, args, io_aliases = specs + [ANY], args + [buf], {n_in: 0}
            n_in += 1
    outs, comm_outs = _call(
        body, name=name, grid=(nJ, nI), in_specs=specs,
        out_specs=out_specs + [pl.BlockSpec((rows, cb), lambda j, i: (0, j)) for rows in acc_rows],
        out_shape=out_shape + [jax.ShapeDtypeStruct((rows, C), F32) for rows in acc_rows],
        scratch_shapes=[], semantics=("parallel", "arbitrary"), args=args, comm=comm, io_aliases=io_aliases)
    return outs if comm is None else (outs, comm_outs)


def _conv_causal(xe, w):
    K = w.shape[0]
    y = xe * w[K - 1:K]
    for j in range(K - 1):
        y = y + pltpu.roll(xe, K - 1 - j, 0) * w[j:j + 1]
    return y


def _conv_anti(de, w):
    K, n = w.shape[0], de.shape[0]
    y = de * w[K - 1:K]
    for j in range(K - 1):
        y = y + pltpu.roll(de, n - (K - 1 - j), 0) * w[j:j + 1]
    return y


def _conv_dw(dce, xe, K):
    n = dce.shape[0]
    tb = n - 2 * HALO
    rows = []
    for j in range(K):
        xs = xe if j == K - 1 else pltpu.roll(xe, K - 1 - j, 0)
        rows.append(jnp.sum((dce * xs)[HALO:HALO + tb], axis=0, keepdims=True))
    rows.append(jnp.zeros((HALO - K, dce.shape[1]), F32))
    return jnp.concatenate(rows, axis=0)


def _own(xe):
    return xe[HALO:xe.shape[0] - HALO]


def _row0(v):
    return jnp.concatenate([v, jnp.zeros((HALO - 1, v.shape[1]), F32)], axis=0)


def _per_head(fn, *xs):
    n = xs[0].shape[1] // HEAD_DIM
    outs = [fn(*[x[:, g * HEAD_DIM:(g + 1) * HEAD_DIM] for x in xs]) for g in range(n)]
    return outs[0] if n == 1 else jnp.concatenate(outs, axis=1)


def _rms_fwd(x, g, name):
    T, D = x.shape

    def fn(j, i, xv, gv):
        r = lax.rsqrt(jnp.mean(xv * xv, axis=1, keepdims=True) + EPS)
        return (xv * r * gv,)

    return _tiled(fn, T=T, C=D, ins=[("cur", x, None), ("row", g, None)], out_dtypes=[BF16], cb=D, name=name)[0]


def _rms_bwd_math(dy, xv, gv):
    r = lax.rsqrt(jnp.mean(xv * xv, axis=1, keepdims=True) + EPS)
    xh = xv * r
    dxh = dy * gv
    dx = r * (dxh - xh * jnp.mean(dxh * xh, axis=1, keepdims=True))
    dg = jnp.sum(dy * xh, axis=0, keepdims=True)
    return dx, dg


def _rms_bwd(dh, x, g, dres, name, comm=None):
    T, D = x.shape

    def fn(j, i, dhv, xv, gv, dr):
        dx, dg = _rms_bwd_math(dhv, xv, gv)
        return dr + dx, dr + dx, _row0(dg)

    return _tiled(fn, T=T, C=D, ins=[("cur", dh, None), ("cur", x, None), ("row", g, None), ("cur", dres, None)],
                  out_dtypes=[F32, BF16], acc_rows=[HALO], cb=D, name=name, comm=comm)


def _ga_fwd(proj, w_a, CW, cb, total):
    T = proj.shape[0]
    n = CW // cb

    def fn(j, i, ax, ab, ac, w):
        c = _conv_causal(ac * ax, w)
        return (ab * _own(c),)

    return _tiled(fn, T=T, C=CW, ins=[("ext", proj, None), ("cur", proj, lambda j: j + n), ("ext", proj, lambda j: j + 2 * n),
                                       ("row", w_a, None)], out_dtypes=[BF16], cb=cb, name="ga_fwd", into=(None, total, 0))[0]


def _ga_bwd(dymix, proj, w_a, CW, cb):
    T = proj.shape[0]
    n = CW // cb
    K = w_a.shape[0]

    def fn(j, i, dy, ax, ab, ac, w):
        u = ac * ax
        c = _conv_causal(u, w)
        dc = dy * ab
        du = _conv_anti(dc, w)
        return _own(du * ac), _own(dy * c), _own(du * ax), _conv_dw(dc, u, K)

    return _tiled(fn, T=T, C=CW, ins=[("ext", dymix, None), ("ext", proj, None), ("ext", proj, lambda j: j + n),
                                       ("ext", proj, lambda j: j + 2 * n), ("row", w_a, None)],
                  out_dtypes=[BF16, BF16, BF16], acc_rows=[HALO], cb=cb, name="ga_bwd")


def _l2n(s):
    return s * lax.rsqrt(jnp.sum(s * s, axis=1, keepdims=True) + EPS)


def _qkv_fwd(proj, w_sec, coff, normalize, DNW, cb, name, comm=None):
    T = proj.shape[0]

    def fn(j, i, pre, w):
        c = _own(_conv_causal(pre, w))
        s = c * _sigmoid(c)
        return (_per_head(_l2n, s) if normalize else s,)

    res = _tiled(fn, T=T, C=DNW, ins=[("ext", proj, lambda j: j + coff), ("row", w_sec, None)],
                 out_dtypes=[F32], cb=cb, name=name, comm=comm)
    return res[0] if comm is None else (res[0][0], res[1])


def _qkv_bwd(dsec, proj, w_sec, coff, normalize, DNW, cb, name):
    T = proj.shape[0]
    K = w_sec.shape[0]

    def l2n_bwd(s, dn):
        r = lax.rsqrt(jnp.sum(s * s, axis=1, keepdims=True) + EPS)
        nrm = s * r
        return r * (dn - nrm * jnp.sum(dn * nrm, axis=1, keepdims=True))

    def fn(j, i, dn, pre, w):
        c = _conv_causal(pre, w)
        sg = _sigmoid(c)
        s = c * sg
        ds = _per_head(l2n_bwd, s, dn) if normalize else dn
        dc = ds * (sg * (1.0 + c * (1.0 - sg)))
        return _own(_conv_anti(dc, w)), _conv_dw(dc, pre, K)

    return _tiled(fn, T=T, C=DNW, ins=[("ext", dsec, None), ("ext", proj, lambda j: j + coff), ("row", w_sec, None)],
                  out_dtypes=[BF16], acc_rows=[HALO], cb=cb, name=name)


def _gb_fwd(small, a_log_row, dt_row, H):
    T = small.shape[0]

    def fn(j, i, sm, al, dt):
        z = sm + dt
        sp = jnp.maximum(z, 0.0) + jnp.log(1.0 + jnp.exp(-jnp.abs(z)))
        g = -jnp.exp(al) * sp
        beta = _sigmoid(pltpu.roll(sm, LANES - H, 1))
        return g, beta

    return _tiled(fn, T=T, C=LANES, ins=[("cur", small, None), ("row", a_log_row, None), ("row", dt_row, None)],
                  out_dtypes=[F32, F32], cb=LANES, name="gb_fwd")


def _gb_bwd(dgB, dbB, small, g, beta, a_log_row, dt_row, H):
    T = small.shape[0]

    def fn(j, i, dgv, dbv, sm, gv, bv, al, dt):
        lane = lax.broadcasted_iota(jnp.int32, sm.shape, 1)
        dg = jnp.zeros(sm.shape, F32)
        db = jnp.zeros(sm.shape, F32)
        for h in range(H):
            dg = jnp.where(lane == h, jnp.sum(dgv[h], axis=1, keepdims=True), dg)
            db = jnp.where(lane == h, jnp.sum(dbv[h], axis=1, keepdims=True), db)
        da = dg * (-jnp.exp(al)) * _sigmoid(sm + dt)
        dbb = db * bv * (1.0 - bv)
        dsm = jnp.where(lane < H, da, 0.0) + pltpu.roll(jnp.where(lane < H, dbb, 0.0), H, 1)
        d_alog = jnp.sum(jnp.where(lane < H, dg * gv, 0.0), axis=0, keepdims=True)
        d_dt = jnp.sum(jnp.where(lane < H, da, 0.0), axis=0, keepdims=True)
        return dsm, jnp.concatenate([d_alog, d_dt, jnp.zeros((HALO - 2, LANES), F32)], axis=0)

    return _tiled(fn, T=T, C=LANES, ins=[("stack", dgB, None), ("stack", dbB, None), ("cur", small, None), ("cur", g, None),
                                          ("cur", beta, None), ("row", a_log_row, None), ("row", dt_row, None)],
                  out_dtypes=[BF16], acc_rows=[HALO], cb=LANES, name="gb_bwd")


_DIMS = {"nn": (((1,), (0,)), ((), ())), "nt": (((1,), (1,)), ((), ())), "tn": (((0,), (0,)), ((), ()))}
_DOT_BWD = {"nn": (("nt", "gb"), ("tn", "ag")), "nt": (("nn", "gb"), ("tn", "ga")), "tn": (("nt", "bg"), ("nn", "ag"))}


def _split(a):
    hi = a.astype(BF16)
    return hi, (a - hi.astype(F32)).astype(BF16)


def _raw_dot(a, b, kind, passes):
    dg = lambda x, y: lax.dot_general(x, y, _DIMS[kind], preferred_element_type=F32)
    if passes == 1:
        return dg(a.astype(BF16), b.astype(BF16))
    ah, al = _split(a)
    bh, bl = _split(b)
    if kind == "tn":
        return dg(ah, bh) + (dg(ah, bl) + dg(al, bh))
    m = a.shape[0]
    top = dg(jnp.concatenate([ah, al], axis=0), bh)
    return top[:m] + (dg(ah, bl) + top[m:])


def _raw_dot_exact(a, b, kind, exact):
    dg = lambda x, y: lax.dot_general(x, y, _DIMS[kind], preferred_element_type=F32)
    if exact == "a":
        bh, bl = _split(b)
        return dg(a.astype(BF16), bh) + dg(a.astype(BF16), bl)
    ah, al = _split(a)
    return dg(ah, b.astype(BF16)) + dg(al, b.astype(BF16))


@functools.lru_cache(maxsize=None)
def _dotc(kind):
    @jax.custom_vjp
    def f(a, b):
        return _raw_dot_exact(a, b, kind, "a")

    def fwd(a, b):
        return _raw_dot_exact(a, b, kind, "a"), a

    def bwd(a, g):
        db = _raw_dot_exact(a, g, "tn", "a") if kind == "nn" else _raw_dot_exact(g, a, "tn", "b")
        return jnp.zeros_like(a), db

    f.defvjp(fwd, bwd)
    return f


@functools.lru_cache(maxsize=None)
def _dotf(kind, passes):
    @jax.custom_vjp
    def f(a, b):
        return _raw_dot(a, b, kind, passes)

    def fwd(a, b):
        return _raw_dot(a, b, kind, passes), (a, b)

    def bwd(res, g):
        ops = {"a": res[0], "b": res[1], "g": g}
        (ka, oa), (kb, ob) = _DOT_BWD[kind]
        return (_raw_dot(ops[oa[0]], ops[oa[1]], ka, passes), _raw_dot(ops[ob[0]], ops[ob[1]], kb, passes))

    f.defvjp(fwd, bwd)
    return f


@jax.custom_vjp
def _saved_inverse(L, inv):
    return inv


def _saved_inverse_fwd(L, inv):
    return inv, inv


def _saved_inverse_bwd(inv, g):
    d3nt, d3tn = _dotf("nt", 3), _dotf("tn", 3)
    return -d3nt(d3tn(inv, g), inv), jnp.zeros_like(inv)


_saved_inverse.defvjp(_saved_inverse_fwd, _saved_inverse_bwd)


def _chunk_fn(q, k, v, gB, bB, S, inv_saved=None):
    C = CHUNK
    d3 = _dotf("nn", 3)
    d1, d1nt, d1tn = _dotf("nn", 1), _dotf("nt", 1), _dotf("tn", 1)
    each = lambda f, *ls: tuple(f(*xs) for xs in zip(*ls))
    row = lax.broadcasted_iota(jnp.int32, (C, C), 0)
    col = lax.broadcasted_iota(jnp.int32, (C, C), 1)
    causal = row >= col
    strict = row > col
    tril = jnp.where(causal, 1.0, 0.0).astype(F32)
    eye = jnp.where(row == col, 1.0, 0.0).astype(F32)
    avg = jnp.full((C, HEAD_DIM), 1.0 / HEAD_DIM, F32)
    gc = each(lambda g: _dotc("nn")(tril, g), gB)
    R = each(lambda g: _dotc("nt")(avg, g), gc)
    decay = each(lambda g, r: jnp.where(causal, jnp.exp(jnp.where(causal, g[:, :C] - r, 0.0)), 0.0), gc, R)
    kk = each(lambda x: d1nt(x, x), k)
    L = each(lambda a, d, b: jnp.where(strict, a * d * b[:, :C], 0.0), kk, decay, bB)
    if inv_saved is None:
        inv = each(lambda l: eye - l, L)
        P = L
        for _ in range(5):
            P = each(lambda p: d3(p, p), P)
            inv = each(lambda a, p: d3(a, eye + p), inv, P)
    else:
        inv = each(_saved_inverse, L, inv_saved)
    eg = each(jnp.exp, gc)
    u = each(lambda a, x, b: d3(a, x * b), inv, v, bB)
    w = each(lambda a, x, b, e: d3(a, x * b * e), inv, k, bB, eg)
    qs = each(lambda x: x * (HEAD_DIM ** -0.5), q)
    qk = each(lambda a, x, d: d1nt(a, x) * d, qs, k, decay)
    gl = each(lambda g: g[C - 1:C, :], gc)
    kd = each(lambda x, a, g: x * jnp.exp(a - g), k, gl, gc)
    qe = each(lambda a, e: a * e, qs, eg)
    nh = len(S)
    o = ()
    for c in range(len(q) // nh):
        sl = slice(c * nh, (c + 1) * nh)
        v_new = each(lambda a, b, s: a - d1(b, s), u[sl], w[sl], S)
        o1 = each(lambda a, s: d1(a, s), qe[sl], S)
        o += each(lambda a, b, vn: a + d1(b, vn), o1, qk[sl], v_new)
        kv = each(lambda x, vn: d1tn(x, vn), kd[sl], v_new)
        S = each(lambda s, a, b: s * jnp.exp(a) + b, S, gl[sl], kv)
    return (o, S), inv


def _sel_lane(x, h):
    lane = lax.broadcasted_iota(jnp.int32, x.shape, 1)
    return jnp.broadcast_to(jnp.sum(jnp.where(lane == h, x, 0.0), axis=1, keepdims=True), x.shape)


def _tile_of(ref, c, h):
    return ref[c * CHUNK:(c + 1) * CHUNK, h * HEAD_DIM:(h + 1) * HEAD_DIM]


def _chunks_per_step(N):
    return 4 if N % 4 == 0 else (2 if N % 2 == 0 else 1)


def _delta_fwd(q, k, v, g, beta, comm=None):
    T = q.shape[0]
    H, N = q.shape[1] // HEAD_DIM, T // CHUNK
    cps = _chunks_per_step(N)
    rows = cps * CHUNK

    def body(q_ref, k_ref, v_ref, g_ref, b_ref, o_ref, s_ref, inv_ref, S):
        @pl.when(pl.program_id(0) == 0)
        def _():
            S[...] = jnp.zeros_like(S)

        gv, bv = g_ref[...], b_ref[...]
        pairs = lambda f: tuple(f(c, h) for c in range(cps) for h in range(H))
        S_in = tuple(S[h] for h in range(H))
        for h in range(H):
            s_ref[h, 0] = S_in[h]
        (o, S_new), inv = _chunk_fn(pairs(lambda c, h: _tile_of(q_ref, c, h)), pairs(lambda c, h: _tile_of(k_ref, c, h)),
                                    pairs(lambda c, h: _tile_of(v_ref, c, h)),
                                    pairs(lambda c, h: _sel_lane(gv[c * CHUNK:(c + 1) * CHUNK], h)),
                                    pairs(lambda c, h: _sel_lane(bv[c * CHUNK:(c + 1) * CHUNK], h)), S_in)
        for c in range(cps):
            for h in range(H):
                o_ref[c * CHUNK:(c + 1) * CHUNK, h * HEAD_DIM:(h + 1) * HEAD_DIM] = o[c * H + h]
                inv_ref[h, c] = inv[c * H + h]
        for h in range(H):
            S[h] = S_new[h]

    blk = pl.BlockSpec((rows, H * HEAD_DIM), lambda n: (n, 0))
    gblk = pl.BlockSpec((rows, LANES), lambda n: (n, 0))
    outs, comm_outs = _call(
        body, name="delta_fwd", grid=(N // cps,), in_specs=[blk, blk, blk, gblk, gblk],
        out_specs=[blk, pl.BlockSpec((H, 1, HEAD_DIM, HEAD_DIM), lambda n: (0, n, 0, 0)),
                   pl.BlockSpec((H, cps, CHUNK, CHUNK), lambda n: (0, n, 0, 0))],
        out_shape=[jax.ShapeDtypeStruct((T, H * HEAD_DIM), F32), jax.ShapeDtypeStruct((H, N // cps, HEAD_DIM, HEAD_DIM), F32),
                   jax.ShapeDtypeStruct((H, N, CHUNK, CHUNK), F32)],
        scratch_shapes=[pltpu.VMEM((H, HEAD_DIM, HEAD_DIM), F32)],
        semantics=("arbitrary",), args=(q, k, v, g, beta), comm=comm)
    return outs[0], outs[1], outs[2], comm_outs


def _delta_bwd(q, k, v, g, beta, S0, inv, do, comm=None):
    T = q.shape[0]
    H, N = q.shape[1] // HEAD_DIM, T // CHUNK
    cps = _chunks_per_step(N)
    rows, NS = cps * CHUNK, N // cps

    def body(q_ref, k_ref, v_ref, g_ref, b_ref, s_ref, inv_ref, do_ref, dq_ref, dk_ref, dv_ref, dg_ref, db_ref, dS):
        @pl.when(pl.program_id(0) == 0)
        def _():
            dS[...] = jnp.zeros_like(dS)

        gv, bv = g_ref[...], b_ref[...]
        pairs = lambda f: tuple(f(c, h) for c in range(cps) for h in range(H))
        heads = lambda f: tuple(f(h) for h in range(H))
        _, vjp, _ = jax.vjp(_chunk_fn, pairs(lambda c, h: _tile_of(q_ref, c, h)), pairs(lambda c, h: _tile_of(k_ref, c, h)),
                            pairs(lambda c, h: _tile_of(v_ref, c, h)),
                            pairs(lambda c, h: _sel_lane(gv[c * CHUNK:(c + 1) * CHUNK], h)),
                            pairs(lambda c, h: _sel_lane(bv[c * CHUNK:(c + 1) * CHUNK], h)),
                            heads(lambda h: s_ref[h, 0]), pairs(lambda c, h: inv_ref[h, c]), has_aux=True)
        dq, dk, dv, dgB, dbB, dS_prev, _ = vjp((pairs(lambda c, h: _tile_of(do_ref, c, h)), heads(lambda h: dS[h])))
        for c in range(cps):
            for h in range(H):
                r, sl = slice(c * CHUNK, (c + 1) * CHUNK), slice(h * HEAD_DIM, (h + 1) * HEAD_DIM)
                dq_ref[r, sl] = dq[c * H + h]
                dk_ref[r, sl] = dk[c * H + h]
                dv_ref[r, sl] = dv[c * H + h]
                dg_ref[h, r] = dgB[c * H + h]
                db_ref[h, r] = dbB[c * H + h]
        for h in range(H):
            dS[h] = dS_prev[h]

    blk = pl.BlockSpec((rows, H * HEAD_DIM), lambda n: (NS - 1 - n, 0))
    gblk = pl.BlockSpec((rows, LANES), lambda n: (NS - 1 - n, 0))
    hblk = pl.BlockSpec((H, rows, LANES), lambda n: (0, NS - 1 - n, 0))
    sd = jax.ShapeDtypeStruct
    outs, comm_outs = _call(
        body, name="delta_bwd", grid=(NS,),
        in_specs=[blk, blk, blk, gblk, gblk, pl.BlockSpec((H, 1, HEAD_DIM, HEAD_DIM), lambda n: (0, NS - 1 - n, 0, 0)),
                  pl.BlockSpec((H, cps, CHUNK, CHUNK), lambda n: (0, NS - 1 - n, 0, 0)), blk],
        out_specs=[blk, blk, blk, hblk, hblk],
        out_shape=[sd((T, H * HEAD_DIM), F32)] * 3 + [sd((H, T, LANES), F32)] * 2,
        scratch_shapes=[pltpu.VMEM((H, HEAD_DIM, HEAD_DIM), F32)],
        semantics=("arbitrary",), args=(q, k, v, g, beta, S0, inv, do), comm=comm)
    return (*outs, comm_outs)


def _gnorm_fwd(o, proj, z_coff, gdn_t, DNW, buf, coff):
    T = o.shape[0]

    def fn(j, i, ov, zv, gv):
        def one(oh, zh, gh):
            r = lax.rsqrt(jnp.mean(oh * oh, axis=1, keepdims=True) + EPS)
            return oh * r * gh * (zh * _sigmoid(zh))
        return (_per_head(one, ov, zv, jnp.broadcast_to(gv, ov.shape)),)

    return _tiled(fn, T=T, C=DNW, ins=[("cur", o, None), ("cur", proj, lambda j: j + z_coff), ("row", gdn_t, None)],
                  out_dtypes=[BF16], cb=DNW, name="gnorm_fwd", into=(buf, buf.shape[1], coff))[0]


def _gnorm_bwd(dymix, y_coff, o, proj, z_coff, gdn_t, DNW):
    T = o.shape[0]
    nh = DNW // HEAD_DIM

    def fn(j, i, dy, ov, zv, gv):
        dos, dzs, dgs = [], [], jnp.zeros((1, HEAD_DIM), F32)
        for h in range(nh):
            sl = slice(h * HEAD_DIM, (h + 1) * HEAD_DIM)
            dyh, oh, zh, gh = dy[:, sl].astype(F32), ov[:, sl], zv[:, sl], gv[:, sl]
            r = lax.rsqrt(jnp.mean(oh * oh, axis=1, keepdims=True) + EPS)
            on = oh * r
            sg = _sigmoid(zh)
            sz = zh * sg
            dzs.append(dyh * on * gh * (sg * (1.0 + zh * (1.0 - sg))))
            don = dyh * gh * sz
            dos.append(r * (don - on * jnp.mean(don * on, axis=1, keepdims=True)))
            dgs = dgs + jnp.sum(dyh * on * sz, axis=0, keepdims=True)
        cat = (lambda xs: xs[0] if nh == 1 else jnp.concatenate(xs, axis=1))
        return cat(dos), cat(dzs), _row0(dgs)

    T_ = T
    nI = T_ // _tile(T_, 256, HALO)
    tb = T_ // nI
    specs_cb = DNW

    def body_wrap():
        def body(dy_ref, o_ref, z_ref, g_ref, do_ref, dz_ref, dg_ref):
            i = pl.program_id(0)
            d_o, d_z, d_g = fn(0, i, dy_ref[...], o_ref[...], z_ref[...], g_ref[...])
            do_ref[...] = d_o
            dz_ref[...] = d_z.astype(dz_ref.dtype)

            @pl.when(i == 0)
            def _():
                dg_ref[...] = d_g

            @pl.when(i > 0)
            def _():
                dg_ref[...] += d_g

        return pl.pallas_call(
            body, name="gnorm_bwd", grid=(nI,),
            in_specs=[pl.BlockSpec((tb, specs_cb), lambda i: (i, y_coff)), pl.BlockSpec((tb, specs_cb), lambda i: (i, 0)),
                      pl.BlockSpec((tb, specs_cb), lambda i: (i, z_coff)), pl.BlockSpec((1, specs_cb), lambda i: (0, 0))],
            out_specs=[pl.BlockSpec((tb, specs_cb), lambda i: (i, 0)), pl.BlockSpec((tb, specs_cb), lambda i: (i, 0)),
                       pl.BlockSpec((HALO, HEAD_DIM), lambda i: (0, 0))],
            out_shape=[jax.ShapeDtypeStruct((T_, DNW), F32), jax.ShapeDtypeStruct((T_, DNW), BF16),
                       jax.ShapeDtypeStruct((HALO, HEAD_DIM), F32)],
            compiler_params=pltpu.CompilerParams(dimension_semantics=("arbitrary",), vmem_limit_bytes=VMEM_LIMIT),
        )(dymix, o, proj, gdn_t)

    return body_wrap()


def _ffn_fwd(up_g, up_v, w_g, w_v, cb):
    T, F = up_g.shape

    def fn(j, i, ug, uv, wg, wv):
        cg = _own(_conv_causal(ug, wg))
        cv = _own(_conv_causal(uv, wv))
        return (cg * _sigmoid(cg) * cv,)

    return _tiled(fn, T=T, C=F, ins=[("ext", up_g, None), ("ext", up_v, None), ("row", w_g, None), ("row", w_v, None)],
                  out_dtypes=[BF16], tb=1024, cb=cb, name="ffn_fwd")[0]


def _ffn_bwd(dact, up_g, up_v, w_g, w_v, cb):
    T, F = up_g.shape
    K = w_g.shape[0]

    def fn(j, i, da, ug, uv, wg, wv):
        cg = _conv_causal(ug, wg)
        cv = _conv_causal(uv, wv)
        sg = _sigmoid(cg)
        dgate = da * cv * (sg * (1.0 + cg * (1.0 - sg)))
        dval = da * (cg * sg)
        return (_own(_conv_anti(dgate, wg)), _own(_conv_anti(dval, wv)), _conv_dw(dgate, ug, K), _conv_dw(dval, uv, K))

    return _tiled(fn, T=T, C=F, ins=[("ext", dact, None), ("ext", up_g, None), ("ext", up_v, None), ("row", w_g, None),
                                      ("row", w_v, None)], out_dtypes=[BF16, BF16], acc_rows=[HALO, HALO], tb=1024, cb=cb,
                  name="ffn_bwd")


def _wide(R, Cc, n_f32, unit=HALO):
    cb = Cc if (Cc % LANES or Cc <= 4096) else _tile(Cc, 2048, LANES)
    cap = max(unit, EW_VMEM_BUDGET // (2 * 4 * n_f32 * cb) // unit * unit)
    return _tile(R, cap, unit), cb


def _adamw(w, g, m, v, name, comm=None):
    R, Cc = w.shape
    tb, cb = _wide(R, Cc, 7) if R % HALO == 0 else (R, _tile(Cc, EW_VMEM_BUDGET // (2 * 4 * 7 * R) // LANES * LANES, LANES))
    c1 = 1.0 / (1.0 - ADAM_B1 ** ADAM_STEP)
    c2 = 1.0 / (1.0 - ADAM_B2 ** ADAM_STEP)

    def fn(j, i, wv, gv, mv, vv):
        m2 = ADAM_B1 * mv + (1.0 - ADAM_B1) * gv
        v2 = ADAM_B2 * vv + (1.0 - ADAM_B2) * (gv * gv)
        delta = -ADAM_LR * ((m2 * c1) / (jnp.sqrt(v2 * c2) + ADAM_EPS) + ADAM_WD * wv)
        return delta, m2, v2

    return _tiled(fn, T=R, C=Cc, ins=[("cur", w, None), ("cur", g, None), ("cur", m, None), ("cur", v, None)],
                  out_dtypes=[F32, F32, F32], tb=tb, cb=cb, name=name, comm=comm)


def _join_shards(w4, n_main):
    S4, R, cs = w4.shape
    n_small = S4 * cs - n_main
    assert 0 < n_small <= LANES and n_small <= cs
    tb = _tile(R, 256, 2 * HALO)

    def body(w_ref, main_ref, small_ref):
        for t in range(S4 - 1):
            main_ref[:, t * cs:(t + 1) * cs] = w_ref[t]
        last = w_ref[S4 - 1]
        main_ref[:, (S4 - 1) * cs:] = last[:, :cs - n_small]
        small_ref[...] = jnp.zeros_like(small_ref)
        small_ref[:, :n_small] = last[:, cs - n_small:]

    return pl.pallas_call(
        body, name="join_w_in", grid=(R // tb,), in_specs=[pl.BlockSpec((S4, tb, cs), lambda i: (0, i, 0))],
        out_specs=[pl.BlockSpec((tb, n_main), lambda i: (i, 0)), pl.BlockSpec((tb, LANES), lambda i: (i, 0))],
        out_shape=[jax.ShapeDtypeStruct((R, n_main), w4.dtype), jax.ShapeDtypeStruct((R, LANES), w4.dtype)],
        compiler_params=pltpu.CompilerParams(dimension_semantics=("parallel",), vmem_limit_bytes=VMEM_LIMIT))(w4)


def _split_shards(main, small, cs):
    R, n_main = main.shape
    n_small = 4 * cs - n_main
    tb = _tile(R, 256, HALO)

    def body(main_ref, small_ref, out_ref):
        for t in range(3):
            out_ref[t] = main_ref[:, t * cs:(t + 1) * cs]
        out_ref[3, :, :cs - n_small] = main_ref[:, 3 * cs:]
        out_ref[3, :, cs - n_small:] = small_ref[:, :n_small]

    return pl.pallas_call(
        body, name="split_g_in", grid=(R // tb,),
        in_specs=[pl.BlockSpec((tb, n_main), lambda i: (i, 0)), pl.BlockSpec((tb, LANES), lambda i: (i, 0))],
        out_specs=pl.BlockSpec((4, tb, cs), lambda i: (0, i, 0)), out_shape=jax.ShapeDtypeStruct((4, R, cs), main.dtype),
        compiler_params=pltpu.CompilerParams(dimension_semantics=("parallel",), vmem_limit_bytes=VMEM_LIMIT))(main, small)


def _sum_stack(st, name):
    S, R, Cc = st.shape
    cb = _tile(Cc, 512, LANES) if Cc % LANES == 0 else Cc

    def fn(j, i, sv):
        t = sv[0]
        for s in range(1, S):
            t = t + sv[s]
        return (t,)

    return _tiled(fn, T=R, C=Cc, ins=[("stack", st, None)], out_dtypes=[F32], cb=cb, name=name)[0]


ANY = pl.BlockSpec(memory_space=pl.ANY)


def _place():
    x, y, c = lax.axis_index("x"), lax.axis_index("y"), lax.axis_index("c")
    return x, y, c, 2 * x + y


def _chip_dev(s, c):
    return (s // 2, s % 2, c)


class _Comm:
    def __init__(self, ins, out_shapes, sems, start, wait, aliases=None):
        self.ins, self.out_shapes, self.sems = list(ins), list(out_shapes), list(sems)
        self.start, self.wait, self.aliases = start, wait, dict(aliases or {})


def _merge(*comms):
    offs, i, o, s = [], 0, 0, 0
    for cm in comms:
        offs.append((i, o, s))
        i, o, s = i + len(cm.ins), o + len(cm.out_shapes), s + len(cm.sems)

    def part(refs, k, cm):
        i0, o0, s0 = offs[k]
        return refs[0][i0:i0 + len(cm.ins)], refs[1][o0:o0 + len(cm.out_shapes)], refs[2][s0:s0 + len(cm.sems)]

    def start(*refs):
        for k, cm in enumerate(comms):
            cm.start(*part(refs, k, cm))

    def wait(*refs):
        for k, cm in enumerate(comms):
            cm.wait(*part(refs, k, cm))

    aliases = {}
    for k, cm in enumerate(comms):
        for a, b in cm.aliases.items():
            aliases[offs[k][0] + a] = offs[k][1] + b
    return _Comm([a for cm in comms for a in cm.ins], [a for cm in comms for a in cm.out_shapes],
                 [a for cm in comms for a in cm.sems], start, wait, aliases)


def _call(body, *, name, grid, in_specs, out_specs, out_shape, scratch_shapes, semantics, args, comm=None, io_aliases=None):
    if comm is None:
        outs = pl.pallas_call(
            body, name=name, grid=grid, in_specs=in_specs, out_specs=out_specs, out_shape=out_shape,
            scratch_shapes=list(scratch_shapes), input_output_aliases=dict(io_aliases or {}),
            compiler_params=pltpu.CompilerParams(dimension_semantics=semantics, vmem_limit_bytes=VMEM_LIMIT))(*args)
        return list(outs), []
    assert not io_aliases
    n_in, n_out, n_scr = len(in_specs), len(out_specs), len(scratch_shapes)
    ci, co = len(comm.ins), len(comm.out_shapes)

    def wrapped(*refs):
        r = 0
        ins, r = refs[r:r + n_in], r + n_in
        cins, r = refs[r:r + ci], r + ci
        outs, r = refs[r:r + n_out], r + n_out
        couts, r = refs[r:r + co], r + co
        scr, r = refs[r:r + n_scr], r + n_scr
        csems = refs[r:]
        ids = [pl.program_id(a) for a in range(len(grid))]
        first, last = ids[0] == 0, ids[0] == grid[0] - 1
        for a in range(1, len(grid)):
            first = jnp.logical_and(first, ids[a] == 0)
            last = jnp.logical_and(last, ids[a] == grid[a] - 1)

        @pl.when(first)
        def _():
            comm.start(cins, couts, csems)

        body(*ins, *outs, *scr)

        @pl.when(last)
        def _():
            comm.wait(cins, couts, csems)

    outs = pl.pallas_call(
        wrapped, name=name, grid=grid, in_specs=list(in_specs) + [ANY] * ci, out_specs=list(out_specs) + [ANY] * co,
        out_shape=list(out_shape) + comm.out_shapes, scratch_shapes=list(scratch_shapes) + comm.sems,
        input_output_aliases={n_in + a: n_out + b for a, b in comm.aliases.items()},
        compiler_params=pltpu.CompilerParams(dimension_semantics=("arbitrary",) * len(grid), vmem_limit_bytes=VMEM_LIMIT),
    )(*args, *comm.ins)
    return list(outs[:n_out]), list(outs[n_out:])


def _run_comm(comm, name):
    ci, co = len(comm.ins), len(comm.out_shapes)

    def body(*refs):
        cins, couts, csems = refs[:ci], refs[ci:ci + co], refs[ci + co:]
        comm.start(cins, couts, csems)
        comm.wait(cins, couts, csems)

    outs = pl.pallas_call(body, name=name, in_specs=[ANY] * ci, out_specs=[ANY] * co, out_shape=comm.out_shapes,
                          scratch_shapes=comm.sems, input_output_aliases=comm.aliases)(*comm.ins)
    return list(outs)


def _ag_comm(shard, land=None, q=0, nq=1):
    two, R2, Cc = shard.shape
    rows = pl.ds(q * (R2 // nq), R2 // nq)
    DMA = pltpu.SemaphoreType.DMA

    def copies(ins, outs, sems, which):
        sh, out = ins[0], outs[0]
        send1, recv1, send2, recv2, send0, recv0 = sems
        x, y, c, s = _place()
        sib = (x, y, 1 - c)
        rc = pltpu.make_async_remote_copy
        if which == "first":
            return [rc(sh.at[c, rows], out.at[s, c, rows], send1.at[m - 1], recv1.at[m - 1],
                       device_id=_chip_dev(s ^ m, c), device_id_type=MESH) for m in range(1, 4)]
        if which == "own":
            return [rc(sh.at[h, rows], out.at[s, h, rows], send0.at[h], recv0.at[h], device_id=sib, device_id_type=MESH)
                    for h in range(2)]
        if which == "landed":
            return [rc(sh.at[c, rows], out.at[s ^ m, c, rows], send1.at[m - 1], recv1.at[m - 1], device_id=sib,
                       device_id_type=MESH) for m in range(1, 4)]
        half = c if which == "passed" else 1 - c
        return [rc(out.at[s ^ m, half, rows], out.at[s ^ m, half, rows], send2.at[m - 1], recv2.at[m - 1], device_id=sib,
                   device_id_type=MESH) for m in range(1, 4)]

    def start(ins, outs, sems):
        for cp in copies(ins, outs, sems, "first") + copies(ins, outs, sems, "own"):
            cp.start()

    def wait(ins, outs, sems):
        passed = copies(ins, outs, sems, "passed")
        for lan, pas in zip(copies(ins, outs, sems, "landed"), passed):
            lan.wait_recv()
            pas.start()
        for cp in copies(ins, outs, sems, "handed"):
            cp.wait_recv()
        for cp in copies(ins, outs, sems, "own"):
            cp.wait()
        for cp in copies(ins, outs, sems, "first") + passed:
            cp.wait_send()

    return _Comm([shard] + ([land] if land is not None else []), [jax.ShapeDtypeStruct((4, two, R2, Cc), shard.dtype)],
                 [DMA((3,)), DMA((3,)), DMA((3,)), DMA((3,)), DMA((2,)), DMA((2,))], start, wait,
                 {1: 0} if land is not None else None)


def _ag_relay_comm(shard):
    two, R2, Cc = shard.shape
    lo, hi = pl.ds(0, R2 // 2), pl.ds(R2 // 2, R2 // 2)
    DMA = pltpu.SemaphoreType.DMA

    def copies(ins, outs, sems, which):
        sh, out = ins[0], outs[0]
        send1, recv1, sendr, recvr, send2, recv2, send0, recv0 = sems
        x, y, c, s = _place()
        sib = (x, y, 1 - c)
        nbr = lambda m: _chip_dev(s ^ m, c)
        rc = functools.partial(pltpu.make_async_remote_copy, device_id_type=MESH)
        if which == "first":
            return [rc(sh.at[c], out.at[s, c], send1.at[m - 1], recv1.at[m - 1], device_id=nbr(m)) for m in (1, 2)]
        if which == "landed":
            return [rc(sh.at[c], out.at[s ^ m, c], send1.at[m - 1], recv1.at[m - 1], device_id=sib) for m in (1, 2)]
        if which == "relay":
            return [rc(out.at[s ^ 2, c, lo], out.at[s ^ 2, c, lo], sendr.at[0], recvr.at[0], device_id=nbr(1)),
                    rc(out.at[s ^ 1, c, hi], out.at[s ^ 1, c, hi], sendr.at[1], recvr.at[1], device_id=nbr(2))]
        if which == "relayed":
            return [rc(out.at[s ^ 3, c, lo], out.at[s ^ 3, c, lo], sendr.at[0], recvr.at[0], device_id=sib),
                    rc(out.at[s ^ 3, c, hi], out.at[s ^ 3, c, hi], sendr.at[1], recvr.at[1], device_id=sib)]
        if which == "own":
            return [rc(sh.at[h], out.at[s, h], send0.at[h], recv0.at[h], device_id=sib) for h in range(2)]
        half = c if which == "passed" else 1 - c
        return [rc(out.at[s ^ m, half], out.at[s ^ m, half], send2.at[m - 1], recv2.at[m - 1], device_id=sib)
                for m in range(1, 4)]

    def start(ins, outs, sems):
        for cp in copies(ins, outs, sems, "first") + copies(ins, outs, sems, "own"):
            cp.start()

    def wait(ins, outs, sems):
        landed, relay = copies(ins, outs, sems, "landed"), copies(ins, outs, sems, "relay")
        passed = copies(ins, outs, sems, "passed")
        landed[1].wait_recv()
        relay[0].start()
        passed[1].start()
        landed[0].wait_recv()
        relay[1].start()
        passed[0].start()
        for cp in copies(ins, outs, sems, "relayed"):
            cp.wait_recv()
        passed[2].start()
        for cp in copies(ins, outs, sems, "handed"):
            cp.wait_recv()
        for cp in copies(ins, outs, sems, "own"):
            cp.wait()
        for cp in copies(ins, outs, sems, "first") + relay + passed:
            cp.wait_send()

    return _Comm([shard], [jax.ShapeDtypeStruct((4, two, R2, Cc), shard.dtype)],
                 [DMA((2,)), DMA((2,)), DMA((2,)), DMA((2,)), DMA((3,)), DMA((3,)), DMA((2,)), DMA((2,))], start, wait)


def _a2a_comm(S1, q=0, nq=1, land=None, cnt=1):
    S4, R2, Cc = S1.shape
    rows = pl.ds(q * (R2 // nq), cnt * (R2 // nq))
    DMA = pltpu.SemaphoreType.DMA

    def copies(ins, outs, sems):
        x, y, c, s = _place()
        return [pltpu.make_async_remote_copy(ins[0].at[s ^ m, rows], outs[0].at[m - 1, rows], sems[0].at[m - 1],
                                             sems[1].at[m - 1], device_id=_chip_dev(s ^ m, c), device_id_type=MESH)
                for m in range(1, 4)]

    def start(ins, outs, sems):
        for cp in copies(ins, outs, sems):
            cp.start()

    def wait(ins, outs, sems):
        for cp in copies(ins, outs, sems):
            cp.wait()

    return _Comm([S1] + ([land] if land is not None else []), [jax.ShapeDtypeStruct((3, R2, Cc), S1.dtype)],
                 [DMA((3,)), DMA((3,))], start, wait, {1: 0} if land is not None else None)


def _halves(G):
    return G.reshape(G.shape[0], 2, G.shape[1] // 2, G.shape[2])


def _swap_comm(piece):
    n, two, R2, Cc = piece.shape
    DMA = pltpu.SemaphoreType.DMA

    def copies(ins, outs, sems):
        x, y, c, s = _place()
        return [pltpu.make_async_remote_copy(ins[0].at[t, 1 - c], outs[0].at[t], sems[0].at[t], sems[1].at[t],
                                             device_id=(x, y, 1 - c), device_id_type=MESH) for t in range(n)]

    def start(ins, outs, sems):
        for cp in copies(ins, outs, sems):
            cp.start()

    def wait(ins, outs, sems):
        for cp in copies(ins, outs, sems):
            cp.wait()

    return _Comm([piece], [jax.ShapeDtypeStruct((n, R2, Cc), piece.dtype)], [DMA((n,)), DMA((n,))], start, wait)


def _add_half(pieces, As, cidx, name):
    R2, Cc = pieces[0].shape[2:]
    S4 = sum(pc.shape[0] for pc in pieces)
    tb, cb = _wide(R2, Cc, 3, 2 * HALO)
    nI, nJ = R2 // tb, Cc // cb

    def body(c_ref, g_ref, a_ref, *rest):
        rest[-1][...] = (g_ref[0, 0] + a_ref[0]).astype(BF16)

    out, t0 = None, 0
    for k, (pc, A) in enumerate(zip(pieces, As)):
        grid_spec = pltpu.PrefetchScalarGridSpec(
            num_scalar_prefetch=1, grid=(pc.shape[0], nI, nJ),
            in_specs=[pl.BlockSpec((1, 1, tb, cb), lambda t, i, j, c_ref: (t, c_ref[0], i, j)),
                      pl.BlockSpec((1, tb, cb), lambda t, i, j, c_ref: (t, i, j))] + ([ANY] if k else []),
            out_specs=pl.BlockSpec((tb, cb), lambda t, i, j, c_ref, t0=t0: ((t0 + t) * nI + i, j)))
        out = pl.pallas_call(
            functools.partial(body), name=f"{name}{k}", grid_spec=grid_spec, out_shape=jax.ShapeDtypeStruct((S4 * R2, Cc), BF16),
            input_output_aliases={3: 0} if k else {},
            compiler_params=pltpu.CompilerParams(dimension_semantics=("parallel", "parallel", "parallel"),
                                                 vmem_limit_bytes=VMEM_LIMIT),
        )(*((cidx, pc, A) + ((out,) if k else ())))
        t0 += pc.shape[0]
    return out.reshape(S4, R2, Cc)


def _add_own(S1, B, chip_idx, cidx, name):
    S4, R2, Cc = S1.shape
    tb, cb = _wide(R2, Cc, 3, 2 * HALO)

    def body(s_idx, c_idx, s_ref, b_ref, o_ref):
        o_ref[...] = ((s_ref[0].astype(F32) + b_ref[0].astype(F32)) + b_ref[1].astype(F32)) + b_ref[2].astype(F32)

    grid_spec = pltpu.PrefetchScalarGridSpec(
        num_scalar_prefetch=2, grid=(R2 // tb, Cc // cb),
        in_specs=[pl.BlockSpec((1, tb, cb), lambda i, j, s_idx, c_idx: (s_idx[0], i, j)),
                  pl.BlockSpec((3, tb, cb), lambda i, j, s_idx, c_idx: (0, i, j))],
        out_specs=pl.BlockSpec((None, tb, cb), lambda i, j, s_idx, c_idx: (c_idx[0], i, j)))
    return pl.pallas_call(body, name=name, grid_spec=grid_spec, out_shape=jax.ShapeDtypeStruct((2, R2, Cc), F32),
                          compiler_params=pltpu.CompilerParams(dimension_semantics=("parallel", "parallel"),
                                                               vmem_limit_bytes=VMEM_LIMIT))(chip_idx, cidx, S1, B)


def _fill_comm(Hs):
    def copy(ins, outs, sems):
        x, y, c, s = _place()
        return pltpu.make_async_remote_copy(ins[0].at[c], outs[0].at[c], sems[0], sems[1], device_id=(x, y, 1 - c),
                                            device_id_type=MESH)

    return _Comm([Hs], [jax.ShapeDtypeStruct(Hs.shape, Hs.dtype)], [pltpu.SemaphoreType.DMA, pltpu.SemaphoreType.DMA],
                 lambda *r: copy(*r).start(), lambda *r: copy(*r).wait(), {0: 0})


def _gather_all_comm(buf):
    R, Cc = buf.shape
    DMA = pltpu.SemaphoreType.DMA

    def copies(ins, outs, sems):
        x, y, c, s = _place()
        d = 2 * s + c
        return ([pltpu.make_async_remote_copy(ins[0], outs[0].at[d], sems[0].at[m - 1], sems[1].at[m - 1],
                                              device_id=((d ^ m) // 4, ((d ^ m) // 2) % 2, (d ^ m) % 2), device_id_type=MESH)
                 for m in range(1, 8)], pltpu.make_async_copy(ins[0], outs[0].at[d], sems[2]))

    def start(ins, outs, sems):
        remote, mine = copies(ins, outs, sems)
        for cp in remote + [mine]:
            cp.start()

    def wait(ins, outs, sems):
        remote, mine = copies(ins, outs, sems)
        for cp in remote + [mine]:
            cp.wait()

    return _Comm([buf], [jax.ShapeDtypeStruct((8, R, Cc), buf.dtype)], [DMA((7,)), DMA((7,)), DMA], start, wait)


def _pack_rows(vs):
    flat = jnp.concatenate([v.reshape(-1) for v in vs])
    n = flat.shape[0]
    rows = -(-n // (LANES * 2 * HALO)) * 2 * HALO
    return jnp.pad(flat, (0, rows * LANES - n)).reshape(rows, LANES)


def _unpack_rows(buf, shapes):
    flat = buf.reshape(-1)
    outs, o = [], 0
    for shp in shapes:
        n = 1
        for d in shp:
            n *= d
        outs.append(flat[o:o + n].reshape(shp))
        o += n
    return outs


def kernel(x, p, norm_mix_g, w_in, conv_a_w, conv_qkv_w, a_log, dt_bias, dn_norm_g, w_out, norm_ffn_g, w_up, conv_ffn_w, w_down, norm_ple_g, w_ple_gate, w_ple_proj, final_norm_g, loss_target, m_norm_mix_g, m_w_in, m_conv_a_w, m_conv_qkv_w, m_a_log, m_dt_bias, m_dn_norm_g, m_w_out, m_norm_ffn_g, m_w_up, m_conv_ffn_w, m_w_down, m_norm_ple_g, m_w_ple_gate, m_w_ple_proj, m_final_norm_g, v_norm_mix_g, v_w_in, v_conv_a_w, v_conv_qkv_w, v_a_log, v_dt_bias, v_dn_norm_g, v_w_out, v_norm_ffn_g, v_w_up, v_conv_ffn_w, v_w_down, v_norm_ple_g, v_w_ple_gate, v_w_ple_proj, v_final_norm_g):
    xs = x[0]
    ps = p[0, 0]
    tgt = loss_target[0]
    T, D = xs.shape
    H = a_log.shape[-1]
    DNW = H * HEAD_DIM
    CW = conv_a_w.shape[-1] * 4
    F = w_down.shape[1] * 4
    PD = ps.shape[-1]
    IN_MAIN = 3 * CW + 4 * DNW
    IN_COLS = IN_MAIN + 2 * H
    assert w_in.shape[-1] * 4 == IN_COLS and CW + DNW == D and 2 * H <= LANES
    cb = _tile(min(CW, DNW), 512, LANES)
    while F % cb:
        cb -= LANES
    cidx = lax.axis_index("c").astype(jnp.int32).reshape(1)
    chip = 2 * lax.axis_index("x") + lax.axis_index("y")

    def halves(w):
        sh = w[0].astype(BF16)
        return sh.reshape(2, sh.shape[0] // 2, sh.shape[1])

    def whole(land):
        return land.reshape(4, 2 * land.shape[2], land.shape[3])

    def rows(g4):
        return g4.reshape(4 * g4.shape[1], g4.shape[2])

    conv_shapes = [conv_a_w[0].shape, conv_qkv_w[0].shape, conv_ffn_w[0].shape]
    cpack = _pack_rows([conv_a_w[0], conv_qkv_w[0], conv_ffn_w[0]])
    sh_in, sh_out, sh_up, sh_down, sh_pg, sh_pp = (halves(w) for w in (w_in, w_out, w_up, w_down, w_ple_gate, w_ple_proj))
    l_in, cg = _run_comm(_merge(_ag_relay_comm(sh_in), _ag_comm(cpack.reshape(2, cpack.shape[0] // 2, LANES))), "ag_w_in_conv")
    w_in_main, w_in_small = _join_shards(whole(l_in), IN_MAIN)
    cg = cg.reshape(4, cpack.shape[0], LANES)
    parts = [_unpack_rows(cg[t], conv_shapes) for t in range(4)]
    cw_a = jnp.concatenate([parts[t][0] for t in range(4)], axis=1)
    cw_qkv = jnp.concatenate([parts[t][1] for t in range(4)], axis=1)
    cw_ffn = jnp.concatenate([parts[t][2] for t in range(4)], axis=1)
    cw_q, cw_k, cw_v = cw_qkv[:, :DNW], cw_qkv[:, DNW:2 * DNW], cw_qkv[:, 2 * DNW:]
    cw_fg, cw_fv = cw_ffn[:, :F], cw_ffn[:, F:]
    pad_row = lambda v: jnp.pad(v, ((0, 0), (0, LANES - v.shape[1])))
    a_log_row, dt_row = pad_row(a_log), pad_row(dt_bias)
    gdn_t = jnp.tile(dn_norm_g, (1, H))
    gfin = final_norm_g.reshape(1, D)

    h1 = _rms_fwd(xs, norm_mix_g, "rms1")
    proj, (l_up,) = _mm(h1, w_in_main, mode="nn", out_dtypes=[F32], name="mm_proj", comm=_ag_comm(sh_up, q=0, nq=2))
    small = _mm(h1, w_in_small, mode="nn", out_dtypes=[F32], name="mm_small")
    ymix = _ga_fwd(proj, cw_a, CW, cb, D)
    nq = 3 * CW // cb
    nd = DNW // cb
    qn, (l_out,) = _qkv_fwd(proj, cw_q, nq, True, DNW, cb, "q_fwd", comm=_ag_comm(sh_out, q=0, nq=2))
    kn, (l_out,) = _qkv_fwd(proj, cw_k, nq + nd, True, DNW, cb, "k_fwd", comm=_ag_comm(sh_out, l_out, q=1, nq=2))
    vs = _qkv_fwd(proj, cw_v, nq + 2 * nd, False, DNW, cb, "v_fwd")
    g, beta = _gb_fwd(small, a_log_row, dt_row, H)
    o, S0, inv_c, (l_up,) = _delta_fwd(qn, kn, vs, g, beta, comm=_ag_comm(sh_up, l_up, q=1, nq=2))
    w_out_f = rows(whole(l_out))
    w_up_4 = whole(l_up)
    z_coff = (3 * CW + 3 * DNW) // DNW
    assert (3 * CW + 3 * DNW) % DNW == 0 and CW % DNW == 0
    ymix = _gnorm_fwd(o, proj, z_coff, gdn_t, DNW, ymix, CW // DNW)
    add = lambda acc, r: (r + acc,)

    def out_epi(acc, xv, gv):
        x1v = xv + acc
        return x1v, x1v * lax.rsqrt(jnp.mean(x1v * x1v, axis=1, keepdims=True) + EPS) * gv

    x1, h2 = _mm(ymix, w_out_f, mode="nn", out_dtypes=[F32, BF16], epi=out_epi, extras=[xs], rows=[norm_ffn_g], name="mm_out")
    up_g, (l_down,) = _mm(h2, w_up_4, mode="nn", b_split=(0, 2), out_dtypes=[F32], name="mm_up_g",
                          comm=_ag_comm(sh_down, q=0, nq=2))
    up_v, (l_down,) = _mm(h2, w_up_4, mode="nn", b_split=(2, 2), out_dtypes=[F32], name="mm_up_v",
                          comm=_ag_comm(sh_down, l_down, q=1, nq=2))
    w_down_f = rows(whole(l_down))
    act = _ffn_fwd(up_g, up_v, cw_fg, cw_fv, cb)
    x2, (l_pg, l_pp) = _mm(act, w_down_f, mode="nn", out_dtypes=[F32], epi=add, extras=[x1], name="mm_down",
                           comm=_merge(_ag_comm(sh_pg), _ag_comm(sh_pp)))
    w_pg_f = rows(whole(l_pg))
    w_pp_4 = whole(l_pp)
    h3 = _rms_fwd(x2, norm_ple_g, "rms3")
    pp = _mm(ps, w_pp_4, mode="nn", b_split=(0, 4), out_dtypes=[F32], name="mm_pp")

    def ple_final_epi(acc, x2v, ppv, tv, gv):
        pg = _sigmoid(acc)
        x3v = x2v + pg * ppv
        r = lax.rsqrt(jnp.mean(x3v * x3v, axis=1, keepdims=True) + EPS)
        xh = x3v * r
        e = xh * gv - tv
        dy = e * (1.0 / D)
        dxh = dy * gv
        dx = r * (dxh - xh * jnp.mean(dxh * xh, axis=1, keepdims=True))
        dg = jnp.sum(dy * xh, axis=0, keepdims=True)
        ls = jnp.sum(e * e, axis=0, keepdims=True) * (0.5 / D)
        return dx, dx * ppv * pg * (1.0 - pg), dx * pg, jnp.concatenate([dg, ls, jnp.zeros((HALO - 2, D), F32)], axis=0)

    dx3, dpg, dpp, fin = _mm(h3, w_pg_f, mode="nn", out_dtypes=[F32, BF16, BF16], parts=1, epi=ple_final_epi,
                             extras=[x2, pp, tgt], rows=[gfin], name="mm_pg_final")
    fin = jnp.sum(fin.reshape(-1, HALO, D), axis=0)
    loss = lax.psum(jnp.sum(fin[1]), ("x", "y", "c"))
    d_gfin = fin[0:1]
    def split_rows(dW):
        return dW.reshape(4, dW.shape[0] // 4, dW.shape[1])

    chip_idx = chip.astype(jnp.int32).reshape(1)
    own_sum = lambda S1, B, name: _add_own(S1, B, chip_idx, cidx, "rs_" + name + "_sum")

    dW_pp = _mm(ps, dpp, mode="tn", out_split=4, out_dtypes=[F32], name="mm_dw_pp")
    dW_pg = _mm(h3, dpg, mode="tn", out_dtypes=[F32], name="mm_dw_pg")
    P_pp, P_pg = _halves(dW_pp), _halves(split_rows(dW_pg))
    def rms_bwd_epi(acc, xv, dr, gv):
        dxv, dg = _rms_bwd_math(acc, xv, gv)
        return dr + dxv, dr + dxv, _row0(dg)

    (dx2, dx2_b, d_gple), (A_pp, A_pg) = _mm(dpg, w_pg_f, mode="nt", out_dtypes=[F32, BF16], parts=1, epi=rms_bwd_epi,
                                             extras=[x2, dx3], rows=[norm_ple_g], name="mm_dh3_rms",
                                             comm=_merge(_swap_comm(P_pp), _swap_comm(P_pg)))
    d_gple = jnp.sum(d_gple.reshape(-1, HALO, D), axis=0)
    S_pp = _add_half([P_pp], [A_pp], cidx, "rs_w_pp_add")
    S_pg = _add_half([P_pg], [A_pg], cidx, "rs_w_pg_add")
    dW_down, (B_pp, B_pg) = _mm(act, dx2_b, mode="tn", out_dtypes=[F32], name="mm_dw_down",
                                comm=_merge(_a2a_comm(S_pp), _a2a_comm(S_pg)))
    P_down = _halves(split_rows(dW_down))
    dact, (A_down, F_pp, F_pg) = _mm(dx2_b, w_down_f, mode="nt", out_dtypes=[F32], name="mm_dact", comm=_merge(
        _swap_comm(P_down), _fill_comm(own_sum(S_pp, B_pp, "w_pp")), _fill_comm(own_sum(S_pg, B_pg, "w_pg"))))
    S_down = _add_half([P_down], [A_down], cidx, "rs_w_down_add")
    dup_g, dup_v, dcw_fg, dcw_fv = _ffn_bwd(dact, up_g, up_v, cw_fg, cw_fv, cb)
    dW_up_g, (B_down,) = _mm(h2, dup_g, mode="tn", out_split=2, out_dtypes=[F32], name="mm_dw_up_g", comm=_a2a_comm(S_down))
    P_ug = _halves(dW_up_g)
    dW_up_v, (A_ug, F_down) = _mm(h2, dup_v, mode="tn", out_split=2, out_dtypes=[F32], name="mm_dw_up_v",
                                  comm=_merge(_swap_comm(P_ug), _fill_comm(own_sum(S_down, B_down, "w_down"))))
    P_uv = _halves(dW_up_v)
    dh2, (A_uv,) = _mm(dup_g, w_up_4, mode="nt", b_split=(0, 2), out_dtypes=[F32], name="mm_dh2_g", comm=_swap_comm(P_uv))
    S_up = _add_half([P_ug, P_uv], [A_ug, A_uv], cidx, "rs_w_up_add")
    dh2, (B_up,) = _mm(dup_v, w_up_4, mode="nt", b_split=(2, 2), out_dtypes=[F32], epi=add, extras=[dh2], name="mm_dh2_v",
                       comm=_a2a_comm(S_up, 0, 2))
    dx1, dx1_b, d_gffn = _rms_bwd(dh2, x1, norm_ffn_g, dx2, "rms2_bwd")
    P_out = _halves(split_rows(_mm(ymix, dx1_b, mode="tn", out_dtypes=[F32], name="mm_dw_out")))
    dymix, (A_out,) = _mm(dx1_b, w_out_f, mode="nt", out_dtypes=[F32], name="mm_dymix", comm=_swap_comm(P_out))
    S_out = _add_half([P_out], [A_out], cidx, "rs_w_out_add")
    dax, dab, dac, dcw_a = _ga_bwd(dymix, proj, cw_a, CW, cb)
    do, dz, d_gdn = _gnorm_bwd(dymix, CW // DNW, o, proj, z_coff, gdn_t, DNW)
    dqn, dkn, dvs, dgB, dbB, (B_up, B_out) = _delta_bwd(qn, kn, vs, g, beta, S0, inv_c, do,
                                                        comm=_merge(_a2a_comm(S_up, 1, 2, B_up), _a2a_comm(S_out)))
    dq_pre, dcw_q = _qkv_bwd(dqn, proj, cw_q, nq, True, DNW, cb, "q_bwd")
    dk_pre, dcw_k = _qkv_bwd(dkn, proj, cw_k, nq + nd, True, DNW, cb, "k_bwd")
    dv_pre, dcw_v = _qkv_bwd(dvs, proj, cw_v, nq + 2 * nd, False, DNW, cb, "v_bwd")
    dsmall, d_ab = _gb_bwd(dgB, dbB, small, g, beta, a_log_row, dt_row, H)
    dproj = jnp.concatenate([dax, dab, dac, dq_pre, dk_pre, dv_pre, dz], axis=1)
    dW_in_main, (F_up, F_out) = _mm(h1, dproj, mode="tn", out_dtypes=[F32], name="mm_dw_in", comm=_merge(
        _fill_comm(own_sum(S_up, B_up, "w_up")), _fill_comm(own_sum(S_out, B_out, "w_out"))))
    dW_in_small = _mm(h1, dsmall, mode="tn", out_dtypes=[F32], name="mm_dw_in_small")
    def update(Hf, w, m, v, name):
        gr = Hf.reshape(2 * Hf.shape[1], Hf.shape[2])
        if gr.shape[1] % LANES == 0:
            delta, m2, v2 = _adamw(w[0], gr, m[0], v[0], "adamw_" + name)
            return gr[None], delta[None], m2[None], v2[None]
        tr = jnp.transpose
        grt = tr(gr)
        delta, m2, v2 = _adamw(tr(w[0]), grt, tr(m[0]), tr(v[0]), "adamw_" + name)
        return tr(grt)[None], tr(delta)[None], tr(m2)[None], tr(v2)[None]

    P_in = _halves(_split_shards(dW_in_main, dW_in_small, IN_COLS // 4))
    (A_in,) = _run_comm(_swap_comm(P_in), "rs_w_in_swap")
    S_in = _add_half([P_in], [A_in], cidx, "rs_w_in_add")
    dh1, (B_in,) = _mm(dproj, w_in_main, mode="nt", out_dtypes=[F32], name="mm_dh1", comm=_a2a_comm(S_in))

    def rms1_epi(acc, dhv, xv, dr, gv):
        dxv, dg = _rms_bwd_math(acc + dhv, xv, gv)
        return dr + dxv, _row0(dg)

    dx, d_gmix = _mm(dsmall, w_in_small, mode="nt", out_dtypes=[F32], parts=1, epi=rms1_epi, extras=[dh1, xs, dx1],
                     rows=[norm_mix_g], name="mm_dh1_small_rms")
    d_gmix = jnp.sum(d_gmix.reshape(-1, HALO, D), axis=0)

    small_grads = [d_gmix[0:1], dcw_a[:cw_a.shape[0]], jnp.concatenate([dcw_q, dcw_k, dcw_v], axis=1)[:cw_qkv.shape[0]],
                   d_ab[0:1, :H], d_ab[1:2, :H], d_gdn[0:1], d_gffn[0:1],
                   jnp.concatenate([dcw_fg, dcw_fv], axis=1)[:cw_ffn.shape[0]], d_gple[0:1], d_gfin]
    small_shapes = [v.shape for v in small_grads]
    gpack = _pack_rows(small_grads)
    F_in, g8 = _run_comm(_merge(_fill_comm(own_sum(S_in, B_in, "w_in")), _gather_all_comm(gpack)), "rs_w_in_gather_small")
    big = {
        "w_in": update(F_in, w_in, m_w_in, v_w_in, "w_in"),
        "w_out": update(F_out, w_out, m_w_out, v_w_out, "w_out"),
        "w_up": update(F_up, w_up, m_w_up, v_w_up, "w_up"),
        "w_down": update(F_down, w_down, m_w_down, v_w_down, "w_down"),
        "w_ple_gate": update(F_pg, w_ple_gate, m_w_ple_gate, v_w_ple_gate, "w_pg"),
        "w_ple_proj": update(F_pp, w_ple_proj, m_w_ple_proj, v_w_ple_proj, "w_pp"),
    }

    gsum = _sum_stack(g8, "sum_small")
    (g_gmix, g_cwa, g_cwqkv, g_alog, g_dt, g_gdn, g_gffn, g_cwffn, g_gple, g_gfin) = _unpack_rows(gsum, small_shapes)

    def my_cols(v):
        Cc = v.shape[1] // 4
        return lax.dynamic_slice_in_dim(v, chip * Cc, Cc, axis=1)

    g_small = [g_gmix, my_cols(g_cwa), my_cols(g_cwqkv), g_alog, g_dt, g_gdn, g_gffn, my_cols(g_cwffn), g_gple, g_gfin]
    w_small = [norm_mix_g, conv_a_w[0], conv_qkv_w[0], a_log, dt_bias, dn_norm_g, norm_ffn_g, conv_ffn_w[0], norm_ple_g, gfin]
    m_small = [m_norm_mix_g, m_conv_a_w[0], m_conv_qkv_w[0], m_a_log, m_dt_bias, m_dn_norm_g, m_norm_ffn_g, m_conv_ffn_w[0],
               m_norm_ple_g, m_final_norm_g.reshape(1, D)]
    v_small = [v_norm_mix_g, v_conv_a_w[0], v_conv_qkv_w[0], v_a_log, v_dt_bias, v_dn_norm_g, v_norm_ffn_g, v_conv_ffn_w[0],
               v_norm_ple_g, v_final_norm_g.reshape(1, D)]
    shp = [v.shape for v in w_small]
    ds_, ms_, vs_ = _adamw(_pack_rows(w_small), _pack_rows(g_small), _pack_rows(m_small), _pack_rows(v_small), "adamw_small")
    out_shapes = [norm_mix_g.shape, conv_a_w.shape, conv_qkv_w.shape, a_log.shape, dt_bias.shape, dn_norm_g.shape,
                  norm_ffn_g.shape, conv_ffn_w.shape, norm_ple_g.shape, final_norm_g.shape]
    rs = lambda vals: [v.reshape(s) for v, s in zip(vals, out_shapes)]
    sg, sd_, sm_, sv_ = rs(g_small), rs(_unpack_rows(ds_, shp)), rs(_unpack_rows(ms_, shp)), rs(_unpack_rows(vs_, shp))
    names_small = ["norm_mix_g", "conv_a_w", "conv_qkv_w", "a_log", "dt_bias", "dn_norm_g", "norm_ffn_g", "conv_ffn_w",
                   "norm_ple_g", "final_norm_g"]
    res = {n: (sg[i], sd_[i], sm_[i], sv_[i]) for i, n in enumerate(names_small)}
    res.update(big)
    order = ["norm_mix_g", "w_in", "conv_a_w", "conv_qkv_w", "a_log", "dt_bias", "dn_norm_g", "w_out", "norm_ffn_g", "w_up",
             "conv_ffn_w", "w_down", "norm_ple_g", "w_ple_gate", "w_ple_proj", "final_norm_g"]
    return (loss, dx[None], *[res[n][0] for n in order], *[res[n][1] for n in order], *[res[n][2] for n in order],
            *[res[n][3] for n in order])
```

```python
import functools

import jax
import jax.numpy as jnp
from jax import lax
from jax.experimental import pallas as pl
from jax.experimental.pallas import tpu as pltpu

F32 = jnp.float32
BF16 = jnp.bfloat16
LANES = 128
HALO = 8
HEAD_DIM = 128
CHUNK = 64
EPS = 1e-6
VMEM_LIMIT = 56 * 1024 * 1024
MM_VMEM_BUDGET = 40 * 1024 * 1024
MM_STEP_BYTES = 1 << 20
EW_VMEM_BUDGET = 28 * 1024 * 1024
MESH = pl.DeviceIdType.MESH

ADAM_LR, ADAM_B1, ADAM_B2, ADAM_EPS, ADAM_WD, ADAM_STEP = 0.001, 0.9, 0.999, 1e-08, 0.01, 10


def _tile(n, cap, unit):
    if n <= cap:
        return n
    d = (cap // unit) * unit
    while d >= unit:
        if n % d == 0:
            return d
        d -= unit
    raise ValueError(f"no tile for {n} (cap {cap}, unit {unit})")


def _sigmoid(x):
    return 1.0 / (1.0 + jnp.exp(-x))


def _divisors(n, cap):
    ds = [d for d in range(cap // LANES * LANES, 0, -LANES) if n % d == 0]
    return [n] if (n <= cap or not ds) else ds


def _mm_tiles(M, N, K, n_unit, k_unit, a_bytes, n_blocks_mn, a_transposed, tn_full=False):
    best = None
    for tm in _divisors(M, 1536):
        for tn in ([N] if tn_full else _divisors(n_unit, 1536)):
            for tk in _divisors(k_unit, 4096):
                nk = K // tk
                vmem = 2 * tm * tk * a_bytes + 2 * tk * tn * 2 + 2 * 4 * tm * tn * n_blocks_mn + (4 * tm * tn if nk > 1 else 0)
                if vmem > MM_VMEM_BUDGET:
                    continue
                steps = (M // tm) * (N // tn) * nk
                b_reads = 1 if (nk == 1 and N == tn) else M // tm
                cost = (M * K * a_bytes * (N // tn if nk > 1 else 1) + K * N * 2 * b_reads + 4 * M * N * n_blocks_mn
                        + (8 * M * N * nk // 3 if nk > 1 else 0) + steps * MM_STEP_BYTES
                        + (2 * steps * tm * tk if a_transposed else 0))
                if best is None or cost < best[0]:
                    best = (cost, tm, tn, tk)
    return best[1:]


def _mm(a, b, *, mode, out_dtypes, name, epi=None, extras=(), comm=None, b_split=None, out_split=None, rows=(), parts=0):
    if b_split is not None:
        lo, ns = b_split
        Rb, Cb = b.shape[1], b.shape[2]
    if mode == "nn":
        (M, K), N = a.shape, (ns * Cb if b_split else b.shape[1])
    elif mode == "nt":
        (M, K), N = a.shape, (Rb if b_split else b.shape[0])
    else:
        (K, M), N = a.shape, b.shape[1]
    n_ex, n_out = len(extras), len(out_dtypes)
    n_unit = Cb if (b_split and mode == "nn") else (N // out_split if out_split else N)
    n_rows = len(rows)
    assert not (n_rows and (b_split or out_split))
    k_unit = Cb if (b_split and mode == "nt") else K
    mn_blocks = (sum(e.dtype.itemsize for e in extras) + sum(jnp.dtype(d).itemsize for d in out_dtypes)) / 4
    tm, tn, tk = _mm_tiles(M, N, K, n_unit, k_unit, a.dtype.itemsize, mn_blocks, mode == "tn", tn_full=bool(n_rows))
    nk = K // tk
    a_spec = pl.BlockSpec((tk, tm), lambda i, j, k: (k, i)) if mode == "tn" else pl.BlockSpec((tm, tk), lambda i, j, k: (i, k))
    if b_split and mode == "nn":
        nb = Cb // tn
        b_spec = pl.BlockSpec((None, tk, tn), lambda i, j, k: (lo + j // nb, k, j % nb))
    elif b_split:
        nb = Cb // tk
        b_spec = pl.BlockSpec((None, tn, tk), lambda i, j, k: (lo + k // nb, j, k % nb))
    else:
        b_spec = pl.BlockSpec((tn, tk), lambda i, j, k: (j, k)) if mode == "nt" else pl.BlockSpec((tk, tn), lambda i, j, k: (k, j))
    mn_spec = pl.BlockSpec((tm, tn), lambda i, j, k: (i, j))
    out_shapes = [jax.ShapeDtypeStruct((M, N), dt) for dt in out_dtypes] + [jax.ShapeDtypeStruct((M // tm * HALO, N), F32)] * parts
    out_specs = [mn_spec] * n_out + [pl.BlockSpec((HALO, tn), lambda i, j, k: (i, j))] * parts
    if out_split:
        assert n_ex == 0 and n_out == 1
        nbo = (N // out_split) // tn
        out_specs = [pl.BlockSpec((None, tm, tn), lambda i, j, k: (j // nbo, i, j % nbo))]
        out_shapes = [jax.ShapeDtypeStruct((out_split, M, N // out_split), out_dtypes[0])]
    dims = {"nn": (((1,), (0,)), ((), ())), "nt": (((1,), (1,)), ((), ())), "tn": (((0,), (0,)), ((), ()))}[mode]

    def body(*refs):
        a_ref, b_ref = refs[0], refs[1]
        ex_refs = refs[2:2 + n_ex + n_rows]
        out_refs = refs[2 + n_ex + n_rows:2 + n_ex + n_rows + n_out + parts]
        part = lax.dot_general(a_ref[...].astype(BF16), b_ref[...].astype(BF16), dims, preferred_element_type=F32)

        def finish(acc):
            outs = (acc,) if epi is None else epi(acc, *[r[...] for r in ex_refs])
            for r, o in zip(out_refs, outs):
                r[...] = o.astype(r.dtype)

        if nk == 1:
            finish(part)
            return
        acc_ref = refs[-1]
        k = pl.program_id(2)

        @pl.when(k == 0)
        def _():
            acc_ref[...] = part

        @pl.when(jnp.logical_and(k > 0, k < nk - 1))
        def _():
            acc_ref[...] += part

        @pl.when(k == nk - 1)
        def _():
            finish(acc_ref[...] + part)

    outs, comm_outs = _call(
        body, name=name, grid=(M // tm, N // tn, nk),
        in_specs=[a_spec, b_spec] + [mn_spec] * n_ex + [pl.BlockSpec((1, tn), lambda i, j, k: (0, j))] * n_rows,
        out_specs=out_specs,
        out_shape=out_shapes,
        scratch_shapes=[pltpu.VMEM((tm, tn), F32)] if nk > 1 else [],
        semantics=("parallel", "parallel", "arbitrary"), args=(a, b, *extras, *rows), comm=comm)
    res = outs[0] if n_out + parts == 1 else outs
    return res if comm is None else (res, comm_outs)


def _tiled(fn, *, T, C, ins, out_dtypes=(), acc_rows=(), tb=None, cb=512, name, comm=None, into=None):
    tb = _tile(T, tb or (1024 if cb <= 512 else 512 if cb <= 1024 else 256), HALO)
    nI, nJ = T // tb, C // cb
    hb, nH = tb // HALO, T // HALO
    specs, args, kinds = [], [], []
    for kind, arr, cmap in ins:
        cm = cmap if cmap is not None else (lambda j: j)
        kinds.append(kind)
        if kind == "cur":
            specs.append(pl.BlockSpec((tb, cb), lambda j, i, cm=cm: (i, cm(j))))
            args.append(arr)
        elif kind == "ext":
            specs.append(pl.BlockSpec((HALO, cb), lambda j, i, cm=cm: (jnp.maximum(i * hb - 1, 0), cm(j))))
            specs.append(pl.BlockSpec((tb, cb), lambda j, i, cm=cm: (i, cm(j))))
            specs.append(pl.BlockSpec((HALO, cb), lambda j, i, cm=cm: (jnp.minimum((i + 1) * hb, nH - 1), cm(j))))
            args += [arr, arr, arr]
        elif kind == "row":
            specs.append(pl.BlockSpec((arr.shape[0], cb), lambda j, i, cm=cm: (0, cm(j))))
            args.append(arr)
        elif kind == "stack":
            specs.append(pl.BlockSpec((arr.shape[0], tb, cb), lambda j, i, cm=cm: (0, i, cm(j))))
            args.append(arr)
        else:
            raise ValueError(kind)
    n_in = len(args)
    n_out, n_acc = len(out_dtypes), len(acc_rows)

    def body(*refs):
        j, i = pl.program_id(0), pl.program_id(1)
        vals, r = [], 0
        for kind in kinds:
            if kind == "ext":
                prev = jnp.where(i == 0, 0.0, refs[r][...].astype(F32))
                cur = refs[r + 1][...].astype(F32)
                nxt = jnp.where(i == nI - 1, 0.0, refs[r + 2][...].astype(F32))
                vals.append(jnp.concatenate([prev, cur, nxt], axis=0))
                r += 3
            else:
                vals.append(refs[r][...])
                r += 1
        res = fn(j, i, *vals)
        for ref, o in zip(refs[n_in:n_in + n_out], res[:n_out]):
            ref[...] = o.astype(ref.dtype)
        for ref, o in zip(refs[n_in + n_out:], res[n_out:]):
            @pl.when(i == 0)
            def _(ref=ref, o=o):
                ref[...] = o

            @pl.when(i > 0)
            def _(ref=ref, o=o):
                ref[...] += o

    out_specs = [pl.BlockSpec((tb, cb), lambda j, i: (i, j))] * n_out
    out_shape = [jax.ShapeDtypeStruct((T, C), dt) for dt in out_dtypes]
    io_aliases = None
    if into is not None:
        buf, total, off = into
        assert n_out == 1 and comm is None
        out_specs = [pl.BlockSpec((tb, cb), lambda j, i: (i, j + off))]
        out_shape = [jax.ShapeDtypeStruct((T, total), out_dtypes[0])]
        if buf is not None:
            specs, args, io_aliases = specs + [ANY], args + [buf], {n_in: 0}
            n_in += 1
    outs, comm_outs = _call(
        body, name=name, grid=(nJ, nI), in_specs=specs,
        out_specs=out_specs + [pl.BlockSpec((rows, cb), lambda j, i: (0, j)) for rows in acc_rows],
        out_shape=out_shape + [jax.ShapeDtypeStruct((rows, C), F32) for rows in acc_rows],
        scratch_shapes=[], semantics=("parallel", "arbitrary"), args=args, comm=comm, io_aliases=io_aliases)
    return outs if comm is None else (outs, comm_outs)


def _conv_causal(xe, w):
    K = w.shape[0]
    y = xe * w[K - 1:K]
    for j in range(K - 1):
        y = y + pltpu.roll(xe, K - 1 - j, 0) * w[j:j + 1]
    return y


def _conv_anti(de, w):
    K, n = w.shape[0], de.shape[0]
    y = de * w[K - 1:K]
    for j in range(K - 1):
        y = y + pltpu.roll(de, n - (K - 1 - j), 0) * w[j:j + 1]
    return y


def _conv_dw(dce, xe, K):
    n = dce.shape[0]
    tb = n - 2 * HALO
    rows = []
    for j in range(K):
        xs = xe if j == K - 1 else pltpu.roll(xe, K - 1 - j, 0)
        rows.append(jnp.sum((dce * xs)[HALO:HALO + tb], axis=0, keepdims=True))
    rows.append(jnp.zeros((HALO - K, dce.shape[1]), F32))
    return jnp.concatenate(rows, axis=0)


def _own(xe):
    return xe[HALO:xe.shape[0] - HALO]


def _row0(v):
    return jnp.concatenate([v, jnp.zeros((HALO - 1, v.shape[1]), F32)], axis=0)


def _per_head(fn, *xs):
    n = xs[0].shape[1] // HEAD_DIM
    outs = [fn(*[x[:, g * HEAD_DIM:(g + 1) * HEAD_DIM] for x in xs]) for g in range(n)]
    return outs[0] if n == 1 else jnp.concatenate(outs, axis=1)


def _rms_fwd(x, g, name):
    T, D = x.shape

    def fn(j, i, xv, gv):
        r = lax.rsqrt(jnp.mean(xv * xv, axis=1, keepdims=True) + EPS)
        return (xv * r * gv,)

    return _tiled(fn, T=T, C=D, ins=[("cur", x, None), ("row", g, None)], out_dtypes=[BF16], cb=D, name=name)[0]


def _rms_bwd_math(dy, xv, gv):
    r = lax.rsqrt(jnp.mean(xv * xv, axis=1, keepdims=True) + EPS)
    xh = xv * r
    dxh = dy * gv
    dx = r * (dxh - xh * jnp.mean(dxh * xh, axis=1, keepdims=True))
    dg = jnp.sum(dy * xh, axis=0, keepdims=True)
    return dx, dg


def _rms_bwd(dh, x, g, dres, name, comm=None):
    T, D = x.shape

    def fn(j, i, dhv, xv, gv, dr):
        dx, dg = _rms_bwd_math(dhv, xv, gv)
        return dr + dx, dr + dx, _row0(dg)

    return _tiled(fn, T=T, C=D, ins=[("cur", dh, None), ("cur", x, None), ("row", g, None), ("cur", dres, None)],
                  out_dtypes=[F32, BF16], acc_rows=[HALO], cb=D, name=name, comm=comm)


def _ga_fwd(proj, w_a, CW, cb, total):
    T = proj.shape[0]
    n = CW // cb

    def fn(j, i, ax, ab, ac, w):
        c = _conv_causal(ac * ax, w)
        return (ab * _own(c),)

    return _tiled(fn, T=T, C=CW, ins=[("ext", proj, None), ("cur", proj, lambda j: j + n), ("ext", proj, lambda j: j + 2 * n),
                                       ("row", w_a, None)], out_dtypes=[BF16], cb=cb, name="ga_fwd", into=(None, total, 0))[0]


def _ga_bwd(dymix, proj, w_a, CW, cb):
    T = proj.shape[0]
    n = CW // cb
    K = w_a.shape[0]

    def fn(j, i, dy, ax, ab, ac, w):
        u = ac * ax
        c = _conv_causal(u, w)
        dc = dy * ab
        du = _conv_anti(dc, w)
        return _own(du * ac), _own(dy * c), _own(du * ax), _conv_dw(dc, u, K)

    return _tiled(fn, T=T, C=CW, ins=[("ext", dymix, None), ("ext", proj, None), ("ext", proj, lambda j: j + n),
                                       ("ext", proj, lambda j: j + 2 * n), ("row", w_a, None)],
                  out_dtypes=[BF16, BF16, BF16], acc_rows=[HALO], cb=cb, name="ga_bwd")


def _l2n(s):
    return s * lax.rsqrt(jnp.sum(s * s, axis=1, keepdims=True) + EPS)


def _qkv_fwd(proj, w_sec, coff, normalize, DNW, cb, name, comm=None):
    T = proj.shape[0]

    def fn(j, i, pre, w):
        c = _own(_conv_causal(pre, w))
        s = c * _sigmoid(c)
        return (_per_head(_l2n, s) if normalize else s,)

    res = _tiled(fn, T=T, C=DNW, ins=[("ext", proj, lambda j: j + coff), ("row", w_sec, None)],
                 out_dtypes=[F32], cb=cb, name=name, comm=comm)
    return res[0] if comm is None else (res[0][0], res[1])


def _qkv_bwd(dsec, proj, w_sec, coff, normalize, DNW, cb, name):
    T = proj.shape[0]
    K = w_sec.shape[0]

    def l2n_bwd(s, dn):
        r = lax.rsqrt(jnp.sum(s * s, axis=1, keepdims=True) + EPS)
        nrm = s * r
        return r * (dn - nrm * jnp.sum(dn * nrm, axis=1, keepdims=True))

    def fn(j, i, dn, pre, w):
        c = _conv_causal(pre, w)
        sg = _sigmoid(c)
        s = c * sg
        ds = _per_head(l2n_bwd, s, dn) if normalize else dn
        dc = ds * (sg * (1.0 + c * (1.0 - sg)))
        return _own(_conv_anti(dc, w)), _conv_dw(dc, pre, K)

    return _tiled(fn, T=T, C=DNW, ins=[("ext", dsec, None), ("ext", proj, lambda j: j + coff), ("row", w_sec, None)],
                  out_dtypes=[BF16], acc_rows=[HALO], cb=cb, name=name)


def _gb_fwd(small, a_log_row, dt_row, H):
    T = small.shape[0]

    def fn(j, i, sm, al, dt):
        z = sm + dt
        sp = jnp.maximum(z, 0.0) + jnp.log(1.0 + jnp.exp(-jnp.abs(z)))
        g = -jnp.exp(al) * sp
        beta = _sigmoid(pltpu.roll(sm, LANES - H, 1))
        return g, beta

    return _tiled(fn, T=T, C=LANES, ins=[("cur", small, None), ("row", a_log_row, None), ("row", dt_row, None)],
                  out_dtypes=[F32, F32], cb=LANES, name="gb_fwd")


def _gb_bwd(dgB, dbB, small, g, beta, a_log_row, dt_row, H):
    T = small.shape[0]

    def fn(j, i, dgv, dbv, sm, gv, bv, al, dt):
        lane = lax.broadcasted_iota(jnp.int32, sm.shape, 1)
        dg = jnp.zeros(sm.shape, F32)
        db = jnp.zeros(sm.shape, F32)
        for h in range(H):
            dg = jnp.where(lane == h, jnp.sum(dgv[h], axis=1, keepdims=True), dg)
            db = jnp.where(lane == h, jnp.sum(dbv[h], axis=1, keepdims=True), db)
        da = dg * (-jnp.exp(al)) * _sigmoid(sm + dt)
        dbb = db * bv * (1.0 - bv)
        dsm = jnp.where(lane < H, da, 0.0) + pltpu.roll(jnp.where(lane < H, dbb, 0.0), H, 1)
        d_alog = jnp.sum(jnp.where(lane < H, dg * gv, 0.0), axis=0, keepdims=True)
        d_dt = jnp.sum(jnp.where(lane < H, da, 0.0), axis=0, keepdims=True)
        return dsm, jnp.concatenate([d_alog, d_dt, jnp.zeros((HALO - 2, LANES), F32)], axis=0)

    return _tiled(fn, T=T, C=LANES, ins=[("stack", dgB, None), ("stack", dbB, None), ("cur", small, None), ("cur", g, None),
                                          ("cur", beta, None), ("row", a_log_row, None), ("row", dt_row, None)],
                  out_dtypes=[BF16], acc_rows=[HALO], cb=LANES, name="gb_bwd")


_DIMS = {"nn": (((1,), (0,)), ((), ())), "nt": (((1,), (1,)), ((), ())), "tn": (((0,), (0,)), ((), ()))}
_DOT_BWD = {"nn": (("nt", "gb"), ("tn", "ag")), "nt": (("nn", "gb"), ("tn", "ga")), "tn": (("nt", "bg"), ("nn", "ag"))}


def _split(a):
    hi = a.astype(BF16)
    return hi, (a - hi.astype(F32)).astype(BF16)


def _raw_dot(a, b, kind, passes):
    dg = lambda x, y: lax.dot_general(x, y, _DIMS[kind], preferred_element_type=F32)
    if passes == 1:
        return dg(a.astype(BF16), b.astype(BF16))
    ah, al = _split(a)
    bh, bl = _split(b)
    if kind == "tn":
        return dg(ah, bh) + (dg(ah, bl) + dg(al, bh))
    m = a.shape[0]
    top = dg(jnp.concatenate([ah, al], axis=0), bh)
    return top[:m] + (dg(ah, bl) + top[m:])


def _raw_dot_exact(a, b, kind, exact):
    dg = lambda x, y: lax.dot_general(x, y, _DIMS[kind], preferred_element_type=F32)
    if exact == "a":
        bh, bl = _split(b)
        return dg(a.astype(BF16), bh) + dg(a.astype(BF16), bl)
    ah, al = _split(a)
    return dg(ah, b.astype(BF16)) + dg(al, b.astype(BF16))


@functools.lru_cache(maxsize=None)
def _dotc(kind):
    @jax.custom_vjp
    def f(a, b):
        return _raw_dot_exact(a, b, kind, "a")

    def fwd(a, b):
        return _raw_dot_exact(a, b, kind, "a"), a

    def bwd(a, g):
        db = _raw_dot_exact(a, g, "tn", "a") if kind == "nn" else _raw_dot_exact(g, a, "tn", "b")
        return jnp.zeros_like(a), db

    f.defvjp(fwd, bwd)
    return f


@functools.lru_cache(maxsize=None)
def _dotf(kind, passes):
    @jax.custom_vjp
    def f(a, b):
        return _raw_dot(a, b, kind, passes)

    def fwd(a, b):
        return _raw_dot(a, b, kind, passes), (a, b)

    def bwd(res, g):
        ops = {"a": res[0], "b": res[1], "g": g}
        (ka, oa), (kb, ob) = _DOT_BWD[kind]
        return (_raw_dot(ops[oa[0]], ops[oa[1]], ka, passes), _raw_dot(ops[ob[0]], ops[ob[1]], kb, passes))

    f.defvjp(fwd, bwd)
    return f


@jax.custom_vjp
def _saved_inverse(L, inv):
    return inv


def _saved_inverse_fwd(L, inv):
    return inv, inv


def _saved_inverse_bwd(inv, g):
    d3nt, d3tn = _dotf("nt", 3), _dotf("tn", 3)
    return -d3nt(d3tn(inv, g), inv), jnp.zeros_like(inv)


_saved_inverse.defvjp(_saved_inverse_fwd, _saved_inverse_bwd)


def _chunk_fn(q, k, v, gB, bB, S, inv_saved=None):
    C = CHUNK
    d3 = _dotf("nn", 3)
    d1, d1nt, d1tn = _dotf("nn", 1), _dotf("nt", 1), _dotf("tn", 1)
    each = lambda f, *ls: tuple(f(*xs) for xs in zip(*ls))
    row = lax.broadcasted_iota(jnp.int32, (C, C), 0)
    col = lax.broadcasted_iota(jnp.int32, (C, C), 1)
    causal = row >= col
    strict = row > col
    tril = jnp.where(causal, 1.0, 0.0).astype(F32)
    eye = jnp.where(row == col, 1.0, 0.0).astype(F32)
    avg = jnp.full((C, HEAD_DIM), 1.0 / HEAD_DIM, F32)
    gc = each(lambda g: _dotc("nn")(tril, g), gB)
    R = each(lambda g: _dotc("nt")(avg, g), gc)
    decay = each(lambda g, r: jnp.where(causal, jnp.exp(jnp.where(causal, g[:, :C] - r, 0.0)), 0.0), gc, R)
    kk = each(lambda x: d1nt(x, x), k)
    L = each(lambda a, d, b: jnp.where(strict, a * d * b[:, :C], 0.0), kk, decay, bB)
    if inv_saved is None:
        inv = each(lambda l: eye - l, L)
        P = L
        for _ in range(5):
            P = each(lambda p: d3(p, p), P)
            inv = each(lambda a, p: d3(a, eye + p), inv, P)
    else:
        inv = each(_saved_inverse, L, inv_saved)
    eg = each(jnp.exp, gc)
    u = each(lambda a, x, b: d3(a, x * b), inv, v, bB)
    w = each(lambda a, x, b, e: d3(a, x * b * e), inv, k, bB, eg)
    qs = each(lambda x: x * (HEAD_DIM ** -0.5), q)
    qk = each(lambda a, x, d: d1nt(a, x) * d, qs, k, decay)
    gl = each(lambda g: g[C - 1:C, :], gc)
    kd = each(lambda x, a, g: x * jnp.exp(a - g), k, gl, gc)
    qe = each(lambda a, e: a * e, qs, eg)
    nh = len(S)
    o = ()
    for c in range(len(q) // nh):
        sl = slice(c * nh, (c + 1) * nh)
        v_new = each(lambda a, b, s: a - d1(b, s), u[sl], w[sl], S)
        o1 = each(lambda a, s: d1(a, s), qe[sl], S)
        o += each(lambda a, b, vn: a + d1(b, vn), o1, qk[sl], v_new)
        kv = each(lambda x, vn: d1tn(x, vn), kd[sl], v_new)
        S = each(lambda s, a, b: s * jnp.exp(a) + b, S, gl[sl], kv)
    return (o, S), inv


def _sel_lane(x, h):
    lane = lax.broadcasted_iota(jnp.int32, x.shape, 1)
    return jnp.broadcast_to(jnp.sum(jnp.where(lane == h, x, 0.0), axis=1, keepdims=True), x.shape)


def _tile_of(ref, c, h):
    return ref[c * CHUNK:(c + 1) * CHUNK, h * HEAD_DIM:(h + 1) * HEAD_DIM]


def _chunks_per_step(N):
    return 4 if N % 4 == 0 else (2 if N % 2 == 0 else 1)


def _delta_fwd(q, k, v, g, beta, comm=None):
    T = q.shape[0]
    H, N = q.shape[1] // HEAD_DIM, T // CHUNK
    cps = _chunks_per_step(N)
    rows = cps * CHUNK

    def body(q_ref, k_ref, v_ref, g_ref, b_ref, o_ref, s_ref, inv_ref, S):
        @pl.when(pl.program_id(0) == 0)
        def _():
            S[...] = jnp.zeros_like(S)

        gv, bv = g_ref[...], b_ref[...]
        pairs = lambda f: tuple(f(c, h) for c in range(cps) for h in range(H))
        S_in = tuple(S[h] for h in range(H))
        for h in range(H):
            s_ref[h, 0] = S_in[h]
        (o, S_new), inv = _chunk_fn(pairs(lambda c, h: _tile_of(q_ref, c, h)), pairs(lambda c, h: _tile_of(k_ref, c, h)),
                                    pairs(lambda c, h: _tile_of(v_ref, c, h)),
                                    pairs(lambda c, h: _sel_lane(gv[c * CHUNK:(c + 1) * CHUNK], h)),
                                    pairs(lambda c, h: _sel_lane(bv[c * CHUNK:(c + 1) * CHUNK], h)), S_in)
        for c in range(cps):
            for h in range(H):
                o_ref[c * CHUNK:(c + 1) * CHUNK, h * HEAD_DIM:(h + 1) * HEAD_DIM] = o[c * H + h]
                inv_ref[h, c] = inv[c * H + h]
        for h in range(H):
            S[h] = S_new[h]

    blk = pl.BlockSpec((rows, H * HEAD_DIM), lambda n: (n, 0))
    gblk = pl.BlockSpec((rows, LANES), lambda n: (n, 0))
    outs, comm_outs = _call(
        body, name="delta_fwd", grid=(N // cps,), in_specs=[blk, blk, blk, gblk, gblk],
        out_specs=[blk, pl.BlockSpec((H, 1, HEAD_DIM, HEAD_DIM), lambda n: (0, n, 0, 0)),
                   pl.BlockSpec((H, cps, CHUNK, CHUNK), lambda n: (0, n, 0, 0))],
        out_shape=[jax.ShapeDtypeStruct((T, H * HEAD_DIM), F32), jax.ShapeDtypeStruct((H, N // cps, HEAD_DIM, HEAD_DIM), F32),
                   jax.ShapeDtypeStruct((H, N, CHUNK, CHUNK), F32)],
        scratch_shapes=[pltpu.VMEM((H, HEAD_DIM, HEAD_DIM), F32)],
        semantics=("arbitrary",), args=(q, k, v, g, beta), comm=comm)
    return outs[0], outs[1], outs[2], comm_outs


def _delta_bwd(q, k, v, g, beta, S0, inv, do, comm=None):
    T = q.shape[0]
    H, N = q.shape[1] // HEAD_DIM, T // CHUNK
    cps = _chunks_per_step(N)
    rows, NS = cps * CHUNK, N // cps

    def body(q_ref, k_ref, v_ref, g_ref, b_ref, s_ref, inv_ref, do_ref, dq_ref, dk_ref, dv_ref, dg_ref, db_ref, dS):
        @pl.when(pl.program_id(0) == 0)
        def _():
            dS[...] = jnp.zeros_like(dS)

        gv, bv = g_ref[...], b_ref[...]
        pairs = lambda f: tuple(f(c, h) for c in range(cps) for h in range(H))
        heads = lambda f: tuple(f(h) for h in range(H))
        _, vjp, _ = jax.vjp(_chunk_fn, pairs(lambda c, h: _tile_of(q_ref, c, h)), pairs(lambda c, h: _tile_of(k_ref, c, h)),
                            pairs(lambda c, h: _tile_of(v_ref, c, h)),
                            pairs(lambda c, h: _sel_lane(gv[c * CHUNK:(c + 1) * CHUNK], h)),
                            pairs(lambda c, h: _sel_lane(bv[c * CHUNK:(c + 1) * CHUNK], h)),
                            heads(lambda h: s_ref[h, 0]), pairs(lambda c, h: inv_ref[h, c]), has_aux=True)
        dq, dk, dv, dgB, dbB, dS_prev, _ = vjp((pairs(lambda c, h: _tile_of(do_ref, c, h)), heads(lambda h: dS[h])))
        for c in range(cps):
            for h in range(H):
                r, sl = slice(c * CHUNK, (c + 1) * CHUNK), slice(h * HEAD_DIM, (h + 1) * HEAD_DIM)
                dq_ref[r, sl] = dq[c * H + h]
                dk_ref[r, sl] = dk[c * H + h]
                dv_ref[r, sl] = dv[c * H + h]
                dg_ref[h, r] = dgB[c * H + h]
                db_ref[h, r] = dbB[c * H + h]
        for h in range(H):
            dS[h] = dS_prev[h]

    blk = pl.BlockSpec((rows, H * HEAD_DIM), lambda n: (NS - 1 - n, 0))
    gblk = pl.BlockSpec((rows, LANES), lambda n: (NS - 1 - n, 0))
    hblk = pl.BlockSpec((H, rows, LANES), lambda n: (0, NS - 1 - n, 0))
    sd = jax.ShapeDtypeStruct
    outs, comm_outs = _call(
        body, name="delta_bwd", grid=(NS,),
        in_specs=[blk, blk, blk, gblk, gblk, pl.BlockSpec((H, 1, HEAD_DIM, HEAD_DIM), lambda n: (0, NS - 1 - n, 0, 0)),
                  pl.BlockSpec((H, cps, CHUNK, CHUNK), lambda n: (0, NS - 1 - n, 0, 0)), blk],
        out_specs=[blk, blk, blk, hblk, hblk],
        out_shape=[sd((T, H * HEAD_DIM), F32)] * 3 + [sd((H, T, LANES), F32)] * 2,
        scratch_shapes=[pltpu.VMEM((H, HEAD_DIM, HEAD_DIM), F32)],
        semantics=("arbitrary",), args=(q, k, v, g, beta, S0, inv, do), comm=comm)
    return (*outs, comm_outs)


def _gnorm_fwd(o, proj, z_coff, gdn_t, DNW, buf, coff):
    T = o.shape[0]

    def fn(j, i, ov, zv, gv):
        def one(oh, zh, gh):
            r = lax.rsqrt(jnp.mean(oh * oh, axis=1, keepdims=True) + EPS)
            return oh * r * gh * (zh * _sigmoid(zh))
        return (_per_head(one, ov, zv, jnp.broadcast_to(gv, ov.shape)),)

    return _tiled(fn, T=T, C=DNW, ins=[("cur", o, None), ("cur", proj, lambda j: j + z_coff), ("row", gdn_t, None)],
                  out_dtypes=[BF16], cb=DNW, name="gnorm_fwd", into=(buf, buf.shape[1], coff))[0]


def _gnorm_bwd(dymix, y_coff, o, proj, z_coff, gdn_t, DNW):
    T = o.shape[0]
    nh = DNW // HEAD_DIM

    def fn(j, i, dy, ov, zv, gv):
        dos, dzs, dgs = [], [], jnp.zeros((1, HEAD_DIM), F32)
        for h in range(nh):
            sl = slice(h * HEAD_DIM, (h + 1) * HEAD_DIM)
            dyh, oh, zh, gh = dy[:, sl].astype(F32), ov[:, sl], zv[:, sl], gv[:, sl]
            r = lax.rsqrt(jnp.mean(oh * oh, axis=1, keepdims=True) + EPS)
            on = oh * r
            sg = _sigmoid(zh)
            sz = zh * sg
            dzs.append(dyh * on * gh * (sg * (1.0 + zh * (1.0 - sg))))
            don = dyh * gh * sz
            dos.append(r * (don - on * jnp.mean(don * on, axis=1, keepdims=True)))
            dgs = dgs + jnp.sum(dyh * on * sz, axis=0, keepdims=True)
        cat = (lambda xs: xs[0] if nh == 1 else jnp.concatenate(xs, axis=1))
        return cat(dos), cat(dzs), _row0(dgs)

    T_ = T
    nI = T_ // _tile(T_, 256, HALO)
    tb = T_ // nI
    specs_cb = DNW

    def body_wrap():
        def body(dy_ref, o_ref, z_ref, g_ref, do_ref, dz_ref, dg_ref):
            i = pl.program_id(0)
            d_o, d_z, d_g = fn(0, i, dy_ref[...], o_ref[...], z_ref[...], g_ref[...])
            do_ref[...] = d_o
            dz_ref[...] = d_z.astype(dz_ref.dtype)

            @pl.when(i == 0)
            def _():
                dg_ref[...] = d_g

            @pl.when(i > 0)
            def _():
                dg_ref[...] += d_g

        return pl.pallas_call(
            body, name="gnorm_bwd", grid=(nI,),
            in_specs=[pl.BlockSpec((tb, specs_cb), lambda i: (i, y_coff)), pl.BlockSpec((tb, specs_cb), lambda i: (i, 0)),
                      pl.BlockSpec((tb, specs_cb), lambda i: (i, z_coff)), pl.BlockSpec((1, specs_cb), lambda i: (0, 0))],
            out_specs=[pl.BlockSpec((tb, specs_cb), lambda i: (i, 0)), pl.BlockSpec((tb, specs_cb), lambda i: (i, 0)),
                       pl.BlockSpec((HALO, HEAD_DIM), lambda i: (0, 0))],
            out_shape=[jax.ShapeDtypeStruct((T_, DNW), F32), jax.ShapeDtypeStruct((T_, DNW), BF16),
                       jax.ShapeDtypeStruct((HALO, HEAD_DIM), F32)],
            compiler_params=pltpu.CompilerParams(dimension_semantics=("arbitrary",), vmem_limit_bytes=VMEM_LIMIT),
        )(dymix, o, proj, gdn_t)

    return body_wrap()


def _ffn_fwd(up_g, up_v, w_g, w_v, cb):
    T, F = up_g.shape

    def fn(j, i, ug, uv, wg, wv):
        cg = _own(_conv_causal(ug, wg))
        cv = _own(_conv_causal(uv, wv))
        return (cg * _sigmoid(cg) * cv,)

    return _tiled(fn, T=T, C=F, ins=[("ext", up_g, None), ("ext", up_v, None), ("row", w_g, None), ("row", w_v, None)],
                  out_dtypes=[BF16], tb=1024, cb=cb, name="ffn_fwd")[0]


def _ffn_bwd(dact, up_g, up_v, w_g, w_v, cb):
    T, F = up_g.shape
    K = w_g.shape[0]

    def fn(j, i, da, ug, uv, wg, wv):
        cg = _conv_causal(ug, wg)
        cv = _conv_causal(uv, wv)
        sg = _sigmoid(cg)
        dgate = da * cv * (sg * (1.0 + cg * (1.0 - sg)))
        dval = da * (cg * sg)
        return (_own(_conv_anti(dgate, wg)), _own(_conv_anti(dval, wv)), _conv_dw(dgate, ug, K), _conv_dw(dval, uv, K))

    return _tiled(fn, T=T, C=F, ins=[("ext", dact, None), ("ext", up_g, None), ("ext", up_v, None), ("row", w_g, None),
                                      ("row", w_v, None)], out_dtypes=[BF16, BF16], acc_rows=[HALO, HALO], tb=1024, cb=cb,
                  name="ffn_bwd")


def _wide(R, Cc, n_f32, unit=HALO):
    cb = Cc if (Cc % LANES or Cc <= 4096) else _tile(Cc, 2048, LANES)
    cap = max(unit, EW_VMEM_BUDGET // (2 * 4 * n_f32 * cb) // unit * unit)
    return _tile(R, cap, unit), cb


def _adamw(w, g, m, v, name, comm=None):
    R, Cc = w.shape
    tb, cb = _wide(R, Cc, 7) if R % HALO == 0 else (R, _tile(Cc, EW_VMEM_BUDGET // (2 * 4 * 7 * R) // LANES * LANES, LANES))
    c1 = 1.0 / (1.0 - ADAM_B1 ** ADAM_STEP)
    c2 = 1.0 / (1.0 - ADAM_B2 ** ADAM_STEP)

    def fn(j, i, wv, gv, mv, vv):
        m2 = ADAM_B1 * mv + (1.0 - ADAM_B1) * gv
        v2 = ADAM_B2 * vv + (1.0 - ADAM_B2) * (gv * gv)
        delta = -ADAM_LR * ((m2 * c1) / (jnp.sqrt(v2 * c2) + ADAM_EPS) + ADAM_WD * wv)
        return delta, m2, v2

    return _tiled(fn, T=R, C=Cc, ins=[("cur", w, None), ("cur", g, None), ("cur", m, None), ("cur", v, None)],
                  out_dtypes=[F32, F32, F32], tb=tb, cb=cb, name=name, comm=comm)


def _join_shards(w4, n_main):
    S4, R, cs = w4.shape
    n_small = S4 * cs - n_main
    assert 0 < n_small <= LANES and n_small <= cs
    tb = _tile(R, 256, 2 * HALO)

    def body(w_ref, main_ref, small_ref):
        for t in range(S4 - 1):
            main_ref[:, t * cs:(t + 1) * cs] = w_ref[t]
        last = w_ref[S4 - 1]
        main_ref[:, (S4 - 1) * cs:] = last[:, :cs - n_small]
        small_ref[...] = jnp.zeros_like(small_ref)
        small_ref[:, :n_small] = last[:, cs - n_small:]

    return pl.pallas_call(
        body, name="join_w_in", grid=(R // tb,), in_specs=[pl.BlockSpec((S4, tb, cs), lambda i: (0, i, 0))],
        out_specs=[pl.BlockSpec((tb, n_main), lambda i: (i, 0)), pl.BlockSpec((tb, LANES), lambda i: (i, 0))],
        out_shape=[jax.ShapeDtypeStruct((R, n_main), w4.dtype), jax.ShapeDtypeStruct((R, LANES), w4.dtype)],
        compiler_params=pltpu.CompilerParams(dimension_semantics=("parallel",), vmem_limit_bytes=VMEM_LIMIT))(w4)


def _split_shards(main, small, cs):
    R, n_main = main.shape
    n_small = 4 * cs - n_main
    tb = _tile(R, 256, HALO)

    def body(main_ref, small_ref, out_ref):
        for t in range(3):
            out_ref[t] = main_ref[:, t * cs:(t + 1) * cs]
        out_ref[3, :, :cs - n_small] = main_ref[:, 3 * cs:]
        out_ref[3, :, cs - n_small:] = small_ref[:, :n_small]

    return pl.pallas_call(
        body, name="split_g_in", grid=(R // tb,),
        in_specs=[pl.BlockSpec((tb, n_main), lambda i: (i, 0)), pl.BlockSpec((tb, LANES), lambda i: (i, 0))],
        out_specs=pl.BlockSpec((4, tb, cs), lambda i: (0, i, 0)), out_shape=jax.ShapeDtypeStruct((4, R, cs), main.dtype),
        compiler_params=pltpu.CompilerParams(dimension_semantics=("parallel",), vmem_limit_bytes=VMEM_LIMIT))(main, small)


def _sum_stack(st, name):
    S, R, Cc = st.shape
    cb = _tile(Cc, 512, LANES) if Cc % LANES == 0 else Cc

    def fn(j, i, sv):
        t = sv[0]
        for s in range(1, S):
            t = t + sv[s]
        return (t,)

    return _tiled(fn, T=R, C=Cc, ins=[("stack", st, None)], out_dtypes=[F32], cb=cb, name=name)[0]


ANY = pl.BlockSpec(memory_space=pl.ANY)


def _place():
    x, y, c = lax.axis_index("x"), lax.axis_index("y"), lax.axis_index("c")
    return x, y, c, 2 * x + y


def _chip_dev(s, c):
    return (s // 2, s % 2, c)


class _Comm:
    def __init__(self, ins, out_shapes, sems, start, wait, aliases=None):
        self.ins, self.out_shapes, self.sems = list(ins), list(out_shapes), list(sems)
        self.start, self.wait, self.aliases = start, wait, dict(aliases or {})


def _merge(*comms):
    offs, i, o, s = [], 0, 0, 0
    for cm in comms:
        offs.append((i, o, s))
        i, o, s = i + len(cm.ins), o + len(cm.out_shapes), s + len(cm.sems)

    def part(refs, k, cm):
        i0, o0, s0 = offs[k]
        return refs[0][i0:i0 + len(cm.ins)], refs[1][o0:o0 + len(cm.out_shapes)], refs[2][s0:s0 + len(cm.sems)]

    def start(*refs):
        for k, cm in enumerate(comms):
            cm.start(*part(refs, k, cm))

    def wait(*refs):
        for k, cm in enumerate(comms):
            cm.wait(*part(refs, k, cm))

    aliases = {}
    for k, cm in enumerate(comms):
        for a, b in cm.aliases.items():
            aliases[offs[k][0] + a] = offs[k][1] + b
    return _Comm([a for cm in comms for a in cm.ins], [a for cm in comms for a in cm.out_shapes],
                 [a for cm in comms for a in cm.sems], start, wait, aliases)


def _call(body, *, name, grid, in_specs, out_specs, out_shape, scratch_shapes, semantics, args, comm=None, io_aliases=None):
    if comm is None:
        outs = pl.pallas_call(
            body, name=name, grid=grid, in_specs=in_specs, out_specs=out_specs, out_shape=out_shape,
            scratch_shapes=list(scratch_shapes), input_output_aliases=dict(io_aliases or {}),
            compiler_params=pltpu.CompilerParams(dimension_semantics=semantics, vmem_limit_bytes=VMEM_LIMIT))(*args)
        return list(outs), []
    assert not io_aliases
    n_in, n_out, n_scr = len(in_specs), len(out_specs), len(scratch_shapes)
    ci, co = len(comm.ins), len(comm.out_shapes)

    def wrapped(*refs):
        r = 0
        ins, r = refs[r:r + n_in], r + n_in
        cins, r = refs[r:r + ci], r + ci
        outs, r = refs[r:r + n_out], r + n_out
        couts, r = refs[r:r + co], r + co
        scr, r = refs[r:r + n_scr], r + n_scr
        csems = refs[r:]
        ids = [pl.program_id(a) for a in range(len(grid))]
        first, last = ids[0] == 0, ids[0] == grid[0] - 1
        for a in range(1, len(grid)):
            first = jnp.logical_and(first, ids[a] == 0)
            last = jnp.logical_and(last, ids[a] == grid[a] - 1)

        @pl.when(first)
        def _():
            comm.start(cins, couts, csems)

        body(*ins, *outs, *scr)

        @pl.when(last)
        def _():
            comm.wait(cins, couts, csems)

    outs = pl.pallas_call(
        wrapped, name=name, grid=grid, in_specs=list(in_specs) + [ANY] * ci, out_specs=list(out_specs) + [ANY] * co,
        out_shape=list(out_shape) + comm.out_shapes, scratch_shapes=list(scratch_shapes) + comm.sems,
        input_output_aliases={n_in + a: n_out + b for a, b in comm.aliases.items()},
        compiler_params=pltpu.CompilerParams(dimension_semantics=("arbitrary",) * len(grid), vmem_limit_bytes=VMEM_LIMIT),
    )(*args, *comm.ins)
    return list(outs[:n_out]), list(outs[n_out:])


def _run_comm(comm, name):
    ci, co = len(comm.ins), len(comm.out_shapes)

    def body(*refs):
        cins, couts, csems = refs[:ci], refs[ci:ci + co], refs[ci + co:]
        comm.start(cins, couts, csems)
        comm.wait(cins, couts, csems)

    outs = pl.pallas_call(body, name=name, in_specs=[ANY] * ci, out_specs=[ANY] * co, out_shape=comm.out_shapes,
                          scratch_shapes=comm.sems, input_output_aliases=comm.aliases)(*comm.ins)
    return list(outs)


def _ag_comm(shard, land=None, q=0, nq=1):
    two, R2, Cc = shard.shape
    rows = pl.ds(q * (R2 // nq), R2 // nq)
    DMA = pltpu.SemaphoreType.DMA

    def copies(ins, outs, sems, which):
        sh, out = ins[0], outs[0]
        send1, recv1, send2, recv2, send0, recv0 = sems
        x, y, c, s = _place()
        sib = (x, y, 1 - c)
        rc = pltpu.make_async_remote_copy
        if which == "first":
            return [rc(sh.at[c, rows], out.at[s, c, rows], send1.at[m - 1], recv1.at[m - 1],
                       device_id=_chip_dev(s ^ m, c), device_id_type=MESH) for m in range(1, 4)]
        if which == "own":
            return [rc(sh.at[h, rows], out.at[s, h, rows], send0.at[h], recv0.at[h], device_id=sib, device_id_type=MESH)
                    for h in range(2)]
        if which == "landed":
            return [rc(sh.at[c, rows], out.at[s ^ m, c, rows], send1.at[m - 1], recv1.at[m - 1], device_id=sib,
                       device_id_type=MESH) for m in range(1, 4)]
        half = c if which == "passed" else 1 - c
        return [rc(out.at[s ^ m, half, rows], out.at[s ^ m, half, rows], send2.at[m - 1], recv2.at[m - 1], device_id=sib,
                   device_id_type=MESH) for m in range(1, 4)]

    def start(ins, outs, sems):
        for cp in copies(ins, outs, sems, "first") + copies(ins, outs, sems, "own"):
            cp.start()

    def wait(ins, outs, sems):
        passed = copies(ins, outs, sems, "passed")
        for lan, pas in zip(copies(ins, outs, sems, "landed"), passed):
            lan.wait_recv()
            pas.start()
        for cp in copies(ins, outs, sems, "handed"):
            cp.wait_recv()
        for cp in copies(ins, outs, sems, "own"):
            cp.wait()
        for cp in copies(ins, outs, sems, "first") + passed:
            cp.wait_send()

    return _Comm([shard] + ([land] if land is not None else []), [jax.ShapeDtypeStruct((4, two, R2, Cc), shard.dtype)],
                 [DMA((3,)), DMA((3,)), DMA((3,)), DMA((3,)), DMA((2,)), DMA((2,))], start, wait,
                 {1: 0} if land is not None else None)


def _ag_relay_comm(shard):
    two, R2, Cc = shard.shape
    lo, hi = pl.ds(0, R2 // 2), pl.ds(R2 // 2, R2 // 2)
    DMA = pltpu.SemaphoreType.DMA

    def copies(ins, outs, sems, which):
        sh, out = ins[0], outs[0]
        send1, recv1, sendr, recvr, send2, recv2, send0, recv0 = sems
        x, y, c, s = _place()
        sib = (x, y, 1 - c)
        nbr = lambda m: _chip_dev(s ^ m, c)
        rc = functools.partial(pltpu.make_async_remote_copy, device_id_type=MESH)
        if which == "first":
            return [rc(sh.at[c], out.at[s, c], send1.at[m - 1], recv1.at[m - 1], device_id=nbr(m)) for m in (1, 2)]
        if which == "landed":
            return [rc(sh.at[c], out.at[s ^ m, c], send1.at[m - 1], recv1.at[m - 1], device_id=sib) for m in (1, 2)]
        if which == "relay":
            return [rc(out.at[s ^ 2, c, lo], out.at[s ^ 2, c, lo], sendr.at[0], recvr.at[0], device_id=nbr(1)),
                    rc(out.at[s ^ 1, c, hi], out.at[s ^ 1, c, hi], sendr.at[1], recvr.at[1], device_id=nbr(2))]
        if which == "relayed":
            return [rc(out.at[s ^ 3, c, lo], out.at[s ^ 3, c, lo], sendr.at[0], recvr.at[0], device_id=sib),
                    rc(out.at[s ^ 3, c, hi], out.at[s ^ 3, c, hi], sendr.at[1], recvr.at[1], device_id=sib)]
        if which == "own":
            return [rc(sh.at[h], out.at[s, h], send0.at[h], recv0.at[h], device_id=sib) for h in range(2)]
        half = c if which == "passed" else 1 - c
        return [rc(out.at[s ^ m, half], out.at[s ^ m, half], send2.at[m - 1], recv2.at[m - 1], device_id=sib)
                for m in range(1, 4)]

    def start(ins, outs, sems):
        for cp in copies(ins, outs, sems, "first") + copies(ins, outs, sems, "own"):
            cp.start()

    def wait(ins, outs, sems):
        landed, relay = copies(ins, outs, sems, "landed"), copies(ins, outs, sems, "relay")
        passed = copies(ins, outs, sems, "passed")
        landed[1].wait_recv()
        relay[0].start()
        passed[1].start()
        landed[0].wait_recv()
        relay[1].start()
        passed[0].start()
        for cp in copies(ins, outs, sems, "relayed"):
            cp.wait_recv()
        passed[2].start()
        for cp in copies(ins, outs, sems, "handed"):
            cp.wait_recv()
        for cp in copies(ins, outs, sems, "own"):
            cp.wait()
        for cp in copies(ins, outs, sems, "first") + relay + passed:
            cp.wait_send()

    return _Comm([shard], [jax.ShapeDtypeStruct((4, two, R2, Cc), shard.dtype)],
                 [DMA((2,)), DMA((2,)), DMA((2,)), DMA((2,)), DMA((3,)), DMA((3,)), DMA((2,)), DMA((2,))], start, wait)


def _a2a_comm(S1, q=0, nq=1, land=None, cnt=1):
    S4, R2, Cc = S1.shape
    rows = pl.ds(q * (R2 // nq), cnt * (R2 // nq))
    DMA = pltpu.SemaphoreType.DMA

    def copies(ins, outs, sems):
        x, y, c, s = _place()
        return [pltpu.make_async_remote_copy(ins[0].at[s ^ m, rows], outs[0].at[m - 1, rows], sems[0].at[m - 1],
                                             sems[1].at[m - 1], device_id=_chip_dev(s ^ m, c), device_id_type=MESH)
                for m in range(1, 4)]

    def start(ins, outs, sems):
        for cp in copies(ins, outs, sems):
            cp.start()

    def wait(ins, outs, sems):
        for cp in copies(ins, outs, sems):
            cp.wait()

    return _Comm([S1] + ([land] if land is not None else []), [jax.ShapeDtypeStruct((3, R2, Cc), S1.dtype)],
                 [DMA((3,)), DMA((3,))], start, wait, {1: 0} if land is not None else None)


def _halves(G):
    return G.reshape(G.shape[0], 2, G.shape[1] // 2, G.shape[2])


def _swap_comm(piece):
    n, two, R2, Cc = piece.shape
    DMA = pltpu.SemaphoreType.DMA

    def copies(ins, outs, sems):
        x, y, c, s = _place()
        return [pltpu.make_async_remote_copy(ins[0].at[t, 1 - c], outs[0].at[t], sems[0].at[t], sems[1].at[t],
                                             device_id=(x, y, 1 - c), device_id_type=MESH) for t in range(n)]

    def start(ins, outs, sems):
        for cp in copies(ins, outs, sems):
            cp.start()

    def wait(ins, outs, sems):
        for cp in copies(ins, outs, sems):
            cp.wait()

    return _Comm([piece], [jax.ShapeDtypeStruct((n, R2, Cc), piece.dtype)], [DMA((n,)), DMA((n,))], start, wait)


def _add_half(pieces, As, cidx, name):
    R2, Cc = pieces[0].shape[2:]
    S4 = sum(pc.shape[0] for pc in pieces)
    tb, cb = _wide(R2, Cc, 3, 2 * HALO)
    nI, nJ = R2 // tb, Cc // cb

    def body(c_ref, g_ref, a_ref, *rest):
        rest[-1][...] = (g_ref[0, 0] + a_ref[0]).astype(BF16)

    out, t0 = None, 0
    for k, (pc, A) in enumerate(zip(pieces, As)):
        grid_spec = pltpu.PrefetchScalarGridSpec(
            num_scalar_prefetch=1, grid=(pc.shape[0], nI, nJ),
            in_specs=[pl.BlockSpec((1, 1, tb, cb), lambda t, i, j, c_ref: (t, c_ref[0], i, j)),
                      pl.BlockSpec((1, tb, cb), lambda t, i, j, c_ref: (t, i, j))] + ([ANY] if k else []),
            out_specs=pl.BlockSpec((tb, cb), lambda t, i, j, c_ref, t0=t0: ((t0 + t) * nI + i, j)))
        out = pl.pallas_call(
            functools.partial(body), name=f"{name}{k}", grid_spec=grid_spec, out_shape=jax.ShapeDtypeStruct((S4 * R2, Cc), BF16),
            input_output_aliases={3: 0} if k else {},
            compiler_params=pltpu.CompilerParams(dimension_semantics=("parallel", "parallel", "parallel"),
                                                 vmem_limit_bytes=VMEM_LIMIT),
        )(*((cidx, pc, A) + ((out,) if k else ())))
        t0 += pc.shape[0]
    return out.reshape(S4, R2, Cc)


def _add_own(S1, B, chip_idx, cidx, name):
    S4, R2, Cc = S1.shape
    tb, cb = _wide(R2, Cc, 3, 2 * HALO)

    def body(s_idx, c_idx, s_ref, b_ref, o_ref):
        o_ref[...] = ((s_ref[0].astype(F32) + b_ref[0].astype(F32)) + b_ref[1].astype(F32)) + b_ref[2].astype(F32)

    grid_spec = pltpu.PrefetchScalarGridSpec(
        num_scalar_prefetch=2, grid=(R2 // tb, Cc // cb),
        in_specs=[pl.BlockSpec((1, tb, cb), lambda i, j, s_idx, c_idx: (s_idx[0], i, j)),
                  pl.BlockSpec((3, tb, cb), lambda i, j, s_idx, c_idx: (0, i, j))],
        out_specs=pl.BlockSpec((None, tb, cb), lambda i, j, s_idx, c_idx: (c_idx[0], i, j)))
    return pl.pallas_call(body, name=name, grid_spec=grid_spec, out_shape=jax.ShapeDtypeStruct((2, R2, Cc), F32),
                          compiler_params=pltpu.CompilerParams(dimension_semantics=("parallel", "parallel"),
                                                               vmem_limit_bytes=VMEM_LIMIT))(chip_idx, cidx, S1, B)


def _fill_comm(Hs):
    def copy(ins, outs, sems):
        x, y, c, s = _place()
        return pltpu.make_async_remote_copy(ins[0].at[c], outs[0].at[c], sems[0], sems[1], device_id=(x, y, 1 - c),
                                            device_id_type=MESH)

    return _Comm([Hs], [jax.ShapeDtypeStruct(Hs.shape, Hs.dtype)], [pltpu.SemaphoreType.DMA, pltpu.SemaphoreType.DMA],
                 lambda *r: copy(*r).start(), lambda *r: copy(*r).wait(), {0: 0})


def _gather_all_comm(buf):
    R, Cc = buf.shape
    DMA = pltpu.SemaphoreType.DMA

    def copies(ins, outs, sems):
        x, y, c, s = _place()
        d = 2 * s + c
        return ([pltpu.make_async_remote_copy(ins[0], outs[0].at[d], sems[0].at[m - 1], sems[1].at[m - 1],
                                              device_id=((d ^ m) // 4, ((d ^ m) // 2) % 2, (d ^ m) % 2), device_id_type=MESH)
                 for m in range(1, 8)], pltpu.make_async_copy(ins[0], outs[0].at[d], sems[2]))

    def start(ins, outs, sems):
        remote, mine = copies(ins, outs, sems)
        for cp in remote + [mine]:
            cp.start()

    def wait(ins, outs, sems):
        remote, mine = copies(ins, outs, sems)
        for cp in remote + [mine]:
            cp.wait()

    return _Comm([buf], [jax.ShapeDtypeStruct((8, R, Cc), buf.dtype)], [DMA((7,)), DMA((7,)), DMA], start, wait)


def _pack_rows(vs):
    flat = jnp.concatenate([v.reshape(-1) for v in vs])
    n = flat.shape[0]
    rows = -(-n // (LANES * 2 * HALO)) * 2 * HALO
    return jnp.pad(flat, (0, rows * LANES - n)).reshape(rows, LANES)


def _unpack_rows(buf, shapes):
    flat = buf.reshape(-1)
    outs, o = [], 0
    for shp in shapes:
        n = 1
        for d in shp:
            n *= d
        outs.append(flat[o:o + n].reshape(shp))
        o += n
    return outs


def kernel(x, p, norm_mix_g, w_in, conv_a_w, conv_qkv_w, a_log, dt_bias, dn_norm_g, w_out, norm_ffn_g, w_up, conv_ffn_w, w_down, norm_ple_g, w_ple_gate, w_ple_proj, final_norm_g, loss_target, m_norm_mix_g, m_w_in, m_conv_a_w, m_conv_qkv_w, m_a_log, m_dt_bias, m_dn_norm_g, m_w_out, m_norm_ffn_g, m_w_up, m_conv_ffn_w, m_w_down, m_norm_ple_g, m_w_ple_gate, m_w_ple_proj, m_final_norm_g, v_norm_mix_g, v_w_in, v_conv_a_w, v_conv_qkv_w, v_a_log, v_dt_bias, v_dn_norm_g, v_w_out, v_norm_ffn_g, v_w_up, v_conv_ffn_w, v_w_down, v_norm_ple_g, v_w_ple_gate, v_w_ple_proj, v_final_norm_g):
    xs = x[0]
    ps = p[0, 0]
    tgt = loss_target[0]
    T, D = xs.shape
    H = a_log.shape[-1]
    DNW = H * HEAD_DIM
    CW = conv_a_w.shape[-1] * 4
    F = w_down.shape[1] * 4
    PD = ps.shape[-1]
    IN_MAIN = 3 * CW + 4 * DNW
    IN_COLS = IN_MAIN + 2 * H
    assert w_in.shape[-1] * 4 == IN_COLS and CW + DNW == D and 2 * H <= LANES
    cb = _tile(min(CW, DNW), 512, LANES)
    while F % cb:
        cb -= LANES
    cidx = lax.axis_index("c").astype(jnp.int32).reshape(1)
    chip = 2 * lax.axis_index("x") + lax.axis_index("y")

    def halves(w):
        sh = w[0].astype(BF16)
        return sh.reshape(2, sh.shape[0] // 2, sh.shape[1])

    def whole(land):
        return land.reshape(4, 2 * land.shape[2], land.shape[3])

    def rows(g4):
        return g4.reshape(4 * g4.shape[1], g4.shape[2])

    conv_shapes = [conv_a_w[0].shape, conv_qkv_w[0].shape, conv_ffn_w[0].shape]
    cpack = _pack_rows([conv_a_w[0], conv_qkv_w[0], conv_ffn_w[0]])
    sh_in, sh_out, sh_up, sh_down, sh_pg, sh_pp = (halves(w) for w in (w_in, w_out, w_up, w_down, w_ple_gate, w_ple_proj))
    l_in, cg = _run_comm(_merge(_ag_relay_comm(sh_in), _ag_comm(cpack.reshape(2, cpack.shape[0] // 2, LANES))), "ag_w_in_conv")
    w_in_main, w_in_small = _join_shards(whole(l_in), IN_MAIN)
    cg = cg.reshape(4, cpack.shape[0], LANES)
    parts = [_unpack_rows(cg[t], conv_shapes) for t in range(4)]
    cw_a = jnp.concatenate([parts[t][0] for t in range(4)], axis=1)
    cw_qkv = jnp.concatenate([parts[t][1] for t in range(4)], axis=1)
    cw_ffn = jnp.concatenate([parts[t][2] for t in range(4)], axis=1)
    cw_q, cw_k, cw_v = cw_qkv[:, :DNW], cw_qkv[:, DNW:2 * DNW], cw_qkv[:, 2 * DNW:]
    cw_fg, cw_fv = cw_ffn[:, :F], cw_ffn[:, F:]
    pad_row = lambda v: jnp.pad(v, ((0, 0), (0, LANES - v.shape[1])))
    a_log_row, dt_row = pad_row(a_log), pad_row(dt_bias)
    gdn_t = jnp.tile(dn_norm_g, (1, H))
    gfin = final_norm_g.reshape(1, D)

    h1 = _rms_fwd(xs, norm_mix_g, "rms1")
    proj, (l_up,) = _mm(h1, w_in_main, mode="nn", out_dtypes=[F32], name="mm_proj", comm=_ag_comm(sh_up, q=0, nq=2))
    small, (l_out,) = _mm(h1, w_in_small, mode="nn", out_dtypes=[F32], name="mm_small", comm=_ag_comm(sh_out, q=0, nq=4))
    ymix = _ga_fwd(proj, cw_a, CW, cb, D)
    nq = 3 * CW // cb
    nd = DNW // cb
    qn, (l_out,) = _qkv_fwd(proj, cw_q, nq, True, DNW, cb, "q_fwd", comm=_ag_comm(sh_out, l_out, q=1, nq=4))
    kn, (l_out,) = _qkv_fwd(proj, cw_k, nq + nd, True, DNW, cb, "k_fwd", comm=_ag_comm(sh_out, l_out, q=2, nq=4))
    vs, (l_out,) = _qkv_fwd(proj, cw_v, nq + 2 * nd, False, DNW, cb, "v_fwd", comm=_ag_comm(sh_out, l_out, q=3, nq=4))
    g, beta = _gb_fwd(small, a_log_row, dt_row, H)
    o, S0, inv_c, (l_up,) = _delta_fwd(qn, kn, vs, g, beta, comm=_ag_comm(sh_up, l_up, q=1, nq=2))
    w_out_f = rows(whole(l_out))
    w_up_4 = whole(l_up)
    z_coff = (3 * CW + 3 * DNW) // DNW
    assert (3 * CW + 3 * DNW) % DNW == 0 and CW % DNW == 0
    ymix = _gnorm_fwd(o, proj, z_coff, gdn_t, DNW, ymix, CW // DNW)
    add = lambda acc, r: (r + acc,)

    def out_epi(acc, xv, gv):
        x1v = xv + acc
        return x1v, x1v * lax.rsqrt(jnp.mean(x1v * x1v, axis=1, keepdims=True) + EPS) * gv

    x1, h2 = _mm(ymix, w_out_f, mode="nn", out_dtypes=[F32, BF16], epi=out_epi, extras=[xs], rows=[norm_ffn_g], name="mm_out")
    up_g, (l_down,) = _mm(h2, w_up_4, mode="nn", b_split=(0, 2), out_dtypes=[F32], name="mm_up_g",
                          comm=_ag_comm(sh_down, q=0, nq=2))
    up_v, (l_down,) = _mm(h2, w_up_4, mode="nn", b_split=(2, 2), out_dtypes=[F32], name="mm_up_v",
                          comm=_ag_comm(sh_down, l_down, q=1, nq=2))
    w_down_f = rows(whole(l_down))
    act = _ffn_fwd(up_g, up_v, cw_fg, cw_fv, cb)
    x2, (l_pg, l_pp) = _mm(act, w_down_f, mode="nn", out_dtypes=[F32], epi=add, extras=[x1], name="mm_down",
                           comm=_merge(_ag_comm(sh_pg), _ag_comm(sh_pp)))
    w_pg_f = rows(whole(l_pg))
    w_pp_4 = whole(l_pp)
    h3 = _rms_fwd(x2, norm_ple_g, "rms3")
    pp = _mm(ps, w_pp_4, mode="nn", b_split=(0, 4), out_dtypes=[F32], name="mm_pp")

    def ple_final_epi(acc, x2v, ppv, tv, gv):
        pg = _sigmoid(acc)
        x3v = x2v + pg * ppv
        r = lax.rsqrt(jnp.mean(x3v * x3v, axis=1, keepdims=True) + EPS)
        xh = x3v * r
        e = xh * gv - tv
        dy = e * (1.0 / D)
        dxh = dy * gv
        dx = r * (dxh - xh * jnp.mean(dxh * xh, axis=1, keepdims=True))
        dg = jnp.sum(dy * xh, axis=0, keepdims=True)
        ls = jnp.sum(e * e, axis=0, keepdims=True) * (0.5 / D)
        return dx, dx * ppv * pg * (1.0 - pg), dx * pg, jnp.concatenate([dg, ls, jnp.zeros((HALO - 2, D), F32)], axis=0)

    dx3, dpg, dpp, fin = _mm(h3, w_pg_f, mode="nn", out_dtypes=[F32, BF16, BF16], parts=1, epi=ple_final_epi,
                             extras=[x2, pp, tgt], rows=[gfin], name="mm_pg_final")
    fin = jnp.sum(fin.reshape(-1, HALO, D), axis=0)
    loss = lax.psum(jnp.sum(fin[1]), ("x", "y", "c"))
    d_gfin = fin[0:1]
    def split_rows(dW):
        return dW.reshape(4, dW.shape[0] // 4, dW.shape[1])

    chip_idx = chip.astype(jnp.int32).reshape(1)
    own_sum = lambda S1, B, name: _add_own(S1, B, chip_idx, cidx, "rs_" + name + "_sum")

    dW_pp = _mm(ps, dpp, mode="tn", out_split=4, out_dtypes=[F32], name="mm_dw_pp")
    dW_pg = _mm(h3, dpg, mode="tn", out_dtypes=[F32], name="mm_dw_pg")
    P_pp, P_pg = _halves(dW_pp), _halves(split_rows(dW_pg))
    def rms_bwd_epi(acc, xv, dr, gv):
        dxv, dg = _rms_bwd_math(acc, xv, gv)
        return dr + dxv, dr + dxv, _row0(dg)

    (dx2, dx2_b, d_gple), (A_pp, A_pg) = _mm(dpg, w_pg_f, mode="nt", out_dtypes=[F32, BF16], parts=1, epi=rms_bwd_epi,
                                             extras=[x2, dx3], rows=[norm_ple_g], name="mm_dh3_rms",
                                             comm=_merge(_swap_comm(P_pp), _swap_comm(P_pg)))
    d_gple = jnp.sum(d_gple.reshape(-1, HALO, D), axis=0)
    S_pp = _add_half([P_pp], [A_pp], cidx, "rs_w_pp_add")
    S_pg = _add_half([P_pg], [A_pg], cidx, "rs_w_pg_add")
    dW_down, (B_pp, B_pg) = _mm(act, dx2_b, mode="tn", out_dtypes=[F32], name="mm_dw_down",
                                comm=_merge(_a2a_comm(S_pp), _a2a_comm(S_pg)))
    P_down = _halves(split_rows(dW_down))
    dact, (A_down, F_pp, F_pg) = _mm(dx2_b, w_down_f, mode="nt", out_dtypes=[F32], name="mm_dact", comm=_merge(
        _swap_comm(P_down), _fill_comm(own_sum(S_pp, B_pp, "w_pp")), _fill_comm(own_sum(S_pg, B_pg, "w_pg"))))
    S_down = _add_half([P_down], [A_down], cidx, "rs_w_down_add")
    dup_g, dup_v, dcw_fg, dcw_fv = _ffn_bwd(dact, up_g, up_v, cw_fg, cw_fv, cb)
    dW_up_g, (B_down,) = _mm(h2, dup_g, mode="tn", out_split=2, out_dtypes=[F32], name="mm_dw_up_g", comm=_a2a_comm(S_down))
    P_ug = _halves(dW_up_g)
    dW_up_v, (A_ug, F_down) = _mm(h2, dup_v, mode="tn", out_split=2, out_dtypes=[F32], name="mm_dw_up_v",
                                  comm=_merge(_swap_comm(P_ug), _fill_comm(own_sum(S_down, B_down, "w_down"))))
    P_uv = _halves(dW_up_v)
    dh2, (A_uv,) = _mm(dup_g, w_up_4, mode="nt", b_split=(0, 2), out_dtypes=[F32], name="mm_dh2_g", comm=_swap_comm(P_uv))
    S_up = _add_half([P_ug, P_uv], [A_ug, A_uv], cidx, "rs_w_up_add")
    dh2, (B_up,) = _mm(dup_v, w_up_4, mode="nt", b_split=(2, 2), out_dtypes=[F32], epi=add, extras=[dh2], name="mm_dh2_v",
                       comm=_a2a_comm(S_up, 0, 2))
    dx1, dx1_b, d_gffn = _rms_bwd(dh2, x1, norm_ffn_g, dx2, "rms2_bwd")
    P_out = _halves(split_rows(_mm(ymix, dx1_b, mode="tn", out_dtypes=[F32], name="mm_dw_out")))
    dymix, (A_out,) = _mm(dx1_b, w_out_f, mode="nt", out_dtypes=[F32], name="mm_dymix", comm=_swap_comm(P_out))
    S_out = _add_half([P_out], [A_out], cidx, "rs_w_out_add")
    dax, dab, dac, dcw_a = _ga_bwd(dymix, proj, cw_a, CW, cb)
    do, dz, d_gdn = _gnorm_bwd(dymix, CW // DNW, o, proj, z_coff, gdn_t, DNW)
    dqn, dkn, dvs, dgB, dbB, (B_up, B_out) = _delta_bwd(qn, kn, vs, g, beta, S0, inv_c, do,
                                                        comm=_merge(_a2a_comm(S_up, 1, 2, B_up), _a2a_comm(S_out)))
    dq_pre, dcw_q = _qkv_bwd(dqn, proj, cw_q, nq, True, DNW, cb, "q_bwd")
    dk_pre, dcw_k = _qkv_bwd(dkn, proj, cw_k, nq + nd, True, DNW, cb, "k_bwd")
    dv_pre, dcw_v = _qkv_bwd(dvs, proj, cw_v, nq + 2 * nd, False, DNW, cb, "v_bwd")
    dsmall, d_ab = _gb_bwd(dgB, dbB, small, g, beta, a_log_row, dt_row, H)
    dproj = jnp.concatenate([dax, dab, dac, dq_pre, dk_pre, dv_pre, dz], axis=1)
    dW_in_main, (F_up, F_out) = _mm(h1, dproj, mode="tn", out_dtypes=[F32], name="mm_dw_in", comm=_merge(
        _fill_comm(own_sum(S_up, B_up, "w_up")), _fill_comm(own_sum(S_out, B_out, "w_out"))))
    dW_in_small = _mm(h1, dsmall, mode="tn", out_dtypes=[F32], name="mm_dw_in_small")
    def update(Hf, w, m, v, name):
        gr = Hf.reshape(2 * Hf.shape[1], Hf.shape[2])
        if gr.shape[1] % LANES == 0:
            delta, m2, v2 = _adamw(w[0], gr, m[0], v[0], "adamw_" + name)
            return gr[None], delta[None], m2[None], v2[None]
        tr = jnp.transpose
        grt = tr(gr)
        delta, m2, v2 = _adamw(tr(w[0]), grt, tr(m[0]), tr(v[0]), "adamw_" + name)
        return tr(grt)[None], tr(delta)[None], tr(m2)[None], tr(v2)[None]

    P_in = _halves(_split_shards(dW_in_main, dW_in_small, IN_COLS // 4))
    (A_in,) = _run_comm(_swap_comm(P_in), "rs_w_in_swap")
    S_in = _add_half([P_in], [A_in], cidx, "rs_w_in_add")
    dh1, (B_in,) = _mm(dproj, w_in_main, mode="nt", out_dtypes=[F32], name="mm_dh1", comm=_a2a_comm(S_in))

    def rms1_epi(acc, dhv, xv, dr, gv):
        dxv, dg = _rms_bwd_math(acc + dhv, xv, gv)
        return dr + dxv, _row0(dg)

    dx, d_gmix = _mm(dsmall, w_in_small, mode="nt", out_dtypes=[F32], parts=1, epi=rms1_epi, extras=[dh1, xs, dx1],
                     rows=[norm_mix_g], name="mm_dh1_small_rms")
    d_gmix = jnp.sum(d_gmix.reshape(-1, HALO, D), axis=0)

    small_grads = [d_gmix[0:1], dcw_a[:cw_a.shape[0]], jnp.concatenate([dcw_q, dcw_k, dcw_v], axis=1)[:cw_qkv.shape[0]],
                   d_ab[0:1, :H], d_ab[1:2, :H], d_gdn[0:1], d_gffn[0:1],
                   jnp.concatenate([dcw_fg, dcw_fv], axis=1)[:cw_ffn.shape[0]], d_gple[0:1], d_gfin]
    small_shapes = [v.shape for v in small_grads]
    gpack = _pack_rows(small_grads)
    F_in, g8 = _run_comm(_merge(_fill_comm(own_sum(S_in, B_in, "w_in")), _gather_all_comm(gpack)), "rs_w_in_gather_small")
    big = {
        "w_in": update(F_in, w_in, m_w_in, v_w_in, "w_in"),
        "w_out": update(F_out, w_out, m_w_out, v_w_out, "w_out"),
        "w_up": update(F_up, w_up, m_w_up, v_w_up, "w_up"),
        "w_down": update(F_down, w_down, m_w_down, v_w_down, "w_down"),
        "w_ple_gate": update(F_pg, w_ple_gate, m_w_ple_gate, v_w_ple_gate, "w_pg"),
        "w_ple_proj": update(F_pp, w_ple_proj, m_w_ple_proj, v_w_ple_proj, "w_pp"),
    }

    gsum = _sum_stack(g8, "sum_small")
    (g_gmix, g_cwa, g_cwqkv, g_alog, g_dt, g_gdn, g_gffn, g_cwffn, g_gple, g_gfin) = _unpack_rows(gsum, small_shapes)

    def my_cols(v):
        Cc = v.shape[1] // 4
        return lax.dynamic_slice_in_dim(v, chip * Cc, Cc, axis=1)

    g_small = [g_gmix, my_cols(g_cwa), my_cols(g_cwqkv), g_alog, g_dt, g_gdn, g_gffn, my_cols(g_cwffn), g_gple, g_gfin]
    w_small = [norm_mix_g, conv_a_w[0], conv_qkv_w[0], a_log, dt_bias, dn_norm_g, norm_ffn_g, conv_ffn_w[0], norm_ple_g, gfin]
    m_small = [m_norm_mix_g, m_conv_a_w[0], m_conv_qkv_w[0], m_a_log, m_dt_bias, m_dn_norm_g, m_norm_ffn_g, m_conv_ffn_w[0],
               m_norm_ple_g, m_final_norm_g.reshape(1, D)]
    v_small = [v_norm_mix_g, v_conv_a_w[0], v_conv_qkv_w[0], v_a_log, v_dt_bias, v_dn_norm_g, v_norm_ffn_g, v_conv_ffn_w[0],
               v_norm_ple_g, v_final_norm_g.reshape(1, D)]
    shp = [v.shape for v in w_small]
    ds_, ms_, vs_ = _adamw(_pack_rows(w_small), _pack_rows(g_small), _pack_rows(m_small), _pack_rows(v_small), "adamw_small")
    out_shapes = [norm_mix_g.shape, conv_a_w.shape, conv_qkv_w.shape, a_log.shape, dt_bias.shape, dn_norm_g.shape,
                  norm_ffn_g.shape, conv_ffn_w.shape, norm_ple_g.shape, final_norm_g.shape]
    rs = lambda vals: [v.reshape(s) for v, s in zip(vals, out_shapes)]
    sg, sd_, sm_, sv_ = rs(g_small), rs(_unpack_rows(ds_, shp)), rs(_unpack_rows(ms_, shp)), rs(_unpack_rows(vs_, shp))
    names_small = ["norm_mix_g", "conv_a_w", "conv_qkv_w", "a_log", "dt_bias", "dn_norm_g", "norm_ffn_g", "conv_ffn_w",
                   "norm_ple_g", "final_norm_g"]
    res = {n: (sg[i], sd_[i], sm_[i], sv_[i]) for i, n in enumerate(names_small)}
    res.update(big)
    order = ["norm_mix_g", "w_in", "conv_a_w", "conv_qkv_w", "a_log", "dt_bias", "dn_norm_g", "w_out", "norm_ffn_g", "w_up",
             "conv_ffn_w", "w_down", "norm_ple_g", "w_ple_gate", "w_ple_proj", "final_norm_g"]
    return (loss, dx[None], *[res[n][0] for n in order], *[res[n][1] for n in order], *[res[n][2] for n in order],
            *[res[n][3] for n in order])
```

```python
import functools

import jax
import jax.numpy as jnp
from jax import lax
from jax.experimental import pallas as pl
from jax.experimental.pallas import tpu as pltpu

F32 = jnp.float32
BF16 = jnp.bfloat16
LANES = 128
HALO = 8
HEAD_DIM = 128
CHUNK = 64
EPS = 1e-6
VMEM_LIMIT = 56 * 1024 * 1024
MM_VMEM_BUDGET = 40 * 1024 * 1024
MM_STEP_BYTES = 1 << 20
EW_VMEM_BUDGET = 28 * 1024 * 1024
MESH = pl.DeviceIdType.MESH

ADAM_LR, ADAM_B1, ADAM_B2, ADAM_EPS, ADAM_WD, ADAM_STEP = 0.001, 0.9, 0.999, 1e-08, 0.01, 10


def _tile(n, cap, unit):
    if n <= cap:
        return n
    d = (cap // unit) * unit
    while d >= unit:
        if n % d == 0:
            return d
        d -= unit
    raise ValueError(f"no tile for {n} (cap {cap}, unit {unit})")


def _sigmoid(x):
    return 1.0 / (1.0 + jnp.exp(-x))


def _divisors(n, cap):
    ds = [d for d in range(cap // LANES * LANES, 0, -LANES) if n % d == 0]
    return [n] if (n <= cap or not ds) else ds


def _mm_tiles(M, N, K, n_unit, k_unit, a_bytes, n_blocks_mn, a_transposed, tn_full=False):
    best = None
    for tm in _divisors(M, 1536):
        for tn in ([N] if tn_full else _divisors(n_unit, 1536)):
            for tk in _divisors(k_unit, 4096):
                nk = K // tk
                vmem = 2 * tm * tk * a_bytes + 2 * tk * tn * 2 + 2 * 4 * tm * tn * n_blocks_mn + (4 * tm * tn if nk > 1 else 0)
                if vmem > MM_VMEM_BUDGET:
                    continue
                steps = (M // tm) * (N // tn) * nk
                b_reads = 1 if (nk == 1 and N == tn) else M // tm
                cost = (M * K * a_bytes * (N // tn if nk > 1 else 1) + K * N * 2 * b_reads + 4 * M * N * n_blocks_mn
                        + (8 * M * N * nk // 3 if nk > 1 else 0) + steps * MM_STEP_BYTES
                        + (2 * steps * tm * tk if a_transposed else 0))
                if best is None or cost < best[0]:
                    best = (cost, tm, tn, tk)
    return best[1:]


def _mm(a, b, *, mode, out_dtypes, name, epi=None, extras=(), comm=None, b_split=None, out_split=None, rows=(), parts=0):
    if b_split is not None:
        lo, ns = b_split
        Rb, Cb = b.shape[1], b.shape[2]
    if mode == "nn":
        (M, K), N = a.shape, (ns * Cb if b_split else b.shape[1])
    elif mode == "nt":
        (M, K), N = a.shape, (Rb if b_split else b.shape[0])
    else:
        (K, M), N = a.shape, b.shape[1]
    n_ex, n_out = len(extras), len(out_dtypes)
    n_unit = Cb if (b_split and mode == "nn") else (N // out_split if out_split else N)
    n_rows = len(rows)
    assert not (n_rows and (b_split or out_split))
    k_unit = Cb if (b_split and mode == "nt") else K
    mn_blocks = (sum(e.dtype.itemsize for e in extras) + sum(jnp.dtype(d).itemsize for d in out_dtypes)) / 4
    tm, tn, tk = _mm_tiles(M, N, K, n_unit, k_unit, a.dtype.itemsize, mn_blocks, mode == "tn", tn_full=bool(n_rows))
    nk = K // tk
    a_spec = pl.BlockSpec((tk, tm), lambda i, j, k: (k, i)) if mode == "tn" else pl.BlockSpec((tm, tk), lambda i, j, k: (i, k))
    if b_split and mode == "nn":
        nb = Cb // tn
        b_spec = pl.BlockSpec((None, tk, tn), lambda i, j, k: (lo + j // nb, k, j % nb))
    elif b_split:
        nb = Cb // tk
        b_spec = pl.BlockSpec((None, tn, tk), lambda i, j, k: (lo + k // nb, j, k % nb))
    else:
        b_spec = pl.BlockSpec((tn, tk), lambda i, j, k: (j, k)) if mode == "nt" else pl.BlockSpec((tk, tn), lambda i, j, k: (k, j))
    mn_spec = pl.BlockSpec((tm, tn), lambda i, j, k: (i, j))
    out_shapes = [jax.ShapeDtypeStruct((M, N), dt) for dt in out_dtypes] + [jax.ShapeDtypeStruct((M // tm * HALO, N), F32)] * parts
    out_specs = [mn_spec] * n_out + [pl.BlockSpec((HALO, tn), lambda i, j, k: (i, j))] * parts
    if out_split:
        assert n_ex == 0 and n_out == 1
        nbo = (N // out_split) // tn
        out_specs = [pl.BlockSpec((None, tm, tn), lambda i, j, k: (j // nbo, i, j % nbo))]
        out_shapes = [jax.ShapeDtypeStruct((out_split, M, N // out_split), out_dtypes[0])]
    dims = {"nn": (((1,), (0,)), ((), ())), "nt": (((1,), (1,)), ((), ())), "tn": (((0,), (0,)), ((), ()))}[mode]

    def body(*refs):
        a_ref, b_ref = refs[0], refs[1]
        ex_refs = refs[2:2 + n_ex + n_rows]
        out_refs = refs[2 + n_ex + n_rows:2 + n_ex + n_rows + n_out + parts]
        part = lax.dot_general(a_ref[...].astype(BF16), b_ref[...].astype(BF16), dims, preferred_element_type=F32)

        def finish(acc):
            outs = (acc,) if epi is None else epi(acc, *[r[...] for r in ex_refs])
            for r, o in zip(out_refs, outs):
                r[...] = o.astype(r.dtype)

        if nk == 1:
            finish(part)
            return
        acc_ref = refs[-1]
        k = pl.program_id(2)

        @pl.when(k == 0)
        def _():
            acc_ref[...] = part

        @pl.when(jnp.logical_and(k > 0, k < nk - 1))
        def _():
            acc_ref[...] += part

        @pl.when(k == nk - 1)
        def _():
            finish(acc_ref[...] + part)

    outs, comm_outs = _call(
        body, name=name, grid=(M // tm, N // tn, nk),
        in_specs=[a_spec, b_spec] + [mn_spec] * n_ex + [pl.BlockSpec((1, tn), lambda i, j, k: (0, j))] * n_rows,
        out_specs=out_specs,
        out_shape=out_shapes,
        scratch_shapes=[pltpu.VMEM((tm, tn), F32)] if nk > 1 else [],
        semantics=("parallel", "parallel", "arbitrary"), args=(a, b, *extras, *rows), comm=comm)
    res = outs[0] if n_out + parts == 1 else outs
    return res if comm is None else (res, comm_outs)


def _tiled(fn, *, T, C, ins, out_dtypes=(), acc_rows=(), tb=None, cb=512, name, comm=None, into=None):
    tb = _tile(T, tb or (1024 if cb <= 512 else 512 if cb <= 1024 else 256), HALO)
    nI, nJ = T // tb, C // cb
    hb, nH = tb // HALO, T // HALO
    specs, args, kinds = [], [], []
    for kind, arr, cmap in ins:
        cm = cmap if cmap is not None else (lambda j: j)
        kinds.append(kind)
        if kind == "cur":
            specs.append(pl.BlockSpec((tb, cb), lambda j, i, cm=cm: (i, cm(j))))
            args.append(arr)
        elif kind == "ext":
            specs.append(pl.BlockSpec((HALO, cb), lambda j, i, cm=cm: (jnp.maximum(i * hb - 1, 0), cm(j))))
            specs.append(pl.BlockSpec((tb, cb), lambda j, i, cm=cm: (i, cm(j))))
            specs.append(pl.BlockSpec((HALO, cb), lambda j, i, cm=cm: (jnp.minimum((i + 1) * hb, nH - 1), cm(j))))
            args += [arr, arr, arr]
        elif kind == "row":
            specs.append(pl.BlockSpec((arr.shape[0], cb), lambda j, i, cm=cm: (0, cm(j))))
            args.append(arr)
        elif kind == "stack":
            specs.append(pl.BlockSpec((arr.shape[0], tb, cb), lambda j, i, cm=cm: (0, i, cm(j))))
            args.append(arr)
        else:
            raise ValueError(kind)
    n_in = len(args)
    n_out, n_acc = len(out_dtypes), len(acc_rows)

    def body(*refs):
        j, i = pl.program_id(0), pl.program_id(1)
        vals, r = [], 0
        for kind in kinds:
            if kind == "ext":
                prev = jnp.where(i == 0, 0.0, refs[r][...].astype(F32))
                cur = refs[r + 1][...].astype(F32)
                nxt = jnp.where(i == nI - 1, 0.0, refs[r + 2][...].astype(F32))
                vals.append(jnp.concatenate([prev, cur, nxt], axis=0))
                r += 3
            else:
                vals.append(refs[r][...])
                r += 1
        res = fn(j, i, *vals)
        for ref, o in zip(refs[n_in:n_in + n_out], res[:n_out]):
            ref[...] = o.astype(ref.dtype)
        for ref, o in zip(refs[n_in + n_out:], res[n_out:]):
            @pl.when(i == 0)
            def _(ref=ref, o=o):
                ref[...] = o

            @pl.when(i > 0)
            def _(ref=ref, o=o):
                ref[...] += o

    out_specs = [pl.BlockSpec((tb, cb), lambda j, i: (i, j))] * n_out
    out_shape = [jax.ShapeDtypeStruct((T, C), dt) for dt in out_dtypes]
    io_aliases = None
    if into is not None:
        buf, total, off = into
        assert n_out == 1 and comm is None
        out_specs = [pl.BlockSpec((tb, cb), lambda j, i: (i, j + off))]
        out_shape = [jax.ShapeDtypeStruct((T, total), out_dtypes[0])]
        if buf is not None:
            specs, args, io_aliases = specs + [ANY], args + [buf], {n_in: 0}
            n_in += 1
    outs, comm_outs = _call(
        body, name=name, grid=(nJ, nI), in_specs=specs,
        out_specs=out_specs + [pl.BlockSpec((rows, cb), lambda j, i: (0, j)) for rows in acc_rows],
        out_shape=out_shape + [jax.ShapeDtypeStruct((rows, C), F32) for rows in acc_rows],
        scratch_shapes=[], semantics=("parallel", "arbitrary"), args=args, comm=comm, io_aliases=io_aliases)
    return outs if comm is None else (outs, comm_outs)


def _conv_causal(xe, w):
    K = w.shape[0]
    y = xe * w[K - 1:K]
    for j in range(K - 1):
        y = y + pltpu.roll(xe, K - 1 - j, 0) * w[j:j + 1]
    return y


def _conv_anti(de, w):
    K, n = w.shape[0], de.shape[0]
    y = de * w[K - 1:K]
    for j in range(K - 1):
        y = y + pltpu.roll(de, n - (K - 1 - j), 0) * w[j:j + 1]
    return y


def _conv_dw(dce, xe, K):
    n = dce.shape[0]
    tb = n - 2 * HALO
    rows = []
    for j in range(K):
        xs = xe if j == K - 1 else pltpu.roll(xe, K - 1 - j, 0)
        rows.append(jnp.sum((dce * xs)[HALO:HALO + tb], axis=0, keepdims=True))
    rows.append(jnp.zeros((HALO - K, dce.shape[1]), F32))
    return jnp.concatenate(rows, axis=0)


def _own(xe):
    return xe[HALO:xe.shape[0] - HALO]


def _row0(v):
    return jnp.concatenate([v, jnp.zeros((HALO - 1, v.shape[1]), F32)], axis=0)


def _per_head(fn, *xs):
    n = xs[0].shape[1] // HEAD_DIM
    outs = [fn(*[x[:, g * HEAD_DIM:(g + 1) * HEAD_DIM] for x in xs]) for g in range(n)]
    return outs[0] if n == 1 else jnp.concatenate(outs, axis=1)


def _rms_fwd(x, g, name):
    T, D = x.shape

    def fn(j, i, xv, gv):
        r = lax.rsqrt(jnp.mean(xv * xv, axis=1, keepdims=True) + EPS)
        return (xv * r * gv,)

    return _tiled(fn, T=T, C=D, ins=[("cur", x, None), ("row", g, None)], out_dtypes=[BF16], cb=D, name=name)[0]


def _rms_bwd_math(dy, xv, gv):
    r = lax.rsqrt(jnp.mean(xv * xv, axis=1, keepdims=True) + EPS)
    xh = xv * r
    dxh = dy * gv
    dx = r * (dxh - xh * jnp.mean(dxh * xh, axis=1, keepdims=True))
    dg = jnp.sum(dy * xh, axis=0, keepdims=True)
    return dx, dg


def _rms_bwd(dh, x, g, dres, name, comm=None):
    T, D = x.shape

    def fn(j, i, dhv, xv, gv, dr):
        dx, dg = _rms_bwd_math(dhv, xv, gv)
        return dr + dx, dr + dx, _row0(dg)

    return _tiled(fn, T=T, C=D, ins=[("cur", dh, None), ("cur", x, None), ("row", g, None), ("cur", dres, None)],
                  out_dtypes=[F32, BF16], acc_rows=[HALO], cb=D, name=name, comm=comm)


def _ga_fwd(proj, w_a, CW, cb, total):
    T = proj.shape[0]
    n = CW // cb

    def fn(j, i, ax, ab, ac, w):
        c = _conv_causal(ac * ax, w)
        return (ab * _own(c),)

    return _tiled(fn, T=T, C=CW, ins=[("ext", proj, None), ("cur", proj, lambda j: j + n), ("ext", proj, lambda j: j + 2 * n),
                                       ("row", w_a, None)], out_dtypes=[BF16], cb=cb, name="ga_fwd", into=(None, total, 0))[0]


def _ga_bwd(dymix, proj, w_a, CW, cb):
    T = proj.shape[0]
    n = CW // cb
    K = w_a.shape[0]

    def fn(j, i, dy, ax, ab, ac, w):
        u = ac * ax
        c = _conv_causal(u, w)
        dc = dy * ab
        du = _conv_anti(dc, w)
        return _own(du * ac), _own(dy * c), _own(du * ax), _conv_dw(dc, u, K)

    return _tiled(fn, T=T, C=CW, ins=[("ext", dymix, None), ("ext", proj, None), ("ext", proj, lambda j: j + n),
                                       ("ext", proj, lambda j: j + 2 * n), ("row", w_a, None)],
                  out_dtypes=[BF16, BF16, BF16], acc_rows=[HALO], cb=cb, name="ga_bwd")


def _l2n(s):
    return s * lax.rsqrt(jnp.sum(s * s, axis=1, keepdims=True) + EPS)


def _qkv_fwd(proj, w_sec, coff, normalize, DNW, cb, name, comm=None):
    T = proj.shape[0]

    def fn(j, i, pre, w):
        c = _own(_conv_causal(pre, w))
        s = c * _sigmoid(c)
        return (_per_head(_l2n, s) if normalize else s,)

    res = _tiled(fn, T=T, C=DNW, ins=[("ext", proj, lambda j: j + coff), ("row", w_sec, None)],
                 out_dtypes=[F32], cb=cb, name=name, comm=comm)
    return res[0] if comm is None else (res[0][0], res[1])


def _qkv_bwd(dsec, proj, w_sec, coff, normalize, DNW, cb, name):
    T = proj.shape[0]
    K = w_sec.shape[0]

    def l2n_bwd(s, dn):
        r = lax.rsqrt(jnp.sum(s * s, axis=1, keepdims=True) + EPS)
        nrm = s * r
        return r * (dn - nrm * jnp.sum(dn * nrm, axis=1, keepdims=True))

    def fn(j, i, dn, pre, w):
        c = _conv_causal(pre, w)
        sg = _sigmoid(c)
        s = c * sg
        ds = _per_head(l2n_bwd, s, dn) if normalize else dn
        dc = ds * (sg * (1.0 + c * (1.0 - sg)))
        return _own(_conv_anti(dc, w)), _conv_dw(dc, pre, K)

    return _tiled(fn, T=T, C=DNW, ins=[("ext", dsec, None), ("ext", proj, lambda j: j + coff), ("row", w_sec, None)],
                  out_dtypes=[BF16], acc_rows=[HALO], cb=cb, name=name)


def _gb_fwd(small, a_log_row, dt_row, H):
    T = small.shape[0]

    def fn(j, i, sm, al, dt):
        z = sm + dt
        sp = jnp.maximum(z, 0.0) + jnp.log(1.0 + jnp.exp(-jnp.abs(z)))
        g = -jnp.exp(al) * sp
        beta = _sigmoid(pltpu.roll(sm, LANES - H, 1))
        return g, beta

    return _tiled(fn, T=T, C=LANES, ins=[("cur", small, None), ("row", a_log_row, None), ("row", dt_row, None)],
                  out_dtypes=[F32, F32], cb=LANES, name="gb_fwd")


def _gb_bwd(dgB, dbB, small, g, beta, a_log_row, dt_row, H):
    T = small.shape[0]

    def fn(j, i, dgv, dbv, sm, gv, bv, al, dt):
        lane = lax.broadcasted_iota(jnp.int32, sm.shape, 1)
        dg = jnp.zeros(sm.shape, F32)
        db = jnp.zeros(sm.shape, F32)
        for h in range(H):
            dg = jnp.where(lane == h, jnp.sum(dgv[h], axis=1, keepdims=True), dg)
            db = jnp.where(lane == h, jnp.sum(dbv[h], axis=1, keepdims=True), db)
        da = dg * (-jnp.exp(al)) * _sigmoid(sm + dt)
        dbb = db * bv * (1.0 - bv)
        dsm = jnp.where(lane < H, da, 0.0) + pltpu.roll(jnp.where(lane < H, dbb, 0.0), H, 1)
        d_alog = jnp.sum(jnp.where(lane < H, dg * gv, 0.0), axis=0, keepdims=True)
        d_dt = jnp.sum(jnp.where(lane < H, da, 0.0), axis=0, keepdims=True)
        return dsm, jnp.concatenate([d_alog, d_dt, jnp.zeros((HALO - 2, LANES), F32)], axis=0)

    return _tiled(fn, T=T, C=LANES, ins=[("stack", dgB, None), ("stack", dbB, None), ("cur", small, None), ("cur", g, None),
                                          ("cur", beta, None), ("row", a_log_row, None), ("row", dt_row, None)],
                  out_dtypes=[BF16], acc_rows=[HALO], cb=LANES, name="gb_bwd")


_DIMS = {"nn": (((1,), (0,)), ((), ())), "nt": (((1,), (1,)), ((), ())), "tn": (((0,), (0,)), ((), ()))}
_DOT_BWD = {"nn": (("nt", "gb"), ("tn", "ag")), "nt": (("nn", "gb"), ("tn", "ga")), "tn": (("nt", "bg"), ("nn", "ag"))}


def _split(a):
    hi = a.astype(BF16)
    return hi, (a - hi.astype(F32)).astype(BF16)


def _raw_dot(a, b, kind, passes):
    dg = lambda x, y: lax.dot_general(x, y, _DIMS[kind], preferred_element_type=F32)
    if passes == 1:
        return dg(a.astype(BF16), b.astype(BF16))
    ah, al = _split(a)
    bh, bl = _split(b)
    if kind == "tn":
        return dg(ah, bh) + (dg(ah, bl) + dg(al, bh))
    m = a.shape[0]
    top = dg(jnp.concatenate([ah, al], axis=0), bh)
    return top[:m] + (dg(ah, bl) + top[m:])


def _raw_dot_exact(a, b, kind, exact):
    dg = lambda x, y: lax.dot_general(x, y, _DIMS[kind], preferred_element_type=F32)
    if exact == "a":
        bh, bl = _split(b)
        return dg(a.astype(BF16), bh) + dg(a.astype(BF16), bl)
    ah, al = _split(a)
    return dg(ah, b.astype(BF16)) + dg(al, b.astype(BF16))


@functools.lru_cache(maxsize=None)
def _dotc(kind):
    @jax.custom_vjp
    def f(a, b):
        return _raw_dot_exact(a, b, kind, "a")

    def fwd(a, b):
        return _raw_dot_exact(a, b, kind, "a"), a

    def bwd(a, g):
        db = _raw_dot_exact(a, g, "tn", "a") if kind == "nn" else _raw_dot_exact(g, a, "tn", "b")
        return jnp.zeros_like(a), db

    f.defvjp(fwd, bwd)
    return f


@functools.lru_cache(maxsize=None)
def _dotf(kind, passes):
    @jax.custom_vjp
    def f(a, b):
        return _raw_dot(a, b, kind, passes)

    def fwd(a, b):
        return _raw_dot(a, b, kind, passes), (a, b)

    def bwd(res, g):
        ops = {"a": res[0], "b": res[1], "g": g}
        (ka, oa), (kb, ob) = _DOT_BWD[kind]
        return (_raw_dot(ops[oa[0]], ops[oa[1]], ka, passes), _raw_dot(ops[ob[0]], ops[ob[1]], kb, passes))

    f.defvjp(fwd, bwd)
    return f


@jax.custom_vjp
def _saved_inverse(L, inv):
    return inv


def _saved_inverse_fwd(L, inv):
    return inv, inv


def _saved_inverse_bwd(inv, g):
    d3nt, d3tn = _dotf("nt", 3), _dotf("tn", 3)
    return -d3nt(d3tn(inv, g), inv), jnp.zeros_like(inv)


_saved_inverse.defvjp(_saved_inverse_fwd, _saved_inverse_bwd)


def _chunk_fn(q, k, v, gB, bB, S, inv_saved=None):
    C = CHUNK
    d3 = _dotf("nn", 3)
    d1, d1nt, d1tn = _dotf("nn", 1), _dotf("nt", 1), _dotf("tn", 1)
    each = lambda f, *ls: tuple(f(*xs) for xs in zip(*ls))
    row = lax.broadcasted_iota(jnp.int32, (C, C), 0)
    col = lax.broadcasted_iota(jnp.int32, (C, C), 1)
    causal = row >= col
    strict = row > col
    tril = jnp.where(causal, 1.0, 0.0).astype(F32)
    eye = jnp.where(row == col, 1.0, 0.0).astype(F32)
    avg = jnp.full((C, HEAD_DIM), 1.0 / HEAD_DIM, F32)
    gc = each(lambda g: _dotc("nn")(tril, g), gB)
    R = each(lambda g: _dotc("nt")(avg, g), gc)
    decay = each(lambda g, r: jnp.where(causal, jnp.exp(jnp.where(causal, g[:, :C] - r, 0.0)), 0.0), gc, R)
    kk = each(lambda x: d1nt(x, x), k)
    L = each(lambda a, d, b: jnp.where(strict, a * d * b[:, :C], 0.0), kk, decay, bB)
    if inv_saved is None:
        inv = each(lambda l: eye - l, L)
        P = L
        for _ in range(5):
            P = each(lambda p: d3(p, p), P)
            inv = each(lambda a, p: d3(a, eye + p), inv, P)
    else:
        inv = each(_saved_inverse, L, inv_saved)
    eg = each(jnp.exp, gc)
    u = each(lambda a, x, b: d3(a, x * b), inv, v, bB)
    w = each(lambda a, x, b, e: d3(a, x * b * e), inv, k, bB, eg)
    qs = each(lambda x: x * (HEAD_DIM ** -0.5), q)
    qk = each(lambda a, x, d: d1nt(a, x) * d, qs, k, decay)
    gl = each(lambda g: g[C - 1:C, :], gc)
    kd = each(lambda x, a, g: x * jnp.exp(a - g), k, gl, gc)
    qe = each(lambda a, e: a * e, qs, eg)
    nh = len(S)
    o = ()
    for c in range(len(q) // nh):
        sl = slice(c * nh, (c + 1) * nh)
        v_new = each(lambda a, b, s: a - d1(b, s), u[sl], w[sl], S)
        o1 = each(lambda a, s: d1(a, s), qe[sl], S)
        o += each(lambda a, b, vn: a + d1(b, vn), o1, qk[sl], v_new)
        kv = each(lambda x, vn: d1tn(x, vn), kd[sl], v_new)
        S = each(lambda s, a, b: s * jnp.exp(a) + b, S, gl[sl], kv)
    return (o, S), inv


def _sel_lane(x, h):
    lane = lax.broadcasted_iota(jnp.int32, x.shape, 1)
    return jnp.broadcast_to(jnp.sum(jnp.where(lane == h, x, 0.0), axis=1, keepdims=True), x.shape)


def _tile_of(ref, c, h):
    return ref[c * CHUNK:(c + 1) * CHUNK, h * HEAD_DIM:(h + 1) * HEAD_DIM]


def _chunks_per_step(N):
    return 4 if N % 4 == 0 else (2 if N % 2 == 0 else 1)


def _delta_fwd(q, k, v, g, beta, comm=None):
    T = q.shape[0]
    H, N = q.shape[1] // HEAD_DIM, T // CHUNK
    cps = _chunks_per_step(N)
    rows = cps * CHUNK

    def body(q_ref, k_ref, v_ref, g_ref, b_ref, o_ref, s_ref, inv_ref, S):
        @pl.when(pl.program_id(0) == 0)
        def _():
            S[...] = jnp.zeros_like(S)

        gv, bv = g_ref[...], b_ref[...]
        pairs = lambda f: tuple(f(c, h) for c in range(cps) for h in range(H))
        S_in = tuple(S[h] for h in range(H))
        for h in range(H):
            s_ref[h, 0] = S_in[h]
        (o, S_new), inv = _chunk_fn(pairs(lambda c, h: _tile_of(q_ref, c, h)), pairs(lambda c, h: _tile_of(k_ref, c, h)),
                                    pairs(lambda c, h: _tile_of(v_ref, c, h)),
                                    pairs(lambda c, h: _sel_lane(gv[c * CHUNK:(c + 1) * CHUNK], h)),
                                    pairs(lambda c, h: _sel_lane(bv[c * CHUNK:(c + 1) * CHUNK], h)), S_in)
        for c in range(cps):
            for h in range(H):
                o_ref[c * CHUNK:(c + 1) * CHUNK, h * HEAD_DIM:(h + 1) * HEAD_DIM] = o[c * H + h]
                inv_ref[h, c] = inv[c * H + h]
        for h in range(H):
            S[h] = S_new[h]

    blk = pl.BlockSpec((rows, H * HEAD_DIM), lambda n: (n, 0))
    gblk = pl.BlockSpec((rows, LANES), lambda n: (n, 0))
    outs, comm_outs = _call(
        body, name="delta_fwd", grid=(N // cps,), in_specs=[blk, blk, blk, gblk, gblk],
        out_specs=[blk, pl.BlockSpec((H, 1, HEAD_DIM, HEAD_DIM), lambda n: (0, n, 0, 0)),
                   pl.BlockSpec((H, cps, CHUNK, CHUNK), lambda n: (0, n, 0, 0))],
        out_shape=[jax.ShapeDtypeStruct((T, H * HEAD_DIM), F32), jax.ShapeDtypeStruct((H, N // cps, HEAD_DIM, HEAD_DIM), F32),
                   jax.ShapeDtypeStruct((H, N, CHUNK, CHUNK), F32)],
        scratch_shapes=[pltpu.VMEM((H, HEAD_DIM, HEAD_DIM), F32)],
        semantics=("arbitrary",), args=(q, k, v, g, beta), comm=comm)
    return outs[0], outs[1], outs[2], comm_outs


def _delta_bwd(q, k, v, g, beta, S0, inv, do, comm=None):
    T = q.shape[0]
    H, N = q.shape[1] // HEAD_DIM, T // CHUNK
    cps = _chunks_per_step(N)
    rows, NS = cps * CHUNK, N // cps

    def body(q_ref, k_ref, v_ref, g_ref, b_ref, s_ref, inv_ref, do_ref, dq_ref, dk_ref, dv_ref, dg_ref, db_ref, dS):
        @pl.when(pl.program_id(0) == 0)
        def _():
            dS[...] = jnp.zeros_like(dS)

        gv, bv = g_ref[...], b_ref[...]
        pairs = lambda f: tuple(f(c, h) for c in range(cps) for h in range(H))
        heads = lambda f: tuple(f(h) for h in range(H))
        _, vjp, _ = jax.vjp(_chunk_fn, pairs(lambda c, h: _tile_of(q_ref, c, h)), pairs(lambda c, h: _tile_of(k_ref, c, h)),
                            pairs(lambda c, h: _tile_of(v_ref, c, h)),
                            pairs(lambda c, h: _sel_lane(gv[c * CHUNK:(c + 1) * CHUNK], h)),
                            pairs(lambda c, h: _sel_lane(bv[c * CHUNK:(c + 1) * CHUNK], h)),
                            heads(lambda h: s_ref[h, 0]), pairs(lambda c, h: inv_ref[h, c]), has_aux=True)
        dq, dk, dv, dgB, dbB, dS_prev, _ = vjp((pairs(lambda c, h: _tile_of(do_ref, c, h)), heads(lambda h: dS[h])))
        for c in range(cps):
            for h in range(H):
                r, sl = slice(c * CHUNK, (c + 1) * CHUNK), slice(h * HEAD_DIM, (h + 1) * HEAD_DIM)
                dq_ref[r, sl] = dq[c * H + h]
                dk_ref[r, sl] = dk[c * H + h]
                dv_ref[r, sl] = dv[c * H + h]
                dg_ref[h, r] = dgB[c * H + h]
                db_ref[h, r] = dbB[c * H + h]
        for h in range(H):
            dS[h] = dS_prev[h]

    blk = pl.BlockSpec((rows, H * HEAD_DIM), lambda n: (NS - 1 - n, 0))
    gblk = pl.BlockSpec((rows, LANES), lambda n: (NS - 1 - n, 0))
    hblk = pl.BlockSpec((H, rows, LANES), lambda n: (0, NS - 1 - n, 0))
    sd = jax.ShapeDtypeStruct
    outs, comm_outs = _call(
        body, name="delta_bwd", grid=(NS,),
        in_specs=[blk, blk, blk, gblk, gblk, pl.BlockSpec((H, 1, HEAD_DIM, HEAD_DIM), lambda n: (0, NS - 1 - n, 0, 0)),
                  pl.BlockSpec((H, cps, CHUNK, CHUNK), lambda n: (0, NS - 1 - n, 0, 0)), blk],
        out_specs=[blk, blk, blk, hblk, hblk],
        out_shape=[sd((T, H * HEAD_DIM), F32)] * 3 + [sd((H, T, LANES), F32)] * 2,
        scratch_shapes=[pltpu.VMEM((H, HEAD_DIM, HEAD_DIM), F32)],
        semantics=("arbitrary",), args=(q, k, v, g, beta, S0, inv, do), comm=comm)
    return (*outs, comm_outs)


def _gnorm_fwd(o, proj, z_coff, gdn_t, DNW, buf, coff):
    T = o.shape[0]

    def fn(j, i, ov, zv, gv):
        def one(oh, zh, gh):
            r = lax.rsqrt(jnp.mean(oh * oh, axis=1, keepdims=True) + EPS)
            return oh * r * gh * (zh * _sigmoid(zh))
        return (_per_head(one, ov, zv, jnp.broadcast_to(gv, ov.shape)),)

    return _tiled(fn, T=T, C=DNW, ins=[("cur", o, None), ("cur", proj, lambda j: j + z_coff), ("row", gdn_t, None)],
                  out_dtypes=[BF16], cb=DNW, name="gnorm_fwd", into=(buf, buf.shape[1], coff))[0]


def _gnorm_bwd(dymix, y_coff, o, proj, z_coff, gdn_t, DNW):
    T = o.shape[0]
    nh = DNW // HEAD_DIM

    def fn(j, i, dy, ov, zv, gv):
        dos, dzs, dgs = [], [], jnp.zeros((1, HEAD_DIM), F32)
        for h in range(nh):
            sl = slice(h * HEAD_DIM, (h + 1) * HEAD_DIM)
            dyh, oh, zh, gh = dy[:, sl].astype(F32), ov[:, sl], zv[:, sl], gv[:, sl]
            r = lax.rsqrt(jnp.mean(oh * oh, axis=1, keepdims=True) + EPS)
            on = oh * r
            sg = _sigmoid(zh)
            sz = zh * sg
            dzs.append(dyh * on * gh * (sg * (1.0 + zh * (1.0 - sg))))
            don = dyh * gh * sz
            dos.append(r * (don - on * jnp.mean(don * on, axis=1, keepdims=True)))
            dgs = dgs + jnp.sum(dyh * on * sz, axis=0, keepdims=True)
        cat = (lambda xs: xs[0] if nh == 1 else jnp.concatenate(xs, axis=1))
        return cat(dos), cat(dzs), _row0(dgs)

    T_ = T
    nI = T_ // _tile(T_, 256, HALO)
    tb = T_ // nI
    specs_cb = DNW

    def body_wrap():
        def body(dy_ref, o_ref, z_ref, g_ref, do_ref, dz_ref, dg_ref):
            i = pl.program_id(0)
            d_o, d_z, d_g = fn(0, i, dy_ref[...], o_ref[...], z_ref[...], g_ref[...])
            do_ref[...] = d_o
            dz_ref[...] = d_z.astype(dz_ref.dtype)

            @pl.when(i == 0)
            def _():
                dg_ref[...] = d_g

            @pl.when(i > 0)
            def _():
                dg_ref[...] += d_g

        return pl.pallas_call(
            body, name="gnorm_bwd", grid=(nI,),
            in_specs=[pl.BlockSpec((tb, specs_cb), lambda i: (i, y_coff)), pl.BlockSpec((tb, specs_cb), lambda i: (i, 0)),
                      pl.BlockSpec((tb, specs_cb), lambda i: (i, z_coff)), pl.BlockSpec((1, specs_cb), lambda i: (0, 0))],
            out_specs=[pl.BlockSpec((tb, specs_cb), lambda i: (i, 0)), pl.BlockSpec((tb, specs_cb), lambda i: (i, 0)),
                       pl.BlockSpec((HALO, HEAD_DIM), lambda i: (0, 0))],
            out_shape=[jax.ShapeDtypeStruct((T_, DNW), F32), jax.ShapeDtypeStruct((T_, DNW), BF16),
                       jax.ShapeDtypeStruct((HALO, HEAD_DIM), F32)],
            compiler_params=pltpu.CompilerParams(dimension_semantics=("arbitrary",), vmem_limit_bytes=VMEM_LIMIT),
        )(dymix, o, proj, gdn_t)

    return body_wrap()


def _ffn_fwd(up_g, up_v, w_g, w_v, cb):
    T, F = up_g.shape

    def fn(j, i, ug, uv, wg, wv):
        cg = _own(_conv_causal(ug, wg))
        cv = _own(_conv_causal(uv, wv))
        return (cg * _sigmoid(cg) * cv,)

    return _tiled(fn, T=T, C=F, ins=[("ext", up_g, None), ("ext", up_v, None), ("row", w_g, None), ("row", w_v, None)],
                  out_dtypes=[BF16], tb=1024, cb=cb, name="ffn_fwd")[0]


def _ffn_bwd(dact, up_g, up_v, w_g, w_v, cb, comm=None):
    T, F = up_g.shape
    K = w_g.shape[0]

    def fn(j, i, da, ug, uv, wg, wv):
        cg = _conv_causal(ug, wg)
        cv = _conv_causal(uv, wv)
        sg = _sigmoid(cg)
        dgate = da * cv * (sg * (1.0 + cg * (1.0 - sg)))
        dval = da * (cg * sg)
        return (_own(_conv_anti(dgate, wg)), _own(_conv_anti(dval, wv)), _conv_dw(dgate, ug, K), _conv_dw(dval, uv, K))

    return _tiled(fn, T=T, C=F, ins=[("ext", dact, None), ("ext", up_g, None), ("ext", up_v, None), ("row", w_g, None),
                                      ("row", w_v, None)], out_dtypes=[BF16, BF16], acc_rows=[HALO, HALO], tb=1024, cb=cb,
                  name="ffn_bwd", comm=comm)


def _wide(R, Cc, n_f32, unit=HALO):
    cb = Cc if (Cc % LANES or Cc <= 4096) else _tile(Cc, 2048, LANES)
    cap = max(unit, EW_VMEM_BUDGET // (2 * 4 * n_f32 * cb) // unit * unit)
    return _tile(R, cap, unit), cb


def _adamw(w, g, m, v, name, comm=None):
    R, Cc = w.shape
    tb, cb = _wide(R, Cc, 7) if R % HALO == 0 else (R, _tile(Cc, EW_VMEM_BUDGET // (2 * 4 * 7 * R) // LANES * LANES, LANES))
    c1 = 1.0 / (1.0 - ADAM_B1 ** ADAM_STEP)
    c2 = 1.0 / (1.0 - ADAM_B2 ** ADAM_STEP)

    def fn(j, i, wv, gv, mv, vv):
        m2 = ADAM_B1 * mv + (1.0 - ADAM_B1) * gv
        v2 = ADAM_B2 * vv + (1.0 - ADAM_B2) * (gv * gv)
        delta = -ADAM_LR * ((m2 * c1) / (jnp.sqrt(v2 * c2) + ADAM_EPS) + ADAM_WD * wv)
        return delta, m2, v2

    return _tiled(fn, T=R, C=Cc, ins=[("cur", w, None), ("cur", g, None), ("cur", m, None), ("cur", v, None)],
                  out_dtypes=[F32, F32, F32], tb=tb, cb=cb, name=name, comm=comm)


def _join_shards(w4, n_main):
    S4, R, cs = w4.shape
    n_small = S4 * cs - n_main
    assert 0 < n_small <= LANES and n_small <= cs
    tb = _tile(R, 256, 2 * HALO)

    def body(w_ref, main_ref, small_ref):
        for t in range(S4 - 1):
            main_ref[:, t * cs:(t + 1) * cs] = w_ref[t]
        last = w_ref[S4 - 1]
        main_ref[:, (S4 - 1) * cs:] = last[:, :cs - n_small]
        small_ref[...] = jnp.zeros_like(small_ref)
        small_ref[:, :n_small] = last[:, cs - n_small:]

    return pl.pallas_call(
        body, name="join_w_in", grid=(R // tb,), in_specs=[pl.BlockSpec((S4, tb, cs), lambda i: (0, i, 0))],
        out_specs=[pl.BlockSpec((tb, n_main), lambda i: (i, 0)), pl.BlockSpec((tb, LANES), lambda i: (i, 0))],
        out_shape=[jax.ShapeDtypeStruct((R, n_main), w4.dtype), jax.ShapeDtypeStruct((R, LANES), w4.dtype)],
        compiler_params=pltpu.CompilerParams(dimension_semantics=("parallel",), vmem_limit_bytes=VMEM_LIMIT))(w4)


def _split_shards(main, small, cs):
    R, n_main = main.shape
    n_small = 4 * cs - n_main
    tb = _tile(R, 256, HALO)

    def body(main_ref, small_ref, out_ref):
        for t in range(3):
            out_ref[t] = main_ref[:, t * cs:(t + 1) * cs]
        out_ref[3, :, :cs - n_small] = main_ref[:, 3 * cs:]
        out_ref[3, :, cs - n_small:] = small_ref[:, :n_small]

    return pl.pallas_call(
        body, name="split_g_in", grid=(R // tb,),
        in_specs=[pl.BlockSpec((tb, n_main), lambda i: (i, 0)), pl.BlockSpec((tb, LANES), lambda i: (i, 0))],
        out_specs=pl.BlockSpec((4, tb, cs), lambda i: (0, i, 0)), out_shape=jax.ShapeDtypeStruct((4, R, cs), main.dtype),
        compiler_params=pltpu.CompilerParams(dimension_semantics=("parallel",), vmem_limit_bytes=VMEM_LIMIT))(main, small)


def _sum_stack(st, name):
    S, R, Cc = st.shape
    cb = _tile(Cc, 512, LANES) if Cc % LANES == 0 else Cc

    def fn(j, i, sv):
        t = sv[0]
        for s in range(1, S):
            t = t + sv[s]
        return (t,)

    return _tiled(fn, T=R, C=Cc, ins=[("stack", st, None)], out_dtypes=[F32], cb=cb, name=name)[0]


ANY = pl.BlockSpec(memory_space=pl.ANY)


def _place():
    x, y, c = lax.axis_index("x"), lax.axis_index("y"), lax.axis_index("c")
    return x, y, c, 2 * x + y


def _chip_dev(s, c):
    return (s // 2, s % 2, c)


class _Comm:
    def __init__(self, ins, out_shapes, sems, start, wait, aliases=None):
        self.ins, self.out_shapes, self.sems = list(ins), list(out_shapes), list(sems)
        self.start, self.wait, self.aliases = start, wait, dict(aliases or {})


def _merge(*comms):
    offs, i, o, s = [], 0, 0, 0
    for cm in comms:
        offs.append((i, o, s))
        i, o, s = i + len(cm.ins), o + len(cm.out_shapes), s + len(cm.sems)

    def part(refs, k, cm):
        i0, o0, s0 = offs[k]
        return refs[0][i0:i0 + len(cm.ins)], refs[1][o0:o0 + len(cm.out_shapes)], refs[2][s0:s0 + len(cm.sems)]

    def start(*refs):
        for k, cm in enumerate(comms):
            cm.start(*part(refs, k, cm))

    def wait(*refs):
        for k, cm in enumerate(comms):
            cm.wait(*part(refs, k, cm))

    aliases = {}
    for k, cm in enumerate(comms):
        for a, b in cm.aliases.items():
            aliases[offs[k][0] + a] = offs[k][1] + b
    return _Comm([a for cm in comms for a in cm.ins], [a for cm in comms for a in cm.out_shapes],
                 [a for cm in comms for a in cm.sems], start, wait, aliases)


def _call(body, *, name, grid, in_specs, out_specs, out_shape, scratch_shapes, semantics, args, comm=None, io_aliases=None):
    if comm is None:
        outs = pl.pallas_call(
            body, name=name, grid=grid, in_specs=in_specs, out_specs=out_specs, out_shape=out_shape,
            scratch_shapes=list(scratch_shapes), input_output_aliases=dict(io_aliases or {}),
            compiler_params=pltpu.CompilerParams(dimension_semantics=semantics, vmem_limit_bytes=VMEM_LIMIT))(*args)
        return list(outs), []
    assert not io_aliases
    n_in, n_out, n_scr = len(in_specs), len(out_specs), len(scratch_shapes)
    ci, co = len(comm.ins), len(comm.out_shapes)

    def wrapped(*refs):
        r = 0
        ins, r = refs[r:r + n_in], r + n_in
        cins, r = refs[r:r + ci], r + ci
        outs, r = refs[r:r + n_out], r + n_out
        couts, r = refs[r:r + co], r + co
        scr, r = refs[r:r + n_scr], r + n_scr
        csems = refs[r:]
        ids = [pl.program_id(a) for a in range(len(grid))]
        first, last = ids[0] == 0, ids[0] == grid[0] - 1
        for a in range(1, len(grid)):
            first = jnp.logical_and(first, ids[a] == 0)
            last = jnp.logical_and(last, ids[a] == grid[a] - 1)

        @pl.when(first)
        def _():
            comm.start(cins, couts, csems)

        body(*ins, *outs, *scr)

        @pl.when(last)
        def _():
            comm.wait(cins, couts, csems)

    outs = pl.pallas_call(
        wrapped, name=name, grid=grid, in_specs=list(in_specs) + [ANY] * ci, out_specs=list(out_specs) + [ANY] * co,
        out_shape=list(out_shape) + comm.out_shapes, scratch_shapes=list(scratch_shapes) + comm.sems,
        input_output_aliases={n_in + a: n_out + b for a, b in comm.aliases.items()},
        compiler_params=pltpu.CompilerParams(dimension_semantics=("arbitrary",) * len(grid), vmem_limit_bytes=VMEM_LIMIT),
    )(*args, *comm.ins)
    return list(outs[:n_out]), list(outs[n_out:])


def _run_comm(comm, name):
    ci, co = len(comm.ins), len(comm.out_shapes)

    def body(*refs):
        cins, couts, csems = refs[:ci], refs[ci:ci + co], refs[ci + co:]
        comm.start(cins, couts, csems)
        comm.wait(cins, couts, csems)

    outs = pl.pallas_call(body, name=name, in_specs=[ANY] * ci, out_specs=[ANY] * co, out_shape=comm.out_shapes,
                          scratch_shapes=comm.sems, input_output_aliases=comm.aliases)(*comm.ins)
    return list(outs)


def _ag_comm(shard, land=None, q=0, nq=1):
    two, R2, Cc = shard.shape
    rows = pl.ds(q * (R2 // nq), R2 // nq)
    DMA = pltpu.SemaphoreType.DMA

    def copies(ins, outs, sems, which):
        sh, out = ins[0], outs[0]
        send1, recv1, send2, recv2, send0, recv0 = sems
        x, y, c, s = _place()
        sib = (x, y, 1 - c)
        rc = pltpu.make_async_remote_copy
        if which == "first":
            return [rc(sh.at[c, rows], out.at[s, c, rows], send1.at[m - 1], recv1.at[m - 1],
                       device_id=_chip_dev(s ^ m, c), device_id_type=MESH) for m in range(1, 4)]
        if which == "own":
            return [rc(sh.at[h, rows], out.at[s, h, rows], send0.at[h], recv0.at[h], device_id=sib, device_id_type=MESH)
                    for h in range(2)]
        if which == "landed":
            return [rc(sh.at[c, rows], out.at[s ^ m, c, rows], send1.at[m - 1], recv1.at[m - 1], device_id=sib,
                       device_id_type=MESH) for m in range(1, 4)]
        half = c if which == "passed" else 1 - c
        return [rc(out.at[s ^ m, half, rows], out.at[s ^ m, half, rows], send2.at[m - 1], recv2.at[m - 1], device_id=sib,
                   device_id_type=MESH) for m in range(1, 4)]

    def start(ins, outs, sems):
        for cp in copies(ins, outs, sems, "first") + copies(ins, outs, sems, "own"):
            cp.start()

    def wait(ins, outs, sems):
        passed = copies(ins, outs, sems, "passed")
        for lan, pas in zip(copies(ins, outs, sems, "landed"), passed):
            lan.wait_recv()
            pas.start()
        for cp in copies(ins, outs, sems, "handed"):
            cp.wait_recv()
        for cp in copies(ins, outs, sems, "own"):
            cp.wait()
        for cp in copies(ins, outs, sems, "first") + passed:
            cp.wait_send()

    return _Comm([shard] + ([land] if land is not None else []), [jax.ShapeDtypeStruct((4, two, R2, Cc), shard.dtype)],
                 [DMA((3,)), DMA((3,)), DMA((3,)), DMA((3,)), DMA((2,)), DMA((2,))], start, wait,
                 {1: 0} if land is not None else None)


def _ag_relay_comm(shard):
    two, R2, Cc = shard.shape
    lo, hi = pl.ds(0, R2 // 2), pl.ds(R2 // 2, R2 // 2)
    DMA = pltpu.SemaphoreType.DMA

    def copies(ins, outs, sems, which):
        sh, out = ins[0], outs[0]
        send1, recv1, sendr, recvr, send2, recv2, send0, recv0 = sems
        x, y, c, s = _place()
        sib = (x, y, 1 - c)
        nbr = lambda m: _chip_dev(s ^ m, c)
        rc = functools.partial(pltpu.make_async_remote_copy, device_id_type=MESH)
        if which == "first":
            return [rc(sh.at[c], out.at[s, c], send1.at[m - 1], recv1.at[m - 1], device_id=nbr(m)) for m in (1, 2)]
        if which == "landed":
            return [rc(sh.at[c], out.at[s ^ m, c], send1.at[m - 1], recv1.at[m - 1], device_id=sib) for m in (1, 2)]
        if which == "relay":
            return [rc(out.at[s ^ 2, c, lo], out.at[s ^ 2, c, lo], sendr.at[0], recvr.at[0], device_id=nbr(1)),
                    rc(out.at[s ^ 1, c, hi], out.at[s ^ 1, c, hi], sendr.at[1], recvr.at[1], device_id=nbr(2))]
        if which == "relayed":
            return [rc(out.at[s ^ 3, c, lo], out.at[s ^ 3, c, lo], sendr.at[0], recvr.at[0], device_id=sib),
                    rc(out.at[s ^ 3, c, hi], out.at[s ^ 3, c, hi], sendr.at[1], recvr.at[1], device_id=sib)]
        if which == "own":
            return [rc(sh.at[h], out.at[s, h], send0.at[h], recv0.at[h], device_id=sib) for h in range(2)]
        half = c if which == "passed" else 1 - c
        return [rc(out.at[s ^ m, half], out.at[s ^ m, half], send2.at[m - 1], recv2.at[m - 1], device_id=sib)
                for m in range(1, 4)]

    def start(ins, outs, sems):
        for cp in copies(ins, outs, sems, "first") + copies(ins, outs, sems, "own"):
            cp.start()

    def wait(ins, outs, sems):
        landed, relay = copies(ins, outs, sems, "landed"), copies(ins, outs, sems, "relay")
        passed = copies(ins, outs, sems, "passed")
        landed[1].wait_recv()
        relay[0].start()
        passed[1].start()
        landed[0].wait_recv()
        relay[1].start()
        passed[0].start()
        for cp in copies(ins, outs, sems, "relayed"):
            cp.wait_recv()
        passed[2].start()
        for cp in copies(ins, outs, sems, "handed"):
            cp.wait_recv()
        for cp in copies(ins, outs, sems, "own"):
            cp.wait()
        for cp in copies(ins, outs, sems, "first") + relay + passed:
            cp.wait_send()

    return _Comm([shard], [jax.ShapeDtypeStruct((4, two, R2, Cc), shard.dtype)],
                 [DMA((2,)), DMA((2,)), DMA((2,)), DMA((2,)), DMA((3,)), DMA((3,)), DMA((2,)), DMA((2,))], start, wait)


def _a2a_comm(S1, q=0, nq=1, land=None, cnt=1):
    S4, R2, Cc = S1.shape
    rows = pl.ds(q * (R2 // nq), cnt * (R2 // nq))
    DMA = pltpu.SemaphoreType.DMA

    def copies(ins, outs, sems):
        x, y, c, s = _place()
        return [pltpu.make_async_remote_copy(ins[0].at[s ^ m, rows], outs[0].at[m - 1, rows], sems[0].at[m - 1],
                                             sems[1].at[m - 1], device_id=_chip_dev(s ^ m, c), device_id_type=MESH)
                for m in range(1, 4)]

    def start(ins, outs, sems):
        for cp in copies(ins, outs, sems):
            cp.start()

    def wait(ins, outs, sems):
        for cp in copies(ins, outs, sems):
            cp.wait()

    return _Comm([S1] + ([land] if land is not None else []), [jax.ShapeDtypeStruct((3, R2, Cc), S1.dtype)],
                 [DMA((3,)), DMA((3,))], start, wait, {1: 0} if land is not None else None)


def _halves(G):
    return G.reshape(G.shape[0], 2, G.shape[1] // 2, G.shape[2])


def _swap_comm(piece):
    n, two, R2, Cc = piece.shape
    DMA = pltpu.SemaphoreType.DMA

    def copies(ins, outs, sems):
        x, y, c, s = _place()
        return [pltpu.make_async_remote_copy(ins[0].at[t, 1 - c], outs[0].at[t], sems[0].at[t], sems[1].at[t],
                                             device_id=(x, y, 1 - c), device_id_type=MESH) for t in range(n)]

    def start(ins, outs, sems):
        for cp in copies(ins, outs, sems):
            cp.start()

    def wait(ins, outs, sems):
        for cp in copies(ins, outs, sems):
            cp.wait()

    return _Comm([piece], [jax.ShapeDtypeStruct((n, R2, Cc), piece.dtype)], [DMA((n,)), DMA((n,))], start, wait)


def _add_half(pieces, As, cidx, name):
    R2, Cc = pieces[0].shape[2:]
    S4 = sum(pc.shape[0] for pc in pieces)
    tb, cb = _wide(R2, Cc, 3, 2 * HALO)
    nI, nJ = R2 // tb, Cc // cb

    def body(c_ref, g_ref, a_ref, *rest):
        rest[-1][...] = (g_ref[0, 0] + a_ref[0]).astype(BF16)

    out, t0 = None, 0
    for k, (pc, A) in enumerate(zip(pieces, As)):
        grid_spec = pltpu.PrefetchScalarGridSpec(
            num_scalar_prefetch=1, grid=(pc.shape[0], nI, nJ),
            in_specs=[pl.BlockSpec((1, 1, tb, cb), lambda t, i, j, c_ref: (t, c_ref[0], i, j)),
                      pl.BlockSpec((1, tb, cb), lambda t, i, j, c_ref: (t, i, j))] + ([ANY] if k else []),
            out_specs=pl.BlockSpec((tb, cb), lambda t, i, j, c_ref, t0=t0: ((t0 + t) * nI + i, j)))
        out = pl.pallas_call(
            functools.partial(body), name=f"{name}{k}", grid_spec=grid_spec, out_shape=jax.ShapeDtypeStruct((S4 * R2, Cc), BF16),
            input_output_aliases={3: 0} if k else {},
            compiler_params=pltpu.CompilerParams(dimension_semantics=("parallel", "parallel", "parallel"),
                                                 vmem_limit_bytes=VMEM_LIMIT),
        )(*((cidx, pc, A) + ((out,) if k else ())))
        t0 += pc.shape[0]
    return out.reshape(S4, R2, Cc)


def _add_own(S1, B, chip_idx, cidx, name):
    S4, R2, Cc = S1.shape
    tb, cb = _wide(R2, Cc, 3, 2 * HALO)

    def body(s_idx, c_idx, s_ref, b_ref, o_ref):
        o_ref[...] = ((s_ref[0].astype(F32) + b_ref[0].astype(F32)) + b_ref[1].astype(F32)) + b_ref[2].astype(F32)

    grid_spec = pltpu.PrefetchScalarGridSpec(
        num_scalar_prefetch=2, grid=(R2 // tb, Cc // cb),
        in_specs=[pl.BlockSpec((1, tb, cb), lambda i, j, s_idx, c_idx: (s_idx[0], i, j)),
                  pl.BlockSpec((3, tb, cb), lambda i, j, s_idx, c_idx: (0, i, j))],
        out_specs=pl.BlockSpec((None, tb, cb), lambda i, j, s_idx, c_idx: (c_idx[0], i, j)))
    return pl.pallas_call(body, name=name, grid_spec=grid_spec, out_shape=jax.ShapeDtypeStruct((2, R2, Cc), F32),
                          compiler_params=pltpu.CompilerParams(dimension_semantics=("parallel", "parallel"),
                                                               vmem_limit_bytes=VMEM_LIMIT))(chip_idx, cidx, S1, B)


def _fill_comm(Hs):
    def copy(ins, outs, sems):
        x, y, c, s = _place()
        return pltpu.make_async_remote_copy(ins[0].at[c], outs[0].at[c], sems[0], sems[1], device_id=(x, y, 1 - c),
                                            device_id_type=MESH)

    return _Comm([Hs], [jax.ShapeDtypeStruct(Hs.shape, Hs.dtype)], [pltpu.SemaphoreType.DMA, pltpu.SemaphoreType.DMA],
                 lambda *r: copy(*r).start(), lambda *r: copy(*r).wait(), {0: 0})


def _gather_all_comm(buf):
    R, Cc = buf.shape
    DMA = pltpu.SemaphoreType.DMA

    def copies(ins, outs, sems):
        x, y, c, s = _place()
        d = 2 * s + c
        return ([pltpu.make_async_remote_copy(ins[0], outs[0].at[d], sems[0].at[m - 1], sems[1].at[m - 1],
                                              device_id=((d ^ m) // 4, ((d ^ m) // 2) % 2, (d ^ m) % 2), device_id_type=MESH)
                 for m in range(1, 8)], pltpu.make_async_copy(ins[0], outs[0].at[d], sems[2]))

    def start(ins, outs, sems):
        remote, mine = copies(ins, outs, sems)
        for cp in remote + [mine]:
            cp.start()

    def wait(ins, outs, sems):
        remote, mine = copies(ins, outs, sems)
        for cp in remote + [mine]:
            cp.wait()

    return _Comm([buf], [jax.ShapeDtypeStruct((8, R, Cc), buf.dtype)], [DMA((7,)), DMA((7,)), DMA], start, wait)


def _pack_rows(vs):
    flat = jnp.concatenate([v.reshape(-1) for v in vs])
    n = flat.shape[0]
    rows = -(-n // (LANES * 2 * HALO)) * 2 * HALO
    return jnp.pad(flat, (0, rows * LANES - n)).reshape(rows, LANES)


def _unpack_rows(buf, shapes):
    flat = buf.reshape(-1)
    outs, o = [], 0
    for shp in shapes:
        n = 1
        for d in shp:
            n *= d
        outs.append(flat[o:o + n].reshape(shp))
        o += n
    return outs


def kernel(x, p, norm_mix_g, w_in, conv_a_w, conv_qkv_w, a_log, dt_bias, dn_norm_g, w_out, norm_ffn_g, w_up, conv_ffn_w, w_down, norm_ple_g, w_ple_gate, w_ple_proj, final_norm_g, loss_target, m_norm_mix_g, m_w_in, m_conv_a_w, m_conv_qkv_w, m_a_log, m_dt_bias, m_dn_norm_g, m_w_out, m_norm_ffn_g, m_w_up, m_conv_ffn_w, m_w_down, m_norm_ple_g, m_w_ple_gate, m_w_ple_proj, m_final_norm_g, v_norm_mix_g, v_w_in, v_conv_a_w, v_conv_qkv_w, v_a_log, v_dt_bias, v_dn_norm_g, v_w_out, v_norm_ffn_g, v_w_up, v_conv_ffn_w, v_w_down, v_norm_ple_g, v_w_ple_gate, v_w_ple_proj, v_final_norm_g):
    xs = x[0]
    ps = p[0, 0]
    tgt = loss_target[0]
    T, D = xs.shape
    H = a_log.shape[-1]
    DNW = H * HEAD_DIM
    CW = conv_a_w.shape[-1] * 4
    F = w_down.shape[1] * 4
    PD = ps.shape[-1]
    IN_MAIN = 3 * CW + 4 * DNW
    IN_COLS = IN_MAIN + 2 * H
    assert w_in.shape[-1] * 4 == IN_COLS and CW + DNW == D and 2 * H <= LANES
    cb = _tile(min(CW, DNW), 512, LANES)
    while F % cb:
        cb -= LANES
    cidx = lax.axis_index("c").astype(jnp.int32).reshape(1)
    chip = 2 * lax.axis_index("x") + lax.axis_index("y")

    def halves(w):
        sh = w[0].astype(BF16)
        return sh.reshape(2, sh.shape[0] // 2, sh.shape[1])

    def whole(land):
        return land.reshape(4, 2 * land.shape[2], land.shape[3])

    def rows(g4):
        return g4.reshape(4 * g4.shape[1], g4.shape[2])

    conv_shapes = [conv_a_w[0].shape, conv_qkv_w[0].shape, conv_ffn_w[0].shape]
    cpack = _pack_rows([conv_a_w[0], conv_qkv_w[0], conv_ffn_w[0]])
    sh_in, sh_out, sh_up, sh_down, sh_pg, sh_pp = (halves(w) for w in (w_in, w_out, w_up, w_down, w_ple_gate, w_ple_proj))
    l_in, cg = _run_comm(_merge(_ag_relay_comm(sh_in), _ag_comm(cpack.reshape(2, cpack.shape[0] // 2, LANES))), "ag_w_in_conv")
    w_in_main, w_in_small = _join_shards(whole(l_in), IN_MAIN)
    cg = cg.reshape(4, cpack.shape[0], LANES)
    parts = [_unpack_rows(cg[t], conv_shapes) for t in range(4)]
    cw_a = jnp.concatenate([parts[t][0] for t in range(4)], axis=1)
    cw_qkv = jnp.concatenate([parts[t][1] for t in range(4)], axis=1)
    cw_ffn = jnp.concatenate([parts[t][2] for t in range(4)], axis=1)
    cw_q, cw_k, cw_v = cw_qkv[:, :DNW], cw_qkv[:, DNW:2 * DNW], cw_qkv[:, 2 * DNW:]
    cw_fg, cw_fv = cw_ffn[:, :F], cw_ffn[:, F:]
    pad_row = lambda v: jnp.pad(v, ((0, 0), (0, LANES - v.shape[1])))
    a_log_row, dt_row = pad_row(a_log), pad_row(dt_bias)
    gdn_t = jnp.tile(dn_norm_g, (1, H))
    gfin = final_norm_g.reshape(1, D)

    h1 = _rms_fwd(xs, norm_mix_g, "rms1")
    proj, (l_up,) = _mm(h1, w_in_main, mode="nn", out_dtypes=[F32], name="mm_proj", comm=_ag_comm(sh_up, q=0, nq=2))
    small = _mm(h1, w_in_small, mode="nn", out_dtypes=[F32], name="mm_small")
    ymix = _ga_fwd(proj, cw_a, CW, cb, D)
    nq = 3 * CW // cb
    nd = DNW // cb
    qn, (l_out,) = _qkv_fwd(proj, cw_q, nq, True, DNW, cb, "q_fwd", comm=_ag_comm(sh_out, q=0, nq=2))
    kn, (l_out,) = _qkv_fwd(proj, cw_k, nq + nd, True, DNW, cb, "k_fwd", comm=_ag_comm(sh_out, l_out, q=1, nq=2))
    vs = _qkv_fwd(proj, cw_v, nq + 2 * nd, False, DNW, cb, "v_fwd")
    g, beta = _gb_fwd(small, a_log_row, dt_row, H)
    o, S0, inv_c, (l_up,) = _delta_fwd(qn, kn, vs, g, beta, comm=_ag_comm(sh_up, l_up, q=1, nq=2))
    w_out_f = rows(whole(l_out))
    w_up_4 = whole(l_up)
    z_coff = (3 * CW + 3 * DNW) // DNW
    assert (3 * CW + 3 * DNW) % DNW == 0 and CW % DNW == 0
    ymix = _gnorm_fwd(o, proj, z_coff, gdn_t, DNW, ymix, CW // DNW)
    add = lambda acc, r: (r + acc,)

    def out_epi(acc, xv, gv):
        x1v = xv + acc
        return x1v, x1v * lax.rsqrt(jnp.mean(x1v * x1v, axis=1, keepdims=True) + EPS) * gv

    x1, h2 = _mm(ymix, w_out_f, mode="nn", out_dtypes=[F32, BF16], epi=out_epi, extras=[xs], rows=[norm_ffn_g], name="mm_out")
    up_g, (l_down,) = _mm(h2, w_up_4, mode="nn", b_split=(0, 2), out_dtypes=[F32], name="mm_up_g",
                          comm=_ag_comm(sh_down, q=0, nq=2))
    up_v, (l_down,) = _mm(h2, w_up_4, mode="nn", b_split=(2, 2), out_dtypes=[F32], name="mm_up_v",
                          comm=_ag_comm(sh_down, l_down, q=1, nq=2))
    w_down_f = rows(whole(l_down))
    act = _ffn_fwd(up_g, up_v, cw_fg, cw_fv, cb)
    x2, (l_pg, l_pp) = _mm(act, w_down_f, mode="nn", out_dtypes=[F32], epi=add, extras=[x1], name="mm_down",
                           comm=_merge(_ag_comm(sh_pg), _ag_comm(sh_pp)))
    w_pg_f = rows(whole(l_pg))
    w_pp_4 = whole(l_pp)
    h3 = _rms_fwd(x2, norm_ple_g, "rms3")
    pp = _mm(ps, w_pp_4, mode="nn", b_split=(0, 4), out_dtypes=[F32], name="mm_pp")

    def ple_final_epi(acc, x2v, ppv, tv, gv):
        pg = _sigmoid(acc)
        x3v = x2v + pg * ppv
        r = lax.rsqrt(jnp.mean(x3v * x3v, axis=1, keepdims=True) + EPS)
        xh = x3v * r
        e = xh * gv - tv
        dy = e * (1.0 / D)
        dxh = dy * gv
        dx = r * (dxh - xh * jnp.mean(dxh * xh, axis=1, keepdims=True))
        dg = jnp.sum(dy * xh, axis=0, keepdims=True)
        ls = jnp.sum(e * e, axis=0, keepdims=True) * (0.5 / D)
        return dx, dx * ppv * pg * (1.0 - pg), dx * pg, jnp.concatenate([dg, ls, jnp.zeros((HALO - 2, D), F32)], axis=0)

    dx3, dpg, dpp, fin = _mm(h3, w_pg_f, mode="nn", out_dtypes=[F32, BF16, BF16], parts=1, epi=ple_final_epi,
                             extras=[x2, pp, tgt], rows=[gfin], name="mm_pg_final")
    fin = jnp.sum(fin.reshape(-1, HALO, D), axis=0)
    loss = lax.psum(jnp.sum(fin[1]), ("x", "y", "c"))
    d_gfin = fin[0:1]
    def split_rows(dW):
        return dW.reshape(4, dW.shape[0] // 4, dW.shape[1])

    chip_idx = chip.astype(jnp.int32).reshape(1)
    own_sum = lambda S1, B, name: _add_own(S1, B, chip_idx, cidx, "rs_" + name + "_sum")

    dW_pp = _mm(ps, dpp, mode="tn", out_split=4, out_dtypes=[F32], name="mm_dw_pp")
    dW_pg = _mm(h3, dpg, mode="tn", out_dtypes=[F32], name="mm_dw_pg")
    P_pp, P_pg = _halves(dW_pp), _halves(split_rows(dW_pg))
    def rms_bwd_epi(acc, xv, dr, gv):
        dxv, dg = _rms_bwd_math(acc, xv, gv)
        return dr + dxv, dr + dxv, _row0(dg)

    (dx2, dx2_b, d_gple), (A_pp, A_pg) = _mm(dpg, w_pg_f, mode="nt", out_dtypes=[F32, BF16], parts=1, epi=rms_bwd_epi,
                                             extras=[x2, dx3], rows=[norm_ple_g], name="mm_dh3_rms",
                                             comm=_merge(_swap_comm(P_pp), _swap_comm(P_pg)))
    d_gple = jnp.sum(d_gple.reshape(-1, HALO, D), axis=0)
    S_pp = _add_half([P_pp], [A_pp], cidx, "rs_w_pp_add")
    S_pg = _add_half([P_pg], [A_pg], cidx, "rs_w_pg_add")
    dW_down, (B_pp, B_pg) = _mm(act, dx2_b, mode="tn", out_dtypes=[F32], name="mm_dw_down",
                                comm=_merge(_a2a_comm(S_pp), _a2a_comm(S_pg)))
    P_down = _halves(split_rows(dW_down))
    dact, (A_down, F_pp, F_pg) = _mm(dx2_b, w_down_f, mode="nt", out_dtypes=[F32], name="mm_dact", comm=_merge(
        _swap_comm(P_down), _fill_comm(own_sum(S_pp, B_pp, "w_pp")), _fill_comm(own_sum(S_pg, B_pg, "w_pg"))))
    S_down = _add_half([P_down], [A_down], cidx, "rs_w_down_add")
    (dup_g, dup_v, dcw_fg, dcw_fv), (B_down,) = _ffn_bwd(dact, up_g, up_v, cw_fg, cw_fv, cb, comm=_a2a_comm(S_down))
    dW_up_g = _mm(h2, dup_g, mode="tn", out_split=2, out_dtypes=[F32], name="mm_dw_up_g")
    P_ug = _halves(dW_up_g)
    dW_up_v, (A_ug, F_down) = _mm(h2, dup_v, mode="tn", out_split=2, out_dtypes=[F32], name="mm_dw_up_v",
                                  comm=_merge(_swap_comm(P_ug), _fill_comm(own_sum(S_down, B_down, "w_down"))))
    P_uv = _halves(dW_up_v)
    dh2, (A_uv,) = _mm(dup_g, w_up_4, mode="nt", b_split=(0, 2), out_dtypes=[F32], name="mm_dh2_g", comm=_swap_comm(P_uv))
    S_up = _add_half([P_ug, P_uv], [A_ug, A_uv], cidx, "rs_w_up_add")
    dh2, (B_up,) = _mm(dup_v, w_up_4, mode="nt", b_split=(2, 2), out_dtypes=[F32], epi=add, extras=[dh2], name="mm_dh2_v",
                       comm=_a2a_comm(S_up, 0, 2))
    dx1, dx1_b, d_gffn = _rms_bwd(dh2, x1, norm_ffn_g, dx2, "rms2_bwd")
    P_out = _halves(split_rows(_mm(ymix, dx1_b, mode="tn", out_dtypes=[F32], name="mm_dw_out")))
    dymix, (A_out,) = _mm(dx1_b, w_out_f, mode="nt", out_dtypes=[F32], name="mm_dymix", comm=_swap_comm(P_out))
    S_out = _add_half([P_out], [A_out], cidx, "rs_w_out_add")
    dax, dab, dac, dcw_a = _ga_bwd(dymix, proj, cw_a, CW, cb)
    do, dz, d_gdn = _gnorm_bwd(dymix, CW // DNW, o, proj, z_coff, gdn_t, DNW)
    dqn, dkn, dvs, dgB, dbB, (B_up, B_out) = _delta_bwd(qn, kn, vs, g, beta, S0, inv_c, do,
                                                        comm=_merge(_a2a_comm(S_up, 1, 2, B_up), _a2a_comm(S_out)))
    dq_pre, dcw_q = _qkv_bwd(dqn, proj, cw_q, nq, True, DNW, cb, "q_bwd")
    dk_pre, dcw_k = _qkv_bwd(dkn, proj, cw_k, nq + nd, True, DNW, cb, "k_bwd")
    dv_pre, dcw_v = _qkv_bwd(dvs, proj, cw_v, nq + 2 * nd, False, DNW, cb, "v_bwd")
    dsmall, d_ab = _gb_bwd(dgB, dbB, small, g, beta, a_log_row, dt_row, H)
    dproj = jnp.concatenate([dax, dab, dac, dq_pre, dk_pre, dv_pre, dz], axis=1)
    dW_in_main, (F_up, F_out) = _mm(h1, dproj, mode="tn", out_dtypes=[F32], name="mm_dw_in", comm=_merge(
        _fill_comm(own_sum(S_up, B_up, "w_up")), _fill_comm(own_sum(S_out, B_out, "w_out"))))
    dW_in_small = _mm(h1, dsmall, mode="tn", out_dtypes=[F32], name="mm_dw_in_small")
    def update(Hf, w, m, v, name):
        gr = Hf.reshape(2 * Hf.shape[1], Hf.shape[2])
        if gr.shape[1] % LANES == 0:
            delta, m2, v2 = _adamw(w[0], gr, m[0], v[0], "adamw_" + name)
            return gr[None], delta[None], m2[None], v2[None]
        tr = jnp.transpose
        grt = tr(gr)
        delta, m2, v2 = _adamw(tr(w[0]), grt, tr(m[0]), tr(v[0]), "adamw_" + name)
        return tr(grt)[None], tr(delta)[None], tr(m2)[None], tr(v2)[None]

    P_in = _halves(_split_shards(dW_in_main, dW_in_small, IN_COLS // 4))
    (A_in,) = _run_comm(_swap_comm(P_in), "rs_w_in_swap")
    S_in = _add_half([P_in], [A_in], cidx, "rs_w_in_add")
    dh1, (B_in,) = _mm(dproj, w_in_main, mode="nt", out_dtypes=[F32], name="mm_dh1", comm=_a2a_comm(S_in))

    def rms1_epi(acc, dhv, xv, dr, gv):
        dxv, dg = _rms_bwd_math(acc + dhv, xv, gv)
        return dr + dxv, _row0(dg)

    dx, d_gmix = _mm(dsmall, w_in_small, mode="nt", out_dtypes=[F32], parts=1, epi=rms1_epi, extras=[dh1, xs, dx1],
                     rows=[norm_mix_g], name="mm_dh1_small_rms")
    d_gmix = jnp.sum(d_gmix.reshape(-1, HALO, D), axis=0)

    small_grads = [d_gmix[0:1], dcw_a[:cw_a.shape[0]], jnp.concatenate([dcw_q, dcw_k, dcw_v], axis=1)[:cw_qkv.shape[0]],
                   d_ab[0:1, :H], d_ab[1:2, :H], d_gdn[0:1], d_gffn[0:1],
                   jnp.concatenate([dcw_fg, dcw_fv], axis=1)[:cw_ffn.shape[0]], d_gple[0:1], d_gfin]
    small_shapes = [v.shape for v in small_grads]
    gpack = _pack_rows(small_grads)
    F_in, g8 = _run_comm(_merge(_fill_comm(own_sum(S_in, B_in, "w_in")), _gather_all_comm(gpack)), "rs_w_in_gather_small")
    big = {
        "w_in": update(F_in, w_in, m_w_in, v_w_in, "w_in"),
        "w_out": update(F_out, w_out, m_w_out, v_w_out, "w_out"),
        "w_up": update(F_up, w_up, m_w_up, v_w_up, "w_up"),
        "w_down": update(F_down, w_down, m_w_down, v_w_down, "w_down"),
        "w_ple_gate": update(F_pg, w_ple_gate, m_w_ple_gate, v_w_ple_gate, "w_pg"),
        "w_ple_proj": update(F_pp, w_ple_proj, m_w_ple_proj, v_w_ple_proj, "w_pp"),
    }

    gsum = _sum_stack(g8, "sum_small")
    (g_gmix, g_cwa, g_cwqkv, g_alog, g_dt, g_gdn, g_gffn, g_cwffn, g_gple, g_gfin) = _unpack_rows(gsum, small_shapes)

    def my_cols(v):
        Cc = v.shape[1] // 4
        return lax.dynamic_slice_in_dim(v, chip * Cc, Cc, axis=1)

    g_small = [g_gmix, my_cols(g_cwa), my_cols(g_cwqkv), g_alog, g_dt, g_gdn, g_gffn, my_cols(g_cwffn), g_gple, g_gfin]
    w_small = [norm_mix_g, conv_a_w[0], conv_qkv_w[0], a_log, dt_bias, dn_norm_g, norm_ffn_g, conv_ffn_w[0], norm_ple_g, gfin]
    m_small = [m_norm_mix_g, m_conv_a_w[0], m_conv_qkv_w[0], m_a_log, m_dt_bias, m_dn_norm_g, m_norm_ffn_g, m_conv_ffn_w[0],
               m_norm_ple_g, m_final_norm_g.reshape(1, D)]
    v_small = [v_norm_mix_g, v_conv_a_w[0], v_conv_qkv_w[0], v_a_log, v_dt_bias, v_dn_norm_g, v_norm_ffn_g, v_conv_ffn_w[0],
               v_norm_ple_g, v_final_norm_g.reshape(1, D)]
    shp = [v.shape for v in w_small]
    ds_, ms_, vs_ = _adamw(_pack_rows(w_small), _pack_rows(g_small), _pack_rows(m_small), _pack_rows(v_small), "adamw_small")
    out_shapes = [norm_mix_g.shape, conv_a_w.shape, conv_qkv_w.shape, a_log.shape, dt_bias.shape, dn_norm_g.shape,
                  norm_ffn_g.shape, conv_ffn_w.shape, norm_ple_g.shape, final_norm_g.shape]
    rs = lambda vals: [v.reshape(s) for v, s in zip(vals, out_shapes)]
    sg, sd_, sm_, sv_ = rs(g_small), rs(_unpack_rows(ds_, shp)), rs(_unpack_rows(ms_, shp)), rs(_unpack_rows(vs_, shp))
    names_small = ["norm_mix_g", "conv_a_w", "conv_qkv_w", "a_log", "dt_bias", "dn_norm_g", "norm_ffn_g", "conv_ffn_w",
                   "norm_ple_g", "final_norm_g"]
    res = {n: (sg[i], sd_[i], sm_[i], sv_[i]) for i, n in enumerate(names_small)}
    res.update(big)
    order = ["norm_mix_g", "w_in", "conv_a_w", "conv_qkv_w", "a_log", "dt_bias", "dn_norm_g", "w_out", "norm_ffn_g", "w_up",
             "conv_ffn_w", "w_down", "norm_ple_g", "w_ple_gate", "w_ple_proj", "final_norm_g"]
    return (loss, dx[None], *[res[n][0] for n in order], *[res[n][1] for n in order], *[res[n][2] for n in order],
            *[res[n][3] for n in order])
```

```python
import functools

import jax
import jax.numpy as jnp
from jax import lax
from jax.experimental import pallas as pl
from jax.experimental.pallas import tpu as pltpu

F32 = jnp.float32
BF16 = jnp.bfloat16
LANES = 128
HALO = 8
HEAD_DIM = 128
CHUNK = 64
EPS = 1e-6
VMEM_LIMIT = 56 * 1024 * 1024
MM_VMEM_BUDGET = 40 * 1024 * 1024
MM_STEP_BYTES = 1 << 20
EW_VMEM_BUDGET = 28 * 1024 * 1024
MESH = pl.DeviceIdType.MESH

ADAM_LR, ADAM_B1, ADAM_B2, ADAM_EPS, ADAM_WD, ADAM_STEP = 0.001, 0.9, 0.999, 1e-08, 0.01, 10


def _tile(n, cap, unit):
    if n <= cap:
        return n
    d = (cap // unit) * unit
    while d >= unit:
        if n % d == 0:
            return d
        d -= unit
    raise ValueError(f"no tile for {n} (cap {cap}, unit {unit})")


def _sigmoid(x):
    return 1.0 / (1.0 + jnp.exp(-x))


def _divisors(n, cap):
    ds = [d for d in range(cap // LANES * LANES, 0, -LANES) if n % d == 0]
    return [n] if (n <= cap or not ds) else ds


def _mm_tiles(M, N, K, n_unit, k_unit, a_bytes, n_blocks_mn, a_transposed, tn_full=False):
    best = None
    for tm in _divisors(M, 1536):
        for tn in ([N] if tn_full else _divisors(n_unit, 1536)):
            for tk in _divisors(k_unit, 4096):
                nk = K // tk
                vmem = 2 * tm * tk * a_bytes + 2 * tk * tn * 2 + 2 * 4 * tm * tn * n_blocks_mn + (4 * tm * tn if nk > 1 else 0)
                if vmem > MM_VMEM_BUDGET:
                    continue
                steps = (M // tm) * (N // tn) * nk
                b_reads = 1 if (nk == 1 and N == tn) else M // tm
                cost = (M * K * a_bytes * (N // tn if nk > 1 else 1) + K * N * 2 * b_reads + 4 * M * N * n_blocks_mn
                        + (8 * M * N * nk // 3 if nk > 1 else 0) + steps * MM_STEP_BYTES
                        + (2 * steps * tm * tk if a_transposed else 0))
                if best is None or cost < best[0]:
                    best = (cost, tm, tn, tk)
    return best[1:]


def _mm(a, b, *, mode, out_dtypes, name, epi=None, extras=(), comm=None, b_split=None, out_split=None, rows=(), parts=0):
    if b_split is not None:
        lo, ns = b_split
        Rb, Cb = b.shape[1], b.shape[2]
    if mode == "nn":
        (M, K), N = a.shape, (ns * Cb if b_split else b.shape[1])
    elif mode == "nt":
        (M, K), N = a.shape, (Rb if b_split else b.shape[0])
    else:
        (K, M), N = a.shape, b.shape[1]
    n_ex, n_out = len(extras), len(out_dtypes)
    n_unit = Cb if (b_split and mode == "nn") else (N // out_split if out_split else N)
    n_rows = len(rows)
    assert not (n_rows and (b_split or out_split))
    k_unit = Cb if (b_split and mode == "nt") else K
    mn_blocks = (sum(e.dtype.itemsize for e in extras) + sum(jnp.dtype(d).itemsize for d in out_dtypes)) / 4
    tm, tn, tk = _mm_tiles(M, N, K, n_unit, k_unit, a.dtype.itemsize, mn_blocks, mode == "tn", tn_full=bool(n_rows))
    nk = K // tk
    a_spec = pl.BlockSpec((tk, tm), lambda i, j, k: (k, i)) if mode == "tn" else pl.BlockSpec((tm, tk), lambda i, j, k: (i, k))
    if b_split and mode == "nn":
        nb = Cb // tn
        b_spec = pl.BlockSpec((None, tk, tn), lambda i, j, k: (lo + j // nb, k, j % nb))
    elif b_split:
        nb = Cb // tk
        b_spec = pl.BlockSpec((None, tn, tk), lambda i, j, k: (lo + k // nb, j, k % nb))
    else:
        b_spec = pl.BlockSpec((tn, tk), lambda i, j, k: (j, k)) if mode == "nt" else pl.BlockSpec((tk, tn), lambda i, j, k: (k, j))
    mn_spec = pl.BlockSpec((tm, tn), lambda i, j, k: (i, j))
    out_shapes = [jax.ShapeDtypeStruct((M, N), dt) for dt in out_dtypes] + [jax.ShapeDtypeStruct((M // tm * HALO, N), F32)] * parts
    out_specs = [mn_spec] * n_out + [pl.BlockSpec((HALO, tn), lambda i, j, k: (i, j))] * parts
    if out_split:
        assert n_ex == 0 and n_out == 1
        nbo = (N // out_split) // tn
        out_specs = [pl.BlockSpec((None, tm, tn), lambda i, j, k: (j // nbo, i, j % nbo))]
        out_shapes = [jax.ShapeDtypeStruct((out_split, M, N // out_split), out_dtypes[0])]
    dims = {"nn": (((1,), (0,)), ((), ())), "nt": (((1,), (1,)), ((), ())), "tn": (((0,), (0,)), ((), ()))}[mode]

    def body(*refs):
        a_ref, b_ref = refs[0], refs[1]
        ex_refs = refs[2:2 + n_ex + n_rows]
        out_refs = refs[2 + n_ex + n_rows:2 + n_ex + n_rows + n_out + parts]
        part = lax.dot_general(a_ref[...].astype(BF16), b_ref[...].astype(BF16), dims, preferred_element_type=F32)

        def finish(acc):
            outs = (acc,) if epi is None else epi(acc, *[r[...] for r in ex_refs])
            for r, o in zip(out_refs, outs):
                r[...] = o.astype(r.dtype)

        if nk == 1:
            finish(part)
            return
        acc_ref = refs[-1]
        k = pl.program_id(2)

        @pl.when(k == 0)
        def _():
            acc_ref[...] = part

        @pl.when(jnp.logical_and(k > 0, k < nk - 1))
        def _():
            acc_ref[...] += part

        @pl.when(k == nk - 1)
        def _():
            finish(acc_ref[...] + part)

    outs, comm_outs = _call(
        body, name=name, grid=(M // tm, N // tn, nk),
        in_specs=[a_spec, b_spec] + [mn_spec] * n_ex + [pl.BlockSpec((1, tn), lambda i, j, k: (0, j))] * n_rows,
        out_specs=out_specs,
        out_shape=out_shapes,
        scratch_shapes=[pltpu.VMEM((tm, tn), F32)] if nk > 1 else [],
        semantics=("parallel", "parallel", "arbitrary"), args=(a, b, *extras, *rows), comm=comm)
    res = outs[0] if n_out + parts == 1 else outs
    return res if comm is None else (res, comm_outs)


def _tiled(fn, *, T, C, ins, out_dtypes=(), acc_rows=(), tb=None, cb=512, name, comm=None, into=None):
    tb = _tile(T, tb or (1024 if cb <= 512 else 512 if cb <= 1024 else 256), HALO)
    nI, nJ = T // tb, C // cb
    hb, nH = tb // HALO, T // HALO
    specs, args, kinds = [], [], []
    for kind, arr, cmap in ins:
        cm = cmap if cmap is not None else (lambda j: j)
        kinds.append(kind)
        if kind == "cur":
            specs.append(pl.BlockSpec((tb, cb), lambda j, i, cm=cm: (i, cm(j))))
            args.append(arr)
        elif kind == "ext":
            specs.append(pl.BlockSpec((HALO, cb), lambda j, i, cm=cm: (jnp.maximum(i * hb - 1, 0), cm(j))))
            specs.append(pl.BlockSpec((tb, cb), lambda j, i, cm=cm: (i, cm(j))))
            specs.append(pl.BlockSpec((HALO, cb), lambda j, i, cm=cm: (jnp.minimum((i + 1) * hb, nH - 1), cm(j))))
            args += [arr, arr, arr]
        elif kind == "row":
            specs.append(pl.BlockSpec((arr.shape[0], cb), lambda j, i, cm=cm: (0, cm(j))))
            args.append(arr)
        elif kind == "stack":
            specs.append(pl.BlockSpec((arr.shape[0], tb, cb), lambda j, i, cm=cm: (0, i, cm(j))))
            args.append(arr)
        else:
            raise ValueError(kind)
    n_in = len(args)
    n_out, n_acc = len(out_dtypes), len(acc_rows)

    def body(*refs):
        j, i = pl.program_id(0), pl.program_id(1)
        vals, r = [], 0
        for kind in kinds:
            if kind == "ext":
                prev = jnp.where(i == 0, 0.0, refs[r][...].astype(F32))
                cur = refs[r + 1][...].astype(F32)
                nxt = jnp.where(i == nI - 1, 0.0, refs[r + 2][...].astype(F32))
                vals.append(jnp.concatenate([prev, cur, nxt], axis=0))
                r += 3
            else:
                vals.append(refs[r][...])
                r += 1
        res = fn(j, i, *vals)
        for ref, o in zip(refs[n_in:n_in + n_out], res[:n_out]):
            ref[...] = o.astype(ref.dtype)
        for ref, o in zip(refs[n_in + n_out:], res[n_out:]):
            @pl.when(i == 0)
            def _(ref=ref, o=o):
                ref[...] = o

            @pl.when(i > 0)
            def _(ref=ref, o=o):
                ref[...] += o

    out_specs = [pl.BlockSpec((tb, cb), lambda j, i: (i, j))] * n_out
    out_shape = [jax.ShapeDtypeStruct((T, C), dt) for dt in out_dtypes]
    io_aliases = None
    if into is not None:
        buf, total, off = into
        assert n_out == 1 and comm is None
        out_specs = [pl.BlockSpec((tb, cb), lambda j, i: (i, j + off))]
        out_shape = [jax.ShapeDtypeStruct((T, total), out_dtypes[0])]
        if buf is not None:
            specs, args, io_aliases = specs + [ANY], args + [buf], {n_in: 0}
            n_in += 1
    outs, comm_outs = _call(
        body, name=name, grid=(nJ, nI), in_specs=specs,
        out_specs=out_specs + [pl.BlockSpec((rows, cb), lambda j, i: (0, j)) for rows in acc_rows],
        out_shape=out_shape + [jax.ShapeDtypeStruct((rows, C), F32) for rows in acc_rows],
        scratch_shapes=[], semantics=("parallel", "arbitrary"), args=args, comm=comm, io_aliases=io_aliases)
    return outs if comm is None else (outs, comm_outs)


def _conv_causal(xe, w):
    K = w.shape[0]
    y = xe * w[K - 1:K]
    for j in range(K - 1):
        y = y + pltpu.roll(xe, K - 1 - j, 0) * w[j:j + 1]
    return y


def _conv_anti(de, w):
    K, n = w.shape[0], de.shape[0]
    y = de * w[K - 1:K]
    for j in range(K - 1):
        y = y + pltpu.roll(de, n - (K - 1 - j), 0) * w[j:j + 1]
    return y


def _conv_dw(dce, xe, K):
    n = dce.shape[0]
    tb = n - 2 * HALO
    rows = []
    for j in range(K):
        xs = xe if j == K - 1 else pltpu.roll(xe, K - 1 - j, 0)
        rows.append(jnp.sum((dce * xs)[HALO:HALO + tb], axis=0, keepdims=True))
    rows.append(jnp.zeros((HALO - K, dce.shape[1]), F32))
    return jnp.concatenate(rows, axis=0)


def _own(xe):
    return xe[HALO:xe.shape[0] - HALO]


def _row0(v):
    return jnp.concatenate([v, jnp.zeros((HALO - 1, v.shape[1]), F32)], axis=0)


def _per_head(fn, *xs):
    n = xs[0].shape[1] // HEAD_DIM
    outs = [fn(*[x[:, g * HEAD_DIM:(g + 1) * HEAD_DIM] for x in xs]) for g in range(n)]
    return outs[0] if n == 1 else jnp.concatenate(outs, axis=1)


def _rms_fwd(x, g, name):
    T, D = x.shape

    def fn(j, i, xv, gv):
        r = lax.rsqrt(jnp.mean(xv * xv, axis=1, keepdims=True) + EPS)
        return (xv * r * gv,)

    return _tiled(fn, T=T, C=D, ins=[("cur", x, None), ("row", g, None)], out_dtypes=[BF16], cb=D, name=name)[0]


def _rms_bwd_math(dy, xv, gv):
    r = lax.rsqrt(jnp.mean(xv * xv, axis=1, keepdims=True) + EPS)
    xh = xv * r
    dxh = dy * gv
    dx = r * (dxh - xh * jnp.mean(dxh * xh, axis=1, keepdims=True))
    dg = jnp.sum(dy * xh, axis=0, keepdims=True)
    return dx, dg


def _rms_bwd(dh, x, g, dres, name, comm=None):
    T, D = x.shape

    def fn(j, i, dhv, xv, gv, dr):
        dx, dg = _rms_bwd_math(dhv, xv, gv)
        return dr + dx, dr + dx, _row0(dg)

    return _tiled(fn, T=T, C=D, ins=[("cur", dh, None), ("cur", x, None), ("row", g, None), ("cur", dres, None)],
                  out_dtypes=[F32, BF16], acc_rows=[HALO], cb=D, name=name, comm=comm)


def _ga_fwd(proj, w_a, CW, cb, total):
    T = proj.shape[0]
    n = CW // cb

    def fn(j, i, ax, ab, ac, w):
        c = _conv_causal(ac * ax, w)
        return (ab * _own(c),)

    return _tiled(fn, T=T, C=CW, ins=[("ext", proj, None), ("cur", proj, lambda j: j + n), ("ext", proj, lambda j: j + 2 * n),
                                       ("row", w_a, None)], out_dtypes=[BF16], cb=cb, name="ga_fwd", into=(None, total, 0))[0]


def _ga_bwd(dymix, proj, w_a, CW, cb):
    T = proj.shape[0]
    n = CW // cb
    K = w_a.shape[0]

    def fn(j, i, dy, ax, ab, ac, w):
        u = ac * ax
        c = _conv_causal(u, w)
        dc = dy * ab
        du = _conv_anti(dc, w)
        return _own(du * ac), _own(dy * c), _own(du * ax), _conv_dw(dc, u, K)

    return _tiled(fn, T=T, C=CW, ins=[("ext", dymix, None), ("ext", proj, None), ("ext", proj, lambda j: j + n),
                                       ("ext", proj, lambda j: j + 2 * n), ("row", w_a, None)],
                  out_dtypes=[BF16, BF16, BF16], acc_rows=[HALO], cb=cb, name="ga_bwd")


def _l2n(s):
    return s * lax.rsqrt(jnp.sum(s * s, axis=1, keepdims=True) + EPS)


def _qkv_fwd(proj, w_sec, coff, normalize, DNW, cb, name, comm=None):
    T = proj.shape[0]

    def fn(j, i, pre, w):
        c = _own(_conv_causal(pre, w))
        s = c * _sigmoid(c)
        if normalize is True or normalize is False:
            return (_per_head(_l2n, s) if normalize else s,)
        return (jnp.where(j < normalize, _per_head(_l2n, s), s),)

    res = _tiled(fn, T=T, C=DNW, ins=[("ext", proj, lambda j: j + coff), ("row", w_sec, None)],
                 out_dtypes=[F32], cb=cb, name=name, comm=comm)
    return res[0] if comm is None else (res[0][0], res[1])


def _qkv_bwd(dsec, proj, w_sec, coff, normalize, DNW, cb, name):
    T = proj.shape[0]
    K = w_sec.shape[0]

    def l2n_bwd(s, dn):
        r = lax.rsqrt(jnp.sum(s * s, axis=1, keepdims=True) + EPS)
        nrm = s * r
        return r * (dn - nrm * jnp.sum(dn * nrm, axis=1, keepdims=True))

    def fn(j, i, dn, pre, w):
        c = _conv_causal(pre, w)
        sg = _sigmoid(c)
        s = c * sg
        ds = _per_head(l2n_bwd, s, dn) if normalize else dn
        dc = ds * (sg * (1.0 + c * (1.0 - sg)))
        return _own(_conv_anti(dc, w)), _conv_dw(dc, pre, K)

    return _tiled(fn, T=T, C=DNW, ins=[("ext", dsec, None), ("ext", proj, lambda j: j + coff), ("row", w_sec, None)],
                  out_dtypes=[BF16], acc_rows=[HALO], cb=cb, name=name)


def _gb_fwd(small, a_log_row, dt_row, H):
    T = small.shape[0]

    def fn(j, i, sm, al, dt):
        z = sm + dt
        sp = jnp.maximum(z, 0.0) + jnp.log(1.0 + jnp.exp(-jnp.abs(z)))
        g = -jnp.exp(al) * sp
        beta = _sigmoid(pltpu.roll(sm, LANES - H, 1))
        return g, beta

    return _tiled(fn, T=T, C=LANES, ins=[("cur", small, None), ("row", a_log_row, None), ("row", dt_row, None)],
                  out_dtypes=[F32, F32], cb=LANES, name="gb_fwd")


def _gb_bwd(dgB, dbB, small, g, beta, a_log_row, dt_row, H):
    T = small.shape[0]

    def fn(j, i, dgv, dbv, sm, gv, bv, al, dt):
        lane = lax.broadcasted_iota(jnp.int32, sm.shape, 1)
        dg = jnp.zeros(sm.shape, F32)
        db = jnp.zeros(sm.shape, F32)
        for h in range(H):
            dg = jnp.where(lane == h, jnp.sum(dgv[h], axis=1, keepdims=True), dg)
            db = jnp.where(lane == h, jnp.sum(dbv[h], axis=1, keepdims=True), db)
        da = dg * (-jnp.exp(al)) * _sigmoid(sm + dt)
        dbb = db * bv * (1.0 - bv)
        dsm = jnp.where(lane < H, da, 0.0) + pltpu.roll(jnp.where(lane < H, dbb, 0.0), H, 1)
        d_alog = jnp.sum(jnp.where(lane < H, dg * gv, 0.0), axis=0, keepdims=True)
        d_dt = jnp.sum(jnp.where(lane < H, da, 0.0), axis=0, keepdims=True)
        return dsm, jnp.concatenate([d_alog, d_dt, jnp.zeros((HALO - 2, LANES), F32)], axis=0)

    return _tiled(fn, T=T, C=LANES, ins=[("stack", dgB, None), ("stack", dbB, None), ("cur", small, None), ("cur", g, None),
                                          ("cur", beta, None), ("row", a_log_row, None), ("row", dt_row, None)],
                  out_dtypes=[BF16], acc_rows=[HALO], cb=LANES, name="gb_bwd")


_DIMS = {"nn": (((1,), (0,)), ((), ())), "nt": (((1,), (1,)), ((), ())), "tn": (((0,), (0,)), ((), ()))}
_DOT_BWD = {"nn": (("nt", "gb"), ("tn", "ag")), "nt": (("nn", "gb"), ("tn", "ga")), "tn": (("nt", "bg"), ("nn", "ag"))}


def _split(a):
    hi = a.astype(BF16)
    return hi, (a - hi.astype(F32)).astype(BF16)


def _raw_dot(a, b, kind, passes):
    dg = lambda x, y: lax.dot_general(x, y, _DIMS[kind], preferred_element_type=F32)
    if passes == 1:
        return dg(a.astype(BF16), b.astype(BF16))
    ah, al = _split(a)
    bh, bl = _split(b)
    if kind == "tn":
        return dg(ah, bh) + (dg(ah, bl) + dg(al, bh))
    m = a.shape[0]
    top = dg(jnp.concatenate([ah, al], axis=0), bh)
    return top[:m] + (dg(ah, bl) + top[m:])


def _raw_dot_exact(a, b, kind, exact):
    dg = lambda x, y: lax.dot_general(x, y, _DIMS[kind], preferred_element_type=F32)
    if exact == "a":
        bh, bl = _split(b)
        return dg(a.astype(BF16), bh) + dg(a.astype(BF16), bl)
    ah, al = _split(a)
    return dg(ah, b.astype(BF16)) + dg(al, b.astype(BF16))


@functools.lru_cache(maxsize=None)
def _dotc(kind):
    @jax.custom_vjp
    def f(a, b):
        return _raw_dot_exact(a, b, kind, "a")

    def fwd(a, b):
        return _raw_dot_exact(a, b, kind, "a"), a

    def bwd(a, g):
        db = _raw_dot_exact(a, g, "tn", "a") if kind == "nn" else _raw_dot_exact(g, a, "tn", "b")
        return jnp.zeros_like(a), db

    f.defvjp(fwd, bwd)
    return f


@functools.lru_cache(maxsize=None)
def _dotf(kind, passes):
    @jax.custom_vjp
    def f(a, b):
        return _raw_dot(a, b, kind, passes)

    def fwd(a, b):
        return _raw_dot(a, b, kind, passes), (a, b)

    def bwd(res, g):
        ops = {"a": res[0], "b": res[1], "g": g}
        (ka, oa), (kb, ob) = _DOT_BWD[kind]
        return (_raw_dot(ops[oa[0]], ops[oa[1]], ka, passes), _raw_dot(ops[ob[0]], ops[ob[1]], kb, passes))

    f.defvjp(fwd, bwd)
    return f


@jax.custom_vjp
def _saved_inverse(L, inv):
    return inv


def _saved_inverse_fwd(L, inv):
    return inv, inv


def _saved_inverse_bwd(inv, g):
    d3nt, d3tn = _dotf("nt", 3), _dotf("tn", 3)
    return -d3nt(d3tn(inv, g), inv), jnp.zeros_like(inv)


_saved_inverse.defvjp(_saved_inverse_fwd, _saved_inverse_bwd)


def _chunk_fn(q, k, v, gB, bB, S, inv_saved=None):
    C = CHUNK
    d3 = _dotf("nn", 3)
    d1, d1nt, d1tn = _dotf("nn", 1), _dotf("nt", 1), _dotf("tn", 1)
    each = lambda f, *ls: tuple(f(*xs) for xs in zip(*ls))
    row = lax.broadcasted_iota(jnp.int32, (C, C), 0)
    col = lax.broadcasted_iota(jnp.int32, (C, C), 1)
    causal = row >= col
    strict = row > col
    tril = jnp.where(causal, 1.0, 0.0).astype(F32)
    eye = jnp.where(row == col, 1.0, 0.0).astype(F32)
    avg = jnp.full((C, HEAD_DIM), 1.0 / HEAD_DIM, F32)
    gc = each(lambda g: _dotc("nn")(tril, g), gB)
    R = each(lambda g: _dotc("nt")(avg, g), gc)
    decay = each(lambda g, r: jnp.where(causal, jnp.exp(jnp.where(causal, g[:, :C] - r, 0.0)), 0.0), gc, R)
    kk = each(lambda x: d1nt(x, x), k)
    L = each(lambda a, d, b: jnp.where(strict, a * d * b[:, :C], 0.0), kk, decay, bB)
    if inv_saved is None:
        inv = each(lambda l: eye - l, L)
        P = L
        for _ in range(5):
            P = each(lambda p: d3(p, p), P)
            inv = each(lambda a, p: d3(a, eye + p), inv, P)
    else:
        inv = each(_saved_inverse, L, inv_saved)
    eg = each(jnp.exp, gc)
    u = each(lambda a, x, b: d3(a, x * b), inv, v, bB)
    w = each(lambda a, x, b, e: d3(a, x * b * e), inv, k, bB, eg)
    qs = each(lambda x: x * (HEAD_DIM ** -0.5), q)
    qk = each(lambda a, x, d: d1nt(a, x) * d, qs, k, decay)
    gl = each(lambda g: g[C - 1:C, :], gc)
    kd = each(lambda x, a, g: x * jnp.exp(a - g), k, gl, gc)
    qe = each(lambda a, e: a * e, qs, eg)
    nh = len(S)
    o = ()
    for c in range(len(q) // nh):
        sl = slice(c * nh, (c + 1) * nh)
        v_new = each(lambda a, b, s: a - d1(b, s), u[sl], w[sl], S)
        o1 = each(lambda a, s: d1(a, s), qe[sl], S)
        o += each(lambda a, b, vn: a + d1(b, vn), o1, qk[sl], v_new)
        kv = each(lambda x, vn: d1tn(x, vn), kd[sl], v_new)
        S = each(lambda s, a, b: s * jnp.exp(a) + b, S, gl[sl], kv)
    return (o, S), inv


def _sel_lane(x, h):
    lane = lax.broadcasted_iota(jnp.int32, x.shape, 1)
    return jnp.broadcast_to(jnp.sum(jnp.where(lane == h, x, 0.0), axis=1, keepdims=True), x.shape)


def _tile_of(ref, c, h):
    return ref[c * CHUNK:(c + 1) * CHUNK, h * HEAD_DIM:(h + 1) * HEAD_DIM]


def _chunks_per_step(N):
    return 4 if N % 4 == 0 else (2 if N % 2 == 0 else 1)


def _delta_fwd(q, k, v, g, beta, comm=None):
    T = q.shape[0]
    one = q is k
    H, N = q.shape[1] // (3 * HEAD_DIM if one else HEAD_DIM), T // CHUNK
    cps = _chunks_per_step(N)
    rows = cps * CHUNK

    def body(q_ref, k_ref, v_ref, g_ref, b_ref, o_ref, s_ref, inv_ref, S):
        @pl.when(pl.program_id(0) == 0)
        def _():
            S[...] = jnp.zeros_like(S)

        gv, bv = g_ref[...], b_ref[...]
        pairs = lambda f: tuple(f(c, h) for c in range(cps) for h in range(H))
        S_in = tuple(S[h] for h in range(H))
        for h in range(H):
            s_ref[h, 0] = S_in[h]
        (o, S_new), inv = _chunk_fn(pairs(lambda c, h: _tile_of(q_ref, c, h)), pairs(lambda c, h: _tile_of(k_ref, c, h)),
                                    pairs(lambda c, h: _tile_of(v_ref, c, h)),
                                    pairs(lambda c, h: _sel_lane(gv[c * CHUNK:(c + 1) * CHUNK], h)),
                                    pairs(lambda c, h: _sel_lane(bv[c * CHUNK:(c + 1) * CHUNK], h)), S_in)
        for c in range(cps):
            for h in range(H):
                o_ref[c * CHUNK:(c + 1) * CHUNK, h * HEAD_DIM:(h + 1) * HEAD_DIM] = o[c * H + h]
                inv_ref[h, c] = inv[c * H + h]
        for h in range(H):
            S[h] = S_new[h]

    blk = pl.BlockSpec((rows, H * HEAD_DIM), lambda n: (n, 0))
    gblk = pl.BlockSpec((rows, LANES), lambda n: (n, 0))
    outs, comm_outs = _call(
        body, name="delta_fwd", grid=(N // cps,),
        in_specs=[blk] + [pl.BlockSpec((rows, H * HEAD_DIM), lambda n, cc=cc: (n, cc if one else 0)) for cc in (1, 2)] + [gblk, gblk],
        out_specs=[blk, pl.BlockSpec((H, 1, HEAD_DIM, HEAD_DIM), lambda n: (0, n, 0, 0)),
                   pl.BlockSpec((H, cps, CHUNK, CHUNK), lambda n: (0, n, 0, 0))],
        out_shape=[jax.ShapeDtypeStruct((T, H * HEAD_DIM), F32), jax.ShapeDtypeStruct((H, N // cps, HEAD_DIM, HEAD_DIM), F32),
                   jax.ShapeDtypeStruct((H, N, CHUNK, CHUNK), F32)],
        scratch_shapes=[pltpu.VMEM((H, HEAD_DIM, HEAD_DIM), F32)],
        semantics=("arbitrary",), args=(q, k, v, g, beta), comm=comm)
    return outs[0], outs[1], outs[2], comm_outs


def _delta_bwd(q, k, v, g, beta, S0, inv, do, comm=None):
    T = q.shape[0]
    one = q is k
    H, N = q.shape[1] // (3 * HEAD_DIM if one else HEAD_DIM), T // CHUNK
    cps = _chunks_per_step(N)
    rows, NS = cps * CHUNK, N // cps

    def body(q_ref, k_ref, v_ref, g_ref, b_ref, s_ref, inv_ref, do_ref, dq_ref, dk_ref, dv_ref, dg_ref, db_ref, dS):
        @pl.when(pl.program_id(0) == 0)
        def _():
            dS[...] = jnp.zeros_like(dS)

        gv, bv = g_ref[...], b_ref[...]
        pairs = lambda f: tuple(f(c, h) for c in range(cps) for h in range(H))
        heads = lambda f: tuple(f(h) for h in range(H))
        _, vjp, _ = jax.vjp(_chunk_fn, pairs(lambda c, h: _tile_of(q_ref, c, h)), pairs(lambda c, h: _tile_of(k_ref, c, h)),
                            pairs(lambda c, h: _tile_of(v_ref, c, h)),
                            pairs(lambda c, h: _sel_lane(gv[c * CHUNK:(c + 1) * CHUNK], h)),
                            pairs(lambda c, h: _sel_lane(bv[c * CHUNK:(c + 1) * CHUNK], h)),
                            heads(lambda h: s_ref[h, 0]), pairs(lambda c, h: inv_ref[h, c]), has_aux=True)
        dq, dk, dv, dgB, dbB, dS_prev, _ = vjp((pairs(lambda c, h: _tile_of(do_ref, c, h)), heads(lambda h: dS[h])))
        for c in range(cps):
            for h in range(H):
                r, sl = slice(c * CHUNK, (c + 1) * CHUNK), slice(h * HEAD_DIM, (h + 1) * HEAD_DIM)
                dq_ref[r, sl] = dq[c * H + h]
                dk_ref[r, sl] = dk[c * H + h]
                dv_ref[r, sl] = dv[c * H + h]
                dg_ref[h, r] = dgB[c * H + h]
                db_ref[h, r] = dbB[c * H + h]
        for h in range(H):
            dS[h] = dS_prev[h]

    blk = pl.BlockSpec((rows, H * HEAD_DIM), lambda n: (NS - 1 - n, 0))
    gblk = pl.BlockSpec((rows, LANES), lambda n: (NS - 1 - n, 0))
    hblk = pl.BlockSpec((H, rows, LANES), lambda n: (0, NS - 1 - n, 0))
    sd = jax.ShapeDtypeStruct
    outs, comm_outs = _call(
        body, name="delta_bwd", grid=(NS,),
        in_specs=[blk] + [pl.BlockSpec((rows, H * HEAD_DIM), lambda n, cc=cc: (NS - 1 - n, cc if one else 0)) for cc in (1, 2)]
        + [gblk, gblk, pl.BlockSpec((H, 1, HEAD_DIM, HEAD_DIM), lambda n: (0, NS - 1 - n, 0, 0)),
                  pl.BlockSpec((H, cps, CHUNK, CHUNK), lambda n: (0, NS - 1 - n, 0, 0)), blk],
        out_specs=[blk, blk, blk, hblk, hblk],
        out_shape=[sd((T, H * HEAD_DIM), F32)] * 3 + [sd((H, T, LANES), F32)] * 2,
        scratch_shapes=[pltpu.VMEM((H, HEAD_DIM, HEAD_DIM), F32)],
        semantics=("arbitrary",), args=(q, k, v, g, beta, S0, inv, do), comm=comm)
    return (*outs, comm_outs)


def _gnorm_fwd(o, proj, z_coff, gdn_t, DNW, buf, coff):
    T = o.shape[0]

    def fn(j, i, ov, zv, gv):
        def one(oh, zh, gh):
            r = lax.rsqrt(jnp.mean(oh * oh, axis=1, keepdims=True) + EPS)
            return oh * r * gh * (zh * _sigmoid(zh))
        return (_per_head(one, ov, zv, jnp.broadcast_to(gv, ov.shape)),)

    return _tiled(fn, T=T, C=DNW, ins=[("cur", o, None), ("cur", proj, lambda j: j + z_coff), ("row", gdn_t, None)],
                  out_dtypes=[BF16], cb=DNW, name="gnorm_fwd", into=(buf, buf.shape[1], coff))[0]


def _gnorm_bwd(dymix, y_coff, o, proj, z_coff, gdn_t, DNW):
    T = o.shape[0]
    nh = DNW // HEAD_DIM

    def fn(j, i, dy, ov, zv, gv):
        dos, dzs, dgs = [], [], jnp.zeros((1, HEAD_DIM), F32)
        for h in range(nh):
            sl = slice(h * HEAD_DIM, (h + 1) * HEAD_DIM)
            dyh, oh, zh, gh = dy[:, sl].astype(F32), ov[:, sl], zv[:, sl], gv[:, sl]
            r = lax.rsqrt(jnp.mean(oh * oh, axis=1, keepdims=True) + EPS)
            on = oh * r
            sg = _sigmoid(zh)
            sz = zh * sg
            dzs.append(dyh * on * gh * (sg * (1.0 + zh * (1.0 - sg))))
            don = dyh * gh * sz
            dos.append(r * (don - on * jnp.mean(don * on, axis=1, keepdims=True)))
            dgs = dgs + jnp.sum(dyh * on * sz, axis=0, keepdims=True)
        cat = (lambda xs: xs[0] if nh == 1 else jnp.concatenate(xs, axis=1))
        return cat(dos), cat(dzs), _row0(dgs)

    T_ = T
    nI = T_ // _tile(T_, 256, HALO)
    tb = T_ // nI
    specs_cb = DNW

    def body_wrap():
        def body(dy_ref, o_ref, z_ref, g_ref, do_ref, dz_ref, dg_ref):
            i = pl.program_id(0)
            d_o, d_z, d_g = fn(0, i, dy_ref[...], o_ref[...], z_ref[...], g_ref[...])
            do_ref[...] = d_o
            dz_ref[...] = d_z.astype(dz_ref.dtype)

            @pl.when(i == 0)
            def _():
                dg_ref[...] = d_g

            @pl.when(i > 0)
            def _():
                dg_ref[...] += d_g

        return pl.pallas_call(
            body, name="gnorm_bwd", grid=(nI,),
            in_specs=[pl.BlockSpec((tb, specs_cb), lambda i: (i, y_coff)), pl.BlockSpec((tb, specs_cb), lambda i: (i, 0)),
                      pl.BlockSpec((tb, specs_cb), lambda i: (i, z_coff)), pl.BlockSpec((1, specs_cb), lambda i: (0, 0))],
            out_specs=[pl.BlockSpec((tb, specs_cb), lambda i: (i, 0)), pl.BlockSpec((tb, specs_cb), lambda i: (i, 0)),
                       pl.BlockSpec((HALO, HEAD_DIM), lambda i: (0, 0))],
            out_shape=[jax.ShapeDtypeStruct((T_, DNW), F32), jax.ShapeDtypeStruct((T_, DNW), BF16),
                       jax.ShapeDtypeStruct((HALO, HEAD_DIM), F32)],
            compiler_params=pltpu.CompilerParams(dimension_semantics=("arbitrary",), vmem_limit_bytes=VMEM_LIMIT),
        )(dymix, o, proj, gdn_t)

    return body_wrap()


def _ffn_fwd(up_g, up_v, w_g, w_v, cb):
    T, F = up_g.shape

    def fn(j, i, ug, uv, wg, wv):
        cg = _own(_conv_causal(ug, wg))
        cv = _own(_conv_causal(uv, wv))
        return (cg * _sigmoid(cg) * cv,)

    return _tiled(fn, T=T, C=F, ins=[("ext", up_g, None), ("ext", up_v, None), ("row", w_g, None), ("row", w_v, None)],
                  out_dtypes=[BF16], tb=1024, cb=cb, name="ffn_fwd")[0]


def _ffn_bwd(dact, up_g, up_v, w_g, w_v, cb):
    T, F = up_g.shape
    K = w_g.shape[0]

    def fn(j, i, da, ug, uv, wg, wv):
        cg = _conv_causal(ug, wg)
        cv = _conv_causal(uv, wv)
        sg = _sigmoid(cg)
        dgate = da * cv * (sg * (1.0 + cg * (1.0 - sg)))
        dval = da * (cg * sg)
        return (_own(_conv_anti(dgate, wg)), _own(_conv_anti(dval, wv)), _conv_dw(dgate, ug, K), _conv_dw(dval, uv, K))

    return _tiled(fn, T=T, C=F, ins=[("ext", dact, None), ("ext", up_g, None), ("ext", up_v, None), ("row", w_g, None),
                                      ("row", w_v, None)], out_dtypes=[BF16, BF16], acc_rows=[HALO, HALO], tb=1024, cb=cb,
                  name="ffn_bwd")


def _wide(R, Cc, n_f32, unit=HALO):
    cb = Cc if (Cc % LANES or Cc <= 4096) else _tile(Cc, 2048, LANES)
    cap = max(unit, EW_VMEM_BUDGET // (2 * 4 * n_f32 * cb) // unit * unit)
    return _tile(R, cap, unit), cb


def _adamw(w, g, m, v, name, comm=None):
    R, Cc = w.shape
    tb, cb = _wide(R, Cc, 7) if R % HALO == 0 else (R, _tile(Cc, EW_VMEM_BUDGET // (2 * 4 * 7 * R) // LANES * LANES, LANES))
    c1 = 1.0 / (1.0 - ADAM_B1 ** ADAM_STEP)
    c2 = 1.0 / (1.0 - ADAM_B2 ** ADAM_STEP)

    def fn(j, i, wv, gv, mv, vv):
        m2 = ADAM_B1 * mv + (1.0 - ADAM_B1) * gv
        v2 = ADAM_B2 * vv + (1.0 - ADAM_B2) * (gv * gv)
        delta = -ADAM_LR * ((m2 * c1) / (jnp.sqrt(v2 * c2) + ADAM_EPS) + ADAM_WD * wv)
        return delta, m2, v2

    return _tiled(fn, T=R, C=Cc, ins=[("cur", w, None), ("cur", g, None), ("cur", m, None), ("cur", v, None)],
                  out_dtypes=[F32, F32, F32], tb=tb, cb=cb, name=name, comm=comm)


def _join_shards(w4, n_main):
    S4, R, cs = w4.shape
    n_small = S4 * cs - n_main
    assert 0 < n_small <= LANES and n_small <= cs
    tb = _tile(R, 256, 2 * HALO)

    def body(w_ref, main_ref, small_ref):
        for t in range(S4 - 1):
            main_ref[:, t * cs:(t + 1) * cs] = w_ref[t]
        last = w_ref[S4 - 1]
        main_ref[:, (S4 - 1) * cs:] = last[:, :cs - n_small]
        small_ref[...] = jnp.zeros_like(small_ref)
        small_ref[:, :n_small] = last[:, cs - n_small:]

    return pl.pallas_call(
        body, name="join_w_in", grid=(R // tb,), in_specs=[pl.BlockSpec((S4, tb, cs), lambda i: (0, i, 0))],
        out_specs=[pl.BlockSpec((tb, n_main), lambda i: (i, 0)), pl.BlockSpec((tb, LANES), lambda i: (i, 0))],
        out_shape=[jax.ShapeDtypeStruct((R, n_main), w4.dtype), jax.ShapeDtypeStruct((R, LANES), w4.dtype)],
        compiler_params=pltpu.CompilerParams(dimension_semantics=("parallel",), vmem_limit_bytes=VMEM_LIMIT))(w4)


def _split_shards(main, small, cs):
    R, n_main = main.shape
    n_small = 4 * cs - n_main
    tb = _tile(R, 256, HALO)

    def body(main_ref, small_ref, out_ref):
        for t in range(3):
            out_ref[t] = main_ref[:, t * cs:(t + 1) * cs]
        out_ref[3, :, :cs - n_small] = main_ref[:, 3 * cs:]
        out_ref[3, :, cs - n_small:] = small_ref[:, :n_small]

    return pl.pallas_call(
        body, name="split_g_in", grid=(R // tb,),
        in_specs=[pl.BlockSpec((tb, n_main), lambda i: (i, 0)), pl.BlockSpec((tb, LANES), lambda i: (i, 0))],
        out_specs=pl.BlockSpec((4, tb, cs), lambda i: (0, i, 0)), out_shape=jax.ShapeDtypeStruct((4, R, cs), main.dtype),
        compiler_params=pltpu.CompilerParams(dimension_semantics=("parallel",), vmem_limit_bytes=VMEM_LIMIT))(main, small)


def _sum_stack(st, name):
    S, R, Cc = st.shape
    cb = _tile(Cc, 512, LANES) if Cc % LANES == 0 else Cc

    def fn(j, i, sv):
        t = sv[0]
        for s in range(1, S):
            t = t + sv[s]
        return (t,)

    return _tiled(fn, T=R, C=Cc, ins=[("stack", st, None)], out_dtypes=[F32], cb=cb, name=name)[0]


ANY = pl.BlockSpec(memory_space=pl.ANY)


def _place():
    x, y, c = lax.axis_index("x"), lax.axis_index("y"), lax.axis_index("c")
    return x, y, c, 2 * x + y


def _chip_dev(s, c):
    return (s // 2, s % 2, c)


class _Comm:
    def __init__(self, ins, out_shapes, sems, start, wait, aliases=None):
        self.ins, self.out_shapes, self.sems = list(ins), list(out_shapes), list(sems)
        self.start, self.wait, self.aliases = start, wait, dict(aliases or {})


def _merge(*comms):
    offs, i, o, s = [], 0, 0, 0
    for cm in comms:
        offs.append((i, o, s))
        i, o, s = i + len(cm.ins), o + len(cm.out_shapes), s + len(cm.sems)

    def part(refs, k, cm):
        i0, o0, s0 = offs[k]
        return refs[0][i0:i0 + len(cm.ins)], refs[1][o0:o0 + len(cm.out_shapes)], refs[2][s0:s0 + len(cm.sems)]

    def start(*refs):
        for k, cm in enumerate(comms):
            cm.start(*part(refs, k, cm))

    def wait(*refs):
        for k, cm in enumerate(comms):
            cm.wait(*part(refs, k, cm))

    aliases = {}
    for k, cm in enumerate(comms):
        for a, b in cm.aliases.items():
            aliases[offs[k][0] + a] = offs[k][1] + b
    return _Comm([a for cm in comms for a in cm.ins], [a for cm in comms for a in cm.out_shapes],
                 [a for cm in comms for a in cm.sems], start, wait, aliases)


def _call(body, *, name, grid, in_specs, out_specs, out_shape, scratch_shapes, semantics, args, comm=None, io_aliases=None):
    if comm is None:
        outs = pl.pallas_call(
            body, name=name, grid=grid, in_specs=in_specs, out_specs=out_specs, out_shape=out_shape,
            scratch_shapes=list(scratch_shapes), input_output_aliases=dict(io_aliases or {}),
            compiler_params=pltpu.CompilerParams(dimension_semantics=semantics, vmem_limit_bytes=VMEM_LIMIT))(*args)
        return list(outs), []
    assert not io_aliases
    n_in, n_out, n_scr = len(in_specs), len(out_specs), len(scratch_shapes)
    ci, co = len(comm.ins), len(comm.out_shapes)

    def wrapped(*refs):
        r = 0
        ins, r = refs[r:r + n_in], r + n_in
        cins, r = refs[r:r + ci], r + ci
        outs, r = refs[r:r + n_out], r + n_out
        couts, r = refs[r:r + co], r + co
        scr, r = refs[r:r + n_scr], r + n_scr
        csems = refs[r:]
        ids = [pl.program_id(a) for a in range(len(grid))]
        first, last = ids[0] == 0, ids[0] == grid[0] - 1
        for a in range(1, len(grid)):
            first = jnp.logical_and(first, ids[a] == 0)
            last = jnp.logical_and(last, ids[a] == grid[a] - 1)

        @pl.when(first)
        def _():
            comm.start(cins, couts, csems)

        body(*ins, *outs, *scr)

        @pl.when(last)
        def _():
            comm.wait(cins, couts, csems)

    outs = pl.pallas_call(
        wrapped, name=name, grid=grid, in_specs=list(in_specs) + [ANY] * ci, out_specs=list(out_specs) + [ANY] * co,
        out_shape=list(out_shape) + comm.out_shapes, scratch_shapes=list(scratch_shapes) + comm.sems,
        input_output_aliases={n_in + a: n_out + b for a, b in comm.aliases.items()},
        compiler_params=pltpu.CompilerParams(dimension_semantics=("arbitrary",) * len(grid), vmem_limit_bytes=VMEM_LIMIT),
    )(*args, *comm.ins)
    return list(outs[:n_out]), list(outs[n_out:])


def _run_comm(comm, name):
    ci, co = len(comm.ins), len(comm.out_shapes)

    def body(*refs):
        cins, couts, csems = refs[:ci], refs[ci:ci + co], refs[ci + co:]
        comm.start(cins, couts, csems)
        comm.wait(cins, couts, csems)

    outs = pl.pallas_call(body, name=name, in_specs=[ANY] * ci, out_specs=[ANY] * co, out_shape=comm.out_shapes,
                          scratch_shapes=comm.sems, input_output_aliases=comm.aliases)(*comm.ins)
    return list(outs)


def _ag_comm(shard, land=None, q=0, nq=1):
    two, R2, Cc = shard.shape
    rows = pl.ds(q * (R2 // nq), R2 // nq)
    DMA = pltpu.SemaphoreType.DMA

    def copies(ins, outs, sems, which):
        sh, out = ins[0], outs[0]
        send1, recv1, send2, recv2, send0, recv0 = sems
        x, y, c, s = _place()
        sib = (x, y, 1 - c)
        rc = pltpu.make_async_remote_copy
        if which == "first":
            return [rc(sh.at[c, rows], out.at[s, c, rows], send1.at[m - 1], recv1.at[m - 1],
                       device_id=_chip_dev(s ^ m, c), device_id_type=MESH) for m in range(1, 4)]
        if which == "own":
            return [rc(sh.at[h, rows], out.at[s, h, rows], send0.at[h], recv0.at[h], device_id=sib, device_id_type=MESH)
                    for h in range(2)]
        if which == "landed":
            return [rc(sh.at[c, rows], out.at[s ^ m, c, rows], send1.at[m - 1], recv1.at[m - 1], device_id=sib,
                       device_id_type=MESH) for m in range(1, 4)]
        half = c if which == "passed" else 1 - c
        return [rc(out.at[s ^ m, half, rows], out.at[s ^ m, half, rows], send2.at[m - 1], recv2.at[m - 1], device_id=sib,
                   device_id_type=MESH) for m in range(1, 4)]

    def start(ins, outs, sems):
        for cp in copies(ins, outs, sems, "first") + copies(ins, outs, sems, "own"):
            cp.start()

    def wait(ins, outs, sems):
        passed = copies(ins, outs, sems, "passed")
        for lan, pas in zip(copies(ins, outs, sems, "landed"), passed):
            lan.wait_recv()
            pas.start()
        for cp in copies(ins, outs, sems, "handed"):
            cp.wait_recv()
        for cp in copies(ins, outs, sems, "own"):
            cp.wait()
        for cp in copies(ins, outs, sems, "first") + passed:
            cp.wait_send()

    return _Comm([shard] + ([land] if land is not None else []), [jax.ShapeDtypeStruct((4, two, R2, Cc), shard.dtype)],
                 [DMA((3,)), DMA((3,)), DMA((3,)), DMA((3,)), DMA((2,)), DMA((2,))], start, wait,
                 {1: 0} if land is not None else None)


def _ag_relay_comm(shard):
    two, R2, Cc = shard.shape
    lo, hi = pl.ds(0, R2 // 2), pl.ds(R2 // 2, R2 // 2)
    DMA = pltpu.SemaphoreType.DMA

    def copies(ins, outs, sems, which):
        sh, out = ins[0], outs[0]
        send1, recv1, sendr, recvr, send2, recv2, send0, recv0 = sems
        x, y, c, s = _place()
        sib = (x, y, 1 - c)
        nbr = lambda m: _chip_dev(s ^ m, c)
        rc = functools.partial(pltpu.make_async_remote_copy, device_id_type=MESH)
        if which == "first":
            return [rc(sh.at[c], out.at[s, c], send1.at[m - 1], recv1.at[m - 1], device_id=nbr(m)) for m in (1, 2)]
        if which == "landed":
            return [rc(sh.at[c], out.at[s ^ m, c], send1.at[m - 1], recv1.at[m - 1], device_id=sib) for m in (1, 2)]
        if which == "relay":
            return [rc(out.at[s ^ 2, c, lo], out.at[s ^ 2, c, lo], sendr.at[0], recvr.at[0], device_id=nbr(1)),
                    rc(out.at[s ^ 1, c, hi], out.at[s ^ 1, c, hi], sendr.at[1], recvr.at[1], device_id=nbr(2))]
        if which == "relayed":
            return [rc(out.at[s ^ 3, c, lo], out.at[s ^ 3, c, lo], sendr.at[0], recvr.at[0], device_id=sib),
                    rc(out.at[s ^ 3, c, hi], out.at[s ^ 3, c, hi], sendr.at[1], recvr.at[1], device_id=sib)]
        if which == "own":
            return [rc(sh.at[h], out.at[s, h], send0.at[h], recv0.at[h], device_id=sib) for h in range(2)]
        half = c if which == "passed" else 1 - c
        return [rc(out.at[s ^ m, half], out.at[s ^ m, half], send2.at[m - 1], recv2.at[m - 1], device_id=sib)
                for m in range(1, 4)]

    def start(ins, outs, sems):
        for cp in copies(ins, outs, sems, "first") + copies(ins, outs, sems, "own"):
            cp.start()

    def wait(ins, outs, sems):
        landed, relay = copies(ins, outs, sems, "landed"), copies(ins, outs, sems, "relay")
        passed = copies(ins, outs, sems, "passed")
        landed[1].wait_recv()
        relay[0].start()
        passed[1].start()
        landed[0].wait_recv()
        relay[1].start()
        passed[0].start()
        for cp in copies(ins, outs, sems, "relayed"):
            cp.wait_recv()
        passed[2].start()
        for cp in copies(ins, outs, sems, "handed"):
            cp.wait_recv()
        for cp in copies(ins, outs, sems, "own"):
            cp.wait()
        for cp in copies(ins, outs, sems, "first") + relay + passed:
            cp.wait_send()

    return _Comm([shard], [jax.ShapeDtypeStruct((4, two, R2, Cc), shard.dtype)],
                 [DMA((2,)), DMA((2,)), DMA((2,)), DMA((2,)), DMA((3,)), DMA((3,)), DMA((2,)), DMA((2,))], start, wait)


def _a2a_comm(S1, q=0, nq=1, land=None, cnt=1):
    S4, R2, Cc = S1.shape
    rows = pl.ds(q * (R2 // nq), cnt * (R2 // nq))
    DMA = pltpu.SemaphoreType.DMA

    def copies(ins, outs, sems):
        x, y, c, s = _place()
        return [pltpu.make_async_remote_copy(ins[0].at[s ^ m, rows], outs[0].at[m - 1, rows], sems[0].at[m - 1],
                                             sems[1].at[m - 1], device_id=_chip_dev(s ^ m, c), device_id_type=MESH)
                for m in range(1, 4)]

    def start(ins, outs, sems):
        for cp in copies(ins, outs, sems):
            cp.start()

    def wait(ins, outs, sems):
        for cp in copies(ins, outs, sems):
            cp.wait()

    return _Comm([S1] + ([land] if land is not None else []), [jax.ShapeDtypeStruct((3, R2, Cc), S1.dtype)],
                 [DMA((3,)), DMA((3,))], start, wait, {1: 0} if land is not None else None)


def _halves(G):
    return G.reshape(G.shape[0], 2, G.shape[1] // 2, G.shape[2])


def _swap_comm(piece):
    n, two, R2, Cc = piece.shape
    DMA = pltpu.SemaphoreType.DMA

    def copies(ins, outs, sems):
        x, y, c, s = _place()
        return [pltpu.make_async_remote_copy(ins[0].at[t, 1 - c], outs[0].at[t], sems[0].at[t], sems[1].at[t],
                                             device_id=(x, y, 1 - c), device_id_type=MESH) for t in range(n)]

    def start(ins, outs, sems):
        for cp in copies(ins, outs, sems):
            cp.start()

    def wait(ins, outs, sems):
        for cp in copies(ins, outs, sems):
            cp.wait()

    return _Comm([piece], [jax.ShapeDtypeStruct((n, R2, Cc), piece.dtype)], [DMA((n,)), DMA((n,))], start, wait)


def _add_half(pieces, As, cidx, name):
    R2, Cc = pieces[0].shape[2:]
    S4 = sum(pc.shape[0] for pc in pieces)
    tb, cb = _wide(R2, Cc, 3, 2 * HALO)
    nI, nJ = R2 // tb, Cc // cb

    def body(c_ref, g_ref, a_ref, *rest):
        rest[-1][...] = (g_ref[0, 0] + a_ref[0]).astype(BF16)

    out, t0 = None, 0
    for k, (pc, A) in enumerate(zip(pieces, As)):
        grid_spec = pltpu.PrefetchScalarGridSpec(
            num_scalar_prefetch=1, grid=(pc.shape[0], nI, nJ),
            in_specs=[pl.BlockSpec((1, 1, tb, cb), lambda t, i, j, c_ref: (t, c_ref[0], i, j)),
                      pl.BlockSpec((1, tb, cb), lambda t, i, j, c_ref: (t, i, j))] + ([ANY] if k else []),
            out_specs=pl.BlockSpec((tb, cb), lambda t, i, j, c_ref, t0=t0: ((t0 + t) * nI + i, j)))
        out = pl.pallas_call(
            functools.partial(body), name=f"{name}{k}", grid_spec=grid_spec, out_shape=jax.ShapeDtypeStruct((S4 * R2, Cc), BF16),
            input_output_aliases={3: 0} if k else {},
            compiler_params=pltpu.CompilerParams(dimension_semantics=("parallel", "parallel", "parallel"),
                                                 vmem_limit_bytes=VMEM_LIMIT),
        )(*((cidx, pc, A) + ((out,) if k else ())))
        t0 += pc.shape[0]
    return out.reshape(S4, R2, Cc)


def _add_own(S1, B, chip_idx, cidx, name):
    S4, R2, Cc = S1.shape
    tb, cb = _wide(R2, Cc, 3, 2 * HALO)

    def body(s_idx, c_idx, s_ref, b_ref, o_ref):
        o_ref[...] = ((s_ref[0].astype(F32) + b_ref[0].astype(F32)) + b_ref[1].astype(F32)) + b_ref[2].astype(F32)

    grid_spec = pltpu.PrefetchScalarGridSpec(
        num_scalar_prefetch=2, grid=(R2 // tb, Cc // cb),
        in_specs=[pl.BlockSpec((1, tb, cb), lambda i, j, s_idx, c_idx: (s_idx[0], i, j)),
                  pl.BlockSpec((3, tb, cb), lambda i, j, s_idx, c_idx: (0, i, j))],
        out_specs=pl.BlockSpec((None, tb, cb), lambda i, j, s_idx, c_idx: (c_idx[0], i, j)))
    return pl.pallas_call(body, name=name, grid_spec=grid_spec, out_shape=jax.ShapeDtypeStruct((2, R2, Cc), F32),
                          compiler_params=pltpu.CompilerParams(dimension_semantics=("parallel", "parallel"),
                                                               vmem_limit_bytes=VMEM_LIMIT))(chip_idx, cidx, S1, B)


def _fill_comm(Hs):
    def copy(ins, outs, sems):
        x, y, c, s = _place()
        return pltpu.make_async_remote_copy(ins[0].at[c], outs[0].at[c], sems[0], sems[1], device_id=(x, y, 1 - c),
                                            device_id_type=MESH)

    return _Comm([Hs], [jax.ShapeDtypeStruct(Hs.shape, Hs.dtype)], [pltpu.SemaphoreType.DMA, pltpu.SemaphoreType.DMA],
                 lambda *r: copy(*r).start(), lambda *r: copy(*r).wait(), {0: 0})


def _gather_all_comm(buf):
    R, Cc = buf.shape
    DMA = pltpu.SemaphoreType.DMA

    def copies(ins, outs, sems):
        x, y, c, s = _place()
        d = 2 * s + c
        return ([pltpu.make_async_remote_copy(ins[0], outs[0].at[d], sems[0].at[m - 1], sems[1].at[m - 1],
                                              device_id=((d ^ m) // 4, ((d ^ m) // 2) % 2, (d ^ m) % 2), device_id_type=MESH)
                 for m in range(1, 8)], pltpu.make_async_copy(ins[0], outs[0].at[d], sems[2]))

    def start(ins, outs, sems):
        remote, mine = copies(ins, outs, sems)
        for cp in remote + [mine]:
            cp.start()

    def wait(ins, outs, sems):
        remote, mine = copies(ins, outs, sems)
        for cp in remote + [mine]:
            cp.wait()

    return _Comm([buf], [jax.ShapeDtypeStruct((8, R, Cc), buf.dtype)], [DMA((7,)), DMA((7,)), DMA], start, wait)


def _pack_rows(vs):
    flat = jnp.concatenate([v.reshape(-1) for v in vs])
    n = flat.shape[0]
    rows = -(-n // (LANES * 2 * HALO)) * 2 * HALO
    return jnp.pad(flat, (0, rows * LANES - n)).reshape(rows, LANES)


def _unpack_rows(buf, shapes):
    flat = buf.reshape(-1)
    outs, o = [], 0
    for shp in shapes:
        n = 1
        for d in shp:
            n *= d
        outs.append(flat[o:o + n].reshape(shp))
        o += n
    return outs


def kernel(x, p, norm_mix_g, w_in, conv_a_w, conv_qkv_w, a_log, dt_bias, dn_norm_g, w_out, norm_ffn_g, w_up, conv_ffn_w, w_down, norm_ple_g, w_ple_gate, w_ple_proj, final_norm_g, loss_target, m_norm_mix_g, m_w_in, m_conv_a_w, m_conv_qkv_w, m_a_log, m_dt_bias, m_dn_norm_g, m_w_out, m_norm_ffn_g, m_w_up, m_conv_ffn_w, m_w_down, m_norm_ple_g, m_w_ple_gate, m_w_ple_proj, m_final_norm_g, v_norm_mix_g, v_w_in, v_conv_a_w, v_conv_qkv_w, v_a_log, v_dt_bias, v_dn_norm_g, v_w_out, v_norm_ffn_g, v_w_up, v_conv_ffn_w, v_w_down, v_norm_ple_g, v_w_ple_gate, v_w_ple_proj, v_final_norm_g):
    xs = x[0]
    ps = p[0, 0]
    tgt = loss_target[0]
    T, D = xs.shape
    H = a_log.shape[-1]
    DNW = H * HEAD_DIM
    CW = conv_a_w.shape[-1] * 4
    F = w_down.shape[1] * 4
    PD = ps.shape[-1]
    IN_MAIN = 3 * CW + 4 * DNW
    IN_COLS = IN_MAIN + 2 * H
    assert w_in.shape[-1] * 4 == IN_COLS and CW + DNW == D and 2 * H <= LANES
    cb = _tile(min(CW, DNW), 512, LANES)
    while F % cb:
        cb -= LANES
    cidx = lax.axis_index("c").astype(jnp.int32).reshape(1)
    chip = 2 * lax.axis_index("x") + lax.axis_index("y")

    def halves(w):
        sh = w[0].astype(BF16)
        return sh.reshape(2, sh.shape[0] // 2, sh.shape[1])

    def whole(land):
        return land.reshape(4, 2 * land.shape[2], land.shape[3])

    def rows(g4):
        return g4.reshape(4 * g4.shape[1], g4.shape[2])

    conv_shapes = [conv_a_w[0].shape, conv_qkv_w[0].shape, conv_ffn_w[0].shape]
    cpack = _pack_rows([conv_a_w[0], conv_qkv_w[0], conv_ffn_w[0]])
    sh_in, sh_out, sh_up, sh_down, sh_pg, sh_pp = (halves(w) for w in (w_in, w_out, w_up, w_down, w_ple_gate, w_ple_proj))
    l_in, cg = _run_comm(_merge(_ag_relay_comm(sh_in), _ag_comm(cpack.reshape(2, cpack.shape[0] // 2, LANES))), "ag_w_in_conv")
    w_in_main, w_in_small = _join_shards(whole(l_in), IN_MAIN)
    cg = cg.reshape(4, cpack.shape[0], LANES)
    parts = [_unpack_rows(cg[t], conv_shapes) for t in range(4)]
    cw_a = jnp.concatenate([parts[t][0] for t in range(4)], axis=1)
    cw_qkv = jnp.concatenate([parts[t][1] for t in range(4)], axis=1)
    cw_ffn = jnp.concatenate([parts[t][2] for t in range(4)], axis=1)
    cw_q, cw_k, cw_v = cw_qkv[:, :DNW], cw_qkv[:, DNW:2 * DNW], cw_qkv[:, 2 * DNW:]
    cw_fg, cw_fv = cw_ffn[:, :F], cw_ffn[:, F:]
    pad_row = lambda v: jnp.pad(v, ((0, 0), (0, LANES - v.shape[1])))
    a_log_row, dt_row = pad_row(a_log), pad_row(dt_bias)
    gdn_t = jnp.tile(dn_norm_g, (1, H))
    gfin = final_norm_g.reshape(1, D)

    h1 = _rms_fwd(xs, norm_mix_g, "rms1")
    proj, (l_up,) = _mm(h1, w_in_main, mode="nn", out_dtypes=[F32], name="mm_proj", comm=_ag_comm(sh_up, q=0, nq=2))
    small = _mm(h1, w_in_small, mode="nn", out_dtypes=[F32], name="mm_small")
    ymix = _ga_fwd(proj, cw_a, CW, cb, D)
    nq = 3 * CW // cb
    nd = DNW // cb
    qkv, (l_out,) = _qkv_fwd(proj, cw_qkv, nq, 2 * nd, 3 * DNW, cb, "qkv_fwd", comm=_ag_comm(sh_out))
    qn = kn = vs = qkv
    g, beta = _gb_fwd(small, a_log_row, dt_row, H)
    o, S0, inv_c, (l_up,) = _delta_fwd(qn, kn, vs, g, beta, comm=_ag_comm(sh_up, l_up, q=1, nq=2))
    w_out_f = rows(whole(l_out))
    w_up_4 = whole(l_up)
    z_coff = (3 * CW + 3 * DNW) // DNW
    assert (3 * CW + 3 * DNW) % DNW == 0 and CW % DNW == 0
    ymix = _gnorm_fwd(o, proj, z_coff, gdn_t, DNW, ymix, CW // DNW)
    add = lambda acc, r: (r + acc,)

    def out_epi(acc, xv, gv):
        x1v = xv + acc
        return x1v, x1v * lax.rsqrt(jnp.mean(x1v * x1v, axis=1, keepdims=True) + EPS) * gv

    x1, h2 = _mm(ymix, w_out_f, mode="nn", out_dtypes=[F32, BF16], epi=out_epi, extras=[xs], rows=[norm_ffn_g], name="mm_out")
    up_g, (l_down,) = _mm(h2, w_up_4, mode="nn", b_split=(0, 2), out_dtypes=[F32], name="mm_up_g",
                          comm=_ag_comm(sh_down, q=0, nq=2))
    up_v, (l_down,) = _mm(h2, w_up_4, mode="nn", b_split=(2, 2), out_dtypes=[F32], name="mm_up_v",
                          comm=_ag_comm(sh_down, l_down, q=1, nq=2))
    w_down_f = rows(whole(l_down))
    act = _ffn_fwd(up_g, up_v, cw_fg, cw_fv, cb)
    x2, (l_pg, l_pp) = _mm(act, w_down_f, mode="nn", out_dtypes=[F32], epi=add, extras=[x1], name="mm_down",
                           comm=_merge(_ag_comm(sh_pg), _ag_comm(sh_pp)))
    w_pg_f = rows(whole(l_pg))
    w_pp_4 = whole(l_pp)
    h3 = _rms_fwd(x2, norm_ple_g, "rms3")
    pp = _mm(ps, w_pp_4, mode="nn", b_split=(0, 4), out_dtypes=[F32], name="mm_pp")

    def ple_final_epi(acc, x2v, ppv, tv, gv):
        pg = _sigmoid(acc)
        x3v = x2v + pg * ppv
        r = lax.rsqrt(jnp.mean(x3v * x3v, axis=1, keepdims=True) + EPS)
        xh = x3v * r
        e = xh * gv - tv
        dy = e * (1.0 / D)
        dxh = dy * gv
        dx = r * (dxh - xh * jnp.mean(dxh * xh, axis=1, keepdims=True))
        dg = jnp.sum(dy * xh, axis=0, keepdims=True)
        ls = jnp.sum(e * e, axis=0, keepdims=True) * (0.5 / D)
        return dx, dx * ppv * pg * (1.0 - pg), dx * pg, jnp.concatenate([dg, ls, jnp.zeros((HALO - 2, D), F32)], axis=0)

    dx3, dpg, dpp, fin = _mm(h3, w_pg_f, mode="nn", out_dtypes=[F32, BF16, BF16], parts=1, epi=ple_final_epi,
                             extras=[x2, pp, tgt], rows=[gfin], name="mm_pg_final")
    fin = jnp.sum(fin.reshape(-1, HALO, D), axis=0)
    loss = lax.psum(jnp.sum(fin[1]), ("x", "y", "c"))
    d_gfin = fin[0:1]
    def split_rows(dW):
        return dW.reshape(4, dW.shape[0] // 4, dW.shape[1])

    chip_idx = chip.astype(jnp.int32).reshape(1)
    own_sum = lambda S1, B, name: _add_own(S1, B, chip_idx, cidx, "rs_" + name + "_sum")

    dW_pp = _mm(ps, dpp, mode="tn", out_split=4, out_dtypes=[F32], name="mm_dw_pp")
    dW_pg = _mm(h3, dpg, mode="tn", out_dtypes=[F32], name="mm_dw_pg")
    P_pp, P_pg = _halves(dW_pp), _halves(split_rows(dW_pg))
    def rms_bwd_epi(acc, xv, dr, gv):
        dxv, dg = _rms_bwd_math(acc, xv, gv)
        return dr + dxv, dr + dxv, _row0(dg)

    (dx2, dx2_b, d_gple), (A_pp, A_pg) = _mm(dpg, w_pg_f, mode="nt", out_dtypes=[F32, BF16], parts=1, epi=rms_bwd_epi,
                                             extras=[x2, dx3], rows=[norm_ple_g], name="mm_dh3_rms",
                                             comm=_merge(_swap_comm(P_pp), _swap_comm(P_pg)))
    d_gple = jnp.sum(d_gple.reshape(-1, HALO, D), axis=0)
    S_pp = _add_half([P_pp], [A_pp], cidx, "rs_w_pp_add")
    S_pg = _add_half([P_pg], [A_pg], cidx, "rs_w_pg_add")
    dW_down, (B_pp, B_pg) = _mm(act, dx2_b, mode="tn", out_dtypes=[F32], name="mm_dw_down",
                                comm=_merge(_a2a_comm(S_pp), _a2a_comm(S_pg)))
    P_down = _halves(split_rows(dW_down))
    dact, (A_down, F_pp, F_pg) = _mm(dx2_b, w_down_f, mode="nt", out_dtypes=[F32], name="mm_dact", comm=_merge(
        _swap_comm(P_down), _fill_comm(own_sum(S_pp, B_pp, "w_pp")), _fill_comm(own_sum(S_pg, B_pg, "w_pg"))))
    S_down = _add_half([P_down], [A_down], cidx, "rs_w_down_add")
    dup_g, dup_v, dcw_fg, dcw_fv = _ffn_bwd(dact, up_g, up_v, cw_fg, cw_fv, cb)
    dW_up_g, (B_down,) = _mm(h2, dup_g, mode="tn", out_split=2, out_dtypes=[F32], name="mm_dw_up_g", comm=_a2a_comm(S_down))
    P_ug = _halves(dW_up_g)
    dW_up_v, (A_ug, F_down) = _mm(h2, dup_v, mode="tn", out_split=2, out_dtypes=[F32], name="mm_dw_up_v",
                                  comm=_merge(_swap_comm(P_ug), _fill_comm(own_sum(S_down, B_down, "w_down"))))
    P_uv = _halves(dW_up_v)
    dh2, (A_uv,) = _mm(dup_g, w_up_4, mode="nt", b_split=(0, 2), out_dtypes=[F32], name="mm_dh2_g", comm=_swap_comm(P_uv))
    S_up = _add_half([P_ug, P_uv], [A_ug, A_uv], cidx, "rs_w_up_add")
    dh2, (B_up,) = _mm(dup_v, w_up_4, mode="nt", b_split=(2, 2), out_dtypes=[F32], epi=add, extras=[dh2], name="mm_dh2_v",
                       comm=_a2a_comm(S_up, 0, 2))
    dx1, dx1_b, d_gffn = _rms_bwd(dh2, x1, norm_ffn_g, dx2, "rms2_bwd")
    P_out = _halves(split_rows(_mm(ymix, dx1_b, mode="tn", out_dtypes=[F32], name="mm_dw_out")))
    dymix, (A_out,) = _mm(dx1_b, w_out_f, mode="nt", out_dtypes=[F32], name="mm_dymix", comm=_swap_comm(P_out))
    S_out = _add_half([P_out], [A_out], cidx, "rs_w_out_add")
    dax, dab, dac, dcw_a = _ga_bwd(dymix, proj, cw_a, CW, cb)
    do, dz, d_gdn = _gnorm_bwd(dymix, CW // DNW, o, proj, z_coff, gdn_t, DNW)
    dqn, dkn, dvs, dgB, dbB, (B_up, B_out) = _delta_bwd(qn, kn, vs, g, beta, S0, inv_c, do,
                                                        comm=_merge(_a2a_comm(S_up, 1, 2, B_up), _a2a_comm(S_out)))
    dq_pre, dcw_q = _qkv_bwd(dqn, proj, cw_q, nq, True, DNW, cb, "q_bwd")
    dk_pre, dcw_k = _qkv_bwd(dkn, proj, cw_k, nq + nd, True, DNW, cb, "k_bwd")
    dv_pre, dcw_v = _qkv_bwd(dvs, proj, cw_v, nq + 2 * nd, False, DNW, cb, "v_bwd")
    dsmall, d_ab = _gb_bwd(dgB, dbB, small, g, beta, a_log_row, dt_row, H)
    dproj = jnp.concatenate([dax, dab, dac, dq_pre, dk_pre, dv_pre, dz], axis=1)
    dW_in_main, (F_up, F_out) = _mm(h1, dproj, mode="tn", out_dtypes=[F32], name="mm_dw_in", comm=_merge(
        _fill_comm(own_sum(S_up, B_up, "w_up")), _fill_comm(own_sum(S_out, B_out, "w_out"))))
    dW_in_small = _mm(h1, dsmall, mode="tn", out_dtypes=[F32], name="mm_dw_in_small")
    def update(Hf, w, m, v, name):
        gr = Hf.reshape(2 * Hf.shape[1], Hf.shape[2])
        if gr.shape[1] % LANES == 0:
            delta, m2, v2 = _adamw(w[0], gr, m[0], v[0], "adamw_" + name)
            return gr[None], delta[None], m2[None], v2[None]
        tr = jnp.transpose
        grt = tr(gr)
        delta, m2, v2 = _adamw(tr(w[0]), grt, tr(m[0]), tr(v[0]), "adamw_" + name)
        return tr(grt)[None], tr(delta)[None], tr(m2)[None], tr(v2)[None]

    P_in = _halves(_split_shards(dW_in_main, dW_in_small, IN_COLS // 4))
    (A_in,) = _run_comm(_swap_comm(P_in), "rs_w_in_swap")
    S_in = _add_half([P_in], [A_in], cidx, "rs_w_in_add")
    dh1, (B_in,) = _mm(dproj, w_in_main, mode="nt", out_dtypes=[F32], name="mm_dh1", comm=_a2a_comm(S_in))

    def rms1_epi(acc, dhv, xv, dr, gv):
        dxv, dg = _rms_bwd_math(acc + dhv, xv, gv)
        return dr + dxv, _row0(dg)

    dx, d_gmix = _mm(dsmall, w_in_small, mode="nt", out_dtypes=[F32], parts=1, epi=rms1_epi, extras=[dh1, xs, dx1],
                     rows=[norm_mix_g], name="mm_dh1_small_rms")
    d_gmix = jnp.sum(d_gmix.reshape(-1, HALO, D), axis=0)

    small_grads = [d_gmix[0:1], dcw_a[:cw_a.shape[0]], jnp.concatenate([dcw_q, dcw_k, dcw_v], axis=1)[:cw_qkv.shape[0]],
                   d_ab[0:1, :H], d_ab[1:2, :H], d_gdn[0:1], d_gffn[0:1],
                   jnp.concatenate([dcw_fg, dcw_fv], axis=1)[:cw_ffn.shape[0]], d_gple[0:1], d_gfin]
    small_shapes = [v.shape for v in small_grads]
    gpack = _pack_rows(small_grads)
    F_in, g8 = _run_comm(_merge(_fill_comm(own_sum(S_in, B_in, "w_in")), _gather_all_comm(gpack)), "rs_w_in_gather_small")
    big = {
        "w_in": update(F_in, w_in, m_w_in, v_w_in, "w_in"),
        "w_out": update(F_out, w_out, m_w_out, v_w_out, "w_out"),
        "w_up": update(F_up, w_up, m_w_up, v_w_up, "w_up"),
        "w_down": update(F_down, w_down, m_w_down, v_w_down, "w_down"),
        "w_ple_gate": update(F_pg, w_ple_gate, m_w_ple_gate, v_w_ple_gate, "w_pg"),
        "w_ple_proj": update(F_pp, w_ple_proj, m_w_ple_proj, v_w_ple_proj, "w_pp"),
    }

    gsum = _sum_stack(g8, "sum_small")
    (g_gmix, g_cwa, g_cwqkv, g_alog, g_dt, g_gdn, g_gffn, g_cwffn, g_gple, g_gfin) = _unpack_rows(gsum, small_shapes)

    def my_cols(v):
        Cc = v.shape[1] // 4
        return lax.dynamic_slice_in_dim(v, chip * Cc, Cc, axis=1)

    g_small = [g_gmix, my_cols(g_cwa), my_cols(g_cwqkv), g_alog, g_dt, g_gdn, g_gffn, my_cols(g_cwffn), g_gple, g_gfin]
    w_small = [norm_mix_g, conv_a_w[0], conv_qkv_w[0], a_log, dt_bias, dn_norm_g, norm_ffn_g, conv_ffn_w[0], norm_ple_g, gfin]
    m_small = [m_norm_mix_g, m_conv_a_w[0], m_conv_qkv_w[0], m_a_log, m_dt_bias, m_dn_norm_g, m_norm_ffn_g, m_conv_ffn_w[0],
               m_norm_ple_g, m_final_norm_g.reshape(1, D)]
    v_small = [v_norm_mix_g, v_conv_a_w[0], v_conv_qkv_w[0], v_a_log, v_dt_bias, v_dn_norm_g, v_norm_ffn_g, v_conv_ffn_w[0],
               v_norm_ple_g, v_final_norm_g.reshape(1, D)]
    shp = [v.shape for v in w_small]
    ds_, ms_, vs_ = _adamw(_pack_rows(w_small), _pack_rows(g_small), _pack_rows(m_small), _pack_rows(v_small), "adamw_small")
    out_shapes = [norm_mix_g.shape, conv_a_w.shape, conv_qkv_w.shape, a_log.shape, dt_bias.shape, dn_norm_g.shape,
                  norm_ffn_g.shape, conv_ffn_w.shape, norm_ple_g.shape, final_norm_g.shape]
    rs = lambda vals: [v.reshape(s) for v, s in zip(vals, out_shapes)]
    sg, sd_, sm_, sv_ = rs(g_small), rs(_unpack_rows(ds_, shp)), rs(_unpack_rows(ms_, shp)), rs(_unpack_rows(vs_, shp))
    names_small = ["norm_mix_g", "conv_a_w", "conv_qkv_w", "a_log", "dt_bias", "dn_norm_g", "norm_ffn_g", "conv_ffn_w",
                   "norm_ple_g", "final_norm_g"]
    res = {n: (sg[i], sd_[i], sm_[i], sv_[i]) for i, n in enumerate(names_small)}
    res.update(big)
    order = ["norm_mix_g", "w_in", "conv_a_w", "conv_qkv_w", "a_log", "dt_bias", "dn_norm_g", "w_out", "norm_ffn_g", "w_up",
             "conv_ffn_w", "w_down", "norm_ple_g", "w_ple_gate", "w_ple_proj", "final_norm_g"]
    return (loss, dx[None], *[res[n][0] for n in order], *[res[n][1] for n in order], *[res[n][2] for n in order],
            *[res[n][3] for n in order])
```
